```python
import math
import jax, jax.numpy as jnp
from jax import lax
import numpy as np

D_MODEL = 2048
BATCH = 8
SEQ = 4096
DEPTH = 4

GRID_W = 64
CTX_LEN = 256
NORM_EPS = 1e-6
N_EVEN = (DEPTH + 1) // 2
N_ODD = DEPTH // 2
MIX_WIDTH = D_MODEL
SSD_INNER = MIX_WIDTH // 2
SSD_HEAD_DIM = 64
SSD_HEADS = SSD_INNER // SSD_HEAD_DIM
SSD_GROUPS = 2
SSD_HPG = SSD_HEADS // SSD_GROUPS
SSD_STATE = 128
SSD_CHUNK = 128
CONV_K = 5
XBC_DIM = SSD_INNER + 2 * SSD_GROUPS * SSD_STATE
SGU_WIDTH = MIX_WIDTH - SSD_INNER
SGU_CHUNK = 128
SGU_GROUP_DIM = 128
SGU_GROUPS = SGU_WIDTH // SGU_GROUP_DIM
Z_END = SSD_INNER
XBC_END = Z_END + XBC_DIM
DT_END = XBC_END + 2 * SSD_HEADS
U_END = DT_END + SGU_WIDTH
EVEN_IN_COLS = U_END + SGU_WIDTH
ATT_HEAD_DIM = 128
ATT_HEADS = MIX_WIDTH // ATT_HEAD_DIM
ATT_KV_HEADS = ATT_HEADS // 4
ATT_GROUP = ATT_HEADS // ATT_KV_HEADS
Q_COLS = ATT_HEADS * ATT_HEAD_DIM
KV_COLS = ATT_KV_HEADS * ATT_HEAD_DIM
QKV_COLS = Q_COLS + 2 * KV_COLS
WINDOW = 128
ATT_BLOCK = 128
ROPE_BASE = 10000.0
NEG_INF = -1e30
FFN_HIDDEN = -(-(8 * D_MODEL) // (3 * 256)) * 256

kernel_name = 'hybrid_ssd_sgu_swa_diffusion_trunk'


def rms_norm(x, w):
    xf = x.astype(jnp.float32)
    y = xf * lax.rsqrt(jnp.mean(xf * xf, axis=-1, keepdims=True) + NORM_EPS)
    return (y * w.astype(jnp.float32)).astype(x.dtype)


def swiglu(h, w_in, w_out):
    g, u = jnp.split(h @ w_in, 2, axis=-1)
    return (jax.nn.silu(g) * u) @ w_out


def dwconv_centred(x, w, b):
    y = lax.conv_general_dilated(x, w.astype(x.dtype)[:, None, :], window_strides=(1,),
                                 padding=[(CONV_K // 2, CONV_K // 2)],
                                 dimension_numbers=('NWC', 'WIO', 'NWC'),
                                 feature_group_count=x.shape[-1])
    return y + b.astype(x.dtype)


def axial_rope_tables(S):
    rows = S // GRID_W
    row = jnp.repeat(jnp.arange(rows, dtype=jnp.float32), GRID_W)
    col = jnp.tile(jnp.arange(GRID_W, dtype=jnp.float32), rows)
    quarter = ATT_HEAD_DIM // 4
    inv = ROPE_BASE ** (-jnp.arange(quarter, dtype=jnp.float32) / quarter)
    ang = jnp.stack([row[:, None] * inv, col[:, None] * inv], axis=1)
    return jnp.cos(ang), jnp.sin(ang)


def axial_rope(x, cos, sin):
    b, S, H, d = x.shape
    xs = x.reshape(b, S, H, 2, 2, d // 4)
    x1, x2 = xs[..., 0, :], xs[..., 1, :]
    cs = cos[:, None].astype(x.dtype)
    sn = sin[:, None].astype(x.dtype)
    return jnp.stack([x1 * cs - x2 * sn, x2 * cs + x1 * sn], axis=-2).reshape(b, S, H, d)


def ssd_chunked(xh, dt, a, bm, cm, init_state, need_y):
    b, L, G, J, P = xh.shape
    N = bm.shape[-1]
    Q = SSD_CHUNK
    nc = L // Q
    X = (xh * dt[..., None]).reshape(b, nc, Q, G, J, P)
    Bc = bm.reshape(b, nc, Q, G, N)
    Cc = cm.reshape(b, nc, Q, G, N)
    a_cs = jnp.cumsum((dt * a).reshape(b, nc, Q, G, J), axis=2)
    a_last = a_cs[:, :, -1]
    states = jnp.einsum('bcsgn,bcsgjp->bcgjpn', Bc, X * jnp.exp(a_last[:, :, None] - a_cs)[..., None])

    def step(s, inp):
        decay, st = inp
        return s * decay[..., None, None] + st, s

    final, s_prev = lax.scan(step, init_state, (jnp.moveaxis(jnp.exp(a_last), 1, 0), jnp.moveaxis(states, 1, 0)))
    if not need_y:
        return None, final
    s_prev = jnp.moveaxis(s_prev, 0, 1)
    seg = a_cs[:, :, :, None] - a_cs[:, :, None]
    lower = jnp.tril(jnp.ones((Q, Q), bool))[None, None, :, :, None, None]
    decay_ls = jnp.where(lower, jnp.exp(jnp.where(lower, seg, 0.0)), 0.0)
    cb = jnp.einsum('bclgn,bcsgn->bclsg', Cc, Bc)
    y_diag = jnp.einsum('bclsgj,bcsgjp->bclgjp', cb[..., None] * decay_ls, X)
    y_off = jnp.einsum('bclgn,bcgjpn->bclgjp', Cc, s_prev) * jnp.exp(a_cs)[..., None]
    return (y_diag + y_off).reshape(b, L, G, J, P), final


def ssd_inputs(p, conv_w, conv_b, dt_bias):
    b, L, _ = p.shape
    xbc = jax.nn.silu(dwconv_centred(p[..., :XBC_DIM], conv_w, conv_b)).astype(jnp.float32)
    gn = SSD_GROUPS * SSD_STATE
    xs = xbc[..., :SSD_INNER].reshape(b, L, SSD_GROUPS, SSD_HPG, SSD_HEAD_DIM)
    bm = xbc[..., SSD_INNER:SSD_INNER + gn].reshape(b, L, SSD_GROUPS, SSD_STATE)
    cm = xbc[..., SSD_INNER + gn:].reshape(b, L, SSD_GROUPS, SSD_STATE)
    dt = jax.nn.softplus(p[..., XBC_DIM:].astype(jnp.float32).reshape(b, L, 2, SSD_GROUPS, SSD_HPG)
                         + dt_bias.astype(jnp.float32).reshape(2, SSD_GROUPS, SSD_HPG))
    return xs, bm, cm, dt


def ssd_bidirectional(ctx_in, lat_in, a_log, need_ctx):
    xc, bc, cc, dtc = ctx_in
    xl, bl, cl, dtl = lat_in
    b = xl.shape[0]
    y_ctx, y_lat = None, None
    for d in range(2):
        a = -jnp.exp(a_log[d].astype(jnp.float32)).reshape(SSD_GROUPS, SSD_HPG)
        fl = (lambda t: jnp.flip(t, axis=1)) if d == 1 else (lambda t: t)
        init = jnp.zeros((b, SSD_GROUPS, SSD_HPG, SSD_HEAD_DIM, SSD_STATE), jnp.float32)
        yc, s_ctx = ssd_chunked(fl(xc), fl(dtc[:, :, d]), a, fl(bc), fl(cc), init, need_ctx)
        yl, _ = ssd_chunked(fl(xl), fl(dtl[:, :, d]), a, fl(bl), fl(cl), s_ctx, True)
        y_lat = fl(yl) if y_lat is None else y_lat + fl(yl)
        if need_ctx:
            y_ctx = fl(yc) if y_ctx is None else y_ctx + fl(yc)
    return y_ctx, y_lat


def ssd_finish(y, xs, z, d_skip, norm_w):
    b, L = y.shape[:2]
    y = y + xs * d_skip.astype(jnp.float32).reshape(SSD_GROUPS, SSD_HPG)[..., None]
    y = y.reshape(b, L, SSD_INNER) * jax.nn.silu(z.astype(jnp.float32))
    yg = y.reshape(b, L, SSD_GROUPS, SSD_INNER // SSD_GROUPS)
    yg = yg * lax.rsqrt(jnp.mean(yg * yg, axis=-1, keepdims=True) + NORM_EPS)
    return (yg.reshape(b, L, SSD_INNER) * norm_w.astype(jnp.float32)).astype(z.dtype)


def spatial_gating(u, v, sgu_w, sgu_b):
    b, L, _ = u.shape
    u = jax.nn.gelu(u)
    vf = jax.nn.gelu(v).astype(jnp.float32)
    mu = jnp.mean(vf, axis=-1, keepdims=True)
    var = jnp.mean(jnp.square(vf - mu), axis=-1, keepdims=True)
    vn = ((vf - mu) * lax.rsqrt(var + NORM_EPS)).astype(v.dtype)
    vc = vn.reshape(b, L // SGU_CHUNK, SGU_CHUNK, SGU_GROUPS, SGU_GROUP_DIM)
    mixed = jnp.einsum('gij,bnjgc->bnigc', sgu_w.astype(v.dtype), vc) + sgu_b.astype(v.dtype).T[None, None, :, :, None]
    return u * mixed.reshape(b, L, SGU_WIDTH)


def ssd_sgu_mixer(h_ctx, h_lat, w_in, conv_w, conv_b, dt_bias, a_log, d_skip, ssd_norm_w, sgu_w, sgu_b, w_out, need_ctx):
    p_lat = h_lat @ w_in
    if need_ctx:
        p_ctx = h_ctx @ w_in
        ctx_ssd_cols = p_ctx[..., Z_END:DT_END]
    else:
        p_ctx = None
        ctx_ssd_cols = h_ctx @ w_in[:, Z_END:DT_END]
    lat_in = ssd_inputs(p_lat[..., Z_END:DT_END], conv_w, conv_b, dt_bias)
    ctx_in = ssd_inputs(ctx_ssd_cols, conv_w, conv_b, dt_bias)
    y_ctx, y_lat = ssd_bidirectional(ctx_in, lat_in, a_log, need_ctx)

    def merge(p, y, xs):
        y_ssd = ssd_finish(y, xs, p[..., :Z_END], d_skip, ssd_norm_w)
        y_sgu = spatial_gating(p[..., DT_END:U_END], p[..., U_END:], sgu_w, sgu_b)
        return jnp.concatenate([y_ssd, y_sgu], axis=-1) @ w_out

    o_lat = merge(p_lat, y_lat, lat_in[0])
    o_ctx = merge(p_ctx, y_ctx, ctx_in[0]) if need_ctx else None
    return o_ctx, o_lat


def banded_window_attention(q, k, v, k_c, v_c, sink_kg):
    b, S, H, Dh = q.shape
    T = k_c.shape[1]
    nb = S // ATT_BLOCK
    nw = 3 * ATT_BLOCK
    qb = q.reshape(b, nb, ATT_BLOCK, ATT_KV_HEADS, ATT_GROUP, Dh)
    pad = ((0, 0), (ATT_BLOCK, ATT_BLOCK), (0, 0), (0, 0))

    def band(t):
        tp = jnp.pad(t, pad).reshape(b, nb + 2, ATT_BLOCK, ATT_KV_HEADS, Dh)
        return jnp.concatenate([tp[:, :-2], tp[:, 1:-1], tp[:, 2:]], axis=2)

    kb, vb = band(k), band(v)
    s_win = jnp.einsum('bnqkgd,bnskd->bnkgqs', qb, kb).astype(jnp.float32)
    qpos = (jnp.arange(nb)[:, None] * ATT_BLOCK + jnp.arange(ATT_BLOCK)[None, :])[:, :, None]
    kpos = ((jnp.arange(nb)[:, None] - 1) * ATT_BLOCK + jnp.arange(nw)[None, :])[:, None, :]
    valid = (jnp.abs(qpos - kpos) <= WINDOW) & (kpos >= 0) & (kpos < S)
    s_win = jnp.where(valid[None, :, None, None], s_win, NEG_INF)
    s_ctx = jnp.einsum('bnqkgd,btkd->bnkgqt', qb, k_c).astype(jnp.float32)
    sink = jnp.broadcast_to(sink_kg[None, None, :, :, None, None], s_win.shape[:-1] + (1,))
    p = jax.nn.softmax(jnp.concatenate([s_win, s_ctx, sink], axis=-1), axis=-1).astype(v.dtype)
    o = (jnp.einsum('bnkgqs,bnskd->bnqkgd', p[..., :nw], vb)
         + jnp.einsum('bnkgqt,btkd->bnqkgd', p[..., nw:nw + T], v_c))
    return o.reshape(b, S, H * Dh)


def context_attention(q_c, k_c, v_c, sink_kg):
    b, T = q_c.shape[:2]
    s = jnp.einsum('btkgd,bukd->bkgtu', q_c, k_c).astype(jnp.float32)
    sink = jnp.broadcast_to(sink_kg[None, :, :, None, None], s.shape[:-1] + (1,))
    p = jax.nn.softmax(jnp.concatenate([s, sink], axis=-1), axis=-1)[..., :-1].astype(v_c.dtype)
    return jnp.einsum('bkgtu,bukd->btkgd', p, v_c).reshape(b, T, ATT_HEADS * ATT_HEAD_DIM)


def window_gqa_mixer(h_ctx, h_lat, w_qkv, sink, w_out, cos, sin, need_ctx):
    b, S, _ = h_lat.shape
    T = h_ctx.shape[1]
    scale = ATT_HEAD_DIM ** -0.5
    p_lat = h_lat @ w_qkv
    q_l = axial_rope(p_lat[..., :Q_COLS].reshape(b, S, ATT_HEADS, ATT_HEAD_DIM), cos, sin) * scale
    k_l = axial_rope(p_lat[..., Q_COLS:Q_COLS + KV_COLS].reshape(b, S, ATT_KV_HEADS, ATT_HEAD_DIM), cos, sin)
    v_l = p_lat[..., Q_COLS + KV_COLS:].reshape(b, S, ATT_KV_HEADS, ATT_HEAD_DIM)
    p_ctx = h_ctx @ w_qkv if need_ctx else h_ctx @ w_qkv[:, Q_COLS:]
    k_c = p_ctx[..., -2 * KV_COLS:-KV_COLS].reshape(b, T, ATT_KV_HEADS, ATT_HEAD_DIM)
    v_c = p_ctx[..., -KV_COLS:].reshape(b, T, ATT_KV_HEADS, ATT_HEAD_DIM)
    sink_kg = sink.astype(jnp.float32).reshape(ATT_KV_HEADS, ATT_GROUP)
    o_lat = banded_window_attention(q_l, k_l, v_l, k_c, v_c, sink_kg) @ w_out
    o_ctx = None
    if need_ctx:
        q_c = p_ctx[..., :Q_COLS].reshape(b, T, ATT_KV_HEADS, ATT_GROUP, ATT_HEAD_DIM) * scale
        o_ctx = context_attention(q_c, k_c, v_c, sink_kg) @ w_out
    return o_ctx, o_lat


def _fwd_setup_inputs(seed: int = 0) -> dict:
    key = jax.random.key(seed)
    ks = jax.random.split(key, 24)
    D = D_MODEL

    def nrm(k, shape, scale):
        return jax.random.normal(k, shape, jnp.float32) * scale

    dt0 = jnp.exp(jax.random.uniform(ks[12], (N_EVEN, 2, SSD_HEADS), jnp.float32,
                                     minval=math.log(1e-3), maxval=math.log(1e-1)))
    return {
        'x': nrm(ks[0], (BATCH, SEQ, D), 1.0),
        'c': nrm(ks[1], (BATCH, D), 1.0),
        'ctx': nrm(ks[2], (BATCH, CTX_LEN, D), 1.0),
        'c_ctx': nrm(ks[3], (D,), 1.0),
        'w_mod': nrm(ks[4], (DEPTH, D, 6 * D), 0.5 * D ** -0.5),
        'b_mod': nrm(ks[5], (DEPTH, 6 * D), 0.02),
        'norm_w': 1.0 + nrm(ks[6], (DEPTH, 4, D), 0.05),
        'w_ffn_in': nrm(ks[7], (DEPTH, D, 2 * FFN_HIDDEN), D ** -0.5),
        'w_ffn_out': nrm(ks[8], (DEPTH, FFN_HIDDEN, D), FFN_HIDDEN ** -0.5),
        'e_w_in': nrm(ks[9], (N_EVEN, D, EVEN_IN_COLS), D ** -0.5),
        'e_conv_w': nrm(ks[10], (N_EVEN, CONV_K, XBC_DIM), CONV_K ** -0.5),
        'e_conv_b': nrm(ks[11], (N_EVEN, XBC_DIM), 0.02),
        'e_dt_bias': dt0 + jnp.log(-jnp.expm1(-dt0)),
        'e_a_log': jnp.log(jax.random.uniform(ks[13], (N_EVEN, 2, SSD_HEADS), jnp.float32, minval=1.0, maxval=16.0)),
        'e_d_skip': 1.0 + nrm(ks[14], (N_EVEN, SSD_HEADS), 0.05),
        'e_ssd_norm_w': 1.0 + nrm(ks[15], (N_EVEN, SSD_INNER), 0.05),
        'e_sgu_w': nrm(ks[16], (N_EVEN, SGU_GROUPS, SGU_CHUNK, SGU_CHUNK), SGU_CHUNK ** -0.5),
        'e_sgu_b': 1.0 + nrm(ks[17], (N_EVEN, SGU_GROUPS, SGU_CHUNK), 0.05),
        'e_w_out': nrm(ks[18], (N_EVEN, MIX_WIDTH, D), MIX_WIDTH ** -0.5),
        'o_w_qkv': nrm(ks[19], (N_ODD, D, QKV_COLS), D ** -0.5),
        'o_sink': nrm(ks[20], (N_ODD, ATT_HEADS), 0.5),
        'o_w_out': nrm(ks[21], (N_ODD, Q_COLS, D), Q_COLS ** -0.5),
    }


def _fwd_reference(x, c, ctx, c_ctx, w_mod, b_mod, norm_w, w_ffn_in, w_ffn_out, e_w_in, e_conv_w, e_conv_b,
              e_dt_bias, e_a_log, e_d_skip, e_ssd_norm_w, e_sgu_w, e_sgu_b, e_w_out, o_w_qkv, o_sink, o_w_out):
    S = x.shape[1]
    cos, sin = axial_rope_tables(S)
    s_c = jax.nn.silu(c)
    s_cc = jax.nn.silu(c_ctx)
    for l in range(DEPTH):
        need_ctx = l < DEPTH - 1
        mod_l = (s_c @ w_mod[l] + b_mod[l])[:, None, :]
        mod_c = s_cc @ w_mod[l] + b_mod[l]
        sh1, sc1, g1, sh2, sc2, g2 = jnp.split(mod_l, 6, axis=-1)
        csh1, csc1, cg1, csh2, csc2, cg2 = jnp.split(mod_c, 6, axis=-1)
        h_lat = rms_norm(x, norm_w[l, 0]) * (1.0 + sc1) + sh1
        h_ctx = rms_norm(ctx, norm_w[l, 0]) * (1.0 + csc1) + csh1
        if l % 2 == 0:
            i = l // 2
            o_ctx, o_lat = ssd_sgu_mixer(h_ctx, h_lat, e_w_in[i], e_conv_w[i], e_conv_b[i], e_dt_bias[i], e_a_log[i],
                                         e_d_skip[i], e_ssd_norm_w[i], e_sgu_w[i], e_sgu_b[i], e_w_out[i], need_ctx)
        else:
            i = l // 2
            o_ctx, o_lat = window_gqa_mixer(h_ctx, h_lat, o_w_qkv[i], o_sink[i], o_w_out[i], cos, sin, need_ctx)
        x = x + g1 * rms_norm(o_lat, norm_w[l, 1])
        f_lat = swiglu(rms_norm(x, norm_w[l, 2]) * (1.0 + sc2) + sh2, w_ffn_in[l], w_ffn_out[l])
        x = x + g2 * rms_norm(f_lat, norm_w[l, 3])
        if need_ctx:
            ctx = ctx + cg1 * rms_norm(o_ctx, norm_w[l, 1])
            f_ctx = swiglu(rms_norm(ctx, norm_w[l, 2]) * (1.0 + csc2) + csh2, w_ffn_in[l], w_ffn_out[l])
            ctx = ctx + cg2 * rms_norm(f_ctx, norm_w[l, 3])
    return x


import jax as _jax
import jax.numpy as _jnp

TWIN_FORMAT = 'train_step'
FWD_PARAMS = ['x', 'c', 'ctx', 'c_ctx', 'w_mod', 'b_mod', 'norm_w', 'w_ffn_in', 'w_ffn_out', 'e_w_in', 'e_conv_w', 'e_conv_b', 'e_dt_bias', 'e_a_log', 'e_d_skip', 'e_ssd_norm_w', 'e_sgu_w', 'e_sgu_b', 'e_w_out', 'o_w_qkv', 'o_sink', 'o_w_out']
TWIN_WEIGHTS = ['c_ctx', 'w_mod', 'b_mod', 'norm_w', 'w_ffn_in', 'w_ffn_out', 'e_w_in', 'e_conv_w', 'e_conv_b', 'e_dt_bias', 'e_a_log', 'e_d_skip', 'e_ssd_norm_w', 'e_sgu_w', 'e_sgu_b', 'e_w_out', 'o_w_qkv', 'o_sink', 'o_w_out']
TWIN_DIFF_INPUT = 'x'
TWIN_INPUTS = ['x', 'c', 'ctx', 'c_ctx', 'w_mod', 'b_mod', 'norm_w', 'w_ffn_in', 'w_ffn_out', 'e_w_in', 'e_conv_w', 'e_conv_b', 'e_dt_bias', 'e_a_log', 'e_d_skip', 'e_ssd_norm_w', 'e_sgu_w', 'e_sgu_b', 'e_w_out', 'o_w_qkv', 'o_sink', 'o_w_out', 'loss_target', 'm_c_ctx', 'm_w_mod', 'm_b_mod', 'm_norm_w', 'm_w_ffn_in', 'm_w_ffn_out', 'm_e_w_in', 'm_e_conv_w', 'm_e_conv_b', 'm_e_dt_bias', 'm_e_a_log', 'm_e_d_skip', 'm_e_ssd_norm_w', 'm_e_sgu_w', 'm_e_sgu_b', 'm_e_w_out', 'm_o_w_qkv', 'm_o_sink', 'm_o_w_out', 'v_c_ctx', 'v_w_mod', 'v_b_mod', 'v_norm_w', 'v_w_ffn_in', 'v_w_ffn_out', 'v_e_w_in', 'v_e_conv_w', 'v_e_conv_b', 'v_e_dt_bias', 'v_e_a_log', 'v_e_d_skip', 'v_e_ssd_norm_w', 'v_e_sgu_w', 'v_e_sgu_b', 'v_e_w_out', 'v_o_w_qkv', 'v_o_sink', 'v_o_w_out']
TWIN_OUTPUTS = ['loss', 'grad_x', 'grad_c_ctx', 'grad_w_mod', 'grad_b_mod', 'grad_norm_w', 'grad_w_ffn_in', 'grad_w_ffn_out', 'grad_e_w_in', 'grad_e_conv_w', 'grad_e_conv_b', 'grad_e_dt_bias', 'grad_e_a_log', 'grad_e_d_skip', 'grad_e_ssd_norm_w', 'grad_e_sgu_w', 'grad_e_sgu_b', 'grad_e_w_out', 'grad_o_w_qkv', 'grad_o_sink', 'grad_o_w_out', 'delta_c_ctx', 'delta_w_mod', 'delta_b_mod', 'delta_norm_w', 'delta_w_ffn_in', 'delta_w_ffn_out', 'delta_e_w_in', 'delta_e_conv_w', 'delta_e_conv_b', 'delta_e_dt_bias', 'delta_e_a_log', 'delta_e_d_skip', 'delta_e_ssd_norm_w', 'delta_e_sgu_w', 'delta_e_sgu_b', 'delta_e_w_out', 'delta_o_w_qkv', 'delta_o_sink', 'delta_o_w_out', 'new_m_c_ctx', 'new_m_w_mod', 'new_m_b_mod', 'new_m_norm_w', 'new_m_w_ffn_in', 'new_m_w_ffn_out', 'new_m_e_w_in', 'new_m_e_conv_w', 'new_m_e_conv_b', 'new_m_e_dt_bias', 'new_m_e_a_log', 'new_m_e_d_skip', 'new_m_e_ssd_norm_w', 'new_m_e_sgu_w', 'new_m_e_sgu_b', 'new_m_e_w_out', 'new_m_o_w_qkv', 'new_m_o_sink', 'new_m_o_w_out', 'new_v_c_ctx', 'new_v_w_mod', 'new_v_b_mod', 'new_v_norm_w', 'new_v_w_ffn_in', 'new_v_w_ffn_out', 'new_v_e_w_in', 'new_v_e_conv_w', 'new_v_e_conv_b', 'new_v_e_dt_bias', 'new_v_e_a_log', 'new_v_e_d_skip', 'new_v_e_ssd_norm_w', 'new_v_e_sgu_w', 'new_v_e_sgu_b', 'new_v_e_w_out', 'new_v_o_w_qkv', 'new_v_o_sink', 'new_v_o_w_out']
TWIN_LEAF_KINDS = {'loss': 'loss', 'grad_x': 'grad_x', 'grad_c_ctx': 'grad_w', 'grad_w_mod': 'grad_w', 'grad_b_mod': 'grad_w', 'grad_norm_w': 'grad_w', 'grad_w_ffn_in': 'grad_w', 'grad_w_ffn_out': 'grad_w', 'grad_e_w_in': 'grad_w', 'grad_e_conv_w': 'grad_w', 'grad_e_conv_b': 'grad_w', 'grad_e_dt_bias': 'grad_w', 'grad_e_a_log': 'grad_w', 'grad_e_d_skip': 'grad_w', 'grad_e_ssd_norm_w': 'grad_w', 'grad_e_sgu_w': 'grad_w', 'grad_e_sgu_b': 'grad_w', 'grad_e_w_out': 'grad_w', 'grad_o_w_qkv': 'grad_w', 'grad_o_sink': 'grad_w', 'grad_o_w_out': 'grad_w', 'delta_c_ctx': 'delta_w', 'delta_w_mod': 'delta_w', 'delta_b_mod': 'delta_w', 'delta_norm_w': 'delta_w', 'delta_w_ffn_in': 'delta_w', 'delta_w_ffn_out': 'delta_w', 'delta_e_w_in': 'delta_w', 'delta_e_conv_w': 'delta_w', 'delta_e_conv_b': 'delta_w', 'delta_e_dt_bias': 'delta_w', 'delta_e_a_log': 'delta_w', 'delta_e_d_skip': 'delta_w', 'delta_e_ssd_norm_w': 'delta_w', 'delta_e_sgu_w': 'delta_w', 'delta_e_sgu_b': 'delta_w', 'delta_e_w_out': 'delta_w', 'delta_o_w_qkv': 'delta_w', 'delta_o_sink': 'delta_w', 'delta_o_w_out': 'delta_w', 'new_m_c_ctx': 'new_m', 'new_m_w_mod': 'new_m', 'new_m_b_mod': 'new_m', 'new_m_norm_w': 'new_m', 'new_m_w_ffn_in': 'new_m', 'new_m_w_ffn_out': 'new_m', 'new_m_e_w_in': 'new_m', 'new_m_e_conv_w': 'new_m', 'new_m_e_conv_b': 'new_m', 'new_m_e_dt_bias': 'new_m', 'new_m_e_a_log': 'new_m', 'new_m_e_d_skip': 'new_m', 'new_m_e_ssd_norm_w': 'new_m', 'new_m_e_sgu_w': 'new_m', 'new_m_e_sgu_b': 'new_m', 'new_m_e_w_out': 'new_m', 'new_m_o_w_qkv': 'new_m', 'new_m_o_sink': 'new_m', 'new_m_o_w_out': 'new_m', 'new_v_c_ctx': 'new_v', 'new_v_w_mod': 'new_v', 'new_v_b_mod': 'new_v', 'new_v_norm_w': 'new_v', 'new_v_w_ffn_in': 'new_v', 'new_v_w_ffn_out': 'new_v', 'new_v_e_w_in': 'new_v', 'new_v_e_conv_w': 'new_v', 'new_v_e_conv_b': 'new_v', 'new_v_e_dt_bias': 'new_v', 'new_v_e_a_log': 'new_v', 'new_v_e_d_skip': 'new_v', 'new_v_e_ssd_norm_w': 'new_v', 'new_v_e_sgu_w': 'new_v', 'new_v_e_sgu_b': 'new_v', 'new_v_e_w_out': 'new_v', 'new_v_o_w_qkv': 'new_v', 'new_v_o_sink': 'new_v', 'new_v_o_w_out': 'new_v'}


def _forward(args):
    return _fwd_reference(*[args[k] for k in FWD_PARAMS])


def _output_shape():
    def fwd():
        inp = _fwd_setup_inputs(0)
        return _fwd_reference(*[inp[k] for k in FWD_PARAMS])
    out = _jax.eval_shape(fwd)
    return out.shape, out.dtype

N_MICROBATCH = 1
ADAM_LR = 0.001
ADAM_B1 = 0.9
ADAM_B2 = 0.999
ADAM_EPS = 1e-08
ADAM_WD = 0.01
ADAM_STEP = 10
PER_EXAMPLE_BATCH_AXIS = {'x': 0, 'c': 0, 'ctx': 0, 'loss_target': 0}
SHARED_INPUTS = []
_WEIGHT_DTYPES = {'c_ctx': _jnp.float32, 'w_mod': _jnp.float32, 'b_mod': _jnp.float32, 'norm_w': _jnp.float32, 'w_ffn_in': _jnp.float32, 'w_ffn_out': _jnp.float32, 'e_w_in': _jnp.float32, 'e_conv_w': _jnp.float32, 'e_conv_b': _jnp.float32, 'e_dt_bias': _jnp.float32, 'e_a_log': _jnp.float32, 'e_d_skip': _jnp.float32, 'e_ssd_norm_w': _jnp.float32, 'e_sgu_w': _jnp.float32, 'e_sgu_b': _jnp.float32, 'e_w_out': _jnp.float32, 'o_w_qkv': _jnp.float32, 'o_sink': _jnp.float32, 'o_w_out': _jnp.float32}
MOMENT_SCALE = {'c_ctx': 9.740951e-01, 'w_mod': 9.624186e-01, 'b_mod': 2.014822e+00, 'norm_w': 1.257594e+00, 'w_ffn_in': 7.625262e-02, 'w_ffn_out': 1.478550e-01, 'e_w_in': 1.554405e-01, 'e_conv_w': 2.170663e-01, 'e_conv_b': 4.820980e-01, 'e_dt_bias': 1.737308e-01, 'e_a_log': 8.441774e-01, 'e_d_skip': 6.856744e-01, 'e_ssd_norm_w': 3.555644e-01, 'e_sgu_w': 4.671862e-02, 'e_sgu_b': 5.066540e-02, 'e_w_out': 3.819256e-01, 'o_w_qkv': 7.620259e-01, 'o_sink': 1.888471e-02, 'o_w_out': 8.989644e-01}


def _to_microbatches(a, axis):
    t = _jnp.moveaxis(a, axis, 0)
    t = t.reshape((N_MICROBATCH, t.shape[0] // N_MICROBATCH) + t.shape[1:])
    return _jnp.moveaxis(t, 1, axis + 1)


def setup_inputs(seed: int = 0) -> dict:
    inp = _fwd_setup_inputs(seed)
    key = _jax.random.fold_in(_jax.random.key(seed), 7919)
    shape, _ = _output_shape()
    out = dict(inp)
    out["loss_target"] = _jax.random.normal(_jax.random.fold_in(key, 0), shape, _jnp.float32)
    for i, name in enumerate(TWIN_WEIGHTS):
        w = inp[name].astype(_jnp.float32)
        if MOMENT_SCALE is None:
            s = _jnp.sqrt(_jnp.mean(_jnp.square(w)) + 1e-30)
        else:
            s = MOMENT_SCALE[name]
        km, kv = _jax.random.split(_jax.random.fold_in(key, i + 1))
        out[name] = w
        out["m_" + name] = s * _jax.random.normal(km, w.shape, _jnp.float32)
        out["v_" + name] = (s * s) * _jax.random.uniform(kv, w.shape, _jnp.float32, 0.5, 1.5)
    if N_MICROBATCH > 1:
        for name, axis in PER_EXAMPLE_BATCH_AXIS.items():
            out[name] = _to_microbatches(out[name], axis)
    return {'x': out['x'], 'c': out['c'], 'ctx': out['ctx'], 'c_ctx': out['c_ctx'], 'w_mod': out['w_mod'], 'b_mod': out['b_mod'], 'norm_w': out['norm_w'], 'w_ffn_in': out['w_ffn_in'], 'w_ffn_out': out['w_ffn_out'], 'e_w_in': out['e_w_in'], 'e_conv_w': out['e_conv_w'], 'e_conv_b': out['e_conv_b'], 'e_dt_bias': out['e_dt_bias'], 'e_a_log': out['e_a_log'], 'e_d_skip': out['e_d_skip'], 'e_ssd_norm_w': out['e_ssd_norm_w'], 'e_sgu_w': out['e_sgu_w'], 'e_sgu_b': out['e_sgu_b'], 'e_w_out': out['e_w_out'], 'o_w_qkv': out['o_w_qkv'], 'o_sink': out['o_sink'], 'o_w_out': out['o_w_out'], 'loss_target': out['loss_target'], 'm_c_ctx': out['m_c_ctx'], 'm_w_mod': out['m_w_mod'], 'm_b_mod': out['m_b_mod'], 'm_norm_w': out['m_norm_w'], 'm_w_ffn_in': out['m_w_ffn_in'], 'm_w_ffn_out': out['m_w_ffn_out'], 'm_e_w_in': out['m_e_w_in'], 'm_e_conv_w': out['m_e_conv_w'], 'm_e_conv_b': out['m_e_conv_b'], 'm_e_dt_bias': out['m_e_dt_bias'], 'm_e_a_log': out['m_e_a_log'], 'm_e_d_skip': out['m_e_d_skip'], 'm_e_ssd_norm_w': out['m_e_ssd_norm_w'], 'm_e_sgu_w': out['m_e_sgu_w'], 'm_e_sgu_b': out['m_e_sgu_b'], 'm_e_w_out': out['m_e_w_out'], 'm_o_w_qkv': out['m_o_w_qkv'], 'm_o_sink': out['m_o_sink'], 'm_o_w_out': out['m_o_w_out'], 'v_c_ctx': out['v_c_ctx'], 'v_w_mod': out['v_w_mod'], 'v_b_mod': out['v_b_mod'], 'v_norm_w': out['v_norm_w'], 'v_w_ffn_in': out['v_w_ffn_in'], 'v_w_ffn_out': out['v_w_ffn_out'], 'v_e_w_in': out['v_e_w_in'], 'v_e_conv_w': out['v_e_conv_w'], 'v_e_conv_b': out['v_e_conv_b'], 'v_e_dt_bias': out['v_e_dt_bias'], 'v_e_a_log': out['v_e_a_log'], 'v_e_d_skip': out['v_e_d_skip'], 'v_e_ssd_norm_w': out['v_e_ssd_norm_w'], 'v_e_sgu_w': out['v_e_sgu_w'], 'v_e_sgu_b': out['v_e_sgu_b'], 'v_e_w_out': out['v_e_w_out'], 'v_o_w_qkv': out['v_o_w_qkv'], 'v_o_sink': out['v_o_sink'], 'v_o_w_out': out['v_o_w_out']}


def _loss(weights, diff, rest, loss_target):
    with _jax.named_scope("forward"):
        args = {**rest, TWIN_DIFF_INPUT: diff, **{k: w.astype(_WEIGHT_DTYPES[k]) for k, w in weights.items()}}
        y = _forward(args)
    with _jax.named_scope("loss_head"):
        err = _jnp.square(y.astype(_jnp.float32) - loss_target)
        return 0.5 * _jnp.sum(_jnp.mean(err, axis=-1)) if err.ndim else 0.5 * err


def _adamw(w, g, m, v):
    m = ADAM_B1 * m + (1.0 - ADAM_B1) * g
    v = ADAM_B2 * v + (1.0 - ADAM_B2) * _jnp.square(g)
    m_hat = m / (1.0 - ADAM_B1 ** ADAM_STEP)
    v_hat = v / (1.0 - ADAM_B2 ** ADAM_STEP)
    delta = -ADAM_LR * (m_hat / (_jnp.sqrt(v_hat) + ADAM_EPS) + ADAM_WD * w)
    return delta, m, v


def reference(x, c, ctx, c_ctx, w_mod, b_mod, norm_w, w_ffn_in, w_ffn_out, e_w_in, e_conv_w, e_conv_b, e_dt_bias, e_a_log, e_d_skip, e_ssd_norm_w, e_sgu_w, e_sgu_b, e_w_out, o_w_qkv, o_sink, o_w_out, loss_target, m_c_ctx, m_w_mod, m_b_mod, m_norm_w, m_w_ffn_in, m_w_ffn_out, m_e_w_in, m_e_conv_w, m_e_conv_b, m_e_dt_bias, m_e_a_log, m_e_d_skip, m_e_ssd_norm_w, m_e_sgu_w, m_e_sgu_b, m_e_w_out, m_o_w_qkv, m_o_sink, m_o_w_out, v_c_ctx, v_w_mod, v_b_mod, v_norm_w, v_w_ffn_in, v_w_ffn_out, v_e_w_in, v_e_conv_w, v_e_conv_b, v_e_dt_bias, v_e_a_log, v_e_d_skip, v_e_ssd_norm_w, v_e_sgu_w, v_e_sgu_b, v_e_w_out, v_o_w_qkv, v_o_sink, v_o_w_out):
    given = dict(x=x, c=c, ctx=ctx, c_ctx=c_ctx, w_mod=w_mod, b_mod=b_mod, norm_w=norm_w, w_ffn_in=w_ffn_in, w_ffn_out=w_ffn_out, e_w_in=e_w_in, e_conv_w=e_conv_w, e_conv_b=e_conv_b, e_dt_bias=e_dt_bias, e_a_log=e_a_log, e_d_skip=e_d_skip, e_ssd_norm_w=e_ssd_norm_w, e_sgu_w=e_sgu_w, e_sgu_b=e_sgu_b, e_w_out=e_w_out, o_w_qkv=o_w_qkv, o_sink=o_sink, o_w_out=o_w_out, loss_target=loss_target, m_c_ctx=m_c_ctx, m_w_mod=m_w_mod, m_b_mod=m_b_mod, m_norm_w=m_norm_w, m_w_ffn_in=m_w_ffn_in, m_w_ffn_out=m_w_ffn_out, m_e_w_in=m_e_w_in, m_e_conv_w=m_e_conv_w, m_e_conv_b=m_e_conv_b, m_e_dt_bias=m_e_dt_bias, m_e_a_log=m_e_a_log, m_e_d_skip=m_e_d_skip, m_e_ssd_norm_w=m_e_ssd_norm_w, m_e_sgu_w=m_e_sgu_w, m_e_sgu_b=m_e_sgu_b, m_e_w_out=m_e_w_out, m_o_w_qkv=m_o_w_qkv, m_o_sink=m_o_sink, m_o_w_out=m_o_w_out, v_c_ctx=v_c_ctx, v_w_mod=v_w_mod, v_b_mod=v_b_mod, v_norm_w=v_norm_w, v_w_ffn_in=v_w_ffn_in, v_w_ffn_out=v_w_ffn_out, v_e_w_in=v_e_w_in, v_e_conv_w=v_e_conv_w, v_e_conv_b=v_e_conv_b, v_e_dt_bias=v_e_dt_bias, v_e_a_log=v_e_a_log, v_e_d_skip=v_e_d_skip, v_e_ssd_norm_w=v_e_ssd_norm_w, v_e_sgu_w=v_e_sgu_w, v_e_sgu_b=v_e_sgu_b, v_e_w_out=v_e_w_out, v_o_w_qkv=v_o_w_qkv, v_o_sink=v_o_sink, v_o_w_out=v_o_w_out)
    weights = {n: given[n] for n in TWIN_WEIGHTS}
    shared = {n: given[n] for n in SHARED_INPUTS}
    per_example = {n: given[n] for n in ['x', 'c', 'ctx']}
    grad_fn = _jax.value_and_grad(_loss, argnums=(0, 1))

    def one_microbatch(ex, loss_target):
        ex = dict(ex)
        diff = ex.pop(TWIN_DIFF_INPUT)
        return grad_fn(weights, diff, {**shared, **ex}, loss_target)

    if N_MICROBATCH == 1:
        loss, (grad_w, grad_x) = one_microbatch(per_example, given["loss_target"])
    else:
        def body(carry, xs):
            loss_sum, grad_sum = carry
            l_k, (gw_k, gx_k) = one_microbatch(xs[0], xs[1])
            with _jax.named_scope("update"):
                return (loss_sum + l_k, _jax.tree.map(_jnp.add, grad_sum, gw_k)), gx_k

        init = (_jnp.zeros((), _jnp.float32), _jax.tree.map(_jnp.zeros_like, weights))
        (loss, grad_w), grad_x = _jax.lax.scan(body, init, (per_example, given["loss_target"]))
    with _jax.named_scope("update"):
        delta_w, new_m, new_v = {}, {}, {}
        for n in TWIN_WEIGHTS:
            delta_w[n], new_m[n], new_v[n] = _adamw(weights[n], grad_w[n], given["m_" + n], given["v_" + n])
    return (loss, grad_x, *[grad_w[n] for n in TWIN_WEIGHTS], *[delta_w[n] for n in TWIN_WEIGHTS],
            *[new_m[n] for n in TWIN_WEIGHTS], *[new_v[n] for n in TWIN_WEIGHTS])
```

```python
import functools
import math

import jax
import jax.numpy as jnp
from jax import lax
from jax.experimental import pallas as pl
from jax.experimental.pallas import tpu as pltpu

F32 = jnp.float32
BF16 = jnp.bfloat16
HI = lax.Precision.HIGHEST

NORM_EPS = 1e-6
SSD_HEAD_DIM = 64
SSD_STATE = 128
CHUNK = 128
CONV_K = 5
ATT_HEAD_DIM = 128
ATT_GROUP = 4
ROPE_BASE = 10000.0
GRID_W = 64
NEG_INF = -1e30
ADAM_LR, ADAM_B1, ADAM_B2, ADAM_EPS, ADAM_WD, ADAM_STEP = 0.001, 0.9, 0.999, 1e-08, 0.01, 10

LANE = 128
VMEM_LIMIT = 56 * 1024 * 1024
MESH_ID = pl.DeviceIdType.MESH
N_DEV = 8
N_CHIP = 4


def _cp(sem=None):
    return pltpu.CompilerParams(dimension_semantics=sem, vmem_limit_bytes=VMEM_LIMIT)


def _sds(shape, dtype):
    return jax.ShapeDtypeStruct(tuple(shape), dtype)


def _pick(n, cands):
    for c in cands:
        if n % c == 0:
            return c
    for step in (LANE, 16):
        for c in range(min(n, cands[0]) // step * step, 0, -step):
            if n % c == 0:
                return c
    raise ValueError((n, cands))


def _w_index(blocked, layer, per_block_k, per_block_n):
    def idx(kblk, nblk):
        if blocked == "n":
            return (nblk // per_block_n, layer, kblk, nblk % per_block_n)
        return (kblk // per_block_k, layer, kblk % per_block_k, nblk)
    return idx


def mm_nn(a, w, layer, blocked, *, name, out_dtype=F32, tm=None, tn=None, tk=None):
    m, k_total = a.shape
    cb, _, kd, nd = w.shape
    n_total = nd * cb if blocked == "n" else nd
    assert k_total == (kd if blocked == "n" else kd * cb)
    tm = tm or _pick(m, (1088, 192))
    tn = tn or _pick(nd, (512, 384, 256))
    tk = tk or _pick(kd, (2048, 1408, 512))
    nk = k_total // tk
    widx = _w_index(blocked, layer, kd // tk, nd // tn)

    def body(a_ref, w_ref, o_ref, acc_ref):
        kk = pl.program_id(2)
        part = jnp.dot(a_ref[...].astype(BF16), w_ref[...].astype(BF16), preferred_element_type=F32)

        @pl.when(kk == 0)
        def _():
            acc_ref[...] = part

        @pl.when(kk > 0)
        def _():
            acc_ref[...] += part

        @pl.when(kk == nk - 1)
        def _():
            o_ref[...] = acc_ref[...].astype(o_ref.dtype)

    return pl.pallas_call(
        body, name=name, grid=(m // tm, n_total // tn, nk),
        in_specs=[pl.BlockSpec((tm, tk), lambda i, j, k: (i, k)),
                  pl.BlockSpec((None, None, tk, tn), lambda i, j, k: widx(k, j))],
        out_specs=pl.BlockSpec((tm, tn), lambda i, j, k: (i, j)),
        out_shape=_sds((m, n_total), out_dtype),
        scratch_shapes=[pltpu.VMEM((tm, tn), F32)],
        compiler_params=_cp(("parallel", "parallel", "arbitrary")),
    )(a, w)


def mm_nt(dy, w, layer, blocked, *, name, out_dtype=F32, tm=None, tn=None, tk=None):
    m, n_total = dy.shape
    cb, _, kd, nd = w.shape
    k_total = kd if blocked == "n" else kd * cb
    assert n_total == (nd * cb if blocked == "n" else nd)
    tm = tm or _pick(m, (1088, 192))
    tn = tn or _pick(kd, (512, 1408))
    tk = tk or _pick(nd, (1408, 1024, 768))
    nk = n_total // tk
    widx = _w_index(blocked, layer, kd // tn, nd // tk)

    def body(a_ref, w_ref, o_ref, acc_ref):
        kk = pl.program_id(2)
        part = lax.dot_general(a_ref[...].astype(BF16), w_ref[...].astype(BF16), (((1,), (1,)), ((), ())),
                               preferred_element_type=F32)

        @pl.when(kk == 0)
        def _():
            acc_ref[...] = part

        @pl.when(kk > 0)
        def _():
            acc_ref[...] += part

        @pl.when(kk == nk - 1)
        def _():
            o_ref[...] = acc_ref[...].astype(o_ref.dtype)

    return pl.pallas_call(
        body, name=name, grid=(m // tm, k_total // tn, nk),
        in_specs=[pl.BlockSpec((tm, tk), lambda i, j, k: (i, k)),
                  pl.BlockSpec((None, None, tn, tk), lambda i, j, k: widx(j, k))],
        out_specs=pl.BlockSpec((tm, tn), lambda i, j, k: (i, j)),
        out_shape=_sds((m, k_total), out_dtype),
        scratch_shapes=[pltpu.VMEM((tm, tn), F32)],
        compiler_params=_cp(("parallel", "parallel", "arbitrary")),
    )(dy, w)


def mm_tn(x, dy, g, layer, blocked, *, name, tm=None, tn=None, tt=None):
    t_total, k_total = x.shape
    n_total = dy.shape[1]
    cb, _, kd, nd = g.shape
    assert k_total == (kd if blocked == "n" else kd * cb) and n_total == (nd * cb if blocked == "n" else nd)
    tm = tm or _pick(kd, (1024, 1408, 512))
    tn = tn or _pick(nd, (512, 384, 256))
    tt = tt or _pick(t_total, (1088, 96))
    nt = t_total // tt
    widx = _w_index(blocked, layer, kd // tm, nd // tn)

    def body(x_ref, dy_ref, g_in, o_ref, acc_ref):
        del g_in
        tstep = pl.program_id(2)
        part = lax.dot_general(x_ref[...].astype(BF16), dy_ref[...].astype(BF16), (((0,), (0,)), ((), ())),
                               preferred_element_type=F32)

        @pl.when(tstep == 0)
        def _():
            acc_ref[...] = part

        @pl.when(tstep > 0)
        def _():
            acc_ref[...] += part

        @pl.when(tstep == nt - 1)
        def _():
            o_ref[...] = acc_ref[...].astype(o_ref.dtype)

    return pl.pallas_call(
        body, name=name, grid=(k_total // tm, n_total // tn, nt),
        in_specs=[pl.BlockSpec((tt, tm), lambda i, j, t: (t, i)),
                  pl.BlockSpec((tt, tn), lambda i, j, t: (t, j)),
                  pl.BlockSpec(memory_space=pl.ANY)],
        out_specs=pl.BlockSpec((None, None, tm, tn), lambda i, j, t: widx(i, j)),
        out_shape=_sds(g.shape, g.dtype),
        scratch_shapes=[pltpu.VMEM((tm, tn), F32)],
        input_output_aliases={2: 0},
        compiler_params=_cp(("parallel", "parallel", "arbitrary")),
    )(x, dy, g)


def _rms(x, w):
    return x * lax.rsqrt(jnp.mean(x * x, axis=-1, keepdims=True) + NORM_EPS) * w


def _row_tile(tc):
    return 256 if tc % 256 == 0 else 128


def _seg_spec(nct, d):
    return pl.BlockSpec((None, 8, d), lambda i: (jnp.minimum(i // nct, 1), 0, 0))


def _acc_rows(acc_ref, i, nct, rows):
    @pl.when((i == 0) | (i == nct))
    def _():
        acc_ref[...] = jnp.zeros_like(acc_ref)

    for r, val in enumerate(rows):
        acc_ref[r:r + 1, :] += val


def norm_mod_fwd(u, nw, modtab, tc, which, *, name):
    t, d = u.shape
    tr = _row_tile(tc)
    nct = tc // tr
    r0 = 3 * which

    def body(u_ref, nw_ref, mt_ref, h_ref):
        sh, sc = mt_ref[r0:r0 + 1, :], mt_ref[r0 + 1:r0 + 2, :]
        h_ref[...] = (_rms(u_ref[...], nw_ref[...]) * (1.0 + sc) + sh).astype(h_ref.dtype)

    return pl.pallas_call(
        body, name=name, grid=(t // tr,),
        in_specs=[pl.BlockSpec((tr, d), lambda i: (i, 0)), pl.BlockSpec((1, d), lambda i: (0, 0)), _seg_spec(nct, d)],
        out_specs=pl.BlockSpec((tr, d), lambda i: (i, 0)),
        out_shape=_sds((t, d), BF16), compiler_params=_cp(("arbitrary",)),
    )(u, nw.reshape(1, d), modtab)


def norm_mod_bwd(u, nw, modtab, dh, du_in, tc, which, *, name):
    t, d = u.shape
    tr = _row_tile(tc)
    nct = tc // tr
    r0 = 3 * which

    def body(u_ref, nw_ref, mt_ref, dh_ref, dui_ref, du_ref, acc_ref):
        i = pl.program_id(0)
        sh, sc = mt_ref[r0:r0 + 1, :], mt_ref[r0 + 1:r0 + 2, :]
        _, vjp = jax.vjp(lambda x, w, a, b: _rms(x, w) * (1.0 + b) + a, u_ref[...], nw_ref[...], sh, sc)
        dx, dw, dsh, dsc = vjp(dh_ref[...].astype(F32))
        du_ref[...] = dui_ref[...] + dx
        _acc_rows(acc_ref, i, nct, (dw, dsh, dsc))

    row = pl.BlockSpec((tr, d), lambda i: (i, 0))
    return pl.pallas_call(
        body, name=name, grid=(t // tr,),
        in_specs=[row, pl.BlockSpec((1, d), lambda i: (0, 0)), _seg_spec(nct, d), row, row],
        out_specs=(row, _seg_spec(nct, d)),
        out_shape=(_sds((t, d), F32), _sds((2, 8, d), F32)), compiler_params=_cp(("arbitrary",)),
    )(u, nw.reshape(1, d), modtab, dh, du_in)


def resid_fwd(u, o, nw, modtab, tc, which, *, name):
    t, d = u.shape
    tr = _row_tile(tc)
    nct = tc // tr
    r0 = 3 * which + 2

    def body(u_ref, o_ref, nw_ref, mt_ref, out_ref):
        out_ref[...] = u_ref[...] + mt_ref[r0:r0 + 1, :] * _rms(o_ref[...], nw_ref[...])

    row = pl.BlockSpec((tr, d), lambda i: (i, 0))
    return pl.pallas_call(
        body, name=name, grid=(t // tr,),
        in_specs=[row, row, pl.BlockSpec((1, d), lambda i: (0, 0)), _seg_spec(nct, d)],
        out_specs=row, out_shape=_sds((t, d), F32), compiler_params=_cp(("arbitrary",)),
    )(u, o, nw.reshape(1, d), modtab)


def resid_bwd(o, nw, modtab, du, tc, which, *, name):
    t, d = o.shape
    tr = _row_tile(tc)
    nct = tc // tr
    r0 = 3 * which + 2

    def body(o_ref, nw_ref, mt_ref, du_ref, do_ref, acc_ref):
        i = pl.program_id(0)
        _, vjp = jax.vjp(lambda x, w, g: g * _rms(x, w), o_ref[...], nw_ref[...], mt_ref[r0:r0 + 1, :])
        dx, dw, dg = vjp(du_ref[...])
        do_ref[...] = dx.astype(do_ref.dtype)
        _acc_rows(acc_ref, i, nct, (dw, dg))

    row = pl.BlockSpec((tr, d), lambda i: (i, 0))
    return pl.pallas_call(
        body, name=name, grid=(t // tr,),
        in_specs=[row, pl.BlockSpec((1, d), lambda i: (0, 0)), _seg_spec(nct, d), row],
        out_specs=(row, _seg_spec(nct, d)),
        out_shape=(_sds((t, d), BF16), _sds((2, 8, d), F32)), compiler_params=_cp(("arbitrary",)),
    )(o, nw.reshape(1, d), modtab, du)


def swiglu_fwd(p, *, name):
    t, h2 = p.shape
    h = h2 // 2
    tr = _pick(t, (1088, 192))
    tc = _pick(h, (512, 256))
    nh = h // tc

    def body(g_ref, u_ref, a_ref):
        a_ref[...] = (jax.nn.silu(g_ref[...]) * u_ref[...]).astype(a_ref.dtype)

    return pl.pallas_call(
        body, name=name, grid=(t // tr, nh),
        in_specs=[pl.BlockSpec((tr, tc), lambda i, j: (i, j)), pl.BlockSpec((tr, tc), lambda i, j: (i, j + nh))],
        out_specs=pl.BlockSpec((tr, tc), lambda i, j: (i, j)),
        out_shape=_sds((t, h), BF16), compiler_params=_cp(("parallel", "parallel")),
    )(p, p)


def swiglu_bwd(p, da, *, name):
    t, h2 = p.shape
    h = h2 // 2
    tr = _pick(t, (1088, 192))
    tc = _pick(h, (512, 256))
    nh = h // tc

    def body(g_ref, u_ref, da_ref, dp_ref):
        j = pl.program_id(1)
        _, vjp = jax.vjp(lambda g, u: jax.nn.silu(g) * u, g_ref[...], u_ref[...])
        dg, du = vjp(da_ref[...])

        @pl.when(j < nh)
        def _():
            dp_ref[...] = dg.astype(dp_ref.dtype)

        @pl.when(j >= nh)
        def _():
            dp_ref[...] = du.astype(dp_ref.dtype)

    return pl.pallas_call(
        body, name=name, grid=(t // tr, 2 * nh),
        in_specs=[pl.BlockSpec((tr, tc), lambda i, j: (i, j % nh)), pl.BlockSpec((tr, tc), lambda i, j: (i, j % nh + nh)),
                  pl.BlockSpec((tr, tc), lambda i, j: (i, j % nh))],
        out_specs=pl.BlockSpec((tr, tc), lambda i, j: (i, j)),
        out_shape=_sds((t, h2), BF16), compiler_params=_cp(("parallel", "parallel")),
    )(p, p, da)


def loss_fwd_bwd(u, target, tc, *, name):
    t, d = u.shape
    tr = _row_tile(tc)
    nct = tc // tr

    def body(u_ref, t_ref, loss_ref, du_ref):
        i = pl.program_id(0)

        @pl.when(i == 0)
        def _():
            loss_ref[...] = jnp.zeros_like(loss_ref)

        @pl.when(i < nct)
        def _():
            du_ref[...] = jnp.zeros_like(du_ref)

        @pl.when(i >= nct)
        def _():
            err = u_ref[...] - t_ref[...]
            du_ref[...] = err * (1.0 / d)
            loss_ref[...] += jnp.sum(jnp.sum(err * err, axis=1, keepdims=True), axis=0, keepdims=True) * (0.5 / d)

    return pl.pallas_call(
        body, name=name, grid=(t // tr,),
        in_specs=[pl.BlockSpec((tr, d), lambda i: (i, 0)), pl.BlockSpec((tr, d), lambda i: (jnp.maximum(i - nct, 0), 0))],
        out_specs=(pl.BlockSpec((1, 1), lambda i: (0, 0)), pl.BlockSpec((tr, d), lambda i: (i, 0))),
        out_shape=(_sds((1, 1), F32), _sds((t, d), F32)), compiler_params=_cp(("arbitrary",)),
    )(u, target)


SSD_INNER = 1024
SGU_WIDTH = 1024
XBC_DIM = 1536
EVEN_COLS = 4640
EVEN_PAD_COLS = 5120
Z_BLK, U_BLK, V_BLK, X_BLK, B_BLK, C_BLK, DT_BLK = 0, 8, 16, 24, 32, 34, 36
PAD_ROWS = 8


def _conv_scratch_fill(pad_ref, val, tc, s):
    pad_ref[...] = jnp.zeros_like(pad_ref)
    pad_ref[PAD_ROWS:PAD_ROWS + tc, :] = val[:tc]
    pad_ref[2 * PAD_ROWS + tc:2 * PAD_ROWS + tc + s, :] = val[tc:]


def _conv_taps(pad_ref, tc, s, k):
    off = k - CONV_K // 2
    return (pad_ref[PAD_ROWS + off:PAD_ROWS + off + tc, :],
            pad_ref[2 * PAD_ROWS + tc + off:2 * PAD_ROWS + tc + off + s, :])


def conv_fwd(p, wb, tc, *, name):
    t = p.shape[0]
    s = t - tc
    nblk = XBC_DIM // LANE

    def body(p_ref, wb_ref, out_ref, pad_ref):
        _conv_scratch_fill(pad_ref, p_ref[...], tc, s)
        acc_c = jnp.zeros((tc, LANE), F32) + wb_ref[5:6, :]
        acc_l = jnp.zeros((s, LANE), F32) + wb_ref[5:6, :]
        for k in range(CONV_K):
            xc, xl = _conv_taps(pad_ref, tc, s, k)
            acc_c += xc * wb_ref[k:k + 1, :]
            acc_l += xl * wb_ref[k:k + 1, :]
        out_ref[:tc, :] = jax.nn.silu(acc_c)
        out_ref[tc:, :] = jax.nn.silu(acc_l)

    return pl.pallas_call(
        body, name=name, grid=(nblk,),
        in_specs=[pl.BlockSpec((t, LANE), lambda j: (0, X_BLK + j)), pl.BlockSpec((8, LANE), lambda j: (0, j))],
        out_specs=pl.BlockSpec((t, LANE), lambda j: (0, j)),
        out_shape=_sds((t, XBC_DIM), F32),
        scratch_shapes=[pltpu.VMEM((t + 3 * PAD_ROWS, LANE), F32)],
        compiler_params=_cp(("parallel",)),
    )(p, wb)


def conv_bwd(p, wb, dxg, dskip, tc, *, name):
    t = p.shape[0]
    s = t - tc
    nblk = XBC_DIM // LANE
    nx = SSD_INNER // LANE

    def grp(j):
        return jnp.where(j < nx, j // 4, (j - nx) % 2)

    def sub(j):
        return jnp.where(j < nx, j % 4, 4 + (j - nx) // 2)

    def body(p_ref, wb_ref, d0_ref, d1_ref, ds_ref, dp_ref, dwb_ref, pad_ref, dpad_ref):
        j = pl.program_id(0)
        _conv_scratch_fill(pad_ref, p_ref[...], tc, s)
        pre = [jnp.zeros((tc, LANE), F32) + wb_ref[5:6, :], jnp.zeros((s, LANE), F32) + wb_ref[5:6, :]]
        for k in range(CONV_K):
            xc, xl = _conv_taps(pad_ref, tc, s, k)
            pre[0] += xc * wb_ref[k:k + 1, :]
            pre[1] += xl * wb_ref[k:k + 1, :]
        dx = d0_ref[...] + d1_ref[...] + jnp.where(j < nx, ds_ref[...], 0.0)
        dpre = []
        for part, rows in ((0, slice(0, tc)), (1, slice(tc, t))):
            sig = jax.nn.sigmoid(pre[part])
            dpre.append(dx[rows] * (sig * (1.0 + pre[part] * (1.0 - sig))))
        dwb_ref[...] = jnp.zeros_like(dwb_ref)
        dwb_ref[5:6, :] = jnp.sum(dpre[0], axis=0, keepdims=True) + jnp.sum(dpre[1], axis=0, keepdims=True)
        for k in range(CONV_K):
            xc, xl = _conv_taps(pad_ref, tc, s, k)
            dwb_ref[k:k + 1, :] = (jnp.sum(dpre[0] * xc, axis=0, keepdims=True)
                                   + jnp.sum(dpre[1] * xl, axis=0, keepdims=True))
        _conv_scratch_fill(dpad_ref, jnp.concatenate(dpre, axis=0), tc, s)
        acc_c = jnp.zeros((tc, LANE), F32)
        acc_l = jnp.zeros((s, LANE), F32)
        for k in range(CONV_K):
            gc, gl = _conv_taps(dpad_ref, tc, s, CONV_K - 1 - k)
            acc_c += gc * wb_ref[k:k + 1, :]
            acc_l += gl * wb_ref[k:k + 1, :]
        dp_ref[:tc, :] = acc_c.astype(dp_ref.dtype)
        dp_ref[tc:, :] = acc_l.astype(dp_ref.dtype)

    col = pl.BlockSpec((t, LANE), lambda j: (0, j))
    return pl.pallas_call(
        body, name=name, grid=(nblk,),
        in_specs=[pl.BlockSpec((t, LANE), lambda j: (0, X_BLK + j)), pl.BlockSpec((8, LANE), lambda j: (0, j)),
                  pl.BlockSpec((None, None, t, LANE), lambda j: (0, grp(j), 0, sub(j))),
                  pl.BlockSpec((None, None, t, LANE), lambda j: (1, grp(j), 0, sub(j))),
                  pl.BlockSpec((t, LANE), lambda j: (0, jnp.minimum(j, nx - 1)))],
        out_specs=(col, pl.BlockSpec((8, LANE), lambda j: (0, j))),
        out_shape=(_sds((t, XBC_DIM), BF16), _sds((8, XBC_DIM), F32)),
        scratch_shapes=[pltpu.VMEM((t + 3 * PAD_ROWS, LANE), F32), pltpu.VMEM((t + 3 * PAD_ROWS, LANE), F32)],
        compiler_params=_cp(("parallel",)),
    )(p, wb, dxg, dxg, dskip)


HEADS_PER_DG = 8


def _ssd_prep_fn(pre, bias, alog, rev):
    q = pre.shape[0]
    lane = lax.broadcasted_iota(jnp.int32, (1, LANE), 1)
    dt = jnp.where(lane < HEADS_PER_DG, jax.nn.softplus(pre + bias), 0.0)
    row = lax.broadcasted_iota(jnp.int32, (q, q), 0)
    col = lax.broadcasted_iota(jnp.int32, (q, q), 1)
    tri = jnp.where((col - row) * jnp.where(rev, 1, -1) >= 0, 1.0, 0.0)
    cs = jnp.dot(tri, dt * (-jnp.exp(alog)), precision=HI, preferred_element_type=F32)
    return dt, cs


def _scan_chunk(nc_ctx, nch):
    def idx(dg, i):
        fwd = i
        bwd = jnp.where(i < nc_ctx, nc_ctx - 1 - i, nch - 1 - (i - nc_ctx))
        return jnp.where(dg // 2 == 0, fwd, bwd)
    return idx


def ssd_prep_fwd(pre, bias, alog, *, name):
    _, t, _ = pre.shape
    blk = pl.BlockSpec((None, CHUNK, LANE), lambda dg, i: (dg, i, 0))
    par = pl.BlockSpec((None, 1, LANE), lambda dg, i: (dg, 0, 0))

    def body(pre_ref, b_ref, a_ref, dt_ref, cs_ref):
        dt, cs = _ssd_prep_fn(pre_ref[...], b_ref[...], a_ref[...], pl.program_id(0) // 2 == 1)
        dt_ref[...] = dt
        cs_ref[...] = cs

    return pl.pallas_call(
        body, name=name, grid=(4, t // CHUNK), in_specs=[blk, par, par], out_specs=(blk, blk),
        out_shape=(_sds(pre.shape, F32), _sds(pre.shape, F32)), compiler_params=_cp(("parallel", "parallel")),
    )(pre, bias, alog)


def ssd_prep_bwd(pre, bias, alog, ddt, dcs, *, name):
    _, t, _ = pre.shape
    blk = pl.BlockSpec((None, CHUNK, LANE), lambda dg, i: (dg, i, 0))
    par = pl.BlockSpec((None, 1, LANE), lambda dg, i: (dg, 0, 0))

    def body(pre_ref, b_ref, a_ref, ddt_ref, dcs_ref, dpre_ref, acc_ref):
        rev = pl.program_id(0) // 2 == 1
        _, vjp = jax.vjp(lambda x, b, a: _ssd_prep_fn(x, b, a, rev), pre_ref[...], b_ref[...], a_ref[...])
        dpre, db, da = vjp((ddt_ref[...], dcs_ref[...]))
        dpre_ref[...] = dpre

        @pl.when(pl.program_id(1) == 0)
        def _():
            acc_ref[...] = jnp.zeros_like(acc_ref)

        acc_ref[0:1, :] += db
        acc_ref[1:2, :] += da

    return pl.pallas_call(
        body, name=name, grid=(4, t // CHUNK), in_specs=[blk, par, par, blk, blk],
        out_specs=(blk, pl.BlockSpec((None, 8, LANE), lambda dg, i: (dg, 0, 0))),
        out_shape=(_sds(pre.shape, F32), _sds((4, 8, LANE), F32)), compiler_params=_cp(("parallel", "arbitrary")),
    )(pre, bias, alog, ddt, dcs)


def _onehot_col(a, h):
    lane = lax.broadcasted_iota(jnp.int32, (1, a.shape[1]), 1)
    return jnp.sum(jnp.where(lane == h, a, 0.0), axis=1, keepdims=True)


def _onehot_row(a, h):
    sub = lax.broadcasted_iota(jnp.int32, (a.shape[0], 1), 0)
    return jnp.sum(jnp.where(sub == h, a, 0.0), axis=0, keepdims=True)


def _bdot(a, b, dims):
    return lax.dot_general(a.astype(BF16), b.astype(BF16), (dims, ((), ())), preferred_element_type=F32)


def _ssd_pair(xblk, dt, cs, bm, cm, sp, rev, pair):
    q = xblk.shape[0]
    lane = lax.broadcasted_iota(jnp.int32, (1, LANE), 1)
    sub = lax.broadcasted_iota(jnp.int32, (LANE, 1), 0)
    row = lax.broadcasted_iota(jnp.int32, (q, q), 0)
    col = lax.broadcasted_iota(jnp.int32, (q, q), 1)
    mask = (col - row) * jnp.where(rev, 1, -1) >= 0
    last = jnp.where(rev, 0, q - 1)
    cs_t = cs.T
    g = _bdot(cm, bm, ((1,), (1,)))
    y = jnp.zeros((q, LANE), F32)
    escale = jnp.zeros((q, LANE), F32)
    xw = jnp.zeros((q, LANE), F32)
    dec = jnp.zeros((LANE, 1), F32)
    for hh in range(2):
        h = 2 * pair + hh
        c_col = _onehot_col(cs, h)
        c_row = _onehot_row(cs_t, h)
        tot = jnp.sum(jnp.where(lax.broadcasted_iota(jnp.int32, (1, q), 1) == last, c_row, 0.0), axis=1, keepdims=True)
        in_head = (lane >= hh * SSD_HEAD_DIM) & (lane < (hh + 1) * SSD_HEAD_DIM)
        xh = jnp.where(in_head, xblk * _onehot_col(dt, h), 0.0)
        ldec = jnp.where(mask, jnp.exp(jnp.where(mask, c_col - c_row, 0.0)), 0.0)
        y = y + _bdot(g * ldec, xh, ((1,), (0,)))
        escale = escale + jnp.where(in_head, jnp.exp(c_col), 0.0)
        xw = xw + xh * jnp.exp(tot - c_col)
        dec = dec + jnp.where((sub >= hh * SSD_HEAD_DIM) & (sub < (hh + 1) * SSD_HEAD_DIM), jnp.exp(tot), 0.0)
    y = y + _bdot(cm, sp, ((1,), (1,))) * escale
    s_new = sp * dec + _bdot(xw, bm, ((0,), (0,)))
    return y, s_new


def ssd_fwd(xbc, dt, cs, tc, *, name):
    t = xbc.shape[0]
    nch = t // CHUNK
    sidx = _scan_chunk(tc // CHUNK, nch)
    gw = SSD_INNER // 2
    nb = SSD_INNER // LANE

    def body(x_ref, b_ref, c_ref, dt_ref, cs_ref, y_ref, sp_ref, s_ref):
        @pl.when(pl.program_id(1) == 0)
        def _():
            s_ref[...] = jnp.zeros_like(s_ref)

        rev = pl.program_id(0) // 2 == 1
        sp_ref[...] = s_ref[...]
        for p in range(gw // LANE):
            blk = slice(p * LANE, (p + 1) * LANE)
            y, s_new = _ssd_pair(x_ref[:, blk], dt_ref[...], cs_ref[...], b_ref[...], c_ref[...], s_ref[blk, :], rev, p)
            y_ref[:, blk] = y
            s_ref[blk, :] = s_new

    return pl.pallas_call(
        body, name=name, grid=(4, nch),
        in_specs=[pl.BlockSpec((CHUNK, gw), lambda dg, i: (sidx(dg, i), dg % 2)),
                  pl.BlockSpec((CHUNK, LANE), lambda dg, i: (sidx(dg, i), nb + dg % 2)),
                  pl.BlockSpec((CHUNK, LANE), lambda dg, i: (sidx(dg, i), nb + 2 + dg % 2)),
                  pl.BlockSpec((None, CHUNK, LANE), lambda dg, i: (dg, sidx(dg, i), 0)),
                  pl.BlockSpec((None, CHUNK, LANE), lambda dg, i: (dg, sidx(dg, i), 0))],
        out_specs=(pl.BlockSpec((None, CHUNK, gw), lambda dg, i: (dg // 2, sidx(dg, i), dg % 2)),
                   pl.BlockSpec((None, None, gw, SSD_STATE), lambda dg, i: (dg, sidx(dg, i), 0, 0))),
        out_shape=(_sds((2, t, SSD_INNER), F32), _sds((4, nch, gw, SSD_STATE), F32)),
        scratch_shapes=[pltpu.VMEM((gw, SSD_STATE), F32)],
        compiler_params=_cp(("parallel", "arbitrary")),
    )(xbc, xbc, xbc, dt, cs)


def ssd_bwd(xbc, dt, cs, sprev, dy, tc, *, name):
    t = xbc.shape[0]
    nch = t // CHUNK
    fidx = _scan_chunk(tc // CHUNK, nch)
    sidx = lambda dg, i: fidx(dg, nch - 1 - i)
    gw = SSD_INNER // 2
    nb = SSD_INNER // LANE

    def body(x_ref, b_ref, c_ref, dt_ref, cs_ref, sp_ref, dy_ref, dxg_ref, ddt_ref, dcs_ref, ds_ref):
        @pl.when(pl.program_id(1) == 0)
        def _():
            ds_ref[...] = jnp.zeros_like(ds_ref)

        rev = pl.program_id(0) // 2 == 1
        ddt = jnp.zeros((CHUNK, LANE), F32)
        dcs = jnp.zeros((CHUNK, LANE), F32)
        db = jnp.zeros((CHUNK, SSD_STATE), F32)
        dc = jnp.zeros((CHUNK, SSD_STATE), F32)
        for p in range(gw // LANE):
            blk = slice(p * LANE, (p + 1) * LANE)
            _, vjp = jax.vjp(functools.partial(_ssd_pair, rev=rev, pair=p),
                             x_ref[:, blk], dt_ref[...], cs_ref[...], b_ref[...], c_ref[...], sp_ref[blk, :])
            dx, ddt_p, dcs_p, db_p, dc_p, dsp = vjp((dy_ref[:, blk], ds_ref[blk, :]))
            dxg_ref[:, blk] = dx
            ds_ref[blk, :] = dsp
            ddt, dcs, db, dc = ddt + ddt_p, dcs + dcs_p, db + db_p, dc + dc_p
        dxg_ref[:, gw:gw + SSD_STATE] = db
        dxg_ref[:, gw + SSD_STATE:] = dc
        ddt_ref[...] = ddt
        dcs_ref[...] = dcs

    hd = pl.BlockSpec((None, CHUNK, LANE), lambda dg, i: (dg, sidx(dg, i), 0))
    return pl.pallas_call(
        body, name=name, grid=(4, nch),
        in_specs=[pl.BlockSpec((CHUNK, gw), lambda dg, i: (sidx(dg, i), dg % 2)),
                  pl.BlockSpec((CHUNK, LANE), lambda dg, i: (sidx(dg, i), nb + dg % 2)),
                  pl.BlockSpec((CHUNK, LANE), lambda dg, i: (sidx(dg, i), nb + 2 + dg % 2)),
                  hd, hd,
                  pl.BlockSpec((None, None, gw, SSD_STATE), lambda dg, i: (dg, sidx(dg, i), 0, 0)),
                  pl.BlockSpec((CHUNK, gw), lambda dg, i: (sidx(dg, i), dg % 2))],
        out_specs=(pl.BlockSpec((None, None, CHUNK, gw + 2 * SSD_STATE), lambda dg, i: (dg // 2, dg % 2, sidx(dg, i), 0)),
                   hd, hd),
        out_shape=(_sds((2, 2, t, gw + 2 * SSD_STATE), F32), _sds((4, t, LANE), F32), _sds((4, t, LANE), F32)),
        scratch_shapes=[pltpu.VMEM((gw, SSD_STATE), F32)],
        compiler_params=_cp(("parallel", "arbitrary")),
    )(xbc, xbc, xbc, dt, cs, sprev, dy)


def _ssd_finish_fn(y0, y1, xs, z, dskip, nw):
    y = (y0 + y1 + xs * dskip) * jax.nn.silu(z)
    half = y.shape[1] // 2
    first = lax.broadcasted_iota(jnp.int32, (1, y.shape[1]), 1) < half
    sq = y * y
    m0 = jnp.sum(jnp.where(first, sq, 0.0), axis=1, keepdims=True) / half
    m1 = jnp.sum(jnp.where(first, 0.0, sq), axis=1, keepdims=True) / half
    return y * jnp.where(first, lax.rsqrt(m0 + NORM_EPS), lax.rsqrt(m1 + NORM_EPS)) * nw


def ssd_finish_fwd(y, xbc, p, dskip, nw, *, name):
    t = xbc.shape[0]
    tr = CHUNK
    w = SSD_INNER
    row = pl.BlockSpec((tr, w), lambda i: (i, 0))
    par = pl.BlockSpec((1, w), lambda i: (0, 0))

    def body(y0_ref, y1_ref, x_ref, z_ref, ds_ref, nw_ref, o_ref):
        o_ref[...] = _ssd_finish_fn(y0_ref[...], y1_ref[...], x_ref[...], z_ref[...], ds_ref[...], nw_ref[...]).astype(o_ref.dtype)

    return pl.pallas_call(
        body, name=name, grid=(t // tr,),
        in_specs=[pl.BlockSpec((None, tr, w), lambda i: (0, i, 0)), pl.BlockSpec((None, tr, w), lambda i: (1, i, 0)),
                  row, row, par, par],
        out_specs=row, out_shape=_sds((t, w), BF16), compiler_params=_cp(("parallel",)),
    )(y, y, xbc, p, dskip, nw)


def ssd_finish_bwd(y, xbc, p, dskip, nw, dout, *, name):
    t = xbc.shape[0]
    tr = CHUNK
    w = SSD_INNER
    row = pl.BlockSpec((tr, w), lambda i: (i, 0))
    par = pl.BlockSpec((1, w), lambda i: (0, 0))

    def body(y0_ref, y1_ref, x_ref, z_ref, ds_ref, nw_ref, do_ref, dy_ref, dx_ref, dz_ref, acc_ref):
        _, vjp = jax.vjp(_ssd_finish_fn, y0_ref[...], y1_ref[...], x_ref[...], z_ref[...], ds_ref[...], nw_ref[...])
        dy0, _, dx, dz, dds, dnw = vjp(do_ref[...])
        dy_ref[...] = dy0
        dx_ref[...] = dx
        dz_ref[...] = dz.astype(dz_ref.dtype)

        @pl.when(pl.program_id(0) == 0)
        def _():
            acc_ref[...] = jnp.zeros_like(acc_ref)

        acc_ref[0:1, :] += dds
        acc_ref[1:2, :] += dnw

    return pl.pallas_call(
        body, name=name, grid=(t // tr,),
        in_specs=[pl.BlockSpec((None, tr, w), lambda i: (0, i, 0)), pl.BlockSpec((None, tr, w), lambda i: (1, i, 0)),
                  row, row, par, par, row],
        out_specs=(row, row, row, pl.BlockSpec((8, w), lambda i: (0, 0))),
        out_shape=(_sds((t, w), F32), _sds((t, w), F32), _sds((t, w), BF16), _sds((8, w), F32)),
        compiler_params=_cp(("arbitrary",)),
    )(y, y, xbc, p, dskip, nw, dout)


SGU_GROUPS = 8


def _sgu_fn(us, vs, ws, bs):
    n = SGU_GROUPS * LANE
    vf = [jax.nn.gelu(v) for v in vs]
    mu = sum(jnp.sum(v, axis=1, keepdims=True) for v in vf) / n
    var = sum(jnp.sum(jnp.square(v - mu), axis=1, keepdims=True) for v in vf) / n
    rstd = lax.rsqrt(var + NORM_EPS)
    return tuple(jax.nn.gelu(u) * (_bdot(w, (v - mu) * rstd, ((1,), (0,))) + b) for u, v, w, b in zip(us, vf, ws, bs))


def sgu_fwd(p, w, b, *, name):
    t = p.shape[0]
    wd = SGU_WIDTH

    def body(u_ref, v_ref, w_ref, b_ref, o_ref):
        sl = [slice(g * LANE, (g + 1) * LANE) for g in range(SGU_GROUPS)]
        ys = _sgu_fn([u_ref[:, s] for s in sl], [v_ref[:, s] for s in sl], [w_ref[g] for g in range(SGU_GROUPS)],
                     [b_ref[g] for g in range(SGU_GROUPS)])
        for s, yv in zip(sl, ys):
            o_ref[:, s] = yv.astype(o_ref.dtype)

    return pl.pallas_call(
        body, name=name, grid=(t // CHUNK,),
        in_specs=[pl.BlockSpec((CHUNK, wd), lambda i: (i, U_BLK * LANE // wd)), pl.BlockSpec((CHUNK, wd), lambda i: (i, V_BLK * LANE // wd)),
                  pl.BlockSpec((SGU_GROUPS, CHUNK, CHUNK), lambda i: (0, 0, 0)), pl.BlockSpec((SGU_GROUPS, CHUNK, 1), lambda i: (0, 0, 0))],
        out_specs=pl.BlockSpec((CHUNK, wd), lambda i: (i, 0)),
        out_shape=_sds((t, wd), BF16), compiler_params=_cp(("parallel",)),
    )(p, p, w, b)


def sgu_bwd(p, w, b, dout, *, name):
    t = p.shape[0]
    wd = SGU_WIDTH

    def body(u_ref, v_ref, w_ref, b_ref, do_ref, duv_ref, dw_ref, db_ref):
        sl = [slice(g * LANE, (g + 1) * LANE) for g in range(SGU_GROUPS)]
        _, vjp = jax.vjp(_sgu_fn, [u_ref[:, s] for s in sl], [v_ref[:, s] for s in sl],
                         [w_ref[g] for g in range(SGU_GROUPS)], [b_ref[g] for g in range(SGU_GROUPS)])
        dus, dvs, dws, dbs = vjp(tuple(do_ref[:, s] for s in sl))

        @pl.when(pl.program_id(0) == 0)
        def _():
            dw_ref[...] = jnp.zeros_like(dw_ref)
            db_ref[...] = jnp.zeros_like(db_ref)

        for g, s in enumerate(sl):
            duv_ref[:, s] = dus[g].astype(duv_ref.dtype)
            duv_ref[:, slice(wd + g * LANE, wd + (g + 1) * LANE)] = dvs[g].astype(duv_ref.dtype)
            dw_ref[g] += dws[g]
            db_ref[g] += dbs[g]

    wspec = pl.BlockSpec((SGU_GROUPS, CHUNK, CHUNK), lambda i: (0, 0, 0))
    bspec = pl.BlockSpec((SGU_GROUPS, CHUNK, 1), lambda i: (0, 0, 0))
    return pl.pallas_call(
        body, name=name, grid=(t // CHUNK,),
        in_specs=[pl.BlockSpec((CHUNK, wd), lambda i: (i, U_BLK * LANE // wd)), pl.BlockSpec((CHUNK, wd), lambda i: (i, V_BLK * LANE // wd)),
                  wspec, bspec, pl.BlockSpec((CHUNK, wd), lambda i: (i, 1))],
        out_specs=(pl.BlockSpec((CHUNK, 2 * wd), lambda i: (i, 0)), wspec, bspec),
        out_shape=(_sds((t, 2 * wd), BF16), _sds(w.shape, F32), _sds(b.shape, F32)),
        compiler_params=_cp(("arbitrary",)),
    )(p, p, w, b, dout)


def even_cols_permute(w):
    z, xbc, dt, u, v = jnp.split(w, (1024, 2560, 2592, 3616), axis=-1)
    pad = jnp.zeros(w.shape[:-1] + (EVEN_PAD_COLS - EVEN_COLS,), w.dtype)
    return jnp.concatenate([z, u, v, xbc, dt, pad], axis=-1)


def even_cols_unpermute(w):
    z, u, v, xbc, dt = jnp.split(w[..., :EVEN_COLS], (1024, 2048, 3072, 4608), axis=-1)
    return jnp.concatenate([z, xbc, dt, u, v], axis=-1)


def _dt_cols(p):
    t = p.shape[0]
    d = p[:, DT_BLK * LANE:DT_BLK * LANE + 4 * HEADS_PER_DG].reshape(t, 4, HEADS_PER_DG).transpose(1, 0, 2)
    return jnp.pad(d, ((0, 0), (0, 0), (0, LANE - HEADS_PER_DG)))


def _heads_to_lanes(a):
    return jnp.pad(a.reshape(4, 1, HEADS_PER_DG), ((0, 0), (0, 0), (0, LANE - HEADS_PER_DG)))


def even_params(conv_w, conv_b, dt_bias, a_log, d_skip, ssd_nw, sgu_w, sgu_b):
    wb = jnp.concatenate([conv_w, conv_b[None], jnp.zeros((2, XBC_DIM), F32)], axis=0)
    return dict(wb=wb, dtb=_heads_to_lanes(dt_bias), alog=_heads_to_lanes(a_log),
                dskip=jnp.repeat(d_skip, SSD_HEAD_DIM)[None], ssd_nw=ssd_nw[None], sgu_w=sgu_w, sgu_b=sgu_b[..., None])


def even_mixer_fwd(h, w_in, w_out, li, ep, tc, tag):
    p = mm_nn(h, w_in, li, "n", name=f"{tag}_in")
    xbc = conv_fwd(p, ep["wb"], tc, name=f"{tag}_conv")
    pre = _dt_cols(p)
    dt, cs = ssd_prep_fwd(pre, ep["dtb"], ep["alog"], name=f"{tag}_prep")
    y, sprev = ssd_fwd(xbc, dt, cs, tc, name=f"{tag}_ssd")
    yssd = ssd_finish_fwd(y, xbc, p, ep["dskip"], ep["ssd_nw"], name=f"{tag}_fin")
    ysgu = sgu_fwd(p, ep["sgu_w"], ep["sgu_b"], name=f"{tag}_sgu")
    ymix = jnp.concatenate([yssd, ysgu], axis=1)
    o = mm_nn(ymix, w_out, li, "k", name=f"{tag}_out")
    return o, (p, xbc, pre, dt, cs, y, sprev, ymix)


def even_mixer_bwd(saved, do, h, w_in, w_out, g_in, g_out, li, ep, tc, tag):
    p, xbc, pre, dt, cs, y, sprev, ymix = saved
    t = h.shape[0]
    g_out = mm_tn(ymix, do, g_out, li, "k", name=f"{tag}_out_dw")
    dymix = mm_nt(do, w_out, li, "k", name=f"{tag}_out_dx")
    dy, dxskip, dz, acc_fin = ssd_finish_bwd(y, xbc, p, ep["dskip"], ep["ssd_nw"], dymix, name=f"{tag}_fin_b")
    duv, dsgu_w, dsgu_b = sgu_bwd(p, ep["sgu_w"], ep["sgu_b"], dymix, name=f"{tag}_sgu_b")
    dxg, ddt, dcs = ssd_bwd(xbc, dt, cs, sprev, dy, tc, name=f"{tag}_ssd_b")
    dpre, acc_prep = ssd_prep_bwd(pre, ep["dtb"], ep["alog"], ddt, dcs, name=f"{tag}_prep_b")
    dxbc, dwb = conv_bwd(p, ep["wb"], dxg, dxskip, tc, name=f"{tag}_conv_b")
    ddt_cols = dpre[:, :, :HEADS_PER_DG].transpose(1, 0, 2).reshape(t, 4 * HEADS_PER_DG).astype(BF16)
    ddt_cols = jnp.pad(ddt_cols, ((0, 0), (0, EVEN_PAD_COLS - DT_BLK * LANE - 4 * HEADS_PER_DG)))
    dp = jnp.concatenate([dz, duv, dxbc, ddt_cols], axis=1)
    dh = mm_nt(dp, w_in, li, "n", name=f"{tag}_in_dx")
    g_in = mm_tn(h, dp, g_in, li, "n", name=f"{tag}_in_dw")
    small = dict(
        e_conv_w=dwb[:CONV_K], e_conv_b=dwb[CONV_K],
        e_dt_bias=acc_prep[:, 0, :HEADS_PER_DG].reshape(2, 2 * HEADS_PER_DG),
        e_a_log=acc_prep[:, 1, :HEADS_PER_DG].reshape(2, 2 * HEADS_PER_DG),
        e_d_skip=acc_fin[0].reshape(-1, SSD_HEAD_DIM).sum(axis=1), e_ssd_norm_w=acc_fin[1],
        e_sgu_w=dsgu_w, e_sgu_b=dsgu_b[..., 0])
    return dh, g_in, g_out, small


ATT_HEADS = 16
ATT_KV = 4
Q_BLKS, K_BLKS = ATT_HEADS, ATT_KV


def rope_tables(tc, s):
    quarter = ATT_HEAD_DIM // 4
    pos = jnp.arange(s)
    inv = ROPE_BASE ** (-jnp.arange(quarter, dtype=F32) / quarter)
    a_row = (pos // GRID_W).astype(F32)[:, None] * inv
    a_col = (pos % GRID_W).astype(F32)[:, None] * inv
    cos = jnp.concatenate([jnp.cos(a_row)] * 2 + [jnp.cos(a_col)] * 2, axis=1)
    sin = jnp.concatenate([-jnp.sin(a_row), jnp.sin(a_row), -jnp.sin(a_col), jnp.sin(a_col)], axis=1)
    return (jnp.concatenate([jnp.ones((tc, ATT_HEAD_DIM), F32), cos], axis=0),
            jnp.concatenate([jnp.zeros((tc, ATT_HEAD_DIM), F32), sin], axis=0))


def _swap_halves(x):
    lane = lax.broadcasted_iota(jnp.int32, x.shape, 1)
    return jnp.where(lane % 64 < 32, pltpu.roll(x, 96, 1), pltpu.roll(x, 32, 1))


def rope_fwd(p, cos, sin, *, name):
    t = p.shape[0]
    tr = _pick(t, (1088, 640))
    scale = ATT_HEAD_DIM ** -0.5

    def body(p_ref, c_ref, s_ref, o_ref):
        x = p_ref[...]
        r = x * c_ref[...] + _swap_halves(x) * s_ref[...]
        o_ref[...] = (r * jnp.where(pl.program_id(1) < Q_BLKS, scale, 1.0)).astype(o_ref.dtype)

    tab = pl.BlockSpec((tr, LANE), lambda i, j: (i, 0))
    return pl.pallas_call(
        body, name=name, grid=(t // tr, Q_BLKS + K_BLKS),
        in_specs=[pl.BlockSpec((tr, LANE), lambda i, j: (i, j)), tab, tab],
        out_specs=pl.BlockSpec((tr, LANE), lambda i, j: (i, j)),
        out_shape=_sds((t, (Q_BLKS + K_BLKS) * LANE), BF16), compiler_params=_cp(("parallel", "parallel")),
    )(p, cos, sin)


def rope_bwd(dq, dk, dv, cos, sin, *, name):
    t = dq.shape[0]
    tr = _pick(t, (1088, 640))
    scale = ATT_HEAD_DIM ** -0.5

    def body(dq_ref, dk_ref, dv_ref, c_ref, s_ref, o_ref):
        j = pl.program_id(1)

        def unrot(g):
            return g * c_ref[...] + _swap_halves(g * s_ref[...])

        @pl.when(j < Q_BLKS)
        def _():
            o_ref[...] = (unrot(dq_ref[...]) * scale).astype(o_ref.dtype)

        @pl.when((j >= Q_BLKS) & (j < Q_BLKS + K_BLKS))
        def _():
            o_ref[...] = unrot(dk_ref[...]).astype(o_ref.dtype)

        @pl.when(j >= Q_BLKS + K_BLKS)
        def _():
            o_ref[...] = dv_ref[...].astype(o_ref.dtype)

    tab = pl.BlockSpec((tr, LANE), lambda i, j: (i, 0))
    return pl.pallas_call(
        body, name=name, grid=(t // tr, Q_BLKS + 2 * K_BLKS),
        in_specs=[pl.BlockSpec((tr, LANE), lambda i, j: (i, jnp.minimum(j, Q_BLKS - 1))),
                  pl.BlockSpec((None, tr, LANE), lambda i, j: (jnp.clip(j - Q_BLKS, 0, K_BLKS - 1), i, 0)),
                  pl.BlockSpec((None, tr, LANE), lambda i, j: (jnp.clip(j - Q_BLKS - K_BLKS, 0, K_BLKS - 1), i, 0)), tab, tab],
        out_specs=pl.BlockSpec((tr, LANE), lambda i, j: (i, j)),
        out_shape=_sds((t, (Q_BLKS + 2 * K_BLKS) * LANE), BF16), compiler_params=_cp(("parallel", "parallel")),
    )(dq, dk, dv, cos, sin)


def _attn_tile(qs, kp, kc, kn, vp, vc, vn, kx, vx, sinks, is_lat, has_prev, has_next):
    q = kp.shape[0]
    row = lax.broadcasted_iota(jnp.int32, (q, q), 0)
    col = lax.broadcasted_iota(jnp.int32, (q, q), 1)
    m_prev = (col - row) >= (1 - has_prev) * q
    m_cur = (row - row) >= (1 - is_lat)
    m_next = (row - col) >= (1 - has_next) * q
    lane = lax.broadcasted_iota(jnp.int32, (1, LANE), 1)
    outs = []
    for qh, sinkrow in zip(qs, sinks):
        sink = jnp.sum(jnp.where(lane == 0, sinkrow, 0.0), axis=1, keepdims=True)
        s_p = jnp.where(m_prev, _bdot(qh, kp, ((1,), (1,))), NEG_INF)
        s_c = jnp.where(m_cur, _bdot(qh, kc, ((1,), (1,))), NEG_INF)
        s_n = jnp.where(m_next, _bdot(qh, kn, ((1,), (1,))), NEG_INF)
        s_x = _bdot(qh, kx, ((1,), (1,)))
        mx = [jnp.max(a, axis=1, keepdims=True) for a in (s_p, s_c, s_n, s_x)]
        m = lax.stop_gradient(jnp.maximum(jnp.maximum(jnp.maximum(mx[0], mx[1]), jnp.maximum(mx[2], mx[3])), sink))
        e = [jnp.exp(a - m) for a in (s_p, s_c, s_n, s_x)]
        inv = 1.0 / (sum(jnp.sum(a, axis=1, keepdims=True) for a in e) + jnp.exp(sink - m))
        outs.append(sum(_bdot(a * inv, v, ((1,), (0,))) for a, v in zip(e, (vp, vc, vn, vx))))
    return tuple(outs)


def _attn_specs(t, tc):
    nblk = t // CHUNK
    hw = ATT_GROUP * LANE
    kcol = lambda kv: Q_BLKS + kv
    vcol = lambda kv: Q_BLKS + K_BLKS + kv
    prev = lambda n: jnp.maximum(n - 1, 0)
    nxt = lambda n: jnp.minimum(n + 1, nblk - 1)
    blk = lambda rowf, colf: pl.BlockSpec((CHUNK, LANE), lambda kv, n: (rowf(n), colf(kv)))
    same = lambda n: n
    return [pl.BlockSpec((CHUNK, hw), lambda kv, n: (n, kv)),
            blk(prev, kcol), blk(same, kcol), blk(nxt, kcol), blk(prev, vcol), blk(same, vcol), blk(nxt, vcol),
            pl.BlockSpec((tc, LANE), lambda kv, n: (0, kcol(kv))), pl.BlockSpec((tc, LANE), lambda kv, n: (0, vcol(kv))),
            pl.BlockSpec((None, 8, LANE), lambda kv, n: (kv, 0, 0))]


def _attn_args(refs, n, nct, nblk):
    q_ref, kp, kc, kn, vp, vc, vn, kx, vx, sk = refs
    f = lambda r: r[...].astype(F32)
    qs = [q_ref[:, g * LANE:(g + 1) * LANE].astype(F32) for g in range(ATT_GROUP)]
    sinks = [sk[g:g + 1, :] for g in range(ATT_GROUP)]
    flags = ((n >= nct).astype(jnp.int32), (n >= nct + 1).astype(jnp.int32), ((n >= nct) & (n + 1 < nblk)).astype(jnp.int32))
    return (qs, f(kp), f(kc), f(kn), f(vp), f(vc), f(vn), f(kx), f(vx), sinks), flags


def attn_fwd(qk, p, sink, tc, *, name):
    t = qk.shape[0]
    nblk, nct = t // CHUNK, tc // CHUNK
    hw = ATT_GROUP * LANE

    def body(*refs):
        o_ref = refs[-1]
        args, flags = _attn_args(refs[:-1], pl.program_id(1), nct, nblk)
        outs = _attn_tile(*args, *flags)
        for g, o in enumerate(outs):
            o_ref[:, g * LANE:(g + 1) * LANE] = o.astype(o_ref.dtype)

    return pl.pallas_call(
        body, name=name, grid=(ATT_KV, nblk), in_specs=_attn_specs(t, tc),
        out_specs=pl.BlockSpec((CHUNK, hw), lambda kv, n: (n, kv)),
        out_shape=_sds((t, ATT_HEADS * LANE), BF16), compiler_params=_cp(("parallel", "parallel")),
    )(qk, qk, qk, qk, p, p, p, qk, p, sink)


def attn_bwd(qk, p, sink, do, tc, *, name):
    t = qk.shape[0]
    nblk, nct = t // CHUNK, tc // CHUNK
    hw = ATT_GROUP * LANE

    def body(*refs):
        do_ref, dq_ref, dk_ref, dv_ref, dsk_ref = refs[-5:]
        n = pl.program_id(1)
        args, flags = _attn_args(refs[:-5], n, nct, nblk)
        _, vjp = jax.vjp(lambda *a: _attn_tile(*a, *flags), *args)
        dqs, dkp, dkc, dkn, dvp, dvc, dvn, dkx, dvx, dsinks = vjp(tuple(do_ref[:, g * LANE:(g + 1) * LANE] for g in range(ATT_GROUP)))

        @pl.when(n == 0)
        def _():
            dk_ref[...] = jnp.zeros_like(dk_ref)
            dv_ref[...] = jnp.zeros_like(dv_ref)
            dsk_ref[...] = jnp.zeros_like(dsk_ref)

        for g in range(ATT_GROUP):
            dq_ref[:, g * LANE:(g + 1) * LANE] = dqs[g]
            dsk_ref[g:g + 1, :] += dsinks[g]
        for blk, dkb, dvb in ((jnp.maximum(n - 1, 0), dkp, dvp), (n, dkc, dvc), (jnp.minimum(n + 1, nblk - 1), dkn, dvn)):
            rows = pl.ds(pl.multiple_of(blk * CHUNK, CHUNK), CHUNK)
            dk_ref[rows, :] += dkb
            dv_ref[rows, :] += dvb
        dk_ref[0:tc, :] += dkx
        dv_ref[0:tc, :] += dvx

    kvacc = pl.BlockSpec((None, t, LANE), lambda kv, n: (kv, 0, 0))
    return pl.pallas_call(
        body, name=name, grid=(ATT_KV, nblk),
        in_specs=_attn_specs(t, tc) + [pl.BlockSpec((CHUNK, hw), lambda kv, n: (n, kv))],
        out_specs=(pl.BlockSpec((CHUNK, hw), lambda kv, n: (n, kv)), kvacc, kvacc,
                   pl.BlockSpec((None, 8, LANE), lambda kv, n: (kv, 0, 0))),
        out_shape=(_sds((t, ATT_HEADS * LANE), F32), _sds((ATT_KV, t, LANE), F32), _sds((ATT_KV, t, LANE), F32),
                   _sds((ATT_KV, 8, LANE), F32)),
        compiler_params=_cp(("parallel", "arbitrary")),
    )(qk, qk, qk, qk, p, p, p, qk, p, sink, do)


def sink_rows(sink):
    s = jnp.broadcast_to(sink.reshape(ATT_KV, ATT_GROUP, 1), (ATT_KV, ATT_GROUP, LANE))
    return jnp.pad(s, ((0, 0), (0, 8 - ATT_GROUP), (0, 0)))


def odd_mixer_fwd(h, w_qkv, w_out, li, sink, cos, sin, tc, tag):
    p = mm_nn(h, w_qkv, li, "n", name=f"{tag}_qkv")
    qk = rope_fwd(p, cos, sin, name=f"{tag}_rope")
    att = attn_fwd(qk, p, sink, tc, name=f"{tag}_att")
    o = mm_nn(att, w_out, li, "k", name=f"{tag}_out")
    return o, (p, qk, att)


def odd_mixer_bwd(saved, do, h, w_qkv, w_out, g_qkv, g_out, li, sink, cos, sin, tc, tag):
    p, qk, att = saved
    g_out = mm_tn(att, do, g_out, li, "k", name=f"{tag}_out_dw")
    datt = mm_nt(do, w_out, li, "k", name=f"{tag}_out_dx")
    dq, dk, dv, dsink = attn_bwd(qk, p, sink, datt, tc, name=f"{tag}_att_b")
    dp = rope_bwd(dq, dk, dv, cos, sin, name=f"{tag}_rope_b")
    dh = mm_nt(dp, w_qkv, li, "n", name=f"{tag}_qkv_dx")
    g_qkv = mm_tn(h, dp, g_qkv, li, "n", name=f"{tag}_qkv_dw")
    return dh, g_qkv, g_out, dict(o_sink=dsink[:, :ATT_GROUP, 0].reshape(-1))


ANY = pl.BlockSpec(memory_space=pl.ANY)


def _place():
    return lax.axis_index("x"), lax.axis_index("y"), lax.axis_index("c")


def allgather8(blk, *, name):
    def body(x_ref, out_ref, send_sems, recv_sems, local_sem):
        x, y, c = _place()
        me, sibling = (x, y, c), (x, y, 1 - c)
        chips = [(1 - x, y), (x, 1 - y), (1 - x, 1 - y)]

        def slot(px, py, pc):
            return out_ref.at[4 * px + 2 * py + pc]

        def copy(k, block, to, src=None):
            return pltpu.make_async_remote_copy(
                src_ref=slot(*block) if src is None else src, dst_ref=slot(*block),
                send_sem=send_sems.at[k], recv_sem=recv_sems.at[k], device_id=to, device_id_type=MESH_ID)

        mine = pltpu.make_async_copy(x_ref, slot(*me), local_sem)
        mine.start()
        first = [copy(0, me, sibling, src=x_ref)]
        first += [copy(1 + j, me, (*chip, c), src=x_ref) for j, chip in enumerate(chips)]
        for cp in first:
            cp.start()
        passed = [copy(4 + j, (*chip, c), sibling) for j, chip in enumerate(chips)]
        for j, chip in enumerate(chips):
            copy(1 + j, (*chip, c), me).wait_recv()
            passed[j].start()
        copy(0, sibling, me).wait_recv()
        for j, chip in enumerate(chips):
            copy(4 + j, (*chip, 1 - c), me).wait_recv()
        for cp in first + passed:
            cp.wait_send()
        mine.wait()

    return pl.pallas_call(
        body, name=name, out_shape=_sds((N_DEV,) + blk.shape, blk.dtype), in_specs=[ANY], out_specs=ANY,
        scratch_shapes=[pltpu.SemaphoreType.DMA((7,)), pltpu.SemaphoreType.DMA((7,)), pltpu.SemaphoreType.DMA],
        compiler_params=pltpu.CompilerParams(has_side_effects=True),
    )(blk)


def rs_sibling(g, *, name):
    nchip, nl, kd, nd = g.shape
    lh = nl // 2

    def body(g_ref, out_ref, send_sem, recv_sem, local_sem):
        x, y, c = _place()
        mine = pltpu.make_async_copy(g_ref.at[:, pl.ds(c * lh, lh)], out_ref.at[0], local_sem)
        mine.start()
        theirs = pltpu.make_async_remote_copy(
            src_ref=g_ref.at[:, pl.ds((1 - c) * lh, lh)], dst_ref=out_ref.at[1], send_sem=send_sem, recv_sem=recv_sem,
            device_id=(x, y, 1 - c), device_id_type=MESH_ID)
        theirs.start()
        theirs.wait()
        mine.wait()

    return pl.pallas_call(
        body, name=name, out_shape=_sds((2, nchip, lh, kd, nd), g.dtype), in_specs=[ANY], out_specs=ANY,
        scratch_shapes=[pltpu.SemaphoreType.DMA, pltpu.SemaphoreType.DMA, pltpu.SemaphoreType.DMA],
        compiler_params=pltpu.CompilerParams(has_side_effects=True),
    )(g)


def rs_chips(h, *, name):
    def body(h_ref, out_ref, send_sems, recv_sems, local_sem):
        x, y, c = _place()
        k = 2 * x + y
        mine = pltpu.make_async_copy(h_ref.at[k], out_ref.at[k], local_sem)
        mine.start()
        copies = []
        for r in (1, 2, 3):
            px, py = (1 - x if r & 2 else x), (1 - y if r & 1 else y)
            copies.append(pltpu.make_async_remote_copy(
                src_ref=h_ref.at[2 * px + py], dst_ref=out_ref.at[k], send_sem=send_sems.at[r - 1], recv_sem=recv_sems.at[r - 1],
                device_id=(px, py, c), device_id_type=MESH_ID))
        for cp in copies:
            cp.start()
        for cp in copies:
            cp.wait()
        mine.wait()

    return pl.pallas_call(
        body, name=name, out_shape=_sds(h.shape, h.dtype), in_specs=[ANY], out_specs=ANY,
        scratch_shapes=[pltpu.SemaphoreType.DMA((3,)), pltpu.SemaphoreType.DMA((3,)), pltpu.SemaphoreType.DMA],
        compiler_params=pltpu.CompilerParams(has_side_effects=True),
    )(h)


def pair_gather(half, *, name):
    def body(h_ref, out_ref, send_sem, recv_sem, local_sem):
        x, y, c = _place()
        mine = pltpu.make_async_copy(h_ref, out_ref.at[c], local_sem)
        mine.start()
        theirs = pltpu.make_async_remote_copy(
            src_ref=h_ref, dst_ref=out_ref.at[c], send_sem=send_sem, recv_sem=recv_sem,
            device_id=(x, y, 1 - c), device_id_type=MESH_ID)
        theirs.start()
        theirs.wait()
        mine.wait()

    return pl.pallas_call(
        body, name=name, out_shape=_sds((2,) + half.shape, half.dtype), in_specs=[ANY], out_specs=ANY,
        scratch_shapes=[pltpu.SemaphoreType.DMA, pltpu.SemaphoreType.DMA, pltpu.SemaphoreType.DMA],
        compiler_params=pltpu.CompilerParams(has_side_effects=True),
    )(half)


def sum_slots(a, out_dtype, *, name):
    n = a.shape[0]
    cols = a.shape[-1]
    a3 = a.reshape(n, -1, cols)
    rows = a3.shape[1]
    tr = _pick(rows, (max(32, (1 << 19) // cols // 32 * 32),))

    def body(*refs):
        acc = refs[0][...].astype(F32)
        for r in refs[1:n]:
            acc = acc + r[...].astype(F32)
        refs[n][...] = acc.astype(out_dtype)

    return pl.pallas_call(
        body, name=name, grid=(rows // tr,),
        in_specs=[pl.BlockSpec((None, tr, cols), functools.partial(lambda j, i: (j, i, 0), j)) for j in range(n)],
        out_specs=pl.BlockSpec((tr, cols), lambda i: (i, 0)),
        out_shape=_sds((rows, cols), out_dtype), compiler_params=_cp(("parallel",)),
    )(*([a3] * n)).reshape(a.shape[1:])


def reduce_scatter_grad(g, tag):
    nchip, nl, kd, nd = g.shape
    pair = rs_sibling(g, name=f"{tag}_rs1")
    chip_sum = sum_slots(pair, BF16, name=f"{tag}_add1")
    parts = rs_chips(chip_sum, name=f"{tag}_rs2")
    half = sum_slots(parts, F32, name=f"{tag}_add2")
    return pair_gather(half, name=f"{tag}_rs3").reshape(nl, kd, nd)


def gather_weight(w, tag):
    nl = w.shape[0]
    lh = nl // 2
    half = lax.dynamic_slice_in_dim(w, lax.axis_index("c") * lh, lh, axis=0).astype(BF16)
    return allgather8(half, name=f"{tag}_ag").reshape((N_CHIP, nl) + w.shape[1:])


def mod_fwd(c16, w_mod, *, name):
    nl, d, ns = w_mod.shape
    tn = _pick(ns, (512,))

    def body(c_ref, w_ref, o_ref):
        o_ref[...] = jnp.dot(jax.nn.silu(c_ref[...]), w_ref[...], precision=HI, preferred_element_type=F32)

    return pl.pallas_call(
        body, name=name, grid=(nl, ns // tn),
        in_specs=[pl.BlockSpec((16, d), lambda l, j: (0, 0)), pl.BlockSpec((None, d, tn), lambda l, j: (l, 0, j))],
        out_specs=pl.BlockSpec((None, 16, tn), lambda l, j: (l, 0, j)),
        out_shape=_sds((nl, 16, ns), F32), compiler_params=_cp(("parallel", "parallel")),
    )(c16, w_mod)


def mod_bwd_w(c16, dm, *, name):
    nl, _, ns = dm.shape
    d = c16.shape[1]
    tn = _pick(ns, (512,))

    def body(c_ref, dm_ref, o_ref):
        o_ref[...] = lax.dot_general(jax.nn.silu(c_ref[...]), dm_ref[...], (((0,), (0,)), ((), ())), precision=HI,
                                     preferred_element_type=F32)

    return pl.pallas_call(
        body, name=name, grid=(nl, ns // tn),
        in_specs=[pl.BlockSpec((16, d), lambda l, j: (0, 0)), pl.BlockSpec((None, 16, tn), lambda l, j: (l, 0, j))],
        out_specs=pl.BlockSpec((None, d, tn), lambda l, j: (l, 0, j)),
        out_shape=_sds((nl, d, ns), F32), compiler_params=_cp(("parallel", "parallel")),
    )(c16, dm)


def mod_bwd_s(dm, w_mod, *, name):
    nl, d, ns = w_mod.shape
    td = _pick(d, (512,))

    def body(dm_ref, w_ref, o_ref):
        part = lax.dot_general(dm_ref[...], w_ref[...], (((1,), (1,)), ((), ())), precision=HI, preferred_element_type=F32)
        rowsum = jnp.sum(part[8:16], axis=0, keepdims=True)

        @pl.when(pl.program_id(1) == 0)
        def _():
            o_ref[...] = jnp.zeros_like(o_ref)

        o_ref[...] += jnp.broadcast_to(rowsum, o_ref.shape)

    return pl.pallas_call(
        body, name=name, grid=(d // td, nl),
        in_specs=[pl.BlockSpec((None, 16, ns), lambda i, l: (l, 0, 0)), pl.BlockSpec((None, td, ns), lambda i, l: (l, i, 0))],
        out_specs=pl.BlockSpec((8, td), lambda i, l: (0, i)),
        out_shape=_sds((8, d), F32), compiler_params=_cp(("parallel", "arbitrary")),
    )(dm, w_mod)


def colsum16(dm, *, name):
    nl, _, n = dm.shape
    tn = _pick(n, (2048,))

    def body(dm_ref, o_ref):
        o_ref[...] = jnp.broadcast_to(jnp.sum(dm_ref[...], axis=0, keepdims=True), o_ref.shape)

    return pl.pallas_call(
        body, name=name, grid=(nl, n // tn),
        in_specs=[pl.BlockSpec((None, 16, tn), lambda l, j: (l, 0, j))],
        out_specs=pl.BlockSpec((None, 8, tn), lambda l, j: (l, 0, j)),
        out_shape=_sds((nl, 8, n), F32), compiler_params=_cp(("parallel", "parallel")),
    )(dm)


def silu_grad_mul(g, c, *, name):
    def body(g_ref, c_ref, o_ref):
        _, vjp = jax.vjp(jax.nn.silu, c_ref[...])
        o_ref[...] = vjp(g_ref[...])[0]

    return pl.pallas_call(body, name=name, out_shape=_sds(g.shape, F32))(g, c)


def adamw(w, g, m, v, *, name):
    shape = w.shape
    cols = shape[-1] if len(shape) > 1 else LANE
    flat = [a.reshape(-1, cols) for a in (w, g, m, v)]
    rows = flat[0].shape[0]
    tr = _pick(rows, (max(8, (1 << 18) // cols // 8 * 8),)) if rows % 8 == 0 else rows
    c1 = 1.0 - ADAM_B1 ** ADAM_STEP
    c2 = 1.0 - ADAM_B2 ** ADAM_STEP

    def body(w_ref, g_ref, m_ref, v_ref, d_ref, nm_ref, nv_ref):
        gv = g_ref[...]
        nm = ADAM_B1 * m_ref[...] + (1.0 - ADAM_B1) * gv
        nv = ADAM_B2 * v_ref[...] + (1.0 - ADAM_B2) * (gv * gv)
        d_ref[...] = -ADAM_LR * ((nm / c1) / (jnp.sqrt(nv / c2) + ADAM_EPS) + ADAM_WD * w_ref[...])
        nm_ref[...] = nm
        nv_ref[...] = nv

    blk = pl.BlockSpec((tr, cols), lambda i: (i, 0))
    outs = pl.pallas_call(
        body, name=name, grid=(rows // tr,), in_specs=[blk] * 4, out_specs=(blk,) * 3,
        out_shape=(_sds((rows, cols), F32),) * 3, compiler_params=_cp(("parallel",)),
    )(*flat)
    return tuple(o.reshape(shape) for o in outs)


PACK_ELEMS = LANE * LANE


def _pack(arrs):
    flat = jnp.concatenate([a.reshape(-1).astype(F32) for a in arrs])
    return jnp.pad(flat, (0, (-flat.shape[0]) % PACK_ELEMS)).reshape(-1, LANE)


def _unpack(packed, shapes):
    flat = packed.reshape(-1)
    out, pos = [], 0
    for s in shapes:
        n = math.prod(s)
        out.append(flat[pos:pos + n].reshape(s))
        pos += n
    return out


def _chip_cols(a, chip, width):
    return lax.dynamic_slice_in_dim(a, chip * width, width, axis=a.ndim - 1)


def kernel(x, c, ctx, c_ctx, w_mod, b_mod, norm_w, w_ffn_in, w_ffn_out, e_w_in, e_conv_w, e_conv_b, e_dt_bias, e_a_log, e_d_skip, e_ssd_norm_w, e_sgu_w, e_sgu_b, e_w_out, o_w_qkv, o_sink, o_w_out, loss_target, m_c_ctx, m_w_mod, m_b_mod, m_norm_w, m_w_ffn_in, m_w_ffn_out, m_e_w_in, m_e_conv_w, m_e_conv_b, m_e_dt_bias, m_e_a_log, m_e_d_skip, m_e_ssd_norm_w, m_e_sgu_w, m_e_sgu_b, m_e_w_out, m_o_w_qkv, m_o_sink, m_o_w_out, v_c_ctx, v_w_mod, v_b_mod, v_norm_w, v_w_ffn_in, v_w_ffn_out, v_e_w_in, v_e_conv_w, v_e_conv_b, v_e_dt_bias, v_e_a_log, v_e_d_skip, v_e_ssd_norm_w, v_e_sgu_w, v_e_sgu_b, v_e_w_out, v_o_w_qkv, v_o_sink, v_o_w_out):
    xi, yi, ci = _place()
    chip = 2 * xi + yi
    me = 2 * chip + ci
    s, d = x.shape[1:]
    tc = ctx.shape[1]
    depth = w_mod.shape[0]
    n_even = e_w_in.shape[0]
    dq = norm_w.shape[-1]
    cq = e_conv_w.shape[-1]
    ns = w_mod.shape[-1]

    gath = allgather8(_pack([c, norm_w, e_conv_w]), name="ag_small").reshape(N_DEV, -1)
    c_all = gath[:, :d]
    per_chip = [_unpack(gath[2 * k, d:], [norm_w.shape, e_conv_w.shape]) for k in range(N_CHIP)]
    nw_full = jnp.concatenate([pc[0] for pc in per_chip], axis=-1)
    convw_full = jnp.concatenate([pc[1] for pc in per_chip], axis=-1)
    c16 = jnp.concatenate([c_all, jnp.broadcast_to(c_ctx[None], (8, d))], axis=0)

    mod_g = allgather8(mod_fwd(c16, w_mod, name="mod_fwd"), name="ag_mod")
    mod_all = jnp.concatenate([mod_g[2 * k] for k in range(N_CHIP)], axis=-1) + b_mod[:, None, :]
    mod_rows = jnp.stack([mod_all[:, 8], lax.dynamic_index_in_dim(mod_all, me, axis=1, keepdims=False)], axis=1)
    modtab = jnp.pad(mod_rows.reshape(depth, 2, 6, d), ((0, 0), (0, 0), (0, 2), (0, 0)))

    wg_ffn_in = gather_weight(w_ffn_in, "w_ffn_in")
    wg_ffn_out = gather_weight(w_ffn_out, "w_ffn_out")
    wg_e_out = gather_weight(e_w_out, "e_w_out")
    wg_qkv = gather_weight(o_w_qkv, "o_w_qkv")
    wg_o_out = gather_weight(o_w_out, "o_w_out")
    e_in_g = gather_weight(e_w_in, "e_w_in")
    wg_e_in = even_cols_permute(jnp.moveaxis(e_in_g, 0, 2).reshape(n_even, d, -1))[None]

    eps_ = [even_params(convw_full[i], e_conv_b[i], e_dt_bias[i], e_a_log[i], e_d_skip[i], e_ssd_norm_w[i], e_sgu_w[i], e_sgu_b[i])
            for i in range(n_even)]
    sinks = [sink_rows(o_sink[i]) for i in range(o_sink.shape[0])]
    cos, sin = rope_tables(tc, s)

    u = jnp.concatenate([ctx[0], x[0]], axis=0)
    saved = []
    for l in range(depth):
        mt, nw = modtab[l], nw_full[l]
        h1 = norm_mod_fwd(u, nw[0], mt, tc, 0, name=f"L{l}_norm1")
        if l % 2 == 0:
            o, ms = even_mixer_fwd(h1, wg_e_in, wg_e_out, l // 2, eps_[l // 2], tc, f"L{l}_mix")
        else:
            o, ms = odd_mixer_fwd(h1, wg_qkv, wg_o_out, l // 2, sinks[l // 2], cos, sin, tc, f"L{l}_mix")
        u1 = resid_fwd(u, o, nw[1], mt, tc, 0, name=f"L{l}_res1")
        h2 = norm_mod_fwd(u1, nw[2], mt, tc, 1, name=f"L{l}_norm2")
        p = mm_nn(h2, wg_ffn_in, l, "n", name=f"L{l}_ffn_in")
        a = swiglu_fwd(p, name=f"L{l}_swiglu")
        f = mm_nn(a, wg_ffn_out, l, "k", name=f"L{l}_ffn_out")
        saved.append((u, h1, ms, o, u1, h2, p, a, f))
        u = resid_fwd(u1, f, nw[3], mt, tc, 1, name=f"L{l}_res2")
    loss_part, du = loss_fwd_bwd(u, loss_target[0], tc, name="loss")
    loss = lax.psum(loss_part[0, 0], ("x", "y", "c"))

    g_ffn_in = jnp.zeros(wg_ffn_in.shape, BF16)
    g_ffn_out = jnp.zeros(wg_ffn_out.shape, BF16)
    g_e_in = jnp.zeros(wg_e_in.shape, BF16)
    g_e_out = jnp.zeros(wg_e_out.shape, BF16)
    g_qkv = jnp.zeros(wg_qkv.shape, BF16)
    g_o_out = jnp.zeros(wg_o_out.shape, BF16)
    d_nw, d_mt = [None] * depth, [None] * depth
    small_e, small_o = [None] * n_even, [None] * (depth - n_even)
    for l in reversed(range(depth)):
        mt, nw = modtab[l], nw_full[l]
        u0, h1, ms, o, u1, h2, p, a, f = saved[l]
        df, acc3 = resid_bwd(f, nw[3], mt, du, tc, 1, name=f"L{l}_res2_b")
        g_ffn_out = mm_tn(a, df, g_ffn_out, l, "k", name=f"L{l}_ffn_out_dw")
        da = mm_nt(df, wg_ffn_out, l, "k", name=f"L{l}_ffn_out_dx")
        dp = swiglu_bwd(p, da, name=f"L{l}_swiglu_b")
        dh2 = mm_nt(dp, wg_ffn_in, l, "n", name=f"L{l}_ffn_in_dx")
        g_ffn_in = mm_tn(h2, dp, g_ffn_in, l, "n", name=f"L{l}_ffn_in_dw")
        du1, acc2 = norm_mod_bwd(u1, nw[2], mt, dh2, du, tc, 1, name=f"L{l}_norm2_b")
        do, acc1 = resid_bwd(o, nw[1], mt, du1, tc, 0, name=f"L{l}_res1_b")
        if l % 2 == 0:
            dh1, g_e_in, g_e_out, small_e[l // 2] = even_mixer_bwd(ms, do, h1, wg_e_in, wg_e_out, g_e_in, g_e_out, l // 2,
                                                                  eps_[l // 2], tc, f"L{l}_mix")
        else:
            dh1, g_qkv, g_o_out, small_o[l // 2] = odd_mixer_bwd(ms, do, h1, wg_qkv, wg_o_out, g_qkv, g_o_out, l // 2,
                                                                 sinks[l // 2], cos, sin, tc, f"L{l}_mix")
        du, acc0 = norm_mod_bwd(u0, nw[0], mt, dh1, du1, tc, 0, name=f"L{l}_norm1_b")
        d_nw[l] = jnp.stack([acc[0, 0] + acc[1, 0] for acc in (acc0, acc1, acc2, acc3)])
        d_mt[l] = jnp.stack([acc0[:, 1], acc0[:, 2], acc1[:, 1], acc2[:, 1], acc2[:, 2], acc3[:, 1]], axis=1)
    grad_x = du[tc:][None]

    dmt_g = allgather8(jnp.pad(jnp.stack(d_mt), ((0, 0), (0, 0), (0, 2), (0, 0))), name="ag_dmod")[:, :, :, :6]
    dm16 = jnp.concatenate([dmt_g[:, :, 1].transpose(1, 0, 2, 3).reshape(depth, N_DEV, 6 * d),
                            dmt_g[:, :, 0].transpose(1, 0, 2, 3).reshape(depth, N_DEV, 6 * d)], axis=1)
    dm_sh = _chip_cols(dm16, chip, ns)
    grad_w_mod = mod_bwd_w(c16, dm_sh, name="mod_bwd_w")
    grad_b_mod = colsum16(dm16, name="mod_bwd_b")[:, 0]
    ds_cc = mod_bwd_s(dm_sh, w_mod, name="mod_bwd_s")[0]

    stack_e = lambda key: jnp.stack([se[key] for se in small_e])
    small_names = ["e_conv_b", "e_dt_bias", "e_a_log", "e_d_skip", "e_ssd_norm_w", "e_sgu_w", "e_sgu_b"]
    small_parts = [jnp.stack(d_nw), stack_e("e_conv_w")] + [stack_e(k) for k in small_names]
    small_parts += [jnp.stack([so["o_sink"] for so in small_o]), 0.5 * ds_cc]
    small_shapes = [a.shape for a in small_parts]
    small_sum = sum_slots(allgather8(_pack(small_parts), name="ag_small_grads"), F32, name="small_grads_sum")
    (g_nw, g_convw, g_convb, g_dtb, g_alog, g_dskip, g_ssdnw, g_sguw, g_sgub, g_sink, g_scc) = _unpack(small_sum, small_shapes)
    grad_c_ctx = silu_grad_mul(jnp.broadcast_to(g_scc[None], (8, d)), jnp.broadcast_to(c_ctx[None], (8, d)), name="c_ctx_grad")[0]
    grads = dict(
        c_ctx=grad_c_ctx, w_mod=grad_w_mod, b_mod=grad_b_mod, norm_w=_chip_cols(g_nw, chip, dq),
        e_conv_w=_chip_cols(g_convw, chip, cq), e_conv_b=g_convb, e_dt_bias=g_dtb.reshape(e_dt_bias.shape),
        e_a_log=g_alog.reshape(e_a_log.shape), e_d_skip=g_dskip, e_ssd_norm_w=g_ssdnw, e_sgu_w=g_sguw, e_sgu_b=g_sgub,
        o_sink=g_sink)

    grads["w_ffn_in"] = reduce_scatter_grad(g_ffn_in, "g_ffn_in")
    grads["w_ffn_out"] = reduce_scatter_grad(g_ffn_out, "g_ffn_out")
    g_e_in_c = jnp.moveaxis(even_cols_unpermute(g_e_in[0]).reshape(n_even, d, N_CHIP, -1), 2, 0)
    grads["e_w_in"] = reduce_scatter_grad(g_e_in_c, "g_e_in")
    grads["e_w_out"] = reduce_scatter_grad(g_e_out, "g_e_out")
    grads["o_w_qkv"] = reduce_scatter_grad(g_qkv, "g_qkv")
    grads["o_w_out"] = reduce_scatter_grad(g_o_out, "g_o_out")

    weights = dict(c_ctx=c_ctx, w_mod=w_mod, b_mod=b_mod, norm_w=norm_w, w_ffn_in=w_ffn_in, w_ffn_out=w_ffn_out, e_w_in=e_w_in,
                   e_conv_w=e_conv_w, e_conv_b=e_conv_b, e_dt_bias=e_dt_bias, e_a_log=e_a_log, e_d_skip=e_d_skip,
                   e_ssd_norm_w=e_ssd_norm_w, e_sgu_w=e_sgu_w, e_sgu_b=e_sgu_b, e_w_out=e_w_out, o_w_qkv=o_w_qkv, o_sink=o_sink,
                   o_w_out=o_w_out)
    ms_ = dict(c_ctx=m_c_ctx, w_mod=m_w_mod, b_mod=m_b_mod, norm_w=m_norm_w, w_ffn_in=m_w_ffn_in, w_ffn_out=m_w_ffn_out,
               e_w_in=m_e_w_in, e_conv_w=m_e_conv_w, e_conv_b=m_e_conv_b, e_dt_bias=m_e_dt_bias, e_a_log=m_e_a_log,
               e_d_skip=m_e_d_skip, e_ssd_norm_w=m_e_ssd_norm_w, e_sgu_w=m_e_sgu_w, e_sgu_b=m_e_sgu_b, e_w_out=m_e_w_out,
               o_w_qkv=m_o_w_qkv, o_sink=m_o_sink, o_w_out=m_o_w_out)
    vs_ = dict(c_ctx=v_c_ctx, w_mod=v_w_mod, b_mod=v_b_mod, norm_w=v_norm_w, w_ffn_in=v_w_ffn_in, w_ffn_out=v_w_ffn_out,
               e_w_in=v_e_w_in, e_conv_w=v_e_conv_w, e_conv_b=v_e_conv_b, e_dt_bias=v_e_dt_bias, e_a_log=v_e_a_log,
               e_d_skip=v_e_d_skip, e_ssd_norm_w=v_e_ssd_norm_w, e_sgu_w=v_e_sgu_w, e_sgu_b=v_e_sgu_b, e_w_out=v_e_w_out,
               o_w_qkv=v_o_w_qkv, o_sink=v_o_sink, o_w_out=v_o_w_out)
    names = list(weights)
    big = ("w_mod", "w_ffn_in", "w_ffn_out", "e_w_in", "e_w_out", "o_w_qkv", "o_w_out")
    small = [n for n in names if n not in big]
    delta, new_m, new_v = {}, {}, {}
    for n in big:
        delta[n], new_m[n], new_v[n] = adamw(weights[n], grads[n], ms_[n], vs_[n], name=f"adamw_{n}")
    packed = adamw(*[_pack([tab[n] for n in small]) for tab in (weights, grads, ms_, vs_)], name="adamw_small")
    shapes = [weights[n].shape for n in small]
    for tab, pk in zip((delta, new_m, new_v), packed):
        for n, val in zip(small, _unpack(pk, shapes)):
            tab[n] = val
    return (loss, grad_x, *[grads[n] for n in names], *[delta[n] for n in names], *[new_m[n] for n in names],
            *[new_v[n] for n in names])
```

```python
import functools
import itertools
import math

import jax
import jax.numpy as jnp
from jax import lax
from jax.experimental import pallas as pl
from jax.experimental.pallas import tpu as pltpu

F32 = jnp.float32
BF16 = jnp.bfloat16
HI = lax.Precision.HIGHEST

NORM_EPS = 1e-6
SSD_HEAD_DIM = 64
SSD_STATE = 128
CHUNK = 128
CONV_K = 5
ATT_HEAD_DIM = 128
ATT_GROUP = 4
ROPE_BASE = 10000.0
GRID_W = 64
NEG_INF = -1e30
ADAM_LR, ADAM_B1, ADAM_B2, ADAM_EPS, ADAM_WD, ADAM_STEP = 0.001, 0.9, 0.999, 1e-08, 0.01, 10

LANE = 128
VMEM_LIMIT = 56 * 1024 * 1024
MESH_ID = pl.DeviceIdType.MESH
N_DEV = 8
N_CHIP = 4


def _cp(sem=None):
    return pltpu.CompilerParams(dimension_semantics=sem, vmem_limit_bytes=VMEM_LIMIT)


def _sds(shape, dtype):
    return jax.ShapeDtypeStruct(tuple(shape), dtype)


def _pick(n, cands):
    for c in cands:
        if n % c == 0:
            return c
    for step in (LANE, 16):
        for c in range(min(n, cands[0]) // step * step, 0, -step):
            if n % c == 0:
                return c
    raise ValueError((n, cands))


def _w_index(blocked, layer, per_block_k, per_block_n):
    def idx(kblk, nblk):
        if blocked == "n":
            return (nblk // per_block_n, layer, kblk, nblk % per_block_n)
        return (kblk // per_block_k, layer, kblk % per_block_k, nblk)
    return idx


def mm_nn(a, w, layer, blocked, *, name, out_dtype=F32, tm=None, tn=None, tk=None):
    m, k_total = a.shape
    cb, _, kd, nd = w.shape
    n_total = nd * cb if blocked == "n" else nd
    assert k_total == (kd if blocked == "n" else kd * cb)
    tm = tm or _pick(m, (1088, 192))
    tn = tn or _pick(nd, (1408, 768, 512))
    tk = tk or _pick(kd, (2048, 1408, 512))
    nk = k_total // tk
    widx = _w_index(blocked, layer, kd // tk, nd // tn)

    def body(a_ref, w_ref, o_ref, acc_ref):
        kk = pl.program_id(2)
        part = jnp.dot(a_ref[...].astype(BF16), w_ref[...].astype(BF16), preferred_element_type=F32)

        @pl.when(kk == 0)
        def _():
            acc_ref[...] = part

        @pl.when(kk > 0)
        def _():
            acc_ref[...] += part

        @pl.when(kk == nk - 1)
        def _():
            o_ref[...] = acc_ref[...].astype(o_ref.dtype)

    return pl.pallas_call(
        body, name=name, grid=(m // tm, n_total // tn, nk),
        in_specs=[pl.BlockSpec((tm, tk), lambda i, j, k: (i, k)),
                  pl.BlockSpec((None, None, tk, tn), lambda i, j, k: widx(k, j))],
        out_specs=pl.BlockSpec((tm, tn), lambda i, j, k: (i, j)),
        out_shape=_sds((m, n_total), out_dtype),
        scratch_shapes=[pltpu.VMEM((tm, tn), F32)],
        compiler_params=_cp(("parallel", "parallel", "arbitrary")),
    )(a, w)


def mm_nt(dy, w, layer, blocked, *, name, out_dtype=F32, tm=None, tn=None, tk=None):
    m, n_total = dy.shape
    cb, _, kd, nd = w.shape
    k_total = kd if blocked == "n" else kd * cb
    assert n_total == (nd * cb if blocked == "n" else nd)
    tm = tm or _pick(m, (1088, 192))
    tn = tn or _pick(kd, (1024, 1408, 512))
    tk = tk or _pick(nd, (1408, 1024, 768))
    nk = n_total // tk
    widx = _w_index(blocked, layer, kd // tn, nd // tk)

    def body(a_ref, w_ref, o_ref, acc_ref):
        kk = pl.program_id(2)
        part = lax.dot_general(a_ref[...].astype(BF16), w_ref[...].astype(BF16), (((1,), (1,)), ((), ())),
                               preferred_element_type=F32)

        @pl.when(kk == 0)
        def _():
            acc_ref[...] = part

        @pl.when(kk > 0)
        def _():
            acc_ref[...] += part

        @pl.when(kk == nk - 1)
        def _():
            o_ref[...] = acc_ref[...].astype(o_ref.dtype)

    return pl.pallas_call(
        body, name=name, grid=(m // tm, k_total // tn, nk),
        in_specs=[pl.BlockSpec((tm, tk), lambda i, j, k: (i, k)),
                  pl.BlockSpec((None, None, tn, tk), lambda i, j, k: widx(j, k))],
        out_specs=pl.BlockSpec((tm, tn), lambda i, j, k: (i, j)),
        out_shape=_sds((m, k_total), out_dtype),
        scratch_shapes=[pltpu.VMEM((tm, tn), F32)],
        compiler_params=_cp(("parallel", "parallel", "arbitrary")),
    )(dy, w)


def mm_tn(x, dy, g, layer, blocked, *, name, tm=None, tn=None, tt=None):
    t_total, k_total = x.shape
    n_total = dy.shape[1]
    cb, _, kd, nd = g.shape
    assert k_total == (kd if blocked == "n" else kd * cb) and n_total == (nd * cb if blocked == "n" else nd)
    tm = tm or _pick(kd, (1024, 1408, 512))
    tn = tn or _pick(nd, (1408, 768, 512))
    tt = tt or _pick(t_total, (1088, 96))
    nt = t_total // tt
    widx = _w_index(blocked, layer, kd // tm, nd // tn)

    def body(x_ref, dy_ref, g_in, o_ref, acc_ref):
        del g_in
        tstep = pl.program_id(2)
        part = lax.dot_general(x_ref[...].astype(BF16), dy_ref[...].astype(BF16), (((0,), (0,)), ((), ())),
                               preferred_element_type=F32)

        @pl.when(tstep == 0)
        def _():
            acc_ref[...] = part

        @pl.when(tstep > 0)
        def _():
            acc_ref[...] += part

        @pl.when(tstep == nt - 1)
        def _():
            o_ref[...] = acc_ref[...].astype(o_ref.dtype)

    return pl.pallas_call(
        body, name=name, grid=(k_total // tm, n_total // tn, nt),
        in_specs=[pl.BlockSpec((tt, tm), lambda i, j, t: (t, i)),
                  pl.BlockSpec((tt, tn), lambda i, j, t: (t, j)),
                  pl.BlockSpec(memory_space=pl.ANY)],
        out_specs=pl.BlockSpec((None, None, tm, tn), lambda i, j, t: widx(i, j)),
        out_shape=_sds(g.shape, g.dtype),
        scratch_shapes=[pltpu.VMEM((tm, tn), F32)],
        input_output_aliases={2: 0},
        compiler_params=_cp(("parallel", "parallel", "arbitrary")),
    )(x, dy, g)


def _rms(x, w):
    return x * lax.rsqrt(jnp.mean(x * x, axis=-1, keepdims=True) + NORM_EPS) * w


def _row_tile(tc):
    return 256 if tc % 256 == 0 else 128


def _seg_spec(nct, d):
    return pl.BlockSpec((None, 8, d), lambda i: (jnp.minimum(i // nct, 1), 0, 0))


def _acc_rows(acc_ref, i, nct, rows):
    @pl.when((i == 0) | (i == nct))
    def _():
        acc_ref[...] = jnp.zeros_like(acc_ref)

    for r, val in enumerate(rows):
        acc_ref[r:r + 1, :] += val


def norm_mod_fwd(u, nw, modtab, tc, which, *, name):
    t, d = u.shape
    tr = _row_tile(tc)
    nct = tc // tr
    r0 = 3 * which

    def body(u_ref, nw_ref, mt_ref, h_ref):
        sh, sc = mt_ref[r0:r0 + 1, :], mt_ref[r0 + 1:r0 + 2, :]
        h_ref[...] = (_rms(u_ref[...], nw_ref[...]) * (1.0 + sc) + sh).astype(h_ref.dtype)

    return pl.pallas_call(
        body, name=name, grid=(t // tr,),
        in_specs=[pl.BlockSpec((tr, d), lambda i: (i, 0)), pl.BlockSpec((1, d), lambda i: (0, 0)), _seg_spec(nct, d)],
        out_specs=pl.BlockSpec((tr, d), lambda i: (i, 0)),
        out_shape=_sds((t, d), BF16), compiler_params=_cp(("arbitrary",)),
    )(u, nw.reshape(1, d), modtab)


def norm_mod_bwd(u, nw, modtab, dh, du_in, tc, which, *, name):
    t, d = u.shape
    tr = _row_tile(tc)
    nct = tc // tr
    r0 = 3 * which

    def body(u_ref, nw_ref, mt_ref, dh_ref, dui_ref, du_ref, acc_ref):
        i = pl.program_id(0)
        sh, sc = mt_ref[r0:r0 + 1, :], mt_ref[r0 + 1:r0 + 2, :]
        _, vjp = jax.vjp(lambda x, w, a, b: _rms(x, w) * (1.0 + b) + a, u_ref[...], nw_ref[...], sh, sc)
        dx, dw, dsh, dsc = vjp(dh_ref[...].astype(F32))
        du_ref[...] = dui_ref[...] + dx
        _acc_rows(acc_ref, i, nct, (dw, dsh, dsc))

    row = pl.BlockSpec((tr, d), lambda i: (i, 0))
    return pl.pallas_call(
        body, name=name, grid=(t // tr,),
        in_specs=[row, pl.BlockSpec((1, d), lambda i: (0, 0)), _seg_spec(nct, d), row, row],
        out_specs=(row, _seg_spec(nct, d)),
        out_shape=(_sds((t, d), F32), _sds((2, 8, d), F32)), compiler_params=_cp(("arbitrary",)),
    )(u, nw.reshape(1, d), modtab, dh, du_in)


def resid_fwd(u, o, nw, modtab, tc, which, *, name):
    t, d = u.shape
    tr = _row_tile(tc)
    nct = tc // tr
    r0 = 3 * which + 2

    def body(u_ref, o_ref, nw_ref, mt_ref, out_ref):
        out_ref[...] = u_ref[...] + mt_ref[r0:r0 + 1, :] * _rms(o_ref[...], nw_ref[...])

    row = pl.BlockSpec((tr, d), lambda i: (i, 0))
    return pl.pallas_call(
        body, name=name, grid=(t // tr,),
        in_specs=[row, row, pl.BlockSpec((1, d), lambda i: (0, 0)), _seg_spec(nct, d)],
        out_specs=row, out_shape=_sds((t, d), F32), compiler_params=_cp(("arbitrary",)),
    )(u, o, nw.reshape(1, d), modtab)


def resid_bwd(o, nw, modtab, du, tc, which, *, name):
    t, d = o.shape
    tr = _row_tile(tc)
    nct = tc // tr
    r0 = 3 * which + 2

    def body(o_ref, nw_ref, mt_ref, du_ref, do_ref, acc_ref):
        i = pl.program_id(0)
        _, vjp = jax.vjp(lambda x, w, g: g * _rms(x, w), o_ref[...], nw_ref[...], mt_ref[r0:r0 + 1, :])
        dx, dw, dg = vjp(du_ref[...])
        do_ref[...] = dx.astype(do_ref.dtype)
        _acc_rows(acc_ref, i, nct, (dw, dg))

    row = pl.BlockSpec((tr, d), lambda i: (i, 0))
    return pl.pallas_call(
        body, name=name, grid=(t // tr,),
        in_specs=[row, pl.BlockSpec((1, d), lambda i: (0, 0)), _seg_spec(nct, d), row],
        out_specs=(row, _seg_spec(nct, d)),
        out_shape=(_sds((t, d), BF16), _sds((2, 8, d), F32)), compiler_params=_cp(("arbitrary",)),
    )(o, nw.reshape(1, d), modtab, du)


def swiglu_fwd(p, *, name):
    t, h2 = p.shape
    h = h2 // 2
    tr = _pick(t, (1088, 192))
    tc = _pick(h, (512, 256))
    nh = h // tc

    def body(g_ref, u_ref, a_ref):
        a_ref[...] = (jax.nn.silu(g_ref[...]) * u_ref[...]).astype(a_ref.dtype)

    return pl.pallas_call(
        body, name=name, grid=(t // tr, nh),
        in_specs=[pl.BlockSpec((tr, tc), lambda i, j: (i, j)), pl.BlockSpec((tr, tc), lambda i, j: (i, j + nh))],
        out_specs=pl.BlockSpec((tr, tc), lambda i, j: (i, j)),
        out_shape=_sds((t, h), BF16), compiler_params=_cp(("parallel", "parallel")),
    )(p, p)


def swiglu_bwd(p, da, *, name):
    t, h2 = p.shape
    h = h2 // 2
    tr = _pick(t, (1088, 192))
    tc = _pick(h, (512, 256))
    nh = h // tc

    def body(g_ref, u_ref, da_ref, dp_ref):
        j = pl.program_id(1)
        _, vjp = jax.vjp(lambda g, u: jax.nn.silu(g) * u, g_ref[...], u_ref[...])
        dg, du = vjp(da_ref[...])

        @pl.when(j < nh)
        def _():
            dp_ref[...] = dg.astype(dp_ref.dtype)

        @pl.when(j >= nh)
        def _():
            dp_ref[...] = du.astype(dp_ref.dtype)

    return pl.pallas_call(
        body, name=name, grid=(t // tr, 2 * nh),
        in_specs=[pl.BlockSpec((tr, tc), lambda i, j: (i, j % nh)), pl.BlockSpec((tr, tc), lambda i, j: (i, j % nh + nh)),
                  pl.BlockSpec((tr, tc), lambda i, j: (i, j % nh))],
        out_specs=pl.BlockSpec((tr, tc), lambda i, j: (i, j)),
        out_shape=_sds((t, h2), BF16), compiler_params=_cp(("parallel", "parallel")),
    )(p, p, da)


def loss_fwd_bwd(u, target, tc, *, name):
    t, d = u.shape
    tr = _row_tile(tc)
    nct = tc // tr

    def body(u_ref, t_ref, loss_ref, du_ref):
        i = pl.program_id(0)

        @pl.when(i == 0)
        def _():
            loss_ref[...] = jnp.zeros_like(loss_ref)

        @pl.when(i < nct)
        def _():
            du_ref[...] = jnp.zeros_like(du_ref)

        @pl.when(i >= nct)
        def _():
            err = u_ref[...] - t_ref[...]
            du_ref[...] = err * (1.0 / d)
            loss_ref[...] += jnp.sum(jnp.sum(err * err, axis=1, keepdims=True), axis=0, keepdims=True) * (0.5 / d)

    return pl.pallas_call(
        body, name=name, grid=(t // tr,),
        in_specs=[pl.BlockSpec((tr, d), lambda i: (i, 0)), pl.BlockSpec((tr, d), lambda i: (jnp.maximum(i - nct, 0), 0))],
        out_specs=(pl.BlockSpec((1, 1), lambda i: (0, 0)), pl.BlockSpec((tr, d), lambda i: (i, 0))),
        out_shape=(_sds((1, 1), F32), _sds((t, d), F32)), compiler_params=_cp(("arbitrary",)),
    )(u, target)


SSD_INNER = 1024
SGU_WIDTH = 1024
XBC_DIM = 1536
EVEN_COLS = 4640
EVEN_PAD_COLS = 5120
Z_BLK, U_BLK, V_BLK, X_BLK, B_BLK, C_BLK, DT_BLK = 0, 8, 16, 24, 32, 34, 36
PAD_ROWS = 8


def _conv_scratch_fill(pad_ref, val, tc, s):
    pad_ref[...] = jnp.zeros_like(pad_ref)
    pad_ref[PAD_ROWS:PAD_ROWS + tc, :] = val[:tc]
    pad_ref[2 * PAD_ROWS + tc:2 * PAD_ROWS + tc + s, :] = val[tc:]


def _conv_taps(pad_ref, tc, s, k):
    off = k - CONV_K // 2
    return (pad_ref[PAD_ROWS + off:PAD_ROWS + off + tc, :],
            pad_ref[2 * PAD_ROWS + tc + off:2 * PAD_ROWS + tc + off + s, :])


def conv_fwd(p, wb, tc, *, name):
    t = p.shape[0]
    s = t - tc
    nblk = XBC_DIM // LANE

    def body(p_ref, wb_ref, out_ref, pad_ref):
        _conv_scratch_fill(pad_ref, p_ref[...], tc, s)
        acc_c = jnp.zeros((tc, LANE), F32) + wb_ref[5:6, :]
        acc_l = jnp.zeros((s, LANE), F32) + wb_ref[5:6, :]
        for k in range(CONV_K):
            xc, xl = _conv_taps(pad_ref, tc, s, k)
            acc_c += xc * wb_ref[k:k + 1, :]
            acc_l += xl * wb_ref[k:k + 1, :]
        out_ref[:tc, :] = jax.nn.silu(acc_c)
        out_ref[tc:, :] = jax.nn.silu(acc_l)

    return pl.pallas_call(
        body, name=name, grid=(nblk,),
        in_specs=[pl.BlockSpec((t, LANE), lambda j: (0, X_BLK + j)), pl.BlockSpec((8, LANE), lambda j: (0, j))],
        out_specs=pl.BlockSpec((t, LANE), lambda j: (0, j)),
        out_shape=_sds((t, XBC_DIM), F32),
        scratch_shapes=[pltpu.VMEM((t + 3 * PAD_ROWS, LANE), F32)],
        compiler_params=_cp(("parallel",)),
    )(p, wb)


def conv_bwd(p, wb, dxg, dskip, tc, *, name):
    t = p.shape[0]
    s = t - tc
    nblk = XBC_DIM // LANE
    nx = SSD_INNER // LANE

    def grp(j):
        return jnp.where(j < nx, j // 4, (j - nx) % 2)

    def sub(j):
        return jnp.where(j < nx, j % 4, 4 + (j - nx) // 2)

    def body(p_ref, wb_ref, d0_ref, d1_ref, ds_ref, dp_ref, dwb_ref, pad_ref, dpad_ref):
        j = pl.program_id(0)
        _conv_scratch_fill(pad_ref, p_ref[...], tc, s)
        pre = [jnp.zeros((tc, LANE), F32) + wb_ref[5:6, :], jnp.zeros((s, LANE), F32) + wb_ref[5:6, :]]
        for k in range(CONV_K):
            xc, xl = _conv_taps(pad_ref, tc, s, k)
            pre[0] += xc * wb_ref[k:k + 1, :]
            pre[1] += xl * wb_ref[k:k + 1, :]
        dx = d0_ref[...] + d1_ref[...] + jnp.where(j < nx, ds_ref[...], 0.0)
        dpre = []
        for part, rows in ((0, slice(0, tc)), (1, slice(tc, t))):
            sig = jax.nn.sigmoid(pre[part])
            dpre.append(dx[rows] * (sig * (1.0 + pre[part] * (1.0 - sig))))
        dwb_ref[...] = jnp.zeros_like(dwb_ref)
        dwb_ref[5:6, :] = jnp.sum(dpre[0], axis=0, keepdims=True) + jnp.sum(dpre[1], axis=0, keepdims=True)
        for k in range(CONV_K):
            xc, xl = _conv_taps(pad_ref, tc, s, k)
            dwb_ref[k:k + 1, :] = (jnp.sum(dpre[0] * xc, axis=0, keepdims=True)
                                   + jnp.sum(dpre[1] * xl, axis=0, keepdims=True))
        _conv_scratch_fill(dpad_ref, jnp.concatenate(dpre, axis=0), tc, s)
        acc_c = jnp.zeros((tc, LANE), F32)
        acc_l = jnp.zeros((s, LANE), F32)
        for k in range(CONV_K):
            gc, gl = _conv_taps(dpad_ref, tc, s, CONV_K - 1 - k)
            acc_c += gc * wb_ref[k:k + 1, :]
            acc_l += gl * wb_ref[k:k + 1, :]
        dp_ref[:tc, :] = acc_c.astype(dp_ref.dtype)
        dp_ref[tc:, :] = acc_l.astype(dp_ref.dtype)

    col = pl.BlockSpec((t, LANE), lambda j: (0, j))
    return pl.pallas_call(
        body, name=name, grid=(nblk,),
        in_specs=[pl.BlockSpec((t, LANE), lambda j: (0, X_BLK + j)), pl.BlockSpec((8, LANE), lambda j: (0, j)),
                  pl.BlockSpec((None, None, t, LANE), lambda j: (0, grp(j), 0, sub(j))),
                  pl.BlockSpec((None, None, t, LANE), lambda j: (1, grp(j), 0, sub(j))),
                  pl.BlockSpec((t, LANE), lambda j: (0, jnp.minimum(j, nx - 1)))],
        out_specs=(col, pl.BlockSpec((8, LANE), lambda j: (0, j))),
        out_shape=(_sds((t, XBC_DIM), BF16), _sds((8, XBC_DIM), F32)),
        scratch_shapes=[pltpu.VMEM((t + 3 * PAD_ROWS, LANE), F32), pltpu.VMEM((t + 3 * PAD_ROWS, LANE), F32)],
        compiler_params=_cp(("parallel",)),
    )(p, wb, dxg, dxg, dskip)


HEADS_PER_DG = 8


def _ssd_prep_fn(pre, bias, alog, rev):
    q = pre.shape[0]
    lane = lax.broadcasted_iota(jnp.int32, (1, LANE), 1)
    dt = jnp.where(lane < HEADS_PER_DG, jax.nn.softplus(pre + bias), 0.0)
    row = lax.broadcasted_iota(jnp.int32, (q, q), 0)
    col = lax.broadcasted_iota(jnp.int32, (q, q), 1)
    tri = jnp.where((col - row) * jnp.where(rev, 1, -1) >= 0, 1.0, 0.0)
    cs = jnp.dot(tri, dt * (-jnp.exp(alog)), precision=HI, preferred_element_type=F32)
    return dt, cs


def _scan_chunk(nc_ctx, nch):
    def idx(dg, i):
        fwd = i
        bwd = jnp.where(i < nc_ctx, nc_ctx - 1 - i, nch - 1 - (i - nc_ctx))
        return jnp.where(dg // 2 == 0, fwd, bwd)
    return idx


def ssd_prep_fwd(pre, bias, alog, *, name):
    _, t, _ = pre.shape
    blk = pl.BlockSpec((None, CHUNK, LANE), lambda dg, i: (dg, i, 0))
    par = pl.BlockSpec((None, 1, LANE), lambda dg, i: (dg, 0, 0))

    def body(pre_ref, b_ref, a_ref, dt_ref, cs_ref):
        dt, cs = _ssd_prep_fn(pre_ref[...], b_ref[...], a_ref[...], pl.program_id(0) // 2 == 1)
        dt_ref[...] = dt
        cs_ref[...] = cs

    return pl.pallas_call(
        body, name=name, grid=(4, t // CHUNK), in_specs=[blk, par, par], out_specs=(blk, blk),
        out_shape=(_sds(pre.shape, F32), _sds(pre.shape, F32)), compiler_params=_cp(("parallel", "parallel")),
    )(pre, bias, alog)


def ssd_prep_bwd(pre, bias, alog, ddt, dcs, *, name):
    _, t, _ = pre.shape
    blk = pl.BlockSpec((None, CHUNK, LANE), lambda dg, i: (dg, i, 0))
    par = pl.BlockSpec((None, 1, LANE), lambda dg, i: (dg, 0, 0))

    def body(pre_ref, b_ref, a_ref, ddt_ref, dcs_ref, dpre_ref, acc_ref):
        rev = pl.program_id(0) // 2 == 1
        _, vjp = jax.vjp(lambda x, b, a: _ssd_prep_fn(x, b, a, rev), pre_ref[...], b_ref[...], a_ref[...])
        dpre, db, da = vjp((ddt_ref[...], dcs_ref[...]))
        dpre_ref[...] = dpre

        @pl.when(pl.program_id(1) == 0)
        def _():
            acc_ref[...] = jnp.zeros_like(acc_ref)

        acc_ref[0:1, :] += db
        acc_ref[1:2, :] += da

    return pl.pallas_call(
        body, name=name, grid=(4, t // CHUNK), in_specs=[blk, par, par, blk, blk],
        out_specs=(blk, pl.BlockSpec((None, 8, LANE), lambda dg, i: (dg, 0, 0))),
        out_shape=(_sds(pre.shape, F32), _sds((4, 8, LANE), F32)), compiler_params=_cp(("parallel", "arbitrary")),
    )(pre, bias, alog, ddt, dcs)


def _onehot_col(a, h):
    lane = lax.broadcasted_iota(jnp.int32, (1, a.shape[1]), 1)
    return jnp.sum(jnp.where(lane == h, a, 0.0), axis=1, keepdims=True)


def _onehot_row(a, h):
    sub = lax.broadcasted_iota(jnp.int32, (a.shape[0], 1), 0)
    return jnp.sum(jnp.where(sub == h, a, 0.0), axis=0, keepdims=True)


def _bdot(a, b, dims):
    return lax.dot_general(a.astype(BF16), b.astype(BF16), (dims, ((), ())), preferred_element_type=F32)


def _ssd_pair(xblk, dt, cs, bm, cm, sp, rev, pair):
    q = xblk.shape[0]
    lane = lax.broadcasted_iota(jnp.int32, (1, LANE), 1)
    sub = lax.broadcasted_iota(jnp.int32, (LANE, 1), 0)
    row = lax.broadcasted_iota(jnp.int32, (q, q), 0)
    col = lax.broadcasted_iota(jnp.int32, (q, q), 1)
    mask = (col - row) * jnp.where(rev, 1, -1) >= 0
    last = jnp.where(rev, 0, q - 1)
    cs_t = cs.T
    g = _bdot(cm, bm, ((1,), (1,)))
    y = jnp.zeros((q, LANE), F32)
    escale = jnp.zeros((q, LANE), F32)
    xw = jnp.zeros((q, LANE), F32)
    dec = jnp.zeros((LANE, 1), F32)
    for hh in range(2):
        h = 2 * pair + hh
        c_col = _onehot_col(cs, h)
        c_row = _onehot_row(cs_t, h)
        tot = jnp.sum(jnp.where(lax.broadcasted_iota(jnp.int32, (1, q), 1) == last, c_row, 0.0), axis=1, keepdims=True)
        in_head = (lane >= hh * SSD_HEAD_DIM) & (lane < (hh + 1) * SSD_HEAD_DIM)
        xh = jnp.where(in_head, xblk * _onehot_col(dt, h), 0.0)
        ldec = jnp.where(mask, jnp.exp(jnp.where(mask, c_col - c_row, 0.0)), 0.0)
        y = y + _bdot(g * ldec, xh, ((1,), (0,)))
        escale = escale + jnp.where(in_head, jnp.exp(c_col), 0.0)
        xw = xw + xh * jnp.exp(tot - c_col)
        dec = dec + jnp.where((sub >= hh * SSD_HEAD_DIM) & (sub < (hh + 1) * SSD_HEAD_DIM), jnp.exp(tot), 0.0)
    y = y + _bdot(cm, sp, ((1,), (1,))) * escale
    s_new = sp * dec + _bdot(xw, bm, ((0,), (0,)))
    return y, s_new


def ssd_fwd(xbc, dt, cs, tc, *, name):
    t = xbc.shape[0]
    nch = t // CHUNK
    sidx = _scan_chunk(tc // CHUNK, nch)
    gw = SSD_INNER // 2
    nb = SSD_INNER // LANE

    def body(x_ref, b_ref, c_ref, dt_ref, cs_ref, y_ref, sp_ref, s_ref):
        @pl.when(pl.program_id(1) == 0)
        def _():
            s_ref[...] = jnp.zeros_like(s_ref)

        rev = pl.program_id(0) // 2 == 1
        sp_ref[...] = s_ref[...]
        for p in range(gw // LANE):
            blk = slice(p * LANE, (p + 1) * LANE)
            y, s_new = _ssd_pair(x_ref[:, blk], dt_ref[...], cs_ref[...], b_ref[...], c_ref[...], s_ref[blk, :], rev, p)
            y_ref[:, blk] = y
            s_ref[blk, :] = s_new

    return pl.pallas_call(
        body, name=name, grid=(4, nch),
        in_specs=[pl.BlockSpec((CHUNK, gw), lambda dg, i: (sidx(dg, i), dg % 2)),
                  pl.BlockSpec((CHUNK, LANE), lambda dg, i: (sidx(dg, i), nb + dg % 2)),
                  pl.BlockSpec((CHUNK, LANE), lambda dg, i: (sidx(dg, i), nb + 2 + dg % 2)),
                  pl.BlockSpec((None, CHUNK, LANE), lambda dg, i: (dg, sidx(dg, i), 0)),
                  pl.BlockSpec((None, CHUNK, LANE), lambda dg, i: (dg, sidx(dg, i), 0))],
        out_specs=(pl.BlockSpec((None, CHUNK, gw), lambda dg, i: (dg // 2, sidx(dg, i), dg % 2)),
                   pl.BlockSpec((None, None, gw, SSD_STATE), lambda dg, i: (dg, sidx(dg, i), 0, 0))),
        out_shape=(_sds((2, t, SSD_INNER), F32), _sds((4, nch, gw, SSD_STATE), F32)),
        scratch_shapes=[pltpu.VMEM((gw, SSD_STATE), F32)],
        compiler_params=_cp(("parallel", "arbitrary")),
    )(xbc, xbc, xbc, dt, cs)


def ssd_bwd(xbc, dt, cs, sprev, dy, tc, *, name):
    t = xbc.shape[0]
    nch = t // CHUNK
    fidx = _scan_chunk(tc // CHUNK, nch)
    sidx = lambda dg, i: fidx(dg, nch - 1 - i)
    gw = SSD_INNER // 2
    nb = SSD_INNER // LANE

    def body(x_ref, b_ref, c_ref, dt_ref, cs_ref, sp_ref, dy_ref, dxg_ref, ddt_ref, dcs_ref, ds_ref):
        @pl.when(pl.program_id(1) == 0)
        def _():
            ds_ref[...] = jnp.zeros_like(ds_ref)

        rev = pl.program_id(0) // 2 == 1
        ddt = jnp.zeros((CHUNK, LANE), F32)
        dcs = jnp.zeros((CHUNK, LANE), F32)
        db = jnp.zeros((CHUNK, SSD_STATE), F32)
        dc = jnp.zeros((CHUNK, SSD_STATE), F32)
        for p in range(gw // LANE):
            blk = slice(p * LANE, (p + 1) * LANE)
            _, vjp = jax.vjp(functools.partial(_ssd_pair, rev=rev, pair=p),
                             x_ref[:, blk], dt_ref[...], cs_ref[...], b_ref[...], c_ref[...], sp_ref[blk, :])
            dx, ddt_p, dcs_p, db_p, dc_p, dsp = vjp((dy_ref[:, blk], ds_ref[blk, :]))
            dxg_ref[:, blk] = dx
            ds_ref[blk, :] = dsp
            ddt, dcs, db, dc = ddt + ddt_p, dcs + dcs_p, db + db_p, dc + dc_p
        dxg_ref[:, gw:gw + SSD_STATE] = db
        dxg_ref[:, gw + SSD_STATE:] = dc
        ddt_ref[...] = ddt
        dcs_ref[...] = dcs

    hd = pl.BlockSpec((None, CHUNK, LANE), lambda dg, i: (dg, sidx(dg, i), 0))
    return pl.pallas_call(
        body, name=name, grid=(4, nch),
        in_specs=[pl.BlockSpec((CHUNK, gw), lambda dg, i: (sidx(dg, i), dg % 2)),
                  pl.BlockSpec((CHUNK, LANE), lambda dg, i: (sidx(dg, i), nb + dg % 2)),
                  pl.BlockSpec((CHUNK, LANE), lambda dg, i: (sidx(dg, i), nb + 2 + dg % 2)),
                  hd, hd,
                  pl.BlockSpec((None, None, gw, SSD_STATE), lambda dg, i: (dg, sidx(dg, i), 0, 0)),
                  pl.BlockSpec((CHUNK, gw), lambda dg, i: (sidx(dg, i), dg % 2))],
        out_specs=(pl.BlockSpec((None, None, CHUNK, gw + 2 * SSD_STATE), lambda dg, i: (dg // 2, dg % 2, sidx(dg, i), 0)),
                   hd, hd),
        out_shape=(_sds((2, 2, t, gw + 2 * SSD_STATE), F32), _sds((4, t, LANE), F32), _sds((4, t, LANE), F32)),
        scratch_shapes=[pltpu.VMEM((gw, SSD_STATE), F32)],
        compiler_params=_cp(("parallel", "arbitrary")),
    )(xbc, xbc, xbc, dt, cs, sprev, dy)


def _ssd_finish_fn(y0, y1, xs, z, dskip, nw):
    y = (y0 + y1 + xs * dskip) * jax.nn.silu(z)
    half = y.shape[1] // 2
    first = lax.broadcasted_iota(jnp.int32, (1, y.shape[1]), 1) < half
    sq = y * y
    m0 = jnp.sum(jnp.where(first, sq, 0.0), axis=1, keepdims=True) / half
    m1 = jnp.sum(jnp.where(first, 0.0, sq), axis=1, keepdims=True) / half
    return y * jnp.where(first, lax.rsqrt(m0 + NORM_EPS), lax.rsqrt(m1 + NORM_EPS)) * nw


def ssd_finish_fwd(y, xbc, p, dskip, nw, *, name):
    t = xbc.shape[0]
    tr = CHUNK
    w = SSD_INNER
    row = pl.BlockSpec((tr, w), lambda i: (i, 0))
    par = pl.BlockSpec((1, w), lambda i: (0, 0))

    def body(y0_ref, y1_ref, x_ref, z_ref, ds_ref, nw_ref, o_ref):
        o_ref[...] = _ssd_finish_fn(y0_ref[...], y1_ref[...], x_ref[...], z_ref[...], ds_ref[...], nw_ref[...]).astype(o_ref.dtype)

    return pl.pallas_call(
        body, name=name, grid=(t // tr,),
        in_specs=[pl.BlockSpec((None, tr, w), lambda i: (0, i, 0)), pl.BlockSpec((None, tr, w), lambda i: (1, i, 0)),
                  row, row, par, par],
        out_specs=row, out_shape=_sds((t, w), BF16), compiler_params=_cp(("parallel",)),
    )(y, y, xbc, p, dskip, nw)


def ssd_finish_bwd(y, xbc, p, dskip, nw, dout, *, name):
    t = xbc.shape[0]
    tr = CHUNK
    w = SSD_INNER
    row = pl.BlockSpec((tr, w), lambda i: (i, 0))
    par = pl.BlockSpec((1, w), lambda i: (0, 0))

    def body(y0_ref, y1_ref, x_ref, z_ref, ds_ref, nw_ref, do_ref, dy_ref, dx_ref, dz_ref, acc_ref):
        _, vjp = jax.vjp(_ssd_finish_fn, y0_ref[...], y1_ref[...], x_ref[...], z_ref[...], ds_ref[...], nw_ref[...])
        dy0, _, dx, dz, dds, dnw = vjp(do_ref[...])
        dy_ref[...] = dy0
        dx_ref[...] = dx
        dz_ref[...] = dz.astype(dz_ref.dtype)

        @pl.when(pl.program_id(0) == 0)
        def _():
            acc_ref[...] = jnp.zeros_like(acc_ref)

        acc_ref[0:1, :] += dds
        acc_ref[1:2, :] += dnw

    return pl.pallas_call(
        body, name=name, grid=(t // tr,),
        in_specs=[pl.BlockSpec((None, tr, w), lambda i: (0, i, 0)), pl.BlockSpec((None, tr, w), lambda i: (1, i, 0)),
                  row, row, par, par, row],
        out_specs=(row, row, row, pl.BlockSpec((8, w), lambda i: (0, 0))),
        out_shape=(_sds((t, w), F32), _sds((t, w), F32), _sds((t, w), BF16), _sds((8, w), F32)),
        compiler_params=_cp(("arbitrary",)),
    )(y, y, xbc, p, dskip, nw, dout)


SGU_GROUPS = 8


def _sgu_fn(us, vs, ws, bs):
    n = SGU_GROUPS * LANE
    vf = [jax.nn.gelu(v) for v in vs]
    mu = sum(jnp.sum(v, axis=1, keepdims=True) for v in vf) / n
    var = sum(jnp.sum(jnp.square(v - mu), axis=1, keepdims=True) for v in vf) / n
    rstd = lax.rsqrt(var + NORM_EPS)
    return tuple(jax.nn.gelu(u) * (_bdot(w, (v - mu) * rstd, ((1,), (0,))) + b) for u, v, w, b in zip(us, vf, ws, bs))


def sgu_fwd(p, w, b, *, name):
    t = p.shape[0]
    wd = SGU_WIDTH

    def body(u_ref, v_ref, w_ref, b_ref, o_ref):
        sl = [slice(g * LANE, (g + 1) * LANE) for g in range(SGU_GROUPS)]
        ys = _sgu_fn([u_ref[:, s] for s in sl], [v_ref[:, s] for s in sl], [w_ref[g] for g in range(SGU_GROUPS)],
                     [b_ref[g] for g in range(SGU_GROUPS)])
        for s, yv in zip(sl, ys):
            o_ref[:, s] = yv.astype(o_ref.dtype)

    return pl.pallas_call(
        body, name=name, grid=(t // CHUNK,),
        in_specs=[pl.BlockSpec((CHUNK, wd), lambda i: (i, U_BLK * LANE // wd)), pl.BlockSpec((CHUNK, wd), lambda i: (i, V_BLK * LANE // wd)),
                  pl.BlockSpec((SGU_GROUPS, CHUNK, CHUNK), lambda i: (0, 0, 0)), pl.BlockSpec((SGU_GROUPS, CHUNK, 1), lambda i: (0, 0, 0))],
        out_specs=pl.BlockSpec((CHUNK, wd), lambda i: (i, 0)),
        out_shape=_sds((t, wd), BF16), compiler_params=_cp(("parallel",)),
    )(p, p, w, b)


def sgu_bwd(p, w, b, dout, *, name):
    t = p.shape[0]
    wd = SGU_WIDTH

    def body(u_ref, v_ref, w_ref, b_ref, do_ref, duv_ref, dw_ref, db_ref):
        sl = [slice(g * LANE, (g + 1) * LANE) for g in range(SGU_GROUPS)]
        _, vjp = jax.vjp(_sgu_fn, [u_ref[:, s] for s in sl], [v_ref[:, s] for s in sl],
                         [w_ref[g] for g in range(SGU_GROUPS)], [b_ref[g] for g in range(SGU_GROUPS)])
        dus, dvs, dws, dbs = vjp(tuple(do_ref[:, s] for s in sl))

        @pl.when(pl.program_id(0) == 0)
        def _():
            dw_ref[...] = jnp.zeros_like(dw_ref)
            db_ref[...] = jnp.zeros_like(db_ref)

        for g, s in enumerate(sl):
            duv_ref[:, s] = dus[g].astype(duv_ref.dtype)
            duv_ref[:, slice(wd + g * LANE, wd + (g + 1) * LANE)] = dvs[g].astype(duv_ref.dtype)
            dw_ref[g] += dws[g]
            db_ref[g] += dbs[g]

    wspec = pl.BlockSpec((SGU_GROUPS, CHUNK, CHUNK), lambda i: (0, 0, 0))
    bspec = pl.BlockSpec((SGU_GROUPS, CHUNK, 1), lambda i: (0, 0, 0))
    return pl.pallas_call(
        body, name=name, grid=(t // CHUNK,),
        in_specs=[pl.BlockSpec((CHUNK, wd), lambda i: (i, U_BLK * LANE // wd)), pl.BlockSpec((CHUNK, wd), lambda i: (i, V_BLK * LANE // wd)),
                  wspec, bspec, pl.BlockSpec((CHUNK, wd), lambda i: (i, 1))],
        out_specs=(pl.BlockSpec((CHUNK, 2 * wd), lambda i: (i, 0)), wspec, bspec),
        out_shape=(_sds((t, 2 * wd), BF16), _sds(w.shape, F32), _sds(b.shape, F32)),
        compiler_params=_cp(("arbitrary",)),
    )(p, p, w, b, dout)


def even_cols_permute(w):
    z, xbc, dt, u, v = jnp.split(w, (1024, 2560, 2592, 3616), axis=-1)
    pad = jnp.zeros(w.shape[:-1] + (EVEN_PAD_COLS - EVEN_COLS,), w.dtype)
    return jnp.concatenate([z, u, v, xbc, dt, pad], axis=-1)


def even_cols_unpermute(w):
    z, u, v, xbc, dt = jnp.split(w[..., :EVEN_COLS], (1024, 2048, 3072, 4608), axis=-1)
    return jnp.concatenate([z, xbc, dt, u, v], axis=-1)


def _dt_cols(p):
    t = p.shape[0]
    d = p[:, DT_BLK * LANE:DT_BLK * LANE + 4 * HEADS_PER_DG].reshape(t, 4, HEADS_PER_DG).transpose(1, 0, 2)
    return jnp.pad(d, ((0, 0), (0, 0), (0, LANE - HEADS_PER_DG)))


def _heads_to_lanes(a):
    return jnp.pad(a.reshape(4, 1, HEADS_PER_DG), ((0, 0), (0, 0), (0, LANE - HEADS_PER_DG)))


def even_params(conv_w, conv_b, dt_bias, a_log, d_skip, ssd_nw, sgu_w, sgu_b):
    wb = jnp.concatenate([conv_w, conv_b[None], jnp.zeros((2, XBC_DIM), F32)], axis=0)
    return dict(wb=wb, dtb=_heads_to_lanes(dt_bias), alog=_heads_to_lanes(a_log),
                dskip=jnp.repeat(d_skip, SSD_HEAD_DIM)[None], ssd_nw=ssd_nw[None], sgu_w=sgu_w, sgu_b=sgu_b[..., None])


def even_mixer_fwd(h, w_in, w_out, li, ep, tc, tag):
    p = mm_nn(h, w_in, li, "n", name=f"{tag}_in")
    xbc = conv_fwd(p, ep["wb"], tc, name=f"{tag}_conv")
    pre = _dt_cols(p)
    dt, cs = ssd_prep_fwd(pre, ep["dtb"], ep["alog"], name=f"{tag}_prep")
    y, sprev = ssd_fwd(xbc, dt, cs, tc, name=f"{tag}_ssd")
    yssd = ssd_finish_fwd(y, xbc, p, ep["dskip"], ep["ssd_nw"], name=f"{tag}_fin")
    ysgu = sgu_fwd(p, ep["sgu_w"], ep["sgu_b"], name=f"{tag}_sgu")
    ymix = jnp.concatenate([yssd, ysgu], axis=1)
    o = mm_nn(ymix, w_out, li, "k", name=f"{tag}_out")
    return o, (p, xbc, pre, dt, cs, y, sprev, ymix)


def even_mixer_bwd(saved, do, h, w_in, w_out, g_in, g_out, li, ep, tc, tag):
    p, xbc, pre, dt, cs, y, sprev, ymix = saved
    t = h.shape[0]
    g_out = mm_tn(ymix, do, g_out, li, "k", name=f"{tag}_out_dw")
    dymix = mm_nt(do, w_out, li, "k", name=f"{tag}_out_dx")
    dy, dxskip, dz, acc_fin = ssd_finish_bwd(y, xbc, p, ep["dskip"], ep["ssd_nw"], dymix, name=f"{tag}_fin_b")
    duv, dsgu_w, dsgu_b = sgu_bwd(p, ep["sgu_w"], ep["sgu_b"], dymix, name=f"{tag}_sgu_b")
    dxg, ddt, dcs = ssd_bwd(xbc, dt, cs, sprev, dy, tc, name=f"{tag}_ssd_b")
    dpre, acc_prep = ssd_prep_bwd(pre, ep["dtb"], ep["alog"], ddt, dcs, name=f"{tag}_prep_b")
    dxbc, dwb = conv_bwd(p, ep["wb"], dxg, dxskip, tc, name=f"{tag}_conv_b")
    ddt_cols = dpre[:, :, :HEADS_PER_DG].transpose(1, 0, 2).reshape(t, 4 * HEADS_PER_DG).astype(BF16)
    ddt_cols = jnp.pad(ddt_cols, ((0, 0), (0, EVEN_PAD_COLS - DT_BLK * LANE - 4 * HEADS_PER_DG)))
    dp = jnp.concatenate([dz, duv, dxbc, ddt_cols], axis=1)
    dh = mm_nt(dp, w_in, li, "n", name=f"{tag}_in_dx")
    g_in = mm_tn(h, dp, g_in, li, "n", name=f"{tag}_in_dw")
    small = dict(
        e_conv_w=dwb[:CONV_K], e_conv_b=dwb[CONV_K],
        e_dt_bias=acc_prep[:, 0, :HEADS_PER_DG].reshape(2, 2 * HEADS_PER_DG),
        e_a_log=acc_prep[:, 1, :HEADS_PER_DG].reshape(2, 2 * HEADS_PER_DG),
        e_d_skip=acc_fin[0].reshape(-1, SSD_HEAD_DIM).sum(axis=1), e_ssd_norm_w=acc_fin[1],
        e_sgu_w=dsgu_w, e_sgu_b=dsgu_b[..., 0])
    return dh, g_in, g_out, small


ATT_HEADS = 16
ATT_KV = 4
Q_BLKS, K_BLKS = ATT_HEADS, ATT_KV


def rope_tables(tc, s):
    quarter = ATT_HEAD_DIM // 4
    pos = jnp.arange(s)
    inv = ROPE_BASE ** (-jnp.arange(quarter, dtype=F32) / quarter)
    a_row = (pos // GRID_W).astype(F32)[:, None] * inv
    a_col = (pos % GRID_W).astype(F32)[:, None] * inv
    cos = jnp.concatenate([jnp.cos(a_row)] * 2 + [jnp.cos(a_col)] * 2, axis=1)
    sin = jnp.concatenate([-jnp.sin(a_row), jnp.sin(a_row), -jnp.sin(a_col), jnp.sin(a_col)], axis=1)
    return (jnp.concatenate([jnp.ones((tc, ATT_HEAD_DIM), F32), cos], axis=0),
            jnp.concatenate([jnp.zeros((tc, ATT_HEAD_DIM), F32), sin], axis=0))


def _swap_halves(x):
    lane = lax.broadcasted_iota(jnp.int32, x.shape, 1)
    return jnp.where(lane % 64 < 32, pltpu.roll(x, 96, 1), pltpu.roll(x, 32, 1))


def rope_fwd(p, cos, sin, *, name):
    t = p.shape[0]
    tr = _pick(t, (1088, 640))
    scale = ATT_HEAD_DIM ** -0.5

    def body(p_ref, c_ref, s_ref, o_ref):
        x = p_ref[...]
        r = x * c_ref[...] + _swap_halves(x) * s_ref[...]
        o_ref[...] = (r * jnp.where(pl.program_id(1) < Q_BLKS, scale, 1.0)).astype(o_ref.dtype)

    tab = pl.BlockSpec((tr, LANE), lambda i, j: (i, 0))
    return pl.pallas_call(
        body, name=name, grid=(t // tr, Q_BLKS + K_BLKS),
        in_specs=[pl.BlockSpec((tr, LANE), lambda i, j: (i, j)), tab, tab],
        out_specs=pl.BlockSpec((tr, LANE), lambda i, j: (i, j)),
        out_shape=_sds((t, (Q_BLKS + K_BLKS) * LANE), BF16), compiler_params=_cp(("parallel", "parallel")),
    )(p, cos, sin)


def rope_bwd(dq, dk, dv, cos, sin, *, name):
    t = dq.shape[0]
    tr = _pick(t, (1088, 640))
    scale = ATT_HEAD_DIM ** -0.5

    def body(dq_ref, dk_ref, dv_ref, c_ref, s_ref, o_ref):
        j = pl.program_id(1)

        def unrot(g):
            return g * c_ref[...] + _swap_halves(g * s_ref[...])

        @pl.when(j < Q_BLKS)
        def _():
            o_ref[...] = (unrot(dq_ref[...]) * scale).astype(o_ref.dtype)

        @pl.when((j >= Q_BLKS) & (j < Q_BLKS + K_BLKS))
        def _():
            o_ref[...] = unrot(dk_ref[...]).astype(o_ref.dtype)

        @pl.when(j >= Q_BLKS + K_BLKS)
        def _():
            o_ref[...] = dv_ref[...].astype(o_ref.dtype)

    tab = pl.BlockSpec((tr, LANE), lambda i, j: (i, 0))
    return pl.pallas_call(
        body, name=name, grid=(t // tr, Q_BLKS + 2 * K_BLKS),
        in_specs=[pl.BlockSpec((tr, LANE), lambda i, j: (i, jnp.minimum(j, Q_BLKS - 1))),
                  pl.BlockSpec((None, tr, LANE), lambda i, j: (jnp.clip(j - Q_BLKS, 0, K_BLKS - 1), i, 0)),
                  pl.BlockSpec((None, tr, LANE), lambda i, j: (jnp.clip(j - Q_BLKS - K_BLKS, 0, K_BLKS - 1), i, 0)), tab, tab],
        out_specs=pl.BlockSpec((tr, LANE), lambda i, j: (i, j)),
        out_shape=_sds((t, (Q_BLKS + 2 * K_BLKS) * LANE), BF16), compiler_params=_cp(("parallel", "parallel")),
    )(dq, dk, dv, cos, sin)


def _attn_tile(qs, kp, kc, kn, vp, vc, vn, kx, vx, sinks, is_lat, has_prev, has_next):
    q = kp.shape[0]
    row = lax.broadcasted_iota(jnp.int32, (q, q), 0)
    col = lax.broadcasted_iota(jnp.int32, (q, q), 1)
    m_prev = (col - row) >= (1 - has_prev) * q
    m_cur = (row - row) >= (1 - is_lat)
    m_next = (row - col) >= (1 - has_next) * q
    lane = lax.broadcasted_iota(jnp.int32, (1, LANE), 1)
    outs = []
    for qh, sinkrow in zip(qs, sinks):
        sink = jnp.sum(jnp.where(lane == 0, sinkrow, 0.0), axis=1, keepdims=True)
        s_p = jnp.where(m_prev, _bdot(qh, kp, ((1,), (1,))), NEG_INF)
        s_c = jnp.where(m_cur, _bdot(qh, kc, ((1,), (1,))), NEG_INF)
        s_n = jnp.where(m_next, _bdot(qh, kn, ((1,), (1,))), NEG_INF)
        s_x = _bdot(qh, kx, ((1,), (1,)))
        mx = [jnp.max(a, axis=1, keepdims=True) for a in (s_p, s_c, s_n, s_x)]
        m = lax.stop_gradient(jnp.maximum(jnp.maximum(jnp.maximum(mx[0], mx[1]), jnp.maximum(mx[2], mx[3])), sink))
        e = [jnp.exp(a - m) for a in (s_p, s_c, s_n, s_x)]
        inv = 1.0 / (sum(jnp.sum(a, axis=1, keepdims=True) for a in e) + jnp.exp(sink - m))
        outs.append(sum(_bdot(a * inv, v, ((1,), (0,))) for a, v in zip(e, (vp, vc, vn, vx))))
    return tuple(outs)


def _attn_specs(t, tc):
    nblk = t // CHUNK
    hw = ATT_GROUP * LANE
    kcol = lambda kv: Q_BLKS + kv
    vcol = lambda kv: Q_BLKS + K_BLKS + kv
    prev = lambda n: jnp.maximum(n - 1, 0)
    nxt = lambda n: jnp.minimum(n + 1, nblk - 1)
    blk = lambda rowf, colf: pl.BlockSpec((CHUNK, LANE), lambda kv, n: (rowf(n), colf(kv)))
    same = lambda n: n
    return [pl.BlockSpec((CHUNK, hw), lambda kv, n: (n, kv)),
            blk(prev, kcol), blk(same, kcol), blk(nxt, kcol), blk(prev, vcol), blk(same, vcol), blk(nxt, vcol),
            pl.BlockSpec((tc, LANE), lambda kv, n: (0, kcol(kv))), pl.BlockSpec((tc, LANE), lambda kv, n: (0, vcol(kv))),
            pl.BlockSpec((None, 8, LANE), lambda kv, n: (kv, 0, 0))]


def _attn_args(refs, n, nct, nblk):
    q_ref, kp, kc, kn, vp, vc, vn, kx, vx, sk = refs
    f = lambda r: r[...].astype(F32)
    qs = [q_ref[:, g * LANE:(g + 1) * LANE].astype(F32) for g in range(ATT_GROUP)]
    sinks = [sk[g:g + 1, :] for g in range(ATT_GROUP)]
    flags = ((n >= nct).astype(jnp.int32), (n >= nct + 1).astype(jnp.int32), ((n >= nct) & (n + 1 < nblk)).astype(jnp.int32))
    return (qs, f(kp), f(kc), f(kn), f(vp), f(vc), f(vn), f(kx), f(vx), sinks), flags


def attn_fwd(qk, p, sink, tc, *, name):
    t = qk.shape[0]
    nblk, nct = t // CHUNK, tc // CHUNK
    hw = ATT_GROUP * LANE

    def body(*refs):
        o_ref = refs[-1]
        args, flags = _attn_args(refs[:-1], pl.program_id(1), nct, nblk)
        outs = _attn_tile(*args, *flags)
        for g, o in enumerate(outs):
            o_ref[:, g * LANE:(g + 1) * LANE] = o.astype(o_ref.dtype)

    return pl.pallas_call(
        body, name=name, grid=(ATT_KV, nblk), in_specs=_attn_specs(t, tc),
        out_specs=pl.BlockSpec((CHUNK, hw), lambda kv, n: (n, kv)),
        out_shape=_sds((t, ATT_HEADS * LANE), BF16), compiler_params=_cp(("parallel", "parallel")),
    )(qk, qk, qk, qk, p, p, p, qk, p, sink)


def attn_bwd(qk, p, sink, do, tc, *, name):
    t = qk.shape[0]
    nblk, nct = t // CHUNK, tc // CHUNK
    hw = ATT_GROUP * LANE

    def body(*refs):
        do_ref, dq_ref, dk_ref, dv_ref, dsk_ref = refs[-5:]
        n = pl.program_id(1)
        args, flags = _attn_args(refs[:-5], n, nct, nblk)
        _, vjp = jax.vjp(lambda *a: _attn_tile(*a, *flags), *args)
        dqs, dkp, dkc, dkn, dvp, dvc, dvn, dkx, dvx, dsinks = vjp(tuple(do_ref[:, g * LANE:(g + 1) * LANE] for g in range(ATT_GROUP)))

        @pl.when(n == 0)
        def _():
            dk_ref[...] = jnp.zeros_like(dk_ref)
            dv_ref[...] = jnp.zeros_like(dv_ref)
            dsk_ref[...] = jnp.zeros_like(dsk_ref)

        for g in range(ATT_GROUP):
            dq_ref[:, g * LANE:(g + 1) * LANE] = dqs[g]
            dsk_ref[g:g + 1, :] += dsinks[g]
        for blk, dkb, dvb in ((jnp.maximum(n - 1, 0), dkp, dvp), (n, dkc, dvc), (jnp.minimum(n + 1, nblk - 1), dkn, dvn)):
            rows = pl.ds(pl.multiple_of(blk * CHUNK, CHUNK), CHUNK)
            dk_ref[rows, :] += dkb
            dv_ref[rows, :] += dvb
        dk_ref[0:tc, :] += dkx
        dv_ref[0:tc, :] += dvx

    kvacc = pl.BlockSpec((None, t, LANE), lambda kv, n: (kv, 0, 0))
    return pl.pallas_call(
        body, name=name, grid=(ATT_KV, nblk),
        in_specs=_attn_specs(t, tc) + [pl.BlockSpec((CHUNK, hw), lambda kv, n: (n, kv))],
        out_specs=(pl.BlockSpec((CHUNK, hw), lambda kv, n: (n, kv)), kvacc, kvacc,
                   pl.BlockSpec((None, 8, LANE), lambda kv, n: (kv, 0, 0))),
        out_shape=(_sds((t, ATT_HEADS * LANE), F32), _sds((ATT_KV, t, LANE), F32), _sds((ATT_KV, t, LANE), F32),
                   _sds((ATT_KV, 8, LANE), F32)),
        compiler_params=_cp(("parallel", "arbitrary")),
    )(qk, qk, qk, qk, p, p, p, qk, p, sink, do)


def sink_rows(sink):
    s = jnp.broadcast_to(sink.reshape(ATT_KV, ATT_GROUP, 1), (ATT_KV, ATT_GROUP, LANE))
    return jnp.pad(s, ((0, 0), (0, 8 - ATT_GROUP), (0, 0)))


def odd_mixer_fwd(h, w_qkv, w_out, li, sink, cos, sin, tc, tag):
    p = mm_nn(h, w_qkv, li, "n", name=f"{tag}_qkv")
    qk = rope_fwd(p, cos, sin, name=f"{tag}_rope")
    att = attn_fwd(qk, p, sink, tc, name=f"{tag}_att")
    o = mm_nn(att, w_out, li, "k", name=f"{tag}_out")
    return o, (p, qk, att)


def odd_mixer_bwd(saved, do, h, w_qkv, w_out, g_qkv, g_out, li, sink, cos, sin, tc, tag):
    p, qk, att = saved
    g_out = mm_tn(att, do, g_out, li, "k", name=f"{tag}_out_dw")
    datt = mm_nt(do, w_out, li, "k", name=f"{tag}_out_dx")
    dq, dk, dv, dsink = attn_bwd(qk, p, sink, datt, tc, name=f"{tag}_att_b")
    dp = rope_bwd(dq, dk, dv, cos, sin, name=f"{tag}_rope_b")
    dh = mm_nt(dp, w_qkv, li, "n", name=f"{tag}_qkv_dx")
    g_qkv = mm_tn(h, dp, g_qkv, li, "n", name=f"{tag}_qkv_dw")
    return dh, g_qkv, g_out, dict(o_sink=dsink[:, :ATT_GROUP, 0].reshape(-1))


ANY = pl.BlockSpec(memory_space=pl.ANY)


def _place():
    return lax.axis_index("x"), lax.axis_index("y"), lax.axis_index("c")


DMA_PIECES = 16


def _pieces(shape):
    if len(shape) < 2:
        return [()]
    lead, k = shape[:-2], shape[-2]
    split = 1
    while math.prod(lead) * split < DMA_PIECES and k % (2 * split) == 0 and (k // (2 * split)) % 16 == 0:
        split *= 2
    rows = k // split
    out = []
    for li in itertools.product(*[range(n) for n in lead]):
        out += [li + (pl.ds(q * rows, rows),) for q in range(split)]
    return out


def _start_pieces(make, src, dst):
    for idx in _pieces(src.shape):
        make(src.at[idx] if idx else src, dst.at[idx] if idx else dst).start()


def allgather8(blk, *, name):
    def body(x_ref, out_ref, send_sems, recv_sems, local_sem):
        x, y, c = _place()
        me, sibling = (x, y, c), (x, y, 1 - c)
        chips = [(1 - x, y), (x, 1 - y), (1 - x, 1 - y)]

        def slot(px, py, pc):
            return out_ref.at[4 * px + 2 * py + pc]

        def remote(k, to):
            return lambda src, dst: pltpu.make_async_remote_copy(
                src_ref=src, dst_ref=dst, send_sem=send_sems.at[k], recv_sem=recv_sems.at[k], device_id=to, device_id_type=MESH_ID)

        def local(src, dst):
            return pltpu.make_async_copy(src, dst, local_sem)

        _start_pieces(local, x_ref, slot(*me))
        _start_pieces(remote(0, sibling), x_ref, slot(*me))
        for j, chip in enumerate(chips):
            remote(1 + j, (*chip, c))(x_ref, slot(*me)).start()
        for j, chip in enumerate(chips):
            blk = slot(*chip, c)
            remote(1 + j, me)(blk, blk).wait_recv()
            _start_pieces(remote(4 + j, sibling), blk, blk)
        remote(0, me)(slot(*sibling), slot(*sibling)).wait_recv()
        for j, chip in enumerate(chips):
            blk = slot(*chip, 1 - c)
            remote(4 + j, me)(blk, blk).wait_recv()
        remote(0, sibling)(x_ref, slot(*me)).wait_send()
        for j, chip in enumerate(chips):
            remote(1 + j, (*chip, c))(x_ref, slot(*me)).wait_send()
            remote(4 + j, sibling)(slot(*chip, c), slot(*chip, c)).wait_send()
        local(x_ref, slot(*me)).wait()

    return pl.pallas_call(
        body, name=name, out_shape=_sds((N_DEV,) + blk.shape, blk.dtype), in_specs=[ANY], out_specs=ANY,
        scratch_shapes=[pltpu.SemaphoreType.DMA((7,)), pltpu.SemaphoreType.DMA((7,)), pltpu.SemaphoreType.DMA],
        compiler_params=pltpu.CompilerParams(has_side_effects=True),
    )(blk)


def rs_sibling(g, *, name):
    nchip, nl, kd, nd = g.shape
    lh = nl // 2

    def body(g_ref, out_ref, send_sem, recv_sem, local_sem):
        x, y, c = _place()

        def local(src, dst):
            return pltpu.make_async_copy(src, dst, local_sem)

        def remote(src, dst):
            return pltpu.make_async_remote_copy(src_ref=src, dst_ref=dst, send_sem=send_sem, recv_sem=recv_sem,
                                                device_id=(x, y, 1 - c), device_id_type=MESH_ID)

        keep, give = g_ref.at[:, pl.ds(c * lh, lh)], g_ref.at[:, pl.ds((1 - c) * lh, lh)]
        _start_pieces(local, keep, out_ref.at[0])
        _start_pieces(remote, give, out_ref.at[1])
        remote(give, out_ref.at[1]).wait()
        local(keep, out_ref.at[0]).wait()

    return pl.pallas_call(
        body, name=name, out_shape=_sds((2, nchip, lh, kd, nd), g.dtype), in_specs=[ANY], out_specs=ANY,
        scratch_shapes=[pltpu.SemaphoreType.DMA, pltpu.SemaphoreType.DMA, pltpu.SemaphoreType.DMA],
        compiler_params=pltpu.CompilerParams(has_side_effects=True),
    )(g)


def rs_chips(h, *, name):
    def body(h_ref, out_ref, send_sems, recv_sems, local_sem):
        x, y, c = _place()
        k = 2 * x + y

        def local(src, dst):
            return pltpu.make_async_copy(src, dst, local_sem)

        _start_pieces(local, h_ref.at[k], out_ref.at[k])
        copies = []
        for r in (1, 2, 3):
            px, py = (1 - x if r & 2 else x), (1 - y if r & 1 else y)
            copies.append(pltpu.make_async_remote_copy(
                src_ref=h_ref.at[2 * px + py], dst_ref=out_ref.at[k], send_sem=send_sems.at[r - 1], recv_sem=recv_sems.at[r - 1],
                device_id=(px, py, c), device_id_type=MESH_ID))
        for cp in copies:
            cp.start()
        for cp in copies:
            cp.wait()
        local(h_ref.at[k], out_ref.at[k]).wait()

    return pl.pallas_call(
        body, name=name, out_shape=_sds(h.shape, h.dtype), in_specs=[ANY], out_specs=ANY,
        scratch_shapes=[pltpu.SemaphoreType.DMA((3,)), pltpu.SemaphoreType.DMA((3,)), pltpu.SemaphoreType.DMA],
        compiler_params=pltpu.CompilerParams(has_side_effects=True),
    )(h)


def pair_gather(half, *, name):
    def body(h_ref, out_ref, send_sem, recv_sem, local_sem):
        x, y, c = _place()

        def local(src, dst):
            return pltpu.make_async_copy(src, dst, local_sem)

        def remote(src, dst):
            return pltpu.make_async_remote_copy(src_ref=src, dst_ref=dst, send_sem=send_sem, recv_sem=recv_sem,
                                                device_id=(x, y, 1 - c), device_id_type=MESH_ID)

        _start_pieces(local, h_ref, out_ref.at[c])
        _start_pieces(remote, h_ref, out_ref.at[c])
        remote(h_ref, out_ref.at[1 - c]).wait_send()
        remote(h_ref, out_ref.at[1 - c]).wait_recv()
        local(h_ref, out_ref.at[c]).wait()

    return pl.pallas_call(
        body, name=name, out_shape=_sds((2,) + half.shape, half.dtype), in_specs=[ANY], out_specs=ANY,
        scratch_shapes=[pltpu.SemaphoreType.DMA, pltpu.SemaphoreType.DMA, pltpu.SemaphoreType.DMA],
        compiler_params=pltpu.CompilerParams(has_side_effects=True),
    )(half)


def sum_slots(a, out_dtype, *, name):
    n = a.shape[0]
    cols = a.shape[-1]
    a3 = a.reshape(n, -1, cols)
    rows = a3.shape[1]
    tr = _pick(rows, (max(32, (1 << 19) // cols // 32 * 32),))

    def body(*refs):
        acc = refs[0][...].astype(F32)
        for r in refs[1:n]:
            acc = acc + r[...].astype(F32)
        refs[n][...] = acc.astype(out_dtype)

    return pl.pallas_call(
        body, name=name, grid=(rows // tr,),
        in_specs=[pl.BlockSpec((None, tr, cols), functools.partial(lambda j, i: (j, i, 0), j)) for j in range(n)],
        out_specs=pl.BlockSpec((tr, cols), lambda i: (i, 0)),
        out_shape=_sds((rows, cols), out_dtype), compiler_params=_cp(("parallel",)),
    )(*([a3] * n)).reshape(a.shape[1:])


def reduce_scatter_grad(g, tag):
    nchip, nl, kd, nd = g.shape
    pair = rs_sibling(g, name=f"{tag}_rs1")
    chip_sum = sum_slots(pair, BF16, name=f"{tag}_add1")
    parts = rs_chips(chip_sum, name=f"{tag}_rs2")
    half = sum_slots(parts, F32, name=f"{tag}_add2")
    return pair_gather(half, name=f"{tag}_rs3").reshape(nl, kd, nd)


def gather_weight(w, tag):
    nl = w.shape[0]
    lh = nl // 2
    half = lax.dynamic_slice_in_dim(w, lax.axis_index("c") * lh, lh, axis=0).astype(BF16)
    return allgather8(half, name=f"{tag}_ag").reshape((N_CHIP, nl) + w.shape[1:])


def mod_fwd(c16, w_mod, *, name):
    nl, d, ns = w_mod.shape
    tn = _pick(ns, (512,))

    def body(c_ref, w_ref, o_ref):
        o_ref[...] = jnp.dot(jax.nn.silu(c_ref[...]), w_ref[...], precision=HI, preferred_element_type=F32)

    return pl.pallas_call(
        body, name=name, grid=(nl, ns // tn),
        in_specs=[pl.BlockSpec((16, d), lambda l, j: (0, 0)), pl.BlockSpec((None, d, tn), lambda l, j: (l, 0, j))],
        out_specs=pl.BlockSpec((None, 16, tn), lambda l, j: (l, 0, j)),
        out_shape=_sds((nl, 16, ns), F32), compiler_params=_cp(("parallel", "parallel")),
    )(c16, w_mod)


def mod_bwd_w(c16, dm, *, name):
    nl, _, ns = dm.shape
    d = c16.shape[1]
    tn = _pick(ns, (512,))

    def body(c_ref, dm_ref, o_ref):
        o_ref[...] = lax.dot_general(jax.nn.silu(c_ref[...]), dm_ref[...], (((0,), (0,)), ((), ())), precision=HI,
                                     preferred_element_type=F32)

    return pl.pallas_call(
        body, name=name, grid=(nl, ns // tn),
        in_specs=[pl.BlockSpec((16, d), lambda l, j: (0, 0)), pl.BlockSpec((None, 16, tn), lambda l, j: (l, 0, j))],
        out_specs=pl.BlockSpec((None, d, tn), lambda l, j: (l, 0, j)),
        out_shape=_sds((nl, d, ns), F32), compiler_params=_cp(("parallel", "parallel")),
    )(c16, dm)


def mod_bwd_s(dm, w_mod, *, name):
    nl, d, ns = w_mod.shape
    td = _pick(d, (512,))

    def body(dm_ref, w_ref, o_ref):
        part = lax.dot_general(dm_ref[...], w_ref[...], (((1,), (1,)), ((), ())), precision=HI, preferred_element_type=F32)
        rowsum = jnp.sum(part[8:16], axis=0, keepdims=True)

        @pl.when(pl.program_id(1) == 0)
        def _():
            o_ref[...] = jnp.zeros_like(o_ref)

        o_ref[...] += jnp.broadcast_to(rowsum, o_ref.shape)

    return pl.pallas_call(
        body, name=name, grid=(d // td, nl),
        in_specs=[pl.BlockSpec((None, 16, ns), lambda i, l: (l, 0, 0)), pl.BlockSpec((None, td, ns), lambda i, l: (l, i, 0))],
        out_specs=pl.BlockSpec((8, td), lambda i, l: (0, i)),
        out_shape=_sds((8, d), F32), compiler_params=_cp(("parallel", "arbitrary")),
    )(dm, w_mod)


def colsum16(dm, *, name):
    nl, _, n = dm.shape
    tn = _pick(n, (2048,))

    def body(dm_ref, o_ref):
        o_ref[...] = jnp.broadcast_to(jnp.sum(dm_ref[...], axis=0, keepdims=True), o_ref.shape)

    return pl.pallas_call(
        body, name=name, grid=(nl, n // tn),
        in_specs=[pl.BlockSpec((None, 16, tn), lambda l, j: (l, 0, j))],
        out_specs=pl.BlockSpec((None, 8, tn), lambda l, j: (l, 0, j)),
        out_shape=_sds((nl, 8, n), F32), compiler_params=_cp(("parallel", "parallel")),
    )(dm)


def silu_grad_mul(g, c, *, name):
    def body(g_ref, c_ref, o_ref):
        _, vjp = jax.vjp(jax.nn.silu, c_ref[...])
        o_ref[...] = vjp(g_ref[...])[0]

    return pl.pallas_call(body, name=name, out_shape=_sds(g.shape, F32))(g, c)


def adamw(w, g, m, v, *, name):
    shape = w.shape
    cols = shape[-1] if len(shape) > 1 else LANE
    flat = [a.reshape(-1, cols) for a in (w, g, m, v)]
    rows = flat[0].shape[0]
    tr = _pick(rows, (max(8, (1 << 18) // cols // 8 * 8),)) if rows % 8 == 0 else rows
    c1 = 1.0 - ADAM_B1 ** ADAM_STEP
    c2 = 1.0 - ADAM_B2 ** ADAM_STEP

    def body(w_ref, g_ref, m_ref, v_ref, d_ref, nm_ref, nv_ref):
        gv = g_ref[...]
        nm = ADAM_B1 * m_ref[...] + (1.0 - ADAM_B1) * gv
        nv = ADAM_B2 * v_ref[...] + (1.0 - ADAM_B2) * (gv * gv)
        d_ref[...] = -ADAM_LR * ((nm / c1) / (jnp.sqrt(nv / c2) + ADAM_EPS) + ADAM_WD * w_ref[...])
        nm_ref[...] = nm
        nv_ref[...] = nv

    blk = pl.BlockSpec((tr, cols), lambda i: (i, 0))
    outs = pl.pallas_call(
        body, name=name, grid=(rows // tr,), in_specs=[blk] * 4, out_specs=(blk,) * 3,
        out_shape=(_sds((rows, cols), F32),) * 3, compiler_params=_cp(("parallel",)),
    )(*flat)
    return tuple(o.reshape(shape) for o in outs)


PACK_ELEMS = LANE * LANE


def _pack(arrs):
    flat = jnp.concatenate([a.reshape(-1).astype(F32) for a in arrs])
    return jnp.pad(flat, (0, (-flat.shape[0]) % PACK_ELEMS)).reshape(-1, LANE)


def _unpack(packed, shapes):
    flat = packed.reshape(-1)
    out, pos = [], 0
    for s in shapes:
        n = math.prod(s)
        out.append(flat[pos:pos + n].reshape(s))
        pos += n
    return out


def _chip_cols(a, chip, width):
    return lax.dynamic_slice_in_dim(a, chip * width, width, axis=a.ndim - 1)


def kernel(x, c, ctx, c_ctx, w_mod, b_mod, norm_w, w_ffn_in, w_ffn_out, e_w_in, e_conv_w, e_conv_b, e_dt_bias, e_a_log, e_d_skip, e_ssd_norm_w, e_sgu_w, e_sgu_b, e_w_out, o_w_qkv, o_sink, o_w_out, loss_target, m_c_ctx, m_w_mod, m_b_mod, m_norm_w, m_w_ffn_in, m_w_ffn_out, m_e_w_in, m_e_conv_w, m_e_conv_b, m_e_dt_bias, m_e_a_log, m_e_d_skip, m_e_ssd_norm_w, m_e_sgu_w, m_e_sgu_b, m_e_w_out, m_o_w_qkv, m_o_sink, m_o_w_out, v_c_ctx, v_w_mod, v_b_mod, v_norm_w, v_w_ffn_in, v_w_ffn_out, v_e_w_in, v_e_conv_w, v_e_conv_b, v_e_dt_bias, v_e_a_log, v_e_d_skip, v_e_ssd_norm_w, v_e_sgu_w, v_e_sgu_b, v_e_w_out, v_o_w_qkv, v_o_sink, v_o_w_out):
    xi, yi, ci = _place()
    chip = 2 * xi + yi
    me = 2 * chip + ci
    s, d = x.shape[1:]
    tc = ctx.shape[1]
    depth = w_mod.shape[0]
    n_even = e_w_in.shape[0]
    dq = norm_w.shape[-1]
    cq = e_conv_w.shape[-1]
    ns = w_mod.shape[-1]

    gath = allgather8(_pack([c, norm_w, e_conv_w]), name="ag_small").reshape(N_DEV, -1)
    c_all = gath[:, :d]
    per_chip = [_unpack(gath[2 * k, d:], [norm_w.shape, e_conv_w.shape]) for k in range(N_CHIP)]
    nw_full = jnp.concatenate([pc[0] for pc in per_chip], axis=-1)
    convw_full = jnp.concatenate([pc[1] for pc in per_chip], axis=-1)
    c16 = jnp.concatenate([c_all, jnp.broadcast_to(c_ctx[None], (8, d))], axis=0)

    mod_g = allgather8(mod_fwd(c16, w_mod, name="mod_fwd"), name="ag_mod")
    mod_all = jnp.concatenate([mod_g[2 * k] for k in range(N_CHIP)], axis=-1) + b_mod[:, None, :]
    mod_rows = jnp.stack([mod_all[:, 8], lax.dynamic_index_in_dim(mod_all, me, axis=1, keepdims=False)], axis=1)
    modtab = jnp.pad(mod_rows.reshape(depth, 2, 6, d), ((0, 0), (0, 0), (0, 2), (0, 0)))

    wg_ffn_in = gather_weight(w_ffn_in, "w_ffn_in")
    wg_ffn_out = gather_weight(w_ffn_out, "w_ffn_out")
    wg_e_out = gather_weight(e_w_out, "e_w_out")
    wg_qkv = gather_weight(o_w_qkv, "o_w_qkv")
    wg_o_out = gather_weight(o_w_out, "o_w_out")
    e_in_g = gather_weight(e_w_in, "e_w_in")
    wg_e_in = even_cols_permute(jnp.moveaxis(e_in_g, 0, 2).reshape(n_even, d, -1))[None]

    eps_ = [even_params(convw_full[i], e_conv_b[i], e_dt_bias[i], e_a_log[i], e_d_skip[i], e_ssd_norm_w[i], e_sgu_w[i], e_sgu_b[i])
            for i in range(n_even)]
    sinks = [sink_rows(o_sink[i]) for i in range(o_sink.shape[0])]
    cos, sin = rope_tables(tc, s)

    u = jnp.concatenate([ctx[0], x[0]], axis=0)
    saved = []
    for l in range(depth):
        mt, nw = modtab[l], nw_full[l]
        h1 = norm_mod_fwd(u, nw[0], mt, tc, 0, name=f"L{l}_norm1")
        if l % 2 == 0:
            o, ms = even_mixer_fwd(h1, wg_e_in, wg_e_out, l // 2, eps_[l // 2], tc, f"L{l}_mix")
        else:
            o, ms = odd_mixer_fwd(h1, wg_qkv, wg_o_out, l // 2, sinks[l // 2], cos, sin, tc, f"L{l}_mix")
        u1 = resid_fwd(u, o, nw[1], mt, tc, 0, name=f"L{l}_res1")
        h2 = norm_mod_fwd(u1, nw[2], mt, tc, 1, name=f"L{l}_norm2")
        p = mm_nn(h2, wg_ffn_in, l, "n", name=f"L{l}_ffn_in")
        a = swiglu_fwd(p, name=f"L{l}_swiglu")
        f = mm_nn(a, wg_ffn_out, l, "k", name=f"L{l}_ffn_out")
        saved.append((u, h1, ms, o, u1, h2, p, a, f))
        u = resid_fwd(u1, f, nw[3], mt, tc, 1, name=f"L{l}_res2")
    loss_part, du = loss_fwd_bwd(u, loss_target[0], tc, name="loss")
    loss = lax.psum(loss_part[0, 0], ("x", "y", "c"))

    g_ffn_in = jnp.zeros(wg_ffn_in.shape, BF16)
    g_ffn_out = jnp.zeros(wg_ffn_out.shape, BF16)
    g_e_in = jnp.zeros(wg_e_in.shape, BF16)
    g_e_out = jnp.zeros(wg_e_out.shape, BF16)
    g_qkv = jnp.zeros(wg_qkv.shape, BF16)
    g_o_out = jnp.zeros(wg_o_out.shape, BF16)
    d_nw, d_mt = [None] * depth, [None] * depth
    small_e, small_o = [None] * n_even, [None] * (depth - n_even)
    for l in reversed(range(depth)):
        mt, nw = modtab[l], nw_full[l]
        u0, h1, ms, o, u1, h2, p, a, f = saved[l]
        df, acc3 = resid_bwd(f, nw[3], mt, du, tc, 1, name=f"L{l}_res2_b")
        g_ffn_out = mm_tn(a, df, g_ffn_out, l, "k", name=f"L{l}_ffn_out_dw")
        da = mm_nt(df, wg_ffn_out, l, "k", name=f"L{l}_ffn_out_dx")
        dp = swiglu_bwd(p, da, name=f"L{l}_swiglu_b")
        dh2 = mm_nt(dp, wg_ffn_in, l, "n", name=f"L{l}_ffn_in_dx")
        g_ffn_in = mm_tn(h2, dp, g_ffn_in, l, "n", name=f"L{l}_ffn_in_dw")
        du1, acc2 = norm_mod_bwd(u1, nw[2], mt, dh2, du, tc, 1, name=f"L{l}_norm2_b")
        do, acc1 = resid_bwd(o, nw[1], mt, du1, tc, 0, name=f"L{l}_res1_b")
        if l % 2 == 0:
            dh1, g_e_in, g_e_out, small_e[l // 2] = even_mixer_bwd(ms, do, h1, wg_e_in, wg_e_out, g_e_in, g_e_out, l // 2,
                                                                  eps_[l // 2], tc, f"L{l}_mix")
        else:
            dh1, g_qkv, g_o_out, small_o[l // 2] = odd_mixer_bwd(ms, do, h1, wg_qkv, wg_o_out, g_qkv, g_o_out, l // 2,
                                                                 sinks[l // 2], cos, sin, tc, f"L{l}_mix")
        du, acc0 = norm_mod_bwd(u0, nw[0], mt, dh1, du1, tc, 0, name=f"L{l}_norm1_b")
        d_nw[l] = jnp.stack([acc[0, 0] + acc[1, 0] for acc in (acc0, acc1, acc2, acc3)])
        d_mt[l] = jnp.stack([acc0[:, 1], acc0[:, 2], acc1[:, 1], acc2[:, 1], acc2[:, 2], acc3[:, 1]], axis=1)
    grad_x = du[tc:][None]

    dmt_g = allgather8(jnp.pad(jnp.stack(d_mt), ((0, 0), (0, 0), (0, 2), (0, 0))), name="ag_dmod")[:, :, :, :6]
    dm16 = jnp.concatenate([dmt_g[:, :, 1].transpose(1, 0, 2, 3).reshape(depth, N_DEV, 6 * d),
                            dmt_g[:, :, 0].transpose(1, 0, 2, 3).reshape(depth, N_DEV, 6 * d)], axis=1)
    dm_sh = _chip_cols(dm16, chip, ns)
    grad_w_mod = mod_bwd_w(c16, dm_sh, name="mod_bwd_w")
    grad_b_mod = colsum16(dm16, name="mod_bwd_b")[:, 0]
    ds_cc = mod_bwd_s(dm_sh, w_mod, name="mod_bwd_s")[0]

    stack_e = lambda key: jnp.stack([se[key] for se in small_e])
    small_names = ["e_conv_b", "e_dt_bias", "e_a_log", "e_d_skip", "e_ssd_norm_w", "e_sgu_w", "e_sgu_b"]
    small_parts = [jnp.stack(d_nw), stack_e("e_conv_w")] + [stack_e(k) for k in small_names]
    small_parts += [jnp.stack([so["o_sink"] for so in small_o]), 0.5 * ds_cc]
    small_shapes = [a.shape for a in small_parts]
    small_sum = sum_slots(allgather8(_pack(small_parts), name="ag_small_grads"), F32, name="small_grads_sum")
    (g_nw, g_convw, g_convb, g_dtb, g_alog, g_dskip, g_ssdnw, g_sguw, g_sgub, g_sink, g_scc) = _unpack(small_sum, small_shapes)
    grad_c_ctx = silu_grad_mul(jnp.broadcast_to(g_scc[None], (8, d)), jnp.broadcast_to(c_ctx[None], (8, d)), name="c_ctx_grad")[0]
    grads = dict(
        c_ctx=grad_c_ctx, w_mod=grad_w_mod, b_mod=grad_b_mod, norm_w=_chip_cols(g_nw, chip, dq),
        e_conv_w=_chip_cols(g_convw, chip, cq), e_conv_b=g_convb, e_dt_bias=g_dtb.reshape(e_dt_bias.shape),
        e_a_log=g_alog.reshape(e_a_log.shape), e_d_skip=g_dskip, e_ssd_norm_w=g_ssdnw, e_sgu_w=g_sguw, e_sgu_b=g_sgub,
        o_sink=g_sink)

    grads["w_ffn_in"] = reduce_scatter_grad(g_ffn_in, "g_ffn_in")
    grads["w_ffn_out"] = reduce_scatter_grad(g_ffn_out, "g_ffn_out")
    g_e_in_c = jnp.moveaxis(even_cols_unpermute(g_e_in[0]).reshape(n_even, d, N_CHIP, -1), 2, 0)
    grads["e_w_in"] = reduce_scatter_grad(g_e_in_c, "g_e_in")
    grads["e_w_out"] = reduce_scatter_grad(g_e_out, "g_e_out")
    grads["o_w_qkv"] = reduce_scatter_grad(g_qkv, "g_qkv")
    grads["o_w_out"] = reduce_scatter_grad(g_o_out, "g_o_out")

    weights = dict(c_ctx=c_ctx, w_mod=w_mod, b_mod=b_mod, norm_w=norm_w, w_ffn_in=w_ffn_in, w_ffn_out=w_ffn_out, e_w_in=e_w_in,
                   e_conv_w=e_conv_w, e_conv_b=e_conv_b, e_dt_bias=e_dt_bias, e_a_log=e_a_log, e_d_skip=e_d_skip,
                   e_ssd_norm_w=e_ssd_norm_w, e_sgu_w=e_sgu_w, e_sgu_b=e_sgu_b, e_w_out=e_w_out, o_w_qkv=o_w_qkv, o_sink=o_sink,
                   o_w_out=o_w_out)
    ms_ = dict(c_ctx=m_c_ctx, w_mod=m_w_mod, b_mod=m_b_mod, norm_w=m_norm_w, w_ffn_in=m_w_ffn_in, w_ffn_out=m_w_ffn_out,
               e_w_in=m_e_w_in, e_conv_w=m_e_conv_w, e_conv_b=m_e_conv_b, e_dt_bias=m_e_dt_bias, e_a_log=m_e_a_log,
               e_d_skip=m_e_d_skip, e_ssd_norm_w=m_e_ssd_norm_w, e_sgu_w=m_e_sgu_w, e_sgu_b=m_e_sgu_b, e_w_out=m_e_w_out,
               o_w_qkv=m_o_w_qkv, o_sink=m_o_sink, o_w_out=m_o_w_out)
    vs_ = dict(c_ctx=v_c_ctx, w_mod=v_w_mod, b_mod=v_b_mod, norm_w=v_norm_w, w_ffn_in=v_w_ffn_in, w_ffn_out=v_w_ffn_out,
               e_w_in=v_e_w_in, e_conv_w=v_e_conv_w, e_conv_b=v_e_conv_b, e_dt_bias=v_e_dt_bias, e_a_log=v_e_a_log,
               e_d_skip=v_e_d_skip, e_ssd_norm_w=v_e_ssd_norm_w, e_sgu_w=v_e_sgu_w, e_sgu_b=v_e_sgu_b, e_w_out=v_e_w_out,
               o_w_qkv=v_o_w_qkv, o_sink=v_o_sink, o_w_out=v_o_w_out)
    names = list(weights)
    big = ("w_mod", "w_ffn_in", "w_ffn_out", "e_w_in", "e_w_out", "o_w_qkv", "o_w_out")
    small = [n for n in names if n not in big]
    delta, new_m, new_v = {}, {}, {}
    for n in big:
        delta[n], new_m[n], new_v[n] = adamw(weights[n], grads[n], ms_[n], vs_[n], name=f"adamw_{n}")
    packed = adamw(*[_pack([tab[n] for n in small]) for tab in (weights, grads, ms_, vs_)], name="adamw_small")
    shapes = [weights[n].shape for n in small]
    for tab, pk in zip((delta, new_m, new_v), packed):
        for n, val in zip(small, _unpack(pk, shapes)):
            tab[n] = val
    return (loss, grad_x, *[grads[n] for n in names], *[delta[n] for n in names], *[new_m[n] for n in names],
            *[new_v[n] for n in names])
```

```python
import functools
import itertools
import math

import jax
import jax.numpy as jnp
from jax import lax
from jax.experimental import pallas as pl
from jax.experimental.pallas import tpu as pltpu

F32 = jnp.float32
BF16 = jnp.bfloat16
HI = lax.Precision.HIGHEST

NORM_EPS = 1e-6
SSD_HEAD_DIM = 64
SSD_STATE = 128
CHUNK = 128
CONV_K = 5
ATT_HEAD_DIM = 128
ATT_GROUP = 4
ROPE_BASE = 10000.0
GRID_W = 64
NEG_INF = -1e30
ADAM_LR, ADAM_B1, ADAM_B2, ADAM_EPS, ADAM_WD, ADAM_STEP = 0.001, 0.9, 0.999, 1e-08, 0.01, 10

LANE = 128
VMEM_LIMIT = 56 * 1024 * 1024
MESH_ID = pl.DeviceIdType.MESH
N_DEV = 8
N_CHIP = 4


def _cp(sem=None):
    return pltpu.CompilerParams(dimension_semantics=sem, vmem_limit_bytes=VMEM_LIMIT)


def _sds(shape, dtype):
    return jax.ShapeDtypeStruct(tuple(shape), dtype)


def _pick(n, cands):
    for c in cands:
        if n % c == 0:
            return c
    for step in (LANE, 16):
        for c in range(min(n, cands[0]) // step * step, 0, -step):
            if n % c == 0:
                return c
    raise ValueError((n, cands))


def _w_index(blocked, layer, per_block_k, per_block_n):
    def idx(kblk, nblk):
        if blocked == "n":
            return (nblk // per_block_n, layer, kblk, nblk % per_block_n)
        return (kblk // per_block_k, layer, kblk % per_block_k, nblk)
    return idx


def mm_nn(a, w, layer, blocked, *, name, out_dtype=F32, tm=None, tn=None, tk=None):
    m, k_total = a.shape
    cb, _, kd, nd = w.shape
    n_total = nd * cb if blocked == "n" else nd
    assert k_total == (kd if blocked == "n" else kd * cb)
    tm = tm or _pick(m, (1088, 192))
    tn = tn or _pick(nd, (1408, 768, 512))
    tk = tk or _pick(kd, (2048, 1408, 512))
    nk = k_total // tk
    widx = _w_index(blocked, layer, kd // tk, nd // tn)

    def body(a_ref, w_ref, o_ref, acc_ref):
        kk = pl.program_id(2)
        part = jnp.dot(a_ref[...].astype(BF16), w_ref[...].astype(BF16), preferred_element_type=F32)

        @pl.when(kk == 0)
        def _():
            acc_ref[...] = part

        @pl.when(kk > 0)
        def _():
            acc_ref[...] += part

        @pl.when(kk == nk - 1)
        def _():
            o_ref[...] = acc_ref[...].astype(o_ref.dtype)

    return pl.pallas_call(
        body, name=name, grid=(m // tm, n_total // tn, nk),
        in_specs=[pl.BlockSpec((tm, tk), lambda i, j, k: (i, k)),
                  pl.BlockSpec((None, None, tk, tn), lambda i, j, k: widx(k, j))],
        out_specs=pl.BlockSpec((tm, tn), lambda i, j, k: (i, j)),
        out_shape=_sds((m, n_total), out_dtype),
        scratch_shapes=[pltpu.VMEM((tm, tn), F32)],
        compiler_params=_cp(("parallel", "parallel", "arbitrary")),
    )(a, w)


def mm_nt(dy, w, layer, blocked, *, name, out_dtype=F32, tm=None, tn=None, tk=None):
    m, n_total = dy.shape
    cb, _, kd, nd = w.shape
    k_total = kd if blocked == "n" else kd * cb
    assert n_total == (nd * cb if blocked == "n" else nd)
    tm = tm or _pick(m, (1088, 192))
    tn = tn or _pick(kd, (1024, 1408, 512))
    tk = tk or _pick(nd, (1408, 1024, 768))
    nk = n_total // tk
    widx = _w_index(blocked, layer, kd // tn, nd // tk)

    def body(a_ref, w_ref, o_ref, acc_ref):
        kk = pl.program_id(2)
        part = lax.dot_general(a_ref[...].astype(BF16), w_ref[...].astype(BF16), (((1,), (1,)), ((), ())),
                               preferred_element_type=F32)

        @pl.when(kk == 0)
        def _():
            acc_ref[...] = part

        @pl.when(kk > 0)
        def _():
            acc_ref[...] += part

        @pl.when(kk == nk - 1)
        def _():
            o_ref[...] = acc_ref[...].astype(o_ref.dtype)

    return pl.pallas_call(
        body, name=name, grid=(m // tm, k_total // tn, nk),
        in_specs=[pl.BlockSpec((tm, tk), lambda i, j, k: (i, k)),
                  pl.BlockSpec((None, None, tn, tk), lambda i, j, k: widx(j, k))],
        out_specs=pl.BlockSpec((tm, tn), lambda i, j, k: (i, j)),
        out_shape=_sds((m, k_total), out_dtype),
        scratch_shapes=[pltpu.VMEM((tm, tn), F32)],
        compiler_params=_cp(("parallel", "parallel", "arbitrary")),
    )(dy, w)


def mm_tn(x, dy, g, layer, blocked, *, name, tm=None, tn=None, tt=None):
    t_total, k_total = x.shape
    n_total = dy.shape[1]
    cb, _, kd, nd = g.shape
    assert k_total == (kd if blocked == "n" else kd * cb) and n_total == (nd * cb if blocked == "n" else nd)
    tm = tm or _pick(kd, (1024, 1408, 512))
    tn = tn or _pick(nd, (1408, 768, 512))
    tt = tt or _pick(t_total, (1088, 96))
    nt = t_total // tt
    widx = _w_index(blocked, layer, kd // tm, nd // tn)

    def body(x_ref, dy_ref, g_in, o_ref, acc_ref):
        del g_in
        tstep = pl.program_id(2)
        part = lax.dot_general(x_ref[...].astype(BF16), dy_ref[...].astype(BF16), (((0,), (0,)), ((), ())),
                               preferred_element_type=F32)

        @pl.when(tstep == 0)
        def _():
            acc_ref[...] = part

        @pl.when(tstep > 0)
        def _():
            acc_ref[...] += part

        @pl.when(tstep == nt - 1)
        def _():
            o_ref[...] = acc_ref[...].astype(o_ref.dtype)

    return pl.pallas_call(
        body, name=name, grid=(k_total // tm, n_total // tn, nt),
        in_specs=[pl.BlockSpec((tt, tm), lambda i, j, t: (t, i)),
                  pl.BlockSpec((tt, tn), lambda i, j, t: (t, j)),
                  pl.BlockSpec(memory_space=pl.ANY)],
        out_specs=pl.BlockSpec((None, None, tm, tn), lambda i, j, t: widx(i, j)),
        out_shape=_sds(g.shape, g.dtype),
        scratch_shapes=[pltpu.VMEM((tm, tn), F32)],
        input_output_aliases={2: 0},
        compiler_params=_cp(("parallel", "parallel", "arbitrary")),
    )(x, dy, g)


def _rms(x, w):
    return x * lax.rsqrt(jnp.mean(x * x, axis=-1, keepdims=True) + NORM_EPS) * w


def _row_tile(tc):
    return 256 if tc % 256 == 0 else 128


def _seg_spec(nct, d):
    return pl.BlockSpec((None, 8, d), lambda i: (jnp.minimum(i // nct, 1), 0, 0))


def _acc_rows(acc_ref, i, nct, rows):
    @pl.when((i == 0) | (i == nct))
    def _():
        acc_ref[...] = jnp.zeros_like(acc_ref)

    for r, val in enumerate(rows):
        acc_ref[r:r + 1, :] += val


def norm_mod_fwd(u, nw, modtab, tc, which, *, name):
    t, d = u.shape
    tr = _row_tile(tc)
    nct = tc // tr
    r0 = 3 * which

    def body(u_ref, nw_ref, mt_ref, h_ref):
        sh, sc = mt_ref[r0:r0 + 1, :], mt_ref[r0 + 1:r0 + 2, :]
        h_ref[...] = (_rms(u_ref[...], nw_ref[...]) * (1.0 + sc) + sh).astype(h_ref.dtype)

    return pl.pallas_call(
        body, name=name, grid=(t // tr,),
        in_specs=[pl.BlockSpec((tr, d), lambda i: (i, 0)), pl.BlockSpec((1, d), lambda i: (0, 0)), _seg_spec(nct, d)],
        out_specs=pl.BlockSpec((tr, d), lambda i: (i, 0)),
        out_shape=_sds((t, d), BF16), compiler_params=_cp(("arbitrary",)),
    )(u, nw.reshape(1, d), modtab)


def norm_mod_bwd(u, nw, modtab, dh, du_in, tc, which, *, name):
    t, d = u.shape
    tr = _row_tile(tc)
    nct = tc // tr
    r0 = 3 * which

    def body(u_ref, nw_ref, mt_ref, dh_ref, dui_ref, du_ref, acc_ref):
        i = pl.program_id(0)
        sh, sc = mt_ref[r0:r0 + 1, :], mt_ref[r0 + 1:r0 + 2, :]
        _, vjp = jax.vjp(lambda x, w, a, b: _rms(x, w) * (1.0 + b) + a, u_ref[...], nw_ref[...], sh, sc)
        dx, dw, dsh, dsc = vjp(dh_ref[...].astype(F32))
        du_ref[...] = dui_ref[...] + dx
        _acc_rows(acc_ref, i, nct, (dw, dsh, dsc))

    row = pl.BlockSpec((tr, d), lambda i: (i, 0))
    return pl.pallas_call(
        body, name=name, grid=(t // tr,),
        in_specs=[row, pl.BlockSpec((1, d), lambda i: (0, 0)), _seg_spec(nct, d), row, row],
        out_specs=(row, _seg_spec(nct, d)),
        out_shape=(_sds((t, d), F32), _sds((2, 8, d), F32)), compiler_params=_cp(("arbitrary",)),
    )(u, nw.reshape(1, d), modtab, dh, du_in)


def resid_fwd(u, o, nw, modtab, tc, which, *, name):
    t, d = u.shape
    tr = _row_tile(tc)
    nct = tc // tr
    r0 = 3 * which + 2

    def body(u_ref, o_ref, nw_ref, mt_ref, out_ref):
        out_ref[...] = u_ref[...] + mt_ref[r0:r0 + 1, :] * _rms(o_ref[...], nw_ref[...])

    row = pl.BlockSpec((tr, d), lambda i: (i, 0))
    return pl.pallas_call(
        body, name=name, grid=(t // tr,),
        in_specs=[row, row, pl.BlockSpec((1, d), lambda i: (0, 0)), _seg_spec(nct, d)],
        out_specs=row, out_shape=_sds((t, d), F32), compiler_params=_cp(("arbitrary",)),
    )(u, o, nw.reshape(1, d), modtab)


def resid_bwd(o, nw, modtab, du, tc, which, *, name):
    t, d = o.shape
    tr = _row_tile(tc)
    nct = tc // tr
    r0 = 3 * which + 2

    def body(o_ref, nw_ref, mt_ref, du_ref, do_ref, acc_ref):
        i = pl.program_id(0)
        _, vjp = jax.vjp(lambda x, w, g: g * _rms(x, w), o_ref[...], nw_ref[...], mt_ref[r0:r0 + 1, :])
        dx, dw, dg = vjp(du_ref[...])
        do_ref[...] = dx.astype(do_ref.dtype)
        _acc_rows(acc_ref, i, nct, (dw, dg))

    row = pl.BlockSpec((tr, d), lambda i: (i, 0))
    return pl.pallas_call(
        body, name=name, grid=(t // tr,),
        in_specs=[row, pl.BlockSpec((1, d), lambda i: (0, 0)), _seg_spec(nct, d), row],
        out_specs=(row, _seg_spec(nct, d)),
        out_shape=(_sds((t, d), BF16), _sds((2, 8, d), F32)), compiler_params=_cp(("arbitrary",)),
    )(o, nw.reshape(1, d), modtab, du)


def swiglu_fwd(p, *, name):
    t, h2 = p.shape
    h = h2 // 2
    tr = _pick(t, (1088, 192))
    tc = _pick(h, (512, 256))
    nh = h // tc

    def body(g_ref, u_ref, a_ref):
        a_ref[...] = (jax.nn.silu(g_ref[...]) * u_ref[...]).astype(a_ref.dtype)

    return pl.pallas_call(
        body, name=name, grid=(t // tr, nh),
        in_specs=[pl.BlockSpec((tr, tc), lambda i, j: (i, j)), pl.BlockSpec((tr, tc), lambda i, j: (i, j + nh))],
        out_specs=pl.BlockSpec((tr, tc), lambda i, j: (i, j)),
        out_shape=_sds((t, h), BF16), compiler_params=_cp(("parallel", "parallel")),
    )(p, p)


def swiglu_bwd(p, da, *, name):
    t, h2 = p.shape
    h = h2 // 2
    tr = _pick(t, (1088, 192))
    tc = _pick(h, (512, 256))
    nh = h // tc

    def body(g_ref, u_ref, da_ref, dp_ref):
        j = pl.program_id(1)
        _, vjp = jax.vjp(lambda g, u: jax.nn.silu(g) * u, g_ref[...], u_ref[...])
        dg, du = vjp(da_ref[...])

        @pl.when(j < nh)
        def _():
            dp_ref[...] = dg.astype(dp_ref.dtype)

        @pl.when(j >= nh)
        def _():
            dp_ref[...] = du.astype(dp_ref.dtype)

    return pl.pallas_call(
        body, name=name, grid=(t // tr, 2 * nh),
        in_specs=[pl.BlockSpec((tr, tc), lambda i, j: (i, j % nh)), pl.BlockSpec((tr, tc), lambda i, j: (i, j % nh + nh)),
                  pl.BlockSpec((tr, tc), lambda i, j: (i, j % nh))],
        out_specs=pl.BlockSpec((tr, tc), lambda i, j: (i, j)),
        out_shape=_sds((t, h2), BF16), compiler_params=_cp(("parallel", "parallel")),
    )(p, p, da)


def loss_fwd_bwd(u, target, tc, *, name):
    t, d = u.shape
    tr = _row_tile(tc)
    nct = tc // tr

    def body(u_ref, t_ref, loss_ref, du_ref):
        i = pl.program_id(0)

        @pl.when(i == 0)
        def _():
            loss_ref[...] = jnp.zeros_like(loss_ref)

        @pl.when(i < nct)
        def _():
            du_ref[...] = jnp.zeros_like(du_ref)

        @pl.when(i >= nct)
        def _():
            err = u_ref[...] - t_ref[...]
            du_ref[...] = err * (1.0 / d)
            loss_ref[...] += jnp.sum(jnp.sum(err * err, axis=1, keepdims=True), axis=0, keepdims=True) * (0.5 / d)

    return pl.pallas_call(
        body, name=name, grid=(t // tr,),
        in_specs=[pl.BlockSpec((tr, d), lambda i: (i, 0)), pl.BlockSpec((tr, d), lambda i: (jnp.maximum(i - nct, 0), 0))],
        out_specs=(pl.BlockSpec((1, 1), lambda i: (0, 0)), pl.BlockSpec((tr, d), lambda i: (i, 0))),
        out_shape=(_sds((1, 1), F32), _sds((t, d), F32)), compiler_params=_cp(("arbitrary",)),
    )(u, target)


SSD_INNER = 1024
SGU_WIDTH = 1024
XBC_DIM = 1536
EVEN_COLS = 4640
EVEN_PAD_COLS = 5120
Z_BLK, U_BLK, V_BLK, X_BLK, B_BLK, C_BLK, DT_BLK = 0, 8, 16, 24, 32, 34, 36
PAD_ROWS = 8


def _conv_scratch_fill(pad_ref, val, tc, s):
    pad_ref[...] = jnp.zeros_like(pad_ref)
    pad_ref[PAD_ROWS:PAD_ROWS + tc, :] = val[:tc]
    pad_ref[2 * PAD_ROWS + tc:2 * PAD_ROWS + tc + s, :] = val[tc:]


def _conv_taps(pad_ref, tc, s, k):
    off = k - CONV_K // 2
    return (pad_ref[PAD_ROWS + off:PAD_ROWS + off + tc, :],
            pad_ref[2 * PAD_ROWS + tc + off:2 * PAD_ROWS + tc + off + s, :])


def conv_fwd(p, wb, tc, *, name):
    t = p.shape[0]
    s = t - tc
    nblk = XBC_DIM // LANE

    def body(p_ref, wb_ref, out_ref, pad_ref):
        _conv_scratch_fill(pad_ref, p_ref[...], tc, s)
        acc_c = jnp.zeros((tc, LANE), F32) + wb_ref[5:6, :]
        acc_l = jnp.zeros((s, LANE), F32) + wb_ref[5:6, :]
        for k in range(CONV_K):
            xc, xl = _conv_taps(pad_ref, tc, s, k)
            acc_c += xc * wb_ref[k:k + 1, :]
            acc_l += xl * wb_ref[k:k + 1, :]
        out_ref[:tc, :] = jax.nn.silu(acc_c)
        out_ref[tc:, :] = jax.nn.silu(acc_l)

    return pl.pallas_call(
        body, name=name, grid=(nblk,),
        in_specs=[pl.BlockSpec((t, LANE), lambda j: (0, X_BLK + j)), pl.BlockSpec((8, LANE), lambda j: (0, j))],
        out_specs=pl.BlockSpec((t, LANE), lambda j: (0, j)),
        out_shape=_sds((t, XBC_DIM), F32),
        scratch_shapes=[pltpu.VMEM((t + 3 * PAD_ROWS, LANE), F32)],
        compiler_params=_cp(("parallel",)),
    )(p, wb)


def conv_bwd(p, wb, dxg, dskip, tc, *, name):
    t = p.shape[0]
    s = t - tc
    nblk = XBC_DIM // LANE
    nx = SSD_INNER // LANE

    def grp(j):
        return jnp.where(j < nx, j // 4, (j - nx) % 2)

    def sub(j):
        return jnp.where(j < nx, j % 4, 4 + (j - nx) // 2)

    def body(p_ref, wb_ref, d0_ref, d1_ref, ds_ref, dp_ref, dwb_ref, pad_ref, dpad_ref):
        j = pl.program_id(0)
        _conv_scratch_fill(pad_ref, p_ref[...], tc, s)
        pre = [jnp.zeros((tc, LANE), F32) + wb_ref[5:6, :], jnp.zeros((s, LANE), F32) + wb_ref[5:6, :]]
        for k in range(CONV_K):
            xc, xl = _conv_taps(pad_ref, tc, s, k)
            pre[0] += xc * wb_ref[k:k + 1, :]
            pre[1] += xl * wb_ref[k:k + 1, :]
        dx = d0_ref[...] + d1_ref[...] + jnp.where(j < nx, ds_ref[...], 0.0)
        dpre = []
        for part, rows in ((0, slice(0, tc)), (1, slice(tc, t))):
            sig = jax.nn.sigmoid(pre[part])
            dpre.append(dx[rows] * (sig * (1.0 + pre[part] * (1.0 - sig))))
        dwb_ref[...] = jnp.zeros_like(dwb_ref)
        dwb_ref[5:6, :] = jnp.sum(dpre[0], axis=0, keepdims=True) + jnp.sum(dpre[1], axis=0, keepdims=True)
        for k in range(CONV_K):
            xc, xl = _conv_taps(pad_ref, tc, s, k)
            dwb_ref[k:k + 1, :] = (jnp.sum(dpre[0] * xc, axis=0, keepdims=True)
                                   + jnp.sum(dpre[1] * xl, axis=0, keepdims=True))
        _conv_scratch_fill(dpad_ref, jnp.concatenate(dpre, axis=0), tc, s)
        acc_c = jnp.zeros((tc, LANE), F32)
        acc_l = jnp.zeros((s, LANE), F32)
        for k in range(CONV_K):
            gc, gl = _conv_taps(dpad_ref, tc, s, CONV_K - 1 - k)
            acc_c += gc * wb_ref[k:k + 1, :]
            acc_l += gl * wb_ref[k:k + 1, :]
        dp_ref[:tc, :] = acc_c.astype(dp_ref.dtype)
        dp_ref[tc:, :] = acc_l.astype(dp_ref.dtype)

    col = pl.BlockSpec((t, LANE), lambda j: (0, j))
    return pl.pallas_call(
        body, name=name, grid=(nblk,),
        in_specs=[pl.BlockSpec((t, LANE), lambda j: (0, X_BLK + j)), pl.BlockSpec((8, LANE), lambda j: (0, j)),
                  pl.BlockSpec((None, None, t, LANE), lambda j: (0, grp(j), 0, sub(j))),
                  pl.BlockSpec((None, None, t, LANE), lambda j: (1, grp(j), 0, sub(j))),
                  pl.BlockSpec((t, LANE), lambda j: (0, jnp.minimum(j, nx - 1)))],
        out_specs=(col, pl.BlockSpec((8, LANE), lambda j: (0, j))),
        out_shape=(_sds((t, XBC_DIM), BF16), _sds((8, XBC_DIM), F32)),
        scratch_shapes=[pltpu.VMEM((t + 3 * PAD_ROWS, LANE), F32), pltpu.VMEM((t + 3 * PAD_ROWS, LANE), F32)],
        compiler_params=_cp(("parallel",)),
    )(p, wb, dxg, dxg, dskip)


HEADS_PER_DG = 8


def _ssd_prep_fn(pre, bias, alog, rev):
    q = pre.shape[0]
    lane = lax.broadcasted_iota(jnp.int32, (1, LANE), 1)
    dt = jnp.where(lane < HEADS_PER_DG, jax.nn.softplus(pre + bias), 0.0)
    row = lax.broadcasted_iota(jnp.int32, (q, q), 0)
    col = lax.broadcasted_iota(jnp.int32, (q, q), 1)
    tri = jnp.where((col - row) * jnp.where(rev, 1, -1) >= 0, 1.0, 0.0)
    cs = jnp.dot(tri, dt * (-jnp.exp(alog)), precision=HI, preferred_element_type=F32)
    return dt, cs


def _scan_chunk(nc_ctx, nch):
    def idx(dg, i):
        fwd = i
        bwd = jnp.where(i < nc_ctx, nc_ctx - 1 - i, nch - 1 - (i - nc_ctx))
        return jnp.where(dg // 2 == 0, fwd, bwd)
    return idx


def ssd_prep_fwd(pre, bias, alog, *, name):
    _, t, _ = pre.shape
    blk = pl.BlockSpec((None, CHUNK, LANE), lambda dg, i: (dg, i, 0))
    par = pl.BlockSpec((None, 1, LANE), lambda dg, i: (dg, 0, 0))

    def body(pre_ref, b_ref, a_ref, dt_ref, cs_ref):
        dt, cs = _ssd_prep_fn(pre_ref[...], b_ref[...], a_ref[...], pl.program_id(0) // 2 == 1)
        dt_ref[...] = dt
        cs_ref[...] = cs

    return pl.pallas_call(
        body, name=name, grid=(4, t // CHUNK), in_specs=[blk, par, par], out_specs=(blk, blk),
        out_shape=(_sds(pre.shape, F32), _sds(pre.shape, F32)), compiler_params=_cp(("parallel", "parallel")),
    )(pre, bias, alog)


def ssd_prep_bwd(pre, bias, alog, ddt, dcs, *, name):
    _, t, _ = pre.shape
    blk = pl.BlockSpec((None, CHUNK, LANE), lambda dg, i: (dg, i, 0))
    par = pl.BlockSpec((None, 1, LANE), lambda dg, i: (dg, 0, 0))

    def body(pre_ref, b_ref, a_ref, ddt_ref, dcs_ref, dpre_ref, acc_ref):
        rev = pl.program_id(0) // 2 == 1
        _, vjp = jax.vjp(lambda x, b, a: _ssd_prep_fn(x, b, a, rev), pre_ref[...], b_ref[...], a_ref[...])
        dpre, db, da = vjp((ddt_ref[...], dcs_ref[...]))
        dpre_ref[...] = dpre

        @pl.when(pl.program_id(1) == 0)
        def _():
            acc_ref[...] = jnp.zeros_like(acc_ref)

        acc_ref[0:1, :] += db
        acc_ref[1:2, :] += da

    return pl.pallas_call(
        body, name=name, grid=(4, t // CHUNK), in_specs=[blk, par, par, blk, blk],
        out_specs=(blk, pl.BlockSpec((None, 8, LANE), lambda dg, i: (dg, 0, 0))),
        out_shape=(_sds(pre.shape, F32), _sds((4, 8, LANE), F32)), compiler_params=_cp(("parallel", "arbitrary")),
    )(pre, bias, alog, ddt, dcs)


def _onehot_col(a, h):
    lane = lax.broadcasted_iota(jnp.int32, (1, a.shape[1]), 1)
    return jnp.sum(jnp.where(lane == h, a, 0.0), axis=1, keepdims=True)


def _onehot_row(a, h):
    sub = lax.broadcasted_iota(jnp.int32, (a.shape[0], 1), 0)
    return jnp.sum(jnp.where(sub == h, a, 0.0), axis=0, keepdims=True)


def _bdot(a, b, dims):
    return lax.dot_general(a.astype(BF16), b.astype(BF16), (dims, ((), ())), preferred_element_type=F32)


def _ssd_pair(xblk, dt, cs, bm, cm, sp, rev, pair):
    q = xblk.shape[0]
    lane = lax.broadcasted_iota(jnp.int32, (1, LANE), 1)
    sub = lax.broadcasted_iota(jnp.int32, (LANE, 1), 0)
    row = lax.broadcasted_iota(jnp.int32, (q, q), 0)
    col = lax.broadcasted_iota(jnp.int32, (q, q), 1)
    mask = (col - row) * jnp.where(rev, 1, -1) >= 0
    last = jnp.where(rev, 0, q - 1)
    cs_t = cs.T
    g = _bdot(cm, bm, ((1,), (1,)))
    y = jnp.zeros((q, LANE), F32)
    escale = jnp.zeros((q, LANE), F32)
    xw = jnp.zeros((q, LANE), F32)
    dec = jnp.zeros((LANE, 1), F32)
    for hh in range(2):
        h = 2 * pair + hh
        c_col = _onehot_col(cs, h)
        c_row = _onehot_row(cs_t, h)
        tot = jnp.sum(jnp.where(lax.broadcasted_iota(jnp.int32, (1, q), 1) == last, c_row, 0.0), axis=1, keepdims=True)
        in_head = (lane >= hh * SSD_HEAD_DIM) & (lane < (hh + 1) * SSD_HEAD_DIM)
        xh = jnp.where(in_head, xblk * _onehot_col(dt, h), 0.0)
        ldec = jnp.where(mask, jnp.exp(jnp.where(mask, c_col - c_row, 0.0)), 0.0)
        y = y + _bdot(g * ldec, xh, ((1,), (0,)))
        escale = escale + jnp.where(in_head, jnp.exp(c_col), 0.0)
        xw = xw + xh * jnp.exp(tot - c_col)
        dec = dec + jnp.where((sub >= hh * SSD_HEAD_DIM) & (sub < (hh + 1) * SSD_HEAD_DIM), jnp.exp(tot), 0.0)
    y = y + _bdot(cm, sp, ((1,), (1,))) * escale
    s_new = sp * dec + _bdot(xw, bm, ((0,), (0,)))
    return y, s_new


def ssd_fwd(xbc, dt, cs, tc, *, name):
    t = xbc.shape[0]
    nch = t // CHUNK
    sidx = _scan_chunk(tc // CHUNK, nch)
    gw = SSD_INNER // 2
    nb = SSD_INNER // LANE

    def body(x_ref, b_ref, c_ref, dt_ref, cs_ref, y_ref, sp_ref, s_ref):
        @pl.when(pl.program_id(1) == 0)
        def _():
            s_ref[...] = jnp.zeros_like(s_ref)

        rev = pl.program_id(0) // 2 == 1
        sp_ref[...] = s_ref[...]
        for p in range(gw // LANE):
            blk = slice(p * LANE, (p + 1) * LANE)
            y, s_new = _ssd_pair(x_ref[:, blk], dt_ref[...], cs_ref[...], b_ref[...], c_ref[...], s_ref[blk, :], rev, p)
            y_ref[:, blk] = y
            s_ref[blk, :] = s_new

    return pl.pallas_call(
        body, name=name, grid=(4, nch),
        in_specs=[pl.BlockSpec((CHUNK, gw), lambda dg, i: (sidx(dg, i), dg % 2)),
                  pl.BlockSpec((CHUNK, LANE), lambda dg, i: (sidx(dg, i), nb + dg % 2)),
                  pl.BlockSpec((CHUNK, LANE), lambda dg, i: (sidx(dg, i), nb + 2 + dg % 2)),
                  pl.BlockSpec((None, CHUNK, LANE), lambda dg, i: (dg, sidx(dg, i), 0)),
                  pl.BlockSpec((None, CHUNK, LANE), lambda dg, i: (dg, sidx(dg, i), 0))],
        out_specs=(pl.BlockSpec((None, CHUNK, gw), lambda dg, i: (dg // 2, sidx(dg, i), dg % 2)),
                   pl.BlockSpec((None, None, gw, SSD_STATE), lambda dg, i: (dg, sidx(dg, i), 0, 0))),
        out_shape=(_sds((2, t, SSD_INNER), F32), _sds((4, nch, gw, SSD_STATE), F32)),
        scratch_shapes=[pltpu.VMEM((gw, SSD_STATE), F32)],
        compiler_params=_cp(("parallel", "arbitrary")),
    )(xbc, xbc, xbc, dt, cs)


def ssd_bwd(xbc, dt, cs, sprev, dy, tc, *, name):
    t = xbc.shape[0]
    nch = t // CHUNK
    fidx = _scan_chunk(tc // CHUNK, nch)
    sidx = lambda dg, i: fidx(dg, nch - 1 - i)
    gw = SSD_INNER // 2
    nb = SSD_INNER // LANE

    def body(x_ref, b_ref, c_ref, dt_ref, cs_ref, sp_ref, dy_ref, dxg_ref, ddt_ref, dcs_ref, ds_ref):
        @pl.when(pl.program_id(1) == 0)
        def _():
            ds_ref[...] = jnp.zeros_like(ds_ref)

        rev = pl.program_id(0) // 2 == 1
        ddt = jnp.zeros((CHUNK, LANE), F32)
        dcs = jnp.zeros((CHUNK, LANE), F32)
        db = jnp.zeros((CHUNK, SSD_STATE), F32)
        dc = jnp.zeros((CHUNK, SSD_STATE), F32)
        for p in range(gw // LANE):
            blk = slice(p * LANE, (p + 1) * LANE)
            _, vjp = jax.vjp(functools.partial(_ssd_pair, rev=rev, pair=p),
                             x_ref[:, blk], dt_ref[...], cs_ref[...], b_ref[...], c_ref[...], sp_ref[blk, :])
            dx, ddt_p, dcs_p, db_p, dc_p, dsp = vjp((dy_ref[:, blk], ds_ref[blk, :]))
            dxg_ref[:, blk] = dx
            ds_ref[blk, :] = dsp
            ddt, dcs, db, dc = ddt + ddt_p, dcs + dcs_p, db + db_p, dc + dc_p
        dxg_ref[:, gw:gw + SSD_STATE] = db
        dxg_ref[:, gw + SSD_STATE:] = dc
        ddt_ref[...] = ddt
        dcs_ref[...] = dcs

    hd = pl.BlockSpec((None, CHUNK, LANE), lambda dg, i: (dg, sidx(dg, i), 0))
    return pl.pallas_call(
        body, name=name, grid=(4, nch),
        in_specs=[pl.BlockSpec((CHUNK, gw), lambda dg, i: (sidx(dg, i), dg % 2)),
                  pl.BlockSpec((CHUNK, LANE), lambda dg, i: (sidx(dg, i), nb + dg % 2)),
                  pl.BlockSpec((CHUNK, LANE), lambda dg, i: (sidx(dg, i), nb + 2 + dg % 2)),
                  hd, hd,
                  pl.BlockSpec((None, None, gw, SSD_STATE), lambda dg, i: (dg, sidx(dg, i), 0, 0)),
                  pl.BlockSpec((CHUNK, gw), lambda dg, i: (sidx(dg, i), dg % 2))],
        out_specs=(pl.BlockSpec((None, None, CHUNK, gw + 2 * SSD_STATE), lambda dg, i: (dg // 2, dg % 2, sidx(dg, i), 0)),
                   hd, hd),
        out_shape=(_sds((2, 2, t, gw + 2 * SSD_STATE), F32), _sds((4, t, LANE), F32), _sds((4, t, LANE), F32)),
        scratch_shapes=[pltpu.VMEM((gw, SSD_STATE), F32)],
        compiler_params=_cp(("parallel", "arbitrary")),
    )(xbc, xbc, xbc, dt, cs, sprev, dy)


def _ssd_finish_fn(y0, y1, xs, z, dskip, nw):
    y = (y0 + y1 + xs * dskip) * jax.nn.silu(z)
    half = y.shape[1] // 2
    first = lax.broadcasted_iota(jnp.int32, (1, y.shape[1]), 1) < half
    sq = y * y
    m0 = jnp.sum(jnp.where(first, sq, 0.0), axis=1, keepdims=True) / half
    m1 = jnp.sum(jnp.where(first, 0.0, sq), axis=1, keepdims=True) / half
    return y * jnp.where(first, lax.rsqrt(m0 + NORM_EPS), lax.rsqrt(m1 + NORM_EPS)) * nw


def ssd_finish_fwd(y, xbc, p, dskip, nw, *, name):
    t = xbc.shape[0]
    tr = CHUNK
    w = SSD_INNER
    row = pl.BlockSpec((tr, w), lambda i: (i, 0))
    par = pl.BlockSpec((1, w), lambda i: (0, 0))

    def body(y0_ref, y1_ref, x_ref, z_ref, ds_ref, nw_ref, o_ref):
        o_ref[...] = _ssd_finish_fn(y0_ref[...], y1_ref[...], x_ref[...], z_ref[...], ds_ref[...], nw_ref[...]).astype(o_ref.dtype)

    return pl.pallas_call(
        body, name=name, grid=(t // tr,),
        in_specs=[pl.BlockSpec((None, tr, w), lambda i: (0, i, 0)), pl.BlockSpec((None, tr, w), lambda i: (1, i, 0)),
                  row, row, par, par],
        out_specs=row, out_shape=_sds((t, w), BF16), compiler_params=_cp(("parallel",)),
    )(y, y, xbc, p, dskip, nw)


def ssd_finish_bwd(y, xbc, p, dskip, nw, dout, *, name):
    t = xbc.shape[0]
    tr = CHUNK
    w = SSD_INNER
    row = pl.BlockSpec((tr, w), lambda i: (i, 0))
    par = pl.BlockSpec((1, w), lambda i: (0, 0))

    def body(y0_ref, y1_ref, x_ref, z_ref, ds_ref, nw_ref, do_ref, dy_ref, dx_ref, dz_ref, acc_ref):
        _, vjp = jax.vjp(_ssd_finish_fn, y0_ref[...], y1_ref[...], x_ref[...], z_ref[...], ds_ref[...], nw_ref[...])
        dy0, _, dx, dz, dds, dnw = vjp(do_ref[...])
        dy_ref[...] = dy0
        dx_ref[...] = dx
        dz_ref[...] = dz.astype(dz_ref.dtype)

        @pl.when(pl.program_id(0) == 0)
        def _():
            acc_ref[...] = jnp.zeros_like(acc_ref)

        acc_ref[0:1, :] += dds
        acc_ref[1:2, :] += dnw

    return pl.pallas_call(
        body, name=name, grid=(t // tr,),
        in_specs=[pl.BlockSpec((None, tr, w), lambda i: (0, i, 0)), pl.BlockSpec((None, tr, w), lambda i: (1, i, 0)),
                  row, row, par, par, row],
        out_specs=(row, row, row, pl.BlockSpec((8, w), lambda i: (0, 0))),
        out_shape=(_sds((t, w), F32), _sds((t, w), F32), _sds((t, w), BF16), _sds((8, w), F32)),
        compiler_params=_cp(("arbitrary",)),
    )(y, y, xbc, p, dskip, nw, dout)


SGU_GROUPS = 8


def _sgu_fn(us, vs, ws, bs):
    n = SGU_GROUPS * LANE
    vf = [jax.nn.gelu(v) for v in vs]
    mu = sum(jnp.sum(v, axis=1, keepdims=True) for v in vf) / n
    var = sum(jnp.sum(jnp.square(v - mu), axis=1, keepdims=True) for v in vf) / n
    rstd = lax.rsqrt(var + NORM_EPS)
    return tuple(jax.nn.gelu(u) * (_bdot(w, (v - mu) * rstd, ((1,), (0,))) + b) for u, v, w, b in zip(us, vf, ws, bs))


def sgu_fwd(p, w, b, *, name):
    t = p.shape[0]
    wd = SGU_WIDTH

    def body(u_ref, v_ref, w_ref, b_ref, o_ref):
        sl = [slice(g * LANE, (g + 1) * LANE) for g in range(SGU_GROUPS)]
        ys = _sgu_fn([u_ref[:, s] for s in sl], [v_ref[:, s] for s in sl], [w_ref[g] for g in range(SGU_GROUPS)],
                     [b_ref[g] for g in range(SGU_GROUPS)])
        for s, yv in zip(sl, ys):
            o_ref[:, s] = yv.astype(o_ref.dtype)

    return pl.pallas_call(
        body, name=name, grid=(t // CHUNK,),
        in_specs=[pl.BlockSpec((CHUNK, wd), lambda i: (i, U_BLK * LANE // wd)), pl.BlockSpec((CHUNK, wd), lambda i: (i, V_BLK * LANE // wd)),
                  pl.BlockSpec((SGU_GROUPS, CHUNK, CHUNK), lambda i: (0, 0, 0)), pl.BlockSpec((SGU_GROUPS, CHUNK, 1), lambda i: (0, 0, 0))],
        out_specs=pl.BlockSpec((CHUNK, wd), lambda i: (i, 0)),
        out_shape=_sds((t, wd), BF16), compiler_params=_cp(("parallel",)),
    )(p, p, w, b)


def sgu_bwd(p, w, b, dout, *, name):
    t = p.shape[0]
    wd = SGU_WIDTH

    def body(u_ref, v_ref, w_ref, b_ref, do_ref, duv_ref, dw_ref, db_ref):
        sl = [slice(g * LANE, (g + 1) * LANE) for g in range(SGU_GROUPS)]
        _, vjp = jax.vjp(_sgu_fn, [u_ref[:, s] for s in sl], [v_ref[:, s] for s in sl],
                         [w_ref[g] for g in range(SGU_GROUPS)], [b_ref[g] for g in range(SGU_GROUPS)])
        dus, dvs, dws, dbs = vjp(tuple(do_ref[:, s] for s in sl))

        @pl.when(pl.program_id(0) == 0)
        def _():
            dw_ref[...] = jnp.zeros_like(dw_ref)
            db_ref[...] = jnp.zeros_like(db_ref)

        for g, s in enumerate(sl):
            duv_ref[:, s] = dus[g].astype(duv_ref.dtype)
            duv_ref[:, slice(wd + g * LANE, wd + (g + 1) * LANE)] = dvs[g].astype(duv_ref.dtype)
            dw_ref[g] += dws[g]
            db_ref[g] += dbs[g]

    wspec = pl.BlockSpec((SGU_GROUPS, CHUNK, CHUNK), lambda i: (0, 0, 0))
    bspec = pl.BlockSpec((SGU_GROUPS, CHUNK, 1), lambda i: (0, 0, 0))
    return pl.pallas_call(
        body, name=name, grid=(t // CHUNK,),
        in_specs=[pl.BlockSpec((CHUNK, wd), lambda i: (i, U_BLK * LANE // wd)), pl.BlockSpec((CHUNK, wd), lambda i: (i, V_BLK * LANE // wd)),
                  wspec, bspec, pl.BlockSpec((CHUNK, wd), lambda i: (i, 1))],
        out_specs=(pl.BlockSpec((CHUNK, 2 * wd), lambda i: (i, 0)), wspec, bspec),
        out_shape=(_sds((t, 2 * wd), BF16), _sds(w.shape, F32), _sds(b.shape, F32)),
        compiler_params=_cp(("arbitrary",)),
    )(p, p, w, b, dout)


def even_cols_permute(w):
    z, xbc, dt, u, v = jnp.split(w, (1024, 2560, 2592, 3616), axis=-1)
    pad = jnp.zeros(w.shape[:-1] + (EVEN_PAD_COLS - EVEN_COLS,), w.dtype)
    return jnp.concatenate([z, u, v, xbc, dt, pad], axis=-1)


def even_cols_unpermute(w):
    z, u, v, xbc, dt = jnp.split(w[..., :EVEN_COLS], (1024, 2048, 3072, 4608), axis=-1)
    return jnp.concatenate([z, xbc, dt, u, v], axis=-1)


def _dt_cols(p):
    t = p.shape[0]
    d = p[:, DT_BLK * LANE:DT_BLK * LANE + 4 * HEADS_PER_DG].reshape(t, 4, HEADS_PER_DG).transpose(1, 0, 2)
    return jnp.pad(d, ((0, 0), (0, 0), (0, LANE - HEADS_PER_DG)))


def _heads_to_lanes(a):
    return jnp.pad(a.reshape(4, 1, HEADS_PER_DG), ((0, 0), (0, 0), (0, LANE - HEADS_PER_DG)))


def even_params(conv_w, conv_b, dt_bias, a_log, d_skip, ssd_nw, sgu_w, sgu_b):
    wb = jnp.concatenate([conv_w, conv_b[None], jnp.zeros((2, XBC_DIM), F32)], axis=0)
    return dict(wb=wb, dtb=_heads_to_lanes(dt_bias), alog=_heads_to_lanes(a_log),
                dskip=jnp.repeat(d_skip, SSD_HEAD_DIM)[None], ssd_nw=ssd_nw[None], sgu_w=sgu_w, sgu_b=sgu_b[..., None])


def even_mixer_fwd(h, w_in, w_out, li, ep, tc, tag):
    p = mm_nn(h, w_in, li, "n", name=f"{tag}_in")
    xbc = conv_fwd(p, ep["wb"], tc, name=f"{tag}_conv")
    pre = _dt_cols(p)
    dt, cs = ssd_prep_fwd(pre, ep["dtb"], ep["alog"], name=f"{tag}_prep")
    y, sprev = ssd_fwd(xbc, dt, cs, tc, name=f"{tag}_ssd")
    yssd = ssd_finish_fwd(y, xbc, p, ep["dskip"], ep["ssd_nw"], name=f"{tag}_fin")
    ysgu = sgu_fwd(p, ep["sgu_w"], ep["sgu_b"], name=f"{tag}_sgu")
    ymix = jnp.concatenate([yssd, ysgu], axis=1)
    o = mm_nn(ymix, w_out, li, "k", name=f"{tag}_out")
    return o, (p, xbc, pre, dt, cs, y, sprev, ymix)


def even_mixer_bwd(saved, do, h, w_in, w_out, g_in, g_out, li, ep, tc, tag):
    p, xbc, pre, dt, cs, y, sprev, ymix = saved
    t = h.shape[0]
    g_out = mm_tn(ymix, do, g_out, li, "k", name=f"{tag}_out_dw")
    dymix = mm_nt(do, w_out, li, "k", name=f"{tag}_out_dx")
    dy, dxskip, dz, acc_fin = ssd_finish_bwd(y, xbc, p, ep["dskip"], ep["ssd_nw"], dymix, name=f"{tag}_fin_b")
    duv, dsgu_w, dsgu_b = sgu_bwd(p, ep["sgu_w"], ep["sgu_b"], dymix, name=f"{tag}_sgu_b")
    dxg, ddt, dcs = ssd_bwd(xbc, dt, cs, sprev, dy, tc, name=f"{tag}_ssd_b")
    dpre, acc_prep = ssd_prep_bwd(pre, ep["dtb"], ep["alog"], ddt, dcs, name=f"{tag}_prep_b")
    dxbc, dwb = conv_bwd(p, ep["wb"], dxg, dxskip, tc, name=f"{tag}_conv_b")
    ddt_cols = dpre[:, :, :HEADS_PER_DG].transpose(1, 0, 2).reshape(t, 4 * HEADS_PER_DG).astype(BF16)
    ddt_cols = jnp.pad(ddt_cols, ((0, 0), (0, EVEN_PAD_COLS - DT_BLK * LANE - 4 * HEADS_PER_DG)))
    dp = jnp.concatenate([dz, duv, dxbc, ddt_cols], axis=1)
    dh = mm_nt(dp, w_in, li, "n", name=f"{tag}_in_dx")
    g_in = mm_tn(h, dp, g_in, li, "n", name=f"{tag}_in_dw")
    small = dict(
        e_conv_w=dwb[:CONV_K], e_conv_b=dwb[CONV_K],
        e_dt_bias=acc_prep[:, 0, :HEADS_PER_DG].reshape(2, 2 * HEADS_PER_DG),
        e_a_log=acc_prep[:, 1, :HEADS_PER_DG].reshape(2, 2 * HEADS_PER_DG),
        e_d_skip=acc_fin[0].reshape(-1, SSD_HEAD_DIM).sum(axis=1), e_ssd_norm_w=acc_fin[1],
        e_sgu_w=dsgu_w, e_sgu_b=dsgu_b[..., 0])
    return dh, g_in, g_out, small


ATT_HEADS = 16
ATT_KV = 4
Q_BLKS, K_BLKS = ATT_HEADS, ATT_KV


def rope_tables(tc, s):
    quarter = ATT_HEAD_DIM // 4
    pos = jnp.arange(s)
    inv = ROPE_BASE ** (-jnp.arange(quarter, dtype=F32) / quarter)
    a_row = (pos // GRID_W).astype(F32)[:, None] * inv
    a_col = (pos % GRID_W).astype(F32)[:, None] * inv
    cos = jnp.concatenate([jnp.cos(a_row)] * 2 + [jnp.cos(a_col)] * 2, axis=1)
    sin = jnp.concatenate([-jnp.sin(a_row), jnp.sin(a_row), -jnp.sin(a_col), jnp.sin(a_col)], axis=1)
    return (jnp.concatenate([jnp.ones((tc, ATT_HEAD_DIM), F32), cos], axis=0),
            jnp.concatenate([jnp.zeros((tc, ATT_HEAD_DIM), F32), sin], axis=0))


def _swap_halves(x):
    lane = lax.broadcasted_iota(jnp.int32, x.shape, 1)
    return jnp.where(lane % 64 < 32, pltpu.roll(x, 96, 1), pltpu.roll(x, 32, 1))


def rope_fwd(p, cos, sin, *, name):
    t = p.shape[0]
    tr = _pick(t, (1088, 640))
    scale = ATT_HEAD_DIM ** -0.5

    def body(p_ref, c_ref, s_ref, o_ref):
        x = p_ref[...]
        r = x * c_ref[...] + _swap_halves(x) * s_ref[...]
        o_ref[...] = (r * jnp.where(pl.program_id(1) < Q_BLKS, scale, 1.0)).astype(o_ref.dtype)

    tab = pl.BlockSpec((tr, LANE), lambda i, j: (i, 0))
    return pl.pallas_call(
        body, name=name, grid=(t // tr, Q_BLKS + K_BLKS),
        in_specs=[pl.BlockSpec((tr, LANE), lambda i, j: (i, j)), tab, tab],
        out_specs=pl.BlockSpec((tr, LANE), lambda i, j: (i, j)),
        out_shape=_sds((t, (Q_BLKS + K_BLKS) * LANE), BF16), compiler_params=_cp(("parallel", "parallel")),
    )(p, cos, sin)


def rope_bwd(dq, dk, dv, cos, sin, *, name):
    t = dq.shape[0]
    tr = _pick(t, (1088, 640))
    scale = ATT_HEAD_DIM ** -0.5

    def body(dq_ref, dk_ref, dv_ref, c_ref, s_ref, o_ref):
        j = pl.program_id(1)

        def unrot(g):
            return g * c_ref[...] + _swap_halves(g * s_ref[...])

        @pl.when(j < Q_BLKS)
        def _():
            o_ref[...] = (unrot(dq_ref[...]) * scale).astype(o_ref.dtype)

        @pl.when((j >= Q_BLKS) & (j < Q_BLKS + K_BLKS))
        def _():
            o_ref[...] = unrot(dk_ref[...]).astype(o_ref.dtype)

        @pl.when(j >= Q_BLKS + K_BLKS)
        def _():
            o_ref[...] = dv_ref[...].astype(o_ref.dtype)

    tab = pl.BlockSpec((tr, LANE), lambda i, j: (i, 0))
    return pl.pallas_call(
        body, name=name, grid=(t // tr, Q_BLKS + 2 * K_BLKS),
        in_specs=[pl.BlockSpec((tr, LANE), lambda i, j: (i, jnp.minimum(j, Q_BLKS - 1))),
                  pl.BlockSpec((None, tr, LANE), lambda i, j: (jnp.clip(j - Q_BLKS, 0, K_BLKS - 1), i, 0)),
                  pl.BlockSpec((None, tr, LANE), lambda i, j: (jnp.clip(j - Q_BLKS - K_BLKS, 0, K_BLKS - 1), i, 0)), tab, tab],
        out_specs=pl.BlockSpec((tr, LANE), lambda i, j: (i, j)),
        out_shape=_sds((t, (Q_BLKS + 2 * K_BLKS) * LANE), BF16), compiler_params=_cp(("parallel", "parallel")),
    )(dq, dk, dv, cos, sin)


def _attn_tile(qs, kp, kc, kn, vp, vc, vn, kx, vx, sinks, is_lat, has_prev, has_next):
    q = kp.shape[0]
    row = lax.broadcasted_iota(jnp.int32, (q, q), 0)
    col = lax.broadcasted_iota(jnp.int32, (q, q), 1)
    m_prev = (col - row) >= (1 - has_prev) * q
    m_cur = (row - row) >= (1 - is_lat)
    m_next = (row - col) >= (1 - has_next) * q
    lane = lax.broadcasted_iota(jnp.int32, (1, LANE), 1)
    outs = []
    for qh, sinkrow in zip(qs, sinks):
        sink = jnp.sum(jnp.where(lane == 0, sinkrow, 0.0), axis=1, keepdims=True)
        s_p = jnp.where(m_prev, _bdot(qh, kp, ((1,), (1,))), NEG_INF)
        s_c = jnp.where(m_cur, _bdot(qh, kc, ((1,), (1,))), NEG_INF)
        s_n = jnp.where(m_next, _bdot(qh, kn, ((1,), (1,))), NEG_INF)
        s_x = _bdot(qh, kx, ((1,), (1,)))
        mx = [jnp.max(a, axis=1, keepdims=True) for a in (s_p, s_c, s_n, s_x)]
        m = lax.stop_gradient(jnp.maximum(jnp.maximum(jnp.maximum(mx[0], mx[1]), jnp.maximum(mx[2], mx[3])), sink))
        e = [jnp.exp(a - m) for a in (s_p, s_c, s_n, s_x)]
        inv = 1.0 / (sum(jnp.sum(a, axis=1, keepdims=True) for a in e) + jnp.exp(sink - m))
        outs.append(sum(_bdot(a * inv, v, ((1,), (0,))) for a, v in zip(e, (vp, vc, vn, vx))))
    return tuple(outs)


def _attn_specs(t, tc):
    nblk = t // CHUNK
    hw = ATT_GROUP * LANE
    kcol = lambda kv: Q_BLKS + kv
    vcol = lambda kv: Q_BLKS + K_BLKS + kv
    prev = lambda n: jnp.maximum(n - 1, 0)
    nxt = lambda n: jnp.minimum(n + 1, nblk - 1)
    blk = lambda rowf, colf: pl.BlockSpec((CHUNK, LANE), lambda kv, n: (rowf(n), colf(kv)))
    same = lambda n: n
    return [pl.BlockSpec((CHUNK, hw), lambda kv, n: (n, kv)),
            blk(prev, kcol), blk(same, kcol), blk(nxt, kcol), blk(prev, vcol), blk(same, vcol), blk(nxt, vcol),
            pl.BlockSpec((tc, LANE), lambda kv, n: (0, kcol(kv))), pl.BlockSpec((tc, LANE), lambda kv, n: (0, vcol(kv))),
            pl.BlockSpec((None, 8, LANE), lambda kv, n: (kv, 0, 0))]


def _attn_args(refs, n, nct, nblk):
    q_ref, kp, kc, kn, vp, vc, vn, kx, vx, sk = refs
    f = lambda r: r[...].astype(F32)
    qs = [q_ref[:, g * LANE:(g + 1) * LANE].astype(F32) for g in range(ATT_GROUP)]
    sinks = [sk[g:g + 1, :] for g in range(ATT_GROUP)]
    flags = ((n >= nct).astype(jnp.int32), (n >= nct + 1).astype(jnp.int32), ((n >= nct) & (n + 1 < nblk)).astype(jnp.int32))
    return (qs, f(kp), f(kc), f(kn), f(vp), f(vc), f(vn), f(kx), f(vx), sinks), flags


def attn_fwd(qk, p, sink, tc, *, name):
    t = qk.shape[0]
    nblk, nct = t // CHUNK, tc // CHUNK
    hw = ATT_GROUP * LANE

    def body(*refs):
        o_ref = refs[-1]
        args, flags = _attn_args(refs[:-1], pl.program_id(1), nct, nblk)
        outs = _attn_tile(*args, *flags)
        for g, o in enumerate(outs):
            o_ref[:, g * LANE:(g + 1) * LANE] = o.astype(o_ref.dtype)

    return pl.pallas_call(
        body, name=name, grid=(ATT_KV, nblk), in_specs=_attn_specs(t, tc),
        out_specs=pl.BlockSpec((CHUNK, hw), lambda kv, n: (n, kv)),
        out_shape=_sds((t, ATT_HEADS * LANE), BF16), compiler_params=_cp(("parallel", "parallel")),
    )(qk, qk, qk, qk, p, p, p, qk, p, sink)


def attn_bwd(qk, p, sink, do, tc, *, name):
    t = qk.shape[0]
    nblk, nct = t // CHUNK, tc // CHUNK
    hw = ATT_GROUP * LANE

    def body(*refs):
        do_ref, dq_ref, dk_ref, dv_ref, dsk_ref = refs[-5:]
        n = pl.program_id(1)
        args, flags = _attn_args(refs[:-5], n, nct, nblk)
        _, vjp = jax.vjp(lambda *a: _attn_tile(*a, *flags), *args)
        dqs, dkp, dkc, dkn, dvp, dvc, dvn, dkx, dvx, dsinks = vjp(tuple(do_ref[:, g * LANE:(g + 1) * LANE] for g in range(ATT_GROUP)))

        @pl.when(n == 0)
        def _():
            dk_ref[...] = jnp.zeros_like(dk_ref)
            dv_ref[...] = jnp.zeros_like(dv_ref)
            dsk_ref[...] = jnp.zeros_like(dsk_ref)

        for g in range(ATT_GROUP):
            dq_ref[:, g * LANE:(g + 1) * LANE] = dqs[g]
            dsk_ref[g:g + 1, :] += dsinks[g]
        for blk, dkb, dvb in ((jnp.maximum(n - 1, 0), dkp, dvp), (n, dkc, dvc), (jnp.minimum(n + 1, nblk - 1), dkn, dvn)):
            rows = pl.ds(pl.multiple_of(blk * CHUNK, CHUNK), CHUNK)
            dk_ref[rows, :] += dkb
            dv_ref[rows, :] += dvb
        dk_ref[0:tc, :] += dkx
        dv_ref[0:tc, :] += dvx

    kvacc = pl.BlockSpec((None, t, LANE), lambda kv, n: (kv, 0, 0))
    return pl.pallas_call(
        body, name=name, grid=(ATT_KV, nblk),
        in_specs=_attn_specs(t, tc) + [pl.BlockSpec((CHUNK, hw), lambda kv, n: (n, kv))],
        out_specs=(pl.BlockSpec((CHUNK, hw), lambda kv, n: (n, kv)), kvacc, kvacc,
                   pl.BlockSpec((None, 8, LANE), lambda kv, n: (kv, 0, 0))),
        out_shape=(_sds((t, ATT_HEADS * LANE), F32), _sds((ATT_KV, t, LANE), F32), _sds((ATT_KV, t, LANE), F32),
                   _sds((ATT_KV, 8, LANE), F32)),
        compiler_params=_cp(("parallel", "arbitrary")),
    )(qk, qk, qk, qk, p, p, p, qk, p, sink, do)


def sink_rows(sink):
    s = jnp.broadcast_to(sink.reshape(ATT_KV, ATT_GROUP, 1), (ATT_KV, ATT_GROUP, LANE))
    return jnp.pad(s, ((0, 0), (0, 8 - ATT_GROUP), (0, 0)))


def odd_mixer_fwd(h, w_qkv, w_out, li, sink, cos, sin, tc, tag):
    p = mm_nn(h, w_qkv, li, "n", name=f"{tag}_qkv")
    qk = rope_fwd(p, cos, sin, name=f"{tag}_rope")
    att = attn_fwd(qk, p, sink, tc, name=f"{tag}_att")
    o = mm_nn(att, w_out, li, "k", name=f"{tag}_out")
    return o, (p, qk, att)


def odd_mixer_bwd(saved, do, h, w_qkv, w_out, g_qkv, g_out, li, sink, cos, sin, tc, tag):
    p, qk, att = saved
    g_out = mm_tn(att, do, g_out, li, "k", name=f"{tag}_out_dw")
    datt = mm_nt(do, w_out, li, "k", name=f"{tag}_out_dx")
    dq, dk, dv, dsink = attn_bwd(qk, p, sink, datt, tc, name=f"{tag}_att_b")
    dp = rope_bwd(dq, dk, dv, cos, sin, name=f"{tag}_rope_b")
    dh = mm_nt(dp, w_qkv, li, "n", name=f"{tag}_qkv_dx")
    g_qkv = mm_tn(h, dp, g_qkv, li, "n", name=f"{tag}_qkv_dw")
    return dh, g_qkv, g_out, dict(o_sink=dsink[:, :ATT_GROUP, 0].reshape(-1))


ANY = pl.BlockSpec(memory_space=pl.ANY)


def _place():
    return lax.axis_index("x"), lax.axis_index("y"), lax.axis_index("c")


DMA_PIECES = 16


def _pieces(shape):
    if len(shape) < 2:
        return [()]
    lead, k = shape[:-2], shape[-2]
    split = 1
    while math.prod(lead) * split < DMA_PIECES and k % (2 * split) == 0 and (k // (2 * split)) % 16 == 0:
        split *= 2
    rows = k // split
    out = []
    for li in itertools.product(*[range(n) for n in lead]):
        out += [li + (pl.ds(q * rows, rows),) for q in range(split)]
    return out


def _start_pieces(make, src, dst):
    for idx in _pieces(src.shape):
        make(src.at[idx] if idx else src, dst.at[idx] if idx else dst).start()


def allgather8(blk, *, name):
    def body(x_ref, out_ref, send_sems, recv_sems, local_sem):
        x, y, c = _place()
        me, sibling = (x, y, c), (x, y, 1 - c)
        chips = [(1 - x, y), (x, 1 - y), (1 - x, 1 - y)]

        def slot(px, py, pc):
            return out_ref.at[4 * px + 2 * py + pc]

        def remote(k, to):
            return lambda src, dst: pltpu.make_async_remote_copy(
                src_ref=src, dst_ref=dst, send_sem=send_sems.at[k], recv_sem=recv_sems.at[k], device_id=to, device_id_type=MESH_ID)

        def local(src, dst):
            return pltpu.make_async_copy(src, dst, local_sem)

        _start_pieces(local, x_ref, slot(*me))
        _start_pieces(remote(0, sibling), x_ref, slot(*me))
        for j, chip in enumerate(chips):
            remote(1 + j, (*chip, c))(x_ref, slot(*me)).start()
        for j, chip in enumerate(chips):
            blk = slot(*chip, c)
            remote(1 + j, me)(blk, blk).wait_recv()
            _start_pieces(remote(4 + j, sibling), blk, blk)
        remote(0, me)(slot(*sibling), slot(*sibling)).wait_recv()
        for j, chip in enumerate(chips):
            blk = slot(*chip, 1 - c)
            remote(4 + j, me)(blk, blk).wait_recv()
        remote(0, sibling)(x_ref, slot(*me)).wait_send()
        for j, chip in enumerate(chips):
            remote(1 + j, (*chip, c))(x_ref, slot(*me)).wait_send()
            remote(4 + j, sibling)(slot(*chip, c), slot(*chip, c)).wait_send()
        local(x_ref, slot(*me)).wait()

    return pl.pallas_call(
        body, name=name, out_shape=_sds((N_DEV,) + blk.shape, blk.dtype), in_specs=[ANY], out_specs=ANY,
        scratch_shapes=[pltpu.SemaphoreType.DMA((7,)), pltpu.SemaphoreType.DMA((7,)), pltpu.SemaphoreType.DMA],
        compiler_params=pltpu.CompilerParams(has_side_effects=True),
    )(blk)


def rs_sibling(g, *, name):
    nchip, nl, kd, nd = g.shape
    lh = nl // 2

    def body(g_ref, out_ref, send_sem, recv_sem):
        x, y, c = _place()

        def remote(src, dst):
            return pltpu.make_async_remote_copy(src_ref=src, dst_ref=dst, send_sem=send_sem, recv_sem=recv_sem,
                                                device_id=(x, y, 1 - c), device_id_type=MESH_ID)

        give = g_ref.at[:, pl.ds((1 - c) * lh, lh)]
        _start_pieces(remote, give, out_ref)
        remote(give, out_ref).wait()

    return pl.pallas_call(
        body, name=name, out_shape=_sds((nchip, lh, kd, nd), g.dtype), in_specs=[ANY], out_specs=ANY,
        scratch_shapes=[pltpu.SemaphoreType.DMA, pltpu.SemaphoreType.DMA],
        compiler_params=pltpu.CompilerParams(has_side_effects=True),
    )(g)


def rs_chips(h, *, name):
    def body(h_ref, out_ref, send_sems, recv_sems):
        x, y, c = _place()
        copies = []
        for r in (1, 2, 3):
            px, py = (1 - x if r & 2 else x), (1 - y if r & 1 else y)
            copies.append(pltpu.make_async_remote_copy(
                src_ref=h_ref.at[2 * px + py], dst_ref=out_ref.at[r - 1], send_sem=send_sems.at[r - 1],
                recv_sem=recv_sems.at[r - 1], device_id=(px, py, c), device_id_type=MESH_ID))
        for cp in copies:
            cp.start()
        for cp in copies:
            cp.wait()

    return pl.pallas_call(
        body, name=name, out_shape=_sds((3,) + h.shape[1:], h.dtype), in_specs=[ANY], out_specs=ANY,
        scratch_shapes=[pltpu.SemaphoreType.DMA((3,)), pltpu.SemaphoreType.DMA((3,))],
        compiler_params=pltpu.CompilerParams(has_side_effects=True),
    )(h)


def sibling_swap(half, *, name):
    def body(h_ref, out_ref, send_sem, recv_sem):
        x, y, c = _place()

        def remote(src, dst):
            return pltpu.make_async_remote_copy(src_ref=src, dst_ref=dst, send_sem=send_sem, recv_sem=recv_sem,
                                                device_id=(x, y, 1 - c), device_id_type=MESH_ID)

        _start_pieces(remote, h_ref, out_ref)
        remote(h_ref, out_ref).wait()

    return pl.pallas_call(
        body, name=name, out_shape=_sds(half.shape, half.dtype), in_specs=[ANY], out_specs=ANY,
        scratch_shapes=[pltpu.SemaphoreType.DMA, pltpu.SemaphoreType.DMA],
        compiler_params=pltpu.CompilerParams(has_side_effects=True),
    )(half)


def _row_block(kd, nd):
    return _pick(kd, (max(32, (1 << 19) // nd // 32 * 32),))


def add_kept_half(g, recv, core, *, name):
    nchip, nl, kd, nd = g.shape
    lh = nl // 2
    tk = _row_block(kd, nd)

    def body(c_ref, g_ref, r_ref, o_ref):
        del c_ref
        o_ref[...] = (g_ref[...].astype(F32) + r_ref[...].astype(F32)).astype(o_ref.dtype)

    blk = lambda f: pl.BlockSpec((None, None, tk, nd), f)
    return pl.pallas_call(
        body, name=name, out_shape=_sds((nchip, lh, kd, nd), BF16),
        grid_spec=pltpu.PrefetchScalarGridSpec(
            num_scalar_prefetch=1, grid=(nchip, lh, kd // tk),
            in_specs=[blk(lambda j, l, i, c_ref: (j, c_ref[0] * lh + l, i, 0)), blk(lambda j, l, i, c_ref: (j, l, i, 0))],
            out_specs=blk(lambda j, l, i, c_ref: (j, l, i, 0))),
        compiler_params=_cp(("parallel", "parallel", "parallel")),
    )(core, g, recv)


def add_chip_parts(h, parts, chip, *, name):
    _, lh, kd, nd = h.shape
    tk = _row_block(kd, nd)

    def body(k_ref, h_ref, p0, p1, p2, o_ref):
        del k_ref
        o_ref[...] = h_ref[...].astype(F32) + p0[...].astype(F32) + p1[...].astype(F32) + p2[...].astype(F32)

    blk = lambda f: pl.BlockSpec((None, None, tk, nd), f)
    part = lambda r: blk(functools.partial(lambda r_, l, i, k_ref: (r_, l, i, 0), r))
    return pl.pallas_call(
        body, name=name, out_shape=_sds((lh, kd, nd), F32),
        grid_spec=pltpu.PrefetchScalarGridSpec(
            num_scalar_prefetch=1, grid=(lh, kd // tk),
            in_specs=[blk(lambda l, i, k_ref: (k_ref[0], l, i, 0)), part(0), part(1), part(2)],
            out_specs=pl.BlockSpec((None, tk, nd), lambda l, i, k_ref: (l, i, 0))),
        compiler_params=_cp(("parallel", "parallel")),
    )(chip, h, parts, parts, parts)


def sum_slots(a, out_dtype, *, name):
    n = a.shape[0]
    cols = a.shape[-1]
    a3 = a.reshape(n, -1, cols)
    rows = a3.shape[1]
    tr = _pick(rows, (max(32, (1 << 19) // cols // 32 * 32),))

    def body(*refs):
        acc = refs[0][...].astype(F32)
        for r in refs[1:n]:
            acc = acc + r[...].astype(F32)
        refs[n][...] = acc.astype(out_dtype)

    return pl.pallas_call(
        body, name=name, grid=(rows // tr,),
        in_specs=[pl.BlockSpec((None, tr, cols), functools.partial(lambda j, i: (j, i, 0), j)) for j in range(n)],
        out_specs=pl.BlockSpec((tr, cols), lambda i: (i, 0)),
        out_shape=_sds((rows, cols), out_dtype), compiler_params=_cp(("parallel",)),
    )(*([a3] * n)).reshape(a.shape[1:])


def reduce_scatter_grad(g, tag):
    nchip, nl, kd, nd = g.shape
    xi, yi, ci = _place()
    core = jnp.reshape(ci, (1,)).astype(jnp.int32)
    chip = jnp.reshape(2 * xi + yi, (1,)).astype(jnp.int32)
    chip_sum = add_kept_half(g, rs_sibling(g, name=f"{tag}_rs1"), core, name=f"{tag}_add1")
    half = add_chip_parts(chip_sum, rs_chips(chip_sum, name=f"{tag}_rs2"), chip, name=f"{tag}_add2")
    other = sibling_swap(half, name=f"{tag}_rs3")
    first, second = jnp.where(ci == 0, half, other), jnp.where(ci == 0, other, half)
    return jnp.concatenate([first, second], axis=0)


def gather_weight(w, tag):
    nl = w.shape[0]
    lh = nl // 2
    half = lax.dynamic_slice_in_dim(w, lax.axis_index("c") * lh, lh, axis=0).astype(BF16)
    return allgather8(half, name=f"{tag}_ag").reshape((N_CHIP, nl) + w.shape[1:])


def mod_fwd(c16, w_mod, *, name):
    nl, d, ns = w_mod.shape
    tn = _pick(ns, (512,))

    def body(c_ref, w_ref, o_ref):
        o_ref[...] = jnp.dot(jax.nn.silu(c_ref[...]), w_ref[...], precision=HI, preferred_element_type=F32)

    return pl.pallas_call(
        body, name=name, grid=(nl, ns // tn),
        in_specs=[pl.BlockSpec((16, d), lambda l, j: (0, 0)), pl.BlockSpec((None, d, tn), lambda l, j: (l, 0, j))],
        out_specs=pl.BlockSpec((None, 16, tn), lambda l, j: (l, 0, j)),
        out_shape=_sds((nl, 16, ns), F32), compiler_params=_cp(("parallel", "parallel")),
    )(c16, w_mod)


def mod_bwd_w(c16, dm, *, name):
    nl, _, ns = dm.shape
    d = c16.shape[1]
    tn = _pick(ns, (512,))

    def body(c_ref, dm_ref, o_ref):
        o_ref[...] = lax.dot_general(jax.nn.silu(c_ref[...]), dm_ref[...], (((0,), (0,)), ((), ())), precision=HI,
                                     preferred_element_type=F32)

    return pl.pallas_call(
        body, name=name, grid=(nl, ns // tn),
        in_specs=[pl.BlockSpec((16, d), lambda l, j: (0, 0)), pl.BlockSpec((None, 16, tn), lambda l, j: (l, 0, j))],
        out_specs=pl.BlockSpec((None, d, tn), lambda l, j: (l, 0, j)),
        out_shape=_sds((nl, d, ns), F32), compiler_params=_cp(("parallel", "parallel")),
    )(c16, dm)


def mod_bwd_s(dm, w_mod, *, name):
    nl, d, ns = w_mod.shape
    td = _pick(d, (512,))

    def body(dm_ref, w_ref, o_ref):
        part = lax.dot_general(dm_ref[...], w_ref[...], (((1,), (1,)), ((), ())), precision=HI, preferred_element_type=F32)
        rowsum = jnp.sum(part[8:16], axis=0, keepdims=True)

        @pl.when(pl.program_id(1) == 0)
        def _():
            o_ref[...] = jnp.zeros_like(o_ref)

        o_ref[...] += jnp.broadcast_to(rowsum, o_ref.shape)

    return pl.pallas_call(
        body, name=name, grid=(d // td, nl),
        in_specs=[pl.BlockSpec((None, 16, ns), lambda i, l: (l, 0, 0)), pl.BlockSpec((None, td, ns), lambda i, l: (l, i, 0))],
        out_specs=pl.BlockSpec((8, td), lambda i, l: (0, i)),
        out_shape=_sds((8, d), F32), compiler_params=_cp(("parallel", "arbitrary")),
    )(dm, w_mod)


def colsum16(dm, *, name):
    nl, _, n = dm.shape
    tn = _pick(n, (2048,))

    def body(dm_ref, o_ref):
        o_ref[...] = jnp.broadcast_to(jnp.sum(dm_ref[...], axis=0, keepdims=True), o_ref.shape)

    return pl.pallas_call(
        body, name=name, grid=(nl, n // tn),
        in_specs=[pl.BlockSpec((None, 16, tn), lambda l, j: (l, 0, j))],
        out_specs=pl.BlockSpec((None, 8, tn), lambda l, j: (l, 0, j)),
        out_shape=_sds((nl, 8, n), F32), compiler_params=_cp(("parallel", "parallel")),
    )(dm)


def silu_grad_mul(g, c, *, name):
    def body(g_ref, c_ref, o_ref):
        _, vjp = jax.vjp(jax.nn.silu, c_ref[...])
        o_ref[...] = vjp(g_ref[...])[0]

    return pl.pallas_call(body, name=name, out_shape=_sds(g.shape, F32))(g, c)


def adamw(w, g, m, v, *, name):
    shape = w.shape
    cols = shape[-1] if len(shape) > 1 else LANE
    flat = [a.reshape(-1, cols) for a in (w, g, m, v)]
    rows = flat[0].shape[0]
    tr = _pick(rows, (max(8, (1 << 18) // cols // 8 * 8),)) if rows % 8 == 0 else rows
    c1 = 1.0 - ADAM_B1 ** ADAM_STEP
    c2 = 1.0 - ADAM_B2 ** ADAM_STEP

    def body(w_ref, g_ref, m_ref, v_ref, d_ref, nm_ref, nv_ref):
        gv = g_ref[...]
        nm = ADAM_B1 * m_ref[...] + (1.0 - ADAM_B1) * gv
        nv = ADAM_B2 * v_ref[...] + (1.0 - ADAM_B2) * (gv * gv)
        d_ref[...] = -ADAM_LR * ((nm / c1) / (jnp.sqrt(nv / c2) + ADAM_EPS) + ADAM_WD * w_ref[...])
        nm_ref[...] = nm
        nv_ref[...] = nv

    blk = pl.BlockSpec((tr, cols), lambda i: (i, 0))
    outs = pl.pallas_call(
        body, name=name, grid=(rows // tr,), in_specs=[blk] * 4, out_specs=(blk,) * 3,
        out_shape=(_sds((rows, cols), F32),) * 3, compiler_params=_cp(("parallel",)),
    )(*flat)
    return tuple(o.reshape(shape) for o in outs)


PACK_ELEMS = LANE * LANE


def _pack(arrs):
    flat = jnp.concatenate([a.reshape(-1).astype(F32) for a in arrs])
    return jnp.pad(flat, (0, (-flat.shape[0]) % PACK_ELEMS)).reshape(-1, LANE)


def _unpack(packed, shapes):
    flat = packed.reshape(-1)
    out, pos = [], 0
    for s in shapes:
        n = math.prod(s)
        out.append(flat[pos:pos + n].reshape(s))
        pos += n
    return out


def _chip_cols(a, chip, width):
    return lax.dynamic_slice_in_dim(a, chip * width, width, axis=a.ndim - 1)


def kernel(x, c, ctx, c_ctx, w_mod, b_mod, norm_w, w_ffn_in, w_ffn_out, e_w_in, e_conv_w, e_conv_b, e_dt_bias, e_a_log, e_d_skip, e_ssd_norm_w, e_sgu_w, e_sgu_b, e_w_out, o_w_qkv, o_sink, o_w_out, loss_target, m_c_ctx, m_w_mod, m_b_mod, m_norm_w, m_w_ffn_in, m_w_ffn_out, m_e_w_in, m_e_conv_w, m_e_conv_b, m_e_dt_bias, m_e_a_log, m_e_d_skip, m_e_ssd_norm_w, m_e_sgu_w, m_e_sgu_b, m_e_w_out, m_o_w_qkv, m_o_sink, m_o_w_out, v_c_ctx, v_w_mod, v_b_mod, v_norm_w, v_w_ffn_in, v_w_ffn_out, v_e_w_in, v_e_conv_w, v_e_conv_b, v_e_dt_bias, v_e_a_log, v_e_d_skip, v_e_ssd_norm_w, v_e_sgu_w, v_e_sgu_b, v_e_w_out, v_o_w_qkv, v_o_sink, v_o_w_out):
    xi, yi, ci = _place()
    chip = 2 * xi + yi
    me = 2 * chip + ci
    s, d = x.shape[1:]
    tc = ctx.shape[1]
    depth = w_mod.shape[0]
    n_even = e_w_in.shape[0]
    dq = norm_w.shape[-1]
    cq = e_conv_w.shape[-1]
    ns = w_mod.shape[-1]

    gath = allgather8(_pack([c, norm_w, e_conv_w]), name="ag_small").reshape(N_DEV, -1)
    c_all = gath[:, :d]
    per_chip = [_unpack(gath[2 * k, d:], [norm_w.shape, e_conv_w.shape]) for k in range(N_CHIP)]
    nw_full = jnp.concatenate([pc[0] for pc in per_chip], axis=-1)
    convw_full = jnp.concatenate([pc[1] for pc in per_chip], axis=-1)
    c16 = jnp.concatenate([c_all, jnp.broadcast_to(c_ctx[None], (8, d))], axis=0)

    mod_g = allgather8(mod_fwd(c16, w_mod, name="mod_fwd"), name="ag_mod")
    mod_all = jnp.concatenate([mod_g[2 * k] for k in range(N_CHIP)], axis=-1) + b_mod[:, None, :]
    mod_rows = jnp.stack([mod_all[:, 8], lax.dynamic_index_in_dim(mod_all, me, axis=1, keepdims=False)], axis=1)
    modtab = jnp.pad(mod_rows.reshape(depth, 2, 6, d), ((0, 0), (0, 0), (0, 2), (0, 0)))

    wg_ffn_in = gather_weight(w_ffn_in, "w_ffn_in")
    wg_ffn_out = gather_weight(w_ffn_out, "w_ffn_out")
    wg_e_out = gather_weight(e_w_out, "e_w_out")
    wg_qkv = gather_weight(o_w_qkv, "o_w_qkv")
    wg_o_out = gather_weight(o_w_out, "o_w_out")
    e_in_g = gather_weight(e_w_in, "e_w_in")
    wg_e_in = even_cols_permute(jnp.moveaxis(e_in_g, 0, 2).reshape(n_even, d, -1))[None]

    eps_ = [even_params(convw_full[i], e_conv_b[i], e_dt_bias[i], e_a_log[i], e_d_skip[i], e_ssd_norm_w[i], e_sgu_w[i], e_sgu_b[i])
            for i in range(n_even)]
    sinks = [sink_rows(o_sink[i]) for i in range(o_sink.shape[0])]
    cos, sin = rope_tables(tc, s)

    u = jnp.concatenate([ctx[0], x[0]], axis=0)
    saved = []
    for l in range(depth):
        mt, nw = modtab[l], nw_full[l]
        h1 = norm_mod_fwd(u, nw[0], mt, tc, 0, name=f"L{l}_norm1")
        if l % 2 == 0:
            o, ms = even_mixer_fwd(h1, wg_e_in, wg_e_out, l // 2, eps_[l // 2], tc, f"L{l}_mix")
        else:
            o, ms = odd_mixer_fwd(h1, wg_qkv, wg_o_out, l // 2, sinks[l // 2], cos, sin, tc, f"L{l}_mix")
        u1 = resid_fwd(u, o, nw[1], mt, tc, 0, name=f"L{l}_res1")
        h2 = norm_mod_fwd(u1, nw[2], mt, tc, 1, name=f"L{l}_norm2")
        p = mm_nn(h2, wg_ffn_in, l, "n", name=f"L{l}_ffn_in")
        a = swiglu_fwd(p, name=f"L{l}_swiglu")
        f = mm_nn(a, wg_ffn_out, l, "k", name=f"L{l}_ffn_out")
        saved.append((u, h1, ms, o, u1, h2, p, a, f))
        u = resid_fwd(u1, f, nw[3], mt, tc, 1, name=f"L{l}_res2")
    loss_part, du = loss_fwd_bwd(u, loss_target[0], tc, name="loss")
    loss = lax.psum(loss_part[0, 0], ("x", "y", "c"))

    g_ffn_in = jnp.zeros(wg_ffn_in.shape, BF16)
    g_ffn_out = jnp.zeros(wg_ffn_out.shape, BF16)
    g_e_in = jnp.zeros(wg_e_in.shape, BF16)
    g_e_out = jnp.zeros(wg_e_out.shape, BF16)
    g_qkv = jnp.zeros(wg_qkv.shape, BF16)
    g_o_out = jnp.zeros(wg_o_out.shape, BF16)
    d_nw, d_mt = [None] * depth, [None] * depth
    small_e, small_o = [None] * n_even, [None] * (depth - n_even)
    for l in reversed(range(depth)):
        mt, nw = modtab[l], nw_full[l]
        u0, h1, ms, o, u1, h2, p, a, f = saved[l]
        df, acc3 = resid_bwd(f, nw[3], mt, du, tc, 1, name=f"L{l}_res2_b")
        g_ffn_out = mm_tn(a, df, g_ffn_out, l, "k", name=f"L{l}_ffn_out_dw")
        da = mm_nt(df, wg_ffn_out, l, "k", name=f"L{l}_ffn_out_dx")
        dp = swiglu_bwd(p, da, name=f"L{l}_swiglu_b")
        dh2 = mm_nt(dp, wg_ffn_in, l, "n", name=f"L{l}_ffn_in_dx")
        g_ffn_in = mm_tn(h2, dp, g_ffn_in, l, "n", name=f"L{l}_ffn_in_dw")
        du1, acc2 = norm_mod_bwd(u1, nw[2], mt, dh2, du, tc, 1, name=f"L{l}_norm2_b")
        do, acc1 = resid_bwd(o, nw[1], mt, du1, tc, 0, name=f"L{l}_res1_b")
        if l % 2 == 0:
            dh1, g_e_in, g_e_out, small_e[l // 2] = even_mixer_bwd(ms, do, h1, wg_e_in, wg_e_out, g_e_in, g_e_out, l // 2,
                                                                  eps_[l // 2], tc, f"L{l}_mix")
        else:
            dh1, g_qkv, g_o_out, small_o[l // 2] = odd_mixer_bwd(ms, do, h1, wg_qkv, wg_o_out, g_qkv, g_o_out, l // 2,
                                                                 sinks[l // 2], cos, sin, tc, f"L{l}_mix")
        du, acc0 = norm_mod_bwd(u0, nw[0], mt, dh1, du1, tc, 0, name=f"L{l}_norm1_b")
        d_nw[l] = jnp.stack([acc[0, 0] + acc[1, 0] for acc in (acc0, acc1, acc2, acc3)])
        d_mt[l] = jnp.stack([acc0[:, 1], acc0[:, 2], acc1[:, 1], acc2[:, 1], acc2[:, 2], acc3[:, 1]], axis=1)
    grad_x = du[tc:][None]

    dmt_g = allgather8(jnp.pad(jnp.stack(d_mt), ((0, 0), (0, 0), (0, 2), (0, 0))), name="ag_dmod")[:, :, :, :6]
    dm16 = jnp.concatenate([dmt_g[:, :, 1].transpose(1, 0, 2, 3).reshape(depth, N_DEV, 6 * d),
                            dmt_g[:, :, 0].transpose(1, 0, 2, 3).reshape(depth, N_DEV, 6 * d)], axis=1)
    dm_sh = _chip_cols(dm16, chip, ns)
    grad_w_mod = mod_bwd_w(c16, dm_sh, name="mod_bwd_w")
    grad_b_mod = colsum16(dm16, name="mod_bwd_b")[:, 0]
    ds_cc = mod_bwd_s(dm_sh, w_mod, name="mod_bwd_s")[0]

    stack_e = lambda key: jnp.stack([se[key] for se in small_e])
    small_names = ["e_conv_b", "e_dt_bias", "e_a_log", "e_d_skip", "e_ssd_norm_w", "e_sgu_w", "e_sgu_b"]
    small_parts = [jnp.stack(d_nw), stack_e("e_conv_w")] + [stack_e(k) for k in small_names]
    small_parts += [jnp.stack([so["o_sink"] for so in small_o]), 0.5 * ds_cc]
    small_shapes = [a.shape for a in small_parts]
    small_sum = sum_slots(allgather8(_pack(small_parts), name="ag_small_grads"), F32, name="small_grads_sum")
    (g_nw, g_convw, g_convb, g_dtb, g_alog, g_dskip, g_ssdnw, g_sguw, g_sgub, g_sink, g_scc) = _unpack(small_sum, small_shapes)
    grad_c_ctx = silu_grad_mul(jnp.broadcast_to(g_scc[None], (8, d)), jnp.broadcast_to(c_ctx[None], (8, d)), name="c_ctx_grad")[0]
    grads = dict(
        c_ctx=grad_c_ctx, w_mod=grad_w_mod, b_mod=grad_b_mod, norm_w=_chip_cols(g_nw, chip, dq),
        e_conv_w=_chip_cols(g_convw, chip, cq), e_conv_b=g_convb, e_dt_bias=g_dtb.reshape(e_dt_bias.shape),
        e_a_log=g_alog.reshape(e_a_log.shape), e_d_skip=g_dskip, e_ssd_norm_w=g_ssdnw, e_sgu_w=g_sguw, e_sgu_b=g_sgub,
        o_sink=g_sink)

    grads["w_ffn_in"] = reduce_scatter_grad(g_ffn_in, "g_ffn_in")
    grads["w_ffn_out"] = reduce_scatter_grad(g_ffn_out, "g_ffn_out")
    g_e_in_c = jnp.moveaxis(even_cols_unpermute(g_e_in[0]).reshape(n_even, d, N_CHIP, -1), 2, 0)
    grads["e_w_in"] = reduce_scatter_grad(g_e_in_c, "g_e_in")
    grads["e_w_out"] = reduce_scatter_grad(g_e_out, "g_e_out")
    grads["o_w_qkv"] = reduce_scatter_grad(g_qkv, "g_qkv")
    grads["o_w_out"] = reduce_scatter_grad(g_o_out, "g_o_out")

    weights = dict(c_ctx=c_ctx, w_mod=w_mod, b_mod=b_mod, norm_w=norm_w, w_ffn_in=w_ffn_in, w_ffn_out=w_ffn_out, e_w_in=e_w_in,
                   e_conv_w=e_conv_w, e_conv_b=e_conv_b, e_dt_bias=e_dt_bias, e_a_log=e_a_log, e_d_skip=e_d_skip,
                   e_ssd_norm_w=e_ssd_norm_w, e_sgu_w=e_sgu_w, e_sgu_b=e_sgu_b, e_w_out=e_w_out, o_w_qkv=o_w_qkv, o_sink=o_sink,
                   o_w_out=o_w_out)
    ms_ = dict(c_ctx=m_c_ctx, w_mod=m_w_mod, b_mod=m_b_mod, norm_w=m_norm_w, w_ffn_in=m_w_ffn_in, w_ffn_out=m_w_ffn_out,
               e_w_in=m_e_w_in, e_conv_w=m_e_conv_w, e_conv_b=m_e_conv_b, e_dt_bias=m_e_dt_bias, e_a_log=m_e_a_log,
               e_d_skip=m_e_d_skip, e_ssd_norm_w=m_e_ssd_norm_w, e_sgu_w=m_e_sgu_w, e_sgu_b=m_e_sgu_b, e_w_out=m_e_w_out,
               o_w_qkv=m_o_w_qkv, o_sink=m_o_sink, o_w_out=m_o_w_out)
    vs_ = dict(c_ctx=v_c_ctx, w_mod=v_w_mod, b_mod=v_b_mod, norm_w=v_norm_w, w_ffn_in=v_w_ffn_in, w_ffn_out=v_w_ffn_out,
               e_w_in=v_e_w_in, e_conv_w=v_e_conv_w, e_conv_b=v_e_conv_b, e_dt_bias=v_e_dt_bias, e_a_log=v_e_a_log,
               e_d_skip=v_e_d_skip, e_ssd_norm_w=v_e_ssd_norm_w, e_sgu_w=v_e_sgu_w, e_sgu_b=v_e_sgu_b, e_w_out=v_e_w_out,
               o_w_qkv=v_o_w_qkv, o_sink=v_o_sink, o_w_out=v_o_w_out)
    names = list(weights)
    big = ("w_mod", "w_ffn_in", "w_ffn_out", "e_w_in", "e_w_out", "o_w_qkv", "o_w_out")
    small = [n for n in names if n not in big]
    delta, new_m, new_v = {}, {}, {}
    for n in big:
        delta[n], new_m[n], new_v[n] = adamw(weights[n], grads[n], ms_[n], vs_[n], name=f"adamw_{n}")
    packed = adamw(*[_pack([tab[n] for n in small]) for tab in (weights, grads, ms_, vs_)], name="adamw_small")
    shapes = [weights[n].shape for n in small]
    for tab, pk in zip((delta, new_m, new_v), packed):
        for n, val in zip(small, _unpack(pk, shapes)):
            tab[n] = val
    return (loss, grad_x, *[grads[n] for n in names], *[delta[n] for n in names], *[new_m[n] for n in names],
            *[new_v[n] for n in names])
```

```python
import functools
import itertools
import math

import jax
import jax.numpy as jnp
from jax import lax
from jax.experimental import pallas as pl
from jax.experimental.pallas import tpu as pltpu

F32 = jnp.float32
BF16 = jnp.bfloat16
HI = lax.Precision.HIGHEST

NORM_EPS = 1e-6
SSD_HEAD_DIM = 64
SSD_STATE = 128
CHUNK = 128
CONV_K = 5
ATT_HEAD_DIM = 128
ATT_GROUP = 4
ROPE_BASE = 10000.0
GRID_W = 64
NEG_INF = -1e30
ADAM_LR, ADAM_B1, ADAM_B2, ADAM_EPS, ADAM_WD, ADAM_STEP = 0.001, 0.9, 0.999, 1e-08, 0.01, 10

LANE = 128
VMEM_LIMIT = 56 * 1024 * 1024
MESH_ID = pl.DeviceIdType.MESH
N_DEV = 8
N_CHIP = 4


def _cp(sem=None):
    return pltpu.CompilerParams(dimension_semantics=sem, vmem_limit_bytes=VMEM_LIMIT)


def _sds(shape, dtype):
    return jax.ShapeDtypeStruct(tuple(shape), dtype)


def _pick(n, cands):
    for c in cands:
        if n % c == 0:
            return c
    for step in (LANE, 16, 8):
        for c in range(min(n, cands[0]) // step * step, 0, -step):
            if n % c == 0:
                return c
    raise ValueError((n, cands))


def _w_index(blocked, layer, per_block_k, per_block_n):
    def idx(kblk, nblk):
        if blocked == "n":
            return (nblk // per_block_n, layer, kblk, nblk % per_block_n)
        return (kblk // per_block_k, layer, kblk % per_block_k, nblk)
    return idx


def mm_nn(a, w, layer, blocked, *, name, out_dtype=F32, tm=None, tn=None, tk=None):
    m, k_total = a.shape
    cb, _, kd, nd = w.shape
    n_total = nd * cb if blocked == "n" else nd
    assert k_total == (kd if blocked == "n" else kd * cb)
    tm = tm or _pick(m, (1088, 192))
    tn = tn or _pick(nd, (1408, 768, 512))
    tk = tk or _pick(kd, (2048, 1408, 512))
    nk = k_total // tk
    widx = _w_index(blocked, layer, kd // tk, nd // tn)

    def body(a_ref, w_ref, o_ref, acc_ref):
        kk = pl.program_id(2)
        part = jnp.dot(a_ref[...].astype(BF16), w_ref[...].astype(BF16), preferred_element_type=F32)

        @pl.when(kk == 0)
        def _():
            acc_ref[...] = part

        @pl.when(kk > 0)
        def _():
            acc_ref[...] += part

        @pl.when(kk == nk - 1)
        def _():
            o_ref[...] = acc_ref[...].astype(o_ref.dtype)

    return pl.pallas_call(
        body, name=name, grid=(m // tm, n_total // tn, nk),
        in_specs=[pl.BlockSpec((tm, tk), lambda i, j, k: (i, k)),
                  pl.BlockSpec((None, None, tk, tn), lambda i, j, k: widx(k, j))],
        out_specs=pl.BlockSpec((tm, tn), lambda i, j, k: (i, j)),
        out_shape=_sds((m, n_total), out_dtype),
        scratch_shapes=[pltpu.VMEM((tm, tn), F32)],
        compiler_params=_cp(("parallel", "parallel", "arbitrary")),
    )(a, w)


def mm_nt(dy, w, layer, blocked, *, name, out_dtype=F32, tm=None, tn=None, tk=None):
    m, n_total = dy.shape
    cb, _, kd, nd = w.shape
    k_total = kd if blocked == "n" else kd * cb
    assert n_total == (nd * cb if blocked == "n" else nd)
    tm = tm or _pick(m, (1088, 192))
    tn = tn or _pick(kd, (1024, 1408, 512))
    tk = tk or _pick(nd, (1408, 1024, 768))
    nk = n_total // tk
    widx = _w_index(blocked, layer, kd // tn, nd // tk)

    def body(a_ref, w_ref, o_ref, acc_ref):
        kk = pl.program_id(2)
        part = lax.dot_general(a_ref[...].astype(BF16), w_ref[...].astype(BF16), (((1,), (1,)), ((), ())),
                               preferred_element_type=F32)

        @pl.when(kk == 0)
        def _():
            acc_ref[...] = part

        @pl.when(kk > 0)
        def _():
            acc_ref[...] += part

        @pl.when(kk == nk - 1)
        def _():
            o_ref[...] = acc_ref[...].astype(o_ref.dtype)

    return pl.pallas_call(
        body, name=name, grid=(m // tm, k_total // tn, nk),
        in_specs=[pl.BlockSpec((tm, tk), lambda i, j, k: (i, k)),
                  pl.BlockSpec((None, None, tn, tk), lambda i, j, k: widx(j, k))],
        out_specs=pl.BlockSpec((tm, tn), lambda i, j, k: (i, j)),
        out_shape=_sds((m, k_total), out_dtype),
        scratch_shapes=[pltpu.VMEM((tm, tn), F32)],
        compiler_params=_cp(("parallel", "parallel", "arbitrary")),
    )(dy, w)


def mm_tn(x, dy, g, layer, blocked, *, name, tm=None, tn=None, tt=None):
    t_total, k_total = x.shape
    n_total = dy.shape[1]
    cb, _, kd, nd = g.shape
    assert k_total == (kd if blocked == "n" else kd * cb) and n_total == (nd * cb if blocked == "n" else nd)
    tm = tm or _pick(kd, (1024, 1408, 512))
    tn = tn or _pick(nd, (1408, 768, 512))
    tt = tt or _pick(t_total, (1088, 96))
    nt = t_total // tt
    widx = _w_index(blocked, layer, kd // tm, nd // tn)

    def body(x_ref, dy_ref, g_in, o_ref, acc_ref):
        del g_in
        tstep = pl.program_id(2)
        part = lax.dot_general(x_ref[...].astype(BF16), dy_ref[...].astype(BF16), (((0,), (0,)), ((), ())),
                               preferred_element_type=F32)

        @pl.when(tstep == 0)
        def _():
            acc_ref[...] = part

        @pl.when(tstep > 0)
        def _():
            acc_ref[...] += part

        @pl.when(tstep == nt - 1)
        def _():
            o_ref[...] = acc_ref[...].astype(o_ref.dtype)

    return pl.pallas_call(
        body, name=name, grid=(k_total // tm, n_total // tn, nt),
        in_specs=[pl.BlockSpec((tt, tm), lambda i, j, t: (t, i)),
                  pl.BlockSpec((tt, tn), lambda i, j, t: (t, j)),
                  pl.BlockSpec(memory_space=pl.ANY)],
        out_specs=pl.BlockSpec((None, None, tm, tn), lambda i, j, t: widx(i, j)),
        out_shape=_sds(g.shape, g.dtype),
        scratch_shapes=[pltpu.VMEM((tm, tn), F32)],
        input_output_aliases={2: 0},
        compiler_params=_cp(("parallel", "parallel", "arbitrary")),
    )(x, dy, g)


def _rms(x, w):
    return x * lax.rsqrt(jnp.mean(x * x, axis=-1, keepdims=True) + NORM_EPS) * w


def _row_tile(tc):
    return 256 if tc % 256 == 0 else 128


def _seg_spec(nct, d):
    return pl.BlockSpec((None, 8, d), lambda i: (jnp.minimum(i // nct, 1), 0, 0))


def _acc_rows(acc_ref, i, nct, rows):
    @pl.when((i == 0) | (i == nct))
    def _():
        acc_ref[...] = jnp.zeros_like(acc_ref)

    for r, val in enumerate(rows):
        acc_ref[r:r + 1, :] += val


def norm_mod_fwd(u, nw, modtab, tc, which, *, name):
    t, d = u.shape
    tr = _row_tile(tc)
    nct = tc // tr
    r0 = 3 * which

    def body(u_ref, nw_ref, mt_ref, h_ref):
        sh, sc = mt_ref[r0:r0 + 1, :], mt_ref[r0 + 1:r0 + 2, :]
        h_ref[...] = (_rms(u_ref[...], nw_ref[...]) * (1.0 + sc) + sh).astype(h_ref.dtype)

    return pl.pallas_call(
        body, name=name, grid=(t // tr,),
        in_specs=[pl.BlockSpec((tr, d), lambda i: (i, 0)), pl.BlockSpec((1, d), lambda i: (0, 0)), _seg_spec(nct, d)],
        out_specs=pl.BlockSpec((tr, d), lambda i: (i, 0)),
        out_shape=_sds((t, d), BF16), compiler_params=_cp(("arbitrary",)),
    )(u, nw.reshape(1, d), modtab)


def norm_mod_bwd(u, nw, modtab, dh, du_in, tc, which, *, name):
    t, d = u.shape
    tr = _row_tile(tc)
    nct = tc // tr
    r0 = 3 * which

    def body(u_ref, nw_ref, mt_ref, dh_ref, dui_ref, du_ref, acc_ref):
        i = pl.program_id(0)
        sh, sc = mt_ref[r0:r0 + 1, :], mt_ref[r0 + 1:r0 + 2, :]
        _, vjp = jax.vjp(lambda x, w, a, b: _rms(x, w) * (1.0 + b) + a, u_ref[...], nw_ref[...], sh, sc)
        dx, dw, dsh, dsc = vjp(dh_ref[...].astype(F32))
        du_ref[...] = dui_ref[...] + dx
        _acc_rows(acc_ref, i, nct, (dw, dsh, dsc))

    row = pl.BlockSpec((tr, d), lambda i: (i, 0))
    return pl.pallas_call(
        body, name=name, grid=(t // tr,),
        in_specs=[row, pl.BlockSpec((1, d), lambda i: (0, 0)), _seg_spec(nct, d), row, row],
        out_specs=(row, _seg_spec(nct, d)),
        out_shape=(_sds((t, d), F32), _sds((2, 8, d), F32)), compiler_params=_cp(("arbitrary",)),
    )(u, nw.reshape(1, d), modtab, dh, du_in)


def resid_fwd(u, o, nw, modtab, tc, which, *, name):
    t, d = u.shape
    tr = _row_tile(tc)
    nct = tc // tr
    r0 = 3 * which + 2

    def body(u_ref, o_ref, nw_ref, mt_ref, out_ref):
        out_ref[...] = u_ref[...] + mt_ref[r0:r0 + 1, :] * _rms(o_ref[...], nw_ref[...])

    row = pl.BlockSpec((tr, d), lambda i: (i, 0))
    return pl.pallas_call(
        body, name=name, grid=(t // tr,),
        in_specs=[row, row, pl.BlockSpec((1, d), lambda i: (0, 0)), _seg_spec(nct, d)],
        out_specs=row, out_shape=_sds((t, d), F32), compiler_params=_cp(("arbitrary",)),
    )(u, o, nw.reshape(1, d), modtab)


def resid_bwd(o, nw, modtab, du, tc, which, *, name):
    t, d = o.shape
    tr = _row_tile(tc)
    nct = tc // tr
    r0 = 3 * which + 2

    def body(o_ref, nw_ref, mt_ref, du_ref, do_ref, acc_ref):
        i = pl.program_id(0)
        _, vjp = jax.vjp(lambda x, w, g: g * _rms(x, w), o_ref[...], nw_ref[...], mt_ref[r0:r0 + 1, :])
        dx, dw, dg = vjp(du_ref[...])
        do_ref[...] = dx.astype(do_ref.dtype)
        _acc_rows(acc_ref, i, nct, (dw, dg))

    row = pl.BlockSpec((tr, d), lambda i: (i, 0))
    return pl.pallas_call(
        body, name=name, grid=(t // tr,),
        in_specs=[row, pl.BlockSpec((1, d), lambda i: (0, 0)), _seg_spec(nct, d), row],
        out_specs=(row, _seg_spec(nct, d)),
        out_shape=(_sds((t, d), BF16), _sds((2, 8, d), F32)), compiler_params=_cp(("arbitrary",)),
    )(o, nw.reshape(1, d), modtab, du)


def swiglu_fwd(p, *, name):
    t, h2 = p.shape
    h = h2 // 2
    tr = _pick(t, (1088, 192))
    tc = _pick(h, (512, 256))
    nh = h // tc

    def body(g_ref, u_ref, a_ref):
        a_ref[...] = (jax.nn.silu(g_ref[...]) * u_ref[...]).astype(a_ref.dtype)

    return pl.pallas_call(
        body, name=name, grid=(t // tr, nh),
        in_specs=[pl.BlockSpec((tr, tc), lambda i, j: (i, j)), pl.BlockSpec((tr, tc), lambda i, j: (i, j + nh))],
        out_specs=pl.BlockSpec((tr, tc), lambda i, j: (i, j)),
        out_shape=_sds((t, h), BF16), compiler_params=_cp(("parallel", "parallel")),
    )(p, p)


def swiglu_bwd(p, da, *, name):
    t, h2 = p.shape
    h = h2 // 2
    tr = _pick(t, (1088, 192))
    tc = _pick(h, (512, 256))
    nh = h // tc

    def body(g_ref, u_ref, da_ref, dp_ref):
        j = pl.program_id(1)
        _, vjp = jax.vjp(lambda g, u: jax.nn.silu(g) * u, g_ref[...], u_ref[...])
        dg, du = vjp(da_ref[...])

        @pl.when(j < nh)
        def _():
            dp_ref[...] = dg.astype(dp_ref.dtype)

        @pl.when(j >= nh)
        def _():
            dp_ref[...] = du.astype(dp_ref.dtype)

    return pl.pallas_call(
        body, name=name, grid=(t // tr, 2 * nh),
        in_specs=[pl.BlockSpec((tr, tc), lambda i, j: (i, j % nh)), pl.BlockSpec((tr, tc), lambda i, j: (i, j % nh + nh)),
                  pl.BlockSpec((tr, tc), lambda i, j: (i, j % nh))],
        out_specs=pl.BlockSpec((tr, tc), lambda i, j: (i, j)),
        out_shape=_sds((t, h2), BF16), compiler_params=_cp(("parallel", "parallel")),
    )(p, p, da)


def loss_fwd_bwd(u, target, tc, *, name):
    t, d = u.shape
    tr = _row_tile(tc)
    nct = tc // tr

    def body(u_ref, t_ref, loss_ref, du_ref):
        i = pl.program_id(0)

        @pl.when(i == 0)
        def _():
            loss_ref[...] = jnp.zeros_like(loss_ref)

        @pl.when(i < nct)
        def _():
            du_ref[...] = jnp.zeros_like(du_ref)

        @pl.when(i >= nct)
        def _():
            err = u_ref[...] - t_ref[...]
            du_ref[...] = err * (1.0 / d)
            loss_ref[...] += jnp.sum(jnp.sum(err * err, axis=1, keepdims=True), axis=0, keepdims=True) * (0.5 / d)

    return pl.pallas_call(
        body, name=name, grid=(t // tr,),
        in_specs=[pl.BlockSpec((tr, d), lambda i: (i, 0)), pl.BlockSpec((tr, d), lambda i: (jnp.maximum(i - nct, 0), 0))],
        out_specs=(pl.BlockSpec((1, 1), lambda i: (0, 0)), pl.BlockSpec((tr, d), lambda i: (i, 0))),
        out_shape=(_sds((1, 1), F32), _sds((t, d), F32)), compiler_params=_cp(("arbitrary",)),
    )(u, target)


SSD_INNER = 1024
SGU_WIDTH = 1024
XBC_DIM = 1536
EVEN_COLS = 4640
EVEN_PAD_COLS = 5120
Z_BLK, U_BLK, V_BLK, X_BLK, B_BLK, C_BLK, DT_BLK = 0, 8, 16, 24, 32, 34, 36
PAD_ROWS = 8


def _conv_scratch_fill(pad_ref, val, tc, s):
    pad_ref[...] = jnp.zeros_like(pad_ref)
    pad_ref[PAD_ROWS:PAD_ROWS + tc, :] = val[:tc]
    pad_ref[2 * PAD_ROWS + tc:2 * PAD_ROWS + tc + s, :] = val[tc:]


def _conv_taps(pad_ref, tc, s, k):
    off = k - CONV_K // 2
    return (pad_ref[PAD_ROWS + off:PAD_ROWS + off + tc, :],
            pad_ref[2 * PAD_ROWS + tc + off:2 * PAD_ROWS + tc + off + s, :])


def conv_fwd(p, wb, tc, *, name):
    t = p.shape[0]
    s = t - tc
    nblk = XBC_DIM // LANE

    def body(p_ref, wb_ref, out_ref, pad_ref):
        _conv_scratch_fill(pad_ref, p_ref[...], tc, s)
        acc_c = jnp.zeros((tc, LANE), F32) + wb_ref[5:6, :]
        acc_l = jnp.zeros((s, LANE), F32) + wb_ref[5:6, :]
        for k in range(CONV_K):
            xc, xl = _conv_taps(pad_ref, tc, s, k)
            acc_c += xc * wb_ref[k:k + 1, :]
            acc_l += xl * wb_ref[k:k + 1, :]
        out_ref[:tc, :] = jax.nn.silu(acc_c)
        out_ref[tc:, :] = jax.nn.silu(acc_l)

    return pl.pallas_call(
        body, name=name, grid=(nblk,),
        in_specs=[pl.BlockSpec((t, LANE), lambda j: (0, X_BLK + j)), pl.BlockSpec((8, LANE), lambda j: (0, j))],
        out_specs=pl.BlockSpec((t, LANE), lambda j: (0, j)),
        out_shape=_sds((t, XBC_DIM), F32),
        scratch_shapes=[pltpu.VMEM((t + 3 * PAD_ROWS, LANE), F32)],
        compiler_params=_cp(("parallel",)),
    )(p, wb)


def conv_bwd(p, wb, dxg, dskip, tc, *, name):
    t = p.shape[0]
    s = t - tc
    nblk = XBC_DIM // LANE
    nx = SSD_INNER // LANE

    def grp(j):
        return jnp.where(j < nx, j // 4, (j - nx) % 2)

    def sub(j):
        return jnp.where(j < nx, j % 4, 4 + (j - nx) // 2)

    def body(p_ref, wb_ref, d0_ref, d1_ref, ds_ref, dp_ref, dwb_ref, pad_ref, dpad_ref):
        j = pl.program_id(0)
        _conv_scratch_fill(pad_ref, p_ref[...], tc, s)
        pre = [jnp.zeros((tc, LANE), F32) + wb_ref[5:6, :], jnp.zeros((s, LANE), F32) + wb_ref[5:6, :]]
        for k in range(CONV_K):
            xc, xl = _conv_taps(pad_ref, tc, s, k)
            pre[0] += xc * wb_ref[k:k + 1, :]
            pre[1] += xl * wb_ref[k:k + 1, :]
        dx = d0_ref[...] + d1_ref[...] + jnp.where(j < nx, ds_ref[...], 0.0)
        dpre = []
        for part, rows in ((0, slice(0, tc)), (1, slice(tc, t))):
            sig = jax.nn.sigmoid(pre[part])
            dpre.append(dx[rows] * (sig * (1.0 + pre[part] * (1.0 - sig))))
        dwb_ref[...] = jnp.zeros_like(dwb_ref)
        dwb_ref[5:6, :] = jnp.sum(dpre[0], axis=0, keepdims=True) + jnp.sum(dpre[1], axis=0, keepdims=True)
        for k in range(CONV_K):
            xc, xl = _conv_taps(pad_ref, tc, s, k)
            dwb_ref[k:k + 1, :] = (jnp.sum(dpre[0] * xc, axis=0, keepdims=True)
                                   + jnp.sum(dpre[1] * xl, axis=0, keepdims=True))
        _conv_scratch_fill(dpad_ref, jnp.concatenate(dpre, axis=0), tc, s)
        acc_c = jnp.zeros((tc, LANE), F32)
        acc_l = jnp.zeros((s, LANE), F32)
        for k in range(CONV_K):
            gc, gl = _conv_taps(dpad_ref, tc, s, CONV_K - 1 - k)
            acc_c += gc * wb_ref[k:k + 1, :]
            acc_l += gl * wb_ref[k:k + 1, :]
        dp_ref[:tc, :] = acc_c.astype(dp_ref.dtype)
        dp_ref[tc:, :] = acc_l.astype(dp_ref.dtype)

    col = pl.BlockSpec((t, LANE), lambda j: (0, j))
    return pl.pallas_call(
        body, name=name, grid=(nblk,),
        in_specs=[pl.BlockSpec((t, LANE), lambda j: (0, X_BLK + j)), pl.BlockSpec((8, LANE), lambda j: (0, j)),
                  pl.BlockSpec((None, None, t, LANE), lambda j: (0, grp(j), 0, sub(j))),
                  pl.BlockSpec((None, None, t, LANE), lambda j: (1, grp(j), 0, sub(j))),
                  pl.BlockSpec((t, LANE), lambda j: (0, jnp.minimum(j, nx - 1)))],
        out_specs=(col, pl.BlockSpec((8, LANE), lambda j: (0, j))),
        out_shape=(_sds((t, XBC_DIM), BF16), _sds((8, XBC_DIM), F32)),
        scratch_shapes=[pltpu.VMEM((t + 3 * PAD_ROWS, LANE), F32), pltpu.VMEM((t + 3 * PAD_ROWS, LANE), F32)],
        compiler_params=_cp(("parallel",)),
    )(p, wb, dxg, dxg, dskip)


HEADS_PER_DG = 8


def _ssd_prep_fn(pre, bias, alog, rev):
    q = pre.shape[0]
    lane = lax.broadcasted_iota(jnp.int32, (1, LANE), 1)
    dt = jnp.where(lane < HEADS_PER_DG, jax.nn.softplus(pre + bias), 0.0)
    row = lax.broadcasted_iota(jnp.int32, (q, q), 0)
    col = lax.broadcasted_iota(jnp.int32, (q, q), 1)
    tri = jnp.where((col - row) * jnp.where(rev, 1, -1) >= 0, 1.0, 0.0)
    cs = jnp.dot(tri, dt * (-jnp.exp(alog)), precision=HI, preferred_element_type=F32)
    return dt, cs


def _scan_chunk(nc_ctx, nch):
    def idx(dg, i):
        fwd = i
        bwd = jnp.where(i < nc_ctx, nc_ctx - 1 - i, nch - 1 - (i - nc_ctx))
        return jnp.where(dg // 2 == 0, fwd, bwd)
    return idx


def ssd_prep_fwd(pre, bias, alog, *, name):
    _, t, _ = pre.shape
    blk = pl.BlockSpec((None, CHUNK, LANE), lambda dg, i: (dg, i, 0))
    par = pl.BlockSpec((None, 1, LANE), lambda dg, i: (dg, 0, 0))

    def body(pre_ref, b_ref, a_ref, dt_ref, cs_ref):
        dt, cs = _ssd_prep_fn(pre_ref[...], b_ref[...], a_ref[...], pl.program_id(0) // 2 == 1)
        dt_ref[...] = dt
        cs_ref[...] = cs

    return pl.pallas_call(
        body, name=name, grid=(4, t // CHUNK), in_specs=[blk, par, par], out_specs=(blk, blk),
        out_shape=(_sds(pre.shape, F32), _sds(pre.shape, F32)), compiler_params=_cp(("parallel", "parallel")),
    )(pre, bias, alog)


def ssd_prep_bwd(pre, bias, alog, ddt, dcs, *, name):
    _, t, _ = pre.shape
    blk = pl.BlockSpec((None, CHUNK, LANE), lambda dg, i: (dg, i, 0))
    par = pl.BlockSpec((None, 1, LANE), lambda dg, i: (dg, 0, 0))

    def body(pre_ref, b_ref, a_ref, ddt_ref, dcs_ref, dpre_ref, acc_ref):
        rev = pl.program_id(0) // 2 == 1
        _, vjp = jax.vjp(lambda x, b, a: _ssd_prep_fn(x, b, a, rev), pre_ref[...], b_ref[...], a_ref[...])
        dpre, db, da = vjp((ddt_ref[...], dcs_ref[...]))
        dpre_ref[...] = dpre

        @pl.when(pl.program_id(1) == 0)
        def _():
            acc_ref[...] = jnp.zeros_like(acc_ref)

        acc_ref[0:1, :] += db
        acc_ref[1:2, :] += da

    return pl.pallas_call(
        body, name=name, grid=(4, t // CHUNK), in_specs=[blk, par, par, blk, blk],
        out_specs=(blk, pl.BlockSpec((None, 8, LANE), lambda dg, i: (dg, 0, 0))),
        out_shape=(_sds(pre.shape, F32), _sds((4, 8, LANE), F32)), compiler_params=_cp(("parallel", "arbitrary")),
    )(pre, bias, alog, ddt, dcs)


def _onehot_col(a, h):
    lane = lax.broadcasted_iota(jnp.int32, (1, a.shape[1]), 1)
    return jnp.sum(jnp.where(lane == h, a, 0.0), axis=1, keepdims=True)


def _onehot_row(a, h):
    sub = lax.broadcasted_iota(jnp.int32, (a.shape[0], 1), 0)
    return jnp.sum(jnp.where(sub == h, a, 0.0), axis=0, keepdims=True)


def _bdot(a, b, dims):
    return lax.dot_general(a.astype(BF16), b.astype(BF16), (dims, ((), ())), preferred_element_type=F32)


def _ssd_pair(xblk, dt, cs, bm, cm, sp, rev, pair):
    q = xblk.shape[0]
    lane = lax.broadcasted_iota(jnp.int32, (1, LANE), 1)
    sub = lax.broadcasted_iota(jnp.int32, (LANE, 1), 0)
    row = lax.broadcasted_iota(jnp.int32, (q, q), 0)
    col = lax.broadcasted_iota(jnp.int32, (q, q), 1)
    mask = (col - row) * jnp.where(rev, 1, -1) >= 0
    last = jnp.where(rev, 0, q - 1)
    cs_t = cs.T
    g = _bdot(cm, bm, ((1,), (1,)))
    y = jnp.zeros((q, LANE), F32)
    escale = jnp.zeros((q, LANE), F32)
    xw = jnp.zeros((q, LANE), F32)
    dec = jnp.zeros((LANE, 1), F32)
    for hh in range(2):
        h = 2 * pair + hh
        c_col = _onehot_col(cs, h)
        c_row = _onehot_row(cs_t, h)
        tot = jnp.sum(jnp.where(lax.broadcasted_iota(jnp.int32, (1, q), 1) == last, c_row, 0.0), axis=1, keepdims=True)
        in_head = (lane >= hh * SSD_HEAD_DIM) & (lane < (hh + 1) * SSD_HEAD_DIM)
        xh = jnp.where(in_head, xblk * _onehot_col(dt, h), 0.0)
        ldec = jnp.where(mask, jnp.exp(jnp.where(mask, c_col - c_row, 0.0)), 0.0)
        y = y + _bdot(g * ldec, xh, ((1,), (0,)))
        escale = escale + jnp.where(in_head, jnp.exp(c_col), 0.0)
        xw = xw + xh * jnp.exp(tot - c_col)
        dec = dec + jnp.where((sub >= hh * SSD_HEAD_DIM) & (sub < (hh + 1) * SSD_HEAD_DIM), jnp.exp(tot), 0.0)
    y = y + _bdot(cm, sp, ((1,), (1,))) * escale
    s_new = sp * dec + _bdot(xw, bm, ((0,), (0,)))
    return y, s_new


def ssd_fwd(xbc, dt, cs, tc, *, name):
    t = xbc.shape[0]
    nch = t // CHUNK
    sidx = _scan_chunk(tc // CHUNK, nch)
    gw = SSD_INNER // 2
    nb = SSD_INNER // LANE

    def body(x_ref, b_ref, c_ref, dt_ref, cs_ref, y_ref, sp_ref, s_ref):
        @pl.when(pl.program_id(1) == 0)
        def _():
            s_ref[...] = jnp.zeros_like(s_ref)

        rev = pl.program_id(0) // 2 == 1
        sp_ref[...] = s_ref[...]
        for p in range(gw // LANE):
            blk = slice(p * LANE, (p + 1) * LANE)
            y, s_new = _ssd_pair(x_ref[:, blk], dt_ref[...], cs_ref[...], b_ref[...], c_ref[...], s_ref[blk, :], rev, p)
            y_ref[:, blk] = y
            s_ref[blk, :] = s_new

    return pl.pallas_call(
        body, name=name, grid=(4, nch),
        in_specs=[pl.BlockSpec((CHUNK, gw), lambda dg, i: (sidx(dg, i), dg % 2)),
                  pl.BlockSpec((CHUNK, LANE), lambda dg, i: (sidx(dg, i), nb + dg % 2)),
                  pl.BlockSpec((CHUNK, LANE), lambda dg, i: (sidx(dg, i), nb + 2 + dg % 2)),
                  pl.BlockSpec((None, CHUNK, LANE), lambda dg, i: (dg, sidx(dg, i), 0)),
                  pl.BlockSpec((None, CHUNK, LANE), lambda dg, i: (dg, sidx(dg, i), 0))],
        out_specs=(pl.BlockSpec((None, CHUNK, gw), lambda dg, i: (dg // 2, sidx(dg, i), dg % 2)),
                   pl.BlockSpec((None, None, gw, SSD_STATE), lambda dg, i: (dg, sidx(dg, i), 0, 0))),
        out_shape=(_sds((2, t, SSD_INNER), F32), _sds((4, nch, gw, SSD_STATE), F32)),
        scratch_shapes=[pltpu.VMEM((gw, SSD_STATE), F32)],
        compiler_params=_cp(("parallel", "arbitrary")),
    )(xbc, xbc, xbc, dt, cs)


def ssd_bwd(xbc, dt, cs, sprev, dy, tc, *, name):
    t = xbc.shape[0]
    nch = t // CHUNK
    fidx = _scan_chunk(tc // CHUNK, nch)
    sidx = lambda dg, i: fidx(dg, nch - 1 - i)
    gw = SSD_INNER // 2
    nb = SSD_INNER // LANE

    def body(x_ref, b_ref, c_ref, dt_ref, cs_ref, sp_ref, dy_ref, dxg_ref, ddt_ref, dcs_ref, ds_ref):
        @pl.when(pl.program_id(1) == 0)
        def _():
            ds_ref[...] = jnp.zeros_like(ds_ref)

        rev = pl.program_id(0) // 2 == 1
        ddt = jnp.zeros((CHUNK, LANE), F32)
        dcs = jnp.zeros((CHUNK, LANE), F32)
        db = jnp.zeros((CHUNK, SSD_STATE), F32)
        dc = jnp.zeros((CHUNK, SSD_STATE), F32)
        for p in range(gw // LANE):
            blk = slice(p * LANE, (p + 1) * LANE)
            _, vjp = jax.vjp(functools.partial(_ssd_pair, rev=rev, pair=p),
                             x_ref[:, blk], dt_ref[...], cs_ref[...], b_ref[...], c_ref[...], sp_ref[blk, :])
            dx, ddt_p, dcs_p, db_p, dc_p, dsp = vjp((dy_ref[:, blk], ds_ref[blk, :]))
            dxg_ref[:, blk] = dx
            ds_ref[blk, :] = dsp
            ddt, dcs, db, dc = ddt + ddt_p, dcs + dcs_p, db + db_p, dc + dc_p
        dxg_ref[:, gw:gw + SSD_STATE] = db
        dxg_ref[:, gw + SSD_STATE:] = dc
        ddt_ref[...] = ddt
        dcs_ref[...] = dcs

    hd = pl.BlockSpec((None, CHUNK, LANE), lambda dg, i: (dg, sidx(dg, i), 0))
    return pl.pallas_call(
        body, name=name, grid=(4, nch),
        in_specs=[pl.BlockSpec((CHUNK, gw), lambda dg, i: (sidx(dg, i), dg % 2)),
                  pl.BlockSpec((CHUNK, LANE), lambda dg, i: (sidx(dg, i), nb + dg % 2)),
                  pl.BlockSpec((CHUNK, LANE), lambda dg, i: (sidx(dg, i), nb + 2 + dg % 2)),
                  hd, hd,
                  pl.BlockSpec((None, None, gw, SSD_STATE), lambda dg, i: (dg, sidx(dg, i), 0, 0)),
                  pl.BlockSpec((CHUNK, gw), lambda dg, i: (sidx(dg, i), dg % 2))],
        out_specs=(pl.BlockSpec((None, None, CHUNK, gw + 2 * SSD_STATE), lambda dg, i: (dg // 2, dg % 2, sidx(dg, i), 0)),
                   hd, hd),
        out_shape=(_sds((2, 2, t, gw + 2 * SSD_STATE), F32), _sds((4, t, LANE), F32), _sds((4, t, LANE), F32)),
        scratch_shapes=[pltpu.VMEM((gw, SSD_STATE), F32)],
        compiler_params=_cp(("parallel", "arbitrary")),
    )(xbc, xbc, xbc, dt, cs, sprev, dy)


def _ssd_finish_fn(y0, y1, xs, z, dskip, nw):
    y = (y0 + y1 + xs * dskip) * jax.nn.silu(z)
    half = y.shape[1] // 2
    first = lax.broadcasted_iota(jnp.int32, (1, y.shape[1]), 1) < half
    sq = y * y
    m0 = jnp.sum(jnp.where(first, sq, 0.0), axis=1, keepdims=True) / half
    m1 = jnp.sum(jnp.where(first, 0.0, sq), axis=1, keepdims=True) / half
    return y * jnp.where(first, lax.rsqrt(m0 + NORM_EPS), lax.rsqrt(m1 + NORM_EPS)) * nw


def ssd_finish_fwd(y, xbc, p, dskip, nw, *, name):
    t = xbc.shape[0]
    tr = CHUNK
    w = SSD_INNER
    row = pl.BlockSpec((tr, w), lambda i: (i, 0))
    par = pl.BlockSpec((1, w), lambda i: (0, 0))

    def body(y0_ref, y1_ref, x_ref, z_ref, ds_ref, nw_ref, o_ref):
        o_ref[...] = _ssd_finish_fn(y0_ref[...], y1_ref[...], x_ref[...], z_ref[...], ds_ref[...], nw_ref[...]).astype(o_ref.dtype)

    return pl.pallas_call(
        body, name=name, grid=(t // tr,),
        in_specs=[pl.BlockSpec((None, tr, w), lambda i: (0, i, 0)), pl.BlockSpec((None, tr, w), lambda i: (1, i, 0)),
                  row, row, par, par],
        out_specs=row, out_shape=_sds((t, w), BF16), compiler_params=_cp(("parallel",)),
    )(y, y, xbc, p, dskip, nw)


def ssd_finish_bwd(y, xbc, p, dskip, nw, dout, *, name):
    t = xbc.shape[0]
    tr = CHUNK
    w = SSD_INNER
    row = pl.BlockSpec((tr, w), lambda i: (i, 0))
    par = pl.BlockSpec((1, w), lambda i: (0, 0))

    def body(y0_ref, y1_ref, x_ref, z_ref, ds_ref, nw_ref, do_ref, dy_ref, dx_ref, dz_ref, acc_ref):
        _, vjp = jax.vjp(_ssd_finish_fn, y0_ref[...], y1_ref[...], x_ref[...], z_ref[...], ds_ref[...], nw_ref[...])
        dy0, _, dx, dz, dds, dnw = vjp(do_ref[...])
        dy_ref[...] = dy0
        dx_ref[...] = dx
        dz_ref[...] = dz.astype(dz_ref.dtype)

        @pl.when(pl.program_id(0) == 0)
        def _():
            acc_ref[...] = jnp.zeros_like(acc_ref)

        acc_ref[0:1, :] += dds
        acc_ref[1:2, :] += dnw

    return pl.pallas_call(
        body, name=name, grid=(t // tr,),
        in_specs=[pl.BlockSpec((None, tr, w), lambda i: (0, i, 0)), pl.BlockSpec((None, tr, w), lambda i: (1, i, 0)),
                  row, row, par, par, row],
        out_specs=(row, row, row, pl.BlockSpec((8, w), lambda i: (0, 0))),
        out_shape=(_sds((t, w), F32), _sds((t, w), F32), _sds((t, w), BF16), _sds((8, w), F32)),
        compiler_params=_cp(("arbitrary",)),
    )(y, y, xbc, p, dskip, nw, dout)


SGU_GROUPS = 8


def _sgu_fn(us, vs, ws, bs):
    n = SGU_GROUPS * LANE
    vf = [jax.nn.gelu(v) for v in vs]
    mu = sum(jnp.sum(v, axis=1, keepdims=True) for v in vf) / n
    var = sum(jnp.sum(jnp.square(v - mu), axis=1, keepdims=True) for v in vf) / n
    rstd = lax.rsqrt(var + NORM_EPS)
    return tuple(jax.nn.gelu(u) * (_bdot(w, (v - mu) * rstd, ((1,), (0,))) + b) for u, v, w, b in zip(us, vf, ws, bs))


def sgu_fwd(p, w, b, *, name):
    t = p.shape[0]
    wd = SGU_WIDTH

    def body(u_ref, v_ref, w_ref, b_ref, o_ref):
        sl = [slice(g * LANE, (g + 1) * LANE) for g in range(SGU_GROUPS)]
        ys = _sgu_fn([u_ref[:, s] for s in sl], [v_ref[:, s] for s in sl], [w_ref[g] for g in range(SGU_GROUPS)],
                     [b_ref[g] for g in range(SGU_GROUPS)])
        for s, yv in zip(sl, ys):
            o_ref[:, s] = yv.astype(o_ref.dtype)

    return pl.pallas_call(
        body, name=name, grid=(t // CHUNK,),
        in_specs=[pl.BlockSpec((CHUNK, wd), lambda i: (i, U_BLK * LANE // wd)), pl.BlockSpec((CHUNK, wd), lambda i: (i, V_BLK * LANE // wd)),
                  pl.BlockSpec((SGU_GROUPS, CHUNK, CHUNK), lambda i: (0, 0, 0)), pl.BlockSpec((SGU_GROUPS, CHUNK, 1), lambda i: (0, 0, 0))],
        out_specs=pl.BlockSpec((CHUNK, wd), lambda i: (i, 0)),
        out_shape=_sds((t, wd), BF16), compiler_params=_cp(("parallel",)),
    )(p, p, w, b)


def sgu_bwd(p, w, b, dout, *, name):
    t = p.shape[0]
    wd = SGU_WIDTH

    def body(u_ref, v_ref, w_ref, b_ref, do_ref, duv_ref, dw_ref, db_ref):
        sl = [slice(g * LANE, (g + 1) * LANE) for g in range(SGU_GROUPS)]
        _, vjp = jax.vjp(_sgu_fn, [u_ref[:, s] for s in sl], [v_ref[:, s] for s in sl],
                         [w_ref[g] for g in range(SGU_GROUPS)], [b_ref[g] for g in range(SGU_GROUPS)])
        dus, dvs, dws, dbs = vjp(tuple(do_ref[:, s] for s in sl))

        @pl.when(pl.program_id(0) == 0)
        def _():
            dw_ref[...] = jnp.zeros_like(dw_ref)
            db_ref[...] = jnp.zeros_like(db_ref)

        for g, s in enumerate(sl):
            duv_ref[:, s] = dus[g].astype(duv_ref.dtype)
            duv_ref[:, slice(wd + g * LANE, wd + (g + 1) * LANE)] = dvs[g].astype(duv_ref.dtype)
            dw_ref[g] += dws[g]
            db_ref[g] += dbs[g]

    wspec = pl.BlockSpec((SGU_GROUPS, CHUNK, CHUNK), lambda i: (0, 0, 0))
    bspec = pl.BlockSpec((SGU_GROUPS, CHUNK, 1), lambda i: (0, 0, 0))
    return pl.pallas_call(
        body, name=name, grid=(t // CHUNK,),
        in_specs=[pl.BlockSpec((CHUNK, wd), lambda i: (i, U_BLK * LANE // wd)), pl.BlockSpec((CHUNK, wd), lambda i: (i, V_BLK * LANE // wd)),
                  wspec, bspec, pl.BlockSpec((CHUNK, wd), lambda i: (i, 1))],
        out_specs=(pl.BlockSpec((CHUNK, 2 * wd), lambda i: (i, 0)), wspec, bspec),
        out_shape=(_sds((t, 2 * wd), BF16), _sds(w.shape, F32), _sds(b.shape, F32)),
        compiler_params=_cp(("arbitrary",)),
    )(p, p, w, b, dout)


def even_cols_permute(w):
    z, xbc, dt, u, v = jnp.split(w, (1024, 2560, 2592, 3616), axis=-1)
    pad = jnp.zeros(w.shape[:-1] + (EVEN_PAD_COLS - EVEN_COLS,), w.dtype)
    return jnp.concatenate([z, u, v, xbc, dt, pad], axis=-1)


def even_cols_unpermute(w):
    z, u, v, xbc, dt = jnp.split(w[..., :EVEN_COLS], (1024, 2048, 3072, 4608), axis=-1)
    return jnp.concatenate([z, xbc, dt, u, v], axis=-1)


def _dt_cols(p):
    t = p.shape[0]
    d = p[:, DT_BLK * LANE:DT_BLK * LANE + 4 * HEADS_PER_DG].reshape(t, 4, HEADS_PER_DG).transpose(1, 0, 2)
    return jnp.pad(d, ((0, 0), (0, 0), (0, LANE - HEADS_PER_DG)))


def _heads_to_lanes(a):
    return jnp.pad(a.reshape(4, 1, HEADS_PER_DG), ((0, 0), (0, 0), (0, LANE - HEADS_PER_DG)))


def even_params(conv_w, conv_b, dt_bias, a_log, d_skip, ssd_nw, sgu_w, sgu_b):
    wb = jnp.concatenate([conv_w, conv_b[None], jnp.zeros((2, XBC_DIM), F32)], axis=0)
    return dict(wb=wb, dtb=_heads_to_lanes(dt_bias), alog=_heads_to_lanes(a_log),
                dskip=jnp.repeat(d_skip, SSD_HEAD_DIM)[None], ssd_nw=ssd_nw[None], sgu_w=sgu_w, sgu_b=sgu_b[..., None])


def even_mixer_fwd(h, w_in, w_out, li, ep, tc, tag):
    p = mm_nn(h, w_in, li, "n", name=f"{tag}_in")
    xbc = conv_fwd(p, ep["wb"], tc, name=f"{tag}_conv")
    pre = _dt_cols(p)
    dt, cs = ssd_prep_fwd(pre, ep["dtb"], ep["alog"], name=f"{tag}_prep")
    y, sprev = ssd_fwd(xbc, dt, cs, tc, name=f"{tag}_ssd")
    yssd = ssd_finish_fwd(y, xbc, p, ep["dskip"], ep["ssd_nw"], name=f"{tag}_fin")
    ysgu = sgu_fwd(p, ep["sgu_w"], ep["sgu_b"], name=f"{tag}_sgu")
    ymix = jnp.concatenate([yssd, ysgu], axis=1)
    o = mm_nn(ymix, w_out, li, "k", name=f"{tag}_out")
    return o, (p, xbc, pre, dt, cs, y, sprev, ymix)


def even_mixer_bwd(saved, do, h, w_in, w_out, g_in, g_out, li, ep, tc, tag):
    p, xbc, pre, dt, cs, y, sprev, ymix = saved
    t = h.shape[0]
    g_out = mm_tn(ymix, do, g_out, li, "k", name=f"{tag}_out_dw")
    dymix = mm_nt(do, w_out, li, "k", name=f"{tag}_out_dx")
    dy, dxskip, dz, acc_fin = ssd_finish_bwd(y, xbc, p, ep["dskip"], ep["ssd_nw"], dymix, name=f"{tag}_fin_b")
    duv, dsgu_w, dsgu_b = sgu_bwd(p, ep["sgu_w"], ep["sgu_b"], dymix, name=f"{tag}_sgu_b")
    dxg, ddt, dcs = ssd_bwd(xbc, dt, cs, sprev, dy, tc, name=f"{tag}_ssd_b")
    dpre, acc_prep = ssd_prep_bwd(pre, ep["dtb"], ep["alog"], ddt, dcs, name=f"{tag}_prep_b")
    dxbc, dwb = conv_bwd(p, ep["wb"], dxg, dxskip, tc, name=f"{tag}_conv_b")
    ddt_cols = dpre[:, :, :HEADS_PER_DG].transpose(1, 0, 2).reshape(t, 4 * HEADS_PER_DG).astype(BF16)
    ddt_cols = jnp.pad(ddt_cols, ((0, 0), (0, EVEN_PAD_COLS - DT_BLK * LANE - 4 * HEADS_PER_DG)))
    dp = jnp.concatenate([dz, duv, dxbc, ddt_cols], axis=1)
    dh = mm_nt(dp, w_in, li, "n", name=f"{tag}_in_dx")
    g_in = mm_tn(h, dp, g_in, li, "n", name=f"{tag}_in_dw")
    small = dict(
        e_conv_w=dwb[:CONV_K], e_conv_b=dwb[CONV_K],
        e_dt_bias=acc_prep[:, 0, :HEADS_PER_DG].reshape(2, 2 * HEADS_PER_DG),
        e_a_log=acc_prep[:, 1, :HEADS_PER_DG].reshape(2, 2 * HEADS_PER_DG),
        e_d_skip=acc_fin[0].reshape(-1, SSD_HEAD_DIM).sum(axis=1), e_ssd_norm_w=acc_fin[1],
        e_sgu_w=dsgu_w, e_sgu_b=dsgu_b[..., 0])
    return dh, g_in, g_out, small


ATT_HEADS = 16
ATT_KV = 4
Q_BLKS, K_BLKS = ATT_HEADS, ATT_KV


def rope_tables(tc, s):
    quarter = ATT_HEAD_DIM // 4
    pos = jnp.arange(s)
    inv = ROPE_BASE ** (-jnp.arange(quarter, dtype=F32) / quarter)
    a_row = (pos // GRID_W).astype(F32)[:, None] * inv
    a_col = (pos % GRID_W).astype(F32)[:, None] * inv
    cos = jnp.concatenate([jnp.cos(a_row)] * 2 + [jnp.cos(a_col)] * 2, axis=1)
    sin = jnp.concatenate([-jnp.sin(a_row), jnp.sin(a_row), -jnp.sin(a_col), jnp.sin(a_col)], axis=1)
    return (jnp.concatenate([jnp.ones((tc, ATT_HEAD_DIM), F32), cos], axis=0),
            jnp.concatenate([jnp.zeros((tc, ATT_HEAD_DIM), F32), sin], axis=0))


def _swap_halves(x):
    lane = lax.broadcasted_iota(jnp.int32, x.shape, 1)
    return jnp.where(lane % 64 < 32, pltpu.roll(x, 96, 1), pltpu.roll(x, 32, 1))


def rope_fwd(p, cos, sin, *, name):
    t = p.shape[0]
    tr = _pick(t, (1088, 640))
    scale = ATT_HEAD_DIM ** -0.5

    def body(p_ref, c_ref, s_ref, o_ref):
        x = p_ref[...]
        r = x * c_ref[...] + _swap_halves(x) * s_ref[...]
        o_ref[...] = (r * jnp.where(pl.program_id(1) < Q_BLKS, scale, 1.0)).astype(o_ref.dtype)

    tab = pl.BlockSpec((tr, LANE), lambda i, j: (i, 0))
    return pl.pallas_call(
        body, name=name, grid=(t // tr, Q_BLKS + K_BLKS),
        in_specs=[pl.BlockSpec((tr, LANE), lambda i, j: (i, j)), tab, tab],
        out_specs=pl.BlockSpec((tr, LANE), lambda i, j: (i, j)),
        out_shape=_sds((t, (Q_BLKS + K_BLKS) * LANE), BF16), compiler_params=_cp(("parallel", "parallel")),
    )(p, cos, sin)


def rope_bwd(dq, dk, dv, cos, sin, *, name):
    t = dq.shape[0]
    tr = _pick(t, (1088, 640))
    scale = ATT_HEAD_DIM ** -0.5

    def body(dq_ref, dk_ref, dv_ref, c_ref, s_ref, o_ref):
        j = pl.program_id(1)

        def unrot(g):
            return g * c_ref[...] + _swap_halves(g * s_ref[...])

        @pl.when(j < Q_BLKS)
        def _():
            o_ref[...] = (unrot(dq_ref[...]) * scale).astype(o_ref.dtype)

        @pl.when((j >= Q_BLKS) & (j < Q_BLKS + K_BLKS))
        def _():
            o_ref[...] = unrot(dk_ref[...]).astype(o_ref.dtype)

        @pl.when(j >= Q_BLKS + K_BLKS)
        def _():
            o_ref[...] = dv_ref[...].astype(o_ref.dtype)

    tab = pl.BlockSpec((tr, LANE), lambda i, j: (i, 0))
    return pl.pallas_call(
        body, name=name, grid=(t // tr, Q_BLKS + 2 * K_BLKS),
        in_specs=[pl.BlockSpec((tr, LANE), lambda i, j: (i, jnp.minimum(j, Q_BLKS - 1))),
                  pl.BlockSpec((None, tr, LANE), lambda i, j: (jnp.clip(j - Q_BLKS, 0, K_BLKS - 1), i, 0)),
                  pl.BlockSpec((None, tr, LANE), lambda i, j: (jnp.clip(j - Q_BLKS - K_BLKS, 0, K_BLKS - 1), i, 0)), tab, tab],
        out_specs=pl.BlockSpec((tr, LANE), lambda i, j: (i, j)),
        out_shape=_sds((t, (Q_BLKS + 2 * K_BLKS) * LANE), BF16), compiler_params=_cp(("parallel", "parallel")),
    )(dq, dk, dv, cos, sin)


def _attn_tile(qs, kp, kc, kn, vp, vc, vn, kx, vx, sinks, is_lat, has_prev, has_next):
    q = kp.shape[0]
    row = lax.broadcasted_iota(jnp.int32, (q, q), 0)
    col = lax.broadcasted_iota(jnp.int32, (q, q), 1)
    m_prev = (col - row) >= (1 - has_prev) * q
    m_cur = (row - row) >= (1 - is_lat)
    m_next = (row - col) >= (1 - has_next) * q
    lane = lax.broadcasted_iota(jnp.int32, (1, LANE), 1)
    outs = []
    for qh, sinkrow in zip(qs, sinks):
        sink = jnp.sum(jnp.where(lane == 0, sinkrow, 0.0), axis=1, keepdims=True)
        s_p = jnp.where(m_prev, _bdot(qh, kp, ((1,), (1,))), NEG_INF)
        s_c = jnp.where(m_cur, _bdot(qh, kc, ((1,), (1,))), NEG_INF)
        s_n = jnp.where(m_next, _bdot(qh, kn, ((1,), (1,))), NEG_INF)
        s_x = _bdot(qh, kx, ((1,), (1,)))
        mx = [jnp.max(a, axis=1, keepdims=True) for a in (s_p, s_c, s_n, s_x)]
        m = lax.stop_gradient(jnp.maximum(jnp.maximum(jnp.maximum(mx[0], mx[1]), jnp.maximum(mx[2], mx[3])), sink))
        e = [jnp.exp(a - m) for a in (s_p, s_c, s_n, s_x)]
        inv = 1.0 / (sum(jnp.sum(a, axis=1, keepdims=True) for a in e) + jnp.exp(sink - m))
        outs.append(sum(_bdot(a * inv, v, ((1,), (0,))) for a, v in zip(e, (vp, vc, vn, vx))))
    return tuple(outs)


def _attn_specs(t, tc):
    nblk = t // CHUNK
    hw = ATT_GROUP * LANE
    kcol = lambda kv: Q_BLKS + kv
    vcol = lambda kv: Q_BLKS + K_BLKS + kv
    prev = lambda n: jnp.maximum(n - 1, 0)
    nxt = lambda n: jnp.minimum(n + 1, nblk - 1)
    blk = lambda rowf, colf: pl.BlockSpec((CHUNK, LANE), lambda kv, n: (rowf(n), colf(kv)))
    same = lambda n: n
    return [pl.BlockSpec((CHUNK, hw), lambda kv, n: (n, kv)),
            blk(prev, kcol), blk(same, kcol), blk(nxt, kcol), blk(prev, vcol), blk(same, vcol), blk(nxt, vcol),
            pl.BlockSpec((tc, LANE), lambda kv, n: (0, kcol(kv))), pl.BlockSpec((tc, LANE), lambda kv, n: (0, vcol(kv))),
            pl.BlockSpec((None, 8, LANE), lambda kv, n: (kv, 0, 0))]


def _attn_args(refs, n, nct, nblk):
    q_ref, kp, kc, kn, vp, vc, vn, kx, vx, sk = refs
    f = lambda r: r[...].astype(F32)
    qs = [q_ref[:, g * LANE:(g + 1) * LANE].astype(F32) for g in range(ATT_GROUP)]
    sinks = [sk[g:g + 1, :] for g in range(ATT_GROUP)]
    flags = ((n >= nct).astype(jnp.int32), (n >= nct + 1).astype(jnp.int32), ((n >= nct) & (n + 1 < nblk)).astype(jnp.int32))
    return (qs, f(kp), f(kc), f(kn), f(vp), f(vc), f(vn), f(kx), f(vx), sinks), flags


def attn_fwd(qk, p, sink, tc, *, name):
    t = qk.shape[0]
    nblk, nct = t // CHUNK, tc // CHUNK
    hw = ATT_GROUP * LANE

    def body(*refs):
        o_ref = refs[-1]
        args, flags = _attn_args(refs[:-1], pl.program_id(1), nct, nblk)
        outs = _attn_tile(*args, *flags)
        for g, o in enumerate(outs):
            o_ref[:, g * LANE:(g + 1) * LANE] = o.astype(o_ref.dtype)

    return pl.pallas_call(
        body, name=name, grid=(ATT_KV, nblk), in_specs=_attn_specs(t, tc),
        out_specs=pl.BlockSpec((CHUNK, hw), lambda kv, n: (n, kv)),
        out_shape=_sds((t, ATT_HEADS * LANE), BF16), compiler_params=_cp(("parallel", "parallel")),
    )(qk, qk, qk, qk, p, p, p, qk, p, sink)


def attn_bwd(qk, p, sink, do, tc, *, name):
    t = qk.shape[0]
    nblk, nct = t // CHUNK, tc // CHUNK
    hw = ATT_GROUP * LANE

    def body(*refs):
        do_ref, dq_ref, dk_ref, dv_ref, dsk_ref = refs[-5:]
        n = pl.program_id(1)
        args, flags = _attn_args(refs[:-5], n, nct, nblk)
        _, vjp = jax.vjp(lambda *a: _attn_tile(*a, *flags), *args)
        dqs, dkp, dkc, dkn, dvp, dvc, dvn, dkx, dvx, dsinks = vjp(tuple(do_ref[:, g * LANE:(g + 1) * LANE] for g in range(ATT_GROUP)))

        @pl.when(n == 0)
        def _():
            dk_ref[...] = jnp.zeros_like(dk_ref)
            dv_ref[...] = jnp.zeros_like(dv_ref)
            dsk_ref[...] = jnp.zeros_like(dsk_ref)

        for g in range(ATT_GROUP):
            dq_ref[:, g * LANE:(g + 1) * LANE] = dqs[g]
            dsk_ref[g:g + 1, :] += dsinks[g]
        for blk, dkb, dvb in ((jnp.maximum(n - 1, 0), dkp, dvp), (n, dkc, dvc), (jnp.minimum(n + 1, nblk - 1), dkn, dvn)):
            rows = pl.ds(pl.multiple_of(blk * CHUNK, CHUNK), CHUNK)
            dk_ref[rows, :] += dkb
            dv_ref[rows, :] += dvb
        dk_ref[0:tc, :] += dkx
        dv_ref[0:tc, :] += dvx

    kvacc = pl.BlockSpec((None, t, LANE), lambda kv, n: (kv, 0, 0))
    return pl.pallas_call(
        body, name=name, grid=(ATT_KV, nblk),
        in_specs=_attn_specs(t, tc) + [pl.BlockSpec((CHUNK, hw), lambda kv, n: (n, kv))],
        out_specs=(pl.BlockSpec((CHUNK, hw), lambda kv, n: (n, kv)), kvacc, kvacc,
                   pl.BlockSpec((None, 8, LANE), lambda kv, n: (kv, 0, 0))),
        out_shape=(_sds((t, ATT_HEADS * LANE), F32), _sds((ATT_KV, t, LANE), F32), _sds((ATT_KV, t, LANE), F32),
                   _sds((ATT_KV, 8, LANE), F32)),
        compiler_params=_cp(("parallel", "arbitrary")),
    )(qk, qk, qk, qk, p, p, p, qk, p, sink, do)


def sink_rows(sink):
    s = jnp.broadcast_to(sink.reshape(ATT_KV, ATT_GROUP, 1), (ATT_KV, ATT_GROUP, LANE))
    return jnp.pad(s, ((0, 0), (0, 8 - ATT_GROUP), (0, 0)))


def odd_mixer_fwd(h, w_qkv, w_out, li, sink, cos, sin, tc, tag):
    p = mm_nn(h, w_qkv, li, "n", name=f"{tag}_qkv")
    qk = rope_fwd(p, cos, sin, name=f"{tag}_rope")
    att = attn_fwd(qk, p, sink, tc, name=f"{tag}_att")
    o = mm_nn(att, w_out, li, "k", name=f"{tag}_out")
    return o, (p, qk, att)


def odd_mixer_bwd(saved, do, h, w_qkv, w_out, g_qkv, g_out, li, sink, cos, sin, tc, tag):
    p, qk, att = saved
    g_out = mm_tn(att, do, g_out, li, "k", name=f"{tag}_out_dw")
    datt = mm_nt(do, w_out, li, "k", name=f"{tag}_out_dx")
    dq, dk, dv, dsink = attn_bwd(qk, p, sink, datt, tc, name=f"{tag}_att_b")
    dp = rope_bwd(dq, dk, dv, cos, sin, name=f"{tag}_rope_b")
    dh = mm_nt(dp, w_qkv, li, "n", name=f"{tag}_qkv_dx")
    g_qkv = mm_tn(h, dp, g_qkv, li, "n", name=f"{tag}_qkv_dw")
    return dh, g_qkv, g_out, dict(o_sink=dsink[:, :ATT_GROUP, 0].reshape(-1))


ANY = pl.BlockSpec(memory_space=pl.ANY)


def _place():
    return lax.axis_index("x"), lax.axis_index("y"), lax.axis_index("c")


DMA_PIECES = 16


def _pieces(shape):
    if len(shape) < 2:
        return [()]
    lead, k = shape[:-2], shape[-2]
    split = 1
    while math.prod(lead) * split < DMA_PIECES and k % (2 * split) == 0 and (k // (2 * split)) % 16 == 0:
        split *= 2
    rows = k // split
    out = []
    for li in itertools.product(*[range(n) for n in lead]):
        out += [li + (pl.ds(q * rows, rows),) for q in range(split)]
    return out


def _start_pieces(make, src, dst):
    for idx in _pieces(src.shape):
        make(src.at[idx] if idx else src, dst.at[idx] if idx else dst).start()


def allgather8(blk, *, name):
    def body(x_ref, out_ref, send_sems, recv_sems, local_sem):
        x, y, c = _place()
        me, sibling = (x, y, c), (x, y, 1 - c)
        chips = [(1 - x, y), (x, 1 - y), (1 - x, 1 - y)]

        def slot(px, py, pc):
            return out_ref.at[4 * px + 2 * py + pc]

        def remote(k, to):
            return lambda src, dst: pltpu.make_async_remote_copy(
                src_ref=src, dst_ref=dst, send_sem=send_sems.at[k], recv_sem=recv_sems.at[k], device_id=to, device_id_type=MESH_ID)

        def local(src, dst):
            return pltpu.make_async_copy(src, dst, local_sem)

        _start_pieces(local, x_ref, slot(*me))
        _start_pieces(remote(0, sibling), x_ref, slot(*me))
        for j, chip in enumerate(chips):
            remote(1 + j, (*chip, c))(x_ref, slot(*me)).start()
        for j, chip in enumerate(chips):
            blk = slot(*chip, c)
            remote(1 + j, me)(blk, blk).wait_recv()
            _start_pieces(remote(4 + j, sibling), blk, blk)
        remote(0, me)(slot(*sibling), slot(*sibling)).wait_recv()
        for j, chip in enumerate(chips):
            blk = slot(*chip, 1 - c)
            remote(4 + j, me)(blk, blk).wait_recv()
        remote(0, sibling)(x_ref, slot(*me)).wait_send()
        for j, chip in enumerate(chips):
            remote(1 + j, (*chip, c))(x_ref, slot(*me)).wait_send()
            remote(4 + j, sibling)(slot(*chip, c), slot(*chip, c)).wait_send()
        local(x_ref, slot(*me)).wait()

    return pl.pallas_call(
        body, name=name, out_shape=_sds((N_DEV,) + blk.shape, blk.dtype), in_specs=[ANY], out_specs=ANY,
        scratch_shapes=[pltpu.SemaphoreType.DMA((7,)), pltpu.SemaphoreType.DMA((7,)), pltpu.SemaphoreType.DMA],
        compiler_params=pltpu.CompilerParams(has_side_effects=True),
    )(blk)


def _flip(r, xi, yi):
    return (1 - xi if r & 2 else xi), (1 - yi if r & 1 else yi)


def _to_sibling(send_sem, recv_sem):
    x, y, c = _place()
    return lambda src, dst: pltpu.make_async_remote_copy(src_ref=src, dst_ref=dst, send_sem=send_sem, recv_sem=recv_sem,
                                                         device_id=(x, y, 1 - c), device_id_type=MESH_ID)


def rs_sibling(gs, *, name):
    n = len(gs)

    def body(*refs):
        g_refs, out_refs, (send_sems, recv_sems) = refs[:n], refs[n:2 * n], refs[2 * n:]
        c = lax.axis_index("c")
        copies = [(_to_sibling(send_sems.at[i], recv_sems.at[i]), g_ref.at[:, pl.ds(1 - c, 1)], out_ref)
                  for i, (g_ref, out_ref) in enumerate(zip(g_refs, out_refs))]
        for remote, src, dst in copies:
            _start_pieces(remote, src, dst)
        for remote, src, dst in copies:
            remote(src, dst).wait()

    return pl.pallas_call(
        body, name=name, out_shape=[_sds((g.shape[0], 1) + g.shape[2:], g.dtype) for g in gs],
        in_specs=[ANY] * n, out_specs=[ANY] * n, scratch_shapes=[pltpu.SemaphoreType.DMA((n,)), pltpu.SemaphoreType.DMA((n,))],
        compiler_params=pltpu.CompilerParams(has_side_effects=True),
    )(*gs)


def sibling_swap(halves, *, name):
    n = len(halves)

    def body(*refs):
        h_refs, out_refs, (send_sems, recv_sems) = refs[:n], refs[n:2 * n], refs[2 * n:]
        copies = [(_to_sibling(send_sems.at[i], recv_sems.at[i]), h_ref, out_ref)
                  for i, (h_ref, out_ref) in enumerate(zip(h_refs, out_refs))]
        for remote, src, dst in copies:
            _start_pieces(remote, src, dst)
        for remote, src, dst in copies:
            remote(src, dst).wait()

    return pl.pallas_call(
        body, name=name, out_shape=[_sds(h.shape, h.dtype) for h in halves], in_specs=[ANY] * n, out_specs=[ANY] * n,
        scratch_shapes=[pltpu.SemaphoreType.DMA((n,)), pltpu.SemaphoreType.DMA((n,))],
        compiler_params=pltpu.CompilerParams(has_side_effects=True),
    )(*halves)


HBM_SPEC = pl.BlockSpec(memory_space=pltpu.HBM)
SEM_SPEC = pl.BlockSpec(memory_space=pltpu.SEMAPHORE)
DATAFLOW = pltpu.SideEffectType.DATAFLOW_SIDE_EFFECTING


def _hbm(a):
    return pltpu.with_memory_space_constraint(a, pltpu.HBM)


def _split_start(srcs, land_shapes, starts, *, name):
    n = len(srcs)

    def body(*refs):
        src_refs, land_refs, (send_sem, recv_sem), token = refs[:n], refs[n:2 * n], refs[2 * n:2 * n + 2], refs[-1]
        starts(src_refs, land_refs, send_sem, recv_sem)
        token[...] = jnp.zeros_like(token)

    out = pl.pallas_call(
        body, name=name,
        out_shape=[pltpu.SemaphoreType.DMA(()), pltpu.SemaphoreType.DMA(())] + [pltpu.HBM(s.shape, s.dtype) for s in srcs]
        + [pltpu.HBM(shape, s.dtype) for shape, s in zip(land_shapes, srcs)] + [_sds((8, LANE), F32)],
        in_specs=[HBM_SPEC] * (2 * n), out_specs=[SEM_SPEC, SEM_SPEC] + [HBM_SPEC] * (2 * n) + [pl.BlockSpec(memory_space=pltpu.VMEM)],
        input_output_aliases={i: 2 + i for i in range(2 * n)},
        compiler_params=pltpu.CompilerParams(has_side_effects=DATAFLOW),
    )(*[_hbm(s) for s in srcs], *[_hbm(lax.empty(shape, s.dtype)) for shape, s in zip(land_shapes, srcs)])
    return out[0], out[1], out[2:2 + n], out[2 + n:2 + 2 * n], out[-1]


def _split_wait(handle, after, sent, landed, *, name):
    send_sem, recv_sem, srcs, lands, _ = handle
    n = len(srcs)

    def body(*refs):
        src_refs, land_refs, (send_sem, recv_sem) = refs[:n], refs[n:2 * n], refs[2 * n:2 * n + 2]
        x, y, c = _place()
        for sized, wait in ((sent, "wait_send"), (landed, "wait_recv")):
            for src_ref, land_ref in zip(src_refs, land_refs):
                ref = sized(src_ref, land_ref)
                getattr(pltpu.make_async_remote_copy(src_ref=ref, dst_ref=ref, send_sem=send_sem, recv_sem=recv_sem,
                                                     device_id=(x, y, c), device_id_type=MESH_ID), wait)()

    out = pl.pallas_call(
        body, name=name, out_shape=[pltpu.HBM(a.shape, a.dtype) for a in (*srcs, *lands)],
        in_specs=[HBM_SPEC] * (2 * n) + [SEM_SPEC, SEM_SPEC, ANY], out_specs=[HBM_SPEC] * (2 * n),
        input_output_aliases={i: i for i in range(2 * n)},
        compiler_params=pltpu.CompilerParams(has_side_effects=DATAFLOW),
    )(*srcs, *lands, send_sem, recv_sem, after)
    return out[:n], out[n:]


def ag_send_start(blks, *, name):
    def starts(src_refs, land_refs, send_sem, recv_sem):
        x, y, c = _place()
        me = 4 * x + 2 * y + c
        for to in ((x, y, 1 - c), (1 - x, y, c), (x, 1 - y, c), (1 - x, 1 - y, c)):
            for src_ref, land_ref in zip(src_refs, land_refs):
                pltpu.make_async_remote_copy(src_ref=src_ref, dst_ref=land_ref.at[me], send_sem=send_sem, recv_sem=recv_sem,
                                             device_id=to, device_id_type=MESH_ID).start()

    return _split_start(blks, [(N_DEV,) + b.shape for b in blks], starts, name=name)


def ag_send_wait(handle, after, *, name):
    four = lambda src_ref, land_ref: land_ref.at[pl.ds(0, 4)]
    return _split_wait(handle, after, four, four, name=name)[1]


def ag_forward(lands, sibling_blks, *, name):
    n = len(lands)

    def body(*refs):
        land_refs, blk_refs, out_refs, (send_sems, recv_sems) = refs[:n], refs[n:2 * n], refs[2 * n:3 * n], refs[3 * n:]
        x, y, c = _place()
        for i, (land_ref, blk_ref, out_ref) in enumerate(zip(land_refs, blk_refs, out_refs)):
            remote = _to_sibling(send_sems.at[i], recv_sems.at[i])
            for r in (1, 2, 3):
                px, py = _flip(r, x, y)
                slot = 4 * px + 2 * py + c
                _start_pieces(remote, land_ref.at[slot], out_ref.at[slot])
            _start_pieces(remote, blk_ref, out_ref.at[4 * x + 2 * y + 1 - c])
        for i, out_ref in enumerate(out_refs):
            four = out_ref.at[pl.ds(0, 4)]
            _to_sibling(send_sems.at[i], recv_sems.at[i])(four, four).wait()

    return pl.pallas_call(
        body, name=name, out_shape=[_sds(a.shape, a.dtype) for a in lands], in_specs=[ANY] * (2 * n), out_specs=[ANY] * n,
        scratch_shapes=[pltpu.SemaphoreType.DMA((n,)), pltpu.SemaphoreType.DMA((n,))], input_output_aliases={i: i for i in range(n)},
        compiler_params=pltpu.CompilerParams(has_side_effects=True),
    )(*lands, *sibling_blks)


def rs_chips_start(hs, *, name):
    def starts(src_refs, land_refs, send_sem, recv_sem):
        x, y, c = _place()
        for r in (1, 2, 3):
            px, py = _flip(r, x, y)
            for src_ref, land_ref in zip(src_refs, land_refs):
                pltpu.make_async_remote_copy(src_ref=src_ref.at[2 * px + py], dst_ref=land_ref.at[r - 1], send_sem=send_sem,
                                             recv_sem=recv_sem, device_id=(px, py, c), device_id_type=MESH_ID).start()

    return _split_start(hs, [(3,) + h.shape[1:] for h in hs], starts, name=name)


def rs_chips_wait(handle, after, *, name):
    return _split_wait(handle, after, lambda src_ref, land_ref: src_ref.at[pl.ds(0, 3)], lambda src_ref, land_ref: land_ref, name=name)


def _row_block(kd, nd):
    return _pick(kd, (max(32, (1 << 19) // nd // 32 * 32),))


def add_kept_half(g, recv, core, *, name):
    nchip, nl, kd, nd = g.shape
    lh = nl // 2
    tk = _row_block(kd, nd)

    def body(c_ref, g_ref, r_ref, o_ref):
        del c_ref
        o_ref[...] = (g_ref[...].astype(F32) + r_ref[...].astype(F32)).astype(o_ref.dtype)

    blk = lambda f: pl.BlockSpec((None, None, tk, nd), f)
    return pl.pallas_call(
        body, name=name, out_shape=_sds((nchip, lh, kd, nd), BF16),
        grid_spec=pltpu.PrefetchScalarGridSpec(
            num_scalar_prefetch=1, grid=(nchip, lh, kd // tk),
            in_specs=[blk(lambda j, l, i, c_ref: (j, c_ref[0] * lh + l, i, 0)), blk(lambda j, l, i, c_ref: (j, l, i, 0))],
            out_specs=blk(lambda j, l, i, c_ref: (j, l, i, 0))),
        compiler_params=_cp(("parallel", "parallel", "parallel")),
    )(core, g, recv)


def add_chip_parts(h, parts, chip, *, name):
    _, lh, kd, nd = h.shape
    tk = _row_block(kd, nd)

    def body(k_ref, h_ref, p0, p1, p2, o_ref):
        del k_ref
        o_ref[...] = h_ref[...].astype(F32) + p0[...].astype(F32) + p1[...].astype(F32) + p2[...].astype(F32)

    blk = lambda f: pl.BlockSpec((None, None, tk, nd), f)
    part = lambda r: blk(functools.partial(lambda r_, l, i, k_ref: (r_, l, i, 0), r))
    return pl.pallas_call(
        body, name=name, out_shape=_sds((lh, kd, nd), F32),
        grid_spec=pltpu.PrefetchScalarGridSpec(
            num_scalar_prefetch=1, grid=(lh, kd // tk),
            in_specs=[blk(lambda l, i, k_ref: (k_ref[0], l, i, 0)), part(0), part(1), part(2)],
            out_specs=pl.BlockSpec((None, tk, nd), lambda l, i, k_ref: (l, i, 0))),
        compiler_params=_cp(("parallel", "parallel")),
    )(chip, h, parts, parts, parts)


def sum_slots(a, out_dtype, *, name):
    n = a.shape[0]
    cols = a.shape[-1]
    a3 = a.reshape(n, -1, cols)
    rows = a3.shape[1]
    tr = _pick(rows, (max(32, (1 << 19) // cols // 32 * 32),))

    def body(*refs):
        acc = refs[0][...].astype(F32)
        for r in refs[1:n]:
            acc = acc + r[...].astype(F32)
        refs[n][...] = acc.astype(out_dtype)

    return pl.pallas_call(
        body, name=name, grid=(rows // tr,),
        in_specs=[pl.BlockSpec((None, tr, cols), functools.partial(lambda j, i: (j, i, 0), j)) for j in range(n)],
        out_specs=pl.BlockSpec((tr, cols), lambda i: (i, 0)),
        out_shape=_sds((rows, cols), out_dtype), compiler_params=_cp(("parallel",)),
    )(*([a3] * n)).reshape(a.shape[1:])


def unit_blocks(shards, ci):
    return [lax.dynamic_index_in_dim(w.reshape(2, w.shape[0] // 2, w.shape[1]), ci, axis=0, keepdims=False).astype(BF16)
            for w in shards]


def gather_finish(lands, sibling_blks, tag):
    full = ag_forward(lands, sibling_blks, name=f"{tag}_fwd")
    return [a.reshape(N_CHIP, 1, 2 * a.shape[1], a.shape[2]) for a in full]


def reduce_scatter_start(gs, tag):
    core = jnp.reshape(lax.axis_index("c"), (1,)).astype(jnp.int32)
    halves = [g.reshape(N_CHIP, 2, g.shape[1] // 2, g.shape[2]) for g in gs]
    recv = rs_sibling(halves, name=f"{tag}_rs1")
    chip_sums = [add_kept_half(h, r, core, name=f"{tag}_add1_{j}") for j, (h, r) in enumerate(zip(halves, recv))]
    return (rs_chips_start(chip_sums, name=f"{tag}_rs2_start"),)


def reduce_scatter_finish(pending, after, tag):
    _, handle = pending
    xi, yi, ci = _place()
    chip = jnp.reshape(2 * xi + yi, (1,)).astype(jnp.int32)
    chip_sums, parts = rs_chips_wait(handle, after, name=f"{tag}_rs2_wait")
    halves = [add_chip_parts(h, p, chip, name=f"{tag}_add2_{j}") for j, (h, p) in enumerate(zip(chip_sums, parts))]
    others = sibling_swap(halves, name=f"{tag}_rs3")
    out = []
    for half, other in zip(halves, others):
        first, second = jnp.where(ci == 0, half, other), jnp.where(ci == 0, other, half)
        out.append(jnp.concatenate([first, second], axis=0).reshape(-1, half.shape[-1]))
    return out


def mod_fwd(c16, w_mod, *, name):
    nl, d, ns = w_mod.shape
    tn = _pick(ns, (512,))

    def body(c_ref, w_ref, o_ref):
        o_ref[...] = jnp.dot(jax.nn.silu(c_ref[...]), w_ref[...], precision=HI, preferred_element_type=F32)

    return pl.pallas_call(
        body, name=name, grid=(nl, ns // tn),
        in_specs=[pl.BlockSpec((16, d), lambda l, j: (0, 0)), pl.BlockSpec((None, d, tn), lambda l, j: (l, 0, j))],
        out_specs=pl.BlockSpec((None, 16, tn), lambda l, j: (l, 0, j)),
        out_shape=_sds((nl, 16, ns), F32), compiler_params=_cp(("parallel", "parallel")),
    )(c16, w_mod)


def mod_bwd_w(c16, dm, *, name):
    nl, _, ns = dm.shape
    d = c16.shape[1]
    tn = _pick(ns, (512,))

    def body(c_ref, dm_ref, o_ref):
        o_ref[...] = lax.dot_general(jax.nn.silu(c_ref[...]), dm_ref[...], (((0,), (0,)), ((), ())), precision=HI,
                                     preferred_element_type=F32)

    return pl.pallas_call(
        body, name=name, grid=(nl, ns // tn),
        in_specs=[pl.BlockSpec((16, d), lambda l, j: (0, 0)), pl.BlockSpec((None, 16, tn), lambda l, j: (l, 0, j))],
        out_specs=pl.BlockSpec((None, d, tn), lambda l, j: (l, 0, j)),
        out_shape=_sds((nl, d, ns), F32), compiler_params=_cp(("parallel", "parallel")),
    )(c16, dm)


def mod_bwd_s(dm, w_mod, *, name):
    nl, d, ns = w_mod.shape
    td = _pick(d, (512,))

    def body(dm_ref, w_ref, o_ref):
        part = lax.dot_general(dm_ref[...], w_ref[...], (((1,), (1,)), ((), ())), precision=HI, preferred_element_type=F32)
        rowsum = jnp.sum(part[8:16], axis=0, keepdims=True)

        @pl.when(pl.program_id(1) == 0)
        def _():
            o_ref[...] = jnp.zeros_like(o_ref)

        o_ref[...] += jnp.broadcast_to(rowsum, o_ref.shape)

    return pl.pallas_call(
        body, name=name, grid=(d // td, nl),
        in_specs=[pl.BlockSpec((None, 16, ns), lambda i, l: (l, 0, 0)), pl.BlockSpec((None, td, ns), lambda i, l: (l, i, 0))],
        out_specs=pl.BlockSpec((8, td), lambda i, l: (0, i)),
        out_shape=_sds((8, d), F32), compiler_params=_cp(("parallel", "arbitrary")),
    )(dm, w_mod)


def colsum16(dm, *, name):
    nl, _, n = dm.shape
    tn = _pick(n, (2048,))

    def body(dm_ref, o_ref):
        o_ref[...] = jnp.broadcast_to(jnp.sum(dm_ref[...], axis=0, keepdims=True), o_ref.shape)

    return pl.pallas_call(
        body, name=name, grid=(nl, n // tn),
        in_specs=[pl.BlockSpec((None, 16, tn), lambda l, j: (l, 0, j))],
        out_specs=pl.BlockSpec((None, 8, tn), lambda l, j: (l, 0, j)),
        out_shape=_sds((nl, 8, n), F32), compiler_params=_cp(("parallel", "parallel")),
    )(dm)


def silu_grad_mul(g, c, *, name):
    def body(g_ref, c_ref, o_ref):
        _, vjp = jax.vjp(jax.nn.silu, c_ref[...])
        o_ref[...] = vjp(g_ref[...])[0]

    return pl.pallas_call(body, name=name, out_shape=_sds(g.shape, F32))(g, c)


def adamw(w, g, m, v, *, name):
    shape = w.shape
    cols = shape[-1] if len(shape) > 1 else LANE
    flat = [a.reshape(-1, cols) for a in (w, g, m, v)]
    rows = flat[0].shape[0]
    tr = _pick(rows, (max(8, (1 << 18) // cols // 8 * 8),)) if rows % 8 == 0 else rows
    c1 = 1.0 - ADAM_B1 ** ADAM_STEP
    c2 = 1.0 - ADAM_B2 ** ADAM_STEP

    def body(w_ref, g_ref, m_ref, v_ref, d_ref, nm_ref, nv_ref):
        gv = g_ref[...]
        nm = ADAM_B1 * m_ref[...] + (1.0 - ADAM_B1) * gv
        nv = ADAM_B2 * v_ref[...] + (1.0 - ADAM_B2) * (gv * gv)
        d_ref[...] = -ADAM_LR * ((nm / c1) / (jnp.sqrt(nv / c2) + ADAM_EPS) + ADAM_WD * w_ref[...])
        nm_ref[...] = nm
        nv_ref[...] = nv

    blk = pl.BlockSpec((tr, cols), lambda i: (i, 0))
    outs = pl.pallas_call(
        body, name=name, grid=(rows // tr,), in_specs=[blk] * 4, out_specs=(blk,) * 3,
        out_shape=(_sds((rows, cols), F32),) * 3, compiler_params=_cp(("parallel",)),
    )(*flat)
    return tuple(o.reshape(shape) for o in outs)


PACK_ELEMS = LANE * LANE


def _pack(arrs):
    flat = jnp.concatenate([a.reshape(-1).astype(F32) for a in arrs])
    return jnp.pad(flat, (0, (-flat.shape[0]) % PACK_ELEMS)).reshape(-1, LANE)


def _unpack(packed, shapes):
    flat = packed.reshape(-1)
    out, pos = [], 0
    for s in shapes:
        n = math.prod(s)
        out.append(flat[pos:pos + n].reshape(s))
        pos += n
    return out


def _chip_cols(a, chip, width):
    return lax.dynamic_slice_in_dim(a, chip * width, width, axis=a.ndim - 1)


def kernel(x, c, ctx, c_ctx, w_mod, b_mod, norm_w, w_ffn_in, w_ffn_out, e_w_in, e_conv_w, e_conv_b, e_dt_bias, e_a_log, e_d_skip, e_ssd_norm_w, e_sgu_w, e_sgu_b, e_w_out, o_w_qkv, o_sink, o_w_out, loss_target, m_c_ctx, m_w_mod, m_b_mod, m_norm_w, m_w_ffn_in, m_w_ffn_out, m_e_w_in, m_e_conv_w, m_e_conv_b, m_e_dt_bias, m_e_a_log, m_e_d_skip, m_e_ssd_norm_w, m_e_sgu_w, m_e_sgu_b, m_e_w_out, m_o_w_qkv, m_o_sink, m_o_w_out, v_c_ctx, v_w_mod, v_b_mod, v_norm_w, v_w_ffn_in, v_w_ffn_out, v_e_w_in, v_e_conv_w, v_e_conv_b, v_e_dt_bias, v_e_a_log, v_e_d_skip, v_e_ssd_norm_w, v_e_sgu_w, v_e_sgu_b, v_e_w_out, v_o_w_qkv, v_o_sink, v_o_w_out):
    xi, yi, ci = _place()
    chip = 2 * xi + yi
    me = 2 * chip + ci
    s, d = x.shape[1:]
    tc = ctx.shape[1]
    depth = w_mod.shape[0]
    n_even = e_w_in.shape[0]
    dq = norm_w.shape[-1]
    cq = e_conv_w.shape[-1]
    ns = w_mod.shape[-1]

    gath = allgather8(_pack([c, norm_w, e_conv_w]), name="ag_small").reshape(N_DEV, -1)
    c_all = gath[:, :d]
    per_chip = [_unpack(gath[2 * k, d:], [norm_w.shape, e_conv_w.shape]) for k in range(N_CHIP)]
    nw_full = jnp.concatenate([pc[0] for pc in per_chip], axis=-1)
    convw_full = jnp.concatenate([pc[1] for pc in per_chip], axis=-1)
    c16 = jnp.concatenate([c_all, jnp.broadcast_to(c_ctx[None], (8, d))], axis=0)

    mod_g = allgather8(mod_fwd(c16, w_mod, name="mod_fwd"), name="ag_mod")
    mod_all = jnp.concatenate([mod_g[2 * k] for k in range(N_CHIP)], axis=-1) + b_mod[:, None, :]
    mod_rows = jnp.stack([mod_all[:, 8], lax.dynamic_index_in_dim(mod_all, me, axis=1, keepdims=False)], axis=1)
    modtab = jnp.pad(mod_rows.reshape(depth, 2, 6, d), ((0, 0), (0, 0), (0, 2), (0, 0)))

    eps_ = [even_params(convw_full[i], e_conv_b[i], e_dt_bias[i], e_a_log[i], e_d_skip[i], e_ssd_norm_w[i], e_sgu_w[i], e_sgu_b[i])
            for i in range(n_even)]
    sinks = [sink_rows(o_sink[i]) for i in range(o_sink.shape[0])]
    cos, sin = rope_tables(tc, s)
    units = [(kind, l) for l in range(depth) for kind in ("mix", "ffn")]

    def unit_shards(kind, l):
        if kind == "ffn":
            return [w_ffn_in[l], w_ffn_out[l]]
        return [e_w_in[l // 2], e_w_out[l // 2]] if l % 2 == 0 else [o_w_qkv[l // 2], o_w_out[l // 2]]

    def unit_weights(kind, l, gathered):
        w_a, w_b = gathered
        if kind == "mix" and l % 2 == 0:
            w_a = even_cols_permute(jnp.moveaxis(w_a[:, 0], 0, 1).reshape(1, d, -1))[None]
        return w_a, w_b

    def unit_fwd(kind, l, u_in, mt, wts):
        nw = nw_full[l]
        w_a, w_b = wts
        if kind == "mix":
            h1 = norm_mod_fwd(u_in, nw[0], mt, tc, 0, name=f"L{l}_norm1")
            if l % 2 == 0:
                o, ms = even_mixer_fwd(h1, w_a, w_b, 0, eps_[l // 2], tc, f"L{l}_mix")
            else:
                o, ms = odd_mixer_fwd(h1, w_a, w_b, 0, sinks[l // 2], cos, sin, tc, f"L{l}_mix")
            return resid_fwd(u_in, o, nw[1], mt, tc, 0, name=f"L{l}_res1"), (u_in, h1, ms, o)
        h2 = norm_mod_fwd(u_in, nw[2], mt, tc, 1, name=f"L{l}_norm2")
        p = mm_nn(h2, w_a, 0, "n", name=f"L{l}_ffn_in")
        a = swiglu_fwd(p, name=f"L{l}_swiglu")
        f = mm_nn(a, w_b, 0, "k", name=f"L{l}_ffn_out")
        return resid_fwd(u_in, f, nw[3], mt, tc, 1, name=f"L{l}_res2"), (u_in, h2, p, a, f)

    def unit_bwd(kind, l, du_out, mt, wts, sv):
        nw = nw_full[l]
        w_a, w_b = wts
        zeros = lambda w: jnp.zeros(w.shape, BF16)
        if kind == "ffn":
            u1, h2, p, a, f = sv
            df, acc_r = resid_bwd(f, nw[3], mt, du_out, tc, 1, name=f"L{l}_res2_b")
            g_b = mm_tn(a, df, zeros(w_b), 0, "k", name=f"L{l}_ffn_out_dw")
            da = mm_nt(df, w_b, 0, "k", name=f"L{l}_ffn_out_dx")
            dp = swiglu_bwd(p, da, name=f"L{l}_swiglu_b")
            dh2 = mm_nt(dp, w_a, 0, "n", name=f"L{l}_ffn_in_dx")
            g_a = mm_tn(h2, dp, zeros(w_a), 0, "n", name=f"L{l}_ffn_in_dw")
            du_in, acc_n = norm_mod_bwd(u1, nw[2], mt, dh2, du_out, tc, 1, name=f"L{l}_norm2_b")
            return du_in, [g_a[:, 0], g_b[:, 0]], (acc_n, acc_r), None
        u0, h1, ms, o = sv
        do, acc_r = resid_bwd(o, nw[1], mt, du_out, tc, 0, name=f"L{l}_res1_b")
        if l % 2 == 0:
            dh1, g_a, g_b, small = even_mixer_bwd(ms, do, h1, w_a, w_b, zeros(w_a), zeros(w_b), 0, eps_[l // 2], tc, f"L{l}_mix")
            g_a = jnp.moveaxis(even_cols_unpermute(g_a[0, 0]).reshape(d, N_CHIP, -1), 1, 0)
        else:
            dh1, g_a, g_b, small = odd_mixer_bwd(ms, do, h1, w_a, w_b, zeros(w_a), zeros(w_b), 0, sinks[l // 2], cos, sin, tc,
                                                 f"L{l}_mix")
            g_a = g_a[:, 0]
        du_in, acc_n = norm_mod_bwd(u0, nw[0], mt, dh1, du_out, tc, 0, name=f"L{l}_norm1_b")
        return du_in, [g_a, g_b[:, 0]], (acc_n, acc_r), small

    u = jnp.concatenate([ctx[0], x[0]], axis=0)
    shards = unit_shards(*units[0])
    handle = ag_send_start(unit_blocks(shards, ci), name="ag0_start")
    lands = ag_send_wait(handle, handle[4], name="ag0_wait")
    wts = [None] * len(units)
    wts[0] = unit_weights(*units[0], gather_finish(lands, unit_blocks(shards, 1 - ci), "ag0"))
    saved = [None] * len(units)
    prev = u
    for i, (kind, l) in enumerate(units):
        tok = 0.0
        if i + 1 < len(units):
            shards = unit_shards(*units[i + 1])
            blks, _ = lax.optimization_barrier((unit_blocks(shards, ci), prev))
            handle = ag_send_start(blks, name=f"ag{i + 1}_start")
            tok = handle[4][0, 0]
        prev = u
        u, saved[i] = unit_fwd(kind, l, u, modtab[l] + tok, wts[i])
        if i + 1 < len(units):
            lands = ag_send_wait(handle, u, name=f"ag{i + 1}_wait")
            wts[i + 1] = unit_weights(*units[i + 1], gather_finish(lands, unit_blocks(shards, 1 - ci), f"ag{i + 1}"))
    loss_part, du = loss_fwd_bwd(u, loss_target[0], tc, name="loss")
    loss = lax.psum(loss_part[0, 0], ("x", "y", "c"))

    accs, smalls, unit_grads = [None] * len(units), [None] * len(units), [None] * len(units)
    pending = None
    for i in reversed(range(len(units))):
        kind, l = units[i]
        tok = pending[1][4][0, 0] if pending is not None else 0.0
        du, gs, accs[i], smalls[i] = unit_bwd(kind, l, du, modtab[l] + tok, wts[i], saved[i])
        if pending is not None:
            unit_grads[pending[0]] = reduce_scatter_finish(pending, du, f"rs{pending[0]}")
        pending = (i,) + reduce_scatter_start(gs, f"rs{i}")
    grad_x = du[tc:][None]
    d_nw, d_mt = [None] * depth, [None] * depth
    for l in range(depth):
        (acc0, acc1), (acc2, acc3) = accs[2 * l], accs[2 * l + 1]
        d_nw[l] = jnp.stack([acc[0, 0] + acc[1, 0] for acc in (acc0, acc1, acc2, acc3)])
        d_mt[l] = jnp.stack([acc0[:, 1], acc0[:, 2], acc1[:, 1], acc2[:, 1], acc2[:, 2], acc3[:, 1]], axis=1)
    small_e = [smalls[2 * l] for l in range(0, depth, 2)]
    small_o = [smalls[2 * l] for l in range(1, depth, 2)]

    d_mt_all = jnp.stack(d_mt) + pending[1][4][0, 0]
    dmt_g = allgather8(jnp.pad(d_mt_all, ((0, 0), (0, 0), (0, 2), (0, 0))), name="ag_dmod")[:, :, :, :6]
    dm16 = jnp.concatenate([dmt_g[:, :, 1].transpose(1, 0, 2, 3).reshape(depth, N_DEV, 6 * d),
                            dmt_g[:, :, 0].transpose(1, 0, 2, 3).reshape(depth, N_DEV, 6 * d)], axis=1)
    dm_sh = _chip_cols(dm16, chip, ns)
    grad_w_mod = mod_bwd_w(c16, dm_sh, name="mod_bwd_w")
    grad_b_mod = colsum16(dm16, name="mod_bwd_b")[:, 0]
    ds_cc = mod_bwd_s(dm_sh, w_mod, name="mod_bwd_s")[0]

    stack_e = lambda key: jnp.stack([se[key] for se in small_e])
    small_names = ["e_conv_b", "e_dt_bias", "e_a_log", "e_d_skip", "e_ssd_norm_w", "e_sgu_w", "e_sgu_b"]
    small_parts = [jnp.stack(d_nw), stack_e("e_conv_w")] + [stack_e(k) for k in small_names]
    small_parts += [jnp.stack([so["o_sink"] for so in small_o]), 0.5 * ds_cc]
    small_shapes = [a.shape for a in small_parts]
    small_sum = sum_slots(allgather8(_pack(small_parts), name="ag_small_grads"), F32, name="small_grads_sum")
    (g_nw, g_convw, g_convb, g_dtb, g_alog, g_dskip, g_ssdnw, g_sguw, g_sgub, g_sink, g_scc) = _unpack(small_sum, small_shapes)
    grad_c_ctx = silu_grad_mul(jnp.broadcast_to(g_scc[None], (8, d)), jnp.broadcast_to(c_ctx[None], (8, d)), name="c_ctx_grad")[0]
    grads = dict(
        c_ctx=grad_c_ctx, w_mod=grad_w_mod, b_mod=grad_b_mod, norm_w=_chip_cols(g_nw, chip, dq),
        e_conv_w=_chip_cols(g_convw, chip, cq), e_conv_b=g_convb, e_dt_bias=g_dtb.reshape(e_dt_bias.shape),
        e_a_log=g_alog.reshape(e_a_log.shape), e_d_skip=g_dskip, e_ssd_norm_w=g_ssdnw, e_sgu_w=g_sguw, e_sgu_b=g_sgub,
        o_sink=g_sink)

    unit_grads[pending[0]] = reduce_scatter_finish(pending, grad_w_mod, f"rs{pending[0]}")
    grads["w_ffn_in"] = jnp.stack([unit_grads[2 * l + 1][0] for l in range(depth)])
    grads["w_ffn_out"] = jnp.stack([unit_grads[2 * l + 1][1] for l in range(depth)])
    grads["e_w_in"] = jnp.stack([unit_grads[2 * l][0] for l in range(0, depth, 2)])
    grads["e_w_out"] = jnp.stack([unit_grads[2 * l][1] for l in range(0, depth, 2)])
    grads["o_w_qkv"] = jnp.stack([unit_grads[2 * l][0] for l in range(1, depth, 2)])
    grads["o_w_out"] = jnp.stack([unit_grads[2 * l][1] for l in range(1, depth, 2)])

    weights = dict(c_ctx=c_ctx, w_mod=w_mod, b_mod=b_mod, norm_w=norm_w, w_ffn_in=w_ffn_in, w_ffn_out=w_ffn_out, e_w_in=e_w_in,
                   e_conv_w=e_conv_w, e_conv_b=e_conv_b, e_dt_bias=e_dt_bias, e_a_log=e_a_log, e_d_skip=e_d_skip,
                   e_ssd_norm_w=e_ssd_norm_w, e_sgu_w=e_sgu_w, e_sgu_b=e_sgu_b, e_w_out=e_w_out, o_w_qkv=o_w_qkv, o_sink=o_sink,
                   o_w_out=o_w_out)
    ms_ = dict(c_ctx=m_c_ctx, w_mod=m_w_mod, b_mod=m_b_mod, norm_w=m_norm_w, w_ffn_in=m_w_ffn_in, w_ffn_out=m_w_ffn_out,
               e_w_in=m_e_w_in, e_conv_w=m_e_conv_w, e_conv_b=m_e_conv_b, e_dt_bias=m_e_dt_bias, e_a_log=m_e_a_log,
               e_d_skip=m_e_d_skip, e_ssd_norm_w=m_e_ssd_norm_w, e_sgu_w=m_e_sgu_w, e_sgu_b=m_e_sgu_b, e_w_out=m_e_w_out,
               o_w_qkv=m_o_w_qkv, o_sink=m_o_sink, o_w_out=m_o_w_out)
    vs_ = dict(c_ctx=v_c_ctx, w_mod=v_w_mod, b_mod=v_b_mod, norm_w=v_norm_w, w_ffn_in=v_w_ffn_in, w_ffn_out=v_w_ffn_out,
               e_w_in=v_e_w_in, e_conv_w=v_e_conv_w, e_conv_b=v_e_conv_b, e_dt_bias=v_e_dt_bias, e_a_log=v_e_a_log,
               e_d_skip=v_e_d_skip, e_ssd_norm_w=v_e_ssd_norm_w, e_sgu_w=v_e_sgu_w, e_sgu_b=v_e_sgu_b, e_w_out=v_e_w_out,
               o_w_qkv=v_o_w_qkv, o_sink=v_o_sink, o_w_out=v_o_w_out)
    names = list(weights)
    big = ("w_mod", "w_ffn_in", "w_ffn_out", "e_w_in", "e_w_out", "o_w_qkv", "o_w_out")
    small = [n for n in names if n not in big]
    delta, new_m, new_v = {}, {}, {}
    for n in big:
        delta[n], new_m[n], new_v[n] = adamw(weights[n], grads[n], ms_[n], vs_[n], name=f"adamw_{n}")
    packed = adamw(*[_pack([tab[n] for n in small]) for tab in (weights, grads, ms_, vs_)], name="adamw_small")
    shapes = [weights[n].shape for n in small]
    for tab, pk in zip((delta, new_m, new_v), packed):
        for n, val in zip(small, _unpack(pk, shapes)):
            tab[n] = val
    return (loss, grad_x, *[grads[n] for n in names], *[delta[n] for n in names], *[new_m[n] for n in names],
            *[new_v[n] for n in names])
```

```python
import functools
import itertools
import math

import jax
import jax.numpy as jnp
from jax import lax
from jax.experimental import pallas as pl
from jax.experimental.pallas import tpu as pltpu

F32 = jnp.float32
BF16 = jnp.bfloat16
HI = lax.Precision.HIGHEST

NORM_EPS = 1e-6
SSD_HEAD_DIM = 64
SSD_STATE = 128
CHUNK = 128
CONV_K = 5
ATT_HEAD_DIM = 128
ATT_GROUP = 4
ROPE_BASE = 10000.0
GRID_W = 64
NEG_INF = -1e30
ADAM_LR, ADAM_B1, ADAM_B2, ADAM_EPS, ADAM_WD, ADAM_STEP = 0.001, 0.9, 0.999, 1e-08, 0.01, 10

LANE = 128
VMEM_LIMIT = 56 * 1024 * 1024
MESH_ID = pl.DeviceIdType.MESH
N_DEV = 8
N_CHIP = 4


def _cp(sem=None):
    return pltpu.CompilerParams(dimension_semantics=sem, vmem_limit_bytes=VMEM_LIMIT)


def _sds(shape, dtype):
    return jax.ShapeDtypeStruct(tuple(shape), dtype)


def _pick(n, cands):
    for c in cands:
        if n % c == 0:
            return c
    for step in (LANE, 16, 8):
        for c in range(min(n, cands[0]) // step * step, 0, -step):
            if n % c == 0:
                return c
    raise ValueError((n, cands))


def _w_index(blocked, layer, per_block_k, per_block_n):
    def idx(kblk, nblk):
        if blocked == "n":
            return (nblk // per_block_n, layer, kblk, nblk % per_block_n)
        return (kblk // per_block_k, layer, kblk % per_block_k, nblk)
    return idx


def mm_nn(a, w, layer, blocked, *, name, out_dtype=F32, tm=None, tn=None, tk=None):
    m, k_total = a.shape
    cb, _, kd, nd = w.shape
    n_total = nd * cb if blocked == "n" else nd
    assert k_total == (kd if blocked == "n" else kd * cb)
    tm = tm or _pick(m, (1088, 192))
    tn = tn or _pick(nd, (1408, 768, 512))
    tk = tk or _pick(kd, (2048, 1408, 512))
    nk = k_total // tk
    widx = _w_index(blocked, layer, kd // tk, nd // tn)

    def body(a_ref, w_ref, o_ref, acc_ref):
        kk = pl.program_id(2)
        part = jnp.dot(a_ref[...].astype(BF16), w_ref[...].astype(BF16), preferred_element_type=F32)

        @pl.when(kk == 0)
        def _():
            acc_ref[...] = part

        @pl.when(kk > 0)
        def _():
            acc_ref[...] += part

        @pl.when(kk == nk - 1)
        def _():
            o_ref[...] = acc_ref[...].astype(o_ref.dtype)

    return pl.pallas_call(
        body, name=name, grid=(m // tm, n_total // tn, nk),
        in_specs=[pl.BlockSpec((tm, tk), lambda i, j, k: (i, k)),
                  pl.BlockSpec((None, None, tk, tn), lambda i, j, k: widx(k, j))],
        out_specs=pl.BlockSpec((tm, tn), lambda i, j, k: (i, j)),
        out_shape=_sds((m, n_total), out_dtype),
        scratch_shapes=[pltpu.VMEM((tm, tn), F32)],
        compiler_params=_cp(("parallel", "parallel", "arbitrary")),
    )(a, w)


def mm_nt(dy, w, layer, blocked, *, name, out_dtype=F32, tm=None, tn=None, tk=None):
    m, n_total = dy.shape
    cb, _, kd, nd = w.shape
    k_total = kd if blocked == "n" else kd * cb
    assert n_total == (nd * cb if blocked == "n" else nd)
    tm = tm or _pick(m, (1088, 192))
    tn = tn or _pick(kd, (1024, 1408, 512))
    tk = tk or _pick(nd, (1408, 1024, 768))
    nk = n_total // tk
    widx = _w_index(blocked, layer, kd // tn, nd // tk)

    def body(a_ref, w_ref, o_ref, acc_ref):
        kk = pl.program_id(2)
        part = lax.dot_general(a_ref[...].astype(BF16), w_ref[...].astype(BF16), (((1,), (1,)), ((), ())),
                               preferred_element_type=F32)

        @pl.when(kk == 0)
        def _():
            acc_ref[...] = part

        @pl.when(kk > 0)
        def _():
            acc_ref[...] += part

        @pl.when(kk == nk - 1)
        def _():
            o_ref[...] = acc_ref[...].astype(o_ref.dtype)

    return pl.pallas_call(
        body, name=name, grid=(m // tm, k_total // tn, nk),
        in_specs=[pl.BlockSpec((tm, tk), lambda i, j, k: (i, k)),
                  pl.BlockSpec((None, None, tn, tk), lambda i, j, k: widx(j, k))],
        out_specs=pl.BlockSpec((tm, tn), lambda i, j, k: (i, j)),
        out_shape=_sds((m, k_total), out_dtype),
        scratch_shapes=[pltpu.VMEM((tm, tn), F32)],
        compiler_params=_cp(("parallel", "parallel", "arbitrary")),
    )(dy, w)


def mm_tn(x, dy, g, layer, blocked, *, name, tm=None, tn=None, tt=None):
    t_total, k_total = x.shape
    n_total = dy.shape[1]
    cb, _, kd, nd = g.shape
    assert k_total == (kd if blocked == "n" else kd * cb) and n_total == (nd * cb if blocked == "n" else nd)
    tm = tm or _pick(kd, (1024, 1408, 512))
    tn = tn or _pick(nd, (1408, 768, 512))
    tt = tt or _pick(t_total, (1088, 96))
    nt = t_total // tt
    widx = _w_index(blocked, layer, kd // tm, nd // tn)

    def body(x_ref, dy_ref, g_in, o_ref, acc_ref):
        del g_in
        tstep = pl.program_id(2)
        part = lax.dot_general(x_ref[...].astype(BF16), dy_ref[...].astype(BF16), (((0,), (0,)), ((), ())),
                               preferred_element_type=F32)

        @pl.when(tstep == 0)
        def _():
            acc_ref[...] = part

        @pl.when(tstep > 0)
        def _():
            acc_ref[...] += part

        @pl.when(tstep == nt - 1)
        def _():
            o_ref[...] = acc_ref[...].astype(o_ref.dtype)

    return pl.pallas_call(
        body, name=name, grid=(k_total // tm, n_total // tn, nt),
        in_specs=[pl.BlockSpec((tt, tm), lambda i, j, t: (t, i)),
                  pl.BlockSpec((tt, tn), lambda i, j, t: (t, j)),
                  pl.BlockSpec(memory_space=pl.ANY)],
        out_specs=pl.BlockSpec((None, None, tm, tn), lambda i, j, t: widx(i, j)),
        out_shape=_sds(g.shape, g.dtype),
        scratch_shapes=[pltpu.VMEM((tm, tn), F32)],
        input_output_aliases={2: 0},
        compiler_params=_cp(("parallel", "parallel", "arbitrary")),
    )(x, dy, g)


def _rms(x, w):
    return x * lax.rsqrt(jnp.mean(x * x, axis=-1, keepdims=True) + NORM_EPS) * w


def _row_tile(tc):
    return 256 if tc % 256 == 0 else 128


def _seg_spec(nct, d):
    return pl.BlockSpec((None, 8, d), lambda i: (jnp.minimum(i // nct, 1), 0, 0))


def _acc_rows(acc_ref, i, nct, rows):
    @pl.when((i == 0) | (i == nct))
    def _():
        acc_ref[...] = jnp.zeros_like(acc_ref)

    for r, val in enumerate(rows):
        acc_ref[r:r + 1, :] += val


def norm_mod_fwd(u, nw, modtab, tc, which, *, name):
    t, d = u.shape
    tr = _row_tile(tc)
    nct = tc // tr
    r0 = 3 * which

    def body(u_ref, nw_ref, mt_ref, h_ref):
        sh, sc = mt_ref[r0:r0 + 1, :], mt_ref[r0 + 1:r0 + 2, :]
        h_ref[...] = (_rms(u_ref[...], nw_ref[...]) * (1.0 + sc) + sh).astype(h_ref.dtype)

    return pl.pallas_call(
        body, name=name, grid=(t // tr,),
        in_specs=[pl.BlockSpec((tr, d), lambda i: (i, 0)), pl.BlockSpec((1, d), lambda i: (0, 0)), _seg_spec(nct, d)],
        out_specs=pl.BlockSpec((tr, d), lambda i: (i, 0)),
        out_shape=_sds((t, d), BF16), compiler_params=_cp(("arbitrary",)),
    )(u, nw.reshape(1, d), modtab)


def norm_mod_bwd(u, nw, modtab, dh, du_in, tc, which, *, name):
    t, d = u.shape
    tr = _row_tile(tc)
    nct = tc // tr
    r0 = 3 * which

    def body(u_ref, nw_ref, mt_ref, dh_ref, dui_ref, du_ref, acc_ref):
        i = pl.program_id(0)
        sh, sc = mt_ref[r0:r0 + 1, :], mt_ref[r0 + 1:r0 + 2, :]
        _, vjp = jax.vjp(lambda x, w, a, b: _rms(x, w) * (1.0 + b) + a, u_ref[...], nw_ref[...], sh, sc)
        dx, dw, dsh, dsc = vjp(dh_ref[...].astype(F32))
        du_ref[...] = dui_ref[...] + dx
        _acc_rows(acc_ref, i, nct, (dw, dsh, dsc))

    row = pl.BlockSpec((tr, d), lambda i: (i, 0))
    return pl.pallas_call(
        body, name=name, grid=(t // tr,),
        in_specs=[row, pl.BlockSpec((1, d), lambda i: (0, 0)), _seg_spec(nct, d), row, row],
        out_specs=(row, _seg_spec(nct, d)),
        out_shape=(_sds((t, d), F32), _sds((2, 8, d), F32)), compiler_params=_cp(("arbitrary",)),
    )(u, nw.reshape(1, d), modtab, dh, du_in)


def resid_fwd(u, o, nw, modtab, tc, which, *, name):
    t, d = u.shape
    tr = _row_tile(tc)
    nct = tc // tr
    r0 = 3 * which + 2

    def body(u_ref, o_ref, nw_ref, mt_ref, out_ref):
        out_ref[...] = u_ref[...] + mt_ref[r0:r0 + 1, :] * _rms(o_ref[...], nw_ref[...])

    row = pl.BlockSpec((tr, d), lambda i: (i, 0))
    return pl.pallas_call(
        body, name=name, grid=(t // tr,),
        in_specs=[row, row, pl.BlockSpec((1, d), lambda i: (0, 0)), _seg_spec(nct, d)],
        out_specs=row, out_shape=_sds((t, d), F32), compiler_params=_cp(("arbitrary",)),
    )(u, o, nw.reshape(1, d), modtab)


def resid_bwd(o, nw, modtab, du, tc, which, *, name):
    t, d = o.shape
    tr = _row_tile(tc)
    nct = tc // tr
    r0 = 3 * which + 2

    def body(o_ref, nw_ref, mt_ref, du_ref, do_ref, acc_ref):
        i = pl.program_id(0)
        _, vjp = jax.vjp(lambda x, w, g: g * _rms(x, w), o_ref[...], nw_ref[...], mt_ref[r0:r0 + 1, :])
        dx, dw, dg = vjp(du_ref[...])
        do_ref[...] = dx.astype(do_ref.dtype)
        _acc_rows(acc_ref, i, nct, (dw, dg))

    row = pl.BlockSpec((tr, d), lambda i: (i, 0))
    return pl.pallas_call(
        body, name=name, grid=(t // tr,),
        in_specs=[row, pl.BlockSpec((1, d), lambda i: (0, 0)), _seg_spec(nct, d), row],
        out_specs=(row, _seg_spec(nct, d)),
        out_shape=(_sds((t, d), BF16), _sds((2, 8, d), F32)), compiler_params=_cp(("arbitrary",)),
    )(o, nw.reshape(1, d), modtab, du)


def swiglu_fwd(p, *, name):
    t, h2 = p.shape
    h = h2 // 2
    tr = CHUNK

    def body(p_ref, a_ref):
        a_ref[...] = (jax.nn.silu(p_ref[:, :h].astype(F32)) * p_ref[:, h:].astype(F32)).astype(a_ref.dtype)

    return pl.pallas_call(
        body, name=name, grid=(t // tr,),
        in_specs=[pl.BlockSpec((tr, h2), lambda i: (i, 0))], out_specs=pl.BlockSpec((tr, h), lambda i: (i, 0)),
        out_shape=_sds((t, h), BF16), compiler_params=_cp(("parallel",)),
    )(p)


def swiglu_bwd(p, da, *, name):
    t, h2 = p.shape
    h = h2 // 2
    tr = CHUNK

    def body(p_ref, da_ref, dp_ref):
        _, vjp = jax.vjp(lambda g, u: jax.nn.silu(g) * u, p_ref[:, :h].astype(F32), p_ref[:, h:].astype(F32))
        dg, du = vjp(da_ref[...].astype(F32))
        dp_ref[:, :h] = dg.astype(dp_ref.dtype)
        dp_ref[:, h:] = du.astype(dp_ref.dtype)

    return pl.pallas_call(
        body, name=name, grid=(t // tr,),
        in_specs=[pl.BlockSpec((tr, h2), lambda i: (i, 0)), pl.BlockSpec((tr, h), lambda i: (i, 0))],
        out_specs=pl.BlockSpec((tr, h2), lambda i: (i, 0)),
        out_shape=_sds((t, h2), BF16), compiler_params=_cp(("parallel",)),
    )(p, da)


def loss_fwd_bwd(u, target, tc, *, name):
    t, d = u.shape
    tr = _row_tile(tc)
    nct = tc // tr

    def body(u_ref, t_ref, loss_ref, du_ref):
        i = pl.program_id(0)

        @pl.when(i == 0)
        def _():
            loss_ref[...] = jnp.zeros_like(loss_ref)

        @pl.when(i < nct)
        def _():
            du_ref[...] = jnp.zeros_like(du_ref)

        @pl.when(i >= nct)
        def _():
            err = u_ref[...] - t_ref[...]
            du_ref[...] = err * (1.0 / d)
            loss_ref[...] += jnp.sum(jnp.sum(err * err, axis=1, keepdims=True), axis=0, keepdims=True) * (0.5 / d)

    return pl.pallas_call(
        body, name=name, grid=(t // tr,),
        in_specs=[pl.BlockSpec((tr, d), lambda i: (i, 0)), pl.BlockSpec((tr, d), lambda i: (jnp.maximum(i - nct, 0), 0))],
        out_specs=(pl.BlockSpec((1, 1), lambda i: (0, 0)), pl.BlockSpec((tr, d), lambda i: (i, 0))),
        out_shape=(_sds((1, 1), F32), _sds((t, d), F32)), compiler_params=_cp(("arbitrary",)),
    )(u, target)


SSD_INNER = 1024
SGU_WIDTH = 1024
XBC_DIM = 1536
EVEN_COLS = 4640
EVEN_PAD_COLS = 5120
Z_BLK, U_BLK, V_BLK, X_BLK, B_BLK, C_BLK, DT_BLK = 0, 8, 16, 24, 32, 34, 36
PAD_ROWS = 8


def _conv_scratch_fill(pad_ref, val, tc, s):
    pad_ref[...] = jnp.zeros_like(pad_ref)
    pad_ref[PAD_ROWS:PAD_ROWS + tc, :] = val[:tc]
    pad_ref[2 * PAD_ROWS + tc:2 * PAD_ROWS + tc + s, :] = val[tc:]


def _conv_taps(pad_ref, tc, s, k):
    off = k - CONV_K // 2
    return (pad_ref[PAD_ROWS + off:PAD_ROWS + off + tc, :],
            pad_ref[2 * PAD_ROWS + tc + off:2 * PAD_ROWS + tc + off + s, :])


def conv_fwd(p, wb, tc, *, name):
    t = p.shape[0]
    s = t - tc
    nblk = XBC_DIM // LANE

    def body(p_ref, wb_ref, out_ref, pad_ref):
        _conv_scratch_fill(pad_ref, p_ref[...], tc, s)
        acc_c = jnp.zeros((tc, LANE), F32) + wb_ref[5:6, :]
        acc_l = jnp.zeros((s, LANE), F32) + wb_ref[5:6, :]
        for k in range(CONV_K):
            xc, xl = _conv_taps(pad_ref, tc, s, k)
            acc_c += xc * wb_ref[k:k + 1, :]
            acc_l += xl * wb_ref[k:k + 1, :]
        out_ref[:tc, :] = jax.nn.silu(acc_c)
        out_ref[tc:, :] = jax.nn.silu(acc_l)

    return pl.pallas_call(
        body, name=name, grid=(nblk,),
        in_specs=[pl.BlockSpec((t, LANE), lambda j: (0, X_BLK + j)), pl.BlockSpec((8, LANE), lambda j: (0, j))],
        out_specs=pl.BlockSpec((t, LANE), lambda j: (0, j)),
        out_shape=_sds((t, XBC_DIM), F32),
        scratch_shapes=[pltpu.VMEM((t + 3 * PAD_ROWS, LANE), F32)],
        compiler_params=_cp(("parallel",)),
    )(p, wb)


def conv_bwd(p, wb, dxg, dskip, tc, *, name):
    t = p.shape[0]
    s = t - tc
    nblk = XBC_DIM // LANE
    nx = SSD_INNER // LANE

    def grp(j):
        return jnp.where(j < nx, j // 4, (j - nx) % 2)

    def sub(j):
        return jnp.where(j < nx, j % 4, 4 + (j - nx) // 2)

    def body(p_ref, wb_ref, d0_ref, d1_ref, ds_ref, dp_ref, dwb_ref, pad_ref, dpad_ref):
        j = pl.program_id(0)
        _conv_scratch_fill(pad_ref, p_ref[...], tc, s)
        pre = [jnp.zeros((tc, LANE), F32) + wb_ref[5:6, :], jnp.zeros((s, LANE), F32) + wb_ref[5:6, :]]
        for k in range(CONV_K):
            xc, xl = _conv_taps(pad_ref, tc, s, k)
            pre[0] += xc * wb_ref[k:k + 1, :]
            pre[1] += xl * wb_ref[k:k + 1, :]
        dx = d0_ref[...] + d1_ref[...] + jnp.where(j < nx, ds_ref[...], 0.0)
        dpre = []
        for part, rows in ((0, slice(0, tc)), (1, slice(tc, t))):
            sig = jax.nn.sigmoid(pre[part])
            dpre.append(dx[rows] * (sig * (1.0 + pre[part] * (1.0 - sig))))
        dwb_ref[...] = jnp.zeros_like(dwb_ref)
        dwb_ref[5:6, :] = jnp.sum(dpre[0], axis=0, keepdims=True) + jnp.sum(dpre[1], axis=0, keepdims=True)
        for k in range(CONV_K):
            xc, xl = _conv_taps(pad_ref, tc, s, k)
            dwb_ref[k:k + 1, :] = (jnp.sum(dpre[0] * xc, axis=0, keepdims=True)
                                   + jnp.sum(dpre[1] * xl, axis=0, keepdims=True))
        _conv_scratch_fill(dpad_ref, jnp.concatenate(dpre, axis=0), tc, s)
        acc_c = jnp.zeros((tc, LANE), F32)
        acc_l = jnp.zeros((s, LANE), F32)
        for k in range(CONV_K):
            gc, gl = _conv_taps(dpad_ref, tc, s, CONV_K - 1 - k)
            acc_c += gc * wb_ref[k:k + 1, :]
            acc_l += gl * wb_ref[k:k + 1, :]
        dp_ref[:tc, :] = acc_c.astype(dp_ref.dtype)
        dp_ref[tc:, :] = acc_l.astype(dp_ref.dtype)

    col = pl.BlockSpec((t, LANE), lambda j: (0, j))
    return pl.pallas_call(
        body, name=name, grid=(nblk,),
        in_specs=[pl.BlockSpec((t, LANE), lambda j: (0, X_BLK + j)), pl.BlockSpec((8, LANE), lambda j: (0, j)),
                  pl.BlockSpec((None, None, t, LANE), lambda j: (0, grp(j), 0, sub(j))),
                  pl.BlockSpec((None, None, t, LANE), lambda j: (1, grp(j), 0, sub(j))),
                  pl.BlockSpec((t, LANE), lambda j: (0, jnp.minimum(j, nx - 1)))],
        out_specs=(col, pl.BlockSpec((8, LANE), lambda j: (0, j))),
        out_shape=(_sds((t, XBC_DIM), BF16), _sds((8, XBC_DIM), F32)),
        scratch_shapes=[pltpu.VMEM((t + 3 * PAD_ROWS, LANE), F32), pltpu.VMEM((t + 3 * PAD_ROWS, LANE), F32)],
        compiler_params=_cp(("parallel",)),
    )(p, wb, dxg, dxg, dskip)


HEADS_PER_DG = 8


def _ssd_prep_fn(pre, bias, alog, rev):
    q = pre.shape[0]
    lane = lax.broadcasted_iota(jnp.int32, (1, LANE), 1)
    dt = jnp.where(lane < HEADS_PER_DG, jax.nn.softplus(pre + bias), 0.0)
    row = lax.broadcasted_iota(jnp.int32, (q, q), 0)
    col = lax.broadcasted_iota(jnp.int32, (q, q), 1)
    tri = jnp.where((col - row) * jnp.where(rev, 1, -1) >= 0, 1.0, 0.0)
    cs = jnp.dot(tri, dt * (-jnp.exp(alog)), precision=HI, preferred_element_type=F32)
    return dt, cs


def _scan_chunk(nc_ctx, nch):
    def idx(dg, i):
        fwd = i
        bwd = jnp.where(i < nc_ctx, nc_ctx - 1 - i, nch - 1 - (i - nc_ctx))
        return jnp.where(dg // 2 == 0, fwd, bwd)
    return idx


def ssd_prep_fwd(pre, bias, alog, *, name):
    _, t, _ = pre.shape
    blk = pl.BlockSpec((None, CHUNK, LANE), lambda dg, i: (dg, i, 0))
    par = pl.BlockSpec((None, 1, LANE), lambda dg, i: (dg, 0, 0))

    def body(pre_ref, b_ref, a_ref, dt_ref, cs_ref):
        dt, cs = _ssd_prep_fn(pre_ref[...], b_ref[...], a_ref[...], pl.program_id(0) // 2 == 1)
        dt_ref[...] = dt
        cs_ref[...] = cs

    return pl.pallas_call(
        body, name=name, grid=(4, t // CHUNK), in_specs=[blk, par, par], out_specs=(blk, blk),
        out_shape=(_sds(pre.shape, F32), _sds(pre.shape, F32)), compiler_params=_cp(("parallel", "parallel")),
    )(pre, bias, alog)


def ssd_prep_bwd(pre, bias, alog, ddt, dcs, *, name):
    _, t, _ = pre.shape
    blk = pl.BlockSpec((None, CHUNK, LANE), lambda dg, i: (dg, i, 0))
    par = pl.BlockSpec((None, 1, LANE), lambda dg, i: (dg, 0, 0))

    def body(pre_ref, b_ref, a_ref, ddt_ref, dcs_ref, dpre_ref, acc_ref):
        rev = pl.program_id(0) // 2 == 1
        _, vjp = jax.vjp(lambda x, b, a: _ssd_prep_fn(x, b, a, rev), pre_ref[...], b_ref[...], a_ref[...])
        dpre, db, da = vjp((ddt_ref[...], dcs_ref[...]))
        dpre_ref[...] = dpre

        @pl.when(pl.program_id(1) == 0)
        def _():
            acc_ref[...] = jnp.zeros_like(acc_ref)

        acc_ref[0:1, :] += db
        acc_ref[1:2, :] += da

    return pl.pallas_call(
        body, name=name, grid=(4, t // CHUNK), in_specs=[blk, par, par, blk, blk],
        out_specs=(blk, pl.BlockSpec((None, 8, LANE), lambda dg, i: (dg, 0, 0))),
        out_shape=(_sds(pre.shape, F32), _sds((4, 8, LANE), F32)), compiler_params=_cp(("parallel", "arbitrary")),
    )(pre, bias, alog, ddt, dcs)


def _onehot_col(a, h):
    lane = lax.broadcasted_iota(jnp.int32, (1, a.shape[1]), 1)
    return jnp.sum(jnp.where(lane == h, a, 0.0), axis=1, keepdims=True)


def _onehot_row(a, h):
    sub = lax.broadcasted_iota(jnp.int32, (a.shape[0], 1), 0)
    return jnp.sum(jnp.where(sub == h, a, 0.0), axis=0, keepdims=True)


def _bdot(a, b, dims):
    return lax.dot_general(a.astype(BF16), b.astype(BF16), (dims, ((), ())), preferred_element_type=F32)


def _ssd_pair(xblk, dt, cs, bm, cm, sp, rev, pair):
    q = xblk.shape[0]
    lane = lax.broadcasted_iota(jnp.int32, (1, LANE), 1)
    sub = lax.broadcasted_iota(jnp.int32, (LANE, 1), 0)
    row = lax.broadcasted_iota(jnp.int32, (q, q), 0)
    col = lax.broadcasted_iota(jnp.int32, (q, q), 1)
    mask = (col - row) * jnp.where(rev, 1, -1) >= 0
    last = jnp.where(rev, 0, q - 1)
    cs_t = cs.T
    g = _bdot(cm, bm, ((1,), (1,)))
    y = jnp.zeros((q, LANE), F32)
    escale = jnp.zeros((q, LANE), F32)
    xw = jnp.zeros((q, LANE), F32)
    dec = jnp.zeros((LANE, 1), F32)
    for hh in range(2):
        h = 2 * pair + hh
        c_col = _onehot_col(cs, h)
        c_row = _onehot_row(cs_t, h)
        tot = jnp.sum(jnp.where(lax.broadcasted_iota(jnp.int32, (1, q), 1) == last, c_row, 0.0), axis=1, keepdims=True)
        in_head = (lane >= hh * SSD_HEAD_DIM) & (lane < (hh + 1) * SSD_HEAD_DIM)
        xh = jnp.where(in_head, xblk * _onehot_col(dt, h), 0.0)
        ldec = jnp.where(mask, jnp.exp(jnp.where(mask, c_col - c_row, 0.0)), 0.0)
        y = y + _bdot(g * ldec, xh, ((1,), (0,)))
        escale = escale + jnp.where(in_head, jnp.exp(c_col), 0.0)
        xw = xw + xh * jnp.exp(tot - c_col)
        dec = dec + jnp.where((sub >= hh * SSD_HEAD_DIM) & (sub < (hh + 1) * SSD_HEAD_DIM), jnp.exp(tot), 0.0)
    y = y + _bdot(cm, sp, ((1,), (1,))) * escale
    s_new = sp * dec + _bdot(xw, bm, ((0,), (0,)))
    return y, s_new


def ssd_fwd(xbc, dt, cs, tc, *, name):
    t = xbc.shape[0]
    nch = t // CHUNK
    sidx = _scan_chunk(tc // CHUNK, nch)
    gw = SSD_INNER // 2
    nb = SSD_INNER // LANE

    def body(x_ref, b_ref, c_ref, dt_ref, cs_ref, y_ref, sp_ref, s_ref):
        @pl.when(pl.program_id(1) == 0)
        def _():
            s_ref[...] = jnp.zeros_like(s_ref)

        rev = pl.program_id(0) // 2 == 1
        sp_ref[...] = s_ref[...]
        for p in range(gw // LANE):
            blk = slice(p * LANE, (p + 1) * LANE)
            y, s_new = _ssd_pair(x_ref[:, blk], dt_ref[...], cs_ref[...], b_ref[...], c_ref[...], s_ref[blk, :], rev, p)
            y_ref[:, blk] = y
            s_ref[blk, :] = s_new

    return pl.pallas_call(
        body, name=name, grid=(4, nch),
        in_specs=[pl.BlockSpec((CHUNK, gw), lambda dg, i: (sidx(dg, i), dg % 2)),
                  pl.BlockSpec((CHUNK, LANE), lambda dg, i: (sidx(dg, i), nb + dg % 2)),
                  pl.BlockSpec((CHUNK, LANE), lambda dg, i: (sidx(dg, i), nb + 2 + dg % 2)),
                  pl.BlockSpec((None, CHUNK, LANE), lambda dg, i: (dg, sidx(dg, i), 0)),
                  pl.BlockSpec((None, CHUNK, LANE), lambda dg, i: (dg, sidx(dg, i), 0))],
        out_specs=(pl.BlockSpec((None, CHUNK, gw), lambda dg, i: (dg // 2, sidx(dg, i), dg % 2)),
                   pl.BlockSpec((None, None, gw, SSD_STATE), lambda dg, i: (dg, sidx(dg, i), 0, 0))),
        out_shape=(_sds((2, t, SSD_INNER), F32), _sds((4, nch, gw, SSD_STATE), F32)),
        scratch_shapes=[pltpu.VMEM((gw, SSD_STATE), F32)],
        compiler_params=_cp(("parallel", "arbitrary")),
    )(xbc, xbc, xbc, dt, cs)


def ssd_bwd(xbc, dt, cs, sprev, dy, tc, *, name):
    t = xbc.shape[0]
    nch = t // CHUNK
    fidx = _scan_chunk(tc // CHUNK, nch)
    sidx = lambda dg, i: fidx(dg, nch - 1 - i)
    gw = SSD_INNER // 2
    nb = SSD_INNER // LANE

    def body(x_ref, b_ref, c_ref, dt_ref, cs_ref, sp_ref, dy_ref, dxg_ref, ddt_ref, dcs_ref, ds_ref):
        @pl.when(pl.program_id(1) == 0)
        def _():
            ds_ref[...] = jnp.zeros_like(ds_ref)

        rev = pl.program_id(0) // 2 == 1
        ddt = jnp.zeros((CHUNK, LANE), F32)
        dcs = jnp.zeros((CHUNK, LANE), F32)
        db = jnp.zeros((CHUNK, SSD_STATE), F32)
        dc = jnp.zeros((CHUNK, SSD_STATE), F32)
        for p in range(gw // LANE):
            blk = slice(p * LANE, (p + 1) * LANE)
            _, vjp = jax.vjp(functools.partial(_ssd_pair, rev=rev, pair=p),
                             x_ref[:, blk], dt_ref[...], cs_ref[...], b_ref[...], c_ref[...], sp_ref[blk, :])
            dx, ddt_p, dcs_p, db_p, dc_p, dsp = vjp((dy_ref[:, blk], ds_ref[blk, :]))
            dxg_ref[:, blk] = dx
            ds_ref[blk, :] = dsp
            ddt, dcs, db, dc = ddt + ddt_p, dcs + dcs_p, db + db_p, dc + dc_p
        dxg_ref[:, gw:gw + SSD_STATE] = db
        dxg_ref[:, gw + SSD_STATE:] = dc
        ddt_ref[...] = ddt
        dcs_ref[...] = dcs

    hd = pl.BlockSpec((None, CHUNK, LANE), lambda dg, i: (dg, sidx(dg, i), 0))
    return pl.pallas_call(
        body, name=name, grid=(4, nch),
        in_specs=[pl.BlockSpec((CHUNK, gw), lambda dg, i: (sidx(dg, i), dg % 2)),
                  pl.BlockSpec((CHUNK, LANE), lambda dg, i: (sidx(dg, i), nb + dg % 2)),
                  pl.BlockSpec((CHUNK, LANE), lambda dg, i: (sidx(dg, i), nb + 2 + dg % 2)),
                  hd, hd,
                  pl.BlockSpec((None, None, gw, SSD_STATE), lambda dg, i: (dg, sidx(dg, i), 0, 0)),
                  pl.BlockSpec((CHUNK, gw), lambda dg, i: (sidx(dg, i), dg % 2))],
        out_specs=(pl.BlockSpec((None, None, CHUNK, gw + 2 * SSD_STATE), lambda dg, i: (dg // 2, dg % 2, sidx(dg, i), 0)),
                   hd, hd),
        out_shape=(_sds((2, 2, t, gw + 2 * SSD_STATE), F32), _sds((4, t, LANE), F32), _sds((4, t, LANE), F32)),
        scratch_shapes=[pltpu.VMEM((gw, SSD_STATE), F32)],
        compiler_params=_cp(("parallel", "arbitrary")),
    )(xbc, xbc, xbc, dt, cs, sprev, dy)


def _ssd_finish_fn(y0, y1, xs, z, dskip, nw):
    y = (y0 + y1 + xs * dskip) * jax.nn.silu(z)
    half = y.shape[1] // 2
    first = lax.broadcasted_iota(jnp.int32, (1, y.shape[1]), 1) < half
    sq = y * y
    m0 = jnp.sum(jnp.where(first, sq, 0.0), axis=1, keepdims=True) / half
    m1 = jnp.sum(jnp.where(first, 0.0, sq), axis=1, keepdims=True) / half
    return y * jnp.where(first, lax.rsqrt(m0 + NORM_EPS), lax.rsqrt(m1 + NORM_EPS)) * nw


def ssd_finish_fwd(y, xbc, p, dskip, nw, *, name):
    t = xbc.shape[0]
    tr = CHUNK
    w = SSD_INNER
    row = pl.BlockSpec((tr, w), lambda i: (i, 0))
    par = pl.BlockSpec((1, w), lambda i: (0, 0))

    def body(y0_ref, y1_ref, x_ref, z_ref, ds_ref, nw_ref, o_ref):
        o_ref[...] = _ssd_finish_fn(y0_ref[...], y1_ref[...], x_ref[...], z_ref[...], ds_ref[...], nw_ref[...]).astype(o_ref.dtype)

    return pl.pallas_call(
        body, name=name, grid=(t // tr,),
        in_specs=[pl.BlockSpec((None, tr, w), lambda i: (0, i, 0)), pl.BlockSpec((None, tr, w), lambda i: (1, i, 0)),
                  row, row, par, par],
        out_specs=row, out_shape=_sds((t, w), BF16), compiler_params=_cp(("parallel",)),
    )(y, y, xbc, p, dskip, nw)


def ssd_finish_bwd(y, xbc, p, dskip, nw, dout, *, name):
    t = xbc.shape[0]
    tr = CHUNK
    w = SSD_INNER
    row = pl.BlockSpec((tr, w), lambda i: (i, 0))
    par = pl.BlockSpec((1, w), lambda i: (0, 0))

    def body(y0_ref, y1_ref, x_ref, z_ref, ds_ref, nw_ref, do_ref, dy_ref, dx_ref, dz_ref, acc_ref):
        _, vjp = jax.vjp(_ssd_finish_fn, y0_ref[...], y1_ref[...], x_ref[...], z_ref[...], ds_ref[...], nw_ref[...])
        dy0, _, dx, dz, dds, dnw = vjp(do_ref[...])
        dy_ref[...] = dy0
        dx_ref[...] = dx
        dz_ref[...] = dz.astype(dz_ref.dtype)

        @pl.when(pl.program_id(0) == 0)
        def _():
            acc_ref[...] = jnp.zeros_like(acc_ref)

        acc_ref[0:1, :] += dds
        acc_ref[1:2, :] += dnw

    return pl.pallas_call(
        body, name=name, grid=(t // tr,),
        in_specs=[pl.BlockSpec((None, tr, w), lambda i: (0, i, 0)), pl.BlockSpec((None, tr, w), lambda i: (1, i, 0)),
                  row, row, par, par, row],
        out_specs=(row, row, row, pl.BlockSpec((8, w), lambda i: (0, 0))),
        out_shape=(_sds((t, w), F32), _sds((t, w), F32), _sds((t, w), BF16), _sds((8, w), F32)),
        compiler_params=_cp(("arbitrary",)),
    )(y, y, xbc, p, dskip, nw, dout)


SGU_GROUPS = 8


def _sgu_fn(us, vs, ws, bs):
    n = SGU_GROUPS * LANE
    vf = [jax.nn.gelu(v) for v in vs]
    mu = sum(jnp.sum(v, axis=1, keepdims=True) for v in vf) / n
    var = sum(jnp.sum(jnp.square(v - mu), axis=1, keepdims=True) for v in vf) / n
    rstd = lax.rsqrt(var + NORM_EPS)
    return tuple(jax.nn.gelu(u) * (_bdot(w, (v - mu) * rstd, ((1,), (0,))) + b) for u, v, w, b in zip(us, vf, ws, bs))


def sgu_fwd(p, w, b, *, name):
    t = p.shape[0]
    wd = SGU_WIDTH

    def body(u_ref, v_ref, w_ref, b_ref, o_ref):
        sl = [slice(g * LANE, (g + 1) * LANE) for g in range(SGU_GROUPS)]
        ys = _sgu_fn([u_ref[:, s] for s in sl], [v_ref[:, s] for s in sl], [w_ref[g] for g in range(SGU_GROUPS)],
                     [b_ref[g] for g in range(SGU_GROUPS)])
        for s, yv in zip(sl, ys):
            o_ref[:, s] = yv.astype(o_ref.dtype)

    return pl.pallas_call(
        body, name=name, grid=(t // CHUNK,),
        in_specs=[pl.BlockSpec((CHUNK, wd), lambda i: (i, U_BLK * LANE // wd)), pl.BlockSpec((CHUNK, wd), lambda i: (i, V_BLK * LANE // wd)),
                  pl.BlockSpec((SGU_GROUPS, CHUNK, CHUNK), lambda i: (0, 0, 0)), pl.BlockSpec((SGU_GROUPS, CHUNK, 1), lambda i: (0, 0, 0))],
        out_specs=pl.BlockSpec((CHUNK, wd), lambda i: (i, 0)),
        out_shape=_sds((t, wd), BF16), compiler_params=_cp(("parallel",)),
    )(p, p, w, b)


def sgu_bwd(p, w, b, dout, *, name):
    t = p.shape[0]
    wd = SGU_WIDTH

    def body(u_ref, v_ref, w_ref, b_ref, do_ref, duv_ref, dw_ref, db_ref):
        sl = [slice(g * LANE, (g + 1) * LANE) for g in range(SGU_GROUPS)]
        _, vjp = jax.vjp(_sgu_fn, [u_ref[:, s] for s in sl], [v_ref[:, s] for s in sl],
                         [w_ref[g] for g in range(SGU_GROUPS)], [b_ref[g] for g in range(SGU_GROUPS)])
        dus, dvs, dws, dbs = vjp(tuple(do_ref[:, s] for s in sl))

        @pl.when(pl.program_id(0) == 0)
        def _():
            dw_ref[...] = jnp.zeros_like(dw_ref)
            db_ref[...] = jnp.zeros_like(db_ref)

        for g, s in enumerate(sl):
            duv_ref[:, s] = dus[g].astype(duv_ref.dtype)
            duv_ref[:, slice(wd + g * LANE, wd + (g + 1) * LANE)] = dvs[g].astype(duv_ref.dtype)
            dw_ref[g] += dws[g]
            db_ref[g] += dbs[g]

    wspec = pl.BlockSpec((SGU_GROUPS, CHUNK, CHUNK), lambda i: (0, 0, 0))
    bspec = pl.BlockSpec((SGU_GROUPS, CHUNK, 1), lambda i: (0, 0, 0))
    return pl.pallas_call(
        body, name=name, grid=(t // CHUNK,),
        in_specs=[pl.BlockSpec((CHUNK, wd), lambda i: (i, U_BLK * LANE // wd)), pl.BlockSpec((CHUNK, wd), lambda i: (i, V_BLK * LANE // wd)),
                  wspec, bspec, pl.BlockSpec((CHUNK, wd), lambda i: (i, 1))],
        out_specs=(pl.BlockSpec((CHUNK, 2 * wd), lambda i: (i, 0)), wspec, bspec),
        out_shape=(_sds((t, 2 * wd), BF16), _sds(w.shape, F32), _sds(b.shape, F32)),
        compiler_params=_cp(("arbitrary",)),
    )(p, p, w, b, dout)


def even_cols_permute(w):
    z, xbc, dt, u, v = jnp.split(w, (1024, 2560, 2592, 3616), axis=-1)
    pad = jnp.zeros(w.shape[:-1] + (EVEN_PAD_COLS - EVEN_COLS,), w.dtype)
    return jnp.concatenate([z, u, v, xbc, dt, pad], axis=-1)


def even_cols_unpermute(w):
    z, u, v, xbc, dt = jnp.split(w[..., :EVEN_COLS], (1024, 2048, 3072, 4608), axis=-1)
    return jnp.concatenate([z, xbc, dt, u, v], axis=-1)


def _dt_cols(p):
    t = p.shape[0]
    d = p[:, DT_BLK * LANE:DT_BLK * LANE + 4 * HEADS_PER_DG].reshape(t, 4, HEADS_PER_DG).transpose(1, 0, 2)
    return jnp.pad(d, ((0, 0), (0, 0), (0, LANE - HEADS_PER_DG)))


def _heads_to_lanes(a):
    return jnp.pad(a.reshape(4, 1, HEADS_PER_DG), ((0, 0), (0, 0), (0, LANE - HEADS_PER_DG)))


def even_params(conv_w, conv_b, dt_bias, a_log, d_skip, ssd_nw, sgu_w, sgu_b):
    wb = jnp.concatenate([conv_w, conv_b[None], jnp.zeros((2, XBC_DIM), F32)], axis=0)
    return dict(wb=wb, dtb=_heads_to_lanes(dt_bias), alog=_heads_to_lanes(a_log),
                dskip=jnp.repeat(d_skip, SSD_HEAD_DIM)[None], ssd_nw=ssd_nw[None], sgu_w=sgu_w, sgu_b=sgu_b[..., None])


def even_mixer_fwd(h, w_in, w_out, li, ep, tc, tag):
    p = mm_nn(h, w_in, li, "n", name=f"{tag}_in")
    xbc = conv_fwd(p, ep["wb"], tc, name=f"{tag}_conv")
    pre = _dt_cols(p)
    dt, cs = ssd_prep_fwd(pre, ep["dtb"], ep["alog"], name=f"{tag}_prep")
    y, sprev = ssd_fwd(xbc, dt, cs, tc, name=f"{tag}_ssd")
    yssd = ssd_finish_fwd(y, xbc, p, ep["dskip"], ep["ssd_nw"], name=f"{tag}_fin")
    ysgu = sgu_fwd(p, ep["sgu_w"], ep["sgu_b"], name=f"{tag}_sgu")
    ymix = jnp.concatenate([yssd, ysgu], axis=1)
    o = mm_nn(ymix, w_out, li, "k", name=f"{tag}_out")
    return o, (p, xbc, pre, dt, cs, y, sprev, ymix)


def even_mixer_bwd(saved, do, h, w_in, w_out, g_in, g_out, li, ep, tc, tag):
    p, xbc, pre, dt, cs, y, sprev, ymix = saved
    t = h.shape[0]
    g_out = mm_tn(ymix, do, g_out, li, "k", name=f"{tag}_out_dw")
    dymix = mm_nt(do, w_out, li, "k", name=f"{tag}_out_dx")
    dy, dxskip, dz, acc_fin = ssd_finish_bwd(y, xbc, p, ep["dskip"], ep["ssd_nw"], dymix, name=f"{tag}_fin_b")
    duv, dsgu_w, dsgu_b = sgu_bwd(p, ep["sgu_w"], ep["sgu_b"], dymix, name=f"{tag}_sgu_b")
    dxg, ddt, dcs = ssd_bwd(xbc, dt, cs, sprev, dy, tc, name=f"{tag}_ssd_b")
    dpre, acc_prep = ssd_prep_bwd(pre, ep["dtb"], ep["alog"], ddt, dcs, name=f"{tag}_prep_b")
    dxbc, dwb = conv_bwd(p, ep["wb"], dxg, dxskip, tc, name=f"{tag}_conv_b")
    ddt_cols = dpre[:, :, :HEADS_PER_DG].transpose(1, 0, 2).reshape(t, 4 * HEADS_PER_DG).astype(BF16)
    ddt_cols = jnp.pad(ddt_cols, ((0, 0), (0, EVEN_PAD_COLS - DT_BLK * LANE - 4 * HEADS_PER_DG)))
    dp = jnp.concatenate([dz, duv, dxbc, ddt_cols], axis=1)
    dh = mm_nt(dp, w_in, li, "n", name=f"{tag}_in_dx")
    g_in = mm_tn(h, dp, g_in, li, "n", name=f"{tag}_in_dw")
    small = dict(
        e_conv_w=dwb[:CONV_K], e_conv_b=dwb[CONV_K],
        e_dt_bias=acc_prep[:, 0, :HEADS_PER_DG].reshape(2, 2 * HEADS_PER_DG),
        e_a_log=acc_prep[:, 1, :HEADS_PER_DG].reshape(2, 2 * HEADS_PER_DG),
        e_d_skip=acc_fin[0].reshape(-1, SSD_HEAD_DIM).sum(axis=1), e_ssd_norm_w=acc_fin[1],
        e_sgu_w=dsgu_w, e_sgu_b=dsgu_b[..., 0])
    return dh, g_in, g_out, small


ATT_HEADS = 16
ATT_KV = 4
Q_BLKS, K_BLKS = ATT_HEADS, ATT_KV


def rope_tables(tc, s):
    quarter = ATT_HEAD_DIM // 4
    pos = jnp.arange(s)
    inv = ROPE_BASE ** (-jnp.arange(quarter, dtype=F32) / quarter)
    a_row = (pos // GRID_W).astype(F32)[:, None] * inv
    a_col = (pos % GRID_W).astype(F32)[:, None] * inv
    cos = jnp.concatenate([jnp.cos(a_row)] * 2 + [jnp.cos(a_col)] * 2, axis=1)
    sin = jnp.concatenate([-jnp.sin(a_row), jnp.sin(a_row), -jnp.sin(a_col), jnp.sin(a_col)], axis=1)
    return (jnp.concatenate([jnp.ones((tc, ATT_HEAD_DIM), F32), cos], axis=0),
            jnp.concatenate([jnp.zeros((tc, ATT_HEAD_DIM), F32), sin], axis=0))


def _swap_halves(x):
    lane = lax.broadcasted_iota(jnp.int32, x.shape, 1)
    return jnp.where(lane % 64 < 32, pltpu.roll(x, 96, 1), pltpu.roll(x, 32, 1))


def rope_fwd(p, cos, sin, *, name):
    t = p.shape[0]
    tr = _pick(t, (1088, 640))
    scale = ATT_HEAD_DIM ** -0.5

    def body(p_ref, c_ref, s_ref, o_ref):
        x = p_ref[...]
        r = x * c_ref[...] + _swap_halves(x) * s_ref[...]
        o_ref[...] = (r * jnp.where(pl.program_id(1) < Q_BLKS, scale, 1.0)).astype(o_ref.dtype)

    tab = pl.BlockSpec((tr, LANE), lambda i, j: (i, 0))
    return pl.pallas_call(
        body, name=name, grid=(t // tr, Q_BLKS + K_BLKS),
        in_specs=[pl.BlockSpec((tr, LANE), lambda i, j: (i, j)), tab, tab],
        out_specs=pl.BlockSpec((tr, LANE), lambda i, j: (i, j)),
        out_shape=_sds((t, (Q_BLKS + K_BLKS) * LANE), BF16), compiler_params=_cp(("parallel", "parallel")),
    )(p, cos, sin)


def rope_bwd(dq, dk, dv, cos, sin, *, name):
    t = dq.shape[0]
    tr = _pick(t, (1088, 640))
    scale = ATT_HEAD_DIM ** -0.5

    def body(dq_ref, dk_ref, dv_ref, c_ref, s_ref, o_ref):
        j = pl.program_id(1)

        def unrot(g):
            return g * c_ref[...] + _swap_halves(g * s_ref[...])

        @pl.when(j < Q_BLKS)
        def _():
            o_ref[...] = (unrot(dq_ref[...]) * scale).astype(o_ref.dtype)

        @pl.when((j >= Q_BLKS) & (j < Q_BLKS + K_BLKS))
        def _():
            o_ref[...] = unrot(dk_ref[...]).astype(o_ref.dtype)

        @pl.when(j >= Q_BLKS + K_BLKS)
        def _():
            o_ref[...] = dv_ref[...].astype(o_ref.dtype)

    tab = pl.BlockSpec((tr, LANE), lambda i, j: (i, 0))
    return pl.pallas_call(
        body, name=name, grid=(t // tr, Q_BLKS + 2 * K_BLKS),
        in_specs=[pl.BlockSpec((tr, LANE), lambda i, j: (i, jnp.minimum(j, Q_BLKS - 1))),
                  pl.BlockSpec((None, tr, LANE), lambda i, j: (jnp.clip(j - Q_BLKS, 0, K_BLKS - 1), i, 0)),
                  pl.BlockSpec((None, tr, LANE), lambda i, j: (jnp.clip(j - Q_BLKS - K_BLKS, 0, K_BLKS - 1), i, 0)), tab, tab],
        out_specs=pl.BlockSpec((tr, LANE), lambda i, j: (i, j)),
        out_shape=_sds((t, (Q_BLKS + 2 * K_BLKS) * LANE), BF16), compiler_params=_cp(("parallel", "parallel")),
    )(dq, dk, dv, cos, sin)


def _attn_tile(q4, kp, kc, kn, vp, vc, vn, kx, vx, sinks, is_lat, has_prev, has_next):
    q = kp.shape[0]
    nq = q4.shape[0]
    row = jnp.bitwise_and(lax.broadcasted_iota(jnp.int32, (nq, q), 0), q - 1)
    col = lax.broadcasted_iota(jnp.int32, (nq, q), 1)
    m_prev = (col - row) >= (1 - has_prev) * q
    m_cur = (row - row) >= (1 - is_lat)
    m_next = (row - col) >= (1 - has_next) * q
    lane = lax.broadcasted_iota(jnp.int32, (1, LANE), 1)
    sink = jnp.concatenate([jnp.broadcast_to(jnp.sum(jnp.where(lane == 0, s, 0.0), axis=1, keepdims=True), (q, 1)) for s in sinks],
                           axis=0)
    s_p = jnp.where(m_prev, _bdot(q4, kp, ((1,), (1,))), NEG_INF)
    s_c = jnp.where(m_cur, _bdot(q4, kc, ((1,), (1,))), NEG_INF)
    s_n = jnp.where(m_next, _bdot(q4, kn, ((1,), (1,))), NEG_INF)
    s_x = _bdot(q4, kx, ((1,), (1,)))
    mx = [jnp.max(a, axis=1, keepdims=True) for a in (s_p, s_c, s_n, s_x)]
    m = lax.stop_gradient(jnp.maximum(jnp.maximum(jnp.maximum(mx[0], mx[1]), jnp.maximum(mx[2], mx[3])), sink))
    e = [jnp.exp(a - m) for a in (s_p, s_c, s_n, s_x)]
    inv = 1.0 / (sum(jnp.sum(a, axis=1, keepdims=True) for a in e) + jnp.exp(sink - m))
    return sum(_bdot(a * inv, v, ((1,), (0,))) for a, v in zip(e, (vp, vc, vn, vx)))


def _attn_specs(t, tc):
    nblk = t // CHUNK
    hw = ATT_GROUP * LANE
    kcol = lambda kv: Q_BLKS + kv
    vcol = lambda kv: Q_BLKS + K_BLKS + kv
    prev = lambda n: jnp.maximum(n - 1, 0)
    nxt = lambda n: jnp.minimum(n + 1, nblk - 1)
    blk = lambda rowf, colf: pl.BlockSpec((CHUNK, LANE), lambda kv, n: (rowf(n), colf(kv)))
    same = lambda n: n
    return [pl.BlockSpec((CHUNK, hw), lambda kv, n: (n, kv)),
            blk(prev, kcol), blk(same, kcol), blk(nxt, kcol), blk(prev, vcol), blk(same, vcol), blk(nxt, vcol),
            pl.BlockSpec((tc, LANE), lambda kv, n: (0, kcol(kv))), pl.BlockSpec((tc, LANE), lambda kv, n: (0, vcol(kv))),
            pl.BlockSpec((None, 8, LANE), lambda kv, n: (kv, 0, 0))]


def _attn_args(refs, n, nct, nblk):
    q_ref, kp, kc, kn, vp, vc, vn, kx, vx, sk = refs
    f = lambda r: r[...].astype(F32)
    q4 = _stack_heads(q_ref)
    sinks = [sk[g:g + 1, :] for g in range(ATT_GROUP)]
    flags = ((n >= nct).astype(jnp.int32), (n >= nct + 1).astype(jnp.int32), ((n >= nct) & (n + 1 < nblk)).astype(jnp.int32))
    return (q4, f(kp), f(kc), f(kn), f(vp), f(vc), f(vn), f(kx), f(vx), sinks), flags


def _stack_heads(ref):
    return jnp.concatenate([ref[:, g * LANE:(g + 1) * LANE].astype(F32) for g in range(ATT_GROUP)], axis=0)


def _unstack_heads(ref, val):
    for g in range(ATT_GROUP):
        ref[:, g * LANE:(g + 1) * LANE] = val[g * CHUNK:(g + 1) * CHUNK].astype(ref.dtype)


def attn_fwd(qk, p, sink, tc, *, name):
    t = qk.shape[0]
    nblk, nct = t // CHUNK, tc // CHUNK
    hw = ATT_GROUP * LANE

    def body(*refs):
        o_ref = refs[-1]
        args, flags = _attn_args(refs[:-1], pl.program_id(1), nct, nblk)
        _unstack_heads(o_ref, _attn_tile(*args, *flags))

    return pl.pallas_call(
        body, name=name, grid=(ATT_KV, nblk), in_specs=_attn_specs(t, tc),
        out_specs=pl.BlockSpec((CHUNK, hw), lambda kv, n: (n, kv)),
        out_shape=_sds((t, ATT_HEADS * LANE), BF16), compiler_params=_cp(("parallel", "parallel")),
    )(qk, qk, qk, qk, p, p, p, qk, p, sink)


def attn_bwd(qk, p, sink, do, tc, *, name):
    t = qk.shape[0]
    nblk, nct = t // CHUNK, tc // CHUNK
    hw = ATT_GROUP * LANE

    def body(*refs):
        do_ref, dq_ref, dk_ref, dv_ref, dsk_ref = refs[-5:]
        n = pl.program_id(1)
        args, flags = _attn_args(refs[:-5], n, nct, nblk)
        _, vjp = jax.vjp(lambda *a: _attn_tile(*a, *flags), *args)
        dq4, dkp, dkc, dkn, dvp, dvc, dvn, dkx, dvx, dsinks = vjp(_stack_heads(do_ref))

        @pl.when(n == 0)
        def _():
            dk_ref[...] = jnp.zeros_like(dk_ref)
            dv_ref[...] = jnp.zeros_like(dv_ref)
            dsk_ref[...] = jnp.zeros_like(dsk_ref)

        _unstack_heads(dq_ref, dq4)
        for g in range(ATT_GROUP):
            dsk_ref[g:g + 1, :] += dsinks[g]
        for blk, dkb, dvb in ((jnp.maximum(n - 1, 0), dkp, dvp), (n, dkc, dvc), (jnp.minimum(n + 1, nblk - 1), dkn, dvn)):
            rows = pl.ds(pl.multiple_of(blk * CHUNK, CHUNK), CHUNK)
            dk_ref[rows, :] += dkb
            dv_ref[rows, :] += dvb
        dk_ref[0:tc, :] += dkx
        dv_ref[0:tc, :] += dvx

    kvacc = pl.BlockSpec((None, t, LANE), lambda kv, n: (kv, 0, 0))
    return pl.pallas_call(
        body, name=name, grid=(ATT_KV, nblk),
        in_specs=_attn_specs(t, tc) + [pl.BlockSpec((CHUNK, hw), lambda kv, n: (n, kv))],
        out_specs=(pl.BlockSpec((CHUNK, hw), lambda kv, n: (n, kv)), kvacc, kvacc,
                   pl.BlockSpec((None, 8, LANE), lambda kv, n: (kv, 0, 0))),
        out_shape=(_sds((t, ATT_HEADS * LANE), F32), _sds((ATT_KV, t, LANE), F32), _sds((ATT_KV, t, LANE), F32),
                   _sds((ATT_KV, 8, LANE), F32)),
        compiler_params=_cp(("parallel", "arbitrary")),
    )(qk, qk, qk, qk, p, p, p, qk, p, sink, do)


def sink_rows(sink):
    s = jnp.broadcast_to(sink.reshape(ATT_KV, ATT_GROUP, 1), (ATT_KV, ATT_GROUP, LANE))
    return jnp.pad(s, ((0, 0), (0, 8 - ATT_GROUP), (0, 0)))


def odd_mixer_fwd(h, w_qkv, w_out, li, sink, cos, sin, tc, tag):
    p = mm_nn(h, w_qkv, li, "n", name=f"{tag}_qkv")
    qk = rope_fwd(p, cos, sin, name=f"{tag}_rope")
    att = attn_fwd(qk, p, sink, tc, name=f"{tag}_att")
    o = mm_nn(att, w_out, li, "k", name=f"{tag}_out")
    return o, (p, qk, att)


def odd_mixer_bwd(saved, do, h, w_qkv, w_out, g_qkv, g_out, li, sink, cos, sin, tc, tag):
    p, qk, att = saved
    g_out = mm_tn(att, do, g_out, li, "k", name=f"{tag}_out_dw")
    datt = mm_nt(do, w_out, li, "k", name=f"{tag}_out_dx")
    dq, dk, dv, dsink = attn_bwd(qk, p, sink, datt, tc, name=f"{tag}_att_b")
    dp = rope_bwd(dq, dk, dv, cos, sin, name=f"{tag}_rope_b")
    dh = mm_nt(dp, w_qkv, li, "n", name=f"{tag}_qkv_dx")
    g_qkv = mm_tn(h, dp, g_qkv, li, "n", name=f"{tag}_qkv_dw")
    return dh, g_qkv, g_out, dict(o_sink=dsink[:, :ATT_GROUP, 0].reshape(-1))


ANY = pl.BlockSpec(memory_space=pl.ANY)


def _place():
    return lax.axis_index("x"), lax.axis_index("y"), lax.axis_index("c")


DMA_PIECES = 16


def _pieces(shape):
    if len(shape) < 2:
        return [()]
    lead, k = shape[:-2], shape[-2]
    split = 1
    while math.prod(lead) * split < DMA_PIECES and k % (2 * split) == 0 and (k // (2 * split)) % 16 == 0:
        split *= 2
    rows = k // split
    out = []
    for li in itertools.product(*[range(n) for n in lead]):
        out += [li + (pl.ds(q * rows, rows),) for q in range(split)]
    return out


def _start_pieces(make, src, dst):
    for idx in _pieces(src.shape):
        make(src.at[idx] if idx else src, dst.at[idx] if idx else dst).start()


def allgather8(blk, *, name):
    def body(x_ref, out_ref, send_sems, recv_sems, local_sem):
        x, y, c = _place()
        me, sibling = (x, y, c), (x, y, 1 - c)
        chips = [(1 - x, y), (x, 1 - y), (1 - x, 1 - y)]

        def slot(px, py, pc):
            return out_ref.at[4 * px + 2 * py + pc]

        def remote(k, to):
            return lambda src, dst: pltpu.make_async_remote_copy(
                src_ref=src, dst_ref=dst, send_sem=send_sems.at[k], recv_sem=recv_sems.at[k], device_id=to, device_id_type=MESH_ID)

        def local(src, dst):
            return pltpu.make_async_copy(src, dst, local_sem)

        _start_pieces(local, x_ref, slot(*me))
        _start_pieces(remote(0, sibling), x_ref, slot(*me))
        for j, chip in enumerate(chips):
            remote(1 + j, (*chip, c))(x_ref, slot(*me)).start()
        for j, chip in enumerate(chips):
            blk = slot(*chip, c)
            remote(1 + j, me)(blk, blk).wait_recv()
            _start_pieces(remote(4 + j, sibling), blk, blk)
        remote(0, me)(slot(*sibling), slot(*sibling)).wait_recv()
        for j, chip in enumerate(chips):
            blk = slot(*chip, 1 - c)
            remote(4 + j, me)(blk, blk).wait_recv()
        remote(0, sibling)(x_ref, slot(*me)).wait_send()
        for j, chip in enumerate(chips):
            remote(1 + j, (*chip, c))(x_ref, slot(*me)).wait_send()
            remote(4 + j, sibling)(slot(*chip, c), slot(*chip, c)).wait_send()
        local(x_ref, slot(*me)).wait()

    return pl.pallas_call(
        body, name=name, out_shape=_sds((N_DEV,) + blk.shape, blk.dtype), in_specs=[ANY], out_specs=ANY,
        scratch_shapes=[pltpu.SemaphoreType.DMA((7,)), pltpu.SemaphoreType.DMA((7,)), pltpu.SemaphoreType.DMA],
        compiler_params=pltpu.CompilerParams(has_side_effects=True),
    )(blk)


def _flip(r, xi, yi):
    return (1 - xi if r & 2 else xi), (1 - yi if r & 1 else yi)


def _to_sibling(send_sem, recv_sem):
    x, y, c = _place()
    return lambda src, dst: pltpu.make_async_remote_copy(src_ref=src, dst_ref=dst, send_sem=send_sem, recv_sem=recv_sem,
                                                         device_id=(x, y, 1 - c), device_id_type=MESH_ID)


def rs_sibling(gs, *, name):
    n = len(gs)

    def body(*refs):
        g_refs, out_refs, (send_sems, recv_sems) = refs[:n], refs[n:2 * n], refs[2 * n:]
        c = lax.axis_index("c")
        copies = [(_to_sibling(send_sems.at[i], recv_sems.at[i]), g_ref.at[:, pl.ds(1 - c, 1)], out_ref)
                  for i, (g_ref, out_ref) in enumerate(zip(g_refs, out_refs))]
        for remote, src, dst in copies:
            _start_pieces(remote, src, dst)
        for remote, src, dst in copies:
            remote(src, dst).wait()

    return pl.pallas_call(
        body, name=name, out_shape=[_sds((g.shape[0], 1) + g.shape[2:], g.dtype) for g in gs],
        in_specs=[ANY] * n, out_specs=[ANY] * n, scratch_shapes=[pltpu.SemaphoreType.DMA((n,)), pltpu.SemaphoreType.DMA((n,))],
        compiler_params=pltpu.CompilerParams(has_side_effects=True),
    )(*gs)


def sibling_swap(halves, *, name):
    n = len(halves)

    def body(*refs):
        h_refs, out_refs, (send_sems, recv_sems) = refs[:n], refs[n:2 * n], refs[2 * n:]
        copies = [(_to_sibling(send_sems.at[i], recv_sems.at[i]), h_ref, out_ref)
                  for i, (h_ref, out_ref) in enumerate(zip(h_refs, out_refs))]
        for remote, src, dst in copies:
            _start_pieces(remote, src, dst)
        for remote, src, dst in copies:
            remote(src, dst).wait()

    return pl.pallas_call(
        body, name=name, out_shape=[_sds(h.shape, h.dtype) for h in halves], in_specs=[ANY] * n, out_specs=[ANY] * n,
        scratch_shapes=[pltpu.SemaphoreType.DMA((n,)), pltpu.SemaphoreType.DMA((n,))],
        compiler_params=pltpu.CompilerParams(has_side_effects=True),
    )(*halves)


HBM_SPEC = pl.BlockSpec(memory_space=pltpu.HBM)
SEM_SPEC = pl.BlockSpec(memory_space=pltpu.SEMAPHORE)
DATAFLOW = pltpu.SideEffectType.DATAFLOW_SIDE_EFFECTING


def _hbm(a):
    return pltpu.with_memory_space_constraint(a, pltpu.HBM)


def _split_start(srcs, land_shapes, starts, *, name):
    n = len(srcs)

    def body(*refs):
        src_refs, land_refs, (send_sem, recv_sem), token = refs[:n], refs[n:2 * n], refs[2 * n:2 * n + 2], refs[-1]
        starts(src_refs, land_refs, send_sem, recv_sem)
        token[...] = jnp.zeros_like(token)

    out = pl.pallas_call(
        body, name=name,
        out_shape=[pltpu.SemaphoreType.DMA(()), pltpu.SemaphoreType.DMA(())] + [pltpu.HBM(s.shape, s.dtype) for s in srcs]
        + [pltpu.HBM(shape, s.dtype) for shape, s in zip(land_shapes, srcs)] + [_sds((8, LANE), F32)],
        in_specs=[HBM_SPEC] * (2 * n), out_specs=[SEM_SPEC, SEM_SPEC] + [HBM_SPEC] * (2 * n) + [pl.BlockSpec(memory_space=pltpu.VMEM)],
        input_output_aliases={i: 2 + i for i in range(2 * n)},
        compiler_params=pltpu.CompilerParams(has_side_effects=DATAFLOW),
    )(*[_hbm(s) for s in srcs], *[_hbm(lax.empty(shape, s.dtype)) for shape, s in zip(land_shapes, srcs)])
    return out[0], out[1], out[2:2 + n], out[2 + n:2 + 2 * n], out[-1]


def _split_wait(handle, after, sent, landed, *, name):
    send_sem, recv_sem, srcs, lands, _ = handle
    n = len(srcs)

    def body(*refs):
        src_refs, land_refs, (send_sem, recv_sem) = refs[:n], refs[n:2 * n], refs[2 * n:2 * n + 2]
        x, y, c = _place()
        for sized, wait in ((sent, "wait_send"), (landed, "wait_recv")):
            for src_ref, land_ref in zip(src_refs, land_refs):
                ref = sized(src_ref, land_ref)
                getattr(pltpu.make_async_remote_copy(src_ref=ref, dst_ref=ref, send_sem=send_sem, recv_sem=recv_sem,
                                                     device_id=(x, y, c), device_id_type=MESH_ID), wait)()

    out = pl.pallas_call(
        body, name=name, out_shape=[pltpu.HBM(a.shape, a.dtype) for a in (*srcs, *lands)],
        in_specs=[HBM_SPEC] * (2 * n) + [SEM_SPEC, SEM_SPEC, ANY], out_specs=[HBM_SPEC] * (2 * n),
        input_output_aliases={i: i for i in range(2 * n)},
        compiler_params=pltpu.CompilerParams(has_side_effects=DATAFLOW),
    )(*srcs, *lands, send_sem, recv_sem, after)
    return out[:n], out[n:]


def ag_send_start(blks, *, name):
    def starts(src_refs, land_refs, send_sem, recv_sem):
        x, y, c = _place()
        me = 4 * x + 2 * y + c
        for to in ((x, y, 1 - c), (1 - x, y, c), (x, 1 - y, c), (1 - x, 1 - y, c)):
            for src_ref, land_ref in zip(src_refs, land_refs):
                pltpu.make_async_remote_copy(src_ref=src_ref, dst_ref=land_ref.at[me], send_sem=send_sem, recv_sem=recv_sem,
                                             device_id=to, device_id_type=MESH_ID).start()

    return _split_start(blks, [(N_DEV,) + b.shape for b in blks], starts, name=name)


def ag_send_wait(handle, after, *, name):
    four = lambda src_ref, land_ref: land_ref.at[pl.ds(0, 4)]
    return _split_wait(handle, after, four, four, name=name)[1]


def ag_forward(lands, sibling_blks, *, name):
    n = len(lands)

    def body(*refs):
        land_refs, blk_refs, out_refs, (send_sems, recv_sems) = refs[:n], refs[n:2 * n], refs[2 * n:3 * n], refs[3 * n:]
        x, y, c = _place()
        for i, (land_ref, blk_ref, out_ref) in enumerate(zip(land_refs, blk_refs, out_refs)):
            remote = _to_sibling(send_sems.at[i], recv_sems.at[i])
            for r in (1, 2, 3):
                px, py = _flip(r, x, y)
                slot = 4 * px + 2 * py + c
                _start_pieces(remote, land_ref.at[slot], out_ref.at[slot])
            _start_pieces(remote, blk_ref, out_ref.at[4 * x + 2 * y + 1 - c])
        for i, out_ref in enumerate(out_refs):
            four = out_ref.at[pl.ds(0, 4)]
            _to_sibling(send_sems.at[i], recv_sems.at[i])(four, four).wait()

    return pl.pallas_call(
        body, name=name, out_shape=[_sds(a.shape, a.dtype) for a in lands], in_specs=[ANY] * (2 * n), out_specs=[ANY] * n,
        scratch_shapes=[pltpu.SemaphoreType.DMA((n,)), pltpu.SemaphoreType.DMA((n,))], input_output_aliases={i: i for i in range(n)},
        compiler_params=pltpu.CompilerParams(has_side_effects=True),
    )(*lands, *sibling_blks)


def rs_chips_start(hs, *, name):
    def starts(src_refs, land_refs, send_sem, recv_sem):
        x, y, c = _place()
        for r in (1, 2, 3):
            px, py = _flip(r, x, y)
            for src_ref, land_ref in zip(src_refs, land_refs):
                pltpu.make_async_remote_copy(src_ref=src_ref.at[2 * px + py], dst_ref=land_ref.at[r - 1], send_sem=send_sem,
                                             recv_sem=recv_sem, device_id=(px, py, c), device_id_type=MESH_ID).start()

    return _split_start(hs, [(3,) + h.shape[1:] for h in hs], starts, name=name)


def rs_chips_wait(handle, after, *, name):
    return _split_wait(handle, after, lambda src_ref, land_ref: src_ref.at[pl.ds(0, 3)], lambda src_ref, land_ref: land_ref, name=name)


def _row_block(kd, nd):
    return _pick(kd, (max(32, (1 << 19) // nd // 32 * 32),))


def add_kept_half(g, recv, core, *, name):
    nchip, nl, kd, nd = g.shape
    lh = nl // 2
    tk = _row_block(kd, nd)

    def body(c_ref, g_ref, r_ref, o_ref):
        del c_ref
        o_ref[...] = (g_ref[...].astype(F32) + r_ref[...].astype(F32)).astype(o_ref.dtype)

    blk = lambda f: pl.BlockSpec((None, None, tk, nd), f)
    return pl.pallas_call(
        body, name=name, out_shape=_sds((nchip, lh, kd, nd), BF16),
        grid_spec=pltpu.PrefetchScalarGridSpec(
            num_scalar_prefetch=1, grid=(nchip, lh, kd // tk),
            in_specs=[blk(lambda j, l, i, c_ref: (j, c_ref[0] * lh + l, i, 0)), blk(lambda j, l, i, c_ref: (j, l, i, 0))],
            out_specs=blk(lambda j, l, i, c_ref: (j, l, i, 0))),
        compiler_params=_cp(("parallel", "parallel", "parallel")),
    )(core, g, recv)


def add_chip_parts(h, parts, chip, *, name):
    _, lh, kd, nd = h.shape
    tk = _row_block(kd, nd)

    def body(k_ref, h_ref, p0, p1, p2, o_ref):
        del k_ref
        o_ref[...] = h_ref[...].astype(F32) + p0[...].astype(F32) + p1[...].astype(F32) + p2[...].astype(F32)

    blk = lambda f: pl.BlockSpec((None, None, tk, nd), f)
    part = lambda r: blk(functools.partial(lambda r_, l, i, k_ref: (r_, l, i, 0), r))
    return pl.pallas_call(
        body, name=name, out_shape=_sds((lh, kd, nd), F32),
        grid_spec=pltpu.PrefetchScalarGridSpec(
            num_scalar_prefetch=1, grid=(lh, kd // tk),
            in_specs=[blk(lambda l, i, k_ref: (k_ref[0], l, i, 0)), part(0), part(1), part(2)],
            out_specs=pl.BlockSpec((None, tk, nd), lambda l, i, k_ref: (l, i, 0))),
        compiler_params=_cp(("parallel", "parallel")),
    )(chip, h, parts, parts, parts)


def sum_slots(a, out_dtype, *, name):
    n = a.shape[0]
    cols = a.shape[-1]
    a3 = a.reshape(n, -1, cols)
    rows = a3.shape[1]
    tr = _pick(rows, (max(32, (1 << 19) // cols // 32 * 32),))

    def body(*refs):
        acc = refs[0][...].astype(F32)
        for r in refs[1:n]:
            acc = acc + r[...].astype(F32)
        refs[n][...] = acc.astype(out_dtype)

    return pl.pallas_call(
        body, name=name, grid=(rows // tr,),
        in_specs=[pl.BlockSpec((None, tr, cols), functools.partial(lambda j, i: (j, i, 0), j)) for j in range(n)],
        out_specs=pl.BlockSpec((tr, cols), lambda i: (i, 0)),
        out_shape=_sds((rows, cols), out_dtype), compiler_params=_cp(("parallel",)),
    )(*([a3] * n)).reshape(a.shape[1:])


def unit_blocks(shards, ci):
    return [lax.dynamic_index_in_dim(w.reshape(2, w.shape[0] // 2, w.shape[1]), ci, axis=0, keepdims=False).astype(BF16)
            for w in shards]


def gather_finish(lands, sibling_blks, tag):
    full = ag_forward(lands, sibling_blks, name=f"{tag}_fwd")
    return [a.reshape(N_CHIP, 1, 2 * a.shape[1], a.shape[2]) for a in full]


def reduce_scatter_start(gs, tag):
    core = jnp.reshape(lax.axis_index("c"), (1,)).astype(jnp.int32)
    halves = [g.reshape(N_CHIP, 2, g.shape[1] // 2, g.shape[2]) for g in gs]
    recv = rs_sibling(halves, name=f"{tag}_rs1")
    chip_sums = [add_kept_half(h, r, core, name=f"{tag}_add1_{j}") for j, (h, r) in enumerate(zip(halves, recv))]
    return (rs_chips_start(chip_sums, name=f"{tag}_rs2_start"),)


def reduce_scatter_finish(pending, after, tag):
    _, handle = pending
    xi, yi, ci = _place()
    chip = jnp.reshape(2 * xi + yi, (1,)).astype(jnp.int32)
    chip_sums, parts = rs_chips_wait(handle, after, name=f"{tag}_rs2_wait")
    halves = [add_chip_parts(h, p, chip, name=f"{tag}_add2_{j}") for j, (h, p) in enumerate(zip(chip_sums, parts))]
    others = sibling_swap(halves, name=f"{tag}_rs3")
    out = []
    for half, other in zip(halves, others):
        first, second = jnp.where(ci == 0, half, other), jnp.where(ci == 0, other, half)
        out.append(jnp.concatenate([first, second], axis=0).reshape(-1, half.shape[-1]))
    return out


def mod_fwd(c16, w_mod, *, name):
    nl, d, ns = w_mod.shape
    tn = _pick(ns, (512,))

    def body(c_ref, w_ref, o_ref):
        o_ref[...] = jnp.dot(jax.nn.silu(c_ref[...]), w_ref[...], precision=HI, preferred_element_type=F32)

    return pl.pallas_call(
        body, name=name, grid=(nl, ns // tn),
        in_specs=[pl.BlockSpec((16, d), lambda l, j: (0, 0)), pl.BlockSpec((None, d, tn), lambda l, j: (l, 0, j))],
        out_specs=pl.BlockSpec((None, 16, tn), lambda l, j: (l, 0, j)),
        out_shape=_sds((nl, 16, ns), F32), compiler_params=_cp(("parallel", "parallel")),
    )(c16, w_mod)


def mod_bwd_w(c16, dm, *, name):
    nl, _, ns = dm.shape
    d = c16.shape[1]
    tn = _pick(ns, (512,))

    def body(c_ref, dm_ref, o_ref):
        o_ref[...] = lax.dot_general(jax.nn.silu(c_ref[...]), dm_ref[...], (((0,), (0,)), ((), ())), precision=HI,
                                     preferred_element_type=F32)

    return pl.pallas_call(
        body, name=name, grid=(nl, ns // tn),
        in_specs=[pl.BlockSpec((16, d), lambda l, j: (0, 0)), pl.BlockSpec((None, 16, tn), lambda l, j: (l, 0, j))],
        out_specs=pl.BlockSpec((None, d, tn), lambda l, j: (l, 0, j)),
        out_shape=_sds((nl, d, ns), F32), compiler_params=_cp(("parallel", "parallel")),
    )(c16, dm)


def mod_bwd_s(dm, w_mod, *, name):
    nl, d, ns = w_mod.shape
    td = _pick(d, (512,))

    def body(dm_ref, w_ref, o_ref):
        part = lax.dot_general(dm_ref[...], w_ref[...], (((1,), (1,)), ((), ())), precision=HI, preferred_element_type=F32)
        rowsum = jnp.sum(part[8:16], axis=0, keepdims=True)

        @pl.when(pl.program_id(1) == 0)
        def _():
            o_ref[...] = jnp.zeros_like(o_ref)

        o_ref[...] += jnp.broadcast_to(rowsum, o_ref.shape)

    return pl.pallas_call(
        body, name=name, grid=(d // td, nl),
        in_specs=[pl.BlockSpec((None, 16, ns), lambda i, l: (l, 0, 0)), pl.BlockSpec((None, td, ns), lambda i, l: (l, i, 0))],
        out_specs=pl.BlockSpec((8, td), lambda i, l: (0, i)),
        out_shape=_sds((8, d), F32), compiler_params=_cp(("parallel", "arbitrary")),
    )(dm, w_mod)


def colsum16(dm, *, name):
    nl, _, n = dm.shape
    tn = _pick(n, (2048,))

    def body(dm_ref, o_ref):
        o_ref[...] = jnp.broadcast_to(jnp.sum(dm_ref[...], axis=0, keepdims=True), o_ref.shape)

    return pl.pallas_call(
        body, name=name, grid=(nl, n // tn),
        in_specs=[pl.BlockSpec((None, 16, tn), lambda l, j: (l, 0, j))],
        out_specs=pl.BlockSpec((None, 8, tn), lambda l, j: (l, 0, j)),
        out_shape=_sds((nl, 8, n), F32), compiler_params=_cp(("parallel", "parallel")),
    )(dm)


def silu_grad_mul(g, c, *, name):
    def body(g_ref, c_ref, o_ref):
        _, vjp = jax.vjp(jax.nn.silu, c_ref[...])
        o_ref[...] = vjp(g_ref[...])[0]

    return pl.pallas_call(body, name=name, out_shape=_sds(g.shape, F32))(g, c)


def adamw(w, g, m, v, *, name):
    shape = w.shape
    cols = shape[-1] if len(shape) > 1 else LANE
    flat = [a.reshape(-1, cols) for a in (w, g, m, v)]
    rows = flat[0].shape[0]
    tr = _pick(rows, (max(8, (1 << 18) // cols // 8 * 8),)) if rows % 8 == 0 else rows
    c1 = 1.0 - ADAM_B1 ** ADAM_STEP
    c2 = 1.0 - ADAM_B2 ** ADAM_STEP

    def body(w_ref, g_ref, m_ref, v_ref, d_ref, nm_ref, nv_ref):
        gv = g_ref[...]
        nm = ADAM_B1 * m_ref[...] + (1.0 - ADAM_B1) * gv
        nv = ADAM_B2 * v_ref[...] + (1.0 - ADAM_B2) * (gv * gv)
        d_ref[...] = -ADAM_LR * ((nm / c1) / (jnp.sqrt(nv / c2) + ADAM_EPS) + ADAM_WD * w_ref[...])
        nm_ref[...] = nm
        nv_ref[...] = nv

    blk = pl.BlockSpec((tr, cols), lambda i: (i, 0))
    outs = pl.pallas_call(
        body, name=name, grid=(rows // tr,), in_specs=[blk] * 4, out_specs=(blk,) * 3,
        out_shape=(_sds((rows, cols), F32),) * 3, compiler_params=_cp(("parallel",)),
    )(*flat)
    return tuple(o.reshape(shape) for o in outs)


PACK_ELEMS = LANE * LANE


def _pack(arrs):
    flat = jnp.concatenate([a.reshape(-1).astype(F32) for a in arrs])
    return jnp.pad(flat, (0, (-flat.shape[0]) % PACK_ELEMS)).reshape(-1, LANE)


def _unpack(packed, shapes):
    flat = packed.reshape(-1)
    out, pos = [], 0
    for s in shapes:
        n = math.prod(s)
        out.append(flat[pos:pos + n].reshape(s))
        pos += n
    return out


def _chip_cols(a, chip, width):
    return lax.dynamic_slice_in_dim(a, chip * width, width, axis=a.ndim - 1)


def kernel(x, c, ctx, c_ctx, w_mod, b_mod, norm_w, w_ffn_in, w_ffn_out, e_w_in, e_conv_w, e_conv_b, e_dt_bias, e_a_log, e_d_skip, e_ssd_norm_w, e_sgu_w, e_sgu_b, e_w_out, o_w_qkv, o_sink, o_w_out, loss_target, m_c_ctx, m_w_mod, m_b_mod, m_norm_w, m_w_ffn_in, m_w_ffn_out, m_e_w_in, m_e_conv_w, m_e_conv_b, m_e_dt_bias, m_e_a_log, m_e_d_skip, m_e_ssd_norm_w, m_e_sgu_w, m_e_sgu_b, m_e_w_out, m_o_w_qkv, m_o_sink, m_o_w_out, v_c_ctx, v_w_mod, v_b_mod, v_norm_w, v_w_ffn_in, v_w_ffn_out, v_e_w_in, v_e_conv_w, v_e_conv_b, v_e_dt_bias, v_e_a_log, v_e_d_skip, v_e_ssd_norm_w, v_e_sgu_w, v_e_sgu_b, v_e_w_out, v_o_w_qkv, v_o_sink, v_o_w_out):
    xi, yi, ci = _place()
    chip = 2 * xi + yi
    me = 2 * chip + ci
    s, d = x.shape[1:]
    tc = ctx.shape[1]
    depth = w_mod.shape[0]
    n_even = e_w_in.shape[0]
    dq = norm_w.shape[-1]
    cq = e_conv_w.shape[-1]
    ns = w_mod.shape[-1]

    gath = allgather8(_pack([c, norm_w, e_conv_w]), name="ag_small").reshape(N_DEV, -1)
    c_all = gath[:, :d]
    per_chip = [_unpack(gath[2 * k, d:], [norm_w.shape, e_conv_w.shape]) for k in range(N_CHIP)]
    nw_full = jnp.concatenate([pc[0] for pc in per_chip], axis=-1)
    convw_full = jnp.concatenate([pc[1] for pc in per_chip], axis=-1)
    c16 = jnp.concatenate([c_all, jnp.broadcast_to(c_ctx[None], (8, d))], axis=0)

    mod_g = allgather8(mod_fwd(c16, w_mod, name="mod_fwd"), name="ag_mod")
    mod_all = jnp.concatenate([mod_g[2 * k] for k in range(N_CHIP)], axis=-1) + b_mod[:, None, :]
    mod_rows = jnp.stack([mod_all[:, 8], lax.dynamic_index_in_dim(mod_all, me, axis=1, keepdims=False)], axis=1)
    modtab = jnp.pad(mod_rows.reshape(depth, 2, 6, d), ((0, 0), (0, 0), (0, 2), (0, 0)))

    eps_ = [even_params(convw_full[i], e_conv_b[i], e_dt_bias[i], e_a_log[i], e_d_skip[i], e_ssd_norm_w[i], e_sgu_w[i], e_sgu_b[i])
            for i in range(n_even)]
    sinks = [sink_rows(o_sink[i]) for i in range(o_sink.shape[0])]
    cos, sin = rope_tables(tc, s)
    units = [(kind, l) for l in range(depth) for kind in ("mix", "ffn")]

    def unit_shards(kind, l):
        if kind == "ffn":
            return [w_ffn_in[l], w_ffn_out[l]]
        return [e_w_in[l // 2], e_w_out[l // 2]] if l % 2 == 0 else [o_w_qkv[l // 2], o_w_out[l // 2]]

    def unit_weights(kind, l, gathered):
        w_a, w_b = gathered
        if kind == "mix" and l % 2 == 0:
            w_a = even_cols_permute(jnp.moveaxis(w_a[:, 0], 0, 1).reshape(1, d, -1))[None]
        return w_a, w_b

    def unit_fwd(kind, l, u_in, mt, wts):
        nw = nw_full[l]
        w_a, w_b = wts
        if kind == "mix":
            h1 = norm_mod_fwd(u_in, nw[0], mt, tc, 0, name=f"L{l}_norm1")
            if l % 2 == 0:
                o, ms = even_mixer_fwd(h1, w_a, w_b, 0, eps_[l // 2], tc, f"L{l}_mix")
            else:
                o, ms = odd_mixer_fwd(h1, w_a, w_b, 0, sinks[l // 2], cos, sin, tc, f"L{l}_mix")
            return resid_fwd(u_in, o, nw[1], mt, tc, 0, name=f"L{l}_res1"), (u_in, h1, ms, o)
        h2 = norm_mod_fwd(u_in, nw[2], mt, tc, 1, name=f"L{l}_norm2")
        p = mm_nn(h2, w_a, 0, "n", name=f"L{l}_ffn_in", out_dtype=BF16)
        a = swiglu_fwd(p, name=f"L{l}_swiglu")
        f = mm_nn(a, w_b, 0, "k", name=f"L{l}_ffn_out")
        return resid_fwd(u_in, f, nw[3], mt, tc, 1, name=f"L{l}_res2"), (u_in, h2, p, a, f)

    def unit_bwd(kind, l, du_out, mt, wts, sv):
        nw = nw_full[l]
        w_a, w_b = wts
        zeros = lambda w: jnp.zeros(w.shape, BF16)
        if kind == "ffn":
            u1, h2, p, a, f = sv
            df, acc_r = resid_bwd(f, nw[3], mt, du_out, tc, 1, name=f"L{l}_res2_b")
            g_b = mm_tn(a, df, zeros(w_b), 0, "k", name=f"L{l}_ffn_out_dw")
            da = mm_nt(df, w_b, 0, "k", name=f"L{l}_ffn_out_dx", out_dtype=BF16)
            dp = swiglu_bwd(p, da, name=f"L{l}_swiglu_b")
            dh2 = mm_nt(dp, w_a, 0, "n", name=f"L{l}_ffn_in_dx")
            g_a = mm_tn(h2, dp, zeros(w_a), 0, "n", name=f"L{l}_ffn_in_dw")
            du_in, acc_n = norm_mod_bwd(u1, nw[2], mt, dh2, du_out, tc, 1, name=f"L{l}_norm2_b")
            return du_in, [g_a[:, 0], g_b[:, 0]], (acc_n, acc_r), None
        u0, h1, ms, o = sv
        do, acc_r = resid_bwd(o, nw[1], mt, du_out, tc, 0, name=f"L{l}_res1_b")
        if l % 2 == 0:
            dh1, g_a, g_b, small = even_mixer_bwd(ms, do, h1, w_a, w_b, zeros(w_a), zeros(w_b), 0, eps_[l // 2], tc, f"L{l}_mix")
            g_a = jnp.moveaxis(even_cols_unpermute(g_a[0, 0]).reshape(d, N_CHIP, -1), 1, 0)
        else:
            dh1, g_a, g_b, small = odd_mixer_bwd(ms, do, h1, w_a, w_b, zeros(w_a), zeros(w_b), 0, sinks[l // 2], cos, sin, tc,
                                                 f"L{l}_mix")
            g_a = g_a[:, 0]
        du_in, acc_n = norm_mod_bwd(u0, nw[0], mt, dh1, du_out, tc, 0, name=f"L{l}_norm1_b")
        return du_in, [g_a, g_b[:, 0]], (acc_n, acc_r), small

    u = jnp.concatenate([ctx[0], x[0]], axis=0)
    shards = unit_shards(*units[0])
    handle = ag_send_start(unit_blocks(shards, ci), name="ag0_start")
    lands = ag_send_wait(handle, handle[4], name="ag0_wait")
    wts = [None] * len(units)
    wts[0] = unit_weights(*units[0], gather_finish(lands, unit_blocks(shards, 1 - ci), "ag0"))
    saved = [None] * len(units)
    prev = u
    for i, (kind, l) in enumerate(units):
        tok = 0.0
        if i + 1 < len(units):
            shards = unit_shards(*units[i + 1])
            blks, _ = lax.optimization_barrier((unit_blocks(shards, ci), prev))
            handle = ag_send_start(blks, name=f"ag{i + 1}_start")
            tok = handle[4][0, 0]
        prev = u
        u, saved[i] = unit_fwd(kind, l, u, modtab[l] + tok, wts[i])
        if i + 1 < len(units):
            lands = ag_send_wait(handle, u, name=f"ag{i + 1}_wait")
            wts[i + 1] = unit_weights(*units[i + 1], gather_finish(lands, unit_blocks(shards, 1 - ci), f"ag{i + 1}"))
    loss_part, du = loss_fwd_bwd(u, loss_target[0], tc, name="loss")
    loss = lax.psum(loss_part[0, 0], ("x", "y", "c"))

    accs, smalls, unit_grads = [None] * len(units), [None] * len(units), [None] * len(units)
    pending = None
    for i in reversed(range(len(units))):
        kind, l = units[i]
        tok = pending[1][4][0, 0] if pending is not None else 0.0
        du, gs, accs[i], smalls[i] = unit_bwd(kind, l, du, modtab[l] + tok, wts[i], saved[i])
        if pending is not None:
            unit_grads[pending[0]] = reduce_scatter_finish(pending, du, f"rs{pending[0]}")
        pending = (i,) + reduce_scatter_start(gs, f"rs{i}")
    grad_x = du[tc:][None]
    d_nw, d_mt = [None] * depth, [None] * depth
    for l in range(depth):
        (acc0, acc1), (acc2, acc3) = accs[2 * l], accs[2 * l + 1]
        d_nw[l] = jnp.stack([acc[0, 0] + acc[1, 0] for acc in (acc0, acc1, acc2, acc3)])
        d_mt[l] = jnp.stack([acc0[:, 1], acc0[:, 2], acc1[:, 1], acc2[:, 1], acc2[:, 2], acc3[:, 1]], axis=1)
    small_e = [smalls[2 * l] for l in range(0, depth, 2)]
    small_o = [smalls[2 * l] for l in range(1, depth, 2)]

    d_mt_all = jnp.stack(d_mt) + pending[1][4][0, 0]
    dmt_g = allgather8(jnp.pad(d_mt_all, ((0, 0), (0, 0), (0, 2), (0, 0))), name="ag_dmod")[:, :, :, :6]
    dm16 = jnp.concatenate([dmt_g[:, :, 1].transpose(1, 0, 2, 3).reshape(depth, N_DEV, 6 * d),
                            dmt_g[:, :, 0].transpose(1, 0, 2, 3).reshape(depth, N_DEV, 6 * d)], axis=1)
    dm_sh = _chip_cols(dm16, chip, ns)
    grad_w_mod = mod_bwd_w(c16, dm_sh, name="mod_bwd_w")
    grad_b_mod = colsum16(dm16, name="mod_bwd_b")[:, 0]
    ds_cc = mod_bwd_s(dm_sh, w_mod, name="mod_bwd_s")[0]

    stack_e = lambda key: jnp.stack([se[key] for se in small_e])
    small_names = ["e_conv_b", "e_dt_bias", "e_a_log", "e_d_skip", "e_ssd_norm_w", "e_sgu_w", "e_sgu_b"]
    small_parts = [jnp.stack(d_nw), stack_e("e_conv_w")] + [stack_e(k) for k in small_names]
    small_parts += [jnp.stack([so["o_sink"] for so in small_o]), 0.5 * ds_cc]
    small_shapes = [a.shape for a in small_parts]
    small_sum = sum_slots(allgather8(_pack(small_parts), name="ag_small_grads"), F32, name="small_grads_sum")
    (g_nw, g_convw, g_convb, g_dtb, g_alog, g_dskip, g_ssdnw, g_sguw, g_sgub, g_sink, g_scc) = _unpack(small_sum, small_shapes)
    grad_c_ctx = silu_grad_mul(jnp.broadcast_to(g_scc[None], (8, d)), jnp.broadcast_to(c_ctx[None], (8, d)), name="c_ctx_grad")[0]
    grads = dict(
        c_ctx=grad_c_ctx, w_mod=grad_w_mod, b_mod=grad_b_mod, norm_w=_chip_cols(g_nw, chip, dq),
        e_conv_w=_chip_cols(g_convw, chip, cq), e_conv_b=g_convb, e_dt_bias=g_dtb.reshape(e_dt_bias.shape),
        e_a_log=g_alog.reshape(e_a_log.shape), e_d_skip=g_dskip, e_ssd_norm_w=g_ssdnw, e_sgu_w=g_sguw, e_sgu_b=g_sgub,
        o_sink=g_sink)

    unit_grads[pending[0]] = reduce_scatter_finish(pending, grad_w_mod, f"rs{pending[0]}")
    grads["w_ffn_in"] = jnp.stack([unit_grads[2 * l + 1][0] for l in range(depth)])
    grads["w_ffn_out"] = jnp.stack([unit_grads[2 * l + 1][1] for l in range(depth)])
    grads["e_w_in"] = jnp.stack([unit_grads[2 * l][0] for l in range(0, depth, 2)])
    grads["e_w_out"] = jnp.stack([unit_grads[2 * l][1] for l in range(0, depth, 2)])
    grads["o_w_qkv"] = jnp.stack([unit_grads[2 * l][0] for l in range(1, depth, 2)])
    grads["o_w_out"] = jnp.stack([unit_grads[2 * l][1] for l in range(1, depth, 2)])

    weights = dict(c_ctx=c_ctx, w_mod=w_mod, b_mod=b_mod, norm_w=norm_w, w_ffn_in=w_ffn_in, w_ffn_out=w_ffn_out, e_w_in=e_w_in,
                   e_conv_w=e_conv_w, e_conv_b=e_conv_b, e_dt_bias=e_dt_bias, e_a_log=e_a_log, e_d_skip=e_d_skip,
                   e_ssd_norm_w=e_ssd_norm_w, e_sgu_w=e_sgu_w, e_sgu_b=e_sgu_b, e_w_out=e_w_out, o_w_qkv=o_w_qkv, o_sink=o_sink,
                   o_w_out=o_w_out)
    ms_ = dict(c_ctx=m_c_ctx, w_mod=m_w_mod, b_mod=m_b_mod, norm_w=m_norm_w, w_ffn_in=m_w_ffn_in, w_ffn_out=m_w_ffn_out,
               e_w_in=m_e_w_in, e_conv_w=m_e_conv_w, e_conv_b=m_e_conv_b, e_dt_bias=m_e_dt_bias, e_a_log=m_e_a_log,
               e_d_skip=m_e_d_skip, e_ssd_norm_w=m_e_ssd_norm_w, e_sgu_w=m_e_sgu_w, e_sgu_b=m_e_sgu_b, e_w_out=m_e_w_out,
               o_w_qkv=m_o_w_qkv, o_sink=m_o_sink, o_w_out=m_o_w_out)
    vs_ = dict(c_ctx=v_c_ctx, w_mod=v_w_mod, b_mod=v_b_mod, norm_w=v_norm_w, w_ffn_in=v_w_ffn_in, w_ffn_out=v_w_ffn_out,
               e_w_in=v_e_w_in, e_conv_w=v_e_conv_w, e_conv_b=v_e_conv_b, e_dt_bias=v_e_dt_bias, e_a_log=v_e_a_log,
               e_d_skip=v_e_d_skip, e_ssd_norm_w=v_e_ssd_norm_w, e_sgu_w=v_e_sgu_w, e_sgu_b=v_e_sgu_b, e_w_out=v_e_w_out,
               o_w_qkv=v_o_w_qkv, o_sink=v_o_sink, o_w_out=v_o_w_out)
    names = list(weights)
    big = ("w_mod", "w_ffn_in", "w_ffn_out", "e_w_in", "e_w_out", "o_w_qkv", "o_w_out")
    small = [n for n in names if n not in big]
    delta, new_m, new_v = {}, {}, {}
    for n in big:
        delta[n], new_m[n], new_v[n] = adamw(weights[n], grads[n], ms_[n], vs_[n], name=f"adamw_{n}")
    packed = adamw(*[_pack([tab[n] for n in small]) for tab in (weights, grads, ms_, vs_)], name="adamw_small")
    shapes = [weights[n].shape for n in small]
    for tab, pk in zip((delta, new_m, new_v), packed):
        for n, val in zip(small, _unpack(pk, shapes)):
            tab[n] = val
    return (loss, grad_x, *[grads[n] for n in names], *[delta[n] for n in names], *[new_m[n] for n in names],
            *[new_v[n] for n in names])
```

```python
import functools
import itertools
import math

import jax
import jax.numpy as jnp
from jax import lax
from jax.experimental import pallas as pl
from jax.experimental.pallas import tpu as pltpu

F32 = jnp.float32
BF16 = jnp.bfloat16
HI = lax.Precision.HIGHEST

NORM_EPS = 1e-6
SSD_HEAD_DIM = 64
SSD_STATE = 128
CHUNK = 128
CONV_K = 5
ATT_HEAD_DIM = 128
ATT_GROUP = 4
ROPE_BASE = 10000.0
GRID_W = 64
NEG_INF = -1e30
ADAM_LR, ADAM_B1, ADAM_B2, ADAM_EPS, ADAM_WD, ADAM_STEP = 0.001, 0.9, 0.999, 1e-08, 0.01, 10

LANE = 128
VMEM_LIMIT = 56 * 1024 * 1024
MESH_ID = pl.DeviceIdType.MESH
N_DEV = 8
N_CHIP = 4


def _cp(sem=None):
    return pltpu.CompilerParams(dimension_semantics=sem, vmem_limit_bytes=VMEM_LIMIT)


def _sds(shape, dtype):
    return jax.ShapeDtypeStruct(tuple(shape), dtype)


def _pick(n, cands):
    for c in cands:
        if n % c == 0:
            return c
    for step in (LANE, 16, 8):
        for c in range(min(n, cands[0]) // step * step, 0, -step):
            if n % c == 0:
                return c
    raise ValueError((n, cands))


def _w_index(blocked, layer, per_block_k, per_block_n):
    def idx(kblk, nblk):
        if blocked == "n":
            return (nblk // per_block_n, layer, kblk, nblk % per_block_n)
        return (kblk // per_block_k, layer, kblk % per_block_k, nblk)
    return idx


def mm_nn(a, w, layer, blocked, *, name, out_dtype=F32, tm=None, tn=None, tk=None):
    m, k_total = a.shape
    cb, _, kd, nd = w.shape
    n_total = nd * cb if blocked == "n" else nd
    assert k_total == (kd if blocked == "n" else kd * cb)
    tm = tm or _pick(m, (1088, 192))
    tn = tn or _pick(nd, (2048, 1408, 768, 512))
    tk = tk or _pick(kd, (2048, 1408, 512))
    nk = k_total // tk
    widx = _w_index(blocked, layer, kd // tk, nd // tn)

    def body(a_ref, w_ref, o_ref, acc_ref):
        kk = pl.program_id(2)
        part = jnp.dot(a_ref[...].astype(BF16), w_ref[...].astype(BF16), preferred_element_type=F32)

        @pl.when(kk == 0)
        def _():
            acc_ref[...] = part

        @pl.when(kk > 0)
        def _():
            acc_ref[...] += part

        @pl.when(kk == nk - 1)
        def _():
            o_ref[...] = acc_ref[...].astype(o_ref.dtype)

    return pl.pallas_call(
        body, name=name, grid=(m // tm, n_total // tn, nk),
        in_specs=[pl.BlockSpec((tm, tk), lambda i, j, k: (i, k)),
                  pl.BlockSpec((None, None, tk, tn), lambda i, j, k: widx(k, j))],
        out_specs=pl.BlockSpec((tm, tn), lambda i, j, k: (i, j)),
        out_shape=_sds((m, n_total), out_dtype),
        scratch_shapes=[pltpu.VMEM((tm, tn), F32)],
        compiler_params=_cp(("parallel", "parallel", "arbitrary")),
    )(a, w)


def mm_nt(dy, w, layer, blocked, *, name, out_dtype=F32, tm=None, tn=None, tk=None):
    m, n_total = dy.shape
    cb, _, kd, nd = w.shape
    k_total = kd if blocked == "n" else kd * cb
    assert n_total == (nd * cb if blocked == "n" else nd)
    tm = tm or _pick(m, (1088, 192))
    tn = tn or _pick(kd, (2048, 1408, 512))
    tk = tk or _pick(nd, (1408, 1024, 768))
    nk = n_total // tk
    widx = _w_index(blocked, layer, kd // tn, nd // tk)

    def body(a_ref, w_ref, o_ref, acc_ref):
        kk = pl.program_id(2)
        part = lax.dot_general(a_ref[...].astype(BF16), w_ref[...].astype(BF16), (((1,), (1,)), ((), ())),
                               preferred_element_type=F32)

        @pl.when(kk == 0)
        def _():
            acc_ref[...] = part

        @pl.when(kk > 0)
        def _():
            acc_ref[...] += part

        @pl.when(kk == nk - 1)
        def _():
            o_ref[...] = acc_ref[...].astype(o_ref.dtype)

    return pl.pallas_call(
        body, name=name, grid=(m // tm, k_total // tn, nk),
        in_specs=[pl.BlockSpec((tm, tk), lambda i, j, k: (i, k)),
                  pl.BlockSpec((None, None, tn, tk), lambda i, j, k: widx(j, k))],
        out_specs=pl.BlockSpec((tm, tn), lambda i, j, k: (i, j)),
        out_shape=_sds((m, k_total), out_dtype),
        scratch_shapes=[pltpu.VMEM((tm, tn), F32)],
        compiler_params=_cp(("parallel", "parallel", "arbitrary")),
    )(dy, w)


def mm_tn(x, dy, g, layer, blocked, *, name, tm=None, tn=None, tt=None, x_is_transposed=False):
    k_total, t_total = x.shape if x_is_transposed else x.shape[::-1]
    n_total = dy.shape[1]
    cb, _, kd, nd = g.shape
    assert k_total == (kd if blocked == "n" else kd * cb) and n_total == (nd * cb if blocked == "n" else nd)
    tm = tm or _pick(kd, (1024, 1408, 512))
    tn = tn or _pick(nd, (1408, 768, 512))
    tt = tt or _pick(t_total, (2176,) if x_is_transposed else (1088, 96))
    nt = t_total // tt
    widx = _w_index(blocked, layer, kd // tm, nd // tn)
    x_spec = pl.BlockSpec((tm, tt), lambda i, j, t: (i, t)) if x_is_transposed else pl.BlockSpec((tt, tm), lambda i, j, t: (t, i))
    x_dim = 1 if x_is_transposed else 0

    def body(x_ref, dy_ref, g_in, o_ref, acc_ref):
        del g_in
        tstep = pl.program_id(2)
        part = lax.dot_general(x_ref[...].astype(BF16), dy_ref[...].astype(BF16), (((x_dim,), (0,)), ((), ())),
                               preferred_element_type=F32)

        @pl.when(tstep == 0)
        def _():
            acc_ref[...] = part

        @pl.when(tstep > 0)
        def _():
            acc_ref[...] += part

        @pl.when(tstep == nt - 1)
        def _():
            o_ref[...] = acc_ref[...].astype(o_ref.dtype)

    return pl.pallas_call(
        body, name=name, grid=(k_total // tm, n_total // tn, nt),
        in_specs=[x_spec, pl.BlockSpec((tt, tn), lambda i, j, t: (t, j)), pl.BlockSpec(memory_space=pl.ANY)],
        out_specs=pl.BlockSpec((None, None, tm, tn), lambda i, j, t: widx(i, j)),
        out_shape=_sds(g.shape, g.dtype),
        scratch_shapes=[pltpu.VMEM((tm, tn), F32)],
        input_output_aliases={2: 0},
        compiler_params=_cp(("parallel", "parallel", "arbitrary")),
    )(x, dy, g)


def _rms(x, w):
    return x * lax.rsqrt(jnp.mean(x * x, axis=-1, keepdims=True) + NORM_EPS) * w


def _row_tile(tc):
    return 256 if tc % 256 == 0 else 128


def _seg_spec(nct, d):
    return pl.BlockSpec((None, 8, d), lambda i: (jnp.minimum(i // nct, 1), 0, 0))


def _acc_rows(acc_ref, i, nct, rows):
    @pl.when((i == 0) | (i == nct))
    def _():
        acc_ref[...] = jnp.zeros_like(acc_ref)

    for r, val in enumerate(rows):
        acc_ref[r:r + 1, :] += val


def norm_mod_fwd(u, nw, modtab, tc, which, *, name):
    t, d = u.shape
    tr = _row_tile(tc)
    nct = tc // tr
    r0 = 3 * which

    def body(u_ref, nw_ref, mt_ref, h_ref, ht_ref):
        sh, sc = mt_ref[r0:r0 + 1, :], mt_ref[r0 + 1:r0 + 2, :]
        h = _rms(u_ref[...], nw_ref[...]) * (1.0 + sc) + sh
        h_ref[...] = h.astype(h_ref.dtype)
        ht_ref[...] = h.T.astype(ht_ref.dtype)

    return pl.pallas_call(
        body, name=name, grid=(t // tr,),
        in_specs=[pl.BlockSpec((tr, d), lambda i: (i, 0)), pl.BlockSpec((1, d), lambda i: (0, 0)), _seg_spec(nct, d)],
        out_specs=(pl.BlockSpec((tr, d), lambda i: (i, 0)), pl.BlockSpec((d, tr), lambda i: (0, i))),
        out_shape=(_sds((t, d), BF16), _sds((d, t), BF16)), compiler_params=_cp(("arbitrary",)),
    )(u, nw.reshape(1, d), modtab)


def norm_mod_bwd(u, nw, modtab, dh, du_in, tc, which, *, name):
    t, d = u.shape
    tr = _row_tile(tc)
    nct = tc // tr
    r0 = 3 * which

    def body(u_ref, nw_ref, mt_ref, dh_ref, dui_ref, du_ref, acc_ref):
        i = pl.program_id(0)
        sh, sc = mt_ref[r0:r0 + 1, :], mt_ref[r0 + 1:r0 + 2, :]
        _, vjp = jax.vjp(lambda x, w, a, b: _rms(x, w) * (1.0 + b) + a, u_ref[...], nw_ref[...], sh, sc)
        dx, dw, dsh, dsc = vjp(dh_ref[...].astype(F32))
        du_ref[...] = dui_ref[...] + dx
        _acc_rows(acc_ref, i, nct, (dw, dsh, dsc))

    row = pl.BlockSpec((tr, d), lambda i: (i, 0))
    return pl.pallas_call(
        body, name=name, grid=(t // tr,),
        in_specs=[row, pl.BlockSpec((1, d), lambda i: (0, 0)), _seg_spec(nct, d), row, row],
        out_specs=(row, _seg_spec(nct, d)),
        out_shape=(_sds((t, d), F32), _sds((2, 8, d), F32)), compiler_params=_cp(("arbitrary",)),
    )(u, nw.reshape(1, d), modtab, dh, du_in)


def resid_fwd(u, o, nw, modtab, tc, which, *, name):
    t, d = u.shape
    tr = _row_tile(tc)
    nct = tc // tr
    r0 = 3 * which + 2

    def body(u_ref, o_ref, nw_ref, mt_ref, out_ref):
        out_ref[...] = u_ref[...] + mt_ref[r0:r0 + 1, :] * _rms(o_ref[...], nw_ref[...])

    row = pl.BlockSpec((tr, d), lambda i: (i, 0))
    return pl.pallas_call(
        body, name=name, grid=(t // tr,),
        in_specs=[row, row, pl.BlockSpec((1, d), lambda i: (0, 0)), _seg_spec(nct, d)],
        out_specs=row, out_shape=_sds((t, d), F32), compiler_params=_cp(("arbitrary",)),
    )(u, o, nw.reshape(1, d), modtab)


def resid_bwd(o, nw, modtab, du, tc, which, *, name):
    t, d = o.shape
    tr = _row_tile(tc)
    nct = tc // tr
    r0 = 3 * which + 2

    def body(o_ref, nw_ref, mt_ref, du_ref, do_ref, acc_ref):
        i = pl.program_id(0)
        _, vjp = jax.vjp(lambda x, w, g: g * _rms(x, w), o_ref[...], nw_ref[...], mt_ref[r0:r0 + 1, :])
        dx, dw, dg = vjp(du_ref[...])
        do_ref[...] = dx.astype(do_ref.dtype)
        _acc_rows(acc_ref, i, nct, (dw, dg))

    row = pl.BlockSpec((tr, d), lambda i: (i, 0))
    return pl.pallas_call(
        body, name=name, grid=(t // tr,),
        in_specs=[row, pl.BlockSpec((1, d), lambda i: (0, 0)), _seg_spec(nct, d), row],
        out_specs=(row, _seg_spec(nct, d)),
        out_shape=(_sds((t, d), BF16), _sds((2, 8, d), F32)), compiler_params=_cp(("arbitrary",)),
    )(o, nw.reshape(1, d), modtab, du)


def swiglu_fwd(p, *, name):
    t, h2 = p.shape
    h = h2 // 2
    tr = CHUNK

    def body(p_ref, a_ref, at_ref):
        a = jax.nn.silu(p_ref[:, :h].astype(F32)) * p_ref[:, h:].astype(F32)
        a_ref[...] = a.astype(a_ref.dtype)
        at_ref[...] = a.T.astype(at_ref.dtype)

    return pl.pallas_call(
        body, name=name, grid=(t // tr,),
        in_specs=[pl.BlockSpec((tr, h2), lambda i: (i, 0))],
        out_specs=(pl.BlockSpec((tr, h), lambda i: (i, 0)), pl.BlockSpec((h, tr), lambda i: (0, i))),
        out_shape=(_sds((t, h), BF16), _sds((h, t), BF16)), compiler_params=_cp(("parallel",)),
    )(p)


def swiglu_bwd(p, da, *, name):
    t, h2 = p.shape
    h = h2 // 2
    tr = CHUNK

    def body(p_ref, da_ref, dp_ref):
        _, vjp = jax.vjp(lambda g, u: jax.nn.silu(g) * u, p_ref[:, :h].astype(F32), p_ref[:, h:].astype(F32))
        dg, du = vjp(da_ref[...].astype(F32))
        dp_ref[:, :h] = dg.astype(dp_ref.dtype)
        dp_ref[:, h:] = du.astype(dp_ref.dtype)

    return pl.pallas_call(
        body, name=name, grid=(t // tr,),
        in_specs=[pl.BlockSpec((tr, h2), lambda i: (i, 0)), pl.BlockSpec((tr, h), lambda i: (i, 0))],
        out_specs=pl.BlockSpec((tr, h2), lambda i: (i, 0)),
        out_shape=_sds((t, h2), BF16), compiler_params=_cp(("parallel",)),
    )(p, da)


def loss_fwd_bwd(u, target, tc, *, name):
    t, d = u.shape
    tr = _row_tile(tc)
    nct = tc // tr

    def body(u_ref, t_ref, loss_ref, du_ref):
        i = pl.program_id(0)

        @pl.when(i == 0)
        def _():
            loss_ref[...] = jnp.zeros_like(loss_ref)

        @pl.when(i < nct)
        def _():
            du_ref[...] = jnp.zeros_like(du_ref)

        @pl.when(i >= nct)
        def _():
            err = u_ref[...] - t_ref[...]
            du_ref[...] = err * (1.0 / d)
            loss_ref[...] += jnp.sum(jnp.sum(err * err, axis=1, keepdims=True), axis=0, keepdims=True) * (0.5 / d)

    return pl.pallas_call(
        body, name=name, grid=(t // tr,),
        in_specs=[pl.BlockSpec((tr, d), lambda i: (i, 0)), pl.BlockSpec((tr, d), lambda i: (jnp.maximum(i - nct, 0), 0))],
        out_specs=(pl.BlockSpec((1, 1), lambda i: (0, 0)), pl.BlockSpec((tr, d), lambda i: (i, 0))),
        out_shape=(_sds((1, 1), F32), _sds((t, d), F32)), compiler_params=_cp(("arbitrary",)),
    )(u, target)


SSD_INNER = 1024
SGU_WIDTH = 1024
XBC_DIM = 1536
EVEN_COLS = 4640
EVEN_PAD_COLS = 5120
Z_BLK, U_BLK, V_BLK, X_BLK, B_BLK, C_BLK, DT_BLK = 0, 8, 16, 24, 32, 34, 36
PAD_ROWS = 8


def _conv_scratch_fill(pad_ref, val, tc, s):
    pad_ref[...] = jnp.zeros_like(pad_ref)
    pad_ref[PAD_ROWS:PAD_ROWS + tc, :] = val[:tc]
    pad_ref[2 * PAD_ROWS + tc:2 * PAD_ROWS + tc + s, :] = val[tc:]


def _conv_taps(pad_ref, tc, s, k):
    off = k - CONV_K // 2
    return (pad_ref[PAD_ROWS + off:PAD_ROWS + off + tc, :],
            pad_ref[2 * PAD_ROWS + tc + off:2 * PAD_ROWS + tc + off + s, :])


def conv_fwd(p, wb, tc, *, name):
    t = p.shape[0]
    s = t - tc
    nblk = XBC_DIM // LANE

    def body(p_ref, wb_ref, out_ref, pad_ref):
        _conv_scratch_fill(pad_ref, p_ref[...], tc, s)
        acc_c = jnp.zeros((tc, LANE), F32) + wb_ref[5:6, :]
        acc_l = jnp.zeros((s, LANE), F32) + wb_ref[5:6, :]
        for k in range(CONV_K):
            xc, xl = _conv_taps(pad_ref, tc, s, k)
            acc_c += xc * wb_ref[k:k + 1, :]
            acc_l += xl * wb_ref[k:k + 1, :]
        out_ref[:tc, :] = jax.nn.silu(acc_c)
        out_ref[tc:, :] = jax.nn.silu(acc_l)

    return pl.pallas_call(
        body, name=name, grid=(nblk,),
        in_specs=[pl.BlockSpec((t, LANE), lambda j: (0, X_BLK + j)), pl.BlockSpec((8, LANE), lambda j: (0, j))],
        out_specs=pl.BlockSpec((t, LANE), lambda j: (0, j)),
        out_shape=_sds((t, XBC_DIM), F32),
        scratch_shapes=[pltpu.VMEM((t + 3 * PAD_ROWS, LANE), F32)],
        compiler_params=_cp(("parallel",)),
    )(p, wb)


def conv_bwd(p, wb, dxg, dskip, tc, *, name):
    t = p.shape[0]
    s = t - tc
    nblk = XBC_DIM // LANE
    nx = SSD_INNER // LANE

    def grp(j):
        return jnp.where(j < nx, j // 4, (j - nx) % 2)

    def sub(j):
        return jnp.where(j < nx, j % 4, 4 + (j - nx) // 2)

    def body(p_ref, wb_ref, d0_ref, d1_ref, ds_ref, dp_ref, dwb_ref, pad_ref, dpad_ref):
        j = pl.program_id(0)
        _conv_scratch_fill(pad_ref, p_ref[...], tc, s)
        pre = [jnp.zeros((tc, LANE), F32) + wb_ref[5:6, :], jnp.zeros((s, LANE), F32) + wb_ref[5:6, :]]
        for k in range(CONV_K):
            xc, xl = _conv_taps(pad_ref, tc, s, k)
            pre[0] += xc * wb_ref[k:k + 1, :]
            pre[1] += xl * wb_ref[k:k + 1, :]
        dx = d0_ref[...] + d1_ref[...] + jnp.where(j < nx, ds_ref[...], 0.0)
        dpre = []
        for part, rows in ((0, slice(0, tc)), (1, slice(tc, t))):
            sig = jax.nn.sigmoid(pre[part])
            dpre.append(dx[rows] * (sig * (1.0 + pre[part] * (1.0 - sig))))
        dwb_ref[...] = jnp.zeros_like(dwb_ref)
        dwb_ref[5:6, :] = jnp.sum(dpre[0], axis=0, keepdims=True) + jnp.sum(dpre[1], axis=0, keepdims=True)
        for k in range(CONV_K):
            xc, xl = _conv_taps(pad_ref, tc, s, k)
            dwb_ref[k:k + 1, :] = (jnp.sum(dpre[0] * xc, axis=0, keepdims=True)
                                   + jnp.sum(dpre[1] * xl, axis=0, keepdims=True))
        _conv_scratch_fill(dpad_ref, jnp.concatenate(dpre, axis=0), tc, s)
        acc_c = jnp.zeros((tc, LANE), F32)
        acc_l = jnp.zeros((s, LANE), F32)
        for k in range(CONV_K):
            gc, gl = _conv_taps(dpad_ref, tc, s, CONV_K - 1 - k)
            acc_c += gc * wb_ref[k:k + 1, :]
            acc_l += gl * wb_ref[k:k + 1, :]
        dp_ref[:tc, :] = acc_c.astype(dp_ref.dtype)
        dp_ref[tc:, :] = acc_l.astype(dp_ref.dtype)

    col = pl.BlockSpec((t, LANE), lambda j: (0, j))
    return pl.pallas_call(
        body, name=name, grid=(nblk,),
        in_specs=[pl.BlockSpec((t, LANE), lambda j: (0, X_BLK + j)), pl.BlockSpec((8, LANE), lambda j: (0, j)),
                  pl.BlockSpec((None, None, t, LANE), lambda j: (0, grp(j), 0, sub(j))),
                  pl.BlockSpec((None, None, t, LANE), lambda j: (1, grp(j), 0, sub(j))),
                  pl.BlockSpec((t, LANE), lambda j: (0, jnp.minimum(j, nx - 1)))],
        out_specs=(col, pl.BlockSpec((8, LANE), lambda j: (0, j))),
        out_shape=(_sds((t, XBC_DIM), BF16), _sds((8, XBC_DIM), F32)),
        scratch_shapes=[pltpu.VMEM((t + 3 * PAD_ROWS, LANE), F32), pltpu.VMEM((t + 3 * PAD_ROWS, LANE), F32)],
        compiler_params=_cp(("parallel",)),
    )(p, wb, dxg, dxg, dskip)


HEADS_PER_DG = 8


def _ssd_prep_fn(pre, bias, alog, rev):
    q = pre.shape[0]
    lane = lax.broadcasted_iota(jnp.int32, (1, LANE), 1)
    dt = jnp.where(lane < HEADS_PER_DG, jax.nn.softplus(pre + bias), 0.0)
    row = lax.broadcasted_iota(jnp.int32, (q, q), 0)
    col = lax.broadcasted_iota(jnp.int32, (q, q), 1)
    tri = jnp.where((col - row) * jnp.where(rev, 1, -1) >= 0, 1.0, 0.0)
    cs = jnp.dot(tri, dt * (-jnp.exp(alog)), precision=HI, preferred_element_type=F32)
    return dt, cs


def _scan_chunk(nc_ctx, nch):
    def idx(dg, i):
        fwd = i
        bwd = jnp.where(i < nc_ctx, nc_ctx - 1 - i, nch - 1 - (i - nc_ctx))
        return jnp.where(dg // 2 == 0, fwd, bwd)
    return idx


def ssd_prep_fwd(pre, bias, alog, *, name):
    _, t, _ = pre.shape
    blk = pl.BlockSpec((None, CHUNK, LANE), lambda dg, i: (dg, i, 0))
    par = pl.BlockSpec((None, 1, LANE), lambda dg, i: (dg, 0, 0))

    def body(pre_ref, b_ref, a_ref, dt_ref, cs_ref):
        dt, cs = _ssd_prep_fn(pre_ref[...], b_ref[...], a_ref[...], pl.program_id(0) // 2 == 1)
        dt_ref[...] = dt
        cs_ref[...] = cs

    return pl.pallas_call(
        body, name=name, grid=(4, t // CHUNK), in_specs=[blk, par, par], out_specs=(blk, blk),
        out_shape=(_sds(pre.shape, F32), _sds(pre.shape, F32)), compiler_params=_cp(("parallel", "parallel")),
    )(pre, bias, alog)


def ssd_prep_bwd(pre, bias, alog, ddt, dcs, *, name):
    _, t, _ = pre.shape
    blk = pl.BlockSpec((None, CHUNK, LANE), lambda dg, i: (dg, i, 0))
    par = pl.BlockSpec((None, 1, LANE), lambda dg, i: (dg, 0, 0))

    def body(pre_ref, b_ref, a_ref, ddt_ref, dcs_ref, dpre_ref, acc_ref):
        rev = pl.program_id(0) // 2 == 1
        _, vjp = jax.vjp(lambda x, b, a: _ssd_prep_fn(x, b, a, rev), pre_ref[...], b_ref[...], a_ref[...])
        dpre, db, da = vjp((ddt_ref[...], dcs_ref[...]))
        dpre_ref[...] = dpre

        @pl.when(pl.program_id(1) == 0)
        def _():
            acc_ref[...] = jnp.zeros_like(acc_ref)

        acc_ref[0:1, :] += db
        acc_ref[1:2, :] += da

    return pl.pallas_call(
        body, name=name, grid=(4, t // CHUNK), in_specs=[blk, par, par, blk, blk],
        out_specs=(blk, pl.BlockSpec((None, 8, LANE), lambda dg, i: (dg, 0, 0))),
        out_shape=(_sds(pre.shape, F32), _sds((4, 8, LANE), F32)), compiler_params=_cp(("parallel", "arbitrary")),
    )(pre, bias, alog, ddt, dcs)


def _onehot_col(a, h):
    lane = lax.broadcasted_iota(jnp.int32, (1, a.shape[1]), 1)
    return jnp.sum(jnp.where(lane == h, a, 0.0), axis=1, keepdims=True)


def _onehot_row(a, h):
    sub = lax.broadcasted_iota(jnp.int32, (a.shape[0], 1), 0)
    return jnp.sum(jnp.where(sub == h, a, 0.0), axis=0, keepdims=True)


def _bdot(a, b, dims):
    return lax.dot_general(a.astype(BF16), b.astype(BF16), (dims, ((), ())), preferred_element_type=F32)


def _ssd_pair(xblk, dt, cs, bm, cm, sp, rev, pair):
    q = xblk.shape[0]
    lane = lax.broadcasted_iota(jnp.int32, (1, LANE), 1)
    sub = lax.broadcasted_iota(jnp.int32, (LANE, 1), 0)
    row = lax.broadcasted_iota(jnp.int32, (q, q), 0)
    col = lax.broadcasted_iota(jnp.int32, (q, q), 1)
    mask = (col - row) * jnp.where(rev, 1, -1) >= 0
    last = jnp.where(rev, 0, q - 1)
    cs_t = cs.T
    g = _bdot(cm, bm, ((1,), (1,)))
    y = jnp.zeros((q, LANE), F32)
    escale = jnp.zeros((q, LANE), F32)
    xw = jnp.zeros((q, LANE), F32)
    dec = jnp.zeros((LANE, 1), F32)
    for hh in range(2):
        h = 2 * pair + hh
        c_col = _onehot_col(cs, h)
        c_row = _onehot_row(cs_t, h)
        tot = jnp.sum(jnp.where(lax.broadcasted_iota(jnp.int32, (1, q), 1) == last, c_row, 0.0), axis=1, keepdims=True)
        in_head = (lane >= hh * SSD_HEAD_DIM) & (lane < (hh + 1) * SSD_HEAD_DIM)
        xh = jnp.where(in_head, xblk * _onehot_col(dt, h), 0.0)
        ldec = jnp.where(mask, jnp.exp(jnp.where(mask, c_col - c_row, 0.0)), 0.0)
        y = y + _bdot(g * ldec, xh, ((1,), (0,)))
        escale = escale + jnp.where(in_head, jnp.exp(c_col), 0.0)
        xw = xw + xh * jnp.exp(tot - c_col)
        dec = dec + jnp.where((sub >= hh * SSD_HEAD_DIM) & (sub < (hh + 1) * SSD_HEAD_DIM), jnp.exp(tot), 0.0)
    y = y + _bdot(cm, sp, ((1,), (1,))) * escale
    s_new = sp * dec + _bdot(xw, bm, ((0,), (0,)))
    return y, s_new


def ssd_fwd(xbc, dt, cs, tc, *, name):
    t = xbc.shape[0]
    nch = t // CHUNK
    sidx = _scan_chunk(tc // CHUNK, nch)
    gw = SSD_INNER // 2
    nb = SSD_INNER // LANE

    def body(x_ref, b_ref, c_ref, dt_ref, cs_ref, y_ref, sp_ref, s_ref):
        @pl.when(pl.program_id(1) == 0)
        def _():
            s_ref[...] = jnp.zeros_like(s_ref)

        rev = pl.program_id(0) // 2 == 1
        sp_ref[...] = s_ref[...]
        for p in range(gw // LANE):
            blk = slice(p * LANE, (p + 1) * LANE)
            y, s_new = _ssd_pair(x_ref[:, blk], dt_ref[...], cs_ref[...], b_ref[...], c_ref[...], s_ref[blk, :], rev, p)
            y_ref[:, blk] = y
            s_ref[blk, :] = s_new

    return pl.pallas_call(
        body, name=name, grid=(4, nch),
        in_specs=[pl.BlockSpec((CHUNK, gw), lambda dg, i: (sidx(dg, i), dg % 2)),
                  pl.BlockSpec((CHUNK, LANE), lambda dg, i: (sidx(dg, i), nb + dg % 2)),
                  pl.BlockSpec((CHUNK, LANE), lambda dg, i: (sidx(dg, i), nb + 2 + dg % 2)),
                  pl.BlockSpec((None, CHUNK, LANE), lambda dg, i: (dg, sidx(dg, i), 0)),
                  pl.BlockSpec((None, CHUNK, LANE), lambda dg, i: (dg, sidx(dg, i), 0))],
        out_specs=(pl.BlockSpec((None, CHUNK, gw), lambda dg, i: (dg // 2, sidx(dg, i), dg % 2)),
                   pl.BlockSpec((None, None, gw, SSD_STATE), lambda dg, i: (dg, sidx(dg, i), 0, 0))),
        out_shape=(_sds((2, t, SSD_INNER), F32), _sds((4, nch, gw, SSD_STATE), F32)),
        scratch_shapes=[pltpu.VMEM((gw, SSD_STATE), F32)],
        compiler_params=_cp(("parallel", "arbitrary")),
    )(xbc, xbc, xbc, dt, cs)


def ssd_bwd(xbc, dt, cs, sprev, dy, tc, *, name):
    t = xbc.shape[0]
    nch = t // CHUNK
    fidx = _scan_chunk(tc // CHUNK, nch)
    sidx = lambda dg, i: fidx(dg, nch - 1 - i)
    gw = SSD_INNER // 2
    nb = SSD_INNER // LANE

    def body(x_ref, b_ref, c_ref, dt_ref, cs_ref, sp_ref, dy_ref, dxg_ref, ddt_ref, dcs_ref, ds_ref):
        @pl.when(pl.program_id(1) == 0)
        def _():
            ds_ref[...] = jnp.zeros_like(ds_ref)

        rev = pl.program_id(0) // 2 == 1
        ddt = jnp.zeros((CHUNK, LANE), F32)
        dcs = jnp.zeros((CHUNK, LANE), F32)
        db = jnp.zeros((CHUNK, SSD_STATE), F32)
        dc = jnp.zeros((CHUNK, SSD_STATE), F32)
        for p in range(gw // LANE):
            blk = slice(p * LANE, (p + 1) * LANE)
            _, vjp = jax.vjp(functools.partial(_ssd_pair, rev=rev, pair=p),
                             x_ref[:, blk], dt_ref[...], cs_ref[...], b_ref[...], c_ref[...], sp_ref[blk, :])
            dx, ddt_p, dcs_p, db_p, dc_p, dsp = vjp((dy_ref[:, blk], ds_ref[blk, :]))
            dxg_ref[:, blk] = dx
            ds_ref[blk, :] = dsp
            ddt, dcs, db, dc = ddt + ddt_p, dcs + dcs_p, db + db_p, dc + dc_p
        dxg_ref[:, gw:gw + SSD_STATE] = db
        dxg_ref[:, gw + SSD_STATE:] = dc
        ddt_ref[...] = ddt
        dcs_ref[...] = dcs

    hd = pl.BlockSpec((None, CHUNK, LANE), lambda dg, i: (dg, sidx(dg, i), 0))
    return pl.pallas_call(
        body, name=name, grid=(4, nch),
        in_specs=[pl.BlockSpec((CHUNK, gw), lambda dg, i: (sidx(dg, i), dg % 2)),
                  pl.BlockSpec((CHUNK, LANE), lambda dg, i: (sidx(dg, i), nb + dg % 2)),
                  pl.BlockSpec((CHUNK, LANE), lambda dg, i: (sidx(dg, i), nb + 2 + dg % 2)),
                  hd, hd,
                  pl.BlockSpec((None, None, gw, SSD_STATE), lambda dg, i: (dg, sidx(dg, i), 0, 0)),
                  pl.BlockSpec((CHUNK, gw), lambda dg, i: (sidx(dg, i), dg % 2))],
        out_specs=(pl.BlockSpec((None, None, CHUNK, gw + 2 * SSD_STATE), lambda dg, i: (dg // 2, dg % 2, sidx(dg, i), 0)),
                   hd, hd),
        out_shape=(_sds((2, 2, t, gw + 2 * SSD_STATE), F32), _sds((4, t, LANE), F32), _sds((4, t, LANE), F32)),
        scratch_shapes=[pltpu.VMEM((gw, SSD_STATE), F32)],
        compiler_params=_cp(("parallel", "arbitrary")),
    )(xbc, xbc, xbc, dt, cs, sprev, dy)


def _ssd_finish_fn(y0, y1, xs, z, dskip, nw):
    y = (y0 + y1 + xs * dskip) * jax.nn.silu(z)
    half = y.shape[1] // 2
    first = lax.broadcasted_iota(jnp.int32, (1, y.shape[1]), 1) < half
    sq = y * y
    m0 = jnp.sum(jnp.where(first, sq, 0.0), axis=1, keepdims=True) / half
    m1 = jnp.sum(jnp.where(first, 0.0, sq), axis=1, keepdims=True) / half
    return y * jnp.where(first, lax.rsqrt(m0 + NORM_EPS), lax.rsqrt(m1 + NORM_EPS)) * nw


def ssd_finish_fwd(y, xbc, p, dskip, nw, *, name):
    t = xbc.shape[0]
    tr = CHUNK
    w = SSD_INNER
    row = pl.BlockSpec((tr, w), lambda i: (i, 0))
    par = pl.BlockSpec((1, w), lambda i: (0, 0))

    def body(y0_ref, y1_ref, x_ref, z_ref, ds_ref, nw_ref, o_ref):
        o_ref[...] = _ssd_finish_fn(y0_ref[...], y1_ref[...], x_ref[...], z_ref[...], ds_ref[...], nw_ref[...]).astype(o_ref.dtype)

    return pl.pallas_call(
        body, name=name, grid=(t // tr,),
        in_specs=[pl.BlockSpec((None, tr, w), lambda i: (0, i, 0)), pl.BlockSpec((None, tr, w), lambda i: (1, i, 0)),
                  row, row, par, par],
        out_specs=row, out_shape=_sds((t, w), BF16), compiler_params=_cp(("parallel",)),
    )(y, y, xbc, p, dskip, nw)


def ssd_finish_bwd(y, xbc, p, dskip, nw, dout, *, name):
    t = xbc.shape[0]
    tr = CHUNK
    w = SSD_INNER
    row = pl.BlockSpec((tr, w), lambda i: (i, 0))
    par = pl.BlockSpec((1, w), lambda i: (0, 0))

    def body(y0_ref, y1_ref, x_ref, z_ref, ds_ref, nw_ref, do_ref, dy_ref, dx_ref, dz_ref, acc_ref):
        _, vjp = jax.vjp(_ssd_finish_fn, y0_ref[...], y1_ref[...], x_ref[...], z_ref[...], ds_ref[...], nw_ref[...])
        dy0, _, dx, dz, dds, dnw = vjp(do_ref[...])
        dy_ref[...] = dy0
        dx_ref[...] = dx
        dz_ref[...] = dz.astype(dz_ref.dtype)

        @pl.when(pl.program_id(0) == 0)
        def _():
            acc_ref[...] = jnp.zeros_like(acc_ref)

        acc_ref[0:1, :] += dds
        acc_ref[1:2, :] += dnw

    return pl.pallas_call(
        body, name=name, grid=(t // tr,),
        in_specs=[pl.BlockSpec((None, tr, w), lambda i: (0, i, 0)), pl.BlockSpec((None, tr, w), lambda i: (1, i, 0)),
                  row, row, par, par, row],
        out_specs=(row, row, row, pl.BlockSpec((8, w), lambda i: (0, 0))),
        out_shape=(_sds((t, w), F32), _sds((t, w), F32), _sds((t, w), BF16), _sds((8, w), F32)),
        compiler_params=_cp(("arbitrary",)),
    )(y, y, xbc, p, dskip, nw, dout)


SGU_GROUPS = 8


def _sgu_fn(us, vs, ws, bs):
    n = SGU_GROUPS * LANE
    vf = [jax.nn.gelu(v) for v in vs]
    mu = sum(jnp.sum(v, axis=1, keepdims=True) for v in vf) / n
    var = sum(jnp.sum(jnp.square(v - mu), axis=1, keepdims=True) for v in vf) / n
    rstd = lax.rsqrt(var + NORM_EPS)
    return tuple(jax.nn.gelu(u) * (_bdot(w, (v - mu) * rstd, ((1,), (0,))) + b) for u, v, w, b in zip(us, vf, ws, bs))


def sgu_fwd(p, w, b, *, name):
    t = p.shape[0]
    wd = SGU_WIDTH

    def body(u_ref, v_ref, w_ref, b_ref, o_ref):
        sl = [slice(g * LANE, (g + 1) * LANE) for g in range(SGU_GROUPS)]
        ys = _sgu_fn([u_ref[:, s] for s in sl], [v_ref[:, s] for s in sl], [w_ref[g] for g in range(SGU_GROUPS)],
                     [b_ref[g] for g in range(SGU_GROUPS)])
        for s, yv in zip(sl, ys):
            o_ref[:, s] = yv.astype(o_ref.dtype)

    return pl.pallas_call(
        body, name=name, grid=(t // CHUNK,),
        in_specs=[pl.BlockSpec((CHUNK, wd), lambda i: (i, U_BLK * LANE // wd)), pl.BlockSpec((CHUNK, wd), lambda i: (i, V_BLK * LANE // wd)),
                  pl.BlockSpec((SGU_GROUPS, CHUNK, CHUNK), lambda i: (0, 0, 0)), pl.BlockSpec((SGU_GROUPS, CHUNK, 1), lambda i: (0, 0, 0))],
        out_specs=pl.BlockSpec((CHUNK, wd), lambda i: (i, 0)),
        out_shape=_sds((t, wd), BF16), compiler_params=_cp(("parallel",)),
    )(p, p, w, b)


def sgu_bwd(p, w, b, dout, *, name):
    t = p.shape[0]
    wd = SGU_WIDTH

    def body(u_ref, v_ref, w_ref, b_ref, do_ref, duv_ref, dw_ref, db_ref):
        sl = [slice(g * LANE, (g + 1) * LANE) for g in range(SGU_GROUPS)]
        _, vjp = jax.vjp(_sgu_fn, [u_ref[:, s] for s in sl], [v_ref[:, s] for s in sl],
                         [w_ref[g] for g in range(SGU_GROUPS)], [b_ref[g] for g in range(SGU_GROUPS)])
        dus, dvs, dws, dbs = vjp(tuple(do_ref[:, s] for s in sl))

        @pl.when(pl.program_id(0) == 0)
        def _():
            dw_ref[...] = jnp.zeros_like(dw_ref)
            db_ref[...] = jnp.zeros_like(db_ref)

        for g, s in enumerate(sl):
            duv_ref[:, s] = dus[g].astype(duv_ref.dtype)
            duv_ref[:, slice(wd + g * LANE, wd + (g + 1) * LANE)] = dvs[g].astype(duv_ref.dtype)
            dw_ref[g] += dws[g]
            db_ref[g] += dbs[g]

    wspec = pl.BlockSpec((SGU_GROUPS, CHUNK, CHUNK), lambda i: (0, 0, 0))
    bspec = pl.BlockSpec((SGU_GROUPS, CHUNK, 1), lambda i: (0, 0, 0))
    return pl.pallas_call(
        body, name=name, grid=(t // CHUNK,),
        in_specs=[pl.BlockSpec((CHUNK, wd), lambda i: (i, U_BLK * LANE // wd)), pl.BlockSpec((CHUNK, wd), lambda i: (i, V_BLK * LANE // wd)),
                  wspec, bspec, pl.BlockSpec((CHUNK, wd), lambda i: (i, 1))],
        out_specs=(pl.BlockSpec((CHUNK, 2 * wd), lambda i: (i, 0)), wspec, bspec),
        out_shape=(_sds((t, 2 * wd), BF16), _sds(w.shape, F32), _sds(b.shape, F32)),
        compiler_params=_cp(("arbitrary",)),
    )(p, p, w, b, dout)


def even_cols_permute(w):
    z, xbc, dt, u, v = jnp.split(w, (1024, 2560, 2592, 3616), axis=-1)
    pad = jnp.zeros(w.shape[:-1] + (EVEN_PAD_COLS - EVEN_COLS,), w.dtype)
    return jnp.concatenate([z, u, v, xbc, dt, pad], axis=-1)


def even_cols_unpermute(w):
    z, u, v, xbc, dt = jnp.split(w[..., :EVEN_COLS], (1024, 2048, 3072, 4608), axis=-1)
    return jnp.concatenate([z, xbc, dt, u, v], axis=-1)


def _dt_cols(p):
    t = p.shape[0]
    d = p[:, DT_BLK * LANE:DT_BLK * LANE + 4 * HEADS_PER_DG].reshape(t, 4, HEADS_PER_DG).transpose(1, 0, 2)
    return jnp.pad(d, ((0, 0), (0, 0), (0, LANE - HEADS_PER_DG)))


def _heads_to_lanes(a):
    return jnp.pad(a.reshape(4, 1, HEADS_PER_DG), ((0, 0), (0, 0), (0, LANE - HEADS_PER_DG)))


def even_params(conv_w, conv_b, dt_bias, a_log, d_skip, ssd_nw, sgu_w, sgu_b):
    wb = jnp.concatenate([conv_w, conv_b[None], jnp.zeros((2, XBC_DIM), F32)], axis=0)
    return dict(wb=wb, dtb=_heads_to_lanes(dt_bias), alog=_heads_to_lanes(a_log),
                dskip=jnp.repeat(d_skip, SSD_HEAD_DIM)[None], ssd_nw=ssd_nw[None], sgu_w=sgu_w, sgu_b=sgu_b[..., None])


def even_mixer_fwd(h, w_in, w_out, li, ep, tc, tag):
    p = mm_nn(h, w_in, li, "n", name=f"{tag}_in")
    xbc = conv_fwd(p, ep["wb"], tc, name=f"{tag}_conv")
    pre = _dt_cols(p)
    dt, cs = ssd_prep_fwd(pre, ep["dtb"], ep["alog"], name=f"{tag}_prep")
    y, sprev = ssd_fwd(xbc, dt, cs, tc, name=f"{tag}_ssd")
    yssd = ssd_finish_fwd(y, xbc, p, ep["dskip"], ep["ssd_nw"], name=f"{tag}_fin")
    ysgu = sgu_fwd(p, ep["sgu_w"], ep["sgu_b"], name=f"{tag}_sgu")
    ymix = jnp.concatenate([yssd, ysgu], axis=1)
    o = mm_nn(ymix, w_out, li, "k", name=f"{tag}_out")
    return o, (p, xbc, pre, dt, cs, y, sprev, ymix)


def even_mixer_bwd(saved, do, ht, w_in, w_out, g_in, g_out, li, ep, tc, tag):
    p, xbc, pre, dt, cs, y, sprev, ymix = saved
    t = p.shape[0]
    g_out = mm_tn(ymix, do, g_out, li, "k", name=f"{tag}_out_dw")
    dymix = mm_nt(do, w_out, li, "k", name=f"{tag}_out_dx")
    dy, dxskip, dz, acc_fin = ssd_finish_bwd(y, xbc, p, ep["dskip"], ep["ssd_nw"], dymix, name=f"{tag}_fin_b")
    duv, dsgu_w, dsgu_b = sgu_bwd(p, ep["sgu_w"], ep["sgu_b"], dymix, name=f"{tag}_sgu_b")
    dxg, ddt, dcs = ssd_bwd(xbc, dt, cs, sprev, dy, tc, name=f"{tag}_ssd_b")
    dpre, acc_prep = ssd_prep_bwd(pre, ep["dtb"], ep["alog"], ddt, dcs, name=f"{tag}_prep_b")
    dxbc, dwb = conv_bwd(p, ep["wb"], dxg, dxskip, tc, name=f"{tag}_conv_b")
    ddt_cols = dpre[:, :, :HEADS_PER_DG].transpose(1, 0, 2).reshape(t, 4 * HEADS_PER_DG).astype(BF16)
    ddt_cols = jnp.pad(ddt_cols, ((0, 0), (0, EVEN_PAD_COLS - DT_BLK * LANE - 4 * HEADS_PER_DG)))
    dp = jnp.concatenate([dz, duv, dxbc, ddt_cols], axis=1)
    dh = mm_nt(dp, w_in, li, "n", name=f"{tag}_in_dx")
    g_in = mm_tn(ht, dp, g_in, li, "n", name=f"{tag}_in_dw", x_is_transposed=True)
    small = dict(
        e_conv_w=dwb[:CONV_K], e_conv_b=dwb[CONV_K],
        e_dt_bias=acc_prep[:, 0, :HEADS_PER_DG].reshape(2, 2 * HEADS_PER_DG),
        e_a_log=acc_prep[:, 1, :HEADS_PER_DG].reshape(2, 2 * HEADS_PER_DG),
        e_d_skip=acc_fin[0].reshape(-1, SSD_HEAD_DIM).sum(axis=1), e_ssd_norm_w=acc_fin[1],
        e_sgu_w=dsgu_w, e_sgu_b=dsgu_b[..., 0])
    return dh, g_in, g_out, small


ATT_HEADS = 16
ATT_KV = 4
Q_BLKS, K_BLKS = ATT_HEADS, ATT_KV


def rope_tables(tc, s):
    quarter = ATT_HEAD_DIM // 4
    pos = jnp.arange(s)
    inv = ROPE_BASE ** (-jnp.arange(quarter, dtype=F32) / quarter)
    a_row = (pos // GRID_W).astype(F32)[:, None] * inv
    a_col = (pos % GRID_W).astype(F32)[:, None] * inv
    cos = jnp.concatenate([jnp.cos(a_row)] * 2 + [jnp.cos(a_col)] * 2, axis=1)
    sin = jnp.concatenate([-jnp.sin(a_row), jnp.sin(a_row), -jnp.sin(a_col), jnp.sin(a_col)], axis=1)
    return (jnp.concatenate([jnp.ones((tc, ATT_HEAD_DIM), F32), cos], axis=0),
            jnp.concatenate([jnp.zeros((tc, ATT_HEAD_DIM), F32), sin], axis=0))


def _swap_halves(x):
    lane = lax.broadcasted_iota(jnp.int32, x.shape, 1)
    return jnp.where(lane % 64 < 32, pltpu.roll(x, 96, 1), pltpu.roll(x, 32, 1))


def rope_fwd(p, cos, sin, *, name):
    t = p.shape[0]
    tr = _pick(t, (1088, 640))
    scale = ATT_HEAD_DIM ** -0.5

    def body(p_ref, c_ref, s_ref, o_ref):
        x = p_ref[...]
        r = x * c_ref[...] + _swap_halves(x) * s_ref[...]
        o_ref[...] = (r * jnp.where(pl.program_id(1) < Q_BLKS, scale, 1.0)).astype(o_ref.dtype)

    tab = pl.BlockSpec((tr, LANE), lambda i, j: (i, 0))
    return pl.pallas_call(
        body, name=name, grid=(t // tr, Q_BLKS + K_BLKS),
        in_specs=[pl.BlockSpec((tr, LANE), lambda i, j: (i, j)), tab, tab],
        out_specs=pl.BlockSpec((tr, LANE), lambda i, j: (i, j)),
        out_shape=_sds((t, (Q_BLKS + K_BLKS) * LANE), BF16), compiler_params=_cp(("parallel", "parallel")),
    )(p, cos, sin)


def rope_bwd(dq, dk, dv, cos, sin, *, name):
    t = dq.shape[0]
    tr = _pick(t, (1088, 640))
    scale = ATT_HEAD_DIM ** -0.5

    def body(dq_ref, dk_ref, dv_ref, c_ref, s_ref, o_ref):
        j = pl.program_id(1)

        def unrot(g):
            return g * c_ref[...] + _swap_halves(g * s_ref[...])

        @pl.when(j < Q_BLKS)
        def _():
            o_ref[...] = (unrot(dq_ref[...]) * scale).astype(o_ref.dtype)

        @pl.when((j >= Q_BLKS) & (j < Q_BLKS + K_BLKS))
        def _():
            o_ref[...] = unrot(dk_ref[...]).astype(o_ref.dtype)

        @pl.when(j >= Q_BLKS + K_BLKS)
        def _():
            o_ref[...] = dv_ref[...].astype(o_ref.dtype)

    tab = pl.BlockSpec((tr, LANE), lambda i, j: (i, 0))
    return pl.pallas_call(
        body, name=name, grid=(t // tr, Q_BLKS + 2 * K_BLKS),
        in_specs=[pl.BlockSpec((tr, LANE), lambda i, j: (i, jnp.minimum(j, Q_BLKS - 1))),
                  pl.BlockSpec((None, tr, LANE), lambda i, j: (jnp.clip(j - Q_BLKS, 0, K_BLKS - 1), i, 0)),
                  pl.BlockSpec((None, tr, LANE), lambda i, j: (jnp.clip(j - Q_BLKS - K_BLKS, 0, K_BLKS - 1), i, 0)), tab, tab],
        out_specs=pl.BlockSpec((tr, LANE), lambda i, j: (i, j)),
        out_shape=_sds((t, (Q_BLKS + 2 * K_BLKS) * LANE), BF16), compiler_params=_cp(("parallel", "parallel")),
    )(dq, dk, dv, cos, sin)


def _attn_tile(q4, kp, kc, kn, vp, vc, vn, kx, vx, sinks, is_lat, has_prev, has_next):
    q = kp.shape[0]
    nq = q4.shape[0]
    row = jnp.bitwise_and(lax.broadcasted_iota(jnp.int32, (nq, q), 0), q - 1)
    col = lax.broadcasted_iota(jnp.int32, (nq, q), 1)
    m_prev = (col - row) >= (1 - has_prev) * q
    m_cur = (row - row) >= (1 - is_lat)
    m_next = (row - col) >= (1 - has_next) * q
    lane = lax.broadcasted_iota(jnp.int32, (1, LANE), 1)
    sink = jnp.concatenate([jnp.broadcast_to(jnp.sum(jnp.where(lane == 0, s, 0.0), axis=1, keepdims=True), (q, 1)) for s in sinks],
                           axis=0)
    s_p = jnp.where(m_prev, _bdot(q4, kp, ((1,), (1,))), NEG_INF)
    s_c = jnp.where(m_cur, _bdot(q4, kc, ((1,), (1,))), NEG_INF)
    s_n = jnp.where(m_next, _bdot(q4, kn, ((1,), (1,))), NEG_INF)
    s_x = _bdot(q4, kx, ((1,), (1,)))
    mx = [jnp.max(a, axis=1, keepdims=True) for a in (s_p, s_c, s_n, s_x)]
    m = lax.stop_gradient(jnp.maximum(jnp.maximum(jnp.maximum(mx[0], mx[1]), jnp.maximum(mx[2], mx[3])), sink))
    e = [jnp.exp(a - m) for a in (s_p, s_c, s_n, s_x)]
    inv = 1.0 / (sum(jnp.sum(a, axis=1, keepdims=True) for a in e) + jnp.exp(sink - m))
    return sum(_bdot(a * inv, v, ((1,), (0,))) for a, v in zip(e, (vp, vc, vn, vx)))


def _attn_specs(t, tc):
    nblk = t // CHUNK
    hw = ATT_GROUP * LANE
    kcol = lambda kv: Q_BLKS + kv
    vcol = lambda kv: Q_BLKS + K_BLKS + kv
    prev = lambda n: jnp.maximum(n - 1, 0)
    nxt = lambda n: jnp.minimum(n + 1, nblk - 1)
    blk = lambda rowf, colf: pl.BlockSpec((CHUNK, LANE), lambda kv, n: (rowf(n), colf(kv)))
    same = lambda n: n
    return [pl.BlockSpec((CHUNK, hw), lambda kv, n: (n, kv)),
            blk(prev, kcol), blk(same, kcol), blk(nxt, kcol), blk(prev, vcol), blk(same, vcol), blk(nxt, vcol),
            pl.BlockSpec((tc, LANE), lambda kv, n: (0, kcol(kv))), pl.BlockSpec((tc, LANE), lambda kv, n: (0, vcol(kv))),
            pl.BlockSpec((None, 8, LANE), lambda kv, n: (kv, 0, 0))]


def _attn_args(refs, n, nct, nblk):
    q_ref, kp, kc, kn, vp, vc, vn, kx, vx, sk = refs
    f = lambda r: r[...].astype(F32)
    q4 = _stack_heads(q_ref)
    sinks = [sk[g:g + 1, :] for g in range(ATT_GROUP)]
    flags = ((n >= nct).astype(jnp.int32), (n >= nct + 1).astype(jnp.int32), ((n >= nct) & (n + 1 < nblk)).astype(jnp.int32))
    return (q4, f(kp), f(kc), f(kn), f(vp), f(vc), f(vn), f(kx), f(vx), sinks), flags


def _stack_heads(ref):
    return jnp.concatenate([ref[:, g * LANE:(g + 1) * LANE].astype(F32) for g in range(ATT_GROUP)], axis=0)


def _unstack_heads(ref, val):
    for g in range(ATT_GROUP):
        ref[:, g * LANE:(g + 1) * LANE] = val[g * CHUNK:(g + 1) * CHUNK].astype(ref.dtype)


def attn_fwd(qk, p, sink, tc, *, name):
    t = qk.shape[0]
    nblk, nct = t // CHUNK, tc // CHUNK
    hw = ATT_GROUP * LANE

    def body(*refs):
        o_ref = refs[-1]
        args, flags = _attn_args(refs[:-1], pl.program_id(1), nct, nblk)
        _unstack_heads(o_ref, _attn_tile(*args, *flags))

    return pl.pallas_call(
        body, name=name, grid=(ATT_KV, nblk), in_specs=_attn_specs(t, tc),
        out_specs=pl.BlockSpec((CHUNK, hw), lambda kv, n: (n, kv)),
        out_shape=_sds((t, ATT_HEADS * LANE), BF16), compiler_params=_cp(("parallel", "parallel")),
    )(qk, qk, qk, qk, p, p, p, qk, p, sink)


def attn_bwd(qk, p, sink, do, tc, *, name):
    t = qk.shape[0]
    nblk, nct = t // CHUNK, tc // CHUNK
    hw = ATT_GROUP * LANE

    def body(*refs):
        do_ref, dq_ref, dk_ref, dv_ref, dsk_ref = refs[-5:]
        n = pl.program_id(1)
        args, flags = _attn_args(refs[:-5], n, nct, nblk)
        _, vjp = jax.vjp(lambda *a: _attn_tile(*a, *flags), *args)
        dq4, dkp, dkc, dkn, dvp, dvc, dvn, dkx, dvx, dsinks = vjp(_stack_heads(do_ref))

        @pl.when(n == 0)
        def _():
            dk_ref[...] = jnp.zeros_like(dk_ref)
            dv_ref[...] = jnp.zeros_like(dv_ref)
            dsk_ref[...] = jnp.zeros_like(dsk_ref)

        _unstack_heads(dq_ref, dq4)
        for g in range(ATT_GROUP):
            dsk_ref[g:g + 1, :] += dsinks[g]
        for blk, dkb, dvb in ((jnp.maximum(n - 1, 0), dkp, dvp), (n, dkc, dvc), (jnp.minimum(n + 1, nblk - 1), dkn, dvn)):
            rows = pl.ds(pl.multiple_of(blk * CHUNK, CHUNK), CHUNK)
            dk_ref[rows, :] += dkb
            dv_ref[rows, :] += dvb
        dk_ref[0:tc, :] += dkx
        dv_ref[0:tc, :] += dvx

    kvacc = pl.BlockSpec((None, t, LANE), lambda kv, n: (kv, 0, 0))
    return pl.pallas_call(
        body, name=name, grid=(ATT_KV, nblk),
        in_specs=_attn_specs(t, tc) + [pl.BlockSpec((CHUNK, hw), lambda kv, n: (n, kv))],
        out_specs=(pl.BlockSpec((CHUNK, hw), lambda kv, n: (n, kv)), kvacc, kvacc,
                   pl.BlockSpec((None, 8, LANE), lambda kv, n: (kv, 0, 0))),
        out_shape=(_sds((t, ATT_HEADS * LANE), F32), _sds((ATT_KV, t, LANE), F32), _sds((ATT_KV, t, LANE), F32),
                   _sds((ATT_KV, 8, LANE), F32)),
        compiler_params=_cp(("parallel", "arbitrary")),
    )(qk, qk, qk, qk, p, p, p, qk, p, sink, do)


def sink_rows(sink):
    s = jnp.broadcast_to(sink.reshape(ATT_KV, ATT_GROUP, 1), (ATT_KV, ATT_GROUP, LANE))
    return jnp.pad(s, ((0, 0), (0, 8 - ATT_GROUP), (0, 0)))


def odd_mixer_fwd(h, w_qkv, w_out, li, sink, cos, sin, tc, tag):
    p = mm_nn(h, w_qkv, li, "n", name=f"{tag}_qkv")
    qk = rope_fwd(p, cos, sin, name=f"{tag}_rope")
    att = attn_fwd(qk, p, sink, tc, name=f"{tag}_att")
    o = mm_nn(att, w_out, li, "k", name=f"{tag}_out")
    return o, (p, qk, att)


def odd_mixer_bwd(saved, do, ht, w_qkv, w_out, g_qkv, g_out, li, sink, cos, sin, tc, tag):
    p, qk, att = saved
    g_out = mm_tn(att, do, g_out, li, "k", name=f"{tag}_out_dw")
    datt = mm_nt(do, w_out, li, "k", name=f"{tag}_out_dx")
    dq, dk, dv, dsink = attn_bwd(qk, p, sink, datt, tc, name=f"{tag}_att_b")
    dp = rope_bwd(dq, dk, dv, cos, sin, name=f"{tag}_rope_b")
    dh = mm_nt(dp, w_qkv, li, "n", name=f"{tag}_qkv_dx")
    g_qkv = mm_tn(ht, dp, g_qkv, li, "n", name=f"{tag}_qkv_dw", x_is_transposed=True)
    return dh, g_qkv, g_out, dict(o_sink=dsink[:, :ATT_GROUP, 0].reshape(-1))


ANY = pl.BlockSpec(memory_space=pl.ANY)


def _place():
    return lax.axis_index("x"), lax.axis_index("y"), lax.axis_index("c")


DMA_PIECES = 16


def _pieces(shape):
    if len(shape) < 2:
        return [()]
    lead, k = shape[:-2], shape[-2]
    split = 1
    while math.prod(lead) * split < DMA_PIECES and k % (2 * split) == 0 and (k // (2 * split)) % 16 == 0:
        split *= 2
    rows = k // split
    out = []
    for li in itertools.product(*[range(n) for n in lead]):
        out += [li + (pl.ds(q * rows, rows),) for q in range(split)]
    return out


def _start_pieces(make, src, dst):
    for idx in _pieces(src.shape):
        make(src.at[idx] if idx else src, dst.at[idx] if idx else dst).start()


def allgather8(blk, *, name):
    def body(x_ref, out_ref, send_sems, recv_sems, local_sem):
        x, y, c = _place()
        me, sibling = (x, y, c), (x, y, 1 - c)
        chips = [(1 - x, y), (x, 1 - y), (1 - x, 1 - y)]

        def slot(px, py, pc):
            return out_ref.at[4 * px + 2 * py + pc]

        def remote(k, to):
            return lambda src, dst: pltpu.make_async_remote_copy(
                src_ref=src, dst_ref=dst, send_sem=send_sems.at[k], recv_sem=recv_sems.at[k], device_id=to, device_id_type=MESH_ID)

        def local(src, dst):
            return pltpu.make_async_copy(src, dst, local_sem)

        _start_pieces(local, x_ref, slot(*me))
        _start_pieces(remote(0, sibling), x_ref, slot(*me))
        for j, chip in enumerate(chips):
            remote(1 + j, (*chip, c))(x_ref, slot(*me)).start()
        for j, chip in enumerate(chips):
            blk = slot(*chip, c)
            remote(1 + j, me)(blk, blk).wait_recv()
            _start_pieces(remote(4 + j, sibling), blk, blk)
        remote(0, me)(slot(*sibling), slot(*sibling)).wait_recv()
        for j, chip in enumerate(chips):
            blk = slot(*chip, 1 - c)
            remote(4 + j, me)(blk, blk).wait_recv()
        remote(0, sibling)(x_ref, slot(*me)).wait_send()
        for j, chip in enumerate(chips):
            remote(1 + j, (*chip, c))(x_ref, slot(*me)).wait_send()
            remote(4 + j, sibling)(slot(*chip, c), slot(*chip, c)).wait_send()
        local(x_ref, slot(*me)).wait()

    return pl.pallas_call(
        body, name=name, out_shape=_sds((N_DEV,) + blk.shape, blk.dtype), in_specs=[ANY], out_specs=ANY,
        scratch_shapes=[pltpu.SemaphoreType.DMA((7,)), pltpu.SemaphoreType.DMA((7,)), pltpu.SemaphoreType.DMA],
        compiler_params=pltpu.CompilerParams(has_side_effects=True),
    )(blk)


def _flip(r, xi, yi):
    return (1 - xi if r & 2 else xi), (1 - yi if r & 1 else yi)


def _to_sibling(send_sem, recv_sem):
    x, y, c = _place()
    return lambda src, dst: pltpu.make_async_remote_copy(src_ref=src, dst_ref=dst, send_sem=send_sem, recv_sem=recv_sem,
                                                         device_id=(x, y, 1 - c), device_id_type=MESH_ID)


def rs_sibling(gs, *, name):
    n = len(gs)

    def body(*refs):
        g_refs, out_refs, (send_sems, recv_sems) = refs[:n], refs[n:2 * n], refs[2 * n:]
        c = lax.axis_index("c")
        copies = [(_to_sibling(send_sems.at[i], recv_sems.at[i]), g_ref.at[:, pl.ds(1 - c, 1)], out_ref)
                  for i, (g_ref, out_ref) in enumerate(zip(g_refs, out_refs))]
        for remote, src, dst in copies:
            _start_pieces(remote, src, dst)
        for remote, src, dst in copies:
            remote(src, dst).wait()

    return pl.pallas_call(
        body, name=name, out_shape=[_sds((g.shape[0], 1) + g.shape[2:], g.dtype) for g in gs],
        in_specs=[ANY] * n, out_specs=[ANY] * n, scratch_shapes=[pltpu.SemaphoreType.DMA((n,)), pltpu.SemaphoreType.DMA((n,))],
        compiler_params=pltpu.CompilerParams(has_side_effects=True),
    )(*gs)


def sibling_swap(halves, *, name):
    n = len(halves)

    def body(*refs):
        h_refs, out_refs, (send_sems, recv_sems) = refs[:n], refs[n:2 * n], refs[2 * n:]
        copies = [(_to_sibling(send_sems.at[i], recv_sems.at[i]), h_ref, out_ref)
                  for i, (h_ref, out_ref) in enumerate(zip(h_refs, out_refs))]
        for remote, src, dst in copies:
            _start_pieces(remote, src, dst)
        for remote, src, dst in copies:
            remote(src, dst).wait()

    return pl.pallas_call(
        body, name=name, out_shape=[_sds(h.shape, h.dtype) for h in halves], in_specs=[ANY] * n, out_specs=[ANY] * n,
        scratch_shapes=[pltpu.SemaphoreType.DMA((n,)), pltpu.SemaphoreType.DMA((n,))],
        compiler_params=pltpu.CompilerParams(has_side_effects=True),
    )(*halves)


HBM_SPEC = pl.BlockSpec(memory_space=pltpu.HBM)
SEM_SPEC = pl.BlockSpec(memory_space=pltpu.SEMAPHORE)
DATAFLOW = pltpu.SideEffectType.DATAFLOW_SIDE_EFFECTING


def _hbm(a):
    return pltpu.with_memory_space_constraint(a, pltpu.HBM)


def _split_start(srcs, land_shapes, starts, *, name):
    n = len(srcs)

    def body(*refs):
        src_refs, land_refs, (send_sem, recv_sem), token = refs[:n], refs[n:2 * n], refs[2 * n:2 * n + 2], refs[-1]
        starts(src_refs, land_refs, send_sem, recv_sem)
        token[...] = jnp.zeros_like(token)

    out = pl.pallas_call(
        body, name=name,
        out_shape=[pltpu.SemaphoreType.DMA(()), pltpu.SemaphoreType.DMA(())] + [pltpu.HBM(s.shape, s.dtype) for s in srcs]
        + [pltpu.HBM(shape, s.dtype) for shape, s in zip(land_shapes, srcs)] + [_sds((8, LANE), F32)],
        in_specs=[HBM_SPEC] * (2 * n), out_specs=[SEM_SPEC, SEM_SPEC] + [HBM_SPEC] * (2 * n) + [pl.BlockSpec(memory_space=pltpu.VMEM)],
        input_output_aliases={i: 2 + i for i in range(2 * n)},
        compiler_params=pltpu.CompilerParams(has_side_effects=DATAFLOW),
    )(*[_hbm(s) for s in srcs], *[_hbm(lax.empty(shape, s.dtype)) for shape, s in zip(land_shapes, srcs)])
    return out[0], out[1], out[2:2 + n], out[2 + n:2 + 2 * n], out[-1]


def _split_wait(handle, after, sent, landed, *, name):
    send_sem, recv_sem, srcs, lands, _ = handle
    n = len(srcs)

    def body(*refs):
        src_refs, land_refs, (send_sem, recv_sem) = refs[:n], refs[n:2 * n], refs[2 * n:2 * n + 2]
        x, y, c = _place()
        for sized, wait in ((sent, "wait_send"), (landed, "wait_recv")):
            for src_ref, land_ref in zip(src_refs, land_refs):
                ref = sized(src_ref, land_ref)
                getattr(pltpu.make_async_remote_copy(src_ref=ref, dst_ref=ref, send_sem=send_sem, recv_sem=recv_sem,
                                                     device_id=(x, y, c), device_id_type=MESH_ID), wait)()

    out = pl.pallas_call(
        body, name=name, out_shape=[pltpu.HBM(a.shape, a.dtype) for a in (*srcs, *lands)],
        in_specs=[HBM_SPEC] * (2 * n) + [SEM_SPEC, SEM_SPEC, ANY], out_specs=[HBM_SPEC] * (2 * n),
        input_output_aliases={i: i for i in range(2 * n)},
        compiler_params=pltpu.CompilerParams(has_side_effects=DATAFLOW),
    )(*srcs, *lands, send_sem, recv_sem, after)
    return out[:n], out[n:]


def ag_send_start(blks, *, name):
    def starts(src_refs, land_refs, send_sem, recv_sem):
        x, y, c = _place()
        me = 4 * x + 2 * y + c
        for to in ((x, y, 1 - c), (1 - x, y, c), (x, 1 - y, c), (1 - x, 1 - y, c)):
            for src_ref, land_ref in zip(src_refs, land_refs):
                pltpu.make_async_remote_copy(src_ref=src_ref, dst_ref=land_ref.at[me], send_sem=send_sem, recv_sem=recv_sem,
                                             device_id=to, device_id_type=MESH_ID).start()

    return _split_start(blks, [(N_DEV,) + b.shape for b in blks], starts, name=name)


def ag_send_wait(handle, after, *, name):
    four = lambda src_ref, land_ref: land_ref.at[pl.ds(0, 4)]
    return _split_wait(handle, after, four, four, name=name)[1]


def ag_forward(lands, sibling_blks, *, name):
    n = len(lands)

    def body(*refs):
        land_refs, blk_refs, out_refs, (send_sems, recv_sems) = refs[:n], refs[n:2 * n], refs[2 * n:3 * n], refs[3 * n:]
        x, y, c = _place()
        for i, (land_ref, blk_ref, out_ref) in enumerate(zip(land_refs, blk_refs, out_refs)):
            remote = _to_sibling(send_sems.at[i], recv_sems.at[i])
            for r in (1, 2, 3):
                px, py = _flip(r, x, y)
                slot = 4 * px + 2 * py + c
                _start_pieces(remote, land_ref.at[slot], out_ref.at[slot])
            _start_pieces(remote, blk_ref, out_ref.at[4 * x + 2 * y + 1 - c])
        for i, out_ref in enumerate(out_refs):
            four = out_ref.at[pl.ds(0, 4)]
            _to_sibling(send_sems.at[i], recv_sems.at[i])(four, four).wait()

    return pl.pallas_call(
        body, name=name, out_shape=[_sds(a.shape, a.dtype) for a in lands], in_specs=[ANY] * (2 * n), out_specs=[ANY] * n,
        scratch_shapes=[pltpu.SemaphoreType.DMA((n,)), pltpu.SemaphoreType.DMA((n,))], input_output_aliases={i: i for i in range(n)},
        compiler_params=pltpu.CompilerParams(has_side_effects=True),
    )(*lands, *sibling_blks)


def rs_chips_start(hs, *, name):
    def starts(src_refs, land_refs, send_sem, recv_sem):
        x, y, c = _place()
        for r in (1, 2, 3):
            px, py = _flip(r, x, y)
            for src_ref, land_ref in zip(src_refs, land_refs):
                pltpu.make_async_remote_copy(src_ref=src_ref.at[2 * px + py], dst_ref=land_ref.at[r - 1], send_sem=send_sem,
                                             recv_sem=recv_sem, device_id=(px, py, c), device_id_type=MESH_ID).start()

    return _split_start(hs, [(3,) + h.shape[1:] for h in hs], starts, name=name)


def rs_chips_wait(handle, after, *, name):
    return _split_wait(handle, after, lambda src_ref, land_ref: src_ref.at[pl.ds(0, 3)], lambda src_ref, land_ref: land_ref, name=name)


def _row_block(kd, nd):
    return _pick(kd, (max(32, (1 << 19) // nd // 32 * 32),))


def add_kept_half(g, recv, core, *, name):
    nchip, nl, kd, nd = g.shape
    lh = nl // 2
    tk = _row_block(kd, nd)

    def body(c_ref, g_ref, r_ref, o_ref):
        del c_ref
        o_ref[...] = (g_ref[...].astype(F32) + r_ref[...].astype(F32)).astype(o_ref.dtype)

    blk = lambda f: pl.BlockSpec((None, None, tk, nd), f)
    return pl.pallas_call(
        body, name=name, out_shape=_sds((nchip, lh, kd, nd), BF16),
        grid_spec=pltpu.PrefetchScalarGridSpec(
            num_scalar_prefetch=1, grid=(nchip, lh, kd // tk),
            in_specs=[blk(lambda j, l, i, c_ref: (j, c_ref[0] * lh + l, i, 0)), blk(lambda j, l, i, c_ref: (j, l, i, 0))],
            out_specs=blk(lambda j, l, i, c_ref: (j, l, i, 0))),
        compiler_params=_cp(("parallel", "parallel", "parallel")),
    )(core, g, recv)


def add_chip_parts(h, parts, chip, *, name):
    _, lh, kd, nd = h.shape
    tk = _row_block(kd, nd)

    def body(k_ref, h_ref, p0, p1, p2, o_ref):
        del k_ref
        o_ref[...] = h_ref[...].astype(F32) + p0[...].astype(F32) + p1[...].astype(F32) + p2[...].astype(F32)

    blk = lambda f: pl.BlockSpec((None, None, tk, nd), f)
    part = lambda r: blk(functools.partial(lambda r_, l, i, k_ref: (r_, l, i, 0), r))
    return pl.pallas_call(
        body, name=name, out_shape=_sds((lh, kd, nd), F32),
        grid_spec=pltpu.PrefetchScalarGridSpec(
            num_scalar_prefetch=1, grid=(lh, kd // tk),
            in_specs=[blk(lambda l, i, k_ref: (k_ref[0], l, i, 0)), part(0), part(1), part(2)],
            out_specs=pl.BlockSpec((None, tk, nd), lambda l, i, k_ref: (l, i, 0))),
        compiler_params=_cp(("parallel", "parallel")),
    )(chip, h, parts, parts, parts)


def sum_slots(a, out_dtype, *, name):
    n = a.shape[0]
    cols = a.shape[-1]
    a3 = a.reshape(n, -1, cols)
    rows = a3.shape[1]
    tr = _pick(rows, (max(32, (1 << 19) // cols // 32 * 32),))

    def body(*refs):
        acc = refs[0][...].astype(F32)
        for r in refs[1:n]:
            acc = acc + r[...].astype(F32)
        refs[n][...] = acc.astype(out_dtype)

    return pl.pallas_call(
        body, name=name, grid=(rows // tr,),
        in_specs=[pl.BlockSpec((None, tr, cols), functools.partial(lambda j, i: (j, i, 0), j)) for j in range(n)],
        out_specs=pl.BlockSpec((tr, cols), lambda i: (i, 0)),
        out_shape=_sds((rows, cols), out_dtype), compiler_params=_cp(("parallel",)),
    )(*([a3] * n)).reshape(a.shape[1:])


def unit_blocks(shards, ci):
    return [lax.dynamic_index_in_dim(w.reshape(2, w.shape[0] // 2, w.shape[1]), ci, axis=0, keepdims=False).astype(BF16)
            for w in shards]


def gather_finish(lands, sibling_blks, tag):
    full = ag_forward(lands, sibling_blks, name=f"{tag}_fwd")
    return [a.reshape(N_CHIP, 1, 2 * a.shape[1], a.shape[2]) for a in full]


def reduce_scatter_start(gs, tag):
    core = jnp.reshape(lax.axis_index("c"), (1,)).astype(jnp.int32)
    halves = [g.reshape(N_CHIP, 2, g.shape[1] // 2, g.shape[2]) for g in gs]
    recv = rs_sibling(halves, name=f"{tag}_rs1")
    chip_sums = [add_kept_half(h, r, core, name=f"{tag}_add1_{j}") for j, (h, r) in enumerate(zip(halves, recv))]
    return (rs_chips_start(chip_sums, name=f"{tag}_rs2_start"),)


def reduce_scatter_finish(pending, after, tag):
    _, handle = pending
    xi, yi, ci = _place()
    chip = jnp.reshape(2 * xi + yi, (1,)).astype(jnp.int32)
    chip_sums, parts = rs_chips_wait(handle, after, name=f"{tag}_rs2_wait")
    halves = [add_chip_parts(h, p, chip, name=f"{tag}_add2_{j}") for j, (h, p) in enumerate(zip(chip_sums, parts))]
    others = sibling_swap(halves, name=f"{tag}_rs3")
    out = []
    for half, other in zip(halves, others):
        first, second = jnp.where(ci == 0, half, other), jnp.where(ci == 0, other, half)
        out.append(jnp.concatenate([first, second], axis=0).reshape(-1, half.shape[-1]))
    return out


def mod_fwd(c16, w_mod, *, name):
    nl, d, ns = w_mod.shape
    tn = _pick(ns, (512,))

    def body(c_ref, w_ref, o_ref):
        o_ref[...] = jnp.dot(jax.nn.silu(c_ref[...]), w_ref[...], precision=HI, preferred_element_type=F32)

    return pl.pallas_call(
        body, name=name, grid=(nl, ns // tn),
        in_specs=[pl.BlockSpec((16, d), lambda l, j: (0, 0)), pl.BlockSpec((None, d, tn), lambda l, j: (l, 0, j))],
        out_specs=pl.BlockSpec((None, 16, tn), lambda l, j: (l, 0, j)),
        out_shape=_sds((nl, 16, ns), F32), compiler_params=_cp(("parallel", "parallel")),
    )(c16, w_mod)


def mod_bwd_w(c16, dm, *, name):
    nl, _, ns = dm.shape
    d = c16.shape[1]
    tn = _pick(ns, (512,))

    def body(c_ref, dm_ref, o_ref):
        o_ref[...] = lax.dot_general(jax.nn.silu(c_ref[...]), dm_ref[...], (((0,), (0,)), ((), ())), precision=HI,
                                     preferred_element_type=F32)

    return pl.pallas_call(
        body, name=name, grid=(nl, ns // tn),
        in_specs=[pl.BlockSpec((16, d), lambda l, j: (0, 0)), pl.BlockSpec((None, 16, tn), lambda l, j: (l, 0, j))],
        out_specs=pl.BlockSpec((None, d, tn), lambda l, j: (l, 0, j)),
        out_shape=_sds((nl, d, ns), F32), compiler_params=_cp(("parallel", "parallel")),
    )(c16, dm)


def mod_bwd_s(dm, w_mod, *, name):
    nl, d, ns = w_mod.shape
    td = _pick(d, (512,))

    def body(dm_ref, w_ref, o_ref):
        part = lax.dot_general(dm_ref[...], w_ref[...], (((1,), (1,)), ((), ())), precision=HI, preferred_element_type=F32)
        rowsum = jnp.sum(part[8:16], axis=0, keepdims=True)

        @pl.when(pl.program_id(1) == 0)
        def _():
            o_ref[...] = jnp.zeros_like(o_ref)

        o_ref[...] += jnp.broadcast_to(rowsum, o_ref.shape)

    return pl.pallas_call(
        body, name=name, grid=(d // td, nl),
        in_specs=[pl.BlockSpec((None, 16, ns), lambda i, l: (l, 0, 0)), pl.BlockSpec((None, td, ns), lambda i, l: (l, i, 0))],
        out_specs=pl.BlockSpec((8, td), lambda i, l: (0, i)),
        out_shape=_sds((8, d), F32), compiler_params=_cp(("parallel", "arbitrary")),
    )(dm, w_mod)


def colsum16(dm, *, name):
    nl, _, n = dm.shape
    tn = _pick(n, (2048,))

    def body(dm_ref, o_ref):
        o_ref[...] = jnp.broadcast_to(jnp.sum(dm_ref[...], axis=0, keepdims=True), o_ref.shape)

    return pl.pallas_call(
        body, name=name, grid=(nl, n // tn),
        in_specs=[pl.BlockSpec((None, 16, tn), lambda l, j: (l, 0, j))],
        out_specs=pl.BlockSpec((None, 8, tn), lambda l, j: (l, 0, j)),
        out_shape=_sds((nl, 8, n), F32), compiler_params=_cp(("parallel", "parallel")),
    )(dm)


def silu_grad_mul(g, c, *, name):
    def body(g_ref, c_ref, o_ref):
        _, vjp = jax.vjp(jax.nn.silu, c_ref[...])
        o_ref[...] = vjp(g_ref[...])[0]

    return pl.pallas_call(body, name=name, out_shape=_sds(g.shape, F32))(g, c)


def adamw(w, g, m, v, *, name):
    shape = w.shape
    cols = shape[-1] if len(shape) > 1 else LANE
    flat = [a.reshape(-1, cols) for a in (w, g, m, v)]
    rows = flat[0].shape[0]
    tr = _pick(rows, (max(8, (1 << 18) // cols // 8 * 8),)) if rows % 8 == 0 else rows
    c1 = 1.0 - ADAM_B1 ** ADAM_STEP
    c2 = 1.0 - ADAM_B2 ** ADAM_STEP

    def body(w_ref, g_ref, m_ref, v_ref, d_ref, nm_ref, nv_ref):
        gv = g_ref[...]
        nm = ADAM_B1 * m_ref[...] + (1.0 - ADAM_B1) * gv
        nv = ADAM_B2 * v_ref[...] + (1.0 - ADAM_B2) * (gv * gv)
        d_ref[...] = -ADAM_LR * ((nm / c1) / (jnp.sqrt(nv / c2) + ADAM_EPS) + ADAM_WD * w_ref[...])
        nm_ref[...] = nm
        nv_ref[...] = nv

    blk = pl.BlockSpec((tr, cols), lambda i: (i, 0))
    outs = pl.pallas_call(
        body, name=name, grid=(rows // tr,), in_specs=[blk] * 4, out_specs=(blk,) * 3,
        out_shape=(_sds((rows, cols), F32),) * 3, compiler_params=_cp(("parallel",)),
    )(*flat)
    return tuple(o.reshape(shape) for o in outs)


PACK_ELEMS = LANE * LANE


def _pack(arrs):
    flat = jnp.concatenate([a.reshape(-1).astype(F32) for a in arrs])
    return jnp.pad(flat, (0, (-flat.shape[0]) % PACK_ELEMS)).reshape(-1, LANE)


def _unpack(packed, shapes):
    flat = packed.reshape(-1)
    out, pos = [], 0
    for s in shapes:
        n = math.prod(s)
        out.append(flat[pos:pos + n].reshape(s))
        pos += n
    return out


def _chip_cols(a, chip, width):
    return lax.dynamic_slice_in_dim(a, chip * width, width, axis=a.ndim - 1)


def kernel(x, c, ctx, c_ctx, w_mod, b_mod, norm_w, w_ffn_in, w_ffn_out, e_w_in, e_conv_w, e_conv_b, e_dt_bias, e_a_log, e_d_skip, e_ssd_norm_w, e_sgu_w, e_sgu_b, e_w_out, o_w_qkv, o_sink, o_w_out, loss_target, m_c_ctx, m_w_mod, m_b_mod, m_norm_w, m_w_ffn_in, m_w_ffn_out, m_e_w_in, m_e_conv_w, m_e_conv_b, m_e_dt_bias, m_e_a_log, m_e_d_skip, m_e_ssd_norm_w, m_e_sgu_w, m_e_sgu_b, m_e_w_out, m_o_w_qkv, m_o_sink, m_o_w_out, v_c_ctx, v_w_mod, v_b_mod, v_norm_w, v_w_ffn_in, v_w_ffn_out, v_e_w_in, v_e_conv_w, v_e_conv_b, v_e_dt_bias, v_e_a_log, v_e_d_skip, v_e_ssd_norm_w, v_e_sgu_w, v_e_sgu_b, v_e_w_out, v_o_w_qkv, v_o_sink, v_o_w_out):
    xi, yi, ci = _place()
    chip = 2 * xi + yi
    me = 2 * chip + ci
    s, d = x.shape[1:]
    tc = ctx.shape[1]
    depth = w_mod.shape[0]
    n_even = e_w_in.shape[0]
    dq = norm_w.shape[-1]
    cq = e_conv_w.shape[-1]
    ns = w_mod.shape[-1]

    gath = allgather8(_pack([c, norm_w, e_conv_w]), name="ag_small").reshape(N_DEV, -1)
    c_all = gath[:, :d]
    per_chip = [_unpack(gath[2 * k, d:], [norm_w.shape, e_conv_w.shape]) for k in range(N_CHIP)]
    nw_full = jnp.concatenate([pc[0] for pc in per_chip], axis=-1)
    convw_full = jnp.concatenate([pc[1] for pc in per_chip], axis=-1)
    c16 = jnp.concatenate([c_all, jnp.broadcast_to(c_ctx[None], (8, d))], axis=0)

    mod_g = allgather8(mod_fwd(c16, w_mod, name="mod_fwd"), name="ag_mod")
    mod_all = jnp.concatenate([mod_g[2 * k] for k in range(N_CHIP)], axis=-1) + b_mod[:, None, :]
    mod_rows = jnp.stack([mod_all[:, 8], lax.dynamic_index_in_dim(mod_all, me, axis=1, keepdims=False)], axis=1)
    modtab = jnp.pad(mod_rows.reshape(depth, 2, 6, d), ((0, 0), (0, 0), (0, 2), (0, 0)))

    eps_ = [even_params(convw_full[i], e_conv_b[i], e_dt_bias[i], e_a_log[i], e_d_skip[i], e_ssd_norm_w[i], e_sgu_w[i], e_sgu_b[i])
            for i in range(n_even)]
    sinks = [sink_rows(o_sink[i]) for i in range(o_sink.shape[0])]
    cos, sin = rope_tables(tc, s)
    units = [(kind, l) for l in range(depth) for kind in ("mix", "ffn")]

    def unit_shards(kind, l):
        if kind == "ffn":
            return [w_ffn_in[l], w_ffn_out[l]]
        return [e_w_in[l // 2], e_w_out[l // 2]] if l % 2 == 0 else [o_w_qkv[l // 2], o_w_out[l // 2]]

    def unit_weights(kind, l, gathered):
        w_a, w_b = gathered
        if kind == "mix" and l % 2 == 0:
            w_a = even_cols_permute(jnp.moveaxis(w_a[:, 0], 0, 1).reshape(1, d, -1))[None]
        return w_a, w_b

    def unit_fwd(kind, l, u_in, mt, wts):
        nw = nw_full[l]
        w_a, w_b = wts
        if kind == "mix":
            h1, h1t = norm_mod_fwd(u_in, nw[0], mt, tc, 0, name=f"L{l}_norm1")
            if l % 2 == 0:
                o, ms = even_mixer_fwd(h1, w_a, w_b, 0, eps_[l // 2], tc, f"L{l}_mix")
            else:
                o, ms = odd_mixer_fwd(h1, w_a, w_b, 0, sinks[l // 2], cos, sin, tc, f"L{l}_mix")
            return resid_fwd(u_in, o, nw[1], mt, tc, 0, name=f"L{l}_res1"), (u_in, h1t, ms, o)
        h2, h2t = norm_mod_fwd(u_in, nw[2], mt, tc, 1, name=f"L{l}_norm2")
        p = mm_nn(h2, w_a, 0, "n", name=f"L{l}_ffn_in", out_dtype=BF16)
        a, at = swiglu_fwd(p, name=f"L{l}_swiglu")
        f = mm_nn(a, w_b, 0, "k", name=f"L{l}_ffn_out")
        return resid_fwd(u_in, f, nw[3], mt, tc, 1, name=f"L{l}_res2"), (u_in, h2t, p, at, f)

    def unit_bwd(kind, l, du_out, mt, wts, sv):
        nw = nw_full[l]
        w_a, w_b = wts
        zeros = lambda w: jnp.zeros(w.shape, BF16)
        if kind == "ffn":
            u1, h2t, p, at, f = sv
            df, acc_r = resid_bwd(f, nw[3], mt, du_out, tc, 1, name=f"L{l}_res2_b")
            g_b = mm_tn(at, df, zeros(w_b), 0, "k", name=f"L{l}_ffn_out_dw", x_is_transposed=True)
            da = mm_nt(df, w_b, 0, "k", name=f"L{l}_ffn_out_dx", out_dtype=BF16)
            dp = swiglu_bwd(p, da, name=f"L{l}_swiglu_b")
            dh2 = mm_nt(dp, w_a, 0, "n", name=f"L{l}_ffn_in_dx")
            g_a = mm_tn(h2t, dp, zeros(w_a), 0, "n", name=f"L{l}_ffn_in_dw", x_is_transposed=True)
            du_in, acc_n = norm_mod_bwd(u1, nw[2], mt, dh2, du_out, tc, 1, name=f"L{l}_norm2_b")
            return du_in, [g_a[:, 0], g_b[:, 0]], (acc_n, acc_r), None
        u0, h1t, ms, o = sv
        do, acc_r = resid_bwd(o, nw[1], mt, du_out, tc, 0, name=f"L{l}_res1_b")
        if l % 2 == 0:
            dh1, g_a, g_b, small = even_mixer_bwd(ms, do, h1t, w_a, w_b, zeros(w_a), zeros(w_b), 0, eps_[l // 2], tc, f"L{l}_mix")
            g_a = jnp.moveaxis(even_cols_unpermute(g_a[0, 0]).reshape(d, N_CHIP, -1), 1, 0)
        else:
            dh1, g_a, g_b, small = odd_mixer_bwd(ms, do, h1t, w_a, w_b, zeros(w_a), zeros(w_b), 0, sinks[l // 2], cos, sin, tc,
                                                 f"L{l}_mix")
            g_a = g_a[:, 0]
        du_in, acc_n = norm_mod_bwd(u0, nw[0], mt, dh1, du_out, tc, 0, name=f"L{l}_norm1_b")
        return du_in, [g_a, g_b[:, 0]], (acc_n, acc_r), small

    u = jnp.concatenate([ctx[0], x[0]], axis=0)
    shards = unit_shards(*units[0])
    handle = ag_send_start(unit_blocks(shards, ci), name="ag0_start")
    lands = ag_send_wait(handle, handle[4], name="ag0_wait")
    wts = [None] * len(units)
    wts[0] = unit_weights(*units[0], gather_finish(lands, unit_blocks(shards, 1 - ci), "ag0"))
    saved = [None] * len(units)
    prev = u
    for i, (kind, l) in enumerate(units):
        tok = 0.0
        if i + 1 < len(units):
            shards = unit_shards(*units[i + 1])
            blks, _ = lax.optimization_barrier((unit_blocks(shards, ci), prev))
            handle = ag_send_start(blks, name=f"ag{i + 1}_start")
            tok = handle[4][0, 0]
        prev = u
        u, saved[i] = unit_fwd(kind, l, u, modtab[l] + tok, wts[i])
        if i + 1 < len(units):
            lands = ag_send_wait(handle, u, name=f"ag{i + 1}_wait")
            wts[i + 1] = unit_weights(*units[i + 1], gather_finish(lands, unit_blocks(shards, 1 - ci), f"ag{i + 1}"))
    loss_part, du = loss_fwd_bwd(u, loss_target[0], tc, name="loss")
    loss = lax.psum(loss_part[0, 0], ("x", "y", "c"))

    accs, smalls, unit_grads = [None] * len(units), [None] * len(units), [None] * len(units)
    pending = None
    for i in reversed(range(len(units))):
        kind, l = units[i]
        tok = pending[1][4][0, 0] if pending is not None else 0.0
        du, gs, accs[i], smalls[i] = unit_bwd(kind, l, du, modtab[l] + tok, wts[i], saved[i])
        if pending is not None:
            unit_grads[pending[0]] = reduce_scatter_finish(pending, du, f"rs{pending[0]}")
        pending = (i,) + reduce_scatter_start(gs, f"rs{i}")
    grad_x = du[tc:][None]
    d_nw, d_mt = [None] * depth, [None] * depth
    for l in range(depth):
        (acc0, acc1), (acc2, acc3) = accs[2 * l], accs[2 * l + 1]
        d_nw[l] = jnp.stack([acc[0, 0] + acc[1, 0] for acc in (acc0, acc1, acc2, acc3)])
        d_mt[l] = jnp.stack([acc0[:, 1], acc0[:, 2], acc1[:, 1], acc2[:, 1], acc2[:, 2], acc3[:, 1]], axis=1)
    small_e = [smalls[2 * l] for l in range(0, depth, 2)]
    small_o = [smalls[2 * l] for l in range(1, depth, 2)]

    d_mt_all = jnp.stack(d_mt) + pending[1][4][0, 0]
    dmt_g = allgather8(jnp.pad(d_mt_all, ((0, 0), (0, 0), (0, 2), (0, 0))), name="ag_dmod")[:, :, :, :6]
    dm16 = jnp.concatenate([dmt_g[:, :, 1].transpose(1, 0, 2, 3).reshape(depth, N_DEV, 6 * d),
                            dmt_g[:, :, 0].transpose(1, 0, 2, 3).reshape(depth, N_DEV, 6 * d)], axis=1)
    dm_sh = _chip_cols(dm16, chip, ns)
    grad_w_mod = mod_bwd_w(c16, dm_sh, name="mod_bwd_w")
    grad_b_mod = colsum16(dm16, name="mod_bwd_b")[:, 0]
    ds_cc = mod_bwd_s(dm_sh, w_mod, name="mod_bwd_s")[0]

    stack_e = lambda key: jnp.stack([se[key] for se in small_e])
    small_names = ["e_conv_b", "e_dt_bias", "e_a_log", "e_d_skip", "e_ssd_norm_w", "e_sgu_w", "e_sgu_b"]
    small_parts = [jnp.stack(d_nw), stack_e("e_conv_w")] + [stack_e(k) for k in small_names]
    small_parts += [jnp.stack([so["o_sink"] for so in small_o]), 0.5 * ds_cc]
    small_shapes = [a.shape for a in small_parts]
    small_sum = sum_slots(allgather8(_pack(small_parts), name="ag_small_grads"), F32, name="small_grads_sum")
    (g_nw, g_convw, g_convb, g_dtb, g_alog, g_dskip, g_ssdnw, g_sguw, g_sgub, g_sink, g_scc) = _unpack(small_sum, small_shapes)
    grad_c_ctx = silu_grad_mul(jnp.broadcast_to(g_scc[None], (8, d)), jnp.broadcast_to(c_ctx[None], (8, d)), name="c_ctx_grad")[0]
    grads = dict(
        c_ctx=grad_c_ctx, w_mod=grad_w_mod, b_mod=grad_b_mod, norm_w=_chip_cols(g_nw, chip, dq),
        e_conv_w=_chip_cols(g_convw, chip, cq), e_conv_b=g_convb, e_dt_bias=g_dtb.reshape(e_dt_bias.shape),
        e_a_log=g_alog.reshape(e_a_log.shape), e_d_skip=g_dskip, e_ssd_norm_w=g_ssdnw, e_sgu_w=g_sguw, e_sgu_b=g_sgub,
        o_sink=g_sink)

    unit_grads[pending[0]] = reduce_scatter_finish(pending, grad_w_mod, f"rs{pending[0]}")
    grads["w_ffn_in"] = jnp.stack([unit_grads[2 * l + 1][0] for l in range(depth)])
    grads["w_ffn_out"] = jnp.stack([unit_grads[2 * l + 1][1] for l in range(depth)])
    grads["e_w_in"] = jnp.stack([unit_grads[2 * l][0] for l in range(0, depth, 2)])
    grads["e_w_out"] = jnp.stack([unit_grads[2 * l][1] for l in range(0, depth, 2)])
    grads["o_w_qkv"] = jnp.stack([unit_grads[2 * l][0] for l in range(1, depth, 2)])
    grads["o_w_out"] = jnp.stack([unit_grads[2 * l][1] for l in range(1, depth, 2)])

    weights = dict(c_ctx=c_ctx, w_mod=w_mod, b_mod=b_mod, norm_w=norm_w, w_ffn_in=w_ffn_in, w_ffn_out=w_ffn_out, e_w_in=e_w_in,
                   e_conv_w=e_conv_w, e_conv_b=e_conv_b, e_dt_bias=e_dt_bias, e_a_log=e_a_log, e_d_skip=e_d_skip,
                   e_ssd_norm_w=e_ssd_norm_w, e_sgu_w=e_sgu_w, e_sgu_b=e_sgu_b, e_w_out=e_w_out, o_w_qkv=o_w_qkv, o_sink=o_sink,
                   o_w_out=o_w_out)
    ms_ = dict(c_ctx=m_c_ctx, w_mod=m_w_mod, b_mod=m_b_mod, norm_w=m_norm_w, w_ffn_in=m_w_ffn_in, w_ffn_out=m_w_ffn_out,
               e_w_in=m_e_w_in, e_conv_w=m_e_conv_w, e_conv_b=m_e_conv_b, e_dt_bias=m_e_dt_bias, e_a_log=m_e_a_log,
               e_d_skip=m_e_d_skip, e_ssd_norm_w=m_e_ssd_norm_w, e_sgu_w=m_e_sgu_w, e_sgu_b=m_e_sgu_b, e_w_out=m_e_w_out,
               o_w_qkv=m_o_w_qkv, o_sink=m_o_sink, o_w_out=m_o_w_out)
    vs_ = dict(c_ctx=v_c_ctx, w_mod=v_w_mod, b_mod=v_b_mod, norm_w=v_norm_w, w_ffn_in=v_w_ffn_in, w_ffn_out=v_w_ffn_out,
               e_w_in=v_e_w_in, e_conv_w=v_e_conv_w, e_conv_b=v_e_conv_b, e_dt_bias=v_e_dt_bias, e_a_log=v_e_a_log,
               e_d_skip=v_e_d_skip, e_ssd_norm_w=v_e_ssd_norm_w, e_sgu_w=v_e_sgu_w, e_sgu_b=v_e_sgu_b, e_w_out=v_e_w_out,
               o_w_qkv=v_o_w_qkv, o_sink=v_o_sink, o_w_out=v_o_w_out)
    names = list(weights)
    big = ("w_mod", "w_ffn_in", "w_ffn_out", "e_w_in", "e_w_out", "o_w_qkv", "o_w_out")
    small = [n for n in names if n not in big]
    delta, new_m, new_v = {}, {}, {}
    for n in big:
        delta[n], new_m[n], new_v[n] = adamw(weights[n], grads[n], ms_[n], vs_[n], name=f"adamw_{n}")
    packed = adamw(*[_pack([tab[n] for n in small]) for tab in (weights, grads, ms_, vs_)], name="adamw_small")
    shapes = [weights[n].shape for n in small]
    for tab, pk in zip((delta, new_m, new_v), packed):
        for n, val in zip(small, _unpack(pk, shapes)):
            tab[n] = val
    return (loss, grad_x, *[grads[n] for n in names], *[delta[n] for n in names], *[new_m[n] for n in names],
            *[new_v[n] for n in names])
```

```python
import functools
import itertools
import math

import jax
import jax.numpy as jnp
from jax import lax
from jax.experimental import pallas as pl
from jax.experimental.pallas import tpu as pltpu

F32 = jnp.float32
BF16 = jnp.bfloat16
HI = lax.Precision.HIGHEST

NORM_EPS = 1e-6
SSD_HEAD_DIM = 64
SSD_STATE = 128
CHUNK = 128
CONV_K = 5
ATT_HEAD_DIM = 128
ATT_GROUP = 4
ROPE_BASE = 10000.0
GRID_W = 64
NEG_INF = -1e30
ADAM_LR, ADAM_B1, ADAM_B2, ADAM_EPS, ADAM_WD, ADAM_STEP = 0.001, 0.9, 0.999, 1e-08, 0.01, 10

LANE = 128
VMEM_LIMIT = 56 * 1024 * 1024
MESH_ID = pl.DeviceIdType.MESH
N_DEV = 8
N_CHIP = 4


def _cp(sem=None):
    return pltpu.CompilerParams(dimension_semantics=sem, vmem_limit_bytes=VMEM_LIMIT)


def _sds(shape, dtype):
    return jax.ShapeDtypeStruct(tuple(shape), dtype)


def _pick(n, cands):
    for c in cands:
        if n % c == 0:
            return c
    for step in (LANE, 16, 8):
        for c in range(min(n, cands[0]) // step * step, 0, -step):
            if n % c == 0:
                return c
    raise ValueError((n, cands))


def _w_index(blocked, layer, per_block_k, per_block_n):
    def idx(kblk, nblk):
        if blocked == "n":
            return (nblk // per_block_n, layer, kblk, nblk % per_block_n)
        return (kblk // per_block_k, layer, kblk % per_block_k, nblk)
    return idx


def mm_nn(a, w, layer, blocked, *, name, out_dtype=F32, tm=None, tn=None, tk=None):
    m, k_total = a.shape
    cb, _, kd, nd = w.shape
    n_total = nd * cb if blocked == "n" else nd
    assert k_total == (kd if blocked == "n" else kd * cb)
    tm = tm or _pick(m, (1088, 192))
    tn = tn or _pick(nd, (2048, 1408, 768, 512))
    tk = tk or _pick(kd, (2048, 1408, 512))
    nk = k_total // tk
    widx = _w_index(blocked, layer, kd // tk, nd // tn)

    def body(a_ref, w_ref, o_ref, acc_ref):
        kk = pl.program_id(2)
        part = jnp.dot(a_ref[...].astype(BF16), w_ref[...].astype(BF16), preferred_element_type=F32)

        @pl.when(kk == 0)
        def _():
            acc_ref[...] = part

        @pl.when(kk > 0)
        def _():
            acc_ref[...] += part

        @pl.when(kk == nk - 1)
        def _():
            o_ref[...] = acc_ref[...].astype(o_ref.dtype)

    return pl.pallas_call(
        body, name=name, grid=(m // tm, n_total // tn, nk),
        in_specs=[pl.BlockSpec((tm, tk), lambda i, j, k: (i, k)),
                  pl.BlockSpec((None, None, tk, tn), lambda i, j, k: widx(k, j))],
        out_specs=pl.BlockSpec((tm, tn), lambda i, j, k: (i, j)),
        out_shape=_sds((m, n_total), out_dtype),
        scratch_shapes=[pltpu.VMEM((tm, tn), F32)],
        compiler_params=_cp(("parallel", "parallel", "arbitrary")),
    )(a, w)


def mm_nt(dy, w, layer, blocked, *, name, out_dtype=F32, tm=None, tn=None, tk=None):
    m, n_total = dy.shape
    cb, _, kd, nd = w.shape
    k_total = kd if blocked == "n" else kd * cb
    assert n_total == (nd * cb if blocked == "n" else nd)
    tm = tm or _pick(m, (1088, 192))
    tn = tn or _pick(kd, (2048, 1408, 512))
    tk = tk or _pick(nd, (1408, 1024, 768))
    nk = n_total // tk
    widx = _w_index(blocked, layer, kd // tn, nd // tk)

    def body(a_ref, w_ref, o_ref, acc_ref):
        kk = pl.program_id(2)
        part = lax.dot_general(a_ref[...].astype(BF16), w_ref[...].astype(BF16), (((1,), (1,)), ((), ())),
                               preferred_element_type=F32)

        @pl.when(kk == 0)
        def _():
            acc_ref[...] = part

        @pl.when(kk > 0)
        def _():
            acc_ref[...] += part

        @pl.when(kk == nk - 1)
        def _():
            o_ref[...] = acc_ref[...].astype(o_ref.dtype)

    return pl.pallas_call(
        body, name=name, grid=(m // tm, k_total // tn, nk),
        in_specs=[pl.BlockSpec((tm, tk), lambda i, j, k: (i, k)),
                  pl.BlockSpec((None, None, tn, tk), lambda i, j, k: widx(j, k))],
        out_specs=pl.BlockSpec((tm, tn), lambda i, j, k: (i, j)),
        out_shape=_sds((m, k_total), out_dtype),
        scratch_shapes=[pltpu.VMEM((tm, tn), F32)],
        compiler_params=_cp(("parallel", "parallel", "arbitrary")),
    )(dy, w)


def mm_tn(x, dy, g, layer, blocked, *, name, tm=None, tn=None, tt=None, x_is_transposed=False):
    k_total, t_total = x.shape if x_is_transposed else x.shape[::-1]
    n_total = dy.shape[1]
    cb, nl, kd, nd = g.shape
    assert nl == 1 and layer == 0
    assert k_total == (kd if blocked == "n" else kd * cb) and n_total == (nd * cb if blocked == "n" else nd)
    tm = tm or _pick(kd, (1024, 1408, 512))
    tn = tn or _pick(nd, (1408, 768, 512))
    tt = tt or _pick(t_total, (2176,) if x_is_transposed else (1088, 96))
    nt = t_total // tt
    widx = _w_index(blocked, layer, kd // tm, nd // tn)
    x_spec = pl.BlockSpec((tm, tt), lambda i, j, t: (i, t)) if x_is_transposed else pl.BlockSpec((tt, tm), lambda i, j, t: (t, i))
    x_dim = 1 if x_is_transposed else 0

    def body(x_ref, dy_ref, o_ref, acc_ref):
        tstep = pl.program_id(2)
        part = lax.dot_general(x_ref[...].astype(BF16), dy_ref[...].astype(BF16), (((x_dim,), (0,)), ((), ())),
                               preferred_element_type=F32)

        @pl.when(tstep == 0)
        def _():
            acc_ref[...] = part

        @pl.when(tstep > 0)
        def _():
            acc_ref[...] += part

        @pl.when(tstep == nt - 1)
        def _():
            o_ref[...] = acc_ref[...].astype(o_ref.dtype)

    return pl.pallas_call(
        body, name=name, grid=(k_total // tm, n_total // tn, nt),
        in_specs=[x_spec, pl.BlockSpec((tt, tn), lambda i, j, t: (t, j))],
        out_specs=pl.BlockSpec((None, None, tm, tn), lambda i, j, t: widx(i, j)),
        out_shape=_sds(g.shape, g.dtype),
        scratch_shapes=[pltpu.VMEM((tm, tn), F32)],
        compiler_params=_cp(("parallel", "parallel", "arbitrary")),
    )(x, dy)


def _rms(x, w):
    return x * lax.rsqrt(jnp.mean(x * x, axis=-1, keepdims=True) + NORM_EPS) * w


def _row_tile(tc):
    return 256 if tc % 256 == 0 else 128


def _seg_spec(nct, d):
    return pl.BlockSpec((None, 8, d), lambda i: (jnp.minimum(i // nct, 1), 0, 0))


def _acc_rows(acc_ref, i, nct, rows):
    @pl.when((i == 0) | (i == nct))
    def _():
        acc_ref[...] = jnp.zeros_like(acc_ref)

    for r, val in enumerate(rows):
        acc_ref[r:r + 1, :] += val


def norm_mod_fwd(u, nw, modtab, tc, which, *, name):
    t, d = u.shape
    tr = _row_tile(tc)
    nct = tc // tr
    r0 = 3 * which

    def body(u_ref, nw_ref, mt_ref, h_ref, ht_ref):
        sh, sc = mt_ref[r0:r0 + 1, :], mt_ref[r0 + 1:r0 + 2, :]
        h = _rms(u_ref[...], nw_ref[...]) * (1.0 + sc) + sh
        h_ref[...] = h.astype(h_ref.dtype)
        ht_ref[...] = h.T.astype(ht_ref.dtype)

    return pl.pallas_call(
        body, name=name, grid=(t // tr,),
        in_specs=[pl.BlockSpec((tr, d), lambda i: (i, 0)), pl.BlockSpec((1, d), lambda i: (0, 0)), _seg_spec(nct, d)],
        out_specs=(pl.BlockSpec((tr, d), lambda i: (i, 0)), pl.BlockSpec((d, tr), lambda i: (0, i))),
        out_shape=(_sds((t, d), BF16), _sds((d, t), BF16)), compiler_params=_cp(("arbitrary",)),
    )(u, nw.reshape(1, d), modtab)


def norm_mod_bwd(u, nw, modtab, dh, du_in, tc, which, *, name):
    t, d = u.shape
    tr = _row_tile(tc)
    nct = tc // tr
    r0 = 3 * which

    def body(u_ref, nw_ref, mt_ref, dh_ref, dui_ref, du_ref, acc_ref):
        i = pl.program_id(0)
        sh, sc = mt_ref[r0:r0 + 1, :], mt_ref[r0 + 1:r0 + 2, :]
        _, vjp = jax.vjp(lambda x, w, a, b: _rms(x, w) * (1.0 + b) + a, u_ref[...], nw_ref[...], sh, sc)
        dx, dw, dsh, dsc = vjp(dh_ref[...].astype(F32))
        du_ref[...] = dui_ref[...] + dx
        _acc_rows(acc_ref, i, nct, (dw, dsh, dsc))

    row = pl.BlockSpec((tr, d), lambda i: (i, 0))
    return pl.pallas_call(
        body, name=name, grid=(t // tr,),
        in_specs=[row, pl.BlockSpec((1, d), lambda i: (0, 0)), _seg_spec(nct, d), row, row],
        out_specs=(row, _seg_spec(nct, d)),
        out_shape=(_sds((t, d), F32), _sds((2, 8, d), F32)), compiler_params=_cp(("arbitrary",)),
    )(u, nw.reshape(1, d), modtab, dh, du_in)


def resid_fwd(u, o, nw, modtab, tc, which, *, name):
    t, d = u.shape
    tr = _row_tile(tc)
    nct = tc // tr
    r0 = 3 * which + 2

    def body(u_ref, o_ref, nw_ref, mt_ref, out_ref):
        out_ref[...] = u_ref[...] + mt_ref[r0:r0 + 1, :] * _rms(o_ref[...], nw_ref[...])

    row = pl.BlockSpec((tr, d), lambda i: (i, 0))
    return pl.pallas_call(
        body, name=name, grid=(t // tr,),
        in_specs=[row, row, pl.BlockSpec((1, d), lambda i: (0, 0)), _seg_spec(nct, d)],
        out_specs=row, out_shape=_sds((t, d), F32), compiler_params=_cp(("arbitrary",)),
    )(u, o, nw.reshape(1, d), modtab)


def resid_bwd(o, nw, modtab, du, tc, which, *, name):
    t, d = o.shape
    tr = _row_tile(tc)
    nct = tc // tr
    r0 = 3 * which + 2

    def body(o_ref, nw_ref, mt_ref, du_ref, do_ref, acc_ref):
        i = pl.program_id(0)
        _, vjp = jax.vjp(lambda x, w, g: g * _rms(x, w), o_ref[...], nw_ref[...], mt_ref[r0:r0 + 1, :])
        dx, dw, dg = vjp(du_ref[...])
        do_ref[...] = dx.astype(do_ref.dtype)
        _acc_rows(acc_ref, i, nct, (dw, dg))

    row = pl.BlockSpec((tr, d), lambda i: (i, 0))
    return pl.pallas_call(
        body, name=name, grid=(t // tr,),
        in_specs=[row, pl.BlockSpec((1, d), lambda i: (0, 0)), _seg_spec(nct, d), row],
        out_specs=(row, _seg_spec(nct, d)),
        out_shape=(_sds((t, d), BF16), _sds((2, 8, d), F32)), compiler_params=_cp(("arbitrary",)),
    )(o, nw.reshape(1, d), modtab, du)


def swiglu_fwd(p, *, name):
    t, h2 = p.shape
    h = h2 // 2
    tr = CHUNK

    def body(p_ref, a_ref, at_ref):
        a = jax.nn.silu(p_ref[:, :h].astype(F32)) * p_ref[:, h:].astype(F32)
        a_ref[...] = a.astype(a_ref.dtype)
        at_ref[...] = a.T.astype(at_ref.dtype)

    return pl.pallas_call(
        body, name=name, grid=(t // tr,),
        in_specs=[pl.BlockSpec((tr, h2), lambda i: (i, 0))],
        out_specs=(pl.BlockSpec((tr, h), lambda i: (i, 0)), pl.BlockSpec((h, tr), lambda i: (0, i))),
        out_shape=(_sds((t, h), BF16), _sds((h, t), BF16)), compiler_params=_cp(("parallel",)),
    )(p)


def swiglu_bwd(p, da, *, name):
    t, h2 = p.shape
    h = h2 // 2
    tr = CHUNK

    def body(p_ref, da_ref, dp_ref):
        _, vjp = jax.vjp(lambda g, u: jax.nn.silu(g) * u, p_ref[:, :h].astype(F32), p_ref[:, h:].astype(F32))
        dg, du = vjp(da_ref[...].astype(F32))
        dp_ref[:, :h] = dg.astype(dp_ref.dtype)
        dp_ref[:, h:] = du.astype(dp_ref.dtype)

    return pl.pallas_call(
        body, name=name, grid=(t // tr,),
        in_specs=[pl.BlockSpec((tr, h2), lambda i: (i, 0)), pl.BlockSpec((tr, h), lambda i: (i, 0))],
        out_specs=pl.BlockSpec((tr, h2), lambda i: (i, 0)),
        out_shape=_sds((t, h2), BF16), compiler_params=_cp(("parallel",)),
    )(p, da)


def loss_fwd_bwd(u, target, tc, *, name):
    t, d = u.shape
    tr = _row_tile(tc)
    nct = tc // tr

    def body(u_ref, t_ref, loss_ref, du_ref):
        i = pl.program_id(0)

        @pl.when(i == 0)
        def _():
            loss_ref[...] = jnp.zeros_like(loss_ref)

        @pl.when(i < nct)
        def _():
            du_ref[...] = jnp.zeros_like(du_ref)

        @pl.when(i >= nct)
        def _():
            err = u_ref[...] - t_ref[...]
            du_ref[...] = err * (1.0 / d)
            loss_ref[...] += jnp.sum(jnp.sum(err * err, axis=1, keepdims=True), axis=0, keepdims=True) * (0.5 / d)

    return pl.pallas_call(
        body, name=name, grid=(t // tr,),
        in_specs=[pl.BlockSpec((tr, d), lambda i: (i, 0)), pl.BlockSpec((tr, d), lambda i: (jnp.maximum(i - nct, 0), 0))],
        out_specs=(pl.BlockSpec((1, 1), lambda i: (0, 0)), pl.BlockSpec((tr, d), lambda i: (i, 0))),
        out_shape=(_sds((1, 1), F32), _sds((t, d), F32)), compiler_params=_cp(("arbitrary",)),
    )(u, target)


SSD_INNER = 1024
SGU_WIDTH = 1024
XBC_DIM = 1536
EVEN_COLS = 4640
EVEN_PAD_COLS = 5120
Z_BLK, U_BLK, V_BLK, X_BLK, B_BLK, C_BLK, DT_BLK = 0, 8, 16, 24, 32, 34, 36
PAD_ROWS = 8


def _conv_scratch_fill(pad_ref, val, tc, s):
    pad_ref[...] = jnp.zeros_like(pad_ref)
    pad_ref[PAD_ROWS:PAD_ROWS + tc, :] = val[:tc]
    pad_ref[2 * PAD_ROWS + tc:2 * PAD_ROWS + tc + s, :] = val[tc:]


def _conv_taps(pad_ref, tc, s, k):
    off = k - CONV_K // 2
    return (pad_ref[PAD_ROWS + off:PAD_ROWS + off + tc, :],
            pad_ref[2 * PAD_ROWS + tc + off:2 * PAD_ROWS + tc + off + s, :])


def conv_fwd(p, wb, tc, *, name):
    t = p.shape[0]
    s = t - tc
    nblk = XBC_DIM // LANE

    def body(p_ref, wb_ref, out_ref, pad_ref):
        _conv_scratch_fill(pad_ref, p_ref[...], tc, s)
        acc_c = jnp.zeros((tc, LANE), F32) + wb_ref[5:6, :]
        acc_l = jnp.zeros((s, LANE), F32) + wb_ref[5:6, :]
        for k in range(CONV_K):
            xc, xl = _conv_taps(pad_ref, tc, s, k)
            acc_c += xc * wb_ref[k:k + 1, :]
            acc_l += xl * wb_ref[k:k + 1, :]
        out_ref[:tc, :] = jax.nn.silu(acc_c)
        out_ref[tc:, :] = jax.nn.silu(acc_l)

    return pl.pallas_call(
        body, name=name, grid=(nblk,),
        in_specs=[pl.BlockSpec((t, LANE), lambda j: (0, X_BLK + j)), pl.BlockSpec((8, LANE), lambda j: (0, j))],
        out_specs=pl.BlockSpec((t, LANE), lambda j: (0, j)),
        out_shape=_sds((t, XBC_DIM), F32),
        scratch_shapes=[pltpu.VMEM((t + 3 * PAD_ROWS, LANE), F32)],
        compiler_params=_cp(("parallel",)),
    )(p, wb)


def conv_bwd(p, wb, dxg, dskip, tc, *, name):
    t = p.shape[0]
    s = t - tc
    nblk = XBC_DIM // LANE
    nx = SSD_INNER // LANE

    def grp(j):
        return jnp.where(j < nx, j // 4, (j - nx) % 2)

    def sub(j):
        return jnp.where(j < nx, j % 4, 4 + (j - nx) // 2)

    def body(p_ref, wb_ref, d0_ref, d1_ref, ds_ref, dp_ref, dwb_ref, pad_ref, dpad_ref):
        j = pl.program_id(0)
        _conv_scratch_fill(pad_ref, p_ref[...], tc, s)
        pre = [jnp.zeros((tc, LANE), F32) + wb_ref[5:6, :], jnp.zeros((s, LANE), F32) + wb_ref[5:6, :]]
        for k in range(CONV_K):
            xc, xl = _conv_taps(pad_ref, tc, s, k)
            pre[0] += xc * wb_ref[k:k + 1, :]
            pre[1] += xl * wb_ref[k:k + 1, :]
        dx = d0_ref[...] + d1_ref[...] + jnp.where(j < nx, ds_ref[...], 0.0)
        dpre = []
        for part, rows in ((0, slice(0, tc)), (1, slice(tc, t))):
            sig = jax.nn.sigmoid(pre[part])
            dpre.append(dx[rows] * (sig * (1.0 + pre[part] * (1.0 - sig))))
        dwb_ref[...] = jnp.zeros_like(dwb_ref)
        dwb_ref[5:6, :] = jnp.sum(dpre[0], axis=0, keepdims=True) + jnp.sum(dpre[1], axis=0, keepdims=True)
        for k in range(CONV_K):
            xc, xl = _conv_taps(pad_ref, tc, s, k)
            dwb_ref[k:k + 1, :] = (jnp.sum(dpre[0] * xc, axis=0, keepdims=True)
                                   + jnp.sum(dpre[1] * xl, axis=0, keepdims=True))
        _conv_scratch_fill(dpad_ref, jnp.concatenate(dpre, axis=0), tc, s)
        acc_c = jnp.zeros((tc, LANE), F32)
        acc_l = jnp.zeros((s, LANE), F32)
        for k in range(CONV_K):
            gc, gl = _conv_taps(dpad_ref, tc, s, CONV_K - 1 - k)
            acc_c += gc * wb_ref[k:k + 1, :]
            acc_l += gl * wb_ref[k:k + 1, :]
        dp_ref[:tc, :] = acc_c.astype(dp_ref.dtype)
        dp_ref[tc:, :] = acc_l.astype(dp_ref.dtype)

    col = pl.BlockSpec((t, LANE), lambda j: (0, j))
    return pl.pallas_call(
        body, name=name, grid=(nblk,),
        in_specs=[pl.BlockSpec((t, LANE), lambda j: (0, X_BLK + j)), pl.BlockSpec((8, LANE), lambda j: (0, j)),
                  pl.BlockSpec((None, None, t, LANE), lambda j: (0, grp(j), 0, sub(j))),
                  pl.BlockSpec((None, None, t, LANE), lambda j: (1, grp(j), 0, sub(j))),
                  pl.BlockSpec((t, LANE), lambda j: (0, jnp.minimum(j, nx - 1)))],
        out_specs=(col, pl.BlockSpec((8, LANE), lambda j: (0, j))),
        out_shape=(_sds((t, XBC_DIM), BF16), _sds((8, XBC_DIM), F32)),
        scratch_shapes=[pltpu.VMEM((t + 3 * PAD_ROWS, LANE), F32), pltpu.VMEM((t + 3 * PAD_ROWS, LANE), F32)],
        compiler_params=_cp(("parallel",)),
    )(p, wb, dxg, dxg, dskip)


HEADS_PER_DG = 8


def _ssd_prep_fn(pre, bias, alog, rev):
    q = pre.shape[0]
    lane = lax.broadcasted_iota(jnp.int32, (1, LANE), 1)
    dt = jnp.where(lane < HEADS_PER_DG, jax.nn.softplus(pre + bias), 0.0)
    row = lax.broadcasted_iota(jnp.int32, (q, q), 0)
    col = lax.broadcasted_iota(jnp.int32, (q, q), 1)
    tri = jnp.where((col - row) * jnp.where(rev, 1, -1) >= 0, 1.0, 0.0)
    cs = jnp.dot(tri, dt * (-jnp.exp(alog)), precision=HI, preferred_element_type=F32)
    return dt, cs


def _scan_chunk(nc_ctx, nch):
    def idx(dg, i):
        fwd = i
        bwd = jnp.where(i < nc_ctx, nc_ctx - 1 - i, nch - 1 - (i - nc_ctx))
        return jnp.where(dg // 2 == 0, fwd, bwd)
    return idx


def ssd_prep_fwd(pre, bias, alog, *, name):
    _, t, _ = pre.shape
    blk = pl.BlockSpec((None, CHUNK, LANE), lambda dg, i: (dg, i, 0))
    par = pl.BlockSpec((None, 1, LANE), lambda dg, i: (dg, 0, 0))

    def body(pre_ref, b_ref, a_ref, dt_ref, cs_ref):
        dt, cs = _ssd_prep_fn(pre_ref[...], b_ref[...], a_ref[...], pl.program_id(0) // 2 == 1)
        dt_ref[...] = dt
        cs_ref[...] = cs

    return pl.pallas_call(
        body, name=name, grid=(4, t // CHUNK), in_specs=[blk, par, par], out_specs=(blk, blk),
        out_shape=(_sds(pre.shape, F32), _sds(pre.shape, F32)), compiler_params=_cp(("parallel", "parallel")),
    )(pre, bias, alog)


def ssd_prep_bwd(pre, bias, alog, ddt, dcs, *, name):
    _, t, _ = pre.shape
    blk = pl.BlockSpec((None, CHUNK, LANE), lambda dg, i: (dg, i, 0))
    par = pl.BlockSpec((None, 1, LANE), lambda dg, i: (dg, 0, 0))

    def body(pre_ref, b_ref, a_ref, ddt_ref, dcs_ref, dpre_ref, acc_ref):
        rev = pl.program_id(0) // 2 == 1
        _, vjp = jax.vjp(lambda x, b, a: _ssd_prep_fn(x, b, a, rev), pre_ref[...], b_ref[...], a_ref[...])
        dpre, db, da = vjp((ddt_ref[...], dcs_ref[...]))
        dpre_ref[...] = dpre

        @pl.when(pl.program_id(1) == 0)
        def _():
            acc_ref[...] = jnp.zeros_like(acc_ref)

        acc_ref[0:1, :] += db
        acc_ref[1:2, :] += da

    return pl.pallas_call(
        body, name=name, grid=(4, t // CHUNK), in_specs=[blk, par, par, blk, blk],
        out_specs=(blk, pl.BlockSpec((None, 8, LANE), lambda dg, i: (dg, 0, 0))),
        out_shape=(_sds(pre.shape, F32), _sds((4, 8, LANE), F32)), compiler_params=_cp(("parallel", "arbitrary")),
    )(pre, bias, alog, ddt, dcs)


def _onehot_col(a, h):
    lane = lax.broadcasted_iota(jnp.int32, (1, a.shape[1]), 1)
    return jnp.sum(jnp.where(lane == h, a, 0.0), axis=1, keepdims=True)


def _onehot_row(a, h):
    sub = lax.broadcasted_iota(jnp.int32, (a.shape[0], 1), 0)
    return jnp.sum(jnp.where(sub == h, a, 0.0), axis=0, keepdims=True)


def _bdot(a, b, dims):
    return lax.dot_general(a.astype(BF16), b.astype(BF16), (dims, ((), ())), preferred_element_type=F32)


def _ssd_pair(xblk, dt, cs, bm, cm, sp, rev, pair):
    q = xblk.shape[0]
    lane = lax.broadcasted_iota(jnp.int32, (1, LANE), 1)
    sub = lax.broadcasted_iota(jnp.int32, (LANE, 1), 0)
    row = lax.broadcasted_iota(jnp.int32, (q, q), 0)
    col = lax.broadcasted_iota(jnp.int32, (q, q), 1)
    mask = (col - row) * jnp.where(rev, 1, -1) >= 0
    last = jnp.where(rev, 0, q - 1)
    cs_t = cs.T
    g = _bdot(cm, bm, ((1,), (1,)))
    y = jnp.zeros((q, LANE), F32)
    escale = jnp.zeros((q, LANE), F32)
    xw = jnp.zeros((q, LANE), F32)
    dec = jnp.zeros((LANE, 1), F32)
    for hh in range(2):
        h = 2 * pair + hh
        c_col = _onehot_col(cs, h)
        c_row = _onehot_row(cs_t, h)
        tot = jnp.sum(jnp.where(lax.broadcasted_iota(jnp.int32, (1, q), 1) == last, c_row, 0.0), axis=1, keepdims=True)
        in_head = (lane >= hh * SSD_HEAD_DIM) & (lane < (hh + 1) * SSD_HEAD_DIM)
        xh = jnp.where(in_head, xblk * _onehot_col(dt, h), 0.0)
        ldec = jnp.where(mask, jnp.exp(jnp.where(mask, c_col - c_row, 0.0)), 0.0)
        y = y + _bdot(g * ldec, xh, ((1,), (0,)))
        escale = escale + jnp.where(in_head, jnp.exp(c_col), 0.0)
        xw = xw + xh * jnp.exp(tot - c_col)
        dec = dec + jnp.where((sub >= hh * SSD_HEAD_DIM) & (sub < (hh + 1) * SSD_HEAD_DIM), jnp.exp(tot), 0.0)
    y = y + _bdot(cm, sp, ((1,), (1,))) * escale
    s_new = sp * dec + _bdot(xw, bm, ((0,), (0,)))
    return y, s_new


def ssd_fwd(xbc, dt, cs, tc, *, name):
    t = xbc.shape[0]
    nch = t // CHUNK
    sidx = _scan_chunk(tc // CHUNK, nch)
    gw = SSD_INNER // 2
    nb = SSD_INNER // LANE

    def body(x_ref, b_ref, c_ref, dt_ref, cs_ref, y_ref, sp_ref, s_ref):
        @pl.when(pl.program_id(1) == 0)
        def _():
            s_ref[...] = jnp.zeros_like(s_ref)

        rev = pl.program_id(0) // 2 == 1
        sp_ref[...] = s_ref[...]
        for p in range(gw // LANE):
            blk = slice(p * LANE, (p + 1) * LANE)
            y, s_new = _ssd_pair(x_ref[:, blk], dt_ref[...], cs_ref[...], b_ref[...], c_ref[...], s_ref[blk, :], rev, p)
            y_ref[:, blk] = y
            s_ref[blk, :] = s_new

    return pl.pallas_call(
        body, name=name, grid=(4, nch),
        in_specs=[pl.BlockSpec((CHUNK, gw), lambda dg, i: (sidx(dg, i), dg % 2)),
                  pl.BlockSpec((CHUNK, LANE), lambda dg, i: (sidx(dg, i), nb + dg % 2)),
                  pl.BlockSpec((CHUNK, LANE), lambda dg, i: (sidx(dg, i), nb + 2 + dg % 2)),
                  pl.BlockSpec((None, CHUNK, LANE), lambda dg, i: (dg, sidx(dg, i), 0)),
                  pl.BlockSpec((None, CHUNK, LANE), lambda dg, i: (dg, sidx(dg, i), 0))],
        out_specs=(pl.BlockSpec((None, CHUNK, gw), lambda dg, i: (dg // 2, sidx(dg, i), dg % 2)),
                   pl.BlockSpec((None, None, gw, SSD_STATE), lambda dg, i: (dg, sidx(dg, i), 0, 0))),
        out_shape=(_sds((2, t, SSD_INNER), F32), _sds((4, nch, gw, SSD_STATE), F32)),
        scratch_shapes=[pltpu.VMEM((gw, SSD_STATE), F32)],
        compiler_params=_cp(("parallel", "arbitrary")),
    )(xbc, xbc, xbc, dt, cs)


def ssd_bwd(xbc, dt, cs, sprev, dy, tc, *, name):
    t = xbc.shape[0]
    nch = t // CHUNK
    fidx = _scan_chunk(tc // CHUNK, nch)
    sidx = lambda dg, i: fidx(dg, nch - 1 - i)
    gw = SSD_INNER // 2
    nb = SSD_INNER // LANE

    def body(x_ref, b_ref, c_ref, dt_ref, cs_ref, sp_ref, dy_ref, dxg_ref, ddt_ref, dcs_ref, ds_ref):
        @pl.when(pl.program_id(1) == 0)
        def _():
            ds_ref[...] = jnp.zeros_like(ds_ref)

        rev = pl.program_id(0) // 2 == 1
        ddt = jnp.zeros((CHUNK, LANE), F32)
        dcs = jnp.zeros((CHUNK, LANE), F32)
        db = jnp.zeros((CHUNK, SSD_STATE), F32)
        dc = jnp.zeros((CHUNK, SSD_STATE), F32)
        for p in range(gw // LANE):
            blk = slice(p * LANE, (p + 1) * LANE)
            _, vjp = jax.vjp(functools.partial(_ssd_pair, rev=rev, pair=p),
                             x_ref[:, blk], dt_ref[...], cs_ref[...], b_ref[...], c_ref[...], sp_ref[blk, :])
            dx, ddt_p, dcs_p, db_p, dc_p, dsp = vjp((dy_ref[:, blk], ds_ref[blk, :]))
            dxg_ref[:, blk] = dx
            ds_ref[blk, :] = dsp
            ddt, dcs, db, dc = ddt + ddt_p, dcs + dcs_p, db + db_p, dc + dc_p
        dxg_ref[:, gw:gw + SSD_STATE] = db
        dxg_ref[:, gw + SSD_STATE:] = dc
        ddt_ref[...] = ddt
        dcs_ref[...] = dcs

    hd = pl.BlockSpec((None, CHUNK, LANE), lambda dg, i: (dg, sidx(dg, i), 0))
    return pl.pallas_call(
        body, name=name, grid=(4, nch),
        in_specs=[pl.BlockSpec((CHUNK, gw), lambda dg, i: (sidx(dg, i), dg % 2)),
                  pl.BlockSpec((CHUNK, LANE), lambda dg, i: (sidx(dg, i), nb + dg % 2)),
                  pl.BlockSpec((CHUNK, LANE), lambda dg, i: (sidx(dg, i), nb + 2 + dg % 2)),
                  hd, hd,
                  pl.BlockSpec((None, None, gw, SSD_STATE), lambda dg, i: (dg, sidx(dg, i), 0, 0)),
                  pl.BlockSpec((CHUNK, gw), lambda dg, i: (sidx(dg, i), dg % 2))],
        out_specs=(pl.BlockSpec((None, None, CHUNK, gw + 2 * SSD_STATE), lambda dg, i: (dg // 2, dg % 2, sidx(dg, i), 0)),
                   hd, hd),
        out_shape=(_sds((2, 2, t, gw + 2 * SSD_STATE), F32), _sds((4, t, LANE), F32), _sds((4, t, LANE), F32)),
        scratch_shapes=[pltpu.VMEM((gw, SSD_STATE), F32)],
        compiler_params=_cp(("parallel", "arbitrary")),
    )(xbc, xbc, xbc, dt, cs, sprev, dy)


def _ssd_finish_fn(y0, y1, xs, z, dskip, nw):
    y = (y0 + y1 + xs * dskip) * jax.nn.silu(z)
    half = y.shape[1] // 2
    first = lax.broadcasted_iota(jnp.int32, (1, y.shape[1]), 1) < half
    sq = y * y
    m0 = jnp.sum(jnp.where(first, sq, 0.0), axis=1, keepdims=True) / half
    m1 = jnp.sum(jnp.where(first, 0.0, sq), axis=1, keepdims=True) / half
    return y * jnp.where(first, lax.rsqrt(m0 + NORM_EPS), lax.rsqrt(m1 + NORM_EPS)) * nw


def ssd_finish_fwd(y, xbc, p, dskip, nw, *, name):
    t = xbc.shape[0]
    tr = CHUNK
    w = SSD_INNER
    row = pl.BlockSpec((tr, w), lambda i: (i, 0))
    par = pl.BlockSpec((1, w), lambda i: (0, 0))

    def body(y0_ref, y1_ref, x_ref, z_ref, ds_ref, nw_ref, o_ref):
        o_ref[...] = _ssd_finish_fn(y0_ref[...], y1_ref[...], x_ref[...], z_ref[...], ds_ref[...], nw_ref[...]).astype(o_ref.dtype)

    return pl.pallas_call(
        body, name=name, grid=(t // tr,),
        in_specs=[pl.BlockSpec((None, tr, w), lambda i: (0, i, 0)), pl.BlockSpec((None, tr, w), lambda i: (1, i, 0)),
                  row, row, par, par],
        out_specs=row, out_shape=_sds((t, w), BF16), compiler_params=_cp(("parallel",)),
    )(y, y, xbc, p, dskip, nw)


def ssd_finish_bwd(y, xbc, p, dskip, nw, dout, *, name):
    t = xbc.shape[0]
    tr = CHUNK
    w = SSD_INNER
    row = pl.BlockSpec((tr, w), lambda i: (i, 0))
    par = pl.BlockSpec((1, w), lambda i: (0, 0))

    def body(y0_ref, y1_ref, x_ref, z_ref, ds_ref, nw_ref, do_ref, dy_ref, dx_ref, dz_ref, acc_ref):
        _, vjp = jax.vjp(_ssd_finish_fn, y0_ref[...], y1_ref[...], x_ref[...], z_ref[...], ds_ref[...], nw_ref[...])
        dy0, _, dx, dz, dds, dnw = vjp(do_ref[...])
        dy_ref[...] = dy0
        dx_ref[...] = dx
        dz_ref[...] = dz.astype(dz_ref.dtype)

        @pl.when(pl.program_id(0) == 0)
        def _():
            acc_ref[...] = jnp.zeros_like(acc_ref)

        acc_ref[0:1, :] += dds
        acc_ref[1:2, :] += dnw

    return pl.pallas_call(
        body, name=name, grid=(t // tr,),
        in_specs=[pl.BlockSpec((None, tr, w), lambda i: (0, i, 0)), pl.BlockSpec((None, tr, w), lambda i: (1, i, 0)),
                  row, row, par, par, row],
        out_specs=(row, row, row, pl.BlockSpec((8, w), lambda i: (0, 0))),
        out_shape=(_sds((t, w), F32), _sds((t, w), F32), _sds((t, w), BF16), _sds((8, w), F32)),
        compiler_params=_cp(("arbitrary",)),
    )(y, y, xbc, p, dskip, nw, dout)


SGU_GROUPS = 8


def _sgu_fn(us, vs, ws, bs):
    n = SGU_GROUPS * LANE
    vf = [jax.nn.gelu(v) for v in vs]
    mu = sum(jnp.sum(v, axis=1, keepdims=True) for v in vf) / n
    var = sum(jnp.sum(jnp.square(v - mu), axis=1, keepdims=True) for v in vf) / n
    rstd = lax.rsqrt(var + NORM_EPS)
    return tuple(jax.nn.gelu(u) * (_bdot(w, (v - mu) * rstd, ((1,), (0,))) + b) for u, v, w, b in zip(us, vf, ws, bs))


def sgu_fwd(p, w, b, *, name):
    t = p.shape[0]
    wd = SGU_WIDTH

    def body(u_ref, v_ref, w_ref, b_ref, o_ref):
        sl = [slice(g * LANE, (g + 1) * LANE) for g in range(SGU_GROUPS)]
        ys = _sgu_fn([u_ref[:, s] for s in sl], [v_ref[:, s] for s in sl], [w_ref[g] for g in range(SGU_GROUPS)],
                     [b_ref[g] for g in range(SGU_GROUPS)])
        for s, yv in zip(sl, ys):
            o_ref[:, s] = yv.astype(o_ref.dtype)

    return pl.pallas_call(
        body, name=name, grid=(t // CHUNK,),
        in_specs=[pl.BlockSpec((CHUNK, wd), lambda i: (i, U_BLK * LANE // wd)), pl.BlockSpec((CHUNK, wd), lambda i: (i, V_BLK * LANE // wd)),
                  pl.BlockSpec((SGU_GROUPS, CHUNK, CHUNK), lambda i: (0, 0, 0)), pl.BlockSpec((SGU_GROUPS, CHUNK, 1), lambda i: (0, 0, 0))],
        out_specs=pl.BlockSpec((CHUNK, wd), lambda i: (i, 0)),
        out_shape=_sds((t, wd), BF16), compiler_params=_cp(("parallel",)),
    )(p, p, w, b)


def sgu_bwd(p, w, b, dout, *, name):
    t = p.shape[0]
    wd = SGU_WIDTH

    def body(u_ref, v_ref, w_ref, b_ref, do_ref, duv_ref, dw_ref, db_ref):
        sl = [slice(g * LANE, (g + 1) * LANE) for g in range(SGU_GROUPS)]
        _, vjp = jax.vjp(_sgu_fn, [u_ref[:, s] for s in sl], [v_ref[:, s] for s in sl],
                         [w_ref[g] for g in range(SGU_GROUPS)], [b_ref[g] for g in range(SGU_GROUPS)])
        dus, dvs, dws, dbs = vjp(tuple(do_ref[:, s] for s in sl))

        @pl.when(pl.program_id(0) == 0)
        def _():
            dw_ref[...] = jnp.zeros_like(dw_ref)
            db_ref[...] = jnp.zeros_like(db_ref)

        for g, s in enumerate(sl):
            duv_ref[:, s] = dus[g].astype(duv_ref.dtype)
            duv_ref[:, slice(wd + g * LANE, wd + (g + 1) * LANE)] = dvs[g].astype(duv_ref.dtype)
            dw_ref[g] += dws[g]
            db_ref[g] += dbs[g]

    wspec = pl.BlockSpec((SGU_GROUPS, CHUNK, CHUNK), lambda i: (0, 0, 0))
    bspec = pl.BlockSpec((SGU_GROUPS, CHUNK, 1), lambda i: (0, 0, 0))
    return pl.pallas_call(
        body, name=name, grid=(t // CHUNK,),
        in_specs=[pl.BlockSpec((CHUNK, wd), lambda i: (i, U_BLK * LANE // wd)), pl.BlockSpec((CHUNK, wd), lambda i: (i, V_BLK * LANE // wd)),
                  wspec, bspec, pl.BlockSpec((CHUNK, wd), lambda i: (i, 1))],
        out_specs=(pl.BlockSpec((CHUNK, 2 * wd), lambda i: (i, 0)), wspec, bspec),
        out_shape=(_sds((t, 2 * wd), BF16), _sds(w.shape, F32), _sds(b.shape, F32)),
        compiler_params=_cp(("arbitrary",)),
    )(p, p, w, b, dout)


def even_cols_permute(w):
    z, xbc, dt, u, v = jnp.split(w, (1024, 2560, 2592, 3616), axis=-1)
    pad = jnp.zeros(w.shape[:-1] + (EVEN_PAD_COLS - EVEN_COLS,), w.dtype)
    return jnp.concatenate([z, u, v, xbc, dt, pad], axis=-1)


def even_cols_unpermute(w):
    z, u, v, xbc, dt = jnp.split(w[..., :EVEN_COLS], (1024, 2048, 3072, 4608), axis=-1)
    return jnp.concatenate([z, xbc, dt, u, v], axis=-1)


def _dt_cols(p):
    t = p.shape[0]
    d = p[:, DT_BLK * LANE:DT_BLK * LANE + 4 * HEADS_PER_DG].reshape(t, 4, HEADS_PER_DG).transpose(1, 0, 2)
    return jnp.pad(d, ((0, 0), (0, 0), (0, LANE - HEADS_PER_DG)))


def _heads_to_lanes(a):
    return jnp.pad(a.reshape(4, 1, HEADS_PER_DG), ((0, 0), (0, 0), (0, LANE - HEADS_PER_DG)))


def even_params(conv_w, conv_b, dt_bias, a_log, d_skip, ssd_nw, sgu_w, sgu_b):
    wb = jnp.concatenate([conv_w, conv_b[None], jnp.zeros((2, XBC_DIM), F32)], axis=0)
    return dict(wb=wb, dtb=_heads_to_lanes(dt_bias), alog=_heads_to_lanes(a_log),
                dskip=jnp.repeat(d_skip, SSD_HEAD_DIM)[None], ssd_nw=ssd_nw[None], sgu_w=sgu_w, sgu_b=sgu_b[..., None])


def even_mixer_fwd(h, w_in, w_out, li, ep, tc, tag):
    p = mm_nn(h, w_in, li, "n", name=f"{tag}_in")
    xbc = conv_fwd(p, ep["wb"], tc, name=f"{tag}_conv")
    pre = _dt_cols(p)
    dt, cs = ssd_prep_fwd(pre, ep["dtb"], ep["alog"], name=f"{tag}_prep")
    y, sprev = ssd_fwd(xbc, dt, cs, tc, name=f"{tag}_ssd")
    yssd = ssd_finish_fwd(y, xbc, p, ep["dskip"], ep["ssd_nw"], name=f"{tag}_fin")
    ysgu = sgu_fwd(p, ep["sgu_w"], ep["sgu_b"], name=f"{tag}_sgu")
    ymix = jnp.concatenate([yssd, ysgu], axis=1)
    o = mm_nn(ymix, w_out, li, "k", name=f"{tag}_out")
    return o, (p, xbc, pre, dt, cs, y, sprev, ymix)


def even_mixer_bwd(saved, do, ht, w_in, w_out, g_in, g_out, li, ep, tc, tag):
    p, xbc, pre, dt, cs, y, sprev, ymix = saved
    t = p.shape[0]
    g_out = mm_tn(ymix, do, g_out, li, "k", name=f"{tag}_out_dw")
    dymix = mm_nt(do, w_out, li, "k", name=f"{tag}_out_dx")
    dy, dxskip, dz, acc_fin = ssd_finish_bwd(y, xbc, p, ep["dskip"], ep["ssd_nw"], dymix, name=f"{tag}_fin_b")
    duv, dsgu_w, dsgu_b = sgu_bwd(p, ep["sgu_w"], ep["sgu_b"], dymix, name=f"{tag}_sgu_b")
    dxg, ddt, dcs = ssd_bwd(xbc, dt, cs, sprev, dy, tc, name=f"{tag}_ssd_b")
    dpre, acc_prep = ssd_prep_bwd(pre, ep["dtb"], ep["alog"], ddt, dcs, name=f"{tag}_prep_b")
    dxbc, dwb = conv_bwd(p, ep["wb"], dxg, dxskip, tc, name=f"{tag}_conv_b")
    ddt_cols = dpre[:, :, :HEADS_PER_DG].transpose(1, 0, 2).reshape(t, 4 * HEADS_PER_DG).astype(BF16)
    ddt_cols = jnp.pad(ddt_cols, ((0, 0), (0, EVEN_PAD_COLS - DT_BLK * LANE - 4 * HEADS_PER_DG)))
    dp = jnp.concatenate([dz, duv, dxbc, ddt_cols], axis=1)
    dh = mm_nt(dp, w_in, li, "n", name=f"{tag}_in_dx")
    g_in = mm_tn(ht, dp, g_in, li, "n", name=f"{tag}_in_dw", x_is_transposed=True)
    small = dict(
        e_conv_w=dwb[:CONV_K], e_conv_b=dwb[CONV_K],
        e_dt_bias=acc_prep[:, 0, :HEADS_PER_DG].reshape(2, 2 * HEADS_PER_DG),
        e_a_log=acc_prep[:, 1, :HEADS_PER_DG].reshape(2, 2 * HEADS_PER_DG),
        e_d_skip=acc_fin[0].reshape(-1, SSD_HEAD_DIM).sum(axis=1), e_ssd_norm_w=acc_fin[1],
        e_sgu_w=dsgu_w, e_sgu_b=dsgu_b[..., 0])
    return dh, g_in, g_out, small


ATT_HEADS = 16
ATT_KV = 4
Q_BLKS, K_BLKS = ATT_HEADS, ATT_KV


def rope_tables(tc, s):
    quarter = ATT_HEAD_DIM // 4
    pos = jnp.arange(s)
    inv = ROPE_BASE ** (-jnp.arange(quarter, dtype=F32) / quarter)
    a_row = (pos // GRID_W).astype(F32)[:, None] * inv
    a_col = (pos % GRID_W).astype(F32)[:, None] * inv
    cos = jnp.concatenate([jnp.cos(a_row)] * 2 + [jnp.cos(a_col)] * 2, axis=1)
    sin = jnp.concatenate([-jnp.sin(a_row), jnp.sin(a_row), -jnp.sin(a_col), jnp.sin(a_col)], axis=1)
    return (jnp.concatenate([jnp.ones((tc, ATT_HEAD_DIM), F32), cos], axis=0),
            jnp.concatenate([jnp.zeros((tc, ATT_HEAD_DIM), F32), sin], axis=0))


def _swap_halves(x):
    lane = lax.broadcasted_iota(jnp.int32, x.shape, 1)
    return jnp.where(lane % 64 < 32, pltpu.roll(x, 96, 1), pltpu.roll(x, 32, 1))


def rope_fwd(p, cos, sin, *, name):
    t = p.shape[0]
    tr = _pick(t, (1088, 640))
    scale = ATT_HEAD_DIM ** -0.5

    def body(p_ref, c_ref, s_ref, o_ref):
        x = p_ref[...]
        r = x * c_ref[...] + _swap_halves(x) * s_ref[...]
        o_ref[...] = (r * jnp.where(pl.program_id(1) < Q_BLKS, scale, 1.0)).astype(o_ref.dtype)

    tab = pl.BlockSpec((tr, LANE), lambda i, j: (i, 0))
    return pl.pallas_call(
        body, name=name, grid=(t // tr, Q_BLKS + K_BLKS),
        in_specs=[pl.BlockSpec((tr, LANE), lambda i, j: (i, j)), tab, tab],
        out_specs=pl.BlockSpec((tr, LANE), lambda i, j: (i, j)),
        out_shape=_sds((t, (Q_BLKS + K_BLKS) * LANE), BF16), compiler_params=_cp(("parallel", "parallel")),
    )(p, cos, sin)


def rope_bwd(dq, dk, dv, cos, sin, *, name):
    t = dq.shape[0]
    tr = _pick(t, (1088, 640))
    scale = ATT_HEAD_DIM ** -0.5

    def body(dq_ref, dk_ref, dv_ref, c_ref, s_ref, o_ref):
        j = pl.program_id(1)

        def unrot(g):
            return g * c_ref[...] + _swap_halves(g * s_ref[...])

        @pl.when(j < Q_BLKS)
        def _():
            o_ref[...] = (unrot(dq_ref[...]) * scale).astype(o_ref.dtype)

        @pl.when((j >= Q_BLKS) & (j < Q_BLKS + K_BLKS))
        def _():
            o_ref[...] = unrot(dk_ref[...]).astype(o_ref.dtype)

        @pl.when(j >= Q_BLKS + K_BLKS)
        def _():
            o_ref[...] = dv_ref[...].astype(o_ref.dtype)

    tab = pl.BlockSpec((tr, LANE), lambda i, j: (i, 0))
    return pl.pallas_call(
        body, name=name, grid=(t // tr, Q_BLKS + 2 * K_BLKS),
        in_specs=[pl.BlockSpec((tr, LANE), lambda i, j: (i, jnp.minimum(j, Q_BLKS - 1))),
                  pl.BlockSpec((None, tr, LANE), lambda i, j: (jnp.clip(j - Q_BLKS, 0, K_BLKS - 1), i, 0)),
                  pl.BlockSpec((None, tr, LANE), lambda i, j: (jnp.clip(j - Q_BLKS - K_BLKS, 0, K_BLKS - 1), i, 0)), tab, tab],
        out_specs=pl.BlockSpec((tr, LANE), lambda i, j: (i, j)),
        out_shape=_sds((t, (Q_BLKS + 2 * K_BLKS) * LANE), BF16), compiler_params=_cp(("parallel", "parallel")),
    )(dq, dk, dv, cos, sin)


def _attn_tile(q4, kp, kc, kn, vp, vc, vn, kx, vx, sinks, is_lat, has_prev, has_next):
    q = kp.shape[0]
    nq = q4.shape[0]
    row = jnp.bitwise_and(lax.broadcasted_iota(jnp.int32, (nq, q), 0), q - 1)
    col = lax.broadcasted_iota(jnp.int32, (nq, q), 1)
    m_prev = (col - row) >= (1 - has_prev) * q
    m_cur = (row - row) >= (1 - is_lat)
    m_next = (row - col) >= (1 - has_next) * q
    lane = lax.broadcasted_iota(jnp.int32, (1, LANE), 1)
    sink = jnp.concatenate([jnp.broadcast_to(jnp.sum(jnp.where(lane == 0, s, 0.0), axis=1, keepdims=True), (q, 1)) for s in sinks],
                           axis=0)
    s_p = jnp.where(m_prev, _bdot(q4, kp, ((1,), (1,))), NEG_INF)
    s_c = jnp.where(m_cur, _bdot(q4, kc, ((1,), (1,))), NEG_INF)
    s_n = jnp.where(m_next, _bdot(q4, kn, ((1,), (1,))), NEG_INF)
    s_x = _bdot(q4, kx, ((1,), (1,)))
    mx = [jnp.max(a, axis=1, keepdims=True) for a in (s_p, s_c, s_n, s_x)]
    m = lax.stop_gradient(jnp.maximum(jnp.maximum(jnp.maximum(mx[0], mx[1]), jnp.maximum(mx[2], mx[3])), sink))
    e = [jnp.exp(a - m) for a in (s_p, s_c, s_n, s_x)]
    inv = 1.0 / (sum(jnp.sum(a, axis=1, keepdims=True) for a in e) + jnp.exp(sink - m))
    return sum(_bdot(a * inv, v, ((1,), (0,))) for a, v in zip(e, (vp, vc, vn, vx)))


def _attn_specs(t, tc):
    nblk = t // CHUNK
    hw = ATT_GROUP * LANE
    kcol = lambda kv: Q_BLKS + kv
    vcol = lambda kv: Q_BLKS + K_BLKS + kv
    prev = lambda n: jnp.maximum(n - 1, 0)
    nxt = lambda n: jnp.minimum(n + 1, nblk - 1)
    blk = lambda rowf, colf: pl.BlockSpec((CHUNK, LANE), lambda kv, n: (rowf(n), colf(kv)))
    same = lambda n: n
    return [pl.BlockSpec((CHUNK, hw), lambda kv, n: (n, kv)),
            blk(prev, kcol), blk(same, kcol), blk(nxt, kcol), blk(prev, vcol), blk(same, vcol), blk(nxt, vcol),
            pl.BlockSpec((tc, LANE), lambda kv, n: (0, kcol(kv))), pl.BlockSpec((tc, LANE), lambda kv, n: (0, vcol(kv))),
            pl.BlockSpec((None, 8, LANE), lambda kv, n: (kv, 0, 0))]


def _attn_args(refs, n, nct, nblk):
    q_ref, kp, kc, kn, vp, vc, vn, kx, vx, sk = refs
    f = lambda r: r[...].astype(F32)
    q4 = _stack_heads(q_ref)
    sinks = [sk[g:g + 1, :] for g in range(ATT_GROUP)]
    flags = ((n >= nct).astype(jnp.int32), (n >= nct + 1).astype(jnp.int32), ((n >= nct) & (n + 1 < nblk)).astype(jnp.int32))
    return (q4, f(kp), f(kc), f(kn), f(vp), f(vc), f(vn), f(kx), f(vx), sinks), flags


def _stack_heads(ref):
    return jnp.concatenate([ref[:, g * LANE:(g + 1) * LANE].astype(F32) for g in range(ATT_GROUP)], axis=0)


def _unstack_heads(ref, val):
    for g in range(ATT_GROUP):
        ref[:, g * LANE:(g + 1) * LANE] = val[g * CHUNK:(g + 1) * CHUNK].astype(ref.dtype)


def attn_fwd(qk, p, sink, tc, *, name):
    t = qk.shape[0]
    nblk, nct = t // CHUNK, tc // CHUNK
    hw = ATT_GROUP * LANE

    def body(*refs):
        o_ref = refs[-1]
        args, flags = _attn_args(refs[:-1], pl.program_id(1), nct, nblk)
        _unstack_heads(o_ref, _attn_tile(*args, *flags))

    return pl.pallas_call(
        body, name=name, grid=(ATT_KV, nblk), in_specs=_attn_specs(t, tc),
        out_specs=pl.BlockSpec((CHUNK, hw), lambda kv, n: (n, kv)),
        out_shape=_sds((t, ATT_HEADS * LANE), BF16), compiler_params=_cp(("parallel", "parallel")),
    )(qk, qk, qk, qk, p, p, p, qk, p, sink)


def attn_bwd(qk, p, sink, do, tc, *, name):
    t = qk.shape[0]
    nblk, nct = t // CHUNK, tc // CHUNK
    hw = ATT_GROUP * LANE

    def body(*refs):
        do_ref, dq_ref, dk_ref, dv_ref, dsk_ref = refs[-5:]
        n = pl.program_id(1)
        args, flags = _attn_args(refs[:-5], n, nct, nblk)
        _, vjp = jax.vjp(lambda *a: _attn_tile(*a, *flags), *args)
        dq4, dkp, dkc, dkn, dvp, dvc, dvn, dkx, dvx, dsinks = vjp(_stack_heads(do_ref))

        @pl.when(n == 0)
        def _():
            dk_ref[...] = jnp.zeros_like(dk_ref)
            dv_ref[...] = jnp.zeros_like(dv_ref)
            dsk_ref[...] = jnp.zeros_like(dsk_ref)

        _unstack_heads(dq_ref, dq4)
        for g in range(ATT_GROUP):
            dsk_ref[g:g + 1, :] += dsinks[g]
        for blk, dkb, dvb in ((jnp.maximum(n - 1, 0), dkp, dvp), (n, dkc, dvc), (jnp.minimum(n + 1, nblk - 1), dkn, dvn)):
            rows = pl.ds(pl.multiple_of(blk * CHUNK, CHUNK), CHUNK)
            dk_ref[rows, :] += dkb
            dv_ref[rows, :] += dvb
        dk_ref[0:tc, :] += dkx
        dv_ref[0:tc, :] += dvx

    kvacc = pl.BlockSpec((None, t, LANE), lambda kv, n: (kv, 0, 0))
    return pl.pallas_call(
        body, name=name, grid=(ATT_KV, nblk),
        in_specs=_attn_specs(t, tc) + [pl.BlockSpec((CHUNK, hw), lambda kv, n: (n, kv))],
        out_specs=(pl.BlockSpec((CHUNK, hw), lambda kv, n: (n, kv)), kvacc, kvacc,
                   pl.BlockSpec((None, 8, LANE), lambda kv, n: (kv, 0, 0))),
        out_shape=(_sds((t, ATT_HEADS * LANE), F32), _sds((ATT_KV, t, LANE), F32), _sds((ATT_KV, t, LANE), F32),
                   _sds((ATT_KV, 8, LANE), F32)),
        compiler_params=_cp(("parallel", "arbitrary")),
    )(qk, qk, qk, qk, p, p, p, qk, p, sink, do)


def sink_rows(sink):
    s = jnp.broadcast_to(sink.reshape(ATT_KV, ATT_GROUP, 1), (ATT_KV, ATT_GROUP, LANE))
    return jnp.pad(s, ((0, 0), (0, 8 - ATT_GROUP), (0, 0)))


def odd_mixer_fwd(h, w_qkv, w_out, li, sink, cos, sin, tc, tag):
    p = mm_nn(h, w_qkv, li, "n", name=f"{tag}_qkv")
    qk = rope_fwd(p, cos, sin, name=f"{tag}_rope")
    att = attn_fwd(qk, p, sink, tc, name=f"{tag}_att")
    o = mm_nn(att, w_out, li, "k", name=f"{tag}_out")
    return o, (p, qk, att)


def odd_mixer_bwd(saved, do, ht, w_qkv, w_out, g_qkv, g_out, li, sink, cos, sin, tc, tag):
    p, qk, att = saved
    g_out = mm_tn(att, do, g_out, li, "k", name=f"{tag}_out_dw")
    datt = mm_nt(do, w_out, li, "k", name=f"{tag}_out_dx")
    dq, dk, dv, dsink = attn_bwd(qk, p, sink, datt, tc, name=f"{tag}_att_b")
    dp = rope_bwd(dq, dk, dv, cos, sin, name=f"{tag}_rope_b")
    dh = mm_nt(dp, w_qkv, li, "n", name=f"{tag}_qkv_dx")
    g_qkv = mm_tn(ht, dp, g_qkv, li, "n", name=f"{tag}_qkv_dw", x_is_transposed=True)
    return dh, g_qkv, g_out, dict(o_sink=dsink[:, :ATT_GROUP, 0].reshape(-1))


ANY = pl.BlockSpec(memory_space=pl.ANY)


def _place():
    return lax.axis_index("x"), lax.axis_index("y"), lax.axis_index("c")


DMA_PIECES = 16


def _pieces(shape):
    if len(shape) < 2:
        return [()]
    lead, k = shape[:-2], shape[-2]
    split = 1
    while math.prod(lead) * split < DMA_PIECES and k % (2 * split) == 0 and (k // (2 * split)) % 16 == 0:
        split *= 2
    rows = k // split
    out = []
    for li in itertools.product(*[range(n) for n in lead]):
        out += [li + (pl.ds(q * rows, rows),) for q in range(split)]
    return out


def _start_pieces(make, src, dst):
    for idx in _pieces(src.shape):
        make(src.at[idx] if idx else src, dst.at[idx] if idx else dst).start()


def allgather8(blk, *, name):
    def body(x_ref, out_ref, send_sems, recv_sems, local_sem):
        x, y, c = _place()
        me, sibling = (x, y, c), (x, y, 1 - c)
        chips = [(1 - x, y), (x, 1 - y), (1 - x, 1 - y)]

        def slot(px, py, pc):
            return out_ref.at[4 * px + 2 * py + pc]

        def remote(k, to):
            return lambda src, dst: pltpu.make_async_remote_copy(
                src_ref=src, dst_ref=dst, send_sem=send_sems.at[k], recv_sem=recv_sems.at[k], device_id=to, device_id_type=MESH_ID)

        def local(src, dst):
            return pltpu.make_async_copy(src, dst, local_sem)

        _start_pieces(local, x_ref, slot(*me))
        _start_pieces(remote(0, sibling), x_ref, slot(*me))
        for j, chip in enumerate(chips):
            remote(1 + j, (*chip, c))(x_ref, slot(*me)).start()
        for j, chip in enumerate(chips):
            blk = slot(*chip, c)
            remote(1 + j, me)(blk, blk).wait_recv()
            _start_pieces(remote(4 + j, sibling), blk, blk)
        remote(0, me)(slot(*sibling), slot(*sibling)).wait_recv()
        for j, chip in enumerate(chips):
            blk = slot(*chip, 1 - c)
            remote(4 + j, me)(blk, blk).wait_recv()
        remote(0, sibling)(x_ref, slot(*me)).wait_send()
        for j, chip in enumerate(chips):
            remote(1 + j, (*chip, c))(x_ref, slot(*me)).wait_send()
            remote(4 + j, sibling)(slot(*chip, c), slot(*chip, c)).wait_send()
        local(x_ref, slot(*me)).wait()

    return pl.pallas_call(
        body, name=name, out_shape=_sds((N_DEV,) + blk.shape, blk.dtype), in_specs=[ANY], out_specs=ANY,
        scratch_shapes=[pltpu.SemaphoreType.DMA((7,)), pltpu.SemaphoreType.DMA((7,)), pltpu.SemaphoreType.DMA],
        compiler_params=pltpu.CompilerParams(has_side_effects=True),
    )(blk)


def _flip(r, xi, yi):
    return (1 - xi if r & 2 else xi), (1 - yi if r & 1 else yi)


def _to_sibling(send_sem, recv_sem):
    x, y, c = _place()
    return lambda src, dst: pltpu.make_async_remote_copy(src_ref=src, dst_ref=dst, send_sem=send_sem, recv_sem=recv_sem,
                                                         device_id=(x, y, 1 - c), device_id_type=MESH_ID)


def rs_sibling(gs, *, name):
    n = len(gs)

    def body(*refs):
        g_refs, out_refs, (send_sems, recv_sems) = refs[:n], refs[n:2 * n], refs[2 * n:]
        c = lax.axis_index("c")
        copies = [(_to_sibling(send_sems.at[i], recv_sems.at[i]), g_ref.at[:, pl.ds(1 - c, 1)], out_ref)
                  for i, (g_ref, out_ref) in enumerate(zip(g_refs, out_refs))]
        for remote, src, dst in copies:
            _start_pieces(remote, src, dst)
        for remote, src, dst in copies:
            remote(src, dst).wait()

    return pl.pallas_call(
        body, name=name, out_shape=[_sds((g.shape[0], 1) + g.shape[2:], g.dtype) for g in gs],
        in_specs=[ANY] * n, out_specs=[ANY] * n, scratch_shapes=[pltpu.SemaphoreType.DMA((n,)), pltpu.SemaphoreType.DMA((n,))],
        compiler_params=pltpu.CompilerParams(has_side_effects=True),
    )(*gs)


def sibling_swap(halves, *, name):
    n = len(halves)

    def body(*refs):
        h_refs, out_refs, (send_sems, recv_sems) = refs[:n], refs[n:2 * n], refs[2 * n:]
        copies = [(_to_sibling(send_sems.at[i], recv_sems.at[i]), h_ref, out_ref)
                  for i, (h_ref, out_ref) in enumerate(zip(h_refs, out_refs))]
        for remote, src, dst in copies:
            _start_pieces(remote, src, dst)
        for remote, src, dst in copies:
            remote(src, dst).wait()

    return pl.pallas_call(
        body, name=name, out_shape=[_sds(h.shape, h.dtype) for h in halves], in_specs=[ANY] * n, out_specs=[ANY] * n,
        scratch_shapes=[pltpu.SemaphoreType.DMA((n,)), pltpu.SemaphoreType.DMA((n,))],
        compiler_params=pltpu.CompilerParams(has_side_effects=True),
    )(*halves)


HBM_SPEC = pl.BlockSpec(memory_space=pltpu.HBM)
SEM_SPEC = pl.BlockSpec(memory_space=pltpu.SEMAPHORE)
DATAFLOW = pltpu.SideEffectType.DATAFLOW_SIDE_EFFECTING


def _hbm(a):
    return pltpu.with_memory_space_constraint(a, pltpu.HBM)


def _split_start(srcs, land_shapes, starts, *, name):
    n = len(srcs)

    def body(*refs):
        src_refs, land_refs, (send_sem, recv_sem), token = refs[:n], refs[n:2 * n], refs[2 * n:2 * n + 2], refs[-1]
        starts(src_refs, land_refs, send_sem, recv_sem)
        token[...] = jnp.zeros_like(token)

    out = pl.pallas_call(
        body, name=name,
        out_shape=[pltpu.SemaphoreType.DMA(()), pltpu.SemaphoreType.DMA(())] + [pltpu.HBM(s.shape, s.dtype) for s in srcs]
        + [pltpu.HBM(shape, s.dtype) for shape, s in zip(land_shapes, srcs)] + [_sds((8, LANE), F32)],
        in_specs=[HBM_SPEC] * (2 * n), out_specs=[SEM_SPEC, SEM_SPEC] + [HBM_SPEC] * (2 * n) + [pl.BlockSpec(memory_space=pltpu.VMEM)],
        input_output_aliases={i: 2 + i for i in range(2 * n)},
        compiler_params=pltpu.CompilerParams(has_side_effects=DATAFLOW),
    )(*[_hbm(s) for s in srcs], *[_hbm(lax.empty(shape, s.dtype)) for shape, s in zip(land_shapes, srcs)])
    return out[0], out[1], out[2:2 + n], out[2 + n:2 + 2 * n], out[-1]


def _split_wait(handle, after, sent, landed, *, name):
    send_sem, recv_sem, srcs, lands, _ = handle
    n = len(srcs)

    def body(*refs):
        src_refs, land_refs, (send_sem, recv_sem) = refs[:n], refs[n:2 * n], refs[2 * n:2 * n + 2]
        x, y, c = _place()
        for sized, wait in ((sent, "wait_send"), (landed, "wait_recv")):
            for src_ref, land_ref in zip(src_refs, land_refs):
                ref = sized(src_ref, land_ref)
                getattr(pltpu.make_async_remote_copy(src_ref=ref, dst_ref=ref, send_sem=send_sem, recv_sem=recv_sem,
                                                     device_id=(x, y, c), device_id_type=MESH_ID), wait)()

    out = pl.pallas_call(
        body, name=name, out_shape=[pltpu.HBM(a.shape, a.dtype) for a in (*srcs, *lands)],
        in_specs=[HBM_SPEC] * (2 * n) + [SEM_SPEC, SEM_SPEC, ANY], out_specs=[HBM_SPEC] * (2 * n),
        input_output_aliases={i: i for i in range(2 * n)},
        compiler_params=pltpu.CompilerParams(has_side_effects=DATAFLOW),
    )(*srcs, *lands, send_sem, recv_sem, after)
    return out[:n], out[n:]


def ag_send_start(blks, *, name):
    def starts(src_refs, land_refs, send_sem, recv_sem):
        x, y, c = _place()
        me = 4 * x + 2 * y + c
        for to in ((x, y, 1 - c), (1 - x, y, c), (x, 1 - y, c), (1 - x, 1 - y, c)):
            for src_ref, land_ref in zip(src_refs, land_refs):
                pltpu.make_async_remote_copy(src_ref=src_ref, dst_ref=land_ref.at[me], send_sem=send_sem, recv_sem=recv_sem,
                                             device_id=to, device_id_type=MESH_ID).start()

    return _split_start(blks, [(N_DEV,) + b.shape for b in blks], starts, name=name)


def ag_send_wait(handle, after, *, name):
    four = lambda src_ref, land_ref: land_ref.at[pl.ds(0, 4)]
    return _split_wait(handle, after, four, four, name=name)[1]


def ag_forward(lands, sibling_blks, *, name):
    n = len(lands)

    def body(*refs):
        land_refs, blk_refs, out_refs, (send_sems, recv_sems) = refs[:n], refs[n:2 * n], refs[2 * n:3 * n], refs[3 * n:]
        x, y, c = _place()
        for i, (land_ref, blk_ref, out_ref) in enumerate(zip(land_refs, blk_refs, out_refs)):
            remote = _to_sibling(send_sems.at[i], recv_sems.at[i])
            for r in (1, 2, 3):
                px, py = _flip(r, x, y)
                slot = 4 * px + 2 * py + c
                _start_pieces(remote, land_ref.at[slot], out_ref.at[slot])
            _start_pieces(remote, blk_ref, out_ref.at[4 * x + 2 * y + 1 - c])
        for i, out_ref in enumerate(out_refs):
            four = out_ref.at[pl.ds(0, 4)]
            _to_sibling(send_sems.at[i], recv_sems.at[i])(four, four).wait()

    return pl.pallas_call(
        body, name=name, out_shape=[_sds(a.shape, a.dtype) for a in lands], in_specs=[ANY] * (2 * n), out_specs=[ANY] * n,
        scratch_shapes=[pltpu.SemaphoreType.DMA((n,)), pltpu.SemaphoreType.DMA((n,))], input_output_aliases={i: i for i in range(n)},
        compiler_params=pltpu.CompilerParams(has_side_effects=True),
    )(*lands, *sibling_blks)


def rs_chips_start(hs, *, name):
    def starts(src_refs, land_refs, send_sem, recv_sem):
        x, y, c = _place()
        for r in (1, 2, 3):
            px, py = _flip(r, x, y)
            for src_ref, land_ref in zip(src_refs, land_refs):
                pltpu.make_async_remote_copy(src_ref=src_ref.at[2 * px + py], dst_ref=land_ref.at[r - 1], send_sem=send_sem,
                                             recv_sem=recv_sem, device_id=(px, py, c), device_id_type=MESH_ID).start()

    return _split_start(hs, [(3,) + h.shape[1:] for h in hs], starts, name=name)


def rs_chips_wait(handle, after, *, name):
    return _split_wait(handle, after, lambda src_ref, land_ref: src_ref.at[pl.ds(0, 3)], lambda src_ref, land_ref: land_ref, name=name)


def _row_block(kd, nd):
    return _pick(kd, (max(32, (1 << 19) // nd // 32 * 32),))


def add_kept_half(g, recv, core, *, name):
    nchip, nl, kd, nd = g.shape
    lh = nl // 2
    tk = _row_block(kd, nd)

    def body(c_ref, g_ref, r_ref, o_ref):
        del c_ref
        o_ref[...] = (g_ref[...].astype(F32) + r_ref[...].astype(F32)).astype(o_ref.dtype)

    blk = lambda f: pl.BlockSpec((None, None, tk, nd), f)
    return pl.pallas_call(
        body, name=name, out_shape=_sds((nchip, lh, kd, nd), BF16),
        grid_spec=pltpu.PrefetchScalarGridSpec(
            num_scalar_prefetch=1, grid=(nchip, lh, kd // tk),
            in_specs=[blk(lambda j, l, i, c_ref: (j, c_ref[0] * lh + l, i, 0)), blk(lambda j, l, i, c_ref: (j, l, i, 0))],
            out_specs=blk(lambda j, l, i, c_ref: (j, l, i, 0))),
        compiler_params=_cp(("parallel", "parallel", "parallel")),
    )(core, g, recv)


def add_chip_parts(h, parts, chip, core, grad, layer, *, name):
    _, _, kh, nd = h.shape
    tk = _row_block(kh, nd)
    nkb = kh // tk

    def body(chip_ref, core_ref, h_ref, p0, p1, p2, g_in, o_ref):
        del chip_ref, core_ref, g_in
        o_ref[...] = h_ref[...].astype(F32) + p0[...].astype(F32) + p1[...].astype(F32) + p2[...].astype(F32)

    blk = lambda f: pl.BlockSpec((None, None, tk, nd), f)
    part = lambda r: blk(functools.partial(lambda r_, i, chip_ref, core_ref: (r_, 0, i, 0), r))
    return pl.pallas_call(
        body, name=name, out_shape=_sds(grad.shape, grad.dtype),
        grid_spec=pltpu.PrefetchScalarGridSpec(
            num_scalar_prefetch=2, grid=(nkb,),
            in_specs=[blk(lambda i, chip_ref, core_ref: (chip_ref[0], 0, i, 0)), part(0), part(1), part(2), ANY],
            out_specs=pl.BlockSpec((None, tk, nd), lambda i, chip_ref, core_ref: (layer, core_ref[0] * nkb + i, 0))),
        input_output_aliases={6: 0},
        compiler_params=_cp(("parallel",)),
    )(chip, core, h, parts, parts, parts, grad)


def sibling_fill(grads, layer, *, name):
    n = len(grads)

    def body(*refs):
        g_refs, out_refs, (send_sems, recv_sems) = refs[:n], refs[n:2 * n], refs[2 * n:]
        c = lax.axis_index("c")
        copies = []
        for i, (g_ref, out_ref) in enumerate(zip(g_refs, out_refs)):
            kh = g_ref.shape[1] // 2
            copies.append((_to_sibling(send_sems.at[i], recv_sems.at[i]), g_ref.at[layer, pl.ds(c * kh, kh)],
                           out_ref.at[layer, pl.ds(c * kh, kh)]))
        for remote, src, dst in copies:
            _start_pieces(remote, src, dst)
        for remote, src, dst in copies:
            remote(src, dst).wait()

    return pl.pallas_call(
        body, name=name, out_shape=[_sds(g.shape, g.dtype) for g in grads], in_specs=[ANY] * n, out_specs=[ANY] * n,
        scratch_shapes=[pltpu.SemaphoreType.DMA((n,)), pltpu.SemaphoreType.DMA((n,))],
        input_output_aliases={i: i for i in range(n)},
        compiler_params=pltpu.CompilerParams(has_side_effects=True),
    )(*grads)


def sum_slots(a, out_dtype, *, name):
    n = a.shape[0]
    cols = a.shape[-1]
    a3 = a.reshape(n, -1, cols)
    rows = a3.shape[1]
    tr = _pick(rows, (max(32, (1 << 19) // cols // 32 * 32),))

    def body(*refs):
        acc = refs[0][...].astype(F32)
        for r in refs[1:n]:
            acc = acc + r[...].astype(F32)
        refs[n][...] = acc.astype(out_dtype)

    return pl.pallas_call(
        body, name=name, grid=(rows // tr,),
        in_specs=[pl.BlockSpec((None, tr, cols), functools.partial(lambda j, i: (j, i, 0), j)) for j in range(n)],
        out_specs=pl.BlockSpec((tr, cols), lambda i: (i, 0)),
        out_shape=_sds((rows, cols), out_dtype), compiler_params=_cp(("parallel",)),
    )(*([a3] * n)).reshape(a.shape[1:])


def unit_blocks(shards, ci):
    return [lax.dynamic_index_in_dim(w.reshape(2, w.shape[0] // 2, w.shape[1]), ci, axis=0, keepdims=False).astype(BF16)
            for w in shards]


def gather_finish(lands, sibling_blks, tag):
    full = ag_forward(lands, sibling_blks, name=f"{tag}_fwd")
    return [a.reshape(N_CHIP, 1, 2 * a.shape[1], a.shape[2]) for a in full]


def reduce_scatter_start(gs, tag):
    core = jnp.reshape(lax.axis_index("c"), (1,)).astype(jnp.int32)
    halves = [g.reshape(N_CHIP, 2, g.shape[1] // 2, g.shape[2]) for g in gs]
    recv = rs_sibling(halves, name=f"{tag}_rs1")
    chip_sums = [add_kept_half(h, r, core, name=f"{tag}_add1_{j}") for j, (h, r) in enumerate(zip(halves, recv))]
    return (rs_chips_start(chip_sums, name=f"{tag}_rs2_start"),)


def reduce_scatter_finish(pending, after, grads, layer, tag):
    _, handle = pending
    xi, yi, ci = _place()
    chip = jnp.reshape(2 * xi + yi, (1,)).astype(jnp.int32)
    core = jnp.reshape(ci, (1,)).astype(jnp.int32)
    chip_sums, parts = rs_chips_wait(handle, after, name=f"{tag}_rs2_wait")
    grads = [add_chip_parts(h, p, chip, core, g, layer, name=f"{tag}_add2_{j}")
             for j, (h, p, g) in enumerate(zip(chip_sums, parts, grads))]
    return sibling_fill(grads, layer, name=f"{tag}_rs3")


def mod_fwd(c16, w_mod, *, name):
    nl, d, ns = w_mod.shape
    tn = _pick(ns, (512,))

    def body(c_ref, w_ref, o_ref):
        o_ref[...] = jnp.dot(jax.nn.silu(c_ref[...]), w_ref[...], precision=HI, preferred_element_type=F32)

    return pl.pallas_call(
        body, name=name, grid=(nl, ns // tn),
        in_specs=[pl.BlockSpec((16, d), lambda l, j: (0, 0)), pl.BlockSpec((None, d, tn), lambda l, j: (l, 0, j))],
        out_specs=pl.BlockSpec((None, 16, tn), lambda l, j: (l, 0, j)),
        out_shape=_sds((nl, 16, ns), F32), compiler_params=_cp(("parallel", "parallel")),
    )(c16, w_mod)


def mod_bwd_w(c16, dm, *, name):
    nl, _, ns = dm.shape
    d = c16.shape[1]
    tn = _pick(ns, (512,))

    def body(c_ref, dm_ref, o_ref):
        o_ref[...] = lax.dot_general(jax.nn.silu(c_ref[...]), dm_ref[...], (((0,), (0,)), ((), ())), precision=HI,
                                     preferred_element_type=F32)

    return pl.pallas_call(
        body, name=name, grid=(nl, ns // tn),
        in_specs=[pl.BlockSpec((16, d), lambda l, j: (0, 0)), pl.BlockSpec((None, 16, tn), lambda l, j: (l, 0, j))],
        out_specs=pl.BlockSpec((None, d, tn), lambda l, j: (l, 0, j)),
        out_shape=_sds((nl, d, ns), F32), compiler_params=_cp(("parallel", "parallel")),
    )(c16, dm)


def mod_bwd_s(dm, w_mod, *, name):
    nl, d, ns = w_mod.shape
    td = _pick(d, (512,))

    def body(dm_ref, w_ref, o_ref):
        part = lax.dot_general(dm_ref[...], w_ref[...], (((1,), (1,)), ((), ())), precision=HI, preferred_element_type=F32)
        rowsum = jnp.sum(part[8:16], axis=0, keepdims=True)

        @pl.when(pl.program_id(1) == 0)
        def _():
            o_ref[...] = jnp.zeros_like(o_ref)

        o_ref[...] += jnp.broadcast_to(rowsum, o_ref.shape)

    return pl.pallas_call(
        body, name=name, grid=(d // td, nl),
        in_specs=[pl.BlockSpec((None, 16, ns), lambda i, l: (l, 0, 0)), pl.BlockSpec((None, td, ns), lambda i, l: (l, i, 0))],
        out_specs=pl.BlockSpec((8, td), lambda i, l: (0, i)),
        out_shape=_sds((8, d), F32), compiler_params=_cp(("parallel", "arbitrary")),
    )(dm, w_mod)


def colsum16(dm, *, name):
    nl, _, n = dm.shape
    tn = _pick(n, (2048,))

    def body(dm_ref, o_ref):
        o_ref[...] = jnp.broadcast_to(jnp.sum(dm_ref[...], axis=0, keepdims=True), o_ref.shape)

    return pl.pallas_call(
        body, name=name, grid=(nl, n // tn),
        in_specs=[pl.BlockSpec((None, 16, tn), lambda l, j: (l, 0, j))],
        out_specs=pl.BlockSpec((None, 8, tn), lambda l, j: (l, 0, j)),
        out_shape=_sds((nl, 8, n), F32), compiler_params=_cp(("parallel", "parallel")),
    )(dm)


def silu_grad_mul(g, c, *, name):
    def body(g_ref, c_ref, o_ref):
        _, vjp = jax.vjp(jax.nn.silu, c_ref[...])
        o_ref[...] = vjp(g_ref[...])[0]

    return pl.pallas_call(body, name=name, out_shape=_sds(g.shape, F32))(g, c)


def adamw(w, g, m, v, *, name):
    shape = w.shape
    cols = shape[-1] if len(shape) > 1 else LANE
    flat = [a.reshape(-1, cols) for a in (w, g, m, v)]
    rows = flat[0].shape[0]
    tr = _pick(rows, (max(8, (1 << 18) // cols // 8 * 8),)) if rows % 8 == 0 else rows
    c1 = 1.0 - ADAM_B1 ** ADAM_STEP
    c2 = 1.0 - ADAM_B2 ** ADAM_STEP

    def body(w_ref, g_ref, m_ref, v_ref, d_ref, nm_ref, nv_ref):
        gv = g_ref[...]
        nm = ADAM_B1 * m_ref[...] + (1.0 - ADAM_B1) * gv
        nv = ADAM_B2 * v_ref[...] + (1.0 - ADAM_B2) * (gv * gv)
        d_ref[...] = -ADAM_LR * ((nm / c1) / (jnp.sqrt(nv / c2) + ADAM_EPS) + ADAM_WD * w_ref[...])
        nm_ref[...] = nm
        nv_ref[...] = nv

    blk = pl.BlockSpec((tr, cols), lambda i: (i, 0))
    outs = pl.pallas_call(
        body, name=name, grid=(rows // tr,), in_specs=[blk] * 4, out_specs=(blk,) * 3,
        out_shape=(_sds((rows, cols), F32),) * 3, compiler_params=_cp(("parallel",)),
    )(*flat)
    return tuple(o.reshape(shape) for o in outs)


PACK_ELEMS = LANE * LANE


def _pack(arrs):
    flat = jnp.concatenate([a.reshape(-1).astype(F32) for a in arrs])
    return jnp.pad(flat, (0, (-flat.shape[0]) % PACK_ELEMS)).reshape(-1, LANE)


def _unpack(packed, shapes):
    flat = packed.reshape(-1)
    out, pos = [], 0
    for s in shapes:
        n = math.prod(s)
        out.append(flat[pos:pos + n].reshape(s))
        pos += n
    return out


def _chip_cols(a, chip, width):
    return lax.dynamic_slice_in_dim(a, chip * width, width, axis=a.ndim - 1)


def kernel(x, c, ctx, c_ctx, w_mod, b_mod, norm_w, w_ffn_in, w_ffn_out, e_w_in, e_conv_w, e_conv_b, e_dt_bias, e_a_log, e_d_skip, e_ssd_norm_w, e_sgu_w, e_sgu_b, e_w_out, o_w_qkv, o_sink, o_w_out, loss_target, m_c_ctx, m_w_mod, m_b_mod, m_norm_w, m_w_ffn_in, m_w_ffn_out, m_e_w_in, m_e_conv_w, m_e_conv_b, m_e_dt_bias, m_e_a_log, m_e_d_skip, m_e_ssd_norm_w, m_e_sgu_w, m_e_sgu_b, m_e_w_out, m_o_w_qkv, m_o_sink, m_o_w_out, v_c_ctx, v_w_mod, v_b_mod, v_norm_w, v_w_ffn_in, v_w_ffn_out, v_e_w_in, v_e_conv_w, v_e_conv_b, v_e_dt_bias, v_e_a_log, v_e_d_skip, v_e_ssd_norm_w, v_e_sgu_w, v_e_sgu_b, v_e_w_out, v_o_w_qkv, v_o_sink, v_o_w_out):
    xi, yi, ci = _place()
    chip = 2 * xi + yi
    me = 2 * chip + ci
    s, d = x.shape[1:]
    tc = ctx.shape[1]
    depth = w_mod.shape[0]
    n_even = e_w_in.shape[0]
    dq = norm_w.shape[-1]
    cq = e_conv_w.shape[-1]
    ns = w_mod.shape[-1]

    gath = allgather8(_pack([c, norm_w, e_conv_w]), name="ag_small").reshape(N_DEV, -1)
    c_all = gath[:, :d]
    per_chip = [_unpack(gath[2 * k, d:], [norm_w.shape, e_conv_w.shape]) for k in range(N_CHIP)]
    nw_full = jnp.concatenate([pc[0] for pc in per_chip], axis=-1)
    convw_full = jnp.concatenate([pc[1] for pc in per_chip], axis=-1)
    c16 = jnp.concatenate([c_all, jnp.broadcast_to(c_ctx[None], (8, d))], axis=0)

    mod_g = allgather8(mod_fwd(c16, w_mod, name="mod_fwd"), name="ag_mod")
    mod_all = jnp.concatenate([mod_g[2 * k] for k in range(N_CHIP)], axis=-1) + b_mod[:, None, :]
    mod_rows = jnp.stack([mod_all[:, 8], lax.dynamic_index_in_dim(mod_all, me, axis=1, keepdims=False)], axis=1)
    modtab = jnp.pad(mod_rows.reshape(depth, 2, 6, d), ((0, 0), (0, 0), (0, 2), (0, 0)))

    eps_ = [even_params(convw_full[i], e_conv_b[i], e_dt_bias[i], e_a_log[i], e_d_skip[i], e_ssd_norm_w[i], e_sgu_w[i], e_sgu_b[i])
            for i in range(n_even)]
    sinks = [sink_rows(o_sink[i]) for i in range(o_sink.shape[0])]
    cos, sin = rope_tables(tc, s)
    units = [(kind, l) for l in range(depth) for kind in ("mix", "ffn")]

    def unit_shards(kind, l):
        if kind == "ffn":
            return [w_ffn_in[l], w_ffn_out[l]]
        return [e_w_in[l // 2], e_w_out[l // 2]] if l % 2 == 0 else [o_w_qkv[l // 2], o_w_out[l // 2]]

    def unit_weights(kind, l, gathered):
        w_a, w_b = gathered
        if kind == "mix" and l % 2 == 0:
            w_a = even_cols_permute(jnp.moveaxis(w_a[:, 0], 0, 1).reshape(1, d, -1))[None]
        return w_a, w_b

    def unit_fwd(kind, l, u_in, mt, wts):
        nw = nw_full[l]
        w_a, w_b = wts
        if kind == "mix":
            h1, h1t = norm_mod_fwd(u_in, nw[0], mt, tc, 0, name=f"L{l}_norm1")
            if l % 2 == 0:
                o, ms = even_mixer_fwd(h1, w_a, w_b, 0, eps_[l // 2], tc, f"L{l}_mix")
            else:
                o, ms = odd_mixer_fwd(h1, w_a, w_b, 0, sinks[l // 2], cos, sin, tc, f"L{l}_mix")
            return resid_fwd(u_in, o, nw[1], mt, tc, 0, name=f"L{l}_res1"), (u_in, h1t, ms, o)
        h2, h2t = norm_mod_fwd(u_in, nw[2], mt, tc, 1, name=f"L{l}_norm2")
        p = mm_nn(h2, w_a, 0, "n", name=f"L{l}_ffn_in", out_dtype=BF16)
        a, at = swiglu_fwd(p, name=f"L{l}_swiglu")
        f = mm_nn(a, w_b, 0, "k", name=f"L{l}_ffn_out")
        return resid_fwd(u_in, f, nw[3], mt, tc, 1, name=f"L{l}_res2"), (u_in, h2t, p, at, f)

    def unit_bwd(kind, l, du_out, mt, wts, sv):
        nw = nw_full[l]
        w_a, w_b = wts
        if kind == "ffn":
            u1, h2t, p, at, f = sv
            df, acc_r = resid_bwd(f, nw[3], mt, du_out, tc, 1, name=f"L{l}_res2_b")
            g_b = mm_tn(at, df, w_b, 0, "k", name=f"L{l}_ffn_out_dw", x_is_transposed=True)
            da = mm_nt(df, w_b, 0, "k", name=f"L{l}_ffn_out_dx", out_dtype=BF16)
            dp = swiglu_bwd(p, da, name=f"L{l}_swiglu_b")
            dh2 = mm_nt(dp, w_a, 0, "n", name=f"L{l}_ffn_in_dx")
            g_a = mm_tn(h2t, dp, w_a, 0, "n", name=f"L{l}_ffn_in_dw", x_is_transposed=True)
            du_in, acc_n = norm_mod_bwd(u1, nw[2], mt, dh2, du_out, tc, 1, name=f"L{l}_norm2_b")
            return du_in, [g_a[:, 0], g_b[:, 0]], (acc_n, acc_r), None
        u0, h1t, ms, o = sv
        do, acc_r = resid_bwd(o, nw[1], mt, du_out, tc, 0, name=f"L{l}_res1_b")
        if l % 2 == 0:
            dh1, g_a, g_b, small = even_mixer_bwd(ms, do, h1t, w_a, w_b, w_a, w_b, 0, eps_[l // 2], tc, f"L{l}_mix")
            g_a = jnp.moveaxis(even_cols_unpermute(g_a[0, 0]).reshape(d, N_CHIP, -1), 1, 0)
        else:
            dh1, g_a, g_b, small = odd_mixer_bwd(ms, do, h1t, w_a, w_b, w_a, w_b, 0, sinks[l // 2], cos, sin, tc,
                                                 f"L{l}_mix")
            g_a = g_a[:, 0]
        du_in, acc_n = norm_mod_bwd(u0, nw[0], mt, dh1, du_out, tc, 0, name=f"L{l}_norm1_b")
        return du_in, [g_a, g_b[:, 0]], (acc_n, acc_r), small

    u = jnp.concatenate([ctx[0], x[0]], axis=0)
    shards = unit_shards(*units[0])
    handle = ag_send_start(unit_blocks(shards, ci), name="ag0_start")
    lands = ag_send_wait(handle, handle[4], name="ag0_wait")
    wts = [None] * len(units)
    wts[0] = unit_weights(*units[0], gather_finish(lands, unit_blocks(shards, 1 - ci), "ag0"))
    saved = [None] * len(units)
    prev = u
    for i, (kind, l) in enumerate(units):
        tok = 0.0
        if i + 1 < len(units):
            shards = unit_shards(*units[i + 1])
            blks, _ = lax.optimization_barrier((unit_blocks(shards, ci), prev))
            handle = ag_send_start(blks, name=f"ag{i + 1}_start")
            tok = handle[4][0, 0]
        prev = u
        u, saved[i] = unit_fwd(kind, l, u, modtab[l] + tok, wts[i])
        if i + 1 < len(units):
            lands = ag_send_wait(handle, u, name=f"ag{i + 1}_wait")
            wts[i + 1] = unit_weights(*units[i + 1], gather_finish(lands, unit_blocks(shards, 1 - ci), f"ag{i + 1}"))
    loss_part, du = loss_fwd_bwd(u, loss_target[0], tc, name="loss")
    loss = lax.psum(loss_part[0, 0], ("x", "y", "c"))

    accs, smalls = [None] * len(units), [None] * len(units)
    gbuf = {n: jnp.zeros(w.shape, F32) for n, w in (("w_ffn_in", w_ffn_in), ("w_ffn_out", w_ffn_out), ("e_w_in", e_w_in),
                                                     ("e_w_out", e_w_out), ("o_w_qkv", o_w_qkv), ("o_w_out", o_w_out))}

    def finish(pending, after):
        kind, l = units[pending[0]]
        names, layer = (("w_ffn_in", "w_ffn_out"), l) if kind == "ffn" else (
            (("e_w_in", "e_w_out"), l // 2) if l % 2 == 0 else (("o_w_qkv", "o_w_out"), l // 2))
        done = reduce_scatter_finish(pending, after, [gbuf[n] for n in names], layer, f"rs{pending[0]}")
        gbuf.update(zip(names, done))

    pending = None
    for i in reversed(range(len(units))):
        kind, l = units[i]
        tok = pending[1][4][0, 0] if pending is not None else 0.0
        du, gs, accs[i], smalls[i] = unit_bwd(kind, l, du, modtab[l] + tok, wts[i], saved[i])
        if pending is not None:
            finish(pending, du)
        pending = (i,) + reduce_scatter_start(gs, f"rs{i}")
    grad_x = du[tc:][None]
    d_nw, d_mt = [None] * depth, [None] * depth
    for l in range(depth):
        (acc0, acc1), (acc2, acc3) = accs[2 * l], accs[2 * l + 1]
        d_nw[l] = jnp.stack([acc[0, 0] + acc[1, 0] for acc in (acc0, acc1, acc2, acc3)])
        d_mt[l] = jnp.stack([acc0[:, 1], acc0[:, 2], acc1[:, 1], acc2[:, 1], acc2[:, 2], acc3[:, 1]], axis=1)
    small_e = [smalls[2 * l] for l in range(0, depth, 2)]
    small_o = [smalls[2 * l] for l in range(1, depth, 2)]

    d_mt_all = jnp.stack(d_mt) + pending[1][4][0, 0]
    dmt_g = allgather8(jnp.pad(d_mt_all, ((0, 0), (0, 0), (0, 2), (0, 0))), name="ag_dmod")[:, :, :, :6]
    dm16 = jnp.concatenate([dmt_g[:, :, 1].transpose(1, 0, 2, 3).reshape(depth, N_DEV, 6 * d),
                            dmt_g[:, :, 0].transpose(1, 0, 2, 3).reshape(depth, N_DEV, 6 * d)], axis=1)
    dm_sh = _chip_cols(dm16, chip, ns)
    grad_w_mod = mod_bwd_w(c16, dm_sh, name="mod_bwd_w")
    grad_b_mod = colsum16(dm16, name="mod_bwd_b")[:, 0]
    ds_cc = mod_bwd_s(dm_sh, w_mod, name="mod_bwd_s")[0]

    stack_e = lambda key: jnp.stack([se[key] for se in small_e])
    small_names = ["e_conv_b", "e_dt_bias", "e_a_log", "e_d_skip", "e_ssd_norm_w", "e_sgu_w", "e_sgu_b"]
    small_parts = [jnp.stack(d_nw), stack_e("e_conv_w")] + [stack_e(k) for k in small_names]
    small_parts += [jnp.stack([so["o_sink"] for so in small_o]), 0.5 * ds_cc]
    small_shapes = [a.shape for a in small_parts]
    small_sum = sum_slots(allgather8(_pack(small_parts), name="ag_small_grads"), F32, name="small_grads_sum")
    (g_nw, g_convw, g_convb, g_dtb, g_alog, g_dskip, g_ssdnw, g_sguw, g_sgub, g_sink, g_scc) = _unpack(small_sum, small_shapes)
    grad_c_ctx = silu_grad_mul(jnp.broadcast_to(g_scc[None], (8, d)), jnp.broadcast_to(c_ctx[None], (8, d)), name="c_ctx_grad")[0]
    grads = dict(
        c_ctx=grad_c_ctx, w_mod=grad_w_mod, b_mod=grad_b_mod, norm_w=_chip_cols(g_nw, chip, dq),
        e_conv_w=_chip_cols(g_convw, chip, cq), e_conv_b=g_convb, e_dt_bias=g_dtb.reshape(e_dt_bias.shape),
        e_a_log=g_alog.reshape(e_a_log.shape), e_d_skip=g_dskip, e_ssd_norm_w=g_ssdnw, e_sgu_w=g_sguw, e_sgu_b=g_sgub,
        o_sink=g_sink)

    finish(pending, grad_w_mod)
    grads.update(gbuf)

    weights = dict(c_ctx=c_ctx, w_mod=w_mod, b_mod=b_mod, norm_w=norm_w, w_ffn_in=w_ffn_in, w_ffn_out=w_ffn_out, e_w_in=e_w_in,
                   e_conv_w=e_conv_w, e_conv_b=e_conv_b, e_dt_bias=e_dt_bias, e_a_log=e_a_log, e_d_skip=e_d_skip,
                   e_ssd_norm_w=e_ssd_norm_w, e_sgu_w=e_sgu_w, e_sgu_b=e_sgu_b, e_w_out=e_w_out, o_w_qkv=o_w_qkv, o_sink=o_sink,
                   o_w_out=o_w_out)
    ms_ = dict(c_ctx=m_c_ctx, w_mod=m_w_mod, b_mod=m_b_mod, norm_w=m_norm_w, w_ffn_in=m_w_ffn_in, w_ffn_out=m_w_ffn_out,
               e_w_in=m_e_w_in, e_conv_w=m_e_conv_w, e_conv_b=m_e_conv_b, e_dt_bias=m_e_dt_bias, e_a_log=m_e_a_log,
               e_d_skip=m_e_d_skip, e_ssd_norm_w=m_e_ssd_norm_w, e_sgu_w=m_e_sgu_w, e_sgu_b=m_e_sgu_b, e_w_out=m_e_w_out,
               o_w_qkv=m_o_w_qkv, o_sink=m_o_sink, o_w_out=m_o_w_out)
    vs_ = dict(c_ctx=v_c_ctx, w_mod=v_w_mod, b_mod=v_b_mod, norm_w=v_norm_w, w_ffn_in=v_w_ffn_in, w_ffn_out=v_w_ffn_out,
               e_w_in=v_e_w_in, e_conv_w=v_e_conv_w, e_conv_b=v_e_conv_b, e_dt_bias=v_e_dt_bias, e_a_log=v_e_a_log,
               e_d_skip=v_e_d_skip, e_ssd_norm_w=v_e_ssd_norm_w, e_sgu_w=v_e_sgu_w, e_sgu_b=v_e_sgu_b, e_w_out=v_e_w_out,
               o_w_qkv=v_o_w_qkv, o_sink=v_o_sink, o_w_out=v_o_w_out)
    names = list(weights)
    big = ("w_mod", "w_ffn_in", "w_ffn_out", "e_w_in", "e_w_out", "o_w_qkv", "o_w_out")
    small = [n for n in names if n not in big]
    delta, new_m, new_v = {}, {}, {}
    for n in big:
        delta[n], new_m[n], new_v[n] = adamw(weights[n], grads[n], ms_[n], vs_[n], name=f"adamw_{n}")
    packed = adamw(*[_pack([tab[n] for n in small]) for tab in (weights, grads, ms_, vs_)], name="adamw_small")
    shapes = [weights[n].shape for n in small]
    for tab, pk in zip((delta, new_m, new_v), packed):
        for n, val in zip(small, _unpack(pk, shapes)):
            tab[n] = val
    return (loss, grad_x, *[grads[n] for n in names], *[delta[n] for n in names], *[new_m[n] for n in names],
            *[new_v[n] for n in names])
```

```python
import functools
import itertools
import math

import jax
import jax.numpy as jnp
from jax import lax
from jax.experimental import pallas as pl
from jax.experimental.pallas import tpu as pltpu

F32 = jnp.float32
BF16 = jnp.bfloat16
HI = lax.Precision.HIGHEST

NORM_EPS = 1e-6
SSD_HEAD_DIM = 64
SSD_STATE = 128
CHUNK = 128
CONV_K = 5
ATT_HEAD_DIM = 128
ATT_GROUP = 4
ROPE_BASE = 10000.0
GRID_W = 64
NEG_INF = -1e30
ADAM_LR, ADAM_B1, ADAM_B2, ADAM_EPS, ADAM_WD, ADAM_STEP = 0.001, 0.9, 0.999, 1e-08, 0.01, 10

LANE = 128
VMEM_LIMIT = 56 * 1024 * 1024
MESH_ID = pl.DeviceIdType.MESH
N_DEV = 8
N_CHIP = 4


def _cp(sem=None):
    return pltpu.CompilerParams(dimension_semantics=sem, vmem_limit_bytes=VMEM_LIMIT)


def _sds(shape, dtype):
    return jax.ShapeDtypeStruct(tuple(shape), dtype)


def _pick(n, cands):
    for c in cands:
        if n % c == 0:
            return c
    for step in (LANE, 16, 8):
        for c in range(min(n, cands[0]) // step * step, 0, -step):
            if n % c == 0:
                return c
    raise ValueError((n, cands))


def _w_index(blocked, layer, per_block_k, per_block_n):
    def idx(kblk, nblk):
        if blocked == "n":
            return (nblk // per_block_n, layer, kblk, nblk % per_block_n)
        return (kblk // per_block_k, layer, kblk % per_block_k, nblk)
    return idx


def mm_nn(a, w, layer, blocked, *, name, out_dtype=F32, tm=None, tn=None, tk=None):
    m, k_total = a.shape
    cb, _, kd, nd = w.shape
    n_total = nd * cb if blocked == "n" else nd
    assert k_total == (kd if blocked == "n" else kd * cb)
    tm = tm or _pick(m, (1088, 192))
    tn = tn or _pick(nd, (2048, 1408, 768, 512))
    tk = tk or _pick(kd, (2048, 1408, 512))
    nk = k_total // tk
    widx = _w_index(blocked, layer, kd // tk, nd // tn)

    def body(a_ref, w_ref, o_ref, acc_ref):
        kk = pl.program_id(2)
        part = jnp.dot(a_ref[...].astype(BF16), w_ref[...].astype(BF16), preferred_element_type=F32)

        @pl.when(kk == 0)
        def _():
            acc_ref[...] = part

        @pl.when(kk > 0)
        def _():
            acc_ref[...] += part

        @pl.when(kk == nk - 1)
        def _():
            o_ref[...] = acc_ref[...].astype(o_ref.dtype)

    return pl.pallas_call(
        body, name=name, grid=(m // tm, n_total // tn, nk),
        in_specs=[pl.BlockSpec((tm, tk), lambda i, j, k: (i, k)),
                  pl.BlockSpec((None, None, tk, tn), lambda i, j, k: widx(k, j))],
        out_specs=pl.BlockSpec((tm, tn), lambda i, j, k: (i, j)),
        out_shape=_sds((m, n_total), out_dtype),
        scratch_shapes=[pltpu.VMEM((tm, tn), F32)],
        compiler_params=_cp(("parallel", "parallel", "arbitrary")),
    )(a, w)


def mm_nt(dy, w, layer, blocked, *, name, out_dtype=F32, tm=None, tn=None, tk=None):
    m, n_total = dy.shape
    cb, _, kd, nd = w.shape
    k_total = kd if blocked == "n" else kd * cb
    assert n_total == (nd * cb if blocked == "n" else nd)
    tm = tm or _pick(m, (1088, 192))
    tn = tn or _pick(kd, (2048, 1408, 512))
    tk = tk or _pick(nd, (1408, 1024, 768))
    nk = n_total // tk
    widx = _w_index(blocked, layer, kd // tn, nd // tk)

    def body(a_ref, w_ref, o_ref, acc_ref):
        kk = pl.program_id(2)
        part = lax.dot_general(a_ref[...].astype(BF16), w_ref[...].astype(BF16), (((1,), (1,)), ((), ())),
                               preferred_element_type=F32)

        @pl.when(kk == 0)
        def _():
            acc_ref[...] = part

        @pl.when(kk > 0)
        def _():
            acc_ref[...] += part

        @pl.when(kk == nk - 1)
        def _():
            o_ref[...] = acc_ref[...].astype(o_ref.dtype)

    return pl.pallas_call(
        body, name=name, grid=(m // tm, k_total // tn, nk),
        in_specs=[pl.BlockSpec((tm, tk), lambda i, j, k: (i, k)),
                  pl.BlockSpec((None, None, tn, tk), lambda i, j, k: widx(j, k))],
        out_specs=pl.BlockSpec((tm, tn), lambda i, j, k: (i, j)),
        out_shape=_sds((m, k_total), out_dtype),
        scratch_shapes=[pltpu.VMEM((tm, tn), F32)],
        compiler_params=_cp(("parallel", "parallel", "arbitrary")),
    )(dy, w)


def mm_tn(x, dy, g, layer, blocked, *, name, tm=None, tn=None, tt=None, x_is_transposed=False):
    k_total, t_total = x.shape if x_is_transposed else x.shape[::-1]
    n_total = dy.shape[1]
    cb, nl, kd, nd = g.shape
    assert nl == 1 and layer == 0
    assert k_total == (kd if blocked == "n" else kd * cb) and n_total == (nd * cb if blocked == "n" else nd)
    tm = tm or _pick(kd, (1024, 1408, 512))
    tn = tn or _pick(nd, (1408, 768, 512))
    tt = tt or _pick(t_total, (2176,) if x_is_transposed else (1088, 96))
    nt = t_total // tt
    widx = _w_index(blocked, layer, kd // tm, nd // tn)
    x_spec = pl.BlockSpec((tm, tt), lambda i, j, t: (i, t)) if x_is_transposed else pl.BlockSpec((tt, tm), lambda i, j, t: (t, i))
    x_dim = 1 if x_is_transposed else 0

    def body(x_ref, dy_ref, o_ref, acc_ref):
        tstep = pl.program_id(2)
        part = lax.dot_general(x_ref[...].astype(BF16), dy_ref[...].astype(BF16), (((x_dim,), (0,)), ((), ())),
                               preferred_element_type=F32)

        @pl.when(tstep == 0)
        def _():
            acc_ref[...] = part

        @pl.when(tstep > 0)
        def _():
            acc_ref[...] += part

        @pl.when(tstep == nt - 1)
        def _():
            o_ref[...] = acc_ref[...].astype(o_ref.dtype)

    return pl.pallas_call(
        body, name=name, grid=(k_total // tm, n_total // tn, nt),
        in_specs=[x_spec, pl.BlockSpec((tt, tn), lambda i, j, t: (t, j))],
        out_specs=pl.BlockSpec((None, None, tm, tn), lambda i, j, t: widx(i, j)),
        out_shape=_sds(g.shape, g.dtype),
        scratch_shapes=[pltpu.VMEM((tm, tn), F32)],
        compiler_params=_cp(("parallel", "parallel", "arbitrary")),
    )(x, dy)


def _rms(x, w):
    return x * lax.rsqrt(jnp.mean(x * x, axis=-1, keepdims=True) + NORM_EPS) * w


def _row_tile(tc):
    return 256 if tc % 256 == 0 else 128


def _seg_spec(nct, d):
    return pl.BlockSpec((None, 8, d), lambda i: (jnp.minimum(i // nct, 1), 0, 0))


def _acc_rows(acc_ref, i, nct, rows):
    @pl.when((i == 0) | (i == nct))
    def _():
        acc_ref[...] = jnp.zeros_like(acc_ref)

    for r, val in enumerate(rows):
        acc_ref[r:r + 1, :] += val


def norm_mod_fwd(u, nw, modtab, tc, which, *, name):
    t, d = u.shape
    tr = _row_tile(tc)
    nct = tc // tr
    r0 = 3 * which

    def body(u_ref, nw_ref, mt_ref, h_ref, ht_ref):
        sh, sc = mt_ref[r0:r0 + 1, :], mt_ref[r0 + 1:r0 + 2, :]
        h = _rms(u_ref[...], nw_ref[...]) * (1.0 + sc) + sh
        h_ref[...] = h.astype(h_ref.dtype)
        ht_ref[...] = h.T.astype(ht_ref.dtype)

    return pl.pallas_call(
        body, name=name, grid=(t // tr,),
        in_specs=[pl.BlockSpec((tr, d), lambda i: (i, 0)), pl.BlockSpec((1, d), lambda i: (0, 0)), _seg_spec(nct, d)],
        out_specs=(pl.BlockSpec((tr, d), lambda i: (i, 0)), pl.BlockSpec((d, tr), lambda i: (0, i))),
        out_shape=(_sds((t, d), BF16), _sds((d, t), BF16)), compiler_params=_cp(("arbitrary",)),
    )(u, nw.reshape(1, d), modtab)


def norm_mod_bwd(u, nw, modtab, dh, du_in, tc, which, *, name):
    t, d = u.shape
    tr = _row_tile(tc)
    nct = tc // tr
    r0 = 3 * which

    def body(u_ref, nw_ref, mt_ref, dh_ref, dui_ref, du_ref, acc_ref):
        i = pl.program_id(0)
        sh, sc = mt_ref[r0:r0 + 1, :], mt_ref[r0 + 1:r0 + 2, :]
        _, vjp = jax.vjp(lambda x, w, a, b: _rms(x, w) * (1.0 + b) + a, u_ref[...], nw_ref[...], sh, sc)
        dx, dw, dsh, dsc = vjp(dh_ref[...].astype(F32))
        du_ref[...] = dui_ref[...] + dx
        _acc_rows(acc_ref, i, nct, (dw, dsh, dsc))

    row = pl.BlockSpec((tr, d), lambda i: (i, 0))
    return pl.pallas_call(
        body, name=name, grid=(t // tr,),
        in_specs=[row, pl.BlockSpec((1, d), lambda i: (0, 0)), _seg_spec(nct, d), row, row],
        out_specs=(row, _seg_spec(nct, d)),
        out_shape=(_sds((t, d), F32), _sds((2, 8, d), F32)), compiler_params=_cp(("arbitrary",)),
    )(u, nw.reshape(1, d), modtab, dh, du_in)


def resid_fwd(u, o, nw, modtab, tc, which, *, name):
    t, d = u.shape
    tr = _row_tile(tc)
    nct = tc // tr
    r0 = 3 * which + 2

    def body(u_ref, o_ref, nw_ref, mt_ref, out_ref):
        out_ref[...] = u_ref[...] + mt_ref[r0:r0 + 1, :] * _rms(o_ref[...], nw_ref[...])

    row = pl.BlockSpec((tr, d), lambda i: (i, 0))
    return pl.pallas_call(
        body, name=name, grid=(t // tr,),
        in_specs=[row, row, pl.BlockSpec((1, d), lambda i: (0, 0)), _seg_spec(nct, d)],
        out_specs=row, out_shape=_sds((t, d), F32), compiler_params=_cp(("arbitrary",)),
    )(u, o, nw.reshape(1, d), modtab)


def resid_bwd(o, nw, modtab, du, tc, which, *, name):
    t, d = o.shape
    tr = _row_tile(tc)
    nct = tc // tr
    r0 = 3 * which + 2

    def body(o_ref, nw_ref, mt_ref, du_ref, do_ref, acc_ref):
        i = pl.program_id(0)
        _, vjp = jax.vjp(lambda x, w, g: g * _rms(x, w), o_ref[...], nw_ref[...], mt_ref[r0:r0 + 1, :])
        dx, dw, dg = vjp(du_ref[...])
        do_ref[...] = dx.astype(do_ref.dtype)
        _acc_rows(acc_ref, i, nct, (dw, dg))

    row = pl.BlockSpec((tr, d), lambda i: (i, 0))
    return pl.pallas_call(
        body, name=name, grid=(t // tr,),
        in_specs=[row, pl.BlockSpec((1, d), lambda i: (0, 0)), _seg_spec(nct, d), row],
        out_specs=(row, _seg_spec(nct, d)),
        out_shape=(_sds((t, d), BF16), _sds((2, 8, d), F32)), compiler_params=_cp(("arbitrary",)),
    )(o, nw.reshape(1, d), modtab, du)


def swiglu_fwd(p, *, name):
    t, h2 = p.shape
    h = h2 // 2
    tr = CHUNK

    def body(p_ref, a_ref, at_ref):
        a = jax.nn.silu(p_ref[:, :h].astype(F32)) * p_ref[:, h:].astype(F32)
        a_ref[...] = a.astype(a_ref.dtype)
        at_ref[...] = a.T.astype(at_ref.dtype)

    return pl.pallas_call(
        body, name=name, grid=(t // tr,),
        in_specs=[pl.BlockSpec((tr, h2), lambda i: (i, 0))],
        out_specs=(pl.BlockSpec((tr, h), lambda i: (i, 0)), pl.BlockSpec((h, tr), lambda i: (0, i))),
        out_shape=(_sds((t, h), BF16), _sds((h, t), BF16)), compiler_params=_cp(("parallel",)),
    )(p)


def swiglu_bwd(p, da, *, name):
    t, h2 = p.shape
    h = h2 // 2
    tr = CHUNK

    def body(p_ref, da_ref, dp_ref):
        _, vjp = jax.vjp(lambda g, u: jax.nn.silu(g) * u, p_ref[:, :h].astype(F32), p_ref[:, h:].astype(F32))
        dg, du = vjp(da_ref[...].astype(F32))
        dp_ref[:, :h] = dg.astype(dp_ref.dtype)
        dp_ref[:, h:] = du.astype(dp_ref.dtype)

    return pl.pallas_call(
        body, name=name, grid=(t // tr,),
        in_specs=[pl.BlockSpec((tr, h2), lambda i: (i, 0)), pl.BlockSpec((tr, h), lambda i: (i, 0))],
        out_specs=pl.BlockSpec((tr, h2), lambda i: (i, 0)),
        out_shape=_sds((t, h2), BF16), compiler_params=_cp(("parallel",)),
    )(p, da)


def loss_fwd_bwd(u, target, tc, *, name):
    t, d = u.shape
    tr = _row_tile(tc)
    nct = tc // tr

    def body(u_ref, t_ref, loss_ref, du_ref):
        i = pl.program_id(0)

        @pl.when(i == 0)
        def _():
            loss_ref[...] = jnp.zeros_like(loss_ref)

        @pl.when(i < nct)
        def _():
            du_ref[...] = jnp.zeros_like(du_ref)

        @pl.when(i >= nct)
        def _():
            err = u_ref[...] - t_ref[...]
            du_ref[...] = err * (1.0 / d)
            loss_ref[...] += jnp.sum(jnp.sum(err * err, axis=1, keepdims=True), axis=0, keepdims=True) * (0.5 / d)

    return pl.pallas_call(
        body, name=name, grid=(t // tr,),
        in_specs=[pl.BlockSpec((tr, d), lambda i: (i, 0)), pl.BlockSpec((tr, d), lambda i: (jnp.maximum(i - nct, 0), 0))],
        out_specs=(pl.BlockSpec((1, 1), lambda i: (0, 0)), pl.BlockSpec((tr, d), lambda i: (i, 0))),
        out_shape=(_sds((1, 1), F32), _sds((t, d), F32)), compiler_params=_cp(("arbitrary",)),
    )(u, target)


SSD_INNER = 1024
SGU_WIDTH = 1024
XBC_DIM = 1536
EVEN_COLS = 4640
EVEN_PAD_COLS = 5120
Z_BLK, U_BLK, V_BLK, X_BLK, B_BLK, C_BLK, DT_BLK = 0, 8, 16, 24, 32, 34, 36
PAD_ROWS = 8


def _conv_scratch_fill(pad_ref, val, tc, s):
    pad_ref[...] = jnp.zeros_like(pad_ref)
    pad_ref[PAD_ROWS:PAD_ROWS + tc, :] = val[:tc]
    pad_ref[2 * PAD_ROWS + tc:2 * PAD_ROWS + tc + s, :] = val[tc:]


def _conv_taps(pad_ref, tc, s, k):
    off = k - CONV_K // 2
    return (pad_ref[PAD_ROWS + off:PAD_ROWS + off + tc, :],
            pad_ref[2 * PAD_ROWS + tc + off:2 * PAD_ROWS + tc + off + s, :])


def conv_fwd(p, wb, tc, *, name):
    t = p.shape[0]
    s = t - tc
    nblk = XBC_DIM // LANE

    def body(p_ref, wb_ref, out_ref, pad_ref):
        _conv_scratch_fill(pad_ref, p_ref[...], tc, s)
        acc_c = jnp.zeros((tc, LANE), F32) + wb_ref[5:6, :]
        acc_l = jnp.zeros((s, LANE), F32) + wb_ref[5:6, :]
        for k in range(CONV_K):
            xc, xl = _conv_taps(pad_ref, tc, s, k)
            acc_c += xc * wb_ref[k:k + 1, :]
            acc_l += xl * wb_ref[k:k + 1, :]
        out_ref[:tc, :] = jax.nn.silu(acc_c)
        out_ref[tc:, :] = jax.nn.silu(acc_l)

    return pl.pallas_call(
        body, name=name, grid=(nblk,),
        in_specs=[pl.BlockSpec((t, LANE), lambda j: (0, X_BLK + j)), pl.BlockSpec((8, LANE), lambda j: (0, j))],
        out_specs=pl.BlockSpec((t, LANE), lambda j: (0, j)),
        out_shape=_sds((t, XBC_DIM), F32),
        scratch_shapes=[pltpu.VMEM((t + 3 * PAD_ROWS, LANE), F32)],
        compiler_params=_cp(("parallel",)),
    )(p, wb)


def conv_bwd(p, wb, dxg, dskip, tc, *, name):
    t = p.shape[0]
    s = t - tc
    nblk = XBC_DIM // LANE
    nx = SSD_INNER // LANE

    def grp(j):
        return jnp.where(j < nx, j // 4, (j - nx) % 2)

    def sub(j):
        return jnp.where(j < nx, j % 4, 4 + (j - nx) // 2)

    def body(p_ref, wb_ref, d0_ref, d1_ref, ds_ref, dp_ref, dwb_ref, pad_ref, dpad_ref):
        j = pl.program_id(0)
        _conv_scratch_fill(pad_ref, p_ref[...], tc, s)
        pre = [jnp.zeros((tc, LANE), F32) + wb_ref[5:6, :], jnp.zeros((s, LANE), F32) + wb_ref[5:6, :]]
        for k in range(CONV_K):
            xc, xl = _conv_taps(pad_ref, tc, s, k)
            pre[0] += xc * wb_ref[k:k + 1, :]
            pre[1] += xl * wb_ref[k:k + 1, :]
        dx = d0_ref[...] + d1_ref[...] + jnp.where(j < nx, ds_ref[...], 0.0)
        dpre = []
        for part, rows in ((0, slice(0, tc)), (1, slice(tc, t))):
            sig = jax.nn.sigmoid(pre[part])
            dpre.append(dx[rows] * (sig * (1.0 + pre[part] * (1.0 - sig))))
        dwb_ref[...] = jnp.zeros_like(dwb_ref)
        dwb_ref[5:6, :] = jnp.sum(dpre[0], axis=0, keepdims=True) + jnp.sum(dpre[1], axis=0, keepdims=True)
        for k in range(CONV_K):
            xc, xl = _conv_taps(pad_ref, tc, s, k)
            dwb_ref[k:k + 1, :] = (jnp.sum(dpre[0] * xc, axis=0, keepdims=True)
                                   + jnp.sum(dpre[1] * xl, axis=0, keepdims=True))
        _conv_scratch_fill(dpad_ref, jnp.concatenate(dpre, axis=0), tc, s)
        acc_c = jnp.zeros((tc, LANE), F32)
        acc_l = jnp.zeros((s, LANE), F32)
        for k in range(CONV_K):
            gc, gl = _conv_taps(dpad_ref, tc, s, CONV_K - 1 - k)
            acc_c += gc * wb_ref[k:k + 1, :]
            acc_l += gl * wb_ref[k:k + 1, :]
        dp_ref[:tc, :] = acc_c.astype(dp_ref.dtype)
        dp_ref[tc:, :] = acc_l.astype(dp_ref.dtype)

    col = pl.BlockSpec((t, LANE), lambda j: (0, j))
    return pl.pallas_call(
        body, name=name, grid=(nblk,),
        in_specs=[pl.BlockSpec((t, LANE), lambda j: (0, X_BLK + j)), pl.BlockSpec((8, LANE), lambda j: (0, j)),
                  pl.BlockSpec((None, None, t, LANE), lambda j: (0, grp(j), 0, sub(j))),
                  pl.BlockSpec((None, None, t, LANE), lambda j: (1, grp(j), 0, sub(j))),
                  pl.BlockSpec((t, LANE), lambda j: (0, jnp.minimum(j, nx - 1)))],
        out_specs=(col, pl.BlockSpec((8, LANE), lambda j: (0, j))),
        out_shape=(_sds((t, XBC_DIM), BF16), _sds((8, XBC_DIM), F32)),
        scratch_shapes=[pltpu.VMEM((t + 3 * PAD_ROWS, LANE), F32), pltpu.VMEM((t + 3 * PAD_ROWS, LANE), F32)],
        compiler_params=_cp(("parallel",)),
    )(p, wb, dxg, dxg, dskip)


HEADS_PER_DG = 8


def _ssd_prep_fn(pre, bias, alog, rev):
    q = pre.shape[0]
    lane = lax.broadcasted_iota(jnp.int32, (1, LANE), 1)
    dt = jnp.where(lane < HEADS_PER_DG, jax.nn.softplus(pre + bias), 0.0)
    row = lax.broadcasted_iota(jnp.int32, (q, q), 0)
    col = lax.broadcasted_iota(jnp.int32, (q, q), 1)
    tri = jnp.where((col - row) * jnp.where(rev, 1, -1) >= 0, 1.0, 0.0)
    cs = jnp.dot(tri, dt * (-jnp.exp(alog)), precision=HI, preferred_element_type=F32)
    return dt, cs


def _scan_chunk(nc_ctx, nch):
    def idx(dg, i):
        fwd = i
        bwd = jnp.where(i < nc_ctx, nc_ctx - 1 - i, nch - 1 - (i - nc_ctx))
        return jnp.where(dg // 2 == 0, fwd, bwd)
    return idx


def ssd_prep_fwd(pre, bias, alog, *, name):
    _, t, _ = pre.shape
    blk = pl.BlockSpec((None, CHUNK, LANE), lambda dg, i: (dg, i, 0))
    par = pl.BlockSpec((None, 1, LANE), lambda dg, i: (dg, 0, 0))

    def body(pre_ref, b_ref, a_ref, dt_ref, cs_ref):
        dt, cs = _ssd_prep_fn(pre_ref[...], b_ref[...], a_ref[...], pl.program_id(0) // 2 == 1)
        dt_ref[...] = dt
        cs_ref[...] = cs

    return pl.pallas_call(
        body, name=name, grid=(4, t // CHUNK), in_specs=[blk, par, par], out_specs=(blk, blk),
        out_shape=(_sds(pre.shape, F32), _sds(pre.shape, F32)), compiler_params=_cp(("parallel", "parallel")),
    )(pre, bias, alog)


def ssd_prep_bwd(pre, bias, alog, ddt, dcs, *, name):
    _, t, _ = pre.shape
    blk = pl.BlockSpec((None, CHUNK, LANE), lambda dg, i: (dg, i, 0))
    par = pl.BlockSpec((None, 1, LANE), lambda dg, i: (dg, 0, 0))

    def body(pre_ref, b_ref, a_ref, ddt_ref, dcs_ref, dpre_ref, acc_ref):
        rev = pl.program_id(0) // 2 == 1
        _, vjp = jax.vjp(lambda x, b, a: _ssd_prep_fn(x, b, a, rev), pre_ref[...], b_ref[...], a_ref[...])
        dpre, db, da = vjp((ddt_ref[...], dcs_ref[...]))
        dpre_ref[...] = dpre

        @pl.when(pl.program_id(1) == 0)
        def _():
            acc_ref[...] = jnp.zeros_like(acc_ref)

        acc_ref[0:1, :] += db
        acc_ref[1:2, :] += da

    return pl.pallas_call(
        body, name=name, grid=(4, t // CHUNK), in_specs=[blk, par, par, blk, blk],
        out_specs=(blk, pl.BlockSpec((None, 8, LANE), lambda dg, i: (dg, 0, 0))),
        out_shape=(_sds(pre.shape, F32), _sds((4, 8, LANE), F32)), compiler_params=_cp(("parallel", "arbitrary")),
    )(pre, bias, alog, ddt, dcs)


def _onehot_col(a, h):
    lane = lax.broadcasted_iota(jnp.int32, (1, a.shape[1]), 1)
    return jnp.sum(jnp.where(lane == h, a, 0.0), axis=1, keepdims=True)


def _onehot_row(a, h):
    sub = lax.broadcasted_iota(jnp.int32, (a.shape[0], 1), 0)
    return jnp.sum(jnp.where(sub == h, a, 0.0), axis=0, keepdims=True)


def _bdot(a, b, dims):
    return lax.dot_general(a.astype(BF16), b.astype(BF16), (dims, ((), ())), preferred_element_type=F32)


def _ssd_pair(xblk, dt, cs, bm, cm, sp, rev, pair):
    q = xblk.shape[0]
    lane = lax.broadcasted_iota(jnp.int32, (1, LANE), 1)
    sub = lax.broadcasted_iota(jnp.int32, (LANE, 1), 0)
    row = lax.broadcasted_iota(jnp.int32, (q, q), 0)
    col = lax.broadcasted_iota(jnp.int32, (q, q), 1)
    mask = (col - row) * jnp.where(rev, 1, -1) >= 0
    last = jnp.where(rev, 0, q - 1)
    cs_t = cs.T
    g = _bdot(cm, bm, ((1,), (1,)))
    y = jnp.zeros((q, LANE), F32)
    escale = jnp.zeros((q, LANE), F32)
    xw = jnp.zeros((q, LANE), F32)
    dec = jnp.zeros((LANE, 1), F32)
    for hh in range(2):
        h = 2 * pair + hh
        c_col = _onehot_col(cs, h)
        c_row = _onehot_row(cs_t, h)
        tot = jnp.sum(jnp.where(lax.broadcasted_iota(jnp.int32, (1, q), 1) == last, c_row, 0.0), axis=1, keepdims=True)
        in_head = (lane >= hh * SSD_HEAD_DIM) & (lane < (hh + 1) * SSD_HEAD_DIM)
        xh = jnp.where(in_head, xblk * _onehot_col(dt, h), 0.0)
        ldec = jnp.where(mask, jnp.exp(jnp.where(mask, c_col - c_row, 0.0)), 0.0)
        y = y + _bdot(g * ldec, xh, ((1,), (0,)))
        escale = escale + jnp.where(in_head, jnp.exp(c_col), 0.0)
        xw = xw + xh * jnp.exp(tot - c_col)
        dec = dec + jnp.where((sub >= hh * SSD_HEAD_DIM) & (sub < (hh + 1) * SSD_HEAD_DIM), jnp.exp(tot), 0.0)
    y = y + _bdot(cm, sp, ((1,), (1,))) * escale
    s_new = sp * dec + _bdot(xw, bm, ((0,), (0,)))
    return y, s_new


def ssd_fwd(xbc, dt, cs, tc, *, name):
    t = xbc.shape[0]
    nch = t // CHUNK
    sidx = _scan_chunk(tc // CHUNK, nch)
    gw = SSD_INNER // 2
    nb = SSD_INNER // LANE

    def body(x_ref, b_ref, c_ref, dt_ref, cs_ref, y_ref, sp_ref, s_ref):
        @pl.when(pl.program_id(1) == 0)
        def _():
            s_ref[...] = jnp.zeros_like(s_ref)

        rev = pl.program_id(0) // 2 == 1
        sp_ref[...] = s_ref[...]
        for p in range(gw // LANE):
            blk = slice(p * LANE, (p + 1) * LANE)
            y, s_new = _ssd_pair(x_ref[:, blk], dt_ref[...], cs_ref[...], b_ref[...], c_ref[...], s_ref[blk, :], rev, p)
            y_ref[:, blk] = y
            s_ref[blk, :] = s_new

    return pl.pallas_call(
        body, name=name, grid=(4, nch),
        in_specs=[pl.BlockSpec((CHUNK, gw), lambda dg, i: (sidx(dg, i), dg % 2)),
                  pl.BlockSpec((CHUNK, LANE), lambda dg, i: (sidx(dg, i), nb + dg % 2)),
                  pl.BlockSpec((CHUNK, LANE), lambda dg, i: (sidx(dg, i), nb + 2 + dg % 2)),
                  pl.BlockSpec((None, CHUNK, LANE), lambda dg, i: (dg, sidx(dg, i), 0)),
                  pl.BlockSpec((None, CHUNK, LANE), lambda dg, i: (dg, sidx(dg, i), 0))],
        out_specs=(pl.BlockSpec((None, CHUNK, gw), lambda dg, i: (dg // 2, sidx(dg, i), dg % 2)),
                   pl.BlockSpec((None, None, gw, SSD_STATE), lambda dg, i: (dg, sidx(dg, i), 0, 0))),
        out_shape=(_sds((2, t, SSD_INNER), F32), _sds((4, nch, gw, SSD_STATE), F32)),
        scratch_shapes=[pltpu.VMEM((gw, SSD_STATE), F32)],
        compiler_params=_cp(("parallel", "arbitrary")),
    )(xbc, xbc, xbc, dt, cs)


def ssd_bwd(xbc, dt, cs, sprev, dy, tc, *, name):
    t = xbc.shape[0]
    nch = t // CHUNK
    fidx = _scan_chunk(tc // CHUNK, nch)
    sidx = lambda dg, i: fidx(dg, nch - 1 - i)
    gw = SSD_INNER // 2
    nb = SSD_INNER // LANE

    def body(x_ref, b_ref, c_ref, dt_ref, cs_ref, sp_ref, dy_ref, dxg_ref, ddt_ref, dcs_ref, ds_ref):
        @pl.when(pl.program_id(1) == 0)
        def _():
            ds_ref[...] = jnp.zeros_like(ds_ref)

        rev = pl.program_id(0) // 2 == 1
        ddt = jnp.zeros((CHUNK, LANE), F32)
        dcs = jnp.zeros((CHUNK, LANE), F32)
        db = jnp.zeros((CHUNK, SSD_STATE), F32)
        dc = jnp.zeros((CHUNK, SSD_STATE), F32)
        for p in range(gw // LANE):
            blk = slice(p * LANE, (p + 1) * LANE)
            _, vjp = jax.vjp(functools.partial(_ssd_pair, rev=rev, pair=p),
                             x_ref[:, blk], dt_ref[...], cs_ref[...], b_ref[...], c_ref[...], sp_ref[blk, :])
            dx, ddt_p, dcs_p, db_p, dc_p, dsp = vjp((dy_ref[:, blk], ds_ref[blk, :]))
            dxg_ref[:, blk] = dx
            ds_ref[blk, :] = dsp
            ddt, dcs, db, dc = ddt + ddt_p, dcs + dcs_p, db + db_p, dc + dc_p
        dxg_ref[:, gw:gw + SSD_STATE] = db
        dxg_ref[:, gw + SSD_STATE:] = dc
        ddt_ref[...] = ddt
        dcs_ref[...] = dcs

    hd = pl.BlockSpec((None, CHUNK, LANE), lambda dg, i: (dg, sidx(dg, i), 0))
    return pl.pallas_call(
        body, name=name, grid=(4, nch),
        in_specs=[pl.BlockSpec((CHUNK, gw), lambda dg, i: (sidx(dg, i), dg % 2)),
                  pl.BlockSpec((CHUNK, LANE), lambda dg, i: (sidx(dg, i), nb + dg % 2)),
                  pl.BlockSpec((CHUNK, LANE), lambda dg, i: (sidx(dg, i), nb + 2 + dg % 2)),
                  hd, hd,
                  pl.BlockSpec((None, None, gw, SSD_STATE), lambda dg, i: (dg, sidx(dg, i), 0, 0)),
                  pl.BlockSpec((CHUNK, gw), lambda dg, i: (sidx(dg, i), dg % 2))],
        out_specs=(pl.BlockSpec((None, None, CHUNK, gw + 2 * SSD_STATE), lambda dg, i: (dg // 2, dg % 2, sidx(dg, i), 0)),
                   hd, hd),
        out_shape=(_sds((2, 2, t, gw + 2 * SSD_STATE), F32), _sds((4, t, LANE), F32), _sds((4, t, LANE), F32)),
        scratch_shapes=[pltpu.VMEM((gw, SSD_STATE), F32)],
        compiler_params=_cp(("parallel", "arbitrary")),
    )(xbc, xbc, xbc, dt, cs, sprev, dy)


def _ssd_finish_fn(y0, y1, xs, z, dskip, nw):
    y = (y0 + y1 + xs * dskip) * jax.nn.silu(z)
    half = y.shape[1] // 2
    first = lax.broadcasted_iota(jnp.int32, (1, y.shape[1]), 1) < half
    sq = y * y
    m0 = jnp.sum(jnp.where(first, sq, 0.0), axis=1, keepdims=True) / half
    m1 = jnp.sum(jnp.where(first, 0.0, sq), axis=1, keepdims=True) / half
    return y * jnp.where(first, lax.rsqrt(m0 + NORM_EPS), lax.rsqrt(m1 + NORM_EPS)) * nw


def ssd_finish_fwd(y, xbc, p, dskip, nw, *, name):
    t = xbc.shape[0]
    tr = CHUNK
    w = SSD_INNER
    row = pl.BlockSpec((tr, w), lambda i: (i, 0))
    par = pl.BlockSpec((1, w), lambda i: (0, 0))

    def body(y0_ref, y1_ref, x_ref, z_ref, ds_ref, nw_ref, o_ref):
        o_ref[...] = _ssd_finish_fn(y0_ref[...], y1_ref[...], x_ref[...], z_ref[...], ds_ref[...], nw_ref[...]).astype(o_ref.dtype)

    return pl.pallas_call(
        body, name=name, grid=(t // tr,),
        in_specs=[pl.BlockSpec((None, tr, w), lambda i: (0, i, 0)), pl.BlockSpec((None, tr, w), lambda i: (1, i, 0)),
                  row, row, par, par],
        out_specs=row, out_shape=_sds((t, w), BF16), compiler_params=_cp(("parallel",)),
    )(y, y, xbc, p, dskip, nw)


def ssd_finish_bwd(y, xbc, p, dskip, nw, dout, *, name):
    t = xbc.shape[0]
    tr = CHUNK
    w = SSD_INNER
    row = pl.BlockSpec((tr, w), lambda i: (i, 0))
    par = pl.BlockSpec((1, w), lambda i: (0, 0))

    def body(y0_ref, y1_ref, x_ref, z_ref, ds_ref, nw_ref, do_ref, dy_ref, dx_ref, dz_ref, acc_ref):
        _, vjp = jax.vjp(_ssd_finish_fn, y0_ref[...], y1_ref[...], x_ref[...], z_ref[...], ds_ref[...], nw_ref[...])
        dy0, _, dx, dz, dds, dnw = vjp(do_ref[...])
        dy_ref[...] = dy0
        dx_ref[...] = dx
        dz_ref[...] = dz.astype(dz_ref.dtype)

        @pl.when(pl.program_id(0) == 0)
        def _():
            acc_ref[...] = jnp.zeros_like(acc_ref)

        acc_ref[0:1, :] += dds
        acc_ref[1:2, :] += dnw

    return pl.pallas_call(
        body, name=name, grid=(t // tr,),
        in_specs=[pl.BlockSpec((None, tr, w), lambda i: (0, i, 0)), pl.BlockSpec((None, tr, w), lambda i: (1, i, 0)),
                  row, row, par, par, row],
        out_specs=(row, row, row, pl.BlockSpec((8, w), lambda i: (0, 0))),
        out_shape=(_sds((t, w), F32), _sds((t, w), F32), _sds((t, w), BF16), _sds((8, w), F32)),
        compiler_params=_cp(("arbitrary",)),
    )(y, y, xbc, p, dskip, nw, dout)


SGU_GROUPS = 8


def _sgu_fn(us, vs, ws, bs):
    n = SGU_GROUPS * LANE
    vf = [jax.nn.gelu(v) for v in vs]
    mu = sum(jnp.sum(v, axis=1, keepdims=True) for v in vf) / n
    var = sum(jnp.sum(jnp.square(v - mu), axis=1, keepdims=True) for v in vf) / n
    rstd = lax.rsqrt(var + NORM_EPS)
    return tuple(jax.nn.gelu(u) * (_bdot(w, (v - mu) * rstd, ((1,), (0,))) + b) for u, v, w, b in zip(us, vf, ws, bs))


def sgu_fwd(p, w, b, *, name):
    t = p.shape[0]
    wd = SGU_WIDTH

    def body(u_ref, v_ref, w_ref, b_ref, o_ref):
        sl = [slice(g * LANE, (g + 1) * LANE) for g in range(SGU_GROUPS)]
        ys = _sgu_fn([u_ref[:, s] for s in sl], [v_ref[:, s] for s in sl], [w_ref[g] for g in range(SGU_GROUPS)],
                     [b_ref[g] for g in range(SGU_GROUPS)])
        for s, yv in zip(sl, ys):
            o_ref[:, s] = yv.astype(o_ref.dtype)

    return pl.pallas_call(
        body, name=name, grid=(t // CHUNK,),
        in_specs=[pl.BlockSpec((CHUNK, wd), lambda i: (i, U_BLK * LANE // wd)), pl.BlockSpec((CHUNK, wd), lambda i: (i, V_BLK * LANE // wd)),
                  pl.BlockSpec((SGU_GROUPS, CHUNK, CHUNK), lambda i: (0, 0, 0)), pl.BlockSpec((SGU_GROUPS, CHUNK, 1), lambda i: (0, 0, 0))],
        out_specs=pl.BlockSpec((CHUNK, wd), lambda i: (i, 0)),
        out_shape=_sds((t, wd), BF16), compiler_params=_cp(("parallel",)),
    )(p, p, w, b)


def sgu_bwd(p, w, b, dout, *, name):
    t = p.shape[0]
    wd = SGU_WIDTH

    def body(u_ref, v_ref, w_ref, b_ref, do_ref, duv_ref, dw_ref, db_ref):
        sl = [slice(g * LANE, (g + 1) * LANE) for g in range(SGU_GROUPS)]
        _, vjp = jax.vjp(_sgu_fn, [u_ref[:, s] for s in sl], [v_ref[:, s] for s in sl],
                         [w_ref[g] for g in range(SGU_GROUPS)], [b_ref[g] for g in range(SGU_GROUPS)])
        dus, dvs, dws, dbs = vjp(tuple(do_ref[:, s] for s in sl))

        @pl.when(pl.program_id(0) == 0)
        def _():
            dw_ref[...] = jnp.zeros_like(dw_ref)
            db_ref[...] = jnp.zeros_like(db_ref)

        for g, s in enumerate(sl):
            duv_ref[:, s] = dus[g].astype(duv_ref.dtype)
            duv_ref[:, slice(wd + g * LANE, wd + (g + 1) * LANE)] = dvs[g].astype(duv_ref.dtype)
            dw_ref[g] += dws[g]
            db_ref[g] += dbs[g]

    wspec = pl.BlockSpec((SGU_GROUPS, CHUNK, CHUNK), lambda i: (0, 0, 0))
    bspec = pl.BlockSpec((SGU_GROUPS, CHUNK, 1), lambda i: (0, 0, 0))
    return pl.pallas_call(
        body, name=name, grid=(t // CHUNK,),
        in_specs=[pl.BlockSpec((CHUNK, wd), lambda i: (i, U_BLK * LANE // wd)), pl.BlockSpec((CHUNK, wd), lambda i: (i, V_BLK * LANE // wd)),
                  wspec, bspec, pl.BlockSpec((CHUNK, wd), lambda i: (i, 1))],
        out_specs=(pl.BlockSpec((CHUNK, 2 * wd), lambda i: (i, 0)), wspec, bspec),
        out_shape=(_sds((t, 2 * wd), BF16), _sds(w.shape, F32), _sds(b.shape, F32)),
        compiler_params=_cp(("arbitrary",)),
    )(p, p, w, b, dout)


def even_cols_permute(w):
    z, xbc, dt, u, v = jnp.split(w, (1024, 2560, 2592, 3616), axis=-1)
    pad = jnp.zeros(w.shape[:-1] + (EVEN_PAD_COLS - EVEN_COLS,), w.dtype)
    return jnp.concatenate([z, u, v, xbc, dt, pad], axis=-1)


def even_cols_unpermute(w):
    z, u, v, xbc, dt = jnp.split(w[..., :EVEN_COLS], (1024, 2048, 3072, 4608), axis=-1)
    return jnp.concatenate([z, xbc, dt, u, v], axis=-1)


def _dt_cols(p):
    t = p.shape[0]
    d = p[:, DT_BLK * LANE:DT_BLK * LANE + 4 * HEADS_PER_DG].reshape(t, 4, HEADS_PER_DG).transpose(1, 0, 2)
    return jnp.pad(d, ((0, 0), (0, 0), (0, LANE - HEADS_PER_DG)))


def _heads_to_lanes(a):
    return jnp.pad(a.reshape(4, 1, HEADS_PER_DG), ((0, 0), (0, 0), (0, LANE - HEADS_PER_DG)))


def even_params(conv_w, conv_b, dt_bias, a_log, d_skip, ssd_nw, sgu_w, sgu_b):
    wb = jnp.concatenate([conv_w, conv_b[None], jnp.zeros((2, XBC_DIM), F32)], axis=0)
    return dict(wb=wb, dtb=_heads_to_lanes(dt_bias), alog=_heads_to_lanes(a_log),
                dskip=jnp.repeat(d_skip, SSD_HEAD_DIM)[None], ssd_nw=ssd_nw[None], sgu_w=sgu_w, sgu_b=sgu_b[..., None])


def even_mixer_fwd(h, w_in, w_out, li, ep, tc, tag):
    p = mm_nn(h, w_in, li, "n", name=f"{tag}_in")
    xbc = conv_fwd(p, ep["wb"], tc, name=f"{tag}_conv")
    pre = _dt_cols(p)
    dt, cs = ssd_prep_fwd(pre, ep["dtb"], ep["alog"], name=f"{tag}_prep")
    y, sprev = ssd_fwd(xbc, dt, cs, tc, name=f"{tag}_ssd")
    yssd = ssd_finish_fwd(y, xbc, p, ep["dskip"], ep["ssd_nw"], name=f"{tag}_fin")
    ysgu = sgu_fwd(p, ep["sgu_w"], ep["sgu_b"], name=f"{tag}_sgu")
    ymix = jnp.concatenate([yssd, ysgu], axis=1)
    o = mm_nn(ymix, w_out, li, "k", name=f"{tag}_out")
    return o, (p, xbc, pre, dt, cs, y, sprev, ymix)


def even_mixer_bwd(saved, do, ht, w_in, w_out, g_in, g_out, li, ep, tc, tag):
    p, xbc, pre, dt, cs, y, sprev, ymix = saved
    t = p.shape[0]
    g_out = mm_tn(ymix, do, g_out, li, "k", name=f"{tag}_out_dw")
    dymix = mm_nt(do, w_out, li, "k", name=f"{tag}_out_dx")
    dy, dxskip, dz, acc_fin = ssd_finish_bwd(y, xbc, p, ep["dskip"], ep["ssd_nw"], dymix, name=f"{tag}_fin_b")
    duv, dsgu_w, dsgu_b = sgu_bwd(p, ep["sgu_w"], ep["sgu_b"], dymix, name=f"{tag}_sgu_b")
    dxg, ddt, dcs = ssd_bwd(xbc, dt, cs, sprev, dy, tc, name=f"{tag}_ssd_b")
    dpre, acc_prep = ssd_prep_bwd(pre, ep["dtb"], ep["alog"], ddt, dcs, name=f"{tag}_prep_b")
    dxbc, dwb = conv_bwd(p, ep["wb"], dxg, dxskip, tc, name=f"{tag}_conv_b")
    ddt_cols = dpre[:, :, :HEADS_PER_DG].transpose(1, 0, 2).reshape(t, 4 * HEADS_PER_DG).astype(BF16)
    ddt_cols = jnp.pad(ddt_cols, ((0, 0), (0, EVEN_PAD_COLS - DT_BLK * LANE - 4 * HEADS_PER_DG)))
    dp = jnp.concatenate([dz, duv, dxbc, ddt_cols], axis=1)
    dh = mm_nt(dp, w_in, li, "n", name=f"{tag}_in_dx")
    g_in = mm_tn(ht, dp, g_in, li, "n", name=f"{tag}_in_dw", x_is_transposed=True)
    small = dict(
        e_conv_w=dwb[:CONV_K], e_conv_b=dwb[CONV_K],
        e_dt_bias=acc_prep[:, 0, :HEADS_PER_DG].reshape(2, 2 * HEADS_PER_DG),
        e_a_log=acc_prep[:, 1, :HEADS_PER_DG].reshape(2, 2 * HEADS_PER_DG),
        e_d_skip=acc_fin[0].reshape(-1, SSD_HEAD_DIM).sum(axis=1), e_ssd_norm_w=acc_fin[1],
        e_sgu_w=dsgu_w, e_sgu_b=dsgu_b[..., 0])
    return dh, g_in, g_out, small


ATT_HEADS = 16
ATT_KV = 4
Q_BLKS, K_BLKS = ATT_HEADS, ATT_KV


def rope_tables(tc, s):
    quarter = ATT_HEAD_DIM // 4
    pos = jnp.arange(s)
    inv = ROPE_BASE ** (-jnp.arange(quarter, dtype=F32) / quarter)
    a_row = (pos // GRID_W).astype(F32)[:, None] * inv
    a_col = (pos % GRID_W).astype(F32)[:, None] * inv
    cos = jnp.concatenate([jnp.cos(a_row)] * 2 + [jnp.cos(a_col)] * 2, axis=1)
    sin = jnp.concatenate([-jnp.sin(a_row), jnp.sin(a_row), -jnp.sin(a_col), jnp.sin(a_col)], axis=1)
    return (jnp.concatenate([jnp.ones((tc, ATT_HEAD_DIM), F32), cos], axis=0),
            jnp.concatenate([jnp.zeros((tc, ATT_HEAD_DIM), F32), sin], axis=0))


def _swap_halves(x):
    lane = lax.broadcasted_iota(jnp.int32, x.shape, 1)
    return jnp.where(lane % 64 < 32, pltpu.roll(x, 96, 1), pltpu.roll(x, 32, 1))


def rope_fwd(p, cos, sin, *, name):
    t = p.shape[0]
    tr = _pick(t, (1088, 640))
    scale = ATT_HEAD_DIM ** -0.5

    def body(p_ref, c_ref, s_ref, o_ref):
        x = p_ref[...]
        r = x * c_ref[...] + _swap_halves(x) * s_ref[...]
        o_ref[...] = (r * jnp.where(pl.program_id(1) < Q_BLKS, scale, 1.0)).astype(o_ref.dtype)

    tab = pl.BlockSpec((tr, LANE), lambda i, j: (i, 0))
    return pl.pallas_call(
        body, name=name, grid=(t // tr, Q_BLKS + K_BLKS),
        in_specs=[pl.BlockSpec((tr, LANE), lambda i, j: (i, j)), tab, tab],
        out_specs=pl.BlockSpec((tr, LANE), lambda i, j: (i, j)),
        out_shape=_sds((t, (Q_BLKS + K_BLKS) * LANE), BF16), compiler_params=_cp(("parallel", "parallel")),
    )(p, cos, sin)


def rope_bwd(dq, dk, dv, cos, sin, *, name):
    t = dq.shape[0]
    tr = _pick(t, (1088, 640))
    scale = ATT_HEAD_DIM ** -0.5

    def body(dq_ref, dk_ref, dv_ref, c_ref, s_ref, o_ref):
        j = pl.program_id(1)

        def unrot(g):
            return g * c_ref[...] + _swap_halves(g * s_ref[...])

        @pl.when(j < Q_BLKS)
        def _():
            o_ref[...] = (unrot(dq_ref[...]) * scale).astype(o_ref.dtype)

        @pl.when((j >= Q_BLKS) & (j < Q_BLKS + K_BLKS))
        def _():
            o_ref[...] = unrot(dk_ref[...]).astype(o_ref.dtype)

        @pl.when(j >= Q_BLKS + K_BLKS)
        def _():
            o_ref[...] = dv_ref[...].astype(o_ref.dtype)

    tab = pl.BlockSpec((tr, LANE), lambda i, j: (i, 0))
    return pl.pallas_call(
        body, name=name, grid=(t // tr, Q_BLKS + 2 * K_BLKS),
        in_specs=[pl.BlockSpec((tr, LANE), lambda i, j: (i, jnp.minimum(j, Q_BLKS - 1))),
                  pl.BlockSpec((None, tr, LANE), lambda i, j: (jnp.clip(j - Q_BLKS, 0, K_BLKS - 1), i, 0)),
                  pl.BlockSpec((None, tr, LANE), lambda i, j: (jnp.clip(j - Q_BLKS - K_BLKS, 0, K_BLKS - 1), i, 0)), tab, tab],
        out_specs=pl.BlockSpec((tr, LANE), lambda i, j: (i, j)),
        out_shape=_sds((t, (Q_BLKS + 2 * K_BLKS) * LANE), BF16), compiler_params=_cp(("parallel", "parallel")),
    )(dq, dk, dv, cos, sin)


def _attn_tile(q4, kp, kc, kn, vp, vc, vn, kx, vx, sinks, is_lat, has_prev, has_next):
    q = kp.shape[0]
    nq = q4.shape[0]
    row = jnp.bitwise_and(lax.broadcasted_iota(jnp.int32, (nq, q), 0), q - 1)
    col = lax.broadcasted_iota(jnp.int32, (nq, q), 1)
    m_prev = (col - row) >= (1 - has_prev) * q
    m_cur = (row - row) >= (1 - is_lat)
    m_next = (row - col) >= (1 - has_next) * q
    lane = lax.broadcasted_iota(jnp.int32, (1, LANE), 1)
    sink = jnp.concatenate([jnp.broadcast_to(jnp.sum(jnp.where(lane == 0, s, 0.0), axis=1, keepdims=True), (q, 1)) for s in sinks],
                           axis=0)
    s_p = jnp.where(m_prev, _bdot(q4, kp, ((1,), (1,))), NEG_INF)
    s_c = jnp.where(m_cur, _bdot(q4, kc, ((1,), (1,))), NEG_INF)
    s_n = jnp.where(m_next, _bdot(q4, kn, ((1,), (1,))), NEG_INF)
    s_x = _bdot(q4, kx, ((1,), (1,)))
    mx = [jnp.max(a, axis=1, keepdims=True) for a in (s_p, s_c, s_n, s_x)]
    m = lax.stop_gradient(jnp.maximum(jnp.maximum(jnp.maximum(mx[0], mx[1]), jnp.maximum(mx[2], mx[3])), sink))
    e = [jnp.exp(a - m) for a in (s_p, s_c, s_n, s_x)]
    inv = 1.0 / (sum(jnp.sum(a, axis=1, keepdims=True) for a in e) + jnp.exp(sink - m))
    return sum(_bdot(a * inv, v, ((1,), (0,))) for a, v in zip(e, (vp, vc, vn, vx)))


def _attn_specs(t, tc):
    nblk = t // CHUNK
    hw = ATT_GROUP * LANE
    kcol = lambda kv: Q_BLKS + kv
    vcol = lambda kv: Q_BLKS + K_BLKS + kv
    prev = lambda n: jnp.maximum(n - 1, 0)
    nxt = lambda n: jnp.minimum(n + 1, nblk - 1)
    blk = lambda rowf, colf: pl.BlockSpec((CHUNK, LANE), lambda kv, n: (rowf(n), colf(kv)))
    same = lambda n: n
    return [pl.BlockSpec((CHUNK, hw), lambda kv, n: (n, kv)),
            blk(prev, kcol), blk(same, kcol), blk(nxt, kcol), blk(prev, vcol), blk(same, vcol), blk(nxt, vcol),
            pl.BlockSpec((tc, LANE), lambda kv, n: (0, kcol(kv))), pl.BlockSpec((tc, LANE), lambda kv, n: (0, vcol(kv))),
            pl.BlockSpec((None, 8, LANE), lambda kv, n: (kv, 0, 0))]


def _attn_args(refs, n, nct, nblk):
    q_ref, kp, kc, kn, vp, vc, vn, kx, vx, sk = refs
    f = lambda r: r[...].astype(F32)
    q4 = _stack_heads(q_ref)
    sinks = [sk[g:g + 1, :] for g in range(ATT_GROUP)]
    flags = ((n >= nct).astype(jnp.int32), (n >= nct + 1).astype(jnp.int32), ((n >= nct) & (n + 1 < nblk)).astype(jnp.int32))
    return (q4, f(kp), f(kc), f(kn), f(vp), f(vc), f(vn), f(kx), f(vx), sinks), flags


def _stack_heads(ref):
    return jnp.concatenate([ref[:, g * LANE:(g + 1) * LANE].astype(F32) for g in range(ATT_GROUP)], axis=0)


def _unstack_heads(ref, val):
    for g in range(ATT_GROUP):
        ref[:, g * LANE:(g + 1) * LANE] = val[g * CHUNK:(g + 1) * CHUNK].astype(ref.dtype)


def attn_fwd(qk, p, sink, tc, *, name):
    t = qk.shape[0]
    nblk, nct = t // CHUNK, tc // CHUNK
    hw = ATT_GROUP * LANE

    def body(*refs):
        o_ref = refs[-1]
        args, flags = _attn_args(refs[:-1], pl.program_id(1), nct, nblk)
        _unstack_heads(o_ref, _attn_tile(*args, *flags))

    return pl.pallas_call(
        body, name=name, grid=(ATT_KV, nblk), in_specs=_attn_specs(t, tc),
        out_specs=pl.BlockSpec((CHUNK, hw), lambda kv, n: (n, kv)),
        out_shape=_sds((t, ATT_HEADS * LANE), BF16), compiler_params=_cp(("parallel", "parallel")),
    )(qk, qk, qk, qk, p, p, p, qk, p, sink)


def attn_bwd(qk, p, sink, do, tc, *, name):
    t = qk.shape[0]
    nblk, nct = t // CHUNK, tc // CHUNK
    hw = ATT_GROUP * LANE

    def body(*refs):
        do_ref, dq_ref, dk_ref, dv_ref, dsk_ref = refs[-5:]
        n = pl.program_id(1)
        args, flags = _attn_args(refs[:-5], n, nct, nblk)
        _, vjp = jax.vjp(lambda *a: _attn_tile(*a, *flags), *args)
        dq4, dkp, dkc, dkn, dvp, dvc, dvn, dkx, dvx, dsinks = vjp(_stack_heads(do_ref))

        @pl.when(n == 0)
        def _():
            dk_ref[...] = jnp.zeros_like(dk_ref)
            dv_ref[...] = jnp.zeros_like(dv_ref)
            dsk_ref[...] = jnp.zeros_like(dsk_ref)

        _unstack_heads(dq_ref, dq4)
        for g in range(ATT_GROUP):
            dsk_ref[g:g + 1, :] += dsinks[g]
        for blk, dkb, dvb in ((jnp.maximum(n - 1, 0), dkp, dvp), (n, dkc, dvc), (jnp.minimum(n + 1, nblk - 1), dkn, dvn)):
            rows = pl.ds(pl.multiple_of(blk * CHUNK, CHUNK), CHUNK)
            dk_ref[rows, :] += dkb
            dv_ref[rows, :] += dvb
        dk_ref[0:tc, :] += dkx
        dv_ref[0:tc, :] += dvx

    kvacc = pl.BlockSpec((None, t, LANE), lambda kv, n: (kv, 0, 0))
    return pl.pallas_call(
        body, name=name, grid=(ATT_KV, nblk),
        in_specs=_attn_specs(t, tc) + [pl.BlockSpec((CHUNK, hw), lambda kv, n: (n, kv))],
        out_specs=(pl.BlockSpec((CHUNK, hw), lambda kv, n: (n, kv)), kvacc, kvacc,
                   pl.BlockSpec((None, 8, LANE), lambda kv, n: (kv, 0, 0))),
        out_shape=(_sds((t, ATT_HEADS * LANE), F32), _sds((ATT_KV, t, LANE), F32), _sds((ATT_KV, t, LANE), F32),
                   _sds((ATT_KV, 8, LANE), F32)),
        compiler_params=_cp(("parallel", "arbitrary")),
    )(qk, qk, qk, qk, p, p, p, qk, p, sink, do)


def sink_rows(sink):
    s = jnp.broadcast_to(sink.reshape(ATT_KV, ATT_GROUP, 1), (ATT_KV, ATT_GROUP, LANE))
    return jnp.pad(s, ((0, 0), (0, 8 - ATT_GROUP), (0, 0)))


def odd_mixer_fwd(h, w_qkv, w_out, li, sink, cos, sin, tc, tag):
    p = mm_nn(h, w_qkv, li, "n", name=f"{tag}_qkv")
    qk = rope_fwd(p, cos, sin, name=f"{tag}_rope")
    att = attn_fwd(qk, p, sink, tc, name=f"{tag}_att")
    o = mm_nn(att, w_out, li, "k", name=f"{tag}_out")
    return o, (p, qk, att)


def odd_mixer_bwd(saved, do, ht, w_qkv, w_out, g_qkv, g_out, li, sink, cos, sin, tc, tag):
    p, qk, att = saved
    g_out = mm_tn(att, do, g_out, li, "k", name=f"{tag}_out_dw")
    datt = mm_nt(do, w_out, li, "k", name=f"{tag}_out_dx")
    dq, dk, dv, dsink = attn_bwd(qk, p, sink, datt, tc, name=f"{tag}_att_b")
    dp = rope_bwd(dq, dk, dv, cos, sin, name=f"{tag}_rope_b")
    dh = mm_nt(dp, w_qkv, li, "n", name=f"{tag}_qkv_dx")
    g_qkv = mm_tn(ht, dp, g_qkv, li, "n", name=f"{tag}_qkv_dw", x_is_transposed=True)
    return dh, g_qkv, g_out, dict(o_sink=dsink[:, :ATT_GROUP, 0].reshape(-1))


ANY = pl.BlockSpec(memory_space=pl.ANY)


def _place():
    return lax.axis_index("x"), lax.axis_index("y"), lax.axis_index("c")


DMA_PIECES = 16


def _pieces(shape):
    if len(shape) < 2:
        return [()]
    lead, k = shape[:-2], shape[-2]
    split = 1
    while math.prod(lead) * split < DMA_PIECES and k % (2 * split) == 0 and (k // (2 * split)) % 16 == 0:
        split *= 2
    rows = k // split
    out = []
    for li in itertools.product(*[range(n) for n in lead]):
        out += [li + (pl.ds(q * rows, rows),) for q in range(split)]
    return out


def _start_pieces(make, src, dst):
    for idx in _pieces(src.shape):
        make(src.at[idx] if idx else src, dst.at[idx] if idx else dst).start()


def allgather8(blk, *, name):
    def body(x_ref, out_ref, send_sems, recv_sems, local_sem):
        x, y, c = _place()
        me, sibling = (x, y, c), (x, y, 1 - c)
        chips = [(1 - x, y), (x, 1 - y), (1 - x, 1 - y)]

        def slot(px, py, pc):
            return out_ref.at[4 * px + 2 * py + pc]

        def remote(k, to):
            return lambda src, dst: pltpu.make_async_remote_copy(
                src_ref=src, dst_ref=dst, send_sem=send_sems.at[k], recv_sem=recv_sems.at[k], device_id=to, device_id_type=MESH_ID)

        def local(src, dst):
            return pltpu.make_async_copy(src, dst, local_sem)

        _start_pieces(local, x_ref, slot(*me))
        _start_pieces(remote(0, sibling), x_ref, slot(*me))
        for j, chip in enumerate(chips):
            remote(1 + j, (*chip, c))(x_ref, slot(*me)).start()
        for j, chip in enumerate(chips):
            blk = slot(*chip, c)
            remote(1 + j, me)(blk, blk).wait_recv()
            _start_pieces(remote(4 + j, sibling), blk, blk)
        remote(0, me)(slot(*sibling), slot(*sibling)).wait_recv()
        for j, chip in enumerate(chips):
            blk = slot(*chip, 1 - c)
            remote(4 + j, me)(blk, blk).wait_recv()
        remote(0, sibling)(x_ref, slot(*me)).wait_send()
        for j, chip in enumerate(chips):
            remote(1 + j, (*chip, c))(x_ref, slot(*me)).wait_send()
            remote(4 + j, sibling)(slot(*chip, c), slot(*chip, c)).wait_send()
        local(x_ref, slot(*me)).wait()

    return pl.pallas_call(
        body, name=name, out_shape=_sds((N_DEV,) + blk.shape, blk.dtype), in_specs=[ANY], out_specs=ANY,
        scratch_shapes=[pltpu.SemaphoreType.DMA((7,)), pltpu.SemaphoreType.DMA((7,)), pltpu.SemaphoreType.DMA],
        compiler_params=pltpu.CompilerParams(has_side_effects=True),
    )(blk)


def _flip(r, xi, yi):
    return (1 - xi if r & 2 else xi), (1 - yi if r & 1 else yi)


def _to_sibling(send_sem, recv_sem):
    x, y, c = _place()
    return lambda src, dst: pltpu.make_async_remote_copy(src_ref=src, dst_ref=dst, send_sem=send_sem, recv_sem=recv_sem,
                                                         device_id=(x, y, 1 - c), device_id_type=MESH_ID)


def rs_sibling(gs, *, name):
    n = len(gs)

    def body(*refs):
        g_refs, out_refs, (send_sems, recv_sems) = refs[:n], refs[n:2 * n], refs[2 * n:]
        c = lax.axis_index("c")
        copies = [(_to_sibling(send_sems.at[i], recv_sems.at[i]), g_ref.at[:, pl.ds(1 - c, 1)], out_ref)
                  for i, (g_ref, out_ref) in enumerate(zip(g_refs, out_refs))]
        for remote, src, dst in copies:
            _start_pieces(remote, src, dst)
        for remote, src, dst in copies:
            remote(src, dst).wait()

    return pl.pallas_call(
        body, name=name, out_shape=[_sds((g.shape[0], 1) + g.shape[2:], g.dtype) for g in gs],
        in_specs=[ANY] * n, out_specs=[ANY] * n, scratch_shapes=[pltpu.SemaphoreType.DMA((n,)), pltpu.SemaphoreType.DMA((n,))],
        compiler_params=pltpu.CompilerParams(has_side_effects=True),
    )(*gs)


HBM_SPEC = pl.BlockSpec(memory_space=pltpu.HBM)
SEM_SPEC = pl.BlockSpec(memory_space=pltpu.SEMAPHORE)
DATAFLOW = pltpu.SideEffectType.DATAFLOW_SIDE_EFFECTING


def _hbm(a):
    return pltpu.with_memory_space_constraint(a, pltpu.HBM)


def _split_start(srcs, land_shapes, starts, *, name):
    n = len(srcs)

    def body(*refs):
        src_refs, land_refs, (send_sem, recv_sem), token = refs[:n], refs[n:2 * n], refs[2 * n:2 * n + 2], refs[-1]
        starts(src_refs, land_refs, send_sem, recv_sem)
        token[...] = jnp.zeros_like(token)

    out = pl.pallas_call(
        body, name=name,
        out_shape=[pltpu.SemaphoreType.DMA(()), pltpu.SemaphoreType.DMA(())] + [pltpu.HBM(s.shape, s.dtype) for s in srcs]
        + [pltpu.HBM(shape, s.dtype) for shape, s in zip(land_shapes, srcs)] + [_sds((8, LANE), F32)],
        in_specs=[HBM_SPEC] * (2 * n), out_specs=[SEM_SPEC, SEM_SPEC] + [HBM_SPEC] * (2 * n) + [pl.BlockSpec(memory_space=pltpu.VMEM)],
        input_output_aliases={i: 2 + i for i in range(2 * n)},
        compiler_params=pltpu.CompilerParams(has_side_effects=DATAFLOW),
    )(*[_hbm(s) for s in srcs], *[_hbm(lax.empty(shape, s.dtype)) for shape, s in zip(land_shapes, srcs)])
    return out[0], out[1], out[2:2 + n], out[2 + n:2 + 2 * n], out[-1]


def _split_wait(handle, after, sent, landed, *, name):
    send_sem, recv_sem, srcs, lands, _ = handle
    n = len(srcs)

    def body(*refs):
        src_refs, land_refs, (send_sem, recv_sem) = refs[:n], refs[n:2 * n], refs[2 * n:2 * n + 2]
        x, y, c = _place()
        for sized, wait in ((sent, "wait_send"), (landed, "wait_recv")):
            for src_ref, land_ref in zip(src_refs, land_refs):
                ref = sized(src_ref, land_ref)
                getattr(pltpu.make_async_remote_copy(src_ref=ref, dst_ref=ref, send_sem=send_sem, recv_sem=recv_sem,
                                                     device_id=(x, y, c), device_id_type=MESH_ID), wait)()

    out = pl.pallas_call(
        body, name=name, out_shape=[pltpu.HBM(a.shape, a.dtype) for a in (*srcs, *lands)],
        in_specs=[HBM_SPEC] * (2 * n) + [SEM_SPEC, SEM_SPEC, ANY], out_specs=[HBM_SPEC] * (2 * n),
        input_output_aliases={i: i for i in range(2 * n)},
        compiler_params=pltpu.CompilerParams(has_side_effects=DATAFLOW),
    )(*srcs, *lands, send_sem, recv_sem, after)
    return out[:n], out[n:]


def ag_send_start(blks, *, name):
    def starts(src_refs, land_refs, send_sem, recv_sem):
        x, y, c = _place()
        me = 4 * x + 2 * y + c
        for to in ((x, y, 1 - c), (1 - x, y, c), (x, 1 - y, c), (1 - x, 1 - y, c)):
            for src_ref, land_ref in zip(src_refs, land_refs):
                pltpu.make_async_remote_copy(src_ref=src_ref, dst_ref=land_ref.at[me], send_sem=send_sem, recv_sem=recv_sem,
                                             device_id=to, device_id_type=MESH_ID).start()

    return _split_start(blks, [(N_DEV,) + b.shape for b in blks], starts, name=name)


def ag_send_wait(handle, after, *, name):
    four = lambda src_ref, land_ref: land_ref.at[pl.ds(0, 4)]
    return _split_wait(handle, after, four, four, name=name)[1]


def ag_forward(lands, sibling_blks, *, name):
    n = len(lands)

    def body(*refs):
        land_refs, blk_refs, out_refs, (send_sems, recv_sems) = refs[:n], refs[n:2 * n], refs[2 * n:3 * n], refs[3 * n:]
        x, y, c = _place()
        for i, (land_ref, blk_ref, out_ref) in enumerate(zip(land_refs, blk_refs, out_refs)):
            remote = _to_sibling(send_sems.at[i], recv_sems.at[i])
            for r in (1, 2, 3):
                px, py = _flip(r, x, y)
                slot = 4 * px + 2 * py + c
                _start_pieces(remote, land_ref.at[slot], out_ref.at[slot])
            _start_pieces(remote, blk_ref, out_ref.at[4 * x + 2 * y + 1 - c])
        for i, out_ref in enumerate(out_refs):
            four = out_ref.at[pl.ds(0, 4)]
            _to_sibling(send_sems.at[i], recv_sems.at[i])(four, four).wait()

    return pl.pallas_call(
        body, name=name, out_shape=[_sds(a.shape, a.dtype) for a in lands], in_specs=[ANY] * (2 * n), out_specs=[ANY] * n,
        scratch_shapes=[pltpu.SemaphoreType.DMA((n,)), pltpu.SemaphoreType.DMA((n,))], input_output_aliases={i: i for i in range(n)},
        compiler_params=pltpu.CompilerParams(has_side_effects=True),
    )(*lands, *sibling_blks)


def rs_chips_start(hs, *, name):
    def starts(src_refs, land_refs, send_sem, recv_sem):
        x, y, c = _place()
        for r in (1, 2, 3):
            px, py = _flip(r, x, y)
            for src_ref, land_ref in zip(src_refs, land_refs):
                pltpu.make_async_remote_copy(src_ref=src_ref.at[2 * px + py], dst_ref=land_ref.at[r - 1], send_sem=send_sem,
                                             recv_sem=recv_sem, device_id=(px, py, c), device_id_type=MESH_ID).start()

    return _split_start(hs, [(3,) + h.shape[1:] for h in hs], starts, name=name)


def rs_chips_wait(handle, after, *, name):
    return _split_wait(handle, after, lambda src_ref, land_ref: src_ref.at[pl.ds(0, 3)], lambda src_ref, land_ref: land_ref, name=name)


def _row_block(kd, nd):
    return _pick(kd, (max(32, (1 << 19) // nd // 32 * 32),))


def add_kept_half(g, recv, core, *, name):
    nchip, nl, kd, nd = g.shape
    lh = nl // 2
    tk = _row_block(kd, nd)

    def body(c_ref, g_ref, r_ref, o_ref):
        del c_ref
        o_ref[...] = (g_ref[...].astype(F32) + r_ref[...].astype(F32)).astype(o_ref.dtype)

    blk = lambda f: pl.BlockSpec((None, None, tk, nd), f)
    return pl.pallas_call(
        body, name=name, out_shape=_sds((nchip, lh, kd, nd), BF16),
        grid_spec=pltpu.PrefetchScalarGridSpec(
            num_scalar_prefetch=1, grid=(nchip, lh, kd // tk),
            in_specs=[blk(lambda j, l, i, c_ref: (j, c_ref[0] * lh + l, i, 0)), blk(lambda j, l, i, c_ref: (j, l, i, 0))],
            out_specs=blk(lambda j, l, i, c_ref: (j, l, i, 0))),
        compiler_params=_cp(("parallel", "parallel", "parallel")),
    )(core, g, recv)


def add_chip_parts(h, parts, chip, core, grad, layer, *, name):
    _, _, kh, nd = h.shape
    tk = _row_block(kh, nd)
    nkb = kh // tk

    def body(chip_ref, core_ref, h_ref, p0, p1, p2, g_in, o_ref):
        del chip_ref, core_ref, g_in
        o_ref[...] = h_ref[...].astype(F32) + p0[...].astype(F32) + p1[...].astype(F32) + p2[...].astype(F32)

    blk = lambda f: pl.BlockSpec((None, None, tk, nd), f)
    part = lambda r: blk(functools.partial(lambda r_, i, chip_ref, core_ref: (r_, 0, i, 0), r))
    return pl.pallas_call(
        body, name=name, out_shape=_sds(grad.shape, grad.dtype),
        grid_spec=pltpu.PrefetchScalarGridSpec(
            num_scalar_prefetch=2, grid=(nkb,),
            in_specs=[blk(lambda i, chip_ref, core_ref: (chip_ref[0], 0, i, 0)), part(0), part(1), part(2), ANY],
            out_specs=pl.BlockSpec((None, tk, nd), lambda i, chip_ref, core_ref: (layer, core_ref[0] * nkb + i, 0))),
        input_output_aliases={6: 0},
        compiler_params=_cp(("parallel",)),
    )(chip, core, h, parts, parts, parts, grad)


def sibling_fill(grads, layer, *, name):
    n = len(grads)

    def body(*refs):
        g_refs, out_refs, (send_sems, recv_sems) = refs[:n], refs[n:2 * n], refs[2 * n:]
        c = lax.axis_index("c")
        copies = []
        for i, (g_ref, out_ref) in enumerate(zip(g_refs, out_refs)):
            kh = g_ref.shape[1] // 2
            copies.append((_to_sibling(send_sems.at[i], recv_sems.at[i]), g_ref.at[layer, pl.ds(c * kh, kh)],
                           out_ref.at[layer, pl.ds(c * kh, kh)]))
        for remote, src, dst in copies:
            _start_pieces(remote, src, dst)
        for remote, src, dst in copies:
            remote(src, dst).wait()

    return pl.pallas_call(
        body, name=name, out_shape=[_sds(g.shape, g.dtype) for g in grads], in_specs=[ANY] * n, out_specs=[ANY] * n,
        scratch_shapes=[pltpu.SemaphoreType.DMA((n,)), pltpu.SemaphoreType.DMA((n,))],
        input_output_aliases={i: i for i in range(n)},
        compiler_params=pltpu.CompilerParams(has_side_effects=True),
    )(*grads)


def sum_slots(a, out_dtype, *, name):
    n = a.shape[0]
    cols = a.shape[-1]
    a3 = a.reshape(n, -1, cols)
    rows = a3.shape[1]
    tr = _pick(rows, (max(32, (1 << 19) // cols // 32 * 32),))

    def body(*refs):
        acc = refs[0][...].astype(F32)
        for r in refs[1:n]:
            acc = acc + r[...].astype(F32)
        refs[n][...] = acc.astype(out_dtype)

    return pl.pallas_call(
        body, name=name, grid=(rows // tr,),
        in_specs=[pl.BlockSpec((None, tr, cols), functools.partial(lambda j, i: (j, i, 0), j)) for j in range(n)],
        out_specs=pl.BlockSpec((tr, cols), lambda i: (i, 0)),
        out_shape=_sds((rows, cols), out_dtype), compiler_params=_cp(("parallel",)),
    )(*([a3] * n)).reshape(a.shape[1:])


def unit_blocks(shards, ci):
    return [lax.dynamic_index_in_dim(w.reshape(2, w.shape[0] // 2, w.shape[1]), ci, axis=0, keepdims=False).astype(BF16)
            for w in shards]


def gather_finish(lands, sibling_blks, tag):
    full = ag_forward(lands, sibling_blks, name=f"{tag}_fwd")
    return [a.reshape(N_CHIP, 1, 2 * a.shape[1], a.shape[2]) for a in full]


def reduce_scatter_start(gs, tag):
    core = jnp.reshape(lax.axis_index("c"), (1,)).astype(jnp.int32)
    halves = [g.reshape(N_CHIP, 2, g.shape[1] // 2, g.shape[2]) for g in gs]
    recv = rs_sibling(halves, name=f"{tag}_rs1")
    chip_sums = [add_kept_half(h, r, core, name=f"{tag}_add1_{j}") for j, (h, r) in enumerate(zip(halves, recv))]
    return (rs_chips_start(chip_sums, name=f"{tag}_rs2_start"),)


def reduce_scatter_finish(pending, after, grads, layer, tag):
    _, handle = pending
    xi, yi, ci = _place()
    chip = jnp.reshape(2 * xi + yi, (1,)).astype(jnp.int32)
    core = jnp.reshape(ci, (1,)).astype(jnp.int32)
    chip_sums, parts = rs_chips_wait(handle, after, name=f"{tag}_rs2_wait")
    grads = [add_chip_parts(h, p, chip, core, g, layer, name=f"{tag}_add2_{j}")
             for j, (h, p, g) in enumerate(zip(chip_sums, parts, grads))]
    return sibling_fill(grads, layer, name=f"{tag}_rs3")


def mod_fwd(c16, w_mod, *, name):
    nl, d, ns = w_mod.shape
    tn = _pick(ns, (512,))

    def body(c_ref, w_ref, o_ref):
        o_ref[...] = jnp.dot(jax.nn.silu(c_ref[...]), w_ref[...], precision=HI, preferred_element_type=F32)

    return pl.pallas_call(
        body, name=name, grid=(nl, ns // tn),
        in_specs=[pl.BlockSpec((16, d), lambda l, j: (0, 0)), pl.BlockSpec((None, d, tn), lambda l, j: (l, 0, j))],
        out_specs=pl.BlockSpec((None, 16, tn), lambda l, j: (l, 0, j)),
        out_shape=_sds((nl, 16, ns), F32), compiler_params=_cp(("parallel", "parallel")),
    )(c16, w_mod)


def mod_bwd_w(c16, dm, *, name):
    nl, _, ns = dm.shape
    d = c16.shape[1]
    tn = _pick(ns, (512,))

    def body(c_ref, dm_ref, o_ref):
        o_ref[...] = lax.dot_general(jax.nn.silu(c_ref[...]), dm_ref[...], (((0,), (0,)), ((), ())), precision=HI,
                                     preferred_element_type=F32)

    return pl.pallas_call(
        body, name=name, grid=(nl, ns // tn),
        in_specs=[pl.BlockSpec((16, d), lambda l, j: (0, 0)), pl.BlockSpec((None, 16, tn), lambda l, j: (l, 0, j))],
        out_specs=pl.BlockSpec((None, d, tn), lambda l, j: (l, 0, j)),
        out_shape=_sds((nl, d, ns), F32), compiler_params=_cp(("parallel", "parallel")),
    )(c16, dm)


def mod_bwd_s(dm, w_mod, *, name):
    nl, d, ns = w_mod.shape
    td = _pick(d, (512,))

    def body(dm_ref, w_ref, o_ref):
        part = lax.dot_general(dm_ref[...], w_ref[...], (((1,), (1,)), ((), ())), precision=HI, preferred_element_type=F32)
        rowsum = jnp.sum(part[8:16], axis=0, keepdims=True)

        @pl.when(pl.program_id(1) == 0)
        def _():
            o_ref[...] = jnp.zeros_like(o_ref)

        o_ref[...] += jnp.broadcast_to(rowsum, o_ref.shape)

    return pl.pallas_call(
        body, name=name, grid=(d // td, nl),
        in_specs=[pl.BlockSpec((None, 16, ns), lambda i, l: (l, 0, 0)), pl.BlockSpec((None, td, ns), lambda i, l: (l, i, 0))],
        out_specs=pl.BlockSpec((8, td), lambda i, l: (0, i)),
        out_shape=_sds((8, d), F32), compiler_params=_cp(("parallel", "arbitrary")),
    )(dm, w_mod)


def colsum16(dm, *, name):
    nl, _, n = dm.shape
    tn = _pick(n, (2048,))

    def body(dm_ref, o_ref):
        o_ref[...] = jnp.broadcast_to(jnp.sum(dm_ref[...], axis=0, keepdims=True), o_ref.shape)

    return pl.pallas_call(
        body, name=name, grid=(nl, n // tn),
        in_specs=[pl.BlockSpec((None, 16, tn), lambda l, j: (l, 0, j))],
        out_specs=pl.BlockSpec((None, 8, tn), lambda l, j: (l, 0, j)),
        out_shape=_sds((nl, 8, n), F32), compiler_params=_cp(("parallel", "parallel")),
    )(dm)


def silu_grad_mul(g, c, *, name):
    def body(g_ref, c_ref, o_ref):
        _, vjp = jax.vjp(jax.nn.silu, c_ref[...])
        o_ref[...] = vjp(g_ref[...])[0]

    return pl.pallas_call(body, name=name, out_shape=_sds(g.shape, F32))(g, c)


def adamw(w, g, m, v, *, name, copy_grad=False):
    shape = w.shape
    cols = shape[-1] if len(shape) > 1 else LANE
    flat = [a.reshape(-1, cols) for a in (w, g, m, v)]
    rows = flat[0].shape[0]
    tr = _pick(rows, (max(8, (1 << 18) // cols // 8 * 8),)) if rows % 8 == 0 else rows
    c1 = 1.0 - ADAM_B1 ** ADAM_STEP
    c2 = 1.0 - ADAM_B2 ** ADAM_STEP

    n_out = 4 if copy_grad else 3

    def body(w_ref, g_ref, m_ref, v_ref, d_ref, nm_ref, nv_ref, *g_out):
        gv = g_ref[...]
        nm = ADAM_B1 * m_ref[...] + (1.0 - ADAM_B1) * gv
        nv = ADAM_B2 * v_ref[...] + (1.0 - ADAM_B2) * (gv * gv)
        d_ref[...] = -ADAM_LR * ((nm / c1) / (jnp.sqrt(nv / c2) + ADAM_EPS) + ADAM_WD * w_ref[...])
        nm_ref[...] = nm
        nv_ref[...] = nv
        if copy_grad:
            g_out[0][...] = gv

    blk = pl.BlockSpec((tr, cols), lambda i: (i, 0))
    outs = pl.pallas_call(
        body, name=name, grid=(rows // tr,), in_specs=[blk] * 4, out_specs=(blk,) * n_out,
        out_shape=(_sds((rows, cols), F32),) * n_out, compiler_params=_cp(("parallel",)),
    )(*flat)
    return tuple(o.reshape(shape) for o in outs)


PACK_ELEMS = LANE * LANE


def _pack(arrs):
    flat = jnp.concatenate([a.reshape(-1).astype(F32) for a in arrs])
    return jnp.pad(flat, (0, (-flat.shape[0]) % PACK_ELEMS)).reshape(-1, LANE)


def _unpack(packed, shapes):
    flat = packed.reshape(-1)
    out, pos = [], 0
    for s in shapes:
        n = math.prod(s)
        out.append(flat[pos:pos + n].reshape(s))
        pos += n
    return out


def _chip_cols(a, chip, width):
    return lax.dynamic_slice_in_dim(a, chip * width, width, axis=a.ndim - 1)


def kernel(x, c, ctx, c_ctx, w_mod, b_mod, norm_w, w_ffn_in, w_ffn_out, e_w_in, e_conv_w, e_conv_b, e_dt_bias, e_a_log, e_d_skip, e_ssd_norm_w, e_sgu_w, e_sgu_b, e_w_out, o_w_qkv, o_sink, o_w_out, loss_target, m_c_ctx, m_w_mod, m_b_mod, m_norm_w, m_w_ffn_in, m_w_ffn_out, m_e_w_in, m_e_conv_w, m_e_conv_b, m_e_dt_bias, m_e_a_log, m_e_d_skip, m_e_ssd_norm_w, m_e_sgu_w, m_e_sgu_b, m_e_w_out, m_o_w_qkv, m_o_sink, m_o_w_out, v_c_ctx, v_w_mod, v_b_mod, v_norm_w, v_w_ffn_in, v_w_ffn_out, v_e_w_in, v_e_conv_w, v_e_conv_b, v_e_dt_bias, v_e_a_log, v_e_d_skip, v_e_ssd_norm_w, v_e_sgu_w, v_e_sgu_b, v_e_w_out, v_o_w_qkv, v_o_sink, v_o_w_out):
    xi, yi, ci = _place()
    chip = 2 * xi + yi
    me = 2 * chip + ci
    s, d = x.shape[1:]
    tc = ctx.shape[1]
    depth = w_mod.shape[0]
    n_even = e_w_in.shape[0]
    dq = norm_w.shape[-1]
    cq = e_conv_w.shape[-1]
    ns = w_mod.shape[-1]

    gath = allgather8(_pack([c, norm_w, e_conv_w]), name="ag_small").reshape(N_DEV, -1)
    c_all = gath[:, :d]
    per_chip = [_unpack(gath[2 * k, d:], [norm_w.shape, e_conv_w.shape]) for k in range(N_CHIP)]
    nw_full = jnp.concatenate([pc[0] for pc in per_chip], axis=-1)
    convw_full = jnp.concatenate([pc[1] for pc in per_chip], axis=-1)
    c16 = jnp.concatenate([c_all, jnp.broadcast_to(c_ctx[None], (8, d))], axis=0)

    mod_g = allgather8(mod_fwd(c16, w_mod, name="mod_fwd"), name="ag_mod")
    mod_all = jnp.concatenate([mod_g[2 * k] for k in range(N_CHIP)], axis=-1) + b_mod[:, None, :]
    mod_rows = jnp.stack([mod_all[:, 8], lax.dynamic_index_in_dim(mod_all, me, axis=1, keepdims=False)], axis=1)
    modtab = jnp.pad(mod_rows.reshape(depth, 2, 6, d), ((0, 0), (0, 0), (0, 2), (0, 0)))

    eps_ = [even_params(convw_full[i], e_conv_b[i], e_dt_bias[i], e_a_log[i], e_d_skip[i], e_ssd_norm_w[i], e_sgu_w[i], e_sgu_b[i])
            for i in range(n_even)]
    sinks = [sink_rows(o_sink[i]) for i in range(o_sink.shape[0])]
    cos, sin = rope_tables(tc, s)
    units = [(kind, l) for l in range(depth) for kind in ("mix", "ffn")]

    def unit_shards(kind, l):
        if kind == "ffn":
            return [w_ffn_in[l], w_ffn_out[l]]
        return [e_w_in[l // 2], e_w_out[l // 2]] if l % 2 == 0 else [o_w_qkv[l // 2], o_w_out[l // 2]]

    def unit_weights(kind, l, gathered):
        w_a, w_b = gathered
        if kind == "mix" and l % 2 == 0:
            w_a = even_cols_permute(jnp.moveaxis(w_a[:, 0], 0, 1).reshape(1, d, -1))[None]
        return w_a, w_b

    def unit_fwd(kind, l, u_in, mt, wts):
        nw = nw_full[l]
        w_a, w_b = wts
        if kind == "mix":
            h1, h1t = norm_mod_fwd(u_in, nw[0], mt, tc, 0, name=f"L{l}_norm1")
            if l % 2 == 0:
                o, ms = even_mixer_fwd(h1, w_a, w_b, 0, eps_[l // 2], tc, f"L{l}_mix")
            else:
                o, ms = odd_mixer_fwd(h1, w_a, w_b, 0, sinks[l // 2], cos, sin, tc, f"L{l}_mix")
            return resid_fwd(u_in, o, nw[1], mt, tc, 0, name=f"L{l}_res1"), (u_in, h1t, ms, o)
        h2, h2t = norm_mod_fwd(u_in, nw[2], mt, tc, 1, name=f"L{l}_norm2")
        p = mm_nn(h2, w_a, 0, "n", name=f"L{l}_ffn_in", out_dtype=BF16)
        a, at = swiglu_fwd(p, name=f"L{l}_swiglu")
        f = mm_nn(a, w_b, 0, "k", name=f"L{l}_ffn_out")
        return resid_fwd(u_in, f, nw[3], mt, tc, 1, name=f"L{l}_res2"), (u_in, h2t, p, at, f)

    def unit_bwd(kind, l, du_out, mt, wts, sv):
        nw = nw_full[l]
        w_a, w_b = wts
        if kind == "ffn":
            u1, h2t, p, at, f = sv
            df, acc_r = resid_bwd(f, nw[3], mt, du_out, tc, 1, name=f"L{l}_res2_b")
            g_b = mm_tn(at, df, w_b, 0, "k", name=f"L{l}_ffn_out_dw", x_is_transposed=True)
            da = mm_nt(df, w_b, 0, "k", name=f"L{l}_ffn_out_dx", out_dtype=BF16)
            dp = swiglu_bwd(p, da, name=f"L{l}_swiglu_b")
            dh2 = mm_nt(dp, w_a, 0, "n", name=f"L{l}_ffn_in_dx")
            g_a = mm_tn(h2t, dp, w_a, 0, "n", name=f"L{l}_ffn_in_dw", x_is_transposed=True)
            du_in, acc_n = norm_mod_bwd(u1, nw[2], mt, dh2, du_out, tc, 1, name=f"L{l}_norm2_b")
            return du_in, [g_a[:, 0], g_b[:, 0]], (acc_n, acc_r), None
        u0, h1t, ms, o = sv
        do, acc_r = resid_bwd(o, nw[1], mt, du_out, tc, 0, name=f"L{l}_res1_b")
        if l % 2 == 0:
            dh1, g_a, g_b, small = even_mixer_bwd(ms, do, h1t, w_a, w_b, w_a, w_b, 0, eps_[l // 2], tc, f"L{l}_mix")
            g_a = jnp.moveaxis(even_cols_unpermute(g_a[0, 0]).reshape(d, N_CHIP, -1), 1, 0)
        else:
            dh1, g_a, g_b, small = odd_mixer_bwd(ms, do, h1t, w_a, w_b, w_a, w_b, 0, sinks[l // 2], cos, sin, tc,
                                                 f"L{l}_mix")
            g_a = g_a[:, 0]
        du_in, acc_n = norm_mod_bwd(u0, nw[0], mt, dh1, du_out, tc, 0, name=f"L{l}_norm1_b")
        return du_in, [g_a, g_b[:, 0]], (acc_n, acc_r), small

    u = jnp.concatenate([ctx[0], x[0]], axis=0)
    shards = unit_shards(*units[0])
    handle = ag_send_start(unit_blocks(shards, ci), name="ag0_start")
    lands = ag_send_wait(handle, handle[4], name="ag0_wait")
    wts = [None] * len(units)
    wts[0] = unit_weights(*units[0], gather_finish(lands, unit_blocks(shards, 1 - ci), "ag0"))
    saved = [None] * len(units)
    prev = u
    for i, (kind, l) in enumerate(units):
        tok = 0.0
        if i + 1 < len(units):
            shards = unit_shards(*units[i + 1])
            blks, _ = lax.optimization_barrier((unit_blocks(shards, ci), prev))
            handle = ag_send_start(blks, name=f"ag{i + 1}_start")
            tok = handle[4][0, 0]
        prev = u
        u, saved[i] = unit_fwd(kind, l, u, modtab[l] + tok, wts[i])
        if i + 1 < len(units):
            lands = ag_send_wait(handle, u, name=f"ag{i + 1}_wait")
            wts[i + 1] = unit_weights(*units[i + 1], gather_finish(lands, unit_blocks(shards, 1 - ci), f"ag{i + 1}"))
    loss_part, du = loss_fwd_bwd(u, loss_target[0], tc, name="loss")
    loss = lax.psum(loss_part[0, 0], ("x", "y", "c"))

    accs, smalls = [None] * len(units), [None] * len(units)
    gbuf = {n: lax.empty(w.shape, F32) for n, w in (("w_ffn_in", w_ffn_in), ("w_ffn_out", w_ffn_out), ("e_w_in", e_w_in),
                                                     ("e_w_out", e_w_out), ("o_w_qkv", o_w_qkv), ("o_w_out", o_w_out))}

    def finish(pending, after):
        kind, l = units[pending[0]]
        names, layer = (("w_ffn_in", "w_ffn_out"), l) if kind == "ffn" else (
            (("e_w_in", "e_w_out"), l // 2) if l % 2 == 0 else (("o_w_qkv", "o_w_out"), l // 2))
        done = reduce_scatter_finish(pending, after, [gbuf[n] for n in names], layer, f"rs{pending[0]}")
        gbuf.update(zip(names, done))

    pending = None
    for i in reversed(range(len(units))):
        kind, l = units[i]
        tok = pending[1][4][0, 0] if pending is not None else 0.0
        du, gs, accs[i], smalls[i] = unit_bwd(kind, l, du, modtab[l] + tok, wts[i], saved[i])
        if pending is not None:
            finish(pending, du)
        pending = (i,) + reduce_scatter_start(gs, f"rs{i}")
    grad_x = du[tc:][None]
    d_nw, d_mt = [None] * depth, [None] * depth
    for l in range(depth):
        (acc0, acc1), (acc2, acc3) = accs[2 * l], accs[2 * l + 1]
        d_nw[l] = jnp.stack([acc[0, 0] + acc[1, 0] for acc in (acc0, acc1, acc2, acc3)])
        d_mt[l] = jnp.stack([acc0[:, 1], acc0[:, 2], acc1[:, 1], acc2[:, 1], acc2[:, 2], acc3[:, 1]], axis=1)
    small_e = [smalls[2 * l] for l in range(0, depth, 2)]
    small_o = [smalls[2 * l] for l in range(1, depth, 2)]

    d_mt_all = jnp.stack(d_mt) + pending[1][4][0, 0]
    dmt_g = allgather8(jnp.pad(d_mt_all, ((0, 0), (0, 0), (0, 2), (0, 0))), name="ag_dmod")[:, :, :, :6]
    dm16 = jnp.concatenate([dmt_g[:, :, 1].transpose(1, 0, 2, 3).reshape(depth, N_DEV, 6 * d),
                            dmt_g[:, :, 0].transpose(1, 0, 2, 3).reshape(depth, N_DEV, 6 * d)], axis=1)
    dm_sh = _chip_cols(dm16, chip, ns)
    grad_w_mod = mod_bwd_w(c16, dm_sh, name="mod_bwd_w")
    grad_b_mod = colsum16(dm16, name="mod_bwd_b")[:, 0]
    ds_cc = mod_bwd_s(dm_sh, w_mod, name="mod_bwd_s")[0]

    stack_e = lambda key: jnp.stack([se[key] for se in small_e])
    small_names = ["e_conv_b", "e_dt_bias", "e_a_log", "e_d_skip", "e_ssd_norm_w", "e_sgu_w", "e_sgu_b"]
    small_parts = [jnp.stack(d_nw), stack_e("e_conv_w")] + [stack_e(k) for k in small_names]
    small_parts += [jnp.stack([so["o_sink"] for so in small_o]), 0.5 * ds_cc]
    small_shapes = [a.shape for a in small_parts]
    small_sum = sum_slots(allgather8(_pack(small_parts), name="ag_small_grads"), F32, name="small_grads_sum")
    (g_nw, g_convw, g_convb, g_dtb, g_alog, g_dskip, g_ssdnw, g_sguw, g_sgub, g_sink, g_scc) = _unpack(small_sum, small_shapes)
    grad_c_ctx = silu_grad_mul(jnp.broadcast_to(g_scc[None], (8, d)), jnp.broadcast_to(c_ctx[None], (8, d)), name="c_ctx_grad")[0]
    grads = dict(
        c_ctx=grad_c_ctx, w_mod=grad_w_mod, b_mod=grad_b_mod, norm_w=_chip_cols(g_nw, chip, dq),
        e_conv_w=_chip_cols(g_convw, chip, cq), e_conv_b=g_convb, e_dt_bias=g_dtb.reshape(e_dt_bias.shape),
        e_a_log=g_alog.reshape(e_a_log.shape), e_d_skip=g_dskip, e_ssd_norm_w=g_ssdnw, e_sgu_w=g_sguw, e_sgu_b=g_sgub,
        o_sink=g_sink)

    finish(pending, grad_w_mod)
    grads.update(gbuf)

    weights = dict(c_ctx=c_ctx, w_mod=w_mod, b_mod=b_mod, norm_w=norm_w, w_ffn_in=w_ffn_in, w_ffn_out=w_ffn_out, e_w_in=e_w_in,
                   e_conv_w=e_conv_w, e_conv_b=e_conv_b, e_dt_bias=e_dt_bias, e_a_log=e_a_log, e_d_skip=e_d_skip,
                   e_ssd_norm_w=e_ssd_norm_w, e_sgu_w=e_sgu_w, e_sgu_b=e_sgu_b, e_w_out=e_w_out, o_w_qkv=o_w_qkv, o_sink=o_sink,
                   o_w_out=o_w_out)
    ms_ = dict(c_ctx=m_c_ctx, w_mod=m_w_mod, b_mod=m_b_mod, norm_w=m_norm_w, w_ffn_in=m_w_ffn_in, w_ffn_out=m_w_ffn_out,
               e_w_in=m_e_w_in, e_conv_w=m_e_conv_w, e_conv_b=m_e_conv_b, e_dt_bias=m_e_dt_bias, e_a_log=m_e_a_log,
               e_d_skip=m_e_d_skip, e_ssd_norm_w=m_e_ssd_norm_w, e_sgu_w=m_e_sgu_w, e_sgu_b=m_e_sgu_b, e_w_out=m_e_w_out,
               o_w_qkv=m_o_w_qkv, o_sink=m_o_sink, o_w_out=m_o_w_out)
    vs_ = dict(c_ctx=v_c_ctx, w_mod=v_w_mod, b_mod=v_b_mod, norm_w=v_norm_w, w_ffn_in=v_w_ffn_in, w_ffn_out=v_w_ffn_out,
               e_w_in=v_e_w_in, e_conv_w=v_e_conv_w, e_conv_b=v_e_conv_b, e_dt_bias=v_e_dt_bias, e_a_log=v_e_a_log,
               e_d_skip=v_e_d_skip, e_ssd_norm_w=v_e_ssd_norm_w, e_sgu_w=v_e_sgu_w, e_sgu_b=v_e_sgu_b, e_w_out=v_e_w_out,
               o_w_qkv=v_o_w_qkv, o_sink=v_o_sink, o_w_out=v_o_w_out)
    names = list(weights)
    big = ("w_mod", "w_ffn_in", "w_ffn_out", "e_w_in", "e_w_out", "o_w_qkv", "o_w_out")
    small = [n for n in names if n not in big]
    delta, new_m, new_v = {}, {}, {}
    for n in big:
        if n == "w_mod":
            delta[n], new_m[n], new_v[n] = adamw(weights[n], grads[n], ms_[n], vs_[n], name=f"adamw_{n}")
        else:
            delta[n], new_m[n], new_v[n], grads[n] = adamw(weights[n], grads[n], ms_[n], vs_[n], name=f"adamw_{n}", copy_grad=True)
    packed = adamw(*[_pack([tab[n] for n in small]) for tab in (weights, grads, ms_, vs_)], name="adamw_small")
    shapes = [weights[n].shape for n in small]
    for tab, pk in zip((delta, new_m, new_v), packed):
        for n, val in zip(small, _unpack(pk, shapes)):
            tab[n] = val
    return (loss, grad_x, *[grads[n] for n in names], *[delta[n] for n in names], *[new_m[n] for n in names],
            *[new_v[n] for n in names])
```

```python
import functools
import itertools
import math

import jax
import jax.numpy as jnp
from jax import lax
from jax.experimental import pallas as pl
from jax.experimental.pallas import tpu as pltpu

F32 = jnp.float32
BF16 = jnp.bfloat16
HI = lax.Precision.HIGHEST

NORM_EPS = 1e-6
SSD_HEAD_DIM = 64
SSD_STATE = 128
CHUNK = 128
CONV_K = 5
ATT_HEAD_DIM = 128
ATT_GROUP = 4
ROPE_BASE = 10000.0
GRID_W = 64
NEG_INF = -1e30
ADAM_LR, ADAM_B1, ADAM_B2, ADAM_EPS, ADAM_WD, ADAM_STEP = 0.001, 0.9, 0.999, 1e-08, 0.01, 10

LANE = 128
VMEM_LIMIT = 56 * 1024 * 1024
MESH_ID = pl.DeviceIdType.MESH
N_DEV = 8
N_CHIP = 4


def _cp(sem=None):
    return pltpu.CompilerParams(dimension_semantics=sem, vmem_limit_bytes=VMEM_LIMIT)


def _sds(shape, dtype):
    return jax.ShapeDtypeStruct(tuple(shape), dtype)


def _pick(n, cands):
    for c in cands:
        if n % c == 0:
            return c
    for step in (LANE, 16, 8):
        for c in range(min(n, cands[0]) // step * step, 0, -step):
            if n % c == 0:
                return c
    raise ValueError((n, cands))


def _w_index(blocked, layer, per_block_k, per_block_n):
    def idx(kblk, nblk):
        if blocked == "n":
            return (nblk // per_block_n, layer, kblk, nblk % per_block_n)
        return (kblk // per_block_k, layer, kblk % per_block_k, nblk)
    return idx


def mm_nn(a, w, layer, blocked, *, name, out_dtype=F32, tm=None, tn=None, tk=None):
    m, k_total = a.shape
    cb, _, kd, nd = w.shape
    n_total = nd * cb if blocked == "n" else nd
    assert k_total == (kd if blocked == "n" else kd * cb)
    tm = tm or _pick(m, (1088, 192))
    tn = tn or _pick(nd, (2048, 1408, 768, 512))
    tk = tk or _pick(kd, (2048, 1408, 512))
    nk = k_total // tk
    widx = _w_index(blocked, layer, kd // tk, nd // tn)

    def body(a_ref, w_ref, o_ref, acc_ref):
        kk = pl.program_id(2)
        part = jnp.dot(a_ref[...].astype(BF16), w_ref[...].astype(BF16), preferred_element_type=F32)

        @pl.when(kk == 0)
        def _():
            acc_ref[...] = part

        @pl.when(kk > 0)
        def _():
            acc_ref[...] += part

        @pl.when(kk == nk - 1)
        def _():
            o_ref[...] = acc_ref[...].astype(o_ref.dtype)

    return pl.pallas_call(
        body, name=name, grid=(m // tm, n_total // tn, nk),
        in_specs=[pl.BlockSpec((tm, tk), lambda i, j, k: (i, k)),
                  pl.BlockSpec((None, None, tk, tn), lambda i, j, k: widx(k, j))],
        out_specs=pl.BlockSpec((tm, tn), lambda i, j, k: (i, j)),
        out_shape=_sds((m, n_total), out_dtype),
        scratch_shapes=[pltpu.VMEM((tm, tn), F32)],
        compiler_params=_cp(("parallel", "parallel", "arbitrary")),
    )(a, w)


def mm_nt(dy, w, layer, blocked, *, name, out_dtype=F32, tm=None, tn=None, tk=None):
    m, n_total = dy.shape
    cb, _, kd, nd = w.shape
    k_total = kd if blocked == "n" else kd * cb
    assert n_total == (nd * cb if blocked == "n" else nd)
    tm = tm or _pick(m, (1088, 192))
    tn = tn or _pick(kd, (2048, 1408, 512))
    tk = tk or _pick(nd, (1408, 1024, 768))
    nk = n_total // tk
    widx = _w_index(blocked, layer, kd // tn, nd // tk)

    def body(a_ref, w_ref, o_ref, acc_ref):
        kk = pl.program_id(2)
        part = lax.dot_general(a_ref[...].astype(BF16), w_ref[...].astype(BF16), (((1,), (1,)), ((), ())),
                               preferred_element_type=F32)

        @pl.when(kk == 0)
        def _():
            acc_ref[...] = part

        @pl.when(kk > 0)
        def _():
            acc_ref[...] += part

        @pl.when(kk == nk - 1)
        def _():
            o_ref[...] = acc_ref[...].astype(o_ref.dtype)

    return pl.pallas_call(
        body, name=name, grid=(m // tm, k_total // tn, nk),
        in_specs=[pl.BlockSpec((tm, tk), lambda i, j, k: (i, k)),
                  pl.BlockSpec((None, None, tn, tk), lambda i, j, k: widx(j, k))],
        out_specs=pl.BlockSpec((tm, tn), lambda i, j, k: (i, j)),
        out_shape=_sds((m, k_total), out_dtype),
        scratch_shapes=[pltpu.VMEM((tm, tn), F32)],
        compiler_params=_cp(("parallel", "parallel", "arbitrary")),
    )(dy, w)


def mm_tn(x, dy, g, layer, blocked, *, name, tm=None, tn=None, tt=None, x_is_transposed=False):
    k_total, t_total = x.shape if x_is_transposed else x.shape[::-1]
    n_total = dy.shape[1]
    cb, nl, kd, nd = g.shape
    assert nl == 1 and layer == 0
    assert k_total == (kd if blocked == "n" else kd * cb) and n_total == (nd * cb if blocked == "n" else nd)
    tm = tm or _pick(kd, (1024, 1408, 512))
    tn = tn or _pick(nd, (1408, 768, 512))
    tt = tt or _pick(t_total, (2176,) if x_is_transposed else (1088, 96))
    nt = t_total // tt
    widx = _w_index(blocked, layer, kd // tm, nd // tn)
    x_spec = pl.BlockSpec((tm, tt), lambda i, j, t: (i, t)) if x_is_transposed else pl.BlockSpec((tt, tm), lambda i, j, t: (t, i))
    x_dim = 1 if x_is_transposed else 0

    def body(x_ref, dy_ref, o_ref, acc_ref):
        tstep = pl.program_id(2)
        part = lax.dot_general(x_ref[...].astype(BF16), dy_ref[...].astype(BF16), (((x_dim,), (0,)), ((), ())),
                               preferred_element_type=F32)

        @pl.when(tstep == 0)
        def _():
            acc_ref[...] = part

        @pl.when(tstep > 0)
        def _():
            acc_ref[...] += part

        @pl.when(tstep == nt - 1)
        def _():
            o_ref[...] = acc_ref[...].astype(o_ref.dtype)

    return pl.pallas_call(
        body, name=name, grid=(k_total // tm, n_total // tn, nt),
        in_specs=[x_spec, pl.BlockSpec((tt, tn), lambda i, j, t: (t, j))],
        out_specs=pl.BlockSpec((None, None, tm, tn), lambda i, j, t: widx(i, j)),
        out_shape=_sds(g.shape, g.dtype),
        scratch_shapes=[pltpu.VMEM((tm, tn), F32)],
        compiler_params=_cp(("parallel", "parallel", "arbitrary")),
    )(x, dy)


def _rms(x, w):
    return x * lax.rsqrt(jnp.mean(x * x, axis=-1, keepdims=True) + NORM_EPS) * w


def _row_tile(tc):
    return 256 if tc % 256 == 0 else 128


def _seg_spec(nct, d):
    return pl.BlockSpec((None, 8, d), lambda i: (jnp.minimum(i // nct, 1), 0, 0))


def _acc_rows(acc_ref, i, nct, rows):
    @pl.when((i == 0) | (i == nct))
    def _():
        acc_ref[...] = jnp.zeros_like(acc_ref)

    for r, val in enumerate(rows):
        acc_ref[r:r + 1, :] += val


def norm_mod_fwd(u, nw, modtab, tc, which, *, name):
    t, d = u.shape
    tr = _row_tile(tc)
    nct = tc // tr
    r0 = 3 * which

    def body(u_ref, nw_ref, mt_ref, h_ref, ht_ref):
        sh, sc = mt_ref[r0:r0 + 1, :], mt_ref[r0 + 1:r0 + 2, :]
        h = _rms(u_ref[...], nw_ref[...]) * (1.0 + sc) + sh
        h_ref[...] = h.astype(h_ref.dtype)
        ht_ref[...] = h.T.astype(ht_ref.dtype)

    return pl.pallas_call(
        body, name=name, grid=(t // tr,),
        in_specs=[pl.BlockSpec((tr, d), lambda i: (i, 0)), pl.BlockSpec((1, d), lambda i: (0, 0)), _seg_spec(nct, d)],
        out_specs=(pl.BlockSpec((tr, d), lambda i: (i, 0)), pl.BlockSpec((d, tr), lambda i: (0, i))),
        out_shape=(_sds((t, d), BF16), _sds((d, t), BF16)), compiler_params=_cp(("arbitrary",)),
    )(u, nw.reshape(1, d), modtab)


def norm_mod_bwd(u, nw, modtab, dh, du_in, tc, which, *, name):
    t, d = u.shape
    tr = _row_tile(tc)
    nct = tc // tr
    r0 = 3 * which

    def body(u_ref, nw_ref, mt_ref, dh_ref, dui_ref, du_ref, acc_ref):
        i = pl.program_id(0)
        sh, sc = mt_ref[r0:r0 + 1, :], mt_ref[r0 + 1:r0 + 2, :]
        _, vjp = jax.vjp(lambda x, w, a, b: _rms(x, w) * (1.0 + b) + a, u_ref[...], nw_ref[...], sh, sc)
        dx, dw, dsh, dsc = vjp(dh_ref[...].astype(F32))
        du_ref[...] = dui_ref[...] + dx
        _acc_rows(acc_ref, i, nct, (dw, dsh, dsc))

    row = pl.BlockSpec((tr, d), lambda i: (i, 0))
    return pl.pallas_call(
        body, name=name, grid=(t // tr,),
        in_specs=[row, pl.BlockSpec((1, d), lambda i: (0, 0)), _seg_spec(nct, d), row, row],
        out_specs=(row, _seg_spec(nct, d)),
        out_shape=(_sds((t, d), F32), _sds((2, 8, d), F32)), compiler_params=_cp(("arbitrary",)),
    )(u, nw.reshape(1, d), modtab, dh, du_in)


def resid_fwd(u, o, nw, modtab, tc, which, *, name):
    t, d = u.shape
    tr = _row_tile(tc)
    nct = tc // tr
    r0 = 3 * which + 2

    def body(u_ref, o_ref, nw_ref, mt_ref, out_ref):
        out_ref[...] = u_ref[...] + mt_ref[r0:r0 + 1, :] * _rms(o_ref[...], nw_ref[...])

    row = pl.BlockSpec((tr, d), lambda i: (i, 0))
    return pl.pallas_call(
        body, name=name, grid=(t // tr,),
        in_specs=[row, row, pl.BlockSpec((1, d), lambda i: (0, 0)), _seg_spec(nct, d)],
        out_specs=row, out_shape=_sds((t, d), F32), compiler_params=_cp(("arbitrary",)),
    )(u, o, nw.reshape(1, d), modtab)


def resid_bwd(o, nw, modtab, du, tc, which, *, name):
    t, d = o.shape
    tr = _row_tile(tc)
    nct = tc // tr
    r0 = 3 * which + 2

    def body(o_ref, nw_ref, mt_ref, du_ref, do_ref, acc_ref):
        i = pl.program_id(0)
        _, vjp = jax.vjp(lambda x, w, g: g * _rms(x, w), o_ref[...], nw_ref[...], mt_ref[r0:r0 + 1, :])
        dx, dw, dg = vjp(du_ref[...])
        do_ref[...] = dx.astype(do_ref.dtype)
        _acc_rows(acc_ref, i, nct, (dw, dg))

    row = pl.BlockSpec((tr, d), lambda i: (i, 0))
    return pl.pallas_call(
        body, name=name, grid=(t // tr,),
        in_specs=[row, pl.BlockSpec((1, d), lambda i: (0, 0)), _seg_spec(nct, d), row],
        out_specs=(row, _seg_spec(nct, d)),
        out_shape=(_sds((t, d), BF16), _sds((2, 8, d), F32)), compiler_params=_cp(("arbitrary",)),
    )(o, nw.reshape(1, d), modtab, du)


def swiglu_fwd(p, *, name):
    t, h2 = p.shape
    h = h2 // 2
    tr = CHUNK

    def body(p_ref, a_ref, at_ref):
        a = jax.nn.silu(p_ref[:, :h].astype(F32)) * p_ref[:, h:].astype(F32)
        a_ref[...] = a.astype(a_ref.dtype)
        at_ref[...] = a.T.astype(at_ref.dtype)

    return pl.pallas_call(
        body, name=name, grid=(t // tr,),
        in_specs=[pl.BlockSpec((tr, h2), lambda i: (i, 0))],
        out_specs=(pl.BlockSpec((tr, h), lambda i: (i, 0)), pl.BlockSpec((h, tr), lambda i: (0, i))),
        out_shape=(_sds((t, h), BF16), _sds((h, t), BF16)), compiler_params=_cp(("parallel",)),
    )(p)


def swiglu_bwd(p, da, *, name):
    t, h2 = p.shape
    h = h2 // 2
    tr = CHUNK

    def body(p_ref, da_ref, dp_ref):
        _, vjp = jax.vjp(lambda g, u: jax.nn.silu(g) * u, p_ref[:, :h].astype(F32), p_ref[:, h:].astype(F32))
        dg, du = vjp(da_ref[...].astype(F32))
        dp_ref[:, :h] = dg.astype(dp_ref.dtype)
        dp_ref[:, h:] = du.astype(dp_ref.dtype)

    return pl.pallas_call(
        body, name=name, grid=(t // tr,),
        in_specs=[pl.BlockSpec((tr, h2), lambda i: (i, 0)), pl.BlockSpec((tr, h), lambda i: (i, 0))],
        out_specs=pl.BlockSpec((tr, h2), lambda i: (i, 0)),
        out_shape=_sds((t, h2), BF16), compiler_params=_cp(("parallel",)),
    )(p, da)


def loss_fwd_bwd(u, target, tc, *, name):
    t, d = u.shape
    tr = _row_tile(tc)
    nct = tc // tr

    def body(u_ref, t_ref, loss_ref, du_ref):
        i = pl.program_id(0)

        @pl.when(i == 0)
        def _():
            loss_ref[...] = jnp.zeros_like(loss_ref)

        @pl.when(i < nct)
        def _():
            du_ref[...] = jnp.zeros_like(du_ref)

        @pl.when(i >= nct)
        def _():
            err = u_ref[...] - t_ref[...]
            du_ref[...] = err * (1.0 / d)
            loss_ref[...] += jnp.sum(jnp.sum(err * err, axis=1, keepdims=True), axis=0, keepdims=True) * (0.5 / d)

    return pl.pallas_call(
        body, name=name, grid=(t // tr,),
        in_specs=[pl.BlockSpec((tr, d), lambda i: (i, 0)), pl.BlockSpec((tr, d), lambda i: (jnp.maximum(i - nct, 0), 0))],
        out_specs=(pl.BlockSpec((1, 1), lambda i: (0, 0)), pl.BlockSpec((tr, d), lambda i: (i, 0))),
        out_shape=(_sds((1, 1), F32), _sds((t, d), F32)), compiler_params=_cp(("arbitrary",)),
    )(u, target)


SSD_INNER = 1024
SGU_WIDTH = 1024
XBC_DIM = 1536
EVEN_COLS = 4640
EVEN_PAD_COLS = 5120
Z_BLK, U_BLK, V_BLK, X_BLK, B_BLK, C_BLK, DT_BLK = 0, 8, 16, 24, 32, 34, 36
PAD_ROWS = 8


def _conv_scratch_fill(pad_ref, val, tc, s):
    pad_ref[...] = jnp.zeros_like(pad_ref)
    pad_ref[PAD_ROWS:PAD_ROWS + tc, :] = val[:tc]
    pad_ref[2 * PAD_ROWS + tc:2 * PAD_ROWS + tc + s, :] = val[tc:]


def _conv_taps(pad_ref, tc, s, k):
    off = k - CONV_K // 2
    return (pad_ref[PAD_ROWS + off:PAD_ROWS + off + tc, :],
            pad_ref[2 * PAD_ROWS + tc + off:2 * PAD_ROWS + tc + off + s, :])


def conv_fwd(p, wb, tc, *, name):
    t = p.shape[0]
    s = t - tc
    nblk = XBC_DIM // LANE

    def body(p_ref, wb_ref, out_ref, pad_ref):
        _conv_scratch_fill(pad_ref, p_ref[...], tc, s)
        acc_c = jnp.zeros((tc, LANE), F32) + wb_ref[5:6, :]
        acc_l = jnp.zeros((s, LANE), F32) + wb_ref[5:6, :]
        for k in range(CONV_K):
            xc, xl = _conv_taps(pad_ref, tc, s, k)
            acc_c += xc * wb_ref[k:k + 1, :]
            acc_l += xl * wb_ref[k:k + 1, :]
        out_ref[:tc, :] = jax.nn.silu(acc_c)
        out_ref[tc:, :] = jax.nn.silu(acc_l)

    return pl.pallas_call(
        body, name=name, grid=(nblk,),
        in_specs=[pl.BlockSpec((t, LANE), lambda j: (0, X_BLK + j)), pl.BlockSpec((8, LANE), lambda j: (0, j))],
        out_specs=pl.BlockSpec((t, LANE), lambda j: (0, j)),
        out_shape=_sds((t, XBC_DIM), F32),
        scratch_shapes=[pltpu.VMEM((t + 3 * PAD_ROWS, LANE), F32)],
        compiler_params=_cp(("parallel",)),
    )(p, wb)


def conv_bwd(p, wb, dxg, dskip, tc, *, name):
    t = p.shape[0]
    s = t - tc
    nblk = XBC_DIM // LANE
    nx = SSD_INNER // LANE

    def grp(j):
        return jnp.where(j < nx, j // 4, (j - nx) % 2)

    def sub(j):
        return jnp.where(j < nx, j % 4, 4 + (j - nx) // 2)

    def body(p_ref, wb_ref, d0_ref, d1_ref, ds_ref, dp_ref, dwb_ref, pad_ref, dpad_ref):
        j = pl.program_id(0)
        _conv_scratch_fill(pad_ref, p_ref[...], tc, s)
        pre = [jnp.zeros((tc, LANE), F32) + wb_ref[5:6, :], jnp.zeros((s, LANE), F32) + wb_ref[5:6, :]]
        for k in range(CONV_K):
            xc, xl = _conv_taps(pad_ref, tc, s, k)
            pre[0] += xc * wb_ref[k:k + 1, :]
            pre[1] += xl * wb_ref[k:k + 1, :]
        dx = d0_ref[...] + d1_ref[...] + jnp.where(j < nx, ds_ref[...], 0.0)
        dpre = []
        for part, rows in ((0, slice(0, tc)), (1, slice(tc, t))):
            sig = jax.nn.sigmoid(pre[part])
            dpre.append(dx[rows] * (sig * (1.0 + pre[part] * (1.0 - sig))))
        dwb_ref[...] = jnp.zeros_like(dwb_ref)
        dwb_ref[5:6, :] = jnp.sum(dpre[0], axis=0, keepdims=True) + jnp.sum(dpre[1], axis=0, keepdims=True)
        for k in range(CONV_K):
            xc, xl = _conv_taps(pad_ref, tc, s, k)
            dwb_ref[k:k + 1, :] = (jnp.sum(dpre[0] * xc, axis=0, keepdims=True)
                                   + jnp.sum(dpre[1] * xl, axis=0, keepdims=True))
        _conv_scratch_fill(dpad_ref, jnp.concatenate(dpre, axis=0), tc, s)
        acc_c = jnp.zeros((tc, LANE), F32)
        acc_l = jnp.zeros((s, LANE), F32)
        for k in range(CONV_K):
            gc, gl = _conv_taps(dpad_ref, tc, s, CONV_K - 1 - k)
            acc_c += gc * wb_ref[k:k + 1, :]
            acc_l += gl * wb_ref[k:k + 1, :]
        dp_ref[:tc, :] = acc_c.astype(dp_ref.dtype)
        dp_ref[tc:, :] = acc_l.astype(dp_ref.dtype)

    col = pl.BlockSpec((t, LANE), lambda j: (0, j))
    return pl.pallas_call(
        body, name=name, grid=(nblk,),
        in_specs=[pl.BlockSpec((t, LANE), lambda j: (0, X_BLK + j)), pl.BlockSpec((8, LANE), lambda j: (0, j)),
                  pl.BlockSpec((None, None, t, LANE), lambda j: (0, grp(j), 0, sub(j))),
                  pl.BlockSpec((None, None, t, LANE), lambda j: (1, grp(j), 0, sub(j))),
                  pl.BlockSpec((t, LANE), lambda j: (0, jnp.minimum(j, nx - 1)))],
        out_specs=(col, pl.BlockSpec((8, LANE), lambda j: (0, j))),
        out_shape=(_sds((t, XBC_DIM), BF16), _sds((8, XBC_DIM), F32)),
        scratch_shapes=[pltpu.VMEM((t + 3 * PAD_ROWS, LANE), F32), pltpu.VMEM((t + 3 * PAD_ROWS, LANE), F32)],
        compiler_params=_cp(("parallel",)),
    )(p, wb, dxg, dxg, dskip)


HEADS_PER_DG = 8


def _ssd_prep_fn(pre, bias, alog, rev):
    q = pre.shape[0]
    lane = lax.broadcasted_iota(jnp.int32, (1, LANE), 1)
    dt = jnp.where(lane < HEADS_PER_DG, jax.nn.softplus(pre + bias), 0.0)
    row = lax.broadcasted_iota(jnp.int32, (q, q), 0)
    col = lax.broadcasted_iota(jnp.int32, (q, q), 1)
    same_chunk = (row // CHUNK) == (col // CHUNK)
    tri = jnp.where(same_chunk & ((col - row) * jnp.where(rev, 1, -1) >= 0), 1.0, 0.0)
    cs = jnp.dot(tri, dt * (-jnp.exp(alog)), precision=HI, preferred_element_type=F32)
    return dt, cs


def _scan_chunk(nc_ctx, nch):
    def idx(dg, i):
        fwd = i
        bwd = jnp.where(i < nc_ctx, nc_ctx - 1 - i, nch - 1 - (i - nc_ctx))
        return jnp.where(dg // 2 == 0, fwd, bwd)
    return idx


def _prep_rows(t):
    return 2 * CHUNK if t % (2 * CHUNK) == 0 else CHUNK


def ssd_prep_fwd(pre, bias, alog, *, name):
    _, t, _ = pre.shape
    rows = _prep_rows(t)
    blk = pl.BlockSpec((None, rows, LANE), lambda dg, i: (dg, i, 0))
    par = pl.BlockSpec((None, 1, LANE), lambda dg, i: (dg, 0, 0))

    def body(pre_ref, b_ref, a_ref, dt_ref, cs_ref):
        dt, cs = _ssd_prep_fn(pre_ref[...], b_ref[...], a_ref[...], pl.program_id(0) // 2 == 1)
        dt_ref[...] = dt
        cs_ref[...] = cs

    return pl.pallas_call(
        body, name=name, grid=(4, t // rows), in_specs=[blk, par, par], out_specs=(blk, blk),
        out_shape=(_sds(pre.shape, F32), _sds(pre.shape, F32)), compiler_params=_cp(("parallel", "parallel")),
    )(pre, bias, alog)


def ssd_prep_bwd(pre, bias, alog, ddt, dcs, *, name):
    _, t, _ = pre.shape
    rows = _prep_rows(t)
    blk = pl.BlockSpec((None, rows, LANE), lambda dg, i: (dg, i, 0))
    par = pl.BlockSpec((None, 1, LANE), lambda dg, i: (dg, 0, 0))

    def body(pre_ref, b_ref, a_ref, ddt_ref, dcs_ref, dpre_ref, acc_ref):
        rev = pl.program_id(0) // 2 == 1
        _, vjp = jax.vjp(lambda x, b, a: _ssd_prep_fn(x, b, a, rev), pre_ref[...], b_ref[...], a_ref[...])
        dpre, db, da = vjp((ddt_ref[...], dcs_ref[...]))
        dpre_ref[...] = dpre

        @pl.when(pl.program_id(1) == 0)
        def _():
            acc_ref[...] = jnp.zeros_like(acc_ref)

        acc_ref[0:1, :] += db
        acc_ref[1:2, :] += da

    return pl.pallas_call(
        body, name=name, grid=(4, t // rows), in_specs=[blk, par, par, blk, blk],
        out_specs=(blk, pl.BlockSpec((None, 8, LANE), lambda dg, i: (dg, 0, 0))),
        out_shape=(_sds(pre.shape, F32), _sds((4, 8, LANE), F32)), compiler_params=_cp(("parallel", "arbitrary")),
    )(pre, bias, alog, ddt, dcs)


def _onehot_col(a, h):
    lane = lax.broadcasted_iota(jnp.int32, (1, a.shape[1]), 1)
    return jnp.sum(jnp.where(lane == h, a, 0.0), axis=1, keepdims=True)


def _onehot_row(a, h):
    sub = lax.broadcasted_iota(jnp.int32, (a.shape[0], 1), 0)
    return jnp.sum(jnp.where(sub == h, a, 0.0), axis=0, keepdims=True)


def _bdot(a, b, dims):
    return lax.dot_general(a.astype(BF16), b.astype(BF16), (dims, ((), ())), preferred_element_type=F32)


def _ssd_pair(xblk, dt, cs, bm, cm, sp, rev, pair):
    q = xblk.shape[0]
    lane = lax.broadcasted_iota(jnp.int32, (1, LANE), 1)
    sub = lax.broadcasted_iota(jnp.int32, (LANE, 1), 0)
    row = lax.broadcasted_iota(jnp.int32, (q, q), 0)
    col = lax.broadcasted_iota(jnp.int32, (q, q), 1)
    mask = (col - row) * jnp.where(rev, 1, -1) >= 0
    last = jnp.where(rev, 0, q - 1)
    cs_t = cs.T
    g = _bdot(cm, bm, ((1,), (1,)))
    y = jnp.zeros((q, LANE), F32)
    escale = jnp.zeros((q, LANE), F32)
    xw = jnp.zeros((q, LANE), F32)
    dec = jnp.zeros((LANE, 1), F32)
    for hh in range(2):
        h = 2 * pair + hh
        c_col = _onehot_col(cs, h)
        c_row = _onehot_row(cs_t, h)
        tot = jnp.sum(jnp.where(lax.broadcasted_iota(jnp.int32, (1, q), 1) == last, c_row, 0.0), axis=1, keepdims=True)
        in_head = (lane >= hh * SSD_HEAD_DIM) & (lane < (hh + 1) * SSD_HEAD_DIM)
        xh = jnp.where(in_head, xblk * _onehot_col(dt, h), 0.0)
        ldec = jnp.where(mask, jnp.exp(jnp.where(mask, c_col - c_row, 0.0)), 0.0)
        y = y + _bdot(g * ldec, xh, ((1,), (0,)))
        escale = escale + jnp.where(in_head, jnp.exp(c_col), 0.0)
        xw = xw + xh * jnp.exp(tot - c_col)
        dec = dec + jnp.where((sub >= hh * SSD_HEAD_DIM) & (sub < (hh + 1) * SSD_HEAD_DIM), jnp.exp(tot), 0.0)
    y = y + _bdot(cm, sp, ((1,), (1,))) * escale
    s_new = sp * dec + _bdot(xw, bm, ((0,), (0,)))
    return y, s_new


def ssd_fwd(xbc, dt, cs, tc, *, name):
    t = xbc.shape[0]
    nch = t // CHUNK
    sidx = _scan_chunk(tc // CHUNK, nch)
    gw = SSD_INNER // 2
    nb = SSD_INNER // LANE

    def body(x_ref, b_ref, c_ref, dt_ref, cs_ref, y_ref, sp_ref, s_ref):
        @pl.when(pl.program_id(1) == 0)
        def _():
            s_ref[...] = jnp.zeros_like(s_ref)

        rev = pl.program_id(0) // 2 == 1
        sp_ref[...] = s_ref[...]
        for p in range(gw // LANE):
            blk = slice(p * LANE, (p + 1) * LANE)
            y, s_new = _ssd_pair(x_ref[:, blk], dt_ref[...], cs_ref[...], b_ref[...], c_ref[...], s_ref[blk, :], rev, p)
            y_ref[:, blk] = y
            s_ref[blk, :] = s_new

    return pl.pallas_call(
        body, name=name, grid=(4, nch),
        in_specs=[pl.BlockSpec((CHUNK, gw), lambda dg, i: (sidx(dg, i), dg % 2)),
                  pl.BlockSpec((CHUNK, LANE), lambda dg, i: (sidx(dg, i), nb + dg % 2)),
                  pl.BlockSpec((CHUNK, LANE), lambda dg, i: (sidx(dg, i), nb + 2 + dg % 2)),
                  pl.BlockSpec((None, CHUNK, LANE), lambda dg, i: (dg, sidx(dg, i), 0)),
                  pl.BlockSpec((None, CHUNK, LANE), lambda dg, i: (dg, sidx(dg, i), 0))],
        out_specs=(pl.BlockSpec((None, CHUNK, gw), lambda dg, i: (dg // 2, sidx(dg, i), dg % 2)),
                   pl.BlockSpec((None, None, gw, SSD_STATE), lambda dg, i: (dg, sidx(dg, i), 0, 0))),
        out_shape=(_sds((2, t, SSD_INNER), F32), _sds((4, nch, gw, SSD_STATE), F32)),
        scratch_shapes=[pltpu.VMEM((gw, SSD_STATE), F32)],
        compiler_params=_cp(("parallel", "arbitrary")),
    )(xbc, xbc, xbc, dt, cs)


def ssd_bwd(xbc, dt, cs, sprev, dy, tc, *, name):
    t = xbc.shape[0]
    nch = t // CHUNK
    fidx = _scan_chunk(tc // CHUNK, nch)
    sidx = lambda dg, i: fidx(dg, nch - 1 - i)
    gw = SSD_INNER // 2
    nb = SSD_INNER // LANE

    def body(x_ref, b_ref, c_ref, dt_ref, cs_ref, sp_ref, dy_ref, dxg_ref, ddt_ref, dcs_ref, ds_ref):
        @pl.when(pl.program_id(1) == 0)
        def _():
            ds_ref[...] = jnp.zeros_like(ds_ref)

        rev = pl.program_id(0) // 2 == 1
        ddt = jnp.zeros((CHUNK, LANE), F32)
        dcs = jnp.zeros((CHUNK, LANE), F32)
        db = jnp.zeros((CHUNK, SSD_STATE), F32)
        dc = jnp.zeros((CHUNK, SSD_STATE), F32)
        for p in range(gw // LANE):
            blk = slice(p * LANE, (p + 1) * LANE)
            _, vjp = jax.vjp(functools.partial(_ssd_pair, rev=rev, pair=p),
                             x_ref[:, blk], dt_ref[...], cs_ref[...], b_ref[...], c_ref[...], sp_ref[blk, :])
            dx, ddt_p, dcs_p, db_p, dc_p, dsp = vjp((dy_ref[:, blk], ds_ref[blk, :]))
            dxg_ref[:, blk] = dx
            ds_ref[blk, :] = dsp
            ddt, dcs, db, dc = ddt + ddt_p, dcs + dcs_p, db + db_p, dc + dc_p
        dxg_ref[:, gw:gw + SSD_STATE] = db
        dxg_ref[:, gw + SSD_STATE:] = dc
        ddt_ref[...] = ddt
        dcs_ref[...] = dcs

    hd = pl.BlockSpec((None, CHUNK, LANE), lambda dg, i: (dg, sidx(dg, i), 0))
    return pl.pallas_call(
        body, name=name, grid=(4, nch),
        in_specs=[pl.BlockSpec((CHUNK, gw), lambda dg, i: (sidx(dg, i), dg % 2)),
                  pl.BlockSpec((CHUNK, LANE), lambda dg, i: (sidx(dg, i), nb + dg % 2)),
                  pl.BlockSpec((CHUNK, LANE), lambda dg, i: (sidx(dg, i), nb + 2 + dg % 2)),
                  hd, hd,
                  pl.BlockSpec((None, None, gw, SSD_STATE), lambda dg, i: (dg, sidx(dg, i), 0, 0)),
                  pl.BlockSpec((CHUNK, gw), lambda dg, i: (sidx(dg, i), dg % 2))],
        out_specs=(pl.BlockSpec((None, None, CHUNK, gw + 2 * SSD_STATE), lambda dg, i: (dg // 2, dg % 2, sidx(dg, i), 0)),
                   hd, hd),
        out_shape=(_sds((2, 2, t, gw + 2 * SSD_STATE), F32), _sds((4, t, LANE), F32), _sds((4, t, LANE), F32)),
        scratch_shapes=[pltpu.VMEM((gw, SSD_STATE), F32)],
        compiler_params=_cp(("parallel", "arbitrary")),
    )(xbc, xbc, xbc, dt, cs, sprev, dy)


def _ssd_finish_fn(y0, y1, xs, z, dskip, nw):
    y = (y0 + y1 + xs * dskip) * jax.nn.silu(z)
    half = y.shape[1] // 2
    first = lax.broadcasted_iota(jnp.int32, (1, y.shape[1]), 1) < half
    sq = y * y
    m0 = jnp.sum(jnp.where(first, sq, 0.0), axis=1, keepdims=True) / half
    m1 = jnp.sum(jnp.where(first, 0.0, sq), axis=1, keepdims=True) / half
    return y * jnp.where(first, lax.rsqrt(m0 + NORM_EPS), lax.rsqrt(m1 + NORM_EPS)) * nw


def ssd_finish_fwd(y, xbc, p, dskip, nw, *, name):
    t = xbc.shape[0]
    tr = CHUNK
    w = SSD_INNER
    row = pl.BlockSpec((tr, w), lambda i: (i, 0))
    par = pl.BlockSpec((1, w), lambda i: (0, 0))

    def body(y0_ref, y1_ref, x_ref, z_ref, ds_ref, nw_ref, o_ref):
        o_ref[...] = _ssd_finish_fn(y0_ref[...], y1_ref[...], x_ref[...], z_ref[...], ds_ref[...], nw_ref[...]).astype(o_ref.dtype)

    return pl.pallas_call(
        body, name=name, grid=(t // tr,),
        in_specs=[pl.BlockSpec((None, tr, w), lambda i: (0, i, 0)), pl.BlockSpec((None, tr, w), lambda i: (1, i, 0)),
                  row, row, par, par],
        out_specs=row, out_shape=_sds((t, w), BF16), compiler_params=_cp(("parallel",)),
    )(y, y, xbc, p, dskip, nw)


def ssd_finish_bwd(y, xbc, p, dskip, nw, dout, *, name):
    t = xbc.shape[0]
    tr = CHUNK
    w = SSD_INNER
    row = pl.BlockSpec((tr, w), lambda i: (i, 0))
    par = pl.BlockSpec((1, w), lambda i: (0, 0))

    def body(y0_ref, y1_ref, x_ref, z_ref, ds_ref, nw_ref, do_ref, dy_ref, dx_ref, dz_ref, acc_ref):
        _, vjp = jax.vjp(_ssd_finish_fn, y0_ref[...], y1_ref[...], x_ref[...], z_ref[...], ds_ref[...], nw_ref[...])
        dy0, _, dx, dz, dds, dnw = vjp(do_ref[...])
        dy_ref[...] = dy0
        dx_ref[...] = dx
        dz_ref[...] = dz.astype(dz_ref.dtype)

        @pl.when(pl.program_id(0) == 0)
        def _():
            acc_ref[...] = jnp.zeros_like(acc_ref)

        acc_ref[0:1, :] += dds
        acc_ref[1:2, :] += dnw

    return pl.pallas_call(
        body, name=name, grid=(t // tr,),
        in_specs=[pl.BlockSpec((None, tr, w), lambda i: (0, i, 0)), pl.BlockSpec((None, tr, w), lambda i: (1, i, 0)),
                  row, row, par, par, row],
        out_specs=(row, row, row, pl.BlockSpec((8, w), lambda i: (0, 0))),
        out_shape=(_sds((t, w), F32), _sds((t, w), F32), _sds((t, w), BF16), _sds((8, w), F32)),
        compiler_params=_cp(("arbitrary",)),
    )(y, y, xbc, p, dskip, nw, dout)


SGU_GROUPS = 8


def _sgu_fn(us, vs, ws, bs):
    n = SGU_GROUPS * LANE
    vf = [jax.nn.gelu(v) for v in vs]
    mu = sum(jnp.sum(v, axis=1, keepdims=True) for v in vf) / n
    var = sum(jnp.sum(jnp.square(v - mu), axis=1, keepdims=True) for v in vf) / n
    rstd = lax.rsqrt(var + NORM_EPS)
    return tuple(jax.nn.gelu(u) * (_bdot(w, (v - mu) * rstd, ((1,), (0,))) + b) for u, v, w, b in zip(us, vf, ws, bs))


def sgu_fwd(p, w, b, *, name):
    t = p.shape[0]
    wd = SGU_WIDTH

    def body(u_ref, v_ref, w_ref, b_ref, o_ref):
        sl = [slice(g * LANE, (g + 1) * LANE) for g in range(SGU_GROUPS)]
        ys = _sgu_fn([u_ref[:, s] for s in sl], [v_ref[:, s] for s in sl], [w_ref[g] for g in range(SGU_GROUPS)],
                     [b_ref[g] for g in range(SGU_GROUPS)])
        for s, yv in zip(sl, ys):
            o_ref[:, s] = yv.astype(o_ref.dtype)

    return pl.pallas_call(
        body, name=name, grid=(t // CHUNK,),
        in_specs=[pl.BlockSpec((CHUNK, wd), lambda i: (i, U_BLK * LANE // wd)), pl.BlockSpec((CHUNK, wd), lambda i: (i, V_BLK * LANE // wd)),
                  pl.BlockSpec((SGU_GROUPS, CHUNK, CHUNK), lambda i: (0, 0, 0)), pl.BlockSpec((SGU_GROUPS, CHUNK, 1), lambda i: (0, 0, 0))],
        out_specs=pl.BlockSpec((CHUNK, wd), lambda i: (i, 0)),
        out_shape=_sds((t, wd), BF16), compiler_params=_cp(("parallel",)),
    )(p, p, w, b)


def sgu_bwd(p, w, b, dout, *, name):
    t = p.shape[0]
    wd = SGU_WIDTH

    def body(u_ref, v_ref, w_ref, b_ref, do_ref, duv_ref, dw_ref, db_ref):
        sl = [slice(g * LANE, (g + 1) * LANE) for g in range(SGU_GROUPS)]
        _, vjp = jax.vjp(_sgu_fn, [u_ref[:, s] for s in sl], [v_ref[:, s] for s in sl],
                         [w_ref[g] for g in range(SGU_GROUPS)], [b_ref[g] for g in range(SGU_GROUPS)])
        dus, dvs, dws, dbs = vjp(tuple(do_ref[:, s] for s in sl))

        @pl.when(pl.program_id(0) == 0)
        def _():
            dw_ref[...] = jnp.zeros_like(dw_ref)
            db_ref[...] = jnp.zeros_like(db_ref)

        for g, s in enumerate(sl):
            duv_ref[:, s] = dus[g].astype(duv_ref.dtype)
            duv_ref[:, slice(wd + g * LANE, wd + (g + 1) * LANE)] = dvs[g].astype(duv_ref.dtype)
            dw_ref[g] += dws[g]
            db_ref[g] += dbs[g]

    wspec = pl.BlockSpec((SGU_GROUPS, CHUNK, CHUNK), lambda i: (0, 0, 0))
    bspec = pl.BlockSpec((SGU_GROUPS, CHUNK, 1), lambda i: (0, 0, 0))
    return pl.pallas_call(
        body, name=name, grid=(t // CHUNK,),
        in_specs=[pl.BlockSpec((CHUNK, wd), lambda i: (i, U_BLK * LANE // wd)), pl.BlockSpec((CHUNK, wd), lambda i: (i, V_BLK * LANE // wd)),
                  wspec, bspec, pl.BlockSpec((CHUNK, wd), lambda i: (i, 1))],
        out_specs=(pl.BlockSpec((CHUNK, 2 * wd), lambda i: (i, 0)), wspec, bspec),
        out_shape=(_sds((t, 2 * wd), BF16), _sds(w.shape, F32), _sds(b.shape, F32)),
        compiler_params=_cp(("arbitrary",)),
    )(p, p, w, b, dout)


def even_cols_permute(w):
    z, xbc, dt, u, v = jnp.split(w, (1024, 2560, 2592, 3616), axis=-1)
    pad = jnp.zeros(w.shape[:-1] + (EVEN_PAD_COLS - EVEN_COLS,), w.dtype)
    return jnp.concatenate([z, u, v, xbc, dt, pad], axis=-1)


def even_cols_unpermute(w):
    z, u, v, xbc, dt = jnp.split(w[..., :EVEN_COLS], (1024, 2048, 3072, 4608), axis=-1)
    return jnp.concatenate([z, xbc, dt, u, v], axis=-1)


def _dt_cols(p):
    t = p.shape[0]
    d = p[:, DT_BLK * LANE:DT_BLK * LANE + 4 * HEADS_PER_DG].reshape(t, 4, HEADS_PER_DG).transpose(1, 0, 2)
    return jnp.pad(d, ((0, 0), (0, 0), (0, LANE - HEADS_PER_DG)))


def _heads_to_lanes(a):
    return jnp.pad(a.reshape(4, 1, HEADS_PER_DG), ((0, 0), (0, 0), (0, LANE - HEADS_PER_DG)))


def even_params(conv_w, conv_b, dt_bias, a_log, d_skip, ssd_nw, sgu_w, sgu_b):
    wb = jnp.concatenate([conv_w, conv_b[None], jnp.zeros((2, XBC_DIM), F32)], axis=0)
    return dict(wb=wb, dtb=_heads_to_lanes(dt_bias), alog=_heads_to_lanes(a_log),
                dskip=jnp.repeat(d_skip, SSD_HEAD_DIM)[None], ssd_nw=ssd_nw[None], sgu_w=sgu_w, sgu_b=sgu_b[..., None])


def even_mixer_fwd(h, w_in, w_out, li, ep, tc, tag):
    p = mm_nn(h, w_in, li, "n", name=f"{tag}_in")
    xbc = conv_fwd(p, ep["wb"], tc, name=f"{tag}_conv")
    pre = _dt_cols(p)
    dt, cs = ssd_prep_fwd(pre, ep["dtb"], ep["alog"], name=f"{tag}_prep")
    y, sprev = ssd_fwd(xbc, dt, cs, tc, name=f"{tag}_ssd")
    yssd = ssd_finish_fwd(y, xbc, p, ep["dskip"], ep["ssd_nw"], name=f"{tag}_fin")
    ysgu = sgu_fwd(p, ep["sgu_w"], ep["sgu_b"], name=f"{tag}_sgu")
    ymix = jnp.concatenate([yssd, ysgu], axis=1)
    o = mm_nn(ymix, w_out, li, "k", name=f"{tag}_out")
    return o, (p, xbc, pre, dt, cs, y, sprev, ymix)


def even_mixer_bwd(saved, do, ht, w_in, w_out, g_in, g_out, li, ep, tc, tag):
    p, xbc, pre, dt, cs, y, sprev, ymix = saved
    t = p.shape[0]
    g_out = mm_tn(ymix, do, g_out, li, "k", name=f"{tag}_out_dw")
    dymix = mm_nt(do, w_out, li, "k", name=f"{tag}_out_dx")
    dy, dxskip, dz, acc_fin = ssd_finish_bwd(y, xbc, p, ep["dskip"], ep["ssd_nw"], dymix, name=f"{tag}_fin_b")
    duv, dsgu_w, dsgu_b = sgu_bwd(p, ep["sgu_w"], ep["sgu_b"], dymix, name=f"{tag}_sgu_b")
    dxg, ddt, dcs = ssd_bwd(xbc, dt, cs, sprev, dy, tc, name=f"{tag}_ssd_b")
    dpre, acc_prep = ssd_prep_bwd(pre, ep["dtb"], ep["alog"], ddt, dcs, name=f"{tag}_prep_b")
    dxbc, dwb = conv_bwd(p, ep["wb"], dxg, dxskip, tc, name=f"{tag}_conv_b")
    ddt_cols = dpre[:, :, :HEADS_PER_DG].transpose(1, 0, 2).reshape(t, 4 * HEADS_PER_DG).astype(BF16)
    ddt_cols = jnp.pad(ddt_cols, ((0, 0), (0, EVEN_PAD_COLS - DT_BLK * LANE - 4 * HEADS_PER_DG)))
    dp = jnp.concatenate([dz, duv, dxbc, ddt_cols], axis=1)
    dh = mm_nt(dp, w_in, li, "n", name=f"{tag}_in_dx")
    g_in = mm_tn(ht, dp, g_in, li, "n", name=f"{tag}_in_dw", x_is_transposed=True)
    small = dict(
        e_conv_w=dwb[:CONV_K], e_conv_b=dwb[CONV_K],
        e_dt_bias=acc_prep[:, 0, :HEADS_PER_DG].reshape(2, 2 * HEADS_PER_DG),
        e_a_log=acc_prep[:, 1, :HEADS_PER_DG].reshape(2, 2 * HEADS_PER_DG),
        e_d_skip=acc_fin[0].reshape(-1, SSD_HEAD_DIM).sum(axis=1), e_ssd_norm_w=acc_fin[1],
        e_sgu_w=dsgu_w, e_sgu_b=dsgu_b[..., 0])
    return dh, g_in, g_out, small


ATT_HEADS = 16
ATT_KV = 4
Q_BLKS, K_BLKS = ATT_HEADS, ATT_KV


def rope_tables(tc, s):
    quarter = ATT_HEAD_DIM // 4
    pos = jnp.arange(s)
    inv = ROPE_BASE ** (-jnp.arange(quarter, dtype=F32) / quarter)
    a_row = (pos // GRID_W).astype(F32)[:, None] * inv
    a_col = (pos % GRID_W).astype(F32)[:, None] * inv
    cos = jnp.concatenate([jnp.cos(a_row)] * 2 + [jnp.cos(a_col)] * 2, axis=1)
    sin = jnp.concatenate([-jnp.sin(a_row), jnp.sin(a_row), -jnp.sin(a_col), jnp.sin(a_col)], axis=1)
    return (jnp.concatenate([jnp.ones((tc, ATT_HEAD_DIM), F32), cos], axis=0),
            jnp.concatenate([jnp.zeros((tc, ATT_HEAD_DIM), F32), sin], axis=0))


def _swap_halves(x):
    lane = lax.broadcasted_iota(jnp.int32, x.shape, 1)
    return jnp.where(lane % 64 < 32, pltpu.roll(x, 96, 1), pltpu.roll(x, 32, 1))


def rope_fwd(p, cos, sin, *, name):
    t = p.shape[0]
    tr = _pick(t, (1088, 640))
    scale = ATT_HEAD_DIM ** -0.5

    def body(p_ref, c_ref, s_ref, o_ref):
        x = p_ref[...]
        r = x * c_ref[...] + _swap_halves(x) * s_ref[...]
        o_ref[...] = (r * jnp.where(pl.program_id(1) < Q_BLKS, scale, 1.0)).astype(o_ref.dtype)

    tab = pl.BlockSpec((tr, LANE), lambda i, j: (i, 0))
    return pl.pallas_call(
        body, name=name, grid=(t // tr, Q_BLKS + K_BLKS),
        in_specs=[pl.BlockSpec((tr, LANE), lambda i, j: (i, j)), tab, tab],
        out_specs=pl.BlockSpec((tr, LANE), lambda i, j: (i, j)),
        out_shape=_sds((t, (Q_BLKS + K_BLKS) * LANE), BF16), compiler_params=_cp(("parallel", "parallel")),
    )(p, cos, sin)


def rope_bwd(dq, dk, dv, cos, sin, *, name):
    t = dq.shape[0]
    tr = _pick(t, (1088, 640))
    scale = ATT_HEAD_DIM ** -0.5

    def body(dq_ref, dk_ref, dv_ref, c_ref, s_ref, o_ref):
        j = pl.program_id(1)

        def unrot(g):
            return g * c_ref[...] + _swap_halves(g * s_ref[...])

        @pl.when(j < Q_BLKS)
        def _():
            o_ref[...] = (unrot(dq_ref[...]) * scale).astype(o_ref.dtype)

        @pl.when((j >= Q_BLKS) & (j < Q_BLKS + K_BLKS))
        def _():
            o_ref[...] = unrot(dk_ref[...]).astype(o_ref.dtype)

        @pl.when(j >= Q_BLKS + K_BLKS)
        def _():
            o_ref[...] = dv_ref[...].astype(o_ref.dtype)

    tab = pl.BlockSpec((tr, LANE), lambda i, j: (i, 0))
    return pl.pallas_call(
        body, name=name, grid=(t // tr, Q_BLKS + 2 * K_BLKS),
        in_specs=[pl.BlockSpec((tr, LANE), lambda i, j: (i, jnp.minimum(j, Q_BLKS - 1))),
                  pl.BlockSpec((None, tr, LANE), lambda i, j: (jnp.clip(j - Q_BLKS, 0, K_BLKS - 1), i, 0)),
                  pl.BlockSpec((None, tr, LANE), lambda i, j: (jnp.clip(j - Q_BLKS - K_BLKS, 0, K_BLKS - 1), i, 0)), tab, tab],
        out_specs=pl.BlockSpec((tr, LANE), lambda i, j: (i, j)),
        out_shape=_sds((t, (Q_BLKS + 2 * K_BLKS) * LANE), BF16), compiler_params=_cp(("parallel", "parallel")),
    )(dq, dk, dv, cos, sin)


def _attn_tile(q4, kp, kc, kn, vp, vc, vn, kx, vx, sinks, is_lat, has_prev, has_next):
    q = kp.shape[0]
    nq = q4.shape[0]
    row = jnp.bitwise_and(lax.broadcasted_iota(jnp.int32, (nq, q), 0), q - 1)
    col = lax.broadcasted_iota(jnp.int32, (nq, q), 1)
    m_prev = (col - row) >= (1 - has_prev) * q
    m_cur = (row - row) >= (1 - is_lat)
    m_next = (row - col) >= (1 - has_next) * q
    lane = lax.broadcasted_iota(jnp.int32, (1, LANE), 1)
    sink = jnp.concatenate([jnp.broadcast_to(jnp.sum(jnp.where(lane == 0, s, 0.0), axis=1, keepdims=True), (q, 1)) for s in sinks],
                           axis=0)
    s_p = jnp.where(m_prev, _bdot(q4, kp, ((1,), (1,))), NEG_INF)
    s_c = jnp.where(m_cur, _bdot(q4, kc, ((1,), (1,))), NEG_INF)
    s_n = jnp.where(m_next, _bdot(q4, kn, ((1,), (1,))), NEG_INF)
    s_x = _bdot(q4, kx, ((1,), (1,)))
    mx = [jnp.max(a, axis=1, keepdims=True) for a in (s_p, s_c, s_n, s_x)]
    m = lax.stop_gradient(jnp.maximum(jnp.maximum(jnp.maximum(mx[0], mx[1]), jnp.maximum(mx[2], mx[3])), sink))
    e = [jnp.exp(a - m) for a in (s_p, s_c, s_n, s_x)]
    inv = 1.0 / (sum(jnp.sum(a, axis=1, keepdims=True) for a in e) + jnp.exp(sink - m))
    return sum(_bdot(a * inv, v, ((1,), (0,))) for a, v in zip(e, (vp, vc, vn, vx)))


def _attn_specs(t, tc):
    nblk = t // CHUNK
    hw = ATT_GROUP * LANE
    kcol = lambda kv: Q_BLKS + kv
    vcol = lambda kv: Q_BLKS + K_BLKS + kv
    prev = lambda n: jnp.maximum(n - 1, 0)
    nxt = lambda n: jnp.minimum(n + 1, nblk - 1)
    blk = lambda rowf, colf: pl.BlockSpec((CHUNK, LANE), lambda kv, n: (rowf(n), colf(kv)))
    same = lambda n: n
    return [pl.BlockSpec((CHUNK, hw), lambda kv, n: (n, kv)),
            blk(prev, kcol), blk(same, kcol), blk(nxt, kcol), blk(prev, vcol), blk(same, vcol), blk(nxt, vcol),
            pl.BlockSpec((tc, LANE), lambda kv, n: (0, kcol(kv))), pl.BlockSpec((tc, LANE), lambda kv, n: (0, vcol(kv))),
            pl.BlockSpec((None, 8, LANE), lambda kv, n: (kv, 0, 0))]


def _attn_args(refs, n, nct, nblk):
    q_ref, kp, kc, kn, vp, vc, vn, kx, vx, sk = refs
    f = lambda r: r[...].astype(F32)
    q4 = _stack_heads(q_ref)
    sinks = [sk[g:g + 1, :] for g in range(ATT_GROUP)]
    flags = ((n >= nct).astype(jnp.int32), (n >= nct + 1).astype(jnp.int32), ((n >= nct) & (n + 1 < nblk)).astype(jnp.int32))
    return (q4, f(kp), f(kc), f(kn), f(vp), f(vc), f(vn), f(kx), f(vx), sinks), flags


def _stack_heads(ref):
    return jnp.concatenate([ref[:, g * LANE:(g + 1) * LANE].astype(F32) for g in range(ATT_GROUP)], axis=0)


def _unstack_heads(ref, val):
    for g in range(ATT_GROUP):
        ref[:, g * LANE:(g + 1) * LANE] = val[g * CHUNK:(g + 1) * CHUNK].astype(ref.dtype)


def attn_fwd(qk, p, sink, tc, *, name):
    t = qk.shape[0]
    nblk, nct = t // CHUNK, tc // CHUNK
    hw = ATT_GROUP * LANE

    def body(*refs):
        o_ref = refs[-1]
        args, flags = _attn_args(refs[:-1], pl.program_id(1), nct, nblk)
        _unstack_heads(o_ref, _attn_tile(*args, *flags))

    return pl.pallas_call(
        body, name=name, grid=(ATT_KV, nblk), in_specs=_attn_specs(t, tc),
        out_specs=pl.BlockSpec((CHUNK, hw), lambda kv, n: (n, kv)),
        out_shape=_sds((t, ATT_HEADS * LANE), BF16), compiler_params=_cp(("parallel", "parallel")),
    )(qk, qk, qk, qk, p, p, p, qk, p, sink)


def attn_bwd(qk, p, sink, do, tc, *, name):
    t = qk.shape[0]
    nblk, nct = t // CHUNK, tc // CHUNK
    hw = ATT_GROUP * LANE

    def body(*refs):
        do_ref, dq_ref, dk_ref, dv_ref, dsk_ref = refs[-5:]
        n = pl.program_id(1)
        args, flags = _attn_args(refs[:-5], n, nct, nblk)
        _, vjp = jax.vjp(lambda *a: _attn_tile(*a, *flags), *args)
        dq4, dkp, dkc, dkn, dvp, dvc, dvn, dkx, dvx, dsinks = vjp(_stack_heads(do_ref))

        @pl.when(n == 0)
        def _():
            dk_ref[...] = jnp.zeros_like(dk_ref)
            dv_ref[...] = jnp.zeros_like(dv_ref)
            dsk_ref[...] = jnp.zeros_like(dsk_ref)

        _unstack_heads(dq_ref, dq4)
        for g in range(ATT_GROUP):
            dsk_ref[g:g + 1, :] += dsinks[g]
        for blk, dkb, dvb in ((jnp.maximum(n - 1, 0), dkp, dvp), (n, dkc, dvc), (jnp.minimum(n + 1, nblk - 1), dkn, dvn)):
            rows = pl.ds(pl.multiple_of(blk * CHUNK, CHUNK), CHUNK)
            dk_ref[rows, :] += dkb
            dv_ref[rows, :] += dvb
        dk_ref[0:tc, :] += dkx
        dv_ref[0:tc, :] += dvx

    kvacc = pl.BlockSpec((None, t, LANE), lambda kv, n: (kv, 0, 0))
    return pl.pallas_call(
        body, name=name, grid=(ATT_KV, nblk),
        in_specs=_attn_specs(t, tc) + [pl.BlockSpec((CHUNK, hw), lambda kv, n: (n, kv))],
        out_specs=(pl.BlockSpec((CHUNK, hw), lambda kv, n: (n, kv)), kvacc, kvacc,
                   pl.BlockSpec((None, 8, LANE), lambda kv, n: (kv, 0, 0))),
        out_shape=(_sds((t, ATT_HEADS * LANE), F32), _sds((ATT_KV, t, LANE), F32), _sds((ATT_KV, t, LANE), F32),
                   _sds((ATT_KV, 8, LANE), F32)),
        compiler_params=_cp(("parallel", "arbitrary")),
    )(qk, qk, qk, qk, p, p, p, qk, p, sink, do)


def sink_rows(sink):
    s = jnp.broadcast_to(sink.reshape(ATT_KV, ATT_GROUP, 1), (ATT_KV, ATT_GROUP, LANE))
    return jnp.pad(s, ((0, 0), (0, 8 - ATT_GROUP), (0, 0)))


def odd_mixer_fwd(h, w_qkv, w_out, li, sink, cos, sin, tc, tag):
    p = mm_nn(h, w_qkv, li, "n", name=f"{tag}_qkv")
    qk = rope_fwd(p, cos, sin, name=f"{tag}_rope")
    att = attn_fwd(qk, p, sink, tc, name=f"{tag}_att")
    o = mm_nn(att, w_out, li, "k", name=f"{tag}_out")
    return o, (p, qk, att)


def odd_mixer_bwd(saved, do, ht, w_qkv, w_out, g_qkv, g_out, li, sink, cos, sin, tc, tag):
    p, qk, att = saved
    g_out = mm_tn(att, do, g_out, li, "k", name=f"{tag}_out_dw")
    datt = mm_nt(do, w_out, li, "k", name=f"{tag}_out_dx")
    dq, dk, dv, dsink = attn_bwd(qk, p, sink, datt, tc, name=f"{tag}_att_b")
    dp = rope_bwd(dq, dk, dv, cos, sin, name=f"{tag}_rope_b")
    dh = mm_nt(dp, w_qkv, li, "n", name=f"{tag}_qkv_dx")
    g_qkv = mm_tn(ht, dp, g_qkv, li, "n", name=f"{tag}_qkv_dw", x_is_transposed=True)
    return dh, g_qkv, g_out, dict(o_sink=dsink[:, :ATT_GROUP, 0].reshape(-1))


ANY = pl.BlockSpec(memory_space=pl.ANY)


def _place():
    return lax.axis_index("x"), lax.axis_index("y"), lax.axis_index("c")


DMA_PIECES = 16


def _pieces(shape):
    if len(shape) < 2:
        return [()]
    lead, k = shape[:-2], shape[-2]
    split = 1
    while math.prod(lead) * split < DMA_PIECES and k % (2 * split) == 0 and (k // (2 * split)) % 16 == 0:
        split *= 2
    rows = k // split
    out = []
    for li in itertools.product(*[range(n) for n in lead]):
        out += [li + (pl.ds(q * rows, rows),) for q in range(split)]
    return out


def _start_pieces(make, src, dst):
    for idx in _pieces(src.shape):
        make(src.at[idx] if idx else src, dst.at[idx] if idx else dst).start()


def allgather8(blk, *, name):
    def body(x_ref, out_ref, send_sems, recv_sems, local_sem):
        x, y, c = _place()
        me, sibling = (x, y, c), (x, y, 1 - c)
        chips = [(1 - x, y), (x, 1 - y), (1 - x, 1 - y)]

        def slot(px, py, pc):
            return out_ref.at[4 * px + 2 * py + pc]

        def remote(k, to):
            return lambda src, dst: pltpu.make_async_remote_copy(
                src_ref=src, dst_ref=dst, send_sem=send_sems.at[k], recv_sem=recv_sems.at[k], device_id=to, device_id_type=MESH_ID)

        def local(src, dst):
            return pltpu.make_async_copy(src, dst, local_sem)

        _start_pieces(local, x_ref, slot(*me))
        _start_pieces(remote(0, sibling), x_ref, slot(*me))
        for j, chip in enumerate(chips):
            remote(1 + j, (*chip, c))(x_ref, slot(*me)).start()
        for j, chip in enumerate(chips):
            blk = slot(*chip, c)
            remote(1 + j, me)(blk, blk).wait_recv()
            _start_pieces(remote(4 + j, sibling), blk, blk)
        remote(0, me)(slot(*sibling), slot(*sibling)).wait_recv()
        for j, chip in enumerate(chips):
            blk = slot(*chip, 1 - c)
            remote(4 + j, me)(blk, blk).wait_recv()
        remote(0, sibling)(x_ref, slot(*me)).wait_send()
        for j, chip in enumerate(chips):
            remote(1 + j, (*chip, c))(x_ref, slot(*me)).wait_send()
            remote(4 + j, sibling)(slot(*chip, c), slot(*chip, c)).wait_send()
        local(x_ref, slot(*me)).wait()

    return pl.pallas_call(
        body, name=name, out_shape=_sds((N_DEV,) + blk.shape, blk.dtype), in_specs=[ANY], out_specs=ANY,
        scratch_shapes=[pltpu.SemaphoreType.DMA((7,)), pltpu.SemaphoreType.DMA((7,)), pltpu.SemaphoreType.DMA],
        compiler_params=pltpu.CompilerParams(has_side_effects=True),
    )(blk)


def _flip(r, xi, yi):
    return (1 - xi if r & 2 else xi), (1 - yi if r & 1 else yi)


def _to_sibling(send_sem, recv_sem):
    x, y, c = _place()
    return lambda src, dst: pltpu.make_async_remote_copy(src_ref=src, dst_ref=dst, send_sem=send_sem, recv_sem=recv_sem,
                                                         device_id=(x, y, 1 - c), device_id_type=MESH_ID)


def rs_sibling(gs, *, name):
    n = len(gs)

    def body(*refs):
        g_refs, out_refs, (send_sems, recv_sems) = refs[:n], refs[n:2 * n], refs[2 * n:]
        c = lax.axis_index("c")
        copies = [(_to_sibling(send_sems.at[i], recv_sems.at[i]), g_ref.at[:, pl.ds(1 - c, 1)], out_ref)
                  for i, (g_ref, out_ref) in enumerate(zip(g_refs, out_refs))]
        for remote, src, dst in copies:
            _start_pieces(remote, src, dst)
        for remote, src, dst in copies:
            remote(src, dst).wait()

    return pl.pallas_call(
        body, name=name, out_shape=[_sds((g.shape[0], 1) + g.shape[2:], g.dtype) for g in gs],
        in_specs=[ANY] * n, out_specs=[ANY] * n, scratch_shapes=[pltpu.SemaphoreType.DMA((n,)), pltpu.SemaphoreType.DMA((n,))],
        compiler_params=pltpu.CompilerParams(has_side_effects=True),
    )(*gs)


HBM_SPEC = pl.BlockSpec(memory_space=pltpu.HBM)
SEM_SPEC = pl.BlockSpec(memory_space=pltpu.SEMAPHORE)
DATAFLOW = pltpu.SideEffectType.DATAFLOW_SIDE_EFFECTING


def _hbm(a):
    return pltpu.with_memory_space_constraint(a, pltpu.HBM)


def _split_start(srcs, land_shapes, starts, *, name):
    n = len(srcs)

    def body(*refs):
        src_refs, land_refs, (send_sem, recv_sem), token = refs[:n], refs[n:2 * n], refs[2 * n:2 * n + 2], refs[-1]
        starts(src_refs, land_refs, send_sem, recv_sem)
        token[...] = jnp.zeros_like(token)

    out = pl.pallas_call(
        body, name=name,
        out_shape=[pltpu.SemaphoreType.DMA(()), pltpu.SemaphoreType.DMA(())] + [pltpu.HBM(s.shape, s.dtype) for s in srcs]
        + [pltpu.HBM(shape, s.dtype) for shape, s in zip(land_shapes, srcs)] + [_sds((8, LANE), F32)],
        in_specs=[HBM_SPEC] * (2 * n), out_specs=[SEM_SPEC, SEM_SPEC] + [HBM_SPEC] * (2 * n) + [pl.BlockSpec(memory_space=pltpu.VMEM)],
        input_output_aliases={i: 2 + i for i in range(2 * n)},
        compiler_params=pltpu.CompilerParams(has_side_effects=DATAFLOW),
    )(*[_hbm(s) for s in srcs], *[_hbm(lax.empty(shape, s.dtype)) for shape, s in zip(land_shapes, srcs)])
    return out[0], out[1], out[2:2 + n], out[2 + n:2 + 2 * n], out[-1]


def _split_wait(handle, after, sent, landed, *, name):
    send_sem, recv_sem, srcs, lands, _ = handle
    n = len(srcs)

    def body(*refs):
        src_refs, land_refs, (send_sem, recv_sem) = refs[:n], refs[n:2 * n], refs[2 * n:2 * n + 2]
        x, y, c = _place()
        for sized, wait in ((sent, "wait_send"), (landed, "wait_recv")):
            for src_ref, land_ref in zip(src_refs, land_refs):
                ref = sized(src_ref, land_ref)
                getattr(pltpu.make_async_remote_copy(src_ref=ref, dst_ref=ref, send_sem=send_sem, recv_sem=recv_sem,
                                                     device_id=(x, y, c), device_id_type=MESH_ID), wait)()

    out = pl.pallas_call(
        body, name=name, out_shape=[pltpu.HBM(a.shape, a.dtype) for a in (*srcs, *lands)],
        in_specs=[HBM_SPEC] * (2 * n) + [SEM_SPEC, SEM_SPEC, ANY], out_specs=[HBM_SPEC] * (2 * n),
        input_output_aliases={i: i for i in range(2 * n)},
        compiler_params=pltpu.CompilerParams(has_side_effects=DATAFLOW),
    )(*srcs, *lands, send_sem, recv_sem, after)
    return out[:n], out[n:]


def ag_send_start(blks, *, name):
    def starts(src_refs, land_refs, send_sem, recv_sem):
        x, y, c = _place()
        me = 4 * x + 2 * y + c
        for to in ((x, y, 1 - c), (1 - x, y, c), (x, 1 - y, c), (1 - x, 1 - y, c)):
            for src_ref, land_ref in zip(src_refs, land_refs):
                pltpu.make_async_remote_copy(src_ref=src_ref, dst_ref=land_ref.at[me], send_sem=send_sem, recv_sem=recv_sem,
                                             device_id=to, device_id_type=MESH_ID).start()

    return _split_start(blks, [(N_DEV,) + b.shape for b in blks], starts, name=name)


def ag_send_wait(handle, after, *, name):
    four = lambda src_ref, land_ref: land_ref.at[pl.ds(0, 4)]
    return _split_wait(handle, after, four, four, name=name)[1]


def ag_forward(lands, sibling_blks, *, name):
    n = len(lands)

    def body(*refs):
        land_refs, blk_refs, out_refs, (send_sems, recv_sems) = refs[:n], refs[n:2 * n], refs[2 * n:3 * n], refs[3 * n:]
        x, y, c = _place()
        for i, (land_ref, blk_ref, out_ref) in enumerate(zip(land_refs, blk_refs, out_refs)):
            remote = _to_sibling(send_sems.at[i], recv_sems.at[i])
            for r in (1, 2, 3):
                px, py = _flip(r, x, y)
                slot = 4 * px + 2 * py + c
                _start_pieces(remote, land_ref.at[slot], out_ref.at[slot])
            _start_pieces(remote, blk_ref, out_ref.at[4 * x + 2 * y + 1 - c])
        for i, out_ref in enumerate(out_refs):
            four = out_ref.at[pl.ds(0, 4)]
            _to_sibling(send_sems.at[i], recv_sems.at[i])(four, four).wait()

    return pl.pallas_call(
        body, name=name, out_shape=[_sds(a.shape, a.dtype) for a in lands], in_specs=[ANY] * (2 * n), out_specs=[ANY] * n,
        scratch_shapes=[pltpu.SemaphoreType.DMA((n,)), pltpu.SemaphoreType.DMA((n,))], input_output_aliases={i: i for i in range(n)},
        compiler_params=pltpu.CompilerParams(has_side_effects=True),
    )(*lands, *sibling_blks)


def rs_chips_start(hs, *, name):
    def starts(src_refs, land_refs, send_sem, recv_sem):
        x, y, c = _place()
        for r in (1, 2, 3):
            px, py = _flip(r, x, y)
            for src_ref, land_ref in zip(src_refs, land_refs):
                pltpu.make_async_remote_copy(src_ref=src_ref.at[2 * px + py], dst_ref=land_ref.at[r - 1], send_sem=send_sem,
                                             recv_sem=recv_sem, device_id=(px, py, c), device_id_type=MESH_ID).start()

    return _split_start(hs, [(3,) + h.shape[1:] for h in hs], starts, name=name)


def rs_chips_wait(handle, after, *, name):
    return _split_wait(handle, after, lambda src_ref, land_ref: src_ref.at[pl.ds(0, 3)], lambda src_ref, land_ref: land_ref, name=name)


def _row_block(kd, nd):
    return _pick(kd, (max(32, (1 << 19) // nd // 32 * 32),))


def add_kept_half(g, recv, core, *, name):
    nchip, nl, kd, nd = g.shape
    lh = nl // 2
    tk = _row_block(kd, nd)

    def body(c_ref, g_ref, r_ref, o_ref):
        del c_ref
        o_ref[...] = (g_ref[...].astype(F32) + r_ref[...].astype(F32)).astype(o_ref.dtype)

    blk = lambda f: pl.BlockSpec((None, None, tk, nd), f)
    return pl.pallas_call(
        body, name=name, out_shape=_sds((nchip, lh, kd, nd), BF16),
        grid_spec=pltpu.PrefetchScalarGridSpec(
            num_scalar_prefetch=1, grid=(nchip, lh, kd // tk),
            in_specs=[blk(lambda j, l, i, c_ref: (j, c_ref[0] * lh + l, i, 0)), blk(lambda j, l, i, c_ref: (j, l, i, 0))],
            out_specs=blk(lambda j, l, i, c_ref: (j, l, i, 0))),
        compiler_params=_cp(("parallel", "parallel", "parallel")),
    )(core, g, recv)


def add_chip_parts(h, parts, chip, core, grad, layer, *, name):
    _, _, kh, nd = h.shape
    tk = _row_block(kh, nd)
    nkb = kh // tk

    def body(chip_ref, core_ref, h_ref, p0, p1, p2, g_in, o_ref):
        del chip_ref, core_ref, g_in
        o_ref[...] = h_ref[...].astype(F32) + p0[...].astype(F32) + p1[...].astype(F32) + p2[...].astype(F32)

    blk = lambda f: pl.BlockSpec((None, None, tk, nd), f)
    part = lambda r: blk(functools.partial(lambda r_, i, chip_ref, core_ref: (r_, 0, i, 0), r))
    return pl.pallas_call(
        body, name=name, out_shape=_sds(grad.shape, grad.dtype),
        grid_spec=pltpu.PrefetchScalarGridSpec(
            num_scalar_prefetch=2, grid=(nkb,),
            in_specs=[blk(lambda i, chip_ref, core_ref: (chip_ref[0], 0, i, 0)), part(0), part(1), part(2), ANY],
            out_specs=pl.BlockSpec((None, tk, nd), lambda i, chip_ref, core_ref: (layer, core_ref[0] * nkb + i, 0))),
        input_output_aliases={6: 0},
        compiler_params=_cp(("parallel",)),
    )(chip, core, h, parts, parts, parts, grad)


def sibling_fill(grads, layer, *, name):
    n = len(grads)

    def body(*refs):
        g_refs, out_refs, (send_sems, recv_sems) = refs[:n], refs[n:2 * n], refs[2 * n:]
        c = lax.axis_index("c")
        copies = []
        for i, (g_ref, out_ref) in enumerate(zip(g_refs, out_refs)):
            kh = g_ref.shape[1] // 2
            copies.append((_to_sibling(send_sems.at[i], recv_sems.at[i]), g_ref.at[layer, pl.ds(c * kh, kh)],
                           out_ref.at[layer, pl.ds(c * kh, kh)]))
        for remote, src, dst in copies:
            _start_pieces(remote, src, dst)
        for remote, src, dst in copies:
            remote(src, dst).wait()

    return pl.pallas_call(
        body, name=name, out_shape=[_sds(g.shape, g.dtype) for g in grads], in_specs=[ANY] * n, out_specs=[ANY] * n,
        scratch_shapes=[pltpu.SemaphoreType.DMA((n,)), pltpu.SemaphoreType.DMA((n,))],
        input_output_aliases={i: i for i in range(n)},
        compiler_params=pltpu.CompilerParams(has_side_effects=True),
    )(*grads)


def sum_slots(a, out_dtype, *, name):
    n = a.shape[0]
    cols = a.shape[-1]
    a3 = a.reshape(n, -1, cols)
    rows = a3.shape[1]
    tr = _pick(rows, (max(32, (1 << 19) // cols // 32 * 32),))

    def body(*refs):
        acc = refs[0][...].astype(F32)
        for r in refs[1:n]:
            acc = acc + r[...].astype(F32)
        refs[n][...] = acc.astype(out_dtype)

    return pl.pallas_call(
        body, name=name, grid=(rows // tr,),
        in_specs=[pl.BlockSpec((None, tr, cols), functools.partial(lambda j, i: (j, i, 0), j)) for j in range(n)],
        out_specs=pl.BlockSpec((tr, cols), lambda i: (i, 0)),
        out_shape=_sds((rows, cols), out_dtype), compiler_params=_cp(("parallel",)),
    )(*([a3] * n)).reshape(a.shape[1:])


def unit_blocks(shards, ci):
    return [lax.dynamic_index_in_dim(w.reshape(2, w.shape[0] // 2, w.shape[1]), ci, axis=0, keepdims=False).astype(BF16)
            for w in shards]


def gather_finish(lands, sibling_blks, tag):
    full = ag_forward(lands, sibling_blks, name=f"{tag}_fwd")
    return [a.reshape(N_CHIP, 1, 2 * a.shape[1], a.shape[2]) for a in full]


def reduce_scatter_start(gs, tag):
    core = jnp.reshape(lax.axis_index("c"), (1,)).astype(jnp.int32)
    halves = [g.reshape(N_CHIP, 2, g.shape[1] // 2, g.shape[2]) for g in gs]
    recv = rs_sibling(halves, name=f"{tag}_rs1")
    chip_sums = [add_kept_half(h, r, core, name=f"{tag}_add1_{j}") for j, (h, r) in enumerate(zip(halves, recv))]
    return (rs_chips_start(chip_sums, name=f"{tag}_rs2_start"),)


def reduce_scatter_finish(pending, after, grads, layer, tag):
    _, handle = pending
    xi, yi, ci = _place()
    chip = jnp.reshape(2 * xi + yi, (1,)).astype(jnp.int32)
    core = jnp.reshape(ci, (1,)).astype(jnp.int32)
    chip_sums, parts = rs_chips_wait(handle, after, name=f"{tag}_rs2_wait")
    grads = [add_chip_parts(h, p, chip, core, g, layer, name=f"{tag}_add2_{j}")
             for j, (h, p, g) in enumerate(zip(chip_sums, parts, grads))]
    return sibling_fill(grads, layer, name=f"{tag}_rs3")


def mod_fwd(c16, w_mod, *, name):
    nl, d, ns = w_mod.shape
    tn = _pick(ns, (512,))

    def body(c_ref, w_ref, o_ref):
        o_ref[...] = jnp.dot(jax.nn.silu(c_ref[...]), w_ref[...], precision=HI, preferred_element_type=F32)

    return pl.pallas_call(
        body, name=name, grid=(nl, ns // tn),
        in_specs=[pl.BlockSpec((16, d), lambda l, j: (0, 0)), pl.BlockSpec((None, d, tn), lambda l, j: (l, 0, j))],
        out_specs=pl.BlockSpec((None, 16, tn), lambda l, j: (l, 0, j)),
        out_shape=_sds((nl, 16, ns), F32), compiler_params=_cp(("parallel", "parallel")),
    )(c16, w_mod)


def mod_bwd_w(c16, dm, *, name):
    nl, _, ns = dm.shape
    d = c16.shape[1]
    tn = _pick(ns, (512,))

    def body(c_ref, dm_ref, o_ref):
        o_ref[...] = lax.dot_general(jax.nn.silu(c_ref[...]), dm_ref[...], (((0,), (0,)), ((), ())), precision=HI,
                                     preferred_element_type=F32)

    return pl.pallas_call(
        body, name=name, grid=(nl, ns // tn),
        in_specs=[pl.BlockSpec((16, d), lambda l, j: (0, 0)), pl.BlockSpec((None, 16, tn), lambda l, j: (l, 0, j))],
        out_specs=pl.BlockSpec((None, d, tn), lambda l, j: (l, 0, j)),
        out_shape=_sds((nl, d, ns), F32), compiler_params=_cp(("parallel", "parallel")),
    )(c16, dm)


def mod_bwd_s(dm, w_mod, *, name):
    nl, d, ns = w_mod.shape
    td = _pick(d, (512,))

    def body(dm_ref, w_ref, o_ref):
        part = lax.dot_general(dm_ref[...], w_ref[...], (((1,), (1,)), ((), ())), precision=HI, preferred_element_type=F32)
        rowsum = jnp.sum(part[8:16], axis=0, keepdims=True)

        @pl.when(pl.program_id(1) == 0)
        def _():
            o_ref[...] = jnp.zeros_like(o_ref)

        o_ref[...] += jnp.broadcast_to(rowsum, o_ref.shape)

    return pl.pallas_call(
        body, name=name, grid=(d // td, nl),
        in_specs=[pl.BlockSpec((None, 16, ns), lambda i, l: (l, 0, 0)), pl.BlockSpec((None, td, ns), lambda i, l: (l, i, 0))],
        out_specs=pl.BlockSpec((8, td), lambda i, l: (0, i)),
        out_shape=_sds((8, d), F32), compiler_params=_cp(("parallel", "arbitrary")),
    )(dm, w_mod)


def colsum16(dm, *, name):
    nl, _, n = dm.shape
    tn = _pick(n, (2048,))

    def body(dm_ref, o_ref):
        o_ref[...] = jnp.broadcast_to(jnp.sum(dm_ref[...], axis=0, keepdims=True), o_ref.shape)

    return pl.pallas_call(
        body, name=name, grid=(nl, n // tn),
        in_specs=[pl.BlockSpec((None, 16, tn), lambda l, j: (l, 0, j))],
        out_specs=pl.BlockSpec((None, 8, tn), lambda l, j: (l, 0, j)),
        out_shape=_sds((nl, 8, n), F32), compiler_params=_cp(("parallel", "parallel")),
    )(dm)


def silu_grad_mul(g, c, *, name):
    def body(g_ref, c_ref, o_ref):
        _, vjp = jax.vjp(jax.nn.silu, c_ref[...])
        o_ref[...] = vjp(g_ref[...])[0]

    return pl.pallas_call(body, name=name, out_shape=_sds(g.shape, F32))(g, c)


def adamw(w, g, m, v, *, name, copy_grad=False):
    shape = w.shape
    cols = shape[-1] if len(shape) > 1 else LANE
    flat = [a.reshape(-1, cols) for a in (w, g, m, v)]
    rows = flat[0].shape[0]
    tr = _pick(rows, (max(8, (1 << 18) // cols // 8 * 8),)) if rows % 8 == 0 else rows
    c1 = 1.0 - ADAM_B1 ** ADAM_STEP
    c2 = 1.0 - ADAM_B2 ** ADAM_STEP

    n_out = 4 if copy_grad else 3

    def body(w_ref, g_ref, m_ref, v_ref, d_ref, nm_ref, nv_ref, *g_out):
        gv = g_ref[...]
        nm = ADAM_B1 * m_ref[...] + (1.0 - ADAM_B1) * gv
        nv = ADAM_B2 * v_ref[...] + (1.0 - ADAM_B2) * (gv * gv)
        d_ref[...] = -ADAM_LR * ((nm / c1) / (jnp.sqrt(nv / c2) + ADAM_EPS) + ADAM_WD * w_ref[...])
        nm_ref[...] = nm
        nv_ref[...] = nv
        if copy_grad:
            g_out[0][...] = gv

    blk = pl.BlockSpec((tr, cols), lambda i: (i, 0))
    outs = pl.pallas_call(
        body, name=name, grid=(rows // tr,), in_specs=[blk] * 4, out_specs=(blk,) * n_out,
        out_shape=(_sds((rows, cols), F32),) * n_out, compiler_params=_cp(("parallel",)),
    )(*flat)
    return tuple(o.reshape(shape) for o in outs)


PACK_ELEMS = LANE * LANE


def _pack(arrs):
    flat = jnp.concatenate([a.reshape(-1).astype(F32) for a in arrs])
    return jnp.pad(flat, (0, (-flat.shape[0]) % PACK_ELEMS)).reshape(-1, LANE)


def _unpack(packed, shapes):
    flat = packed.reshape(-1)
    out, pos = [], 0
    for s in shapes:
        n = math.prod(s)
        out.append(flat[pos:pos + n].reshape(s))
        pos += n
    return out


def _chip_cols(a, chip, width):
    return lax.dynamic_slice_in_dim(a, chip * width, width, axis=a.ndim - 1)


def kernel(x, c, ctx, c_ctx, w_mod, b_mod, norm_w, w_ffn_in, w_ffn_out, e_w_in, e_conv_w, e_conv_b, e_dt_bias, e_a_log, e_d_skip, e_ssd_norm_w, e_sgu_w, e_sgu_b, e_w_out, o_w_qkv, o_sink, o_w_out, loss_target, m_c_ctx, m_w_mod, m_b_mod, m_norm_w, m_w_ffn_in, m_w_ffn_out, m_e_w_in, m_e_conv_w, m_e_conv_b, m_e_dt_bias, m_e_a_log, m_e_d_skip, m_e_ssd_norm_w, m_e_sgu_w, m_e_sgu_b, m_e_w_out, m_o_w_qkv, m_o_sink, m_o_w_out, v_c_ctx, v_w_mod, v_b_mod, v_norm_w, v_w_ffn_in, v_w_ffn_out, v_e_w_in, v_e_conv_w, v_e_conv_b, v_e_dt_bias, v_e_a_log, v_e_d_skip, v_e_ssd_norm_w, v_e_sgu_w, v_e_sgu_b, v_e_w_out, v_o_w_qkv, v_o_sink, v_o_w_out):
    xi, yi, ci = _place()
    chip = 2 * xi + yi
    me = 2 * chip + ci
    s, d = x.shape[1:]
    tc = ctx.shape[1]
    depth = w_mod.shape[0]
    n_even = e_w_in.shape[0]
    dq = norm_w.shape[-1]
    cq = e_conv_w.shape[-1]
    ns = w_mod.shape[-1]

    gath = allgather8(_pack([c, norm_w, e_conv_w]), name="ag_small").reshape(N_DEV, -1)
    c_all = gath[:, :d]
    per_chip = [_unpack(gath[2 * k, d:], [norm_w.shape, e_conv_w.shape]) for k in range(N_CHIP)]
    nw_full = jnp.concatenate([pc[0] for pc in per_chip], axis=-1)
    convw_full = jnp.concatenate([pc[1] for pc in per_chip], axis=-1)
    c16 = jnp.concatenate([c_all, jnp.broadcast_to(c_ctx[None], (8, d))], axis=0)

    mod_g = allgather8(mod_fwd(c16, w_mod, name="mod_fwd"), name="ag_mod")
    mod_all = jnp.concatenate([mod_g[2 * k] for k in range(N_CHIP)], axis=-1) + b_mod[:, None, :]
    mod_rows = jnp.stack([mod_all[:, 8], lax.dynamic_index_in_dim(mod_all, me, axis=1, keepdims=False)], axis=1)
    modtab = jnp.pad(mod_rows.reshape(depth, 2, 6, d), ((0, 0), (0, 0), (0, 2), (0, 0)))

    eps_ = [even_params(convw_full[i], e_conv_b[i], e_dt_bias[i], e_a_log[i], e_d_skip[i], e_ssd_norm_w[i], e_sgu_w[i], e_sgu_b[i])
            for i in range(n_even)]
    sinks = [sink_rows(o_sink[i]) for i in range(o_sink.shape[0])]
    cos, sin = rope_tables(tc, s)
    units = [(kind, l) for l in range(depth) for kind in ("mix", "ffn")]

    def unit_shards(kind, l):
        if kind == "ffn":
            return [w_ffn_in[l], w_ffn_out[l]]
        return [e_w_in[l // 2], e_w_out[l // 2]] if l % 2 == 0 else [o_w_qkv[l // 2], o_w_out[l // 2]]

    def unit_weights(kind, l, gathered):
        w_a, w_b = gathered
        if kind == "mix" and l % 2 == 0:
            w_a = even_cols_permute(jnp.moveaxis(w_a[:, 0], 0, 1).reshape(1, d, -1))[None]
        return w_a, w_b

    def unit_fwd(kind, l, u_in, mt, wts):
        nw = nw_full[l]
        w_a, w_b = wts
        if kind == "mix":
            h1, h1t = norm_mod_fwd(u_in, nw[0], mt, tc, 0, name=f"L{l}_norm1")
            if l % 2 == 0:
                o, ms = even_mixer_fwd(h1, w_a, w_b, 0, eps_[l // 2], tc, f"L{l}_mix")
            else:
                o, ms = odd_mixer_fwd(h1, w_a, w_b, 0, sinks[l // 2], cos, sin, tc, f"L{l}_mix")
            return resid_fwd(u_in, o, nw[1], mt, tc, 0, name=f"L{l}_res1"), (u_in, h1t, ms, o)
        h2, h2t = norm_mod_fwd(u_in, nw[2], mt, tc, 1, name=f"L{l}_norm2")
        p = mm_nn(h2, w_a, 0, "n", name=f"L{l}_ffn_in", out_dtype=BF16)
        a, at = swiglu_fwd(p, name=f"L{l}_swiglu")
        f = mm_nn(a, w_b, 0, "k", name=f"L{l}_ffn_out")
        return resid_fwd(u_in, f, nw[3], mt, tc, 1, name=f"L{l}_res2"), (u_in, h2t, p, at, f)

    def unit_bwd(kind, l, du_out, mt, wts, sv):
        nw = nw_full[l]
        w_a, w_b = wts
        if kind == "ffn":
            u1, h2t, p, at, f = sv
            df, acc_r = resid_bwd(f, nw[3], mt, du_out, tc, 1, name=f"L{l}_res2_b")
            g_b = mm_tn(at, df, w_b, 0, "k", name=f"L{l}_ffn_out_dw", x_is_transposed=True)
            da = mm_nt(df, w_b, 0, "k", name=f"L{l}_ffn_out_dx", out_dtype=BF16)
            dp = swiglu_bwd(p, da, name=f"L{l}_swiglu_b")
            dh2 = mm_nt(dp, w_a, 0, "n", name=f"L{l}_ffn_in_dx")
            g_a = mm_tn(h2t, dp, w_a, 0, "n", name=f"L{l}_ffn_in_dw", x_is_transposed=True)
            du_in, acc_n = norm_mod_bwd(u1, nw[2], mt, dh2, du_out, tc, 1, name=f"L{l}_norm2_b")
            return du_in, [g_a[:, 0], g_b[:, 0]], (acc_n, acc_r), None
        u0, h1t, ms, o = sv
        do, acc_r = resid_bwd(o, nw[1], mt, du_out, tc, 0, name=f"L{l}_res1_b")
        if l % 2 == 0:
            dh1, g_a, g_b, small = even_mixer_bwd(ms, do, h1t, w_a, w_b, w_a, w_b, 0, eps_[l // 2], tc, f"L{l}_mix")
            g_a = jnp.moveaxis(even_cols_unpermute(g_a[0, 0]).reshape(d, N_CHIP, -1), 1, 0)
        else:
            dh1, g_a, g_b, small = odd_mixer_bwd(ms, do, h1t, w_a, w_b, w_a, w_b, 0, sinks[l // 2], cos, sin, tc,
                                                 f"L{l}_mix")
            g_a = g_a[:, 0]
        du_in, acc_n = norm_mod_bwd(u0, nw[0], mt, dh1, du_out, tc, 0, name=f"L{l}_norm1_b")
        return du_in, [g_a, g_b[:, 0]], (acc_n, acc_r), small

    u = jnp.concatenate([ctx[0], x[0]], axis=0)
    shards = unit_shards(*units[0])
    handle = ag_send_start(unit_blocks(shards, ci), name="ag0_start")
    lands = ag_send_wait(handle, handle[4], name="ag0_wait")
    wts = [None] * len(units)
    wts[0] = unit_weights(*units[0], gather_finish(lands, unit_blocks(shards, 1 - ci), "ag0"))
    saved = [None] * len(units)
    prev = u
    for i, (kind, l) in enumerate(units):
        tok = 0.0
        if i + 1 < len(units):
            shards = unit_shards(*units[i + 1])
            blks, _ = lax.optimization_barrier((unit_blocks(shards, ci), prev))
            handle = ag_send_start(blks, name=f"ag{i + 1}_start")
            tok = handle[4][0, 0]
        prev = u
        u, saved[i] = unit_fwd(kind, l, u, modtab[l] + tok, wts[i])
        if i + 1 < len(units):
            lands = ag_send_wait(handle, u, name=f"ag{i + 1}_wait")
            wts[i + 1] = unit_weights(*units[i + 1], gather_finish(lands, unit_blocks(shards, 1 - ci), f"ag{i + 1}"))
    loss_part, du = loss_fwd_bwd(u, loss_target[0], tc, name="loss")
    loss = lax.psum(loss_part[0, 0], ("x", "y", "c"))

    accs, smalls = [None] * len(units), [None] * len(units)
    gbuf = {n: lax.empty(w.shape, F32) for n, w in (("w_ffn_in", w_ffn_in), ("w_ffn_out", w_ffn_out), ("e_w_in", e_w_in),
                                                     ("e_w_out", e_w_out), ("o_w_qkv", o_w_qkv), ("o_w_out", o_w_out))}

    def finish(pending, after):
        kind, l = units[pending[0]]
        names, layer = (("w_ffn_in", "w_ffn_out"), l) if kind == "ffn" else (
            (("e_w_in", "e_w_out"), l // 2) if l % 2 == 0 else (("o_w_qkv", "o_w_out"), l // 2))
        done = reduce_scatter_finish(pending, after, [gbuf[n] for n in names], layer, f"rs{pending[0]}")
        gbuf.update(zip(names, done))

    pending = None
    for i in reversed(range(len(units))):
        kind, l = units[i]
        tok = pending[1][4][0, 0] if pending is not None else 0.0
        du, gs, accs[i], smalls[i] = unit_bwd(kind, l, du, modtab[l] + tok, wts[i], saved[i])
        if pending is not None:
            finish(pending, du)
        pending = (i,) + reduce_scatter_start(gs, f"rs{i}")
    grad_x = du[tc:][None]
    d_nw, d_mt = [None] * depth, [None] * depth
    for l in range(depth):
        (acc0, acc1), (acc2, acc3) = accs[2 * l], accs[2 * l + 1]
        d_nw[l] = jnp.stack([acc[0, 0] + acc[1, 0] for acc in (acc0, acc1, acc2, acc3)])
        d_mt[l] = jnp.stack([acc0[:, 1], acc0[:, 2], acc1[:, 1], acc2[:, 1], acc2[:, 2], acc3[:, 1]], axis=1)
    small_e = [smalls[2 * l] for l in range(0, depth, 2)]
    small_o = [smalls[2 * l] for l in range(1, depth, 2)]

    d_mt_all = jnp.stack(d_mt) + pending[1][4][0, 0]
    dmt_g = allgather8(jnp.pad(d_mt_all, ((0, 0), (0, 0), (0, 2), (0, 0))), name="ag_dmod")[:, :, :, :6]
    dm16 = jnp.concatenate([dmt_g[:, :, 1].transpose(1, 0, 2, 3).reshape(depth, N_DEV, 6 * d),
                            dmt_g[:, :, 0].transpose(1, 0, 2, 3).reshape(depth, N_DEV, 6 * d)], axis=1)
    dm_sh = _chip_cols(dm16, chip, ns)
    grad_w_mod = mod_bwd_w(c16, dm_sh, name="mod_bwd_w")
    grad_b_mod = colsum16(dm16, name="mod_bwd_b")[:, 0]
    ds_cc = mod_bwd_s(dm_sh, w_mod, name="mod_bwd_s")[0]

    stack_e = lambda key: jnp.stack([se[key] for se in small_e])
    small_names = ["e_conv_b", "e_dt_bias", "e_a_log", "e_d_skip", "e_ssd_norm_w", "e_sgu_w", "e_sgu_b"]
    small_parts = [jnp.stack(d_nw), stack_e("e_conv_w")] + [stack_e(k) for k in small_names]
    small_parts += [jnp.stack([so["o_sink"] for so in small_o]), 0.5 * ds_cc]
    small_shapes = [a.shape for a in small_parts]
    small_sum = sum_slots(allgather8(_pack(small_parts), name="ag_small_grads"), F32, name="small_grads_sum")
    (g_nw, g_convw, g_convb, g_dtb, g_alog, g_dskip, g_ssdnw, g_sguw, g_sgub, g_sink, g_scc) = _unpack(small_sum, small_shapes)
    grad_c_ctx = silu_grad_mul(jnp.broadcast_to(g_scc[None], (8, d)), jnp.broadcast_to(c_ctx[None], (8, d)), name="c_ctx_grad")[0]
    grads = dict(
        c_ctx=grad_c_ctx, w_mod=grad_w_mod, b_mod=grad_b_mod, norm_w=_chip_cols(g_nw, chip, dq),
        e_conv_w=_chip_cols(g_convw, chip, cq), e_conv_b=g_convb, e_dt_bias=g_dtb.reshape(e_dt_bias.shape),
        e_a_log=g_alog.reshape(e_a_log.shape), e_d_skip=g_dskip, e_ssd_norm_w=g_ssdnw, e_sgu_w=g_sguw, e_sgu_b=g_sgub,
        o_sink=g_sink)

    finish(pending, grad_w_mod)
    grads.update(gbuf)

    weights = dict(c_ctx=c_ctx, w_mod=w_mod, b_mod=b_mod, norm_w=norm_w, w_ffn_in=w_ffn_in, w_ffn_out=w_ffn_out, e_w_in=e_w_in,
                   e_conv_w=e_conv_w, e_conv_b=e_conv_b, e_dt_bias=e_dt_bias, e_a_log=e_a_log, e_d_skip=e_d_skip,
                   e_ssd_norm_w=e_ssd_norm_w, e_sgu_w=e_sgu_w, e_sgu_b=e_sgu_b, e_w_out=e_w_out, o_w_qkv=o_w_qkv, o_sink=o_sink,
                   o_w_out=o_w_out)
    ms_ = dict(c_ctx=m_c_ctx, w_mod=m_w_mod, b_mod=m_b_mod, norm_w=m_norm_w, w_ffn_in=m_w_ffn_in, w_ffn_out=m_w_ffn_out,
               e_w_in=m_e_w_in, e_conv_w=m_e_conv_w, e_conv_b=m_e_conv_b, e_dt_bias=m_e_dt_bias, e_a_log=m_e_a_log,
               e_d_skip=m_e_d_skip, e_ssd_norm_w=m_e_ssd_norm_w, e_sgu_w=m_e_sgu_w, e_sgu_b=m_e_sgu_b, e_w_out=m_e_w_out,
               o_w_qkv=m_o_w_qkv, o_sink=m_o_sink, o_w_out=m_o_w_out)
    vs_ = dict(c_ctx=v_c_ctx, w_mod=v_w_mod, b_mod=v_b_mod, norm_w=v_norm_w, w_ffn_in=v_w_ffn_in, w_ffn_out=v_w_ffn_out,
               e_w_in=v_e_w_in, e_conv_w=v_e_conv_w, e_conv_b=v_e_conv_b, e_dt_bias=v_e_dt_bias, e_a_log=v_e_a_log,
               e_d_skip=v_e_d_skip, e_ssd_norm_w=v_e_ssd_norm_w, e_sgu_w=v_e_sgu_w, e_sgu_b=v_e_sgu_b, e_w_out=v_e_w_out,
               o_w_qkv=v_o_w_qkv, o_sink=v_o_sink, o_w_out=v_o_w_out)
    names = list(weights)
    big = ("w_mod", "w_ffn_in", "w_ffn_out", "e_w_in", "e_w_out", "o_w_qkv", "o_w_out")
    small = [n for n in names if n not in big]
    delta, new_m, new_v = {}, {}, {}
    for n in big:
        if n == "w_mod":
            delta[n], new_m[n], new_v[n] = adamw(weights[n], grads[n], ms_[n], vs_[n], name=f"adamw_{n}")
        else:
            delta[n], new_m[n], new_v[n], grads[n] = adamw(weights[n], grads[n], ms_[n], vs_[n], name=f"adamw_{n}", copy_grad=True)
    packed = adamw(*[_pack([tab[n] for n in small]) for tab in (weights, grads, ms_, vs_)], name="adamw_small")
    shapes = [weights[n].shape for n in small]
    for tab, pk in zip((delta, new_m, new_v), packed):
        for n, val in zip(small, _unpack(pk, shapes)):
            tab[n] = val
    return (loss, grad_x, *[grads[n] for n in names], *[delta[n] for n in names], *[new_m[n] for n in names],
            *[new_v[n] for n in names])
```
